```python
import jax, jax.numpy as jnp
from jax import lax
import numpy as np

D_MODEL = 1024
BATCH = 8
SEQ = 8192
DEPTH = 4

N_MIXERS = 2
N_LAYERS_A = (DEPTH + N_MIXERS - 1) // N_MIXERS
N_LAYERS_B = DEPTH // N_MIXERS
LRU_WIDTH = D_MODEL
LRU_HEADS = 4
LRU_HEAD_DIM = LRU_WIDTH // LRU_HEADS
CONV_WIDTH = 4
LRU_C = 8.0
MIN_RAD = 0.9
MAX_RAD = 0.999
POOL_WINDOWS = (2, 4, 8, 16)
POOL_GROUPS = len(POOL_WINDOWS)
POOL_GROUP_DIM = D_MODEL // POOL_GROUPS
D_FF = 4 * D_MODEL
N_MOD = 6
EPS = 1e-6

kernel_name = "hybrid_rglru_multiscale_pool_adaln"


def rms_norm(x, g):
    xf = x.astype(jnp.float32)
    y = xf * lax.rsqrt(jnp.mean(xf * xf, axis=-1, keepdims=True) + EPS)
    return (y * g.astype(jnp.float32)).astype(x.dtype)


def modulate(h, shift, scale):
    return h * (1.0 + scale[:, None, :]) + shift[:, None, :]


def causal_depthwise_conv(x, w, b):
    s = x.shape[1]
    xp = jnp.pad(x, ((0, 0), (CONV_WIDTH - 1, 0), (0, 0)))
    y = xp[:, 0:s] * w[0]
    for k in range(1, CONV_WIDTH):
        y = y + xp[:, k:k + s] * w[k]
    return y + b


def _lru_combine(left, right):
    a1, b1 = left
    a2, b2 = right
    return a1 * a2, a2 * b1 + b2


def block_diag_linear(x, w, b):
    bsz, s, _ = x.shape
    xh = x.reshape(bsz, s, LRU_HEADS, LRU_HEAD_DIM)
    y = jnp.einsum("bshi,hij->bshj", xh, w) + b
    return y.reshape(bsz, s, LRU_WIDTH)


def rg_lru(x, w_a, b_a, w_x, b_x, lam):
    gate_r = jax.nn.sigmoid(block_diag_linear(x, w_a, b_a)).astype(jnp.float32)
    gate_i = jax.nn.sigmoid(block_diag_linear(x, w_x, b_x)).astype(jnp.float32)
    log_a = LRU_C * gate_r * jax.nn.log_sigmoid(lam.astype(jnp.float32))
    a = jnp.exp(log_a)
    mult = jnp.sqrt(-jnp.expm1(2.0 * log_a))
    u = mult * (gate_i * x.astype(jnp.float32))
    _, h = lax.associative_scan(_lru_combine, (a, u), axis=1)
    return h.astype(x.dtype)


def recurrent_mixer(h, w_y, b_y, w_in, b_in, conv_w, conv_b, w_a, b_a, w_x, b_x, lam, w_out, b_out):
    gate_branch = jax.nn.gelu(jnp.einsum("bsd,dw->bsw", h, w_y) + b_y)
    xr = jnp.einsum("bsd,dw->bsw", h, w_in) + b_in
    xr = causal_depthwise_conv(xr, conv_w, conv_b)
    xr = rg_lru(xr, w_a, b_a, w_x, b_x, lam)
    return jnp.einsum("bsw,wd->bsd", xr * gate_branch, w_out) + b_out


def pool_mixer(h, w_pool, pool_scale):
    bsz, s, _ = h.shape
    hf = h.astype(jnp.float32)
    counts = jnp.arange(1, s + 1, dtype=jnp.float32)
    outs = []
    for g, win in enumerate(POOL_WINDOWS):
        xg = hf[..., g * POOL_GROUP_DIM:(g + 1) * POOL_GROUP_DIM]
        cs = jnp.cumsum(xg, axis=1)
        cs_lag = jnp.pad(cs, ((0, 0), (win, 0), (0, 0)))[:, :s]
        mean = (cs - cs_lag) / jnp.minimum(counts, float(win))[None, :, None]
        outs.append(mean - xg)
    pooled = jnp.stack(outs, axis=2).astype(h.dtype)
    mixed = jnp.einsum("bsgi,gij->bsgj", pooled, w_pool).reshape(bsz, s, D_MODEL)
    return mixed * pool_scale


def sq_relu_mlp(h, w1, w2):
    u = jax.nn.relu(jnp.einsum("bsd,df->bsf", h, w1))
    return jnp.einsum("bsf,fd->bsd", u * u, w2)


def _fwd_setup_inputs(seed: int = 0) -> dict:
    key = jax.random.key(seed)
    ks = jax.random.split(key, 26)
    f32 = jnp.float32
    nrm = lambda k, shape, s: (jax.random.normal(k, shape, f32) * s)
    d, w, hd, na, nb = D_MODEL, LRU_WIDTH, LRU_HEAD_DIM, N_LAYERS_A, N_LAYERS_B
    rad = jnp.sqrt(jax.random.uniform(ks[15], (na, w), f32, MIN_RAD ** 2, MAX_RAD ** 2))
    return {
        "x": nrm(ks[0], (BATCH, SEQ, d), 1.0),
        "c": nrm(ks[1], (BATCH, d), 1.0),
        "w_mod": nrm(ks[2], (DEPTH, d, N_MOD * d), 0.5 * d ** -0.5),
        "b_mod": nrm(ks[3], (DEPTH, N_MOD * d), 0.02),
        "norm_mix_g": 1.0 + nrm(ks[4], (DEPTH, d), 0.05),
        "norm_ffn_g": 1.0 + nrm(ks[5], (DEPTH, d), 0.05),
        "lru_w_y": nrm(ks[6], (na, d, w), d ** -0.5),
        "lru_b_y": nrm(ks[7], (na, w), 0.02),
        "lru_w_in": nrm(ks[8], (na, d, w), d ** -0.5),
        "lru_b_in": nrm(ks[9], (na, w), 0.02),
        "lru_conv_w": nrm(ks[10], (na, CONV_WIDTH, w), CONV_WIDTH ** -0.5),
        "lru_conv_b": nrm(ks[11], (na, w), 0.02),
        "lru_w_a": nrm(ks[12], (na, LRU_HEADS, hd, hd), hd ** -0.5),
        "lru_b_a": nrm(ks[13], (na, LRU_HEADS, hd), 0.02),
        "lru_w_x": nrm(ks[14], (na, LRU_HEADS, hd, hd), hd ** -0.5),
        "lru_b_x": nrm(ks[16], (na, LRU_HEADS, hd), 0.02),
        "lru_lambda": jnp.log(rad) - jnp.log1p(-rad),
        "lru_w_out": nrm(ks[17], (na, w, d), w ** -0.5),
        "lru_b_out": nrm(ks[18], (na, d), 0.02),
        "pool_w": nrm(ks[19], (nb, POOL_GROUPS, POOL_GROUP_DIM, POOL_GROUP_DIM), POOL_GROUP_DIM ** -0.5),
        "pool_scale": 1.0 + nrm(ks[20], (nb, d), 0.1),
        "ffn_w1": nrm(ks[21], (DEPTH, d, D_FF), d ** -0.5),
        "ffn_w2": nrm(ks[22], (DEPTH, D_FF, d), D_FF ** -0.5),
        "final_norm_g": 1.0 + nrm(ks[23], (d,), 0.05),
    }


def _fwd_reference(x, c, w_mod, b_mod, norm_mix_g, norm_ffn_g, lru_w_y, lru_b_y, lru_w_in, lru_b_in,
              lru_conv_w, lru_conv_b, lru_w_a, lru_b_a, lru_w_x, lru_b_x, lru_lambda, lru_w_out,
              lru_b_out, pool_w, pool_scale, ffn_w1, ffn_w2, final_norm_g):
    cond = jax.nn.silu(c)
    for i in range(DEPTH):
        mod = jnp.einsum("bd,de->be", cond, w_mod[i]) + b_mod[i]
        sh_m, sc_m, gt_m, sh_f, sc_f, gt_f = jnp.split(mod, N_MOD, axis=-1)
        h = modulate(rms_norm(x, norm_mix_g[i]), sh_m, sc_m)
        j = i // N_MIXERS
        if i % N_MIXERS == 0:
            y = recurrent_mixer(h, lru_w_y[j], lru_b_y[j], lru_w_in[j], lru_b_in[j],
                                lru_conv_w[j], lru_conv_b[j], lru_w_a[j], lru_b_a[j],
                                lru_w_x[j], lru_b_x[j], lru_lambda[j], lru_w_out[j], lru_b_out[j])
        else:
            y = pool_mixer(h, pool_w[j], pool_scale[j])
        x = x + gt_m[:, None, :] * y
        h = modulate(rms_norm(x, norm_ffn_g[i]), sh_f, sc_f)
        x = x + gt_f[:, None, :] * sq_relu_mlp(h, ffn_w1[i], ffn_w2[i])
    return rms_norm(x, final_norm_g)


import jax as _jax
import jax.numpy as _jnp

TWIN_FORMAT = 'train_step'
FWD_PARAMS = ['x', 'c', 'w_mod', 'b_mod', 'norm_mix_g', 'norm_ffn_g', 'lru_w_y', 'lru_b_y', 'lru_w_in', 'lru_b_in', 'lru_conv_w', 'lru_conv_b', 'lru_w_a', 'lru_b_a', 'lru_w_x', 'lru_b_x', 'lru_lambda', 'lru_w_out', 'lru_b_out', 'pool_w', 'pool_scale', 'ffn_w1', 'ffn_w2', 'final_norm_g']
TWIN_WEIGHTS = ['w_mod', 'b_mod', 'norm_mix_g', 'norm_ffn_g', 'lru_w_y', 'lru_b_y', 'lru_w_in', 'lru_b_in', 'lru_conv_w', 'lru_conv_b', 'lru_w_a', 'lru_b_a', 'lru_w_x', 'lru_b_x', 'lru_lambda', 'lru_w_out', 'lru_b_out', 'pool_w', 'pool_scale', 'ffn_w1', 'ffn_w2', 'final_norm_g']
TWIN_DIFF_INPUT = 'x'
TWIN_INPUTS = ['x', 'c', 'w_mod', 'b_mod', 'norm_mix_g', 'norm_ffn_g', 'lru_w_y', 'lru_b_y', 'lru_w_in', 'lru_b_in', 'lru_conv_w', 'lru_conv_b', 'lru_w_a', 'lru_b_a', 'lru_w_x', 'lru_b_x', 'lru_lambda', 'lru_w_out', 'lru_b_out', 'pool_w', 'pool_scale', 'ffn_w1', 'ffn_w2', 'final_norm_g', 'loss_target', 'm_w_mod', 'm_b_mod', 'm_norm_mix_g', 'm_norm_ffn_g', 'm_lru_w_y', 'm_lru_b_y', 'm_lru_w_in', 'm_lru_b_in', 'm_lru_conv_w', 'm_lru_conv_b', 'm_lru_w_a', 'm_lru_b_a', 'm_lru_w_x', 'm_lru_b_x', 'm_lru_lambda', 'm_lru_w_out', 'm_lru_b_out', 'm_pool_w', 'm_pool_scale', 'm_ffn_w1', 'm_ffn_w2', 'm_final_norm_g', 'v_w_mod', 'v_b_mod', 'v_norm_mix_g', 'v_norm_ffn_g', 'v_lru_w_y', 'v_lru_b_y', 'v_lru_w_in', 'v_lru_b_in', 'v_lru_conv_w', 'v_lru_conv_b', 'v_lru_w_a', 'v_lru_b_a', 'v_lru_w_x', 'v_lru_b_x', 'v_lru_lambda', 'v_lru_w_out', 'v_lru_b_out', 'v_pool_w', 'v_pool_scale', 'v_ffn_w1', 'v_ffn_w2', 'v_final_norm_g']
TWIN_OUTPUTS = ['loss', 'grad_x', 'grad_w_mod', 'grad_b_mod', 'grad_norm_mix_g', 'grad_norm_ffn_g', 'grad_lru_w_y', 'grad_lru_b_y', 'grad_lru_w_in', 'grad_lru_b_in', 'grad_lru_conv_w', 'grad_lru_conv_b', 'grad_lru_w_a', 'grad_lru_b_a', 'grad_lru_w_x', 'grad_lru_b_x', 'grad_lru_lambda', 'grad_lru_w_out', 'grad_lru_b_out', 'grad_pool_w', 'grad_pool_scale', 'grad_ffn_w1', 'grad_ffn_w2', 'grad_final_norm_g', 'delta_w_mod', 'delta_b_mod', 'delta_norm_mix_g', 'delta_norm_ffn_g', 'delta_lru_w_y', 'delta_lru_b_y', 'delta_lru_w_in', 'delta_lru_b_in', 'delta_lru_conv_w', 'delta_lru_conv_b', 'delta_lru_w_a', 'delta_lru_b_a', 'delta_lru_w_x', 'delta_lru_b_x', 'delta_lru_lambda', 'delta_lru_w_out', 'delta_lru_b_out', 'delta_pool_w', 'delta_pool_scale', 'delta_ffn_w1', 'delta_ffn_w2', 'delta_final_norm_g', 'new_m_w_mod', 'new_m_b_mod', 'new_m_norm_mix_g', 'new_m_norm_ffn_g', 'new_m_lru_w_y', 'new_m_lru_b_y', 'new_m_lru_w_in', 'new_m_lru_b_in', 'new_m_lru_conv_w', 'new_m_lru_conv_b', 'new_m_lru_w_a', 'new_m_lru_b_a', 'new_m_lru_w_x', 'new_m_lru_b_x', 'new_m_lru_lambda', 'new_m_lru_w_out', 'new_m_lru_b_out', 'new_m_pool_w', 'new_m_pool_scale', 'new_m_ffn_w1', 'new_m_ffn_w2', 'new_m_final_norm_g', 'new_v_w_mod', 'new_v_b_mod', 'new_v_norm_mix_g', 'new_v_norm_ffn_g', 'new_v_lru_w_y', 'new_v_lru_b_y', 'new_v_lru_w_in', 'new_v_lru_b_in', 'new_v_lru_conv_w', 'new_v_lru_conv_b', 'new_v_lru_w_a', 'new_v_lru_b_a', 'new_v_lru_w_x', 'new_v_lru_b_x', 'new_v_lru_lambda', 'new_v_lru_w_out', 'new_v_lru_b_out', 'new_v_pool_w', 'new_v_pool_scale', 'new_v_ffn_w1', 'new_v_ffn_w2', 'new_v_final_norm_g']
TWIN_LEAF_KINDS = {'loss': 'loss', 'grad_x': 'grad_x', 'grad_w_mod': 'grad_w', 'grad_b_mod': 'grad_w', 'grad_norm_mix_g': 'grad_w', 'grad_norm_ffn_g': 'grad_w', 'grad_lru_w_y': 'grad_w', 'grad_lru_b_y': 'grad_w', 'grad_lru_w_in': 'grad_w', 'grad_lru_b_in': 'grad_w', 'grad_lru_conv_w': 'grad_w', 'grad_lru_conv_b': 'grad_w', 'grad_lru_w_a': 'grad_w', 'grad_lru_b_a': 'grad_w', 'grad_lru_w_x': 'grad_w', 'grad_lru_b_x': 'grad_w', 'grad_lru_lambda': 'grad_w', 'grad_lru_w_out': 'grad_w', 'grad_lru_b_out': 'grad_w', 'grad_pool_w': 'grad_w', 'grad_pool_scale': 'grad_w', 'grad_ffn_w1': 'grad_w', 'grad_ffn_w2': 'grad_w', 'grad_final_norm_g': 'grad_w', 'delta_w_mod': 'delta_w', 'delta_b_mod': 'delta_w', 'delta_norm_mix_g': 'delta_w', 'delta_norm_ffn_g': 'delta_w', 'delta_lru_w_y': 'delta_w', 'delta_lru_b_y': 'delta_w', 'delta_lru_w_in': 'delta_w', 'delta_lru_b_in': 'delta_w', 'delta_lru_conv_w': 'delta_w', 'delta_lru_conv_b': 'delta_w', 'delta_lru_w_a': 'delta_w', 'delta_lru_b_a': 'delta_w', 'delta_lru_w_x': 'delta_w', 'delta_lru_b_x': 'delta_w', 'delta_lru_lambda': 'delta_w', 'delta_lru_w_out': 'delta_w', 'delta_lru_b_out': 'delta_w', 'delta_pool_w': 'delta_w', 'delta_pool_scale': 'delta_w', 'delta_ffn_w1': 'delta_w', 'delta_ffn_w2': 'delta_w', 'delta_final_norm_g': 'delta_w', 'new_m_w_mod': 'new_m', 'new_m_b_mod': 'new_m', 'new_m_norm_mix_g': 'new_m', 'new_m_norm_ffn_g': 'new_m', 'new_m_lru_w_y': 'new_m', 'new_m_lru_b_y': 'new_m', 'new_m_lru_w_in': 'new_m', 'new_m_lru_b_in': 'new_m', 'new_m_lru_conv_w': 'new_m', 'new_m_lru_conv_b': 'new_m', 'new_m_lru_w_a': 'new_m', 'new_m_lru_b_a': 'new_m', 'new_m_lru_w_x': 'new_m', 'new_m_lru_b_x': 'new_m', 'new_m_lru_lambda': 'new_m', 'new_m_lru_w_out': 'new_m', 'new_m_lru_b_out': 'new_m', 'new_m_pool_w': 'new_m', 'new_m_pool_scale': 'new_m', 'new_m_ffn_w1': 'new_m', 'new_m_ffn_w2': 'new_m', 'new_m_final_norm_g': 'new_m', 'new_v_w_mod': 'new_v', 'new_v_b_mod': 'new_v', 'new_v_norm_mix_g': 'new_v', 'new_v_norm_ffn_g': 'new_v', 'new_v_lru_w_y': 'new_v', 'new_v_lru_b_y': 'new_v', 'new_v_lru_w_in': 'new_v', 'new_v_lru_b_in': 'new_v', 'new_v_lru_conv_w': 'new_v', 'new_v_lru_conv_b': 'new_v', 'new_v_lru_w_a': 'new_v', 'new_v_lru_b_a': 'new_v', 'new_v_lru_w_x': 'new_v', 'new_v_lru_b_x': 'new_v', 'new_v_lru_lambda': 'new_v', 'new_v_lru_w_out': 'new_v', 'new_v_lru_b_out': 'new_v', 'new_v_pool_w': 'new_v', 'new_v_pool_scale': 'new_v', 'new_v_ffn_w1': 'new_v', 'new_v_ffn_w2': 'new_v', 'new_v_final_norm_g': 'new_v'}


def _forward(args):
    return _fwd_reference(*[args[k] for k in FWD_PARAMS])


def _output_shape():
    def fwd():
        inp = _fwd_setup_inputs(0)
        return _fwd_reference(*[inp[k] for k in FWD_PARAMS])
    out = _jax.eval_shape(fwd)
    return out.shape, out.dtype

N_MICROBATCH = 1
ADAM_LR = 0.001
ADAM_B1 = 0.9
ADAM_B2 = 0.999
ADAM_EPS = 1e-08
ADAM_WD = 0.01
ADAM_STEP = 10
PER_EXAMPLE_BATCH_AXIS = {'x': 0, 'c': 0, 'loss_target': 0}
SHARED_INPUTS = []
_WEIGHT_DTYPES = {'w_mod': _jnp.float32, 'b_mod': _jnp.float32, 'norm_mix_g': _jnp.float32, 'norm_ffn_g': _jnp.float32, 'lru_w_y': _jnp.float32, 'lru_b_y': _jnp.float32, 'lru_w_in': _jnp.float32, 'lru_b_in': _jnp.float32, 'lru_conv_w': _jnp.float32, 'lru_conv_b': _jnp.float32, 'lru_w_a': _jnp.float32, 'lru_b_a': _jnp.float32, 'lru_w_x': _jnp.float32, 'lru_b_x': _jnp.float32, 'lru_lambda': _jnp.float32, 'lru_w_out': _jnp.float32, 'lru_b_out': _jnp.float32, 'pool_w': _jnp.float32, 'pool_scale': _jnp.float32, 'ffn_w1': _jnp.float32, 'ffn_w2': _jnp.float32, 'final_norm_g': _jnp.float32}
MOMENT_SCALE = {'w_mod': 1.718915e-01, 'b_mod': 3.721343e-01, 'norm_mix_g': 6.089527e-02, 'norm_ffn_g': 1.081071e-01, 'lru_w_y': 6.018975e-02, 'lru_b_y': 7.282632e-02, 'lru_w_in': 6.688562e-02, 'lru_b_in': 1.769472e-01, 'lru_conv_w': 6.795024e-02, 'lru_conv_b': 1.694678e-01, 'lru_w_a': 9.076946e-03, 'lru_b_a': 1.287232e-02, 'lru_w_x': 1.586878e-02, 'lru_b_x': 2.439059e-02, 'lru_lambda': 3.145243e-02, 'lru_w_out': 6.537107e-02, 'lru_b_out': 1.264731e-01, 'pool_w': 5.948372e-02, 'pool_scale': 1.919939e-01, 'ffn_w1': 5.575227e-02, 'ffn_w2': 1.311146e-01, 'final_norm_g': 6.457466e+01}


def _to_microbatches(a, axis):
    t = _jnp.moveaxis(a, axis, 0)
    t = t.reshape((N_MICROBATCH, t.shape[0] // N_MICROBATCH) + t.shape[1:])
    return _jnp.moveaxis(t, 1, axis + 1)


def setup_inputs(seed: int = 0) -> dict:
    inp = _fwd_setup_inputs(seed)
    key = _jax.random.fold_in(_jax.random.key(seed), 7919)
    shape, _ = _output_shape()
    out = dict(inp)
    out["loss_target"] = _jax.random.normal(_jax.random.fold_in(key, 0), shape, _jnp.float32)
    for i, name in enumerate(TWIN_WEIGHTS):
        w = inp[name].astype(_jnp.float32)
        if MOMENT_SCALE is None:
            s = _jnp.sqrt(_jnp.mean(_jnp.square(w)) + 1e-30)
        else:
            s = MOMENT_SCALE[name]
        km, kv = _jax.random.split(_jax.random.fold_in(key, i + 1))
        out[name] = w
        out["m_" + name] = s * _jax.random.normal(km, w.shape, _jnp.float32)
        out["v_" + name] = (s * s) * _jax.random.uniform(kv, w.shape, _jnp.float32, 0.5, 1.5)
    if N_MICROBATCH > 1:
        for name, axis in PER_EXAMPLE_BATCH_AXIS.items():
            out[name] = _to_microbatches(out[name], axis)
    return {'x': out['x'], 'c': out['c'], 'w_mod': out['w_mod'], 'b_mod': out['b_mod'], 'norm_mix_g': out['norm_mix_g'], 'norm_ffn_g': out['norm_ffn_g'], 'lru_w_y': out['lru_w_y'], 'lru_b_y': out['lru_b_y'], 'lru_w_in': out['lru_w_in'], 'lru_b_in': out['lru_b_in'], 'lru_conv_w': out['lru_conv_w'], 'lru_conv_b': out['lru_conv_b'], 'lru_w_a': out['lru_w_a'], 'lru_b_a': out['lru_b_a'], 'lru_w_x': out['lru_w_x'], 'lru_b_x': out['lru_b_x'], 'lru_lambda': out['lru_lambda'], 'lru_w_out': out['lru_w_out'], 'lru_b_out': out['lru_b_out'], 'pool_w': out['pool_w'], 'pool_scale': out['pool_scale'], 'ffn_w1': out['ffn_w1'], 'ffn_w2': out['ffn_w2'], 'final_norm_g': out['final_norm_g'], 'loss_target': out['loss_target'], 'm_w_mod': out['m_w_mod'], 'm_b_mod': out['m_b_mod'], 'm_norm_mix_g': out['m_norm_mix_g'], 'm_norm_ffn_g': out['m_norm_ffn_g'], 'm_lru_w_y': out['m_lru_w_y'], 'm_lru_b_y': out['m_lru_b_y'], 'm_lru_w_in': out['m_lru_w_in'], 'm_lru_b_in': out['m_lru_b_in'], 'm_lru_conv_w': out['m_lru_conv_w'], 'm_lru_conv_b': out['m_lru_conv_b'], 'm_lru_w_a': out['m_lru_w_a'], 'm_lru_b_a': out['m_lru_b_a'], 'm_lru_w_x': out['m_lru_w_x'], 'm_lru_b_x': out['m_lru_b_x'], 'm_lru_lambda': out['m_lru_lambda'], 'm_lru_w_out': out['m_lru_w_out'], 'm_lru_b_out': out['m_lru_b_out'], 'm_pool_w': out['m_pool_w'], 'm_pool_scale': out['m_pool_scale'], 'm_ffn_w1': out['m_ffn_w1'], 'm_ffn_w2': out['m_ffn_w2'], 'm_final_norm_g': out['m_final_norm_g'], 'v_w_mod': out['v_w_mod'], 'v_b_mod': out['v_b_mod'], 'v_norm_mix_g': out['v_norm_mix_g'], 'v_norm_ffn_g': out['v_norm_ffn_g'], 'v_lru_w_y': out['v_lru_w_y'], 'v_lru_b_y': out['v_lru_b_y'], 'v_lru_w_in': out['v_lru_w_in'], 'v_lru_b_in': out['v_lru_b_in'], 'v_lru_conv_w': out['v_lru_conv_w'], 'v_lru_conv_b': out['v_lru_conv_b'], 'v_lru_w_a': out['v_lru_w_a'], 'v_lru_b_a': out['v_lru_b_a'], 'v_lru_w_x': out['v_lru_w_x'], 'v_lru_b_x': out['v_lru_b_x'], 'v_lru_lambda': out['v_lru_lambda'], 'v_lru_w_out': out['v_lru_w_out'], 'v_lru_b_out': out['v_lru_b_out'], 'v_pool_w': out['v_pool_w'], 'v_pool_scale': out['v_pool_scale'], 'v_ffn_w1': out['v_ffn_w1'], 'v_ffn_w2': out['v_ffn_w2'], 'v_final_norm_g': out['v_final_norm_g']}


def _loss(weights, diff, rest, loss_target):
    with _jax.named_scope("forward"):
        args = {**rest, TWIN_DIFF_INPUT: diff, **{k: w.astype(_WEIGHT_DTYPES[k]) for k, w in weights.items()}}
        y = _forward(args)
    with _jax.named_scope("loss_head"):
        err = _jnp.square(y.astype(_jnp.float32) - loss_target)
        return 0.5 * _jnp.sum(_jnp.mean(err, axis=-1)) if err.ndim else 0.5 * err


def _adamw(w, g, m, v):
    m = ADAM_B1 * m + (1.0 - ADAM_B1) * g
    v = ADAM_B2 * v + (1.0 - ADAM_B2) * _jnp.square(g)
    m_hat = m / (1.0 - ADAM_B1 ** ADAM_STEP)
    v_hat = v / (1.0 - ADAM_B2 ** ADAM_STEP)
    delta = -ADAM_LR * (m_hat / (_jnp.sqrt(v_hat) + ADAM_EPS) + ADAM_WD * w)
    return delta, m, v


def reference(x, c, w_mod, b_mod, norm_mix_g, norm_ffn_g, lru_w_y, lru_b_y, lru_w_in, lru_b_in, lru_conv_w, lru_conv_b, lru_w_a, lru_b_a, lru_w_x, lru_b_x, lru_lambda, lru_w_out, lru_b_out, pool_w, pool_scale, ffn_w1, ffn_w2, final_norm_g, loss_target, m_w_mod, m_b_mod, m_norm_mix_g, m_norm_ffn_g, m_lru_w_y, m_lru_b_y, m_lru_w_in, m_lru_b_in, m_lru_conv_w, m_lru_conv_b, m_lru_w_a, m_lru_b_a, m_lru_w_x, m_lru_b_x, m_lru_lambda, m_lru_w_out, m_lru_b_out, m_pool_w, m_pool_scale, m_ffn_w1, m_ffn_w2, m_final_norm_g, v_w_mod, v_b_mod, v_norm_mix_g, v_norm_ffn_g, v_lru_w_y, v_lru_b_y, v_lru_w_in, v_lru_b_in, v_lru_conv_w, v_lru_conv_b, v_lru_w_a, v_lru_b_a, v_lru_w_x, v_lru_b_x, v_lru_lambda, v_lru_w_out, v_lru_b_out, v_pool_w, v_pool_scale, v_ffn_w1, v_ffn_w2, v_final_norm_g):
    given = dict(x=x, c=c, w_mod=w_mod, b_mod=b_mod, norm_mix_g=norm_mix_g, norm_ffn_g=norm_ffn_g, lru_w_y=lru_w_y, lru_b_y=lru_b_y, lru_w_in=lru_w_in, lru_b_in=lru_b_in, lru_conv_w=lru_conv_w, lru_conv_b=lru_conv_b, lru_w_a=lru_w_a, lru_b_a=lru_b_a, lru_w_x=lru_w_x, lru_b_x=lru_b_x, lru_lambda=lru_lambda, lru_w_out=lru_w_out, lru_b_out=lru_b_out, pool_w=pool_w, pool_scale=pool_scale, ffn_w1=ffn_w1, ffn_w2=ffn_w2, final_norm_g=final_norm_g, loss_target=loss_target, m_w_mod=m_w_mod, m_b_mod=m_b_mod, m_norm_mix_g=m_norm_mix_g, m_norm_ffn_g=m_norm_ffn_g, m_lru_w_y=m_lru_w_y, m_lru_b_y=m_lru_b_y, m_lru_w_in=m_lru_w_in, m_lru_b_in=m_lru_b_in, m_lru_conv_w=m_lru_conv_w, m_lru_conv_b=m_lru_conv_b, m_lru_w_a=m_lru_w_a, m_lru_b_a=m_lru_b_a, m_lru_w_x=m_lru_w_x, m_lru_b_x=m_lru_b_x, m_lru_lambda=m_lru_lambda, m_lru_w_out=m_lru_w_out, m_lru_b_out=m_lru_b_out, m_pool_w=m_pool_w, m_pool_scale=m_pool_scale, m_ffn_w1=m_ffn_w1, m_ffn_w2=m_ffn_w2, m_final_norm_g=m_final_norm_g, v_w_mod=v_w_mod, v_b_mod=v_b_mod, v_norm_mix_g=v_norm_mix_g, v_norm_ffn_g=v_norm_ffn_g, v_lru_w_y=v_lru_w_y, v_lru_b_y=v_lru_b_y, v_lru_w_in=v_lru_w_in, v_lru_b_in=v_lru_b_in, v_lru_conv_w=v_lru_conv_w, v_lru_conv_b=v_lru_conv_b, v_lru_w_a=v_lru_w_a, v_lru_b_a=v_lru_b_a, v_lru_w_x=v_lru_w_x, v_lru_b_x=v_lru_b_x, v_lru_lambda=v_lru_lambda, v_lru_w_out=v_lru_w_out, v_lru_b_out=v_lru_b_out, v_pool_w=v_pool_w, v_pool_scale=v_pool_scale, v_ffn_w1=v_ffn_w1, v_ffn_w2=v_ffn_w2, v_final_norm_g=v_final_norm_g)
    weights = {n: given[n] for n in TWIN_WEIGHTS}
    shared = {n: given[n] for n in SHARED_INPUTS}
    per_example = {n: given[n] for n in ['x', 'c']}
    grad_fn = _jax.value_and_grad(_loss, argnums=(0, 1))

    def one_microbatch(ex, loss_target):
        ex = dict(ex)
        diff = ex.pop(TWIN_DIFF_INPUT)
        return grad_fn(weights, diff, {**shared, **ex}, loss_target)

    if N_MICROBATCH == 1:
        loss, (grad_w, grad_x) = one_microbatch(per_example, given["loss_target"])
    else:
        def body(carry, xs):
            loss_sum, grad_sum = carry
            l_k, (gw_k, gx_k) = one_microbatch(xs[0], xs[1])
            with _jax.named_scope("update"):
                return (loss_sum + l_k, _jax.tree.map(_jnp.add, grad_sum, gw_k)), gx_k

        init = (_jnp.zeros((), _jnp.float32), _jax.tree.map(_jnp.zeros_like, weights))
        (loss, grad_w), grad_x = _jax.lax.scan(body, init, (per_example, given["loss_target"]))
    with _jax.named_scope("update"):
        delta_w, new_m, new_v = {}, {}, {}
        for n in TWIN_WEIGHTS:
            delta_w[n], new_m[n], new_v[n] = _adamw(weights[n], grad_w[n], given["m_" + n], given["v_" + n])
    return (loss, grad_x, *[grad_w[n] for n in TWIN_WEIGHTS], *[delta_w[n] for n in TWIN_WEIGHTS],
            *[new_m[n] for n in TWIN_WEIGHTS], *[new_v[n] for n in TWIN_WEIGHTS])
```

```python
import functools

import jax
import jax.numpy as jnp
from jax import lax
from jax.experimental import pallas as pl
from jax.experimental.pallas import tpu as pltpu

F32 = jnp.float32
BF16 = jnp.bfloat16
MESH = pl.DeviceIdType.MESH

NQ = 4
NDEV = 8
DEPTH = 4
N_MOD = 6
HEADS = 4
CONV_W = 4
POOL_WINDOWS = (2, 4, 8, 16)
LRU_C = 8.0
EPS = 1e-6
ADAM_LR, ADAM_B1, ADAM_B2, ADAM_EPS, ADAM_WD, ADAM_STEP = 0.001, 0.9, 0.999, 1e-08, 0.01, 10

TM = 512
TT = 256
TP = 256
TK = 512
SMALL_ROWS = 64
VMEM_LIMIT = 60 * 1024 * 1024


def _cparams(*sem):
    return pltpu.CompilerParams(dimension_semantics=tuple(sem), vmem_limit_bytes=VMEM_LIMIT)


def _dot(a, b):
    return jnp.dot(a, b, preferred_element_type=F32)


def _dot_nt(a, b):
    return lax.dot_general(a, b, (((1,), (1,)), ((), ())), preferred_element_type=F32)


def _dot_tn(a, b):
    return lax.dot_general(a, b, (((0,), (0,)), ((), ())), preferred_element_type=F32)


def _resident(shape, index_map):
    return pl.BlockSpec(shape, index_map, pipeline_mode=pl.Buffered(1))


def _rms_fwd(x, g, sc, sh):
    r = lax.rsqrt(jnp.mean(x * x, axis=-1, keepdims=True) + EPS)
    xhat = x * r
    n = xhat * g
    return n * (1.0 + sc) + sh, xhat, r, n


def _rms_bwd(dh, xhat, r, n, g, sc):
    dsh = jnp.sum(dh, axis=0, keepdims=True)
    dsc = jnp.sum(dh * n, axis=0, keepdims=True)
    dn = dh * (1.0 + sc)
    dg = jnp.sum(dn * xhat, axis=0, keepdims=True)
    dxh = dn * g
    dx = r * (dxh - xhat * jnp.mean(dxh * xhat, axis=-1, keepdims=True))
    return dx, dsh, dsc, dg


_GELU_K = 0.7978845608028654
_GELU_C = 0.044715


def _gelu(x):
    t = jnp.tanh(_GELU_K * (x + _GELU_C * x * x * x))
    return 0.5 * x * (1.0 + t), t


def _gelu_grad(x, t):
    return 0.5 * (1.0 + t) + 0.5 * x * (1.0 - t * t) * (_GELU_K * (1.0 + 3.0 * _GELU_C * x * x))


def _neg_expm1(y):
    series = -(y * (1.0 + y * (0.5 + y * (1.0 / 6.0 + y * (1.0 / 24.0 + y * (1.0 / 120.0))))))
    return jnp.where(y > -0.25, series, 1.0 - jnp.exp(y))


def _zero_first(ref):
    @pl.when(pl.program_id(0) == 0)
    def _():
        ref[...] = jnp.zeros_like(ref)


def _my_pos():
    return lax.axis_index("x"), lax.axis_index("y"), lax.axis_index("c")


def _dev_index(x, y, c):
    return 4 * x + 2 * y + c


def _chip_peers(x, y):
    return [(1 - x, y), (x, 1 - y), (1 - x, 1 - y)]


def _all_peers(x, y, c):
    return [(px, py, c) for (px, py) in _chip_peers(x, y)] + [(x, y, 1 - c)] + \
           [(px, py, 1 - c) for (px, py) in _chip_peers(x, y)]


def _gather_weights(pack, small):
    R, D = pack.shape

    def body(pack_ref, small_ref, g_ref, sg_ref, big_send, big_recv, sm_send, sm_recv, loc_sem):
        x, y, c = _my_pos()
        q = 2 * x + y
        me = _dev_index(x, y, c)
        own = pltpu.make_async_copy(pack_ref, g_ref.at[q], loc_sem)
        own.start()
        sg_ref[me] = small_ref[...]
        sends = []
        for k, (px, py) in enumerate(_chip_peers(x, y)):
            cp = pltpu.make_async_remote_copy(pack_ref, g_ref.at[q], big_send.at[k], big_recv.at[k],
                                              device_id=(px, py, c), device_id_type=MESH)
            cp.start()
            sends.append(cp)
        for k, peer in enumerate(_all_peers(x, y, c)):
            cp = pltpu.make_async_remote_copy(small_ref, sg_ref.at[me], sm_send.at[k], sm_recv.at[k],
                                              device_id=peer, device_id_type=MESH)
            cp.start()
            sends.append(cp)
        for k, (px, py) in enumerate(_chip_peers(x, y)):
            pltpu.make_async_remote_copy(pack_ref, g_ref.at[2 * px + py], big_send.at[k], big_recv.at[k],
                                         device_id=(px, py, c), device_id_type=MESH).wait_recv()
        for k, (px, py, pc) in enumerate(_all_peers(x, y, c)):
            pltpu.make_async_remote_copy(small_ref, sg_ref.at[_dev_index(px, py, pc)], sm_send.at[k], sm_recv.at[k],
                                         device_id=(px, py, pc), device_id_type=MESH).wait_recv()
        for cp in sends:
            cp.wait_send()
        own.wait()

    return pl.pallas_call(
        body, name="gather_weights",
        out_shape=(jax.ShapeDtypeStruct((NQ, R, D), pack.dtype), jax.ShapeDtypeStruct((NDEV,) + small.shape, small.dtype)),
        in_specs=[pl.BlockSpec(memory_space=pl.ANY), pl.BlockSpec(memory_space=pltpu.VMEM)],
        out_specs=(pl.BlockSpec(memory_space=pl.ANY), pl.BlockSpec(memory_space=pltpu.VMEM)),
        scratch_shapes=[pltpu.SemaphoreType.DMA((3,)), pltpu.SemaphoreType.DMA((3,)),
                        pltpu.SemaphoreType.DMA((7,)), pltpu.SemaphoreType.DMA((7,)), pltpu.SemaphoreType.DMA],
        compiler_params=pltpu.CompilerParams(has_side_effects=True),
    )(pack, small)


def _exchange_mod(modpart):
    _, L, Cs = modpart.shape

    def body(part_ref, out_ref, send, recv):
        x, y, c = _my_pos()
        q = 2 * x + y
        me = _dev_index(x, y, c)
        out_ref[q] = part_ref[me]
        sends = []
        for k, (px, py) in enumerate(_chip_peers(x, y)):
            cp = pltpu.make_async_remote_copy(part_ref.at[_dev_index(px, py, c)], out_ref.at[q], send.at[k], recv.at[k],
                                              device_id=(px, py, c), device_id_type=MESH)
            cp.start()
            sends.append(cp)
        for k, (px, py) in enumerate(_chip_peers(x, y)):
            pltpu.make_async_remote_copy(part_ref.at[me], out_ref.at[2 * px + py], send.at[k], recv.at[k],
                                         device_id=(px, py, c), device_id_type=MESH).wait_recv()
        for cp in sends:
            cp.wait_send()

    return pl.pallas_call(
        body, name="exchange_mod",
        out_shape=jax.ShapeDtypeStruct((NQ, L, Cs), modpart.dtype),
        in_specs=[pl.BlockSpec(memory_space=pltpu.VMEM)],
        out_specs=pl.BlockSpec(memory_space=pltpu.VMEM),
        scratch_shapes=[pltpu.SemaphoreType.DMA((3,)), pltpu.SemaphoreType.DMA((3,))],
        compiler_params=pltpu.CompilerParams(has_side_effects=True),
    )(modpart)


def _scatter_grads(dwp, small):
    _, R, D = dwp.shape

    def body(dwp_ref, small_ref, rb_ref, sg_ref, sum_ref, big_send, big_recv, sm_send, sm_recv):
        x, y, c = _my_pos()
        me = _dev_index(x, y, c)
        sg_ref[me] = small_ref[...]
        sends = []
        for k, (px, py) in enumerate(_chip_peers(x, y)):
            cp = pltpu.make_async_remote_copy(dwp_ref.at[2 * px + py], rb_ref.at[k], big_send.at[k], big_recv.at[k],
                                              device_id=(px, py, c), device_id_type=MESH)
            cp.start()
            sends.append(cp)
        for k, peer in enumerate(_all_peers(x, y, c)):
            cp = pltpu.make_async_remote_copy(small_ref, sg_ref.at[me], sm_send.at[k], sm_recv.at[k],
                                              device_id=peer, device_id_type=MESH)
            cp.start()
            sends.append(cp)
        for k, (px, py, pc) in enumerate(_all_peers(x, y, c)):
            pltpu.make_async_remote_copy(small_ref, sg_ref.at[_dev_index(px, py, pc)], sm_send.at[k], sm_recv.at[k],
                                         device_id=(px, py, pc), device_id_type=MESH).wait_recv()
        acc = sg_ref[0]
        for d in range(1, NDEV):
            acc = acc + sg_ref[d]
        sum_ref[...] = acc
        for k, (px, py) in enumerate(_chip_peers(x, y)):
            pltpu.make_async_remote_copy(dwp_ref.at[0], rb_ref.at[k], big_send.at[k], big_recv.at[k],
                                         device_id=(px, py, c), device_id_type=MESH).wait_recv()
        for cp in sends:
            cp.wait_send()

    return pl.pallas_call(
        body, name="scatter_grads",
        out_shape=(jax.ShapeDtypeStruct((3, R, D), dwp.dtype), jax.ShapeDtypeStruct((NDEV,) + small.shape, small.dtype),
                   jax.ShapeDtypeStruct(small.shape, small.dtype)),
        in_specs=[pl.BlockSpec(memory_space=pl.ANY), pl.BlockSpec(memory_space=pltpu.VMEM)],
        out_specs=(pl.BlockSpec(memory_space=pl.ANY), pl.BlockSpec(memory_space=pltpu.VMEM),
                   pl.BlockSpec(memory_space=pltpu.VMEM)),
        scratch_shapes=[pltpu.SemaphoreType.DMA((3,)), pltpu.SemaphoreType.DMA((3,)),
                        pltpu.SemaphoreType.DMA((7,)), pltpu.SemaphoreType.DMA((7,))],
        compiler_params=pltpu.CompilerParams(has_side_effects=True),
    )(dwp, small)


def _sibling_exchange(p):
    def body(p_ref, o_ref, send, recv):
        x, y, c = _my_pos()
        cp = pltpu.make_async_remote_copy(p_ref, o_ref, send, recv, device_id=(x, y, 1 - c), device_id_type=MESH)
        cp.start()
        cp.wait()

    return pl.pallas_call(
        body, name="sibling_exchange",
        out_shape=jax.ShapeDtypeStruct(p.shape, p.dtype),
        in_specs=[pl.BlockSpec(memory_space=pl.ANY)],
        out_specs=pl.BlockSpec(memory_space=pl.ANY),
        scratch_shapes=[pltpu.SemaphoreType.DMA, pltpu.SemaphoreType.DMA],
        compiler_params=pltpu.CompilerParams(has_side_effects=True),
    )(p)


def _mod_part(c_all, w_mod, b_mod_sh):
    L, D, Cs = w_mod.shape
    tn = 512 if Cs % 512 == 0 else Cs

    def body(c_ref, w_ref, b_ref, o_ref):
        cv = c_ref[...]
        cond = cv * jax.nn.sigmoid(cv)
        o_ref[...] = jnp.dot(cond, w_ref[...], preferred_element_type=F32, precision=lax.Precision.HIGHEST) + b_ref[...]

    return pl.pallas_call(
        body, name="mod_part", grid=(L, Cs // tn),
        out_shape=jax.ShapeDtypeStruct((L, NDEV, Cs), F32),
        in_specs=[pl.BlockSpec((NDEV, D), lambda i, j: (0, 0)),
                  pl.BlockSpec((None, D, tn), lambda i, j: (i, 0, j)),
                  pl.BlockSpec((None, 1, tn), lambda i, j: (i, 0, j))],
        out_specs=pl.BlockSpec((None, NDEV, tn), lambda i, j: (i, 0, j)),
        compiler_params=_cparams("parallel", "parallel"),
    )(c_all, w_mod, b_mod_sh)


def _adam(w, g, m, v):
    m2 = ADAM_B1 * m + (1.0 - ADAM_B1) * g
    v2 = ADAM_B2 * v + (1.0 - ADAM_B2) * (g * g)
    m_hat = m2 / (1.0 - ADAM_B1 ** ADAM_STEP)
    v_hat = v2 / (1.0 - ADAM_B2 ** ADAM_STEP)
    delta = -ADAM_LR * (m_hat / (jnp.sqrt(v_hat) + ADAM_EPS) + ADAM_WD * w)
    return delta, m2, v2


def _wmod_update(c_all_t, dmod_sh, w, m, v):
    L, D, Cs = w.shape
    td = 256 if D % 256 == 0 else D

    def body(ct_ref, d_ref, w_ref, m_ref, v_ref, g_ref, dl_ref, m2_ref, v2_ref):
        cv = ct_ref[...]
        cond = cv * jax.nn.sigmoid(cv)
        g = cond[:, 0:1] * d_ref[0:1, :]
        for b in range(1, NDEV):
            g = g + cond[:, b:b + 1] * d_ref[b:b + 1, :]
        g_ref[...] = g
        dl_ref[...], m2_ref[...], v2_ref[...] = _adam(w_ref[...], g, m_ref[...], v_ref[...])

    blk = pl.BlockSpec((None, td, Cs), lambda i, j: (i, j, 0))
    out = jax.ShapeDtypeStruct((L, D, Cs), F32)
    return pl.pallas_call(
        body, name="wmod_update", grid=(L, D // td),
        out_shape=(out, out, out, out),
        in_specs=[pl.BlockSpec((td, NDEV), lambda i, j: (j, 0)),
                  pl.BlockSpec((None, NDEV, Cs), lambda i, j: (i, 0, 0)), blk, blk, blk],
        out_specs=(blk, blk, blk, blk),
        compiler_params=_cparams("parallel", "parallel"),
    )(c_all_t, dmod_sh, w, m, v)


def _adam_rows(w, m, v, pa, pb, row_off, name):
    rows, C = w.shape
    tr = 512 if rows % 512 == 0 else (128 if rows % 128 == 0 else rows)
    assert row_off % tr == 0
    ob = row_off // tr

    def body(w_ref, m_ref, v_ref, pa_ref, pb_ref, g_ref, dl_ref, m2_ref, v2_ref):
        g = pa_ref[...] + pb_ref[...]
        g_ref[...] = g
        dl_ref[...], m2_ref[...], v2_ref[...] = _adam(w_ref[...], g, m_ref[...], v_ref[...])

    blk = pl.BlockSpec((tr, C), lambda i: (i, 0))
    pblk = pl.BlockSpec((tr, C), lambda i: (ob + i, 0))
    out = jax.ShapeDtypeStruct((rows, C), F32)
    return pl.pallas_call(
        body, name=name, grid=(rows // tr,), out_shape=(out, out, out, out),
        in_specs=[blk, blk, blk, pblk, pblk], out_specs=(blk, blk, blk, blk),
        compiler_params=_cparams("parallel"),
    )(w, m, v, pa, pb)


def _adam_small(w, g, m, v):
    def body(w_ref, g_ref, m_ref, v_ref, dl_ref, m2_ref, v2_ref):
        dl_ref[...], m2_ref[...], v2_ref[...] = _adam(w_ref[...], g_ref[...], m_ref[...], v_ref[...])

    out = jax.ShapeDtypeStruct(w.shape, F32)
    return pl.pallas_call(body, name="adam_small", out_shape=(out, out, out))(w, g, m, v)


def _sum_partials(own, rb):
    R, D = own.shape
    tr = 512 if R % 512 == 0 else (128 if R % 128 == 0 else R)

    def body(o_ref, r_ref, p_ref):
        acc = o_ref[...].astype(F32)
        for k in range(3):
            acc = acc + r_ref[k].astype(F32)
        p_ref[...] = acc

    return pl.pallas_call(
        body, name="sum_partials", grid=(R // tr,), out_shape=jax.ShapeDtypeStruct((R, D), F32),
        in_specs=[pl.BlockSpec((tr, D), lambda i: (i, 0)), pl.BlockSpec((3, tr, D), lambda i: (0, i, 0))],
        out_specs=pl.BlockSpec((tr, D), lambda i: (i, 0)),
        compiler_params=_cparams("parallel"),
    )(own, rb)


def _wspec(rows, D, off):
    assert off % rows == 0
    return _resident((NQ, rows, D), lambda i, _b=off // rows: (0, _b, 0))


def _ffn_fwd_inner(x1, mod_ref, gf_ref, w1_ref, w2_ref, h2_ref, a_ref, z_ref, x2_ref):
    h2 = _rms_fwd(x1, gf_ref[...], mod_ref[4:5, :], mod_ref[3:4, :])[0]
    h2b = h2.astype(BF16)
    h2_ref[...] = h2b
    f4 = w1_ref.shape[2]
    z = jnp.zeros(x1.shape, F32)
    for q in range(NQ):
        a = jnp.maximum(_dot(h2b, w1_ref[q]), 0.0)
        a_ref[:, q * f4:(q + 1) * f4] = a.astype(BF16)
        z = z + _dot((a * a).astype(BF16), w2_ref[q])
    z_ref[...] = z.astype(BF16)
    x2_ref[...] = x1 + mod_ref[5:6, :] * z


def _lru_in_fwd(x, mod_l, g_mix, G, off_wy, off_win, b_y, b_in):
    S, D = x.shape
    tm = min(TM, S)
    dq = D // NQ

    def body(x_ref, mod_ref, g_ref, wy_ref, win_ref, by_ref, bin_ref, h_ref, gb_ref, xr_ref):
        h = _rms_fwd(x_ref[...], g_ref[...], mod_ref[1:2, :], mod_ref[0:1, :])[0]
        hb = h.astype(BF16)
        h_ref[...] = hb
        gb_ref[...] = _dot(hb, wy_ref[...].reshape(D, D)) + by_ref[...]
        xr_ref[...] = _dot(hb, win_ref[...].reshape(D, D)) + bin_ref[...]

    tile = pl.BlockSpec((tm, D), lambda i: (i, 0))
    row = pl.BlockSpec((1, D), lambda i: (0, 0))
    return pl.pallas_call(
        body, name="lru_in_fwd", grid=(S // tm,),
        out_shape=(jax.ShapeDtypeStruct((S, D), BF16), jax.ShapeDtypeStruct((S, D), F32), jax.ShapeDtypeStruct((S, D), F32)),
        in_specs=[tile, pl.BlockSpec((8, D), lambda i: (0, 0)), row, _wspec(dq, D, off_wy), _wspec(dq, D, off_win), row, row],
        out_specs=(tile, tile, tile),
        compiler_params=_cparams("parallel"),
    )(x, mod_l, g_mix, G, G, b_y, b_in)


def _heads_dot(xb, w_ref, hd, nt=False):
    outs = []
    for h in range(HEADS):
        xs = xb[:, h * hd:(h + 1) * hd]
        outs.append(_dot_nt(xs, w_ref[h]) if nt else _dot(xs, w_ref[h]))
    return jnp.concatenate(outs, axis=1)


def _lru_gates(xc, wa_ref, ba, wx_ref, bx, lam, hd):
    xcb = xc.astype(BF16)
    gate_r = jax.nn.sigmoid(_heads_dot(xcb, wa_ref, hd) + ba)
    gate_i = jax.nn.sigmoid(_heads_dot(xcb, wx_ref, hd) + bx)
    ls = jax.nn.log_sigmoid(lam)
    log_a = LRU_C * gate_r * ls
    a = jnp.exp(log_a)
    mult = jnp.sqrt(_neg_expm1(2.0 * log_a))
    return xcb, gate_r, gate_i, ls, a, mult


def _conv_taps(xext, cw, tt):
    acc = cw[0:1, :] * xext[pl.ds(8 - (CONV_W - 1), tt), :]
    for k in range(1, CONV_W):
        acc = acc + cw[k:k + 1, :] * xext[pl.ds(8 - (CONV_W - 1) + k, tt), :]
    return acc


def _lru_scan_fwd(xr0, gb, cw, cb, wa, ba, wx, bx, lam):
    S, W = xr0.shape
    tt = min(TT, S)
    hd = W // HEADS

    def body(xr_ref, xrh_ref, gb_ref, cw_ref, cb_ref, wa_ref, ba_ref, wx_ref, bx_ref, lam_ref,
             hs_ref, p_ref, xext, a_s, u_s, carry):
        i = pl.program_id(0)

        @pl.when(i == 0)
        def _():
            carry[...] = jnp.zeros_like(carry)

        xext[0:8, :] = jnp.where(i > 0, xrh_ref[...], 0.0)
        xext[pl.ds(8, tt), :] = xr_ref[...]
        xc = _conv_taps(xext, cw_ref[...], tt) + cb_ref[...]
        _, _, gate_i, _, a, mult = _lru_gates(xc, wa_ref, ba_ref[...], wx_ref, bx_ref[...], lam_ref[...], hd)
        a_s[...] = a
        u_s[...] = mult * (gate_i * xc)
        row = lax.broadcasted_iota(jnp.int32, (8, W), 0)

        def step(k, _):
            off = pl.multiple_of(k * 8, 8)
            A = a_s[pl.ds(off, 8), :]
            U = u_s[pl.ds(off, 8), :]
            for d in (1, 2, 4):
                keep = row >= d
                Us = jnp.where(keep, pltpu.roll(U, d, 0), 0.0)
                As = jnp.where(keep, pltpu.roll(A, d, 0), 1.0)
                U = U + A * Us
                A = A * As
            H = U + A * carry[...]
            hs_ref[pl.ds(off, 8), :] = H
            carry[...] = jnp.broadcast_to(H[7:8, :], (8, W))
            return 0

        lax.fori_loop(0, tt // 8, step, 0)
        p_ref[...] = (hs_ref[...] * _gelu(gb_ref[...])[0]).astype(BF16)

    tile = pl.BlockSpec((tt, W), lambda i: (i, 0))
    halo = pl.BlockSpec((8, W), lambda i: (jnp.maximum(i * (tt // 8) - 1, 0), 0))
    row = pl.BlockSpec((1, W), lambda i: (0, 0))
    wblk = pl.BlockSpec((HEADS, hd, hd), lambda i: (0, 0, 0))
    return pl.pallas_call(
        body, name="lru_scan_fwd", grid=(S // tt,),
        out_shape=(jax.ShapeDtypeStruct((S, W), F32), jax.ShapeDtypeStruct((S, W), BF16)),
        in_specs=[tile, halo, tile, pl.BlockSpec((CONV_W, W), lambda i: (0, 0)), row, wblk, row, wblk, row, row],
        out_specs=(tile, tile),
        scratch_shapes=[pltpu.VMEM((tt + 8, W), F32), pltpu.VMEM((tt, W), F32), pltpu.VMEM((tt, W), F32),
                        pltpu.VMEM((8, W), F32)],
        compiler_params=_cparams("arbitrary"),
    )(xr0, xr0, gb, cw, cb, wa, ba, wx, bx, lam)


def _ffn_out_shapes(S, D, F):
    return (jax.ShapeDtypeStruct((S, D), F32), jax.ShapeDtypeStruct((S, D), BF16), jax.ShapeDtypeStruct((S, F), BF16),
            jax.ShapeDtypeStruct((S, D), BF16), jax.ShapeDtypeStruct((S, D), F32))


def _lru_mix_ffn_fwd(x, p, mod_l, g_ffn, b_out, G, off_wout, off_w1, off_w2, F):
    S, D = x.shape
    tm = min(TM, S)
    f4 = F // NQ

    def body(x_ref, p_ref, mod_ref, gf_ref, bo_ref, wo_ref, w1_ref, w2_ref,
             y_ref, x1_ref, h2_ref, a_ref, z_ref, x2_ref):
        y = _dot(p_ref[...], wo_ref[...].reshape(D, D)) + bo_ref[...]
        y_ref[...] = y.astype(BF16)
        x1 = x_ref[...] + mod_ref[2:3, :] * y
        x1_ref[...] = x1
        _ffn_fwd_inner(x1, mod_ref, gf_ref, w1_ref, w2_ref, h2_ref, a_ref, z_ref, x2_ref)

    tile = pl.BlockSpec((tm, D), lambda i: (i, 0))
    row = pl.BlockSpec((1, D), lambda i: (0, 0))
    return pl.pallas_call(
        body, name="lru_mix_ffn_fwd", grid=(S // tm,),
        out_shape=(jax.ShapeDtypeStruct((S, D), BF16),) + _ffn_out_shapes(S, D, F),
        in_specs=[tile, tile, pl.BlockSpec((8, D), lambda i: (0, 0)), row, row,
                  _wspec(D // NQ, D, off_wout), _wspec(D, f4, off_w1), _wspec(f4, D, off_w2)],
        out_specs=(tile, tile, tile, pl.BlockSpec((tm, F), lambda i: (i, 0)), tile, tile),
        compiler_params=_cparams("parallel"),
    )(x, p, mod_l, g_ffn, b_out, G, G, G)


def _window_vec(D):
    gd = D // len(POOL_WINDOWS)
    lane = lax.broadcasted_iota(jnp.int32, (1, D), 1)
    w = jnp.full((1, D), float(POOL_WINDOWS[0]), F32)
    for g in range(1, len(POOL_WINDOWS)):
        w = jnp.where(lane >= g * gd, float(POOL_WINDOWS[g]), w)
    return w


def _pool_mix_ffn_fwd(x, mod_l, g_mix, pw, ps, g_ffn, G, off_w1, off_w2, F):
    S, D = x.shape
    tm = min(TP, S)
    f4 = F // NQ
    gd = D // len(POOL_WINDOWS)
    n = tm + 24

    def body(x_ref, xh_ref, mod_ref, gm_ref, pw_ref, ps_ref, gf_ref, w1_ref, w2_ref,
             pl_ref, x1_ref, h2_ref, a_ref, z_ref, x2_ref, ext, b1, b2):
        i = pl.program_id(0)
        g, sc, sh = gm_ref[...], mod_ref[1:2, :], mod_ref[0:1, :]
        xv = x_ref[...]
        h = _rms_fwd(xv, g, sc, sh)[0]
        hh = _rms_fwd(xh_ref[...], g, sc, sh)[0]
        zeros8 = jnp.zeros((8, D), F32)
        ext[0:8, :] = zeros8
        b1[0:8, :] = zeros8
        b2[0:8, :] = zeros8
        ext[8:24, :] = jnp.where(i > 0, hh, 0.0)
        ext[pl.ds(24, tm), :] = h
        m = n - 8
        b1[pl.ds(8, m), :] = ext[pl.ds(8, m), :] + ext[pl.ds(7, m), :]
        b2[pl.ds(8, m), gd:] = b1[pl.ds(8, m), gd:] + b1[pl.ds(6, m), gd:]
        b1[pl.ds(8, m), 2 * gd:] = b2[pl.ds(8, m), 2 * gd:] + b2[pl.ds(4, m), 2 * gd:]
        b2[pl.ds(8, m), 3 * gd:] = b1[pl.ds(8, m), 3 * gd:] + b1[pl.ds(0, m), 3 * gd:]
        wsum = jnp.concatenate([b1[pl.ds(24, tm), 0:gd], b2[pl.ds(24, tm), gd:2 * gd],
                                b1[pl.ds(24, tm), 2 * gd:3 * gd], b2[pl.ds(24, tm), 3 * gd:]], axis=1)
        t1 = (lax.broadcasted_iota(jnp.int32, (tm, 1), 0) + (i * tm + 1)).astype(F32)
        cnt = jnp.minimum(t1, _window_vec(D))
        pooled = (wsum / cnt - h).astype(BF16)
        pl_ref[...] = pooled
        y = _heads_dot(pooled, pw_ref, gd) * ps_ref[...]
        x1 = xv + mod_ref[2:3, :] * y
        x1_ref[...] = x1
        _ffn_fwd_inner(x1, mod_ref, gf_ref, w1_ref, w2_ref, h2_ref, a_ref, z_ref, x2_ref)

    tile = pl.BlockSpec((tm, D), lambda i: (i, 0))
    halo = pl.BlockSpec((16, D), lambda i: (jnp.maximum(i * (tm // 16) - 1, 0), 0))
    row = pl.BlockSpec((1, D), lambda i: (0, 0))
    return pl.pallas_call(
        body, name="pool_mix_ffn_fwd", grid=(S // tm,),
        out_shape=(jax.ShapeDtypeStruct((S, D), BF16),) + _ffn_out_shapes(S, D, F),
        in_specs=[tile, halo, pl.BlockSpec((8, D), lambda i: (0, 0)), row,
                  pl.BlockSpec((len(POOL_WINDOWS), gd, gd), lambda i: (0, 0, 0)), row, row,
                  _wspec(D, f4, off_w1), _wspec(f4, D, off_w2)],
        out_specs=(tile, tile, tile, pl.BlockSpec((tm, F), lambda i: (i, 0)), tile, tile),
        scratch_shapes=[pltpu.VMEM((n, D), F32), pltpu.VMEM((n, D), F32), pltpu.VMEM((n, D), F32)],
        compiler_params=_cparams("parallel"),
    )(x, x, mod_l, g_mix, pw, ps, g_ffn, G, G)


def _final_loss_bwd(x, g, target):
    S, D = x.shape
    tm = min(TM, S)

    def body(x_ref, g_ref, t_ref, dx_ref, acc_ref):
        _zero_first(acc_ref)
        xv = x_ref[...]
        gv = g_ref[...]
        r = lax.rsqrt(jnp.mean(xv * xv, axis=-1, keepdims=True) + EPS)
        xhat = xv * r
        err = xhat * gv - t_ref[...]
        acc_ref[0:1, :] += jnp.sum(err * err, axis=0, keepdims=True)
        dy = err * (1.0 / D)
        acc_ref[1:2, :] += jnp.sum(dy * xhat, axis=0, keepdims=True)
        dxh = dy * gv
        dx_ref[...] = r * (dxh - xhat * jnp.mean(dxh * xhat, axis=-1, keepdims=True))

    tile = pl.BlockSpec((tm, D), lambda i: (i, 0))
    return pl.pallas_call(
        body, name="final_loss_bwd", grid=(S // tm,),
        out_shape=(jax.ShapeDtypeStruct((S, D), F32), jax.ShapeDtypeStruct((8, D), F32)),
        in_specs=[tile, pl.BlockSpec((1, D), lambda i: (0, 0)), tile],
        out_specs=(tile, pl.BlockSpec((8, D), lambda i: (0, 0))),
        compiler_params=_cparams("arbitrary"),
    )(x, g, target)


def _ffn_bwd(dx2, x1, a, z, mod_l, g_ffn, G, off_w1, off_w2):
    S, D = dx2.shape
    F = a.shape[1]
    f4 = F // NQ
    tm = min(TM, S)

    def body(dx2_ref, x1_ref, a_ref, z_ref, mod_ref, gf_ref, w1_ref, w2_ref, dx1_ref, du_ref, dz_ref, acc_ref):
        _zero_first(acc_ref)
        dx2v = dx2_ref[...]
        acc_ref[5:6, :] += jnp.sum(dx2v * z_ref[...].astype(F32), axis=0, keepdims=True)
        dzb = (dx2v * mod_ref[5:6, :]).astype(BF16)
        dz_ref[...] = dzb
        dh2 = jnp.zeros((tm, D), F32)
        for q in range(NQ):
            av = a_ref[:, q * f4:(q + 1) * f4].astype(F32)
            du = (_dot_nt(dzb, w2_ref[q]) * (2.0 * av)).astype(BF16)
            du_ref[:, q * f4:(q + 1) * f4] = du
            dh2 = dh2 + _dot_nt(du, w1_ref[q])
        g, sc = gf_ref[...], mod_ref[4:5, :]
        _, xhat, r, n = _rms_fwd(x1_ref[...], g, sc, mod_ref[3:4, :])
        dx, dsh, dsc, dg = _rms_bwd(dh2, xhat, r, n, g, sc)
        acc_ref[3:4, :] += dsh
        acc_ref[4:5, :] += dsc
        acc_ref[7:8, :] += dg
        dx1_ref[...] = dx2v + dx

    tile = pl.BlockSpec((tm, D), lambda i: (i, 0))
    wide = pl.BlockSpec((tm, F), lambda i: (i, 0))
    return pl.pallas_call(
        body, name="ffn_bwd", grid=(S // tm,),
        out_shape=(jax.ShapeDtypeStruct((S, D), F32), jax.ShapeDtypeStruct((S, F), BF16),
                   jax.ShapeDtypeStruct((S, D), BF16), jax.ShapeDtypeStruct((8, D), F32)),
        in_specs=[tile, tile, wide, tile, pl.BlockSpec((8, D), lambda i: (0, 0)), pl.BlockSpec((1, D), lambda i: (0, 0)),
                  _wspec(D, f4, off_w1), _wspec(f4, D, off_w2)],
        out_specs=(tile, wide, tile, pl.BlockSpec((8, D), lambda i: (0, 0))),
        compiler_params=_cparams("arbitrary"),
    )(dx2, x1, a, z, mod_l, g_ffn, G, G)


def _matmul_tn(a, b, dwp, row_off, m_blocks, n_blocks, out_by, square_a, name):
    S = a.shape[0]
    bm = a.shape[1] // m_blocks
    bn = b.shape[1] // n_blocks
    tk = min(TK, S)
    nk = S // tk
    assert row_off % bm == 0
    ob = row_off // bm
    if out_by == "n":
        omap = lambda mi, ni, ki: (ni, ob + mi, 0)
    else:
        omap = lambda mi, ni, ki: (mi, ob + ni, 0)

    def body(a_ref, b_ref, dwp_ref, o_ref, acc):
        k = pl.program_id(2)

        @pl.when(k == 0)
        def _():
            acc[...] = jnp.zeros_like(acc)

        av = a_ref[...]
        if square_a:
            af = av.astype(F32)
            av = (af * af).astype(BF16)
        acc[...] += _dot_tn(av, b_ref[...])

        @pl.when(k == nk - 1)
        def _():
            o_ref[...] = acc[...].astype(o_ref.dtype)

    return pl.pallas_call(
        body, name=name, grid=(m_blocks, n_blocks, nk),
        out_shape=jax.ShapeDtypeStruct(dwp.shape, dwp.dtype),
        in_specs=[pl.BlockSpec((tk, bm), lambda mi, ni, ki: (ki, mi)),
                  pl.BlockSpec((tk, bn), lambda mi, ni, ki: (ki, ni)),
                  pl.BlockSpec(memory_space=pl.ANY)],
        out_specs=pl.BlockSpec((None, bm, bn), omap),
        scratch_shapes=[pltpu.VMEM((bm, bn), F32)],
        input_output_aliases={2: 0},
        compiler_params=_cparams("parallel", "parallel", "arbitrary"),
    )(a, b, dwp)


def _lru_out_bwd(dx1, y, mod_l, G, off_wout):
    S, D = dx1.shape
    tm = min(TM, S)

    def body(dx1_ref, y_ref, mod_ref, wo_ref, dy_ref, dp_ref, acc_ref):
        _zero_first(acc_ref)
        dx1v = dx1_ref[...]
        acc_ref[2:3, :] += jnp.sum(dx1v * y_ref[...].astype(F32), axis=0, keepdims=True)
        dy = dx1v * mod_ref[2:3, :]
        acc_ref[3:4, :] += jnp.sum(dy, axis=0, keepdims=True)
        dyb = dy.astype(BF16)
        dy_ref[...] = dyb
        dp_ref[...] = _dot_nt(dyb, wo_ref[...].reshape(D, D))

    tile = pl.BlockSpec((tm, D), lambda i: (i, 0))
    return pl.pallas_call(
        body, name="lru_out_bwd", grid=(S // tm,),
        out_shape=(jax.ShapeDtypeStruct((S, D), BF16), jax.ShapeDtypeStruct((S, D), F32), jax.ShapeDtypeStruct((8, D), F32)),
        in_specs=[tile, tile, pl.BlockSpec((8, D), lambda i: (0, 0)), _wspec(D // NQ, D, off_wout)],
        out_specs=(tile, tile, pl.BlockSpec((8, D), lambda i: (0, 0))),
        compiler_params=_cparams("arbitrary"),
    )(dx1, y, mod_l, G)


def _lru_scan_bwd(dp, xr0, gb, hs, cw, cb, wa, ba, wx, bx, lam):
    S, W = xr0.shape
    tt = min(TT, S)
    nb = S // tt
    hd = W // HEADS

    def body(dp_ref, xr_ref, xrh_ref, gb_ref, hs_ref, hsh_ref, cw_ref, cb_ref, wa_ref, ba_ref, wx_ref, bx_ref, lam_ref,
             dgb_ref, dxr_ref, sm_ref, dwa_ref, dwx_ref, xext, hext, qext, dext, a_s, b_s, qc, dc):
        i = pl.program_id(0)
        blk = nb - 1 - i

        @pl.when(i == 0)
        def _():
            sm_ref[...] = jnp.zeros_like(sm_ref)
            dwa_ref[...] = jnp.zeros_like(dwa_ref)
            dwx_ref[...] = jnp.zeros_like(dwx_ref)
            qc[...] = jnp.zeros_like(qc)
            dc[...] = jnp.zeros_like(dc)

        xext[0:8, :] = jnp.where(blk > 0, xrh_ref[...], 0.0)
        xext[pl.ds(8, tt), :] = xr_ref[...]
        hext[0:8, :] = jnp.where(blk > 0, hsh_ref[...], 0.0)
        hext[pl.ds(8, tt), :] = hs_ref[...]
        cw = cw_ref[...]
        lam = lam_ref[...]
        xc = _conv_taps(xext, cw, tt) + cb_ref[...]
        xcb, gate_r, gate_i, ls, a, mult = _lru_gates(xc, wa_ref, ba_ref[...], wx_ref, bx_ref[...], lam, hd)

        gbv = gb_ref[...]
        gate, th = _gelu(gbv)
        dpv = dp_ref[...]
        dgb = dpv * hs_ref[...] * _gelu_grad(gbv, th)
        dgb_ref[...] = dgb.astype(BF16)
        sm_ref[9:10, :] += jnp.sum(dgb, axis=0, keepdims=True)
        dhs = dpv * gate

        a_s[...] = a
        b_s[...] = a * dhs
        qext[pl.ds(tt, 8), :] = qc[...]
        row = lax.broadcasted_iota(jnp.int32, (8, W), 0)

        def step(k, _):
            off = pl.multiple_of((tt // 8 - 1 - k) * 8, 8)
            A = a_s[pl.ds(off, 8), :]
            B = b_s[pl.ds(off, 8), :]
            for d in (1, 2, 4):
                keep = row < 8 - d
                Bs = jnp.where(keep, pltpu.roll(B, 8 - d, 0), 0.0)
                As = jnp.where(keep, pltpu.roll(A, 8 - d, 0), 1.0)
                B = B + A * Bs
                A = A * As
            Q = B + A * qc[...]
            qext[pl.ds(off, 8), :] = Q
            qc[...] = jnp.broadcast_to(Q[0:1, :], (8, W))
            return 0

        lax.fori_loop(0, tt // 8, step, 0)
        gsc = dhs + qext[pl.ds(1, tt), :]
        da = gsc * hext[pl.ds(7, tt), :]
        t1 = gsc * xc
        dmult = t1 * gate_i
        dgate_i = t1 * mult
        dxc = gsc * (mult * gate_i)
        dlog_a = da * a - dmult * (a * a) / mult
        dgate_r = dlog_a * (LRU_C * ls)
        sm_ref[7:8, :] += jnp.sum(dlog_a * (LRU_C * gate_r), axis=0, keepdims=True)
        dga = dgate_r * gate_r * (1.0 - gate_r)
        dgx = dgate_i * gate_i * (1.0 - gate_i)
        sm_ref[5:6, :] += jnp.sum(dga, axis=0, keepdims=True)
        sm_ref[6:7, :] += jnp.sum(dgx, axis=0, keepdims=True)
        dgab = dga.astype(BF16)
        dgxb = dgx.astype(BF16)
        dxc = dxc + _heads_dot(dgab, wa_ref, hd, nt=True) + _heads_dot(dgxb, wx_ref, hd, nt=True)
        for h in range(HEADS):
            sl = slice(h * hd, (h + 1) * hd)
            dwa_ref[h] += _dot_tn(xcb[:, sl], dgab[:, sl])
            dwx_ref[h] += _dot_tn(xcb[:, sl], dgxb[:, sl])
        sm_ref[4:5, :] += jnp.sum(dxc, axis=0, keepdims=True)
        for k in range(CONV_W):
            sm_ref[k:k + 1, :] += jnp.sum(dxc * xext[pl.ds(8 - (CONV_W - 1) + k, tt), :], axis=0, keepdims=True)
        dext[pl.ds(0, tt), :] = dxc
        dext[pl.ds(tt, 8), :] = dc[...]
        dxr = cw[0:1, :] * dext[pl.ds(CONV_W - 1, tt), :]
        for k in range(1, CONV_W):
            dxr = dxr + cw[k:k + 1, :] * dext[pl.ds(CONV_W - 1 - k, tt), :]
        dc[...] = dext[0:8, :]
        sm_ref[8:9, :] += jnp.sum(dxr, axis=0, keepdims=True)
        dxr_ref[...] = dxr.astype(BF16)

        @pl.when(i == nb - 1)
        def _():
            sm_ref[7:8, :] = sm_ref[7:8, :] * jax.nn.sigmoid(-lam)

    rev = lambda i: (nb - 1 - i, 0)
    tile = pl.BlockSpec((tt, W), rev)
    halo = pl.BlockSpec((8, W), lambda i: (jnp.maximum((nb - 1 - i) * (tt // 8) - 1, 0), 0))
    row = pl.BlockSpec((1, W), lambda i: (0, 0))
    wblk = pl.BlockSpec((HEADS, hd, hd), lambda i: (0, 0, 0))
    return pl.pallas_call(
        body, name="lru_scan_bwd", grid=(nb,),
        out_shape=(jax.ShapeDtypeStruct((S, W), BF16), jax.ShapeDtypeStruct((S, W), BF16),
                   jax.ShapeDtypeStruct((16, W), F32), jax.ShapeDtypeStruct((HEADS, hd, hd), F32),
                   jax.ShapeDtypeStruct((HEADS, hd, hd), F32)),
        in_specs=[tile, tile, halo, tile, tile, halo, pl.BlockSpec((CONV_W, W), lambda i: (0, 0)), row,
                  wblk, row, wblk, row, row],
        out_specs=(tile, tile, pl.BlockSpec((16, W), lambda i: (0, 0)), wblk, wblk),
        scratch_shapes=[pltpu.VMEM((tt + 8, W), F32), pltpu.VMEM((tt + 8, W), F32), pltpu.VMEM((tt + 8, W), F32),
                        pltpu.VMEM((tt + 8, W), F32), pltpu.VMEM((tt, W), F32), pltpu.VMEM((tt, W), F32),
                        pltpu.VMEM((8, W), F32), pltpu.VMEM((8, W), F32)],
        compiler_params=_cparams("arbitrary"),
    )(dp, xr0, xr0, gb, hs, hs, cw, cb, wa, ba, wx, bx, lam)


def _lru_in_bwd(dxr, dgb, dx1, x, mod_l, g_mix, G, off_wy, off_win):
    S, D = x.shape
    tm = min(TM, S)
    dq = D // NQ

    def body(dxr_ref, dgb_ref, dx1_ref, x_ref, mod_ref, g_ref, wy_ref, win_ref, dx_ref, acc_ref):
        _zero_first(acc_ref)
        dh = _dot_nt(dxr_ref[...], win_ref[...].reshape(D, D)) + _dot_nt(dgb_ref[...], wy_ref[...].reshape(D, D))
        g, sc = g_ref[...], mod_ref[1:2, :]
        _, xhat, r, n = _rms_fwd(x_ref[...], g, sc, mod_ref[0:1, :])
        dx, dsh, dsc, dg = _rms_bwd(dh, xhat, r, n, g, sc)
        acc_ref[0:1, :] += dsh
        acc_ref[1:2, :] += dsc
        acc_ref[6:7, :] += dg
        dx_ref[...] = dx1_ref[...] + dx

    tile = pl.BlockSpec((tm, D), lambda i: (i, 0))
    return pl.pallas_call(
        body, name="lru_in_bwd", grid=(S // tm,),
        out_shape=(jax.ShapeDtypeStruct((S, D), F32), jax.ShapeDtypeStruct((8, D), F32)),
        in_specs=[tile, tile, tile, tile, pl.BlockSpec((8, D), lambda i: (0, 0)), pl.BlockSpec((1, D), lambda i: (0, 0)),
                  _wspec(dq, D, off_wy), _wspec(dq, D, off_win)],
        out_specs=(tile, pl.BlockSpec((8, D), lambda i: (0, 0))),
        compiler_params=_cparams("arbitrary"),
    )(dxr, dgb, dx1, x, mod_l, g_mix, G, G)


def _pool_bwd(dx1, x, pooled, mod_l, g_mix, pw, ps):
    S, D = x.shape
    tm = min(TP, S)
    nb = S // tm
    ng = len(POOL_WINDOWS)
    gd = D // ng
    n = tm + 24

    def body(dx1_ref, dxh_ref, x_ref, pl_ref, mod_ref, gm_ref, pw_ref, ps_ref, dx_ref, acc_ref, dpw_ref, ext, b1, b2):
        i = pl.program_id(0)

        @pl.when(i == 0)
        def _():
            acc_ref[...] = jnp.zeros_like(acc_ref)
            dpw_ref[...] = jnp.zeros_like(dpw_ref)

        gt, psv = mod_ref[2:3, :], ps_ref[...]
        wvec = _window_vec(D)
        dx1v = dx1_ref[...]
        pooled = pl_ref[...]
        mixed = _heads_dot(pooled, pw_ref, gd)
        acc_ref[2:3, :] += jnp.sum(dx1v * (mixed * psv), axis=0, keepdims=True)
        dy = dx1v * gt
        acc_ref[3:4, :] += jnp.sum(dy * mixed, axis=0, keepdims=True)
        dmix = (dy * psv).astype(BF16)
        for gi in range(ng):
            sl = slice(gi * gd, (gi + 1) * gd)
            dpw_ref[gi] += _dot_tn(pooled[:, sl], dmix[:, sl])
        dpooled = _heads_dot(dmix, pw_ref, gd, nt=True)
        dmix_h = (dxh_ref[...] * gt * psv).astype(BF16)
        dpooled_h = jnp.where(i < nb - 1, _heads_dot(dmix_h, pw_ref, gd, nt=True), 0.0)
        t1 = (lax.broadcasted_iota(jnp.int32, (tm, 1), 0) + (i * tm + 1)).astype(F32)
        t1h = (lax.broadcasted_iota(jnp.int32, (16, 1), 0) + ((i + 1) * tm + 1)).astype(F32)
        zeros8 = jnp.zeros((8, D), F32)
        ext[pl.ds(0, tm), :] = dpooled / jnp.minimum(t1, wvec)
        ext[pl.ds(tm, 16), :] = dpooled_h / jnp.minimum(t1h, wvec)
        ext[pl.ds(tm + 16, 8), :] = zeros8
        b1[pl.ds(tm + 16, 8), :] = zeros8
        b2[pl.ds(tm + 16, 8), :] = zeros8
        m = n - 8
        b1[pl.ds(0, m), :] = ext[pl.ds(0, m), :] + ext[pl.ds(1, m), :]
        b2[pl.ds(0, m), gd:] = b1[pl.ds(0, m), gd:] + b1[pl.ds(2, m), gd:]
        b1[pl.ds(0, m), 2 * gd:] = b2[pl.ds(0, m), 2 * gd:] + b2[pl.ds(4, m), 2 * gd:]
        b2[pl.ds(0, m), 3 * gd:] = b1[pl.ds(0, m), 3 * gd:] + b1[pl.ds(8, m), 3 * gd:]
        wsum = jnp.concatenate([b1[pl.ds(0, tm), 0:gd], b2[pl.ds(0, tm), gd:2 * gd],
                                b1[pl.ds(0, tm), 2 * gd:3 * gd], b2[pl.ds(0, tm), 3 * gd:]], axis=1)
        dh = wsum - dpooled
        g, sc = gm_ref[...], mod_ref[1:2, :]
        _, xhat, r, nn = _rms_fwd(x_ref[...], g, sc, mod_ref[0:1, :])
        dx, dsh, dsc, dg = _rms_bwd(dh, xhat, r, nn, g, sc)
        acc_ref[0:1, :] += dsh
        acc_ref[1:2, :] += dsc
        acc_ref[6:7, :] += dg
        dx_ref[...] = dx1v + dx

    tile = pl.BlockSpec((tm, D), lambda i: (i, 0))
    halo = pl.BlockSpec((16, D), lambda i: (jnp.minimum((i + 1) * (tm // 16), S // 16 - 1), 0))
    row = pl.BlockSpec((1, D), lambda i: (0, 0))
    wblk = pl.BlockSpec((ng, gd, gd), lambda i: (0, 0, 0))
    return pl.pallas_call(
        body, name="pool_bwd", grid=(nb,),
        out_shape=(jax.ShapeDtypeStruct((S, D), F32), jax.ShapeDtypeStruct((8, D), F32),
                   jax.ShapeDtypeStruct((ng, gd, gd), F32)),
        in_specs=[tile, halo, tile, tile, pl.BlockSpec((8, D), lambda i: (0, 0)), row, wblk, row],
        out_specs=(tile, pl.BlockSpec((8, D), lambda i: (0, 0)), wblk),
        scratch_shapes=[pltpu.VMEM((n, D), F32), pltpu.VMEM((n, D), F32), pltpu.VMEM((n, D), F32)],
        compiler_params=_cparams("arbitrary"),
    )(dx1, dx1, x, pooled, mod_l, g_mix, pw, ps)


def _shard_to_rows(w, D):
    return w.reshape(-1, D)


def _blockdiag_full(gq, na, hd):
    return gq.reshape(NQ, na, HEADS, hd // NQ, hd).transpose(1, 2, 0, 3, 4).reshape(na, HEADS, hd, hd)


def _blockdiag_by_chip(dw, D):
    na, _, hd, _ = dw.shape
    return dw.reshape(na, HEADS, NQ, hd // NQ, hd).transpose(2, 0, 1, 3, 4).reshape(NQ, -1, D)


def kernel(x, c, w_mod, b_mod, norm_mix_g, norm_ffn_g, lru_w_y, lru_b_y, lru_w_in, lru_b_in, lru_conv_w, lru_conv_b, lru_w_a, lru_b_a, lru_w_x, lru_b_x, lru_lambda, lru_w_out, lru_b_out, pool_w, pool_scale, ffn_w1, ffn_w2, final_norm_g, loss_target, m_w_mod, m_b_mod, m_norm_mix_g, m_norm_ffn_g, m_lru_w_y, m_lru_b_y, m_lru_w_in, m_lru_b_in, m_lru_conv_w, m_lru_conv_b, m_lru_w_a, m_lru_b_a, m_lru_w_x, m_lru_b_x, m_lru_lambda, m_lru_w_out, m_lru_b_out, m_pool_w, m_pool_scale, m_ffn_w1, m_ffn_w2, m_final_norm_g, v_w_mod, v_b_mod, v_norm_mix_g, v_norm_ffn_g, v_lru_w_y, v_lru_b_y, v_lru_w_in, v_lru_b_in, v_lru_conv_w, v_lru_conv_b, v_lru_w_a, v_lru_b_a, v_lru_w_x, v_lru_b_x, v_lru_lambda, v_lru_w_out, v_lru_b_out, v_pool_w, v_pool_scale, v_ffn_w1, v_ffn_w2, v_final_norm_g):
    S, D = x.shape[1], x.shape[2]
    L = w_mod.shape[0]
    NA = lru_w_y.shape[0]
    NB = pool_w.shape[0]
    F = ffn_w1.shape[2] * NQ
    f4 = F // NQ
    hd = D // HEADS
    Cs = w_mod.shape[2]
    assert L == DEPTH and Cs * NQ == N_MOD * D and D % 1024 == 0
    x2d = x.reshape(S, D)
    tgt = loss_target.reshape(S, D)
    q = 2 * lax.axis_index("x") + lax.axis_index("y")

    big = [ffn_w1, ffn_w2, lru_w_y, lru_w_in, lru_w_out, lru_w_a, lru_w_x, pool_w]
    rows = [int(w.size) // D for w in big]
    offs = [sum(rows[:k]) for k in range(len(big))]
    O_W1, O_W2, O_WY, O_WIN, O_WOUT, O_WA, O_WX, O_PW = offs
    R = sum(rows)
    pack = jnp.concatenate([_shard_to_rows(w, D) for w in big], axis=0).astype(BF16)

    cshard = lru_conv_w.reshape(-1)
    small_fwd = jnp.concatenate([c.reshape(-1), cshard, lru_b_a.reshape(-1), lru_b_x.reshape(-1),
                                 pool_scale.reshape(-1)])
    small_fwd = jnp.pad(small_fwd, (0, 8 * D - small_fwd.shape[0])).reshape(8, D)

    G, SG = _gather_weights(pack, small_fwd)
    SGf = SG.reshape(NDEV, 8 * D)
    c_all = SGf[:, :D]
    SGq = SGf[0::2]
    o = D
    n_cw = NA * CONV_W * D // NQ
    conv_w_full = SGq[:, o:o + n_cw].reshape(NQ, NA, CONV_W, D // NQ).transpose(1, 2, 0, 3).reshape(NA, CONV_W, D)
    o += n_cw
    n_b = NA * HEADS * hd // NQ
    b_a_full = SGq[:, o:o + n_b].reshape(NQ, NA, HEADS, hd // NQ).transpose(1, 2, 0, 3).reshape(NA, 1, D)
    o += n_b
    b_x_full = SGq[:, o:o + n_b].reshape(NQ, NA, HEADS, hd // NQ).transpose(1, 2, 0, 3).reshape(NA, 1, D)
    o += n_b
    n_ps = NB * D // NQ
    pool_scale_full = SGq[:, o:o + n_ps].reshape(NQ, NB, D // NQ).transpose(1, 0, 2).reshape(NB, 1, D)

    wa_full = _blockdiag_full(G[:, O_WA:O_WA + rows[5]], NA, hd)
    wx_full = _blockdiag_full(G[:, O_WX:O_WX + rows[6]], NA, hd)
    pw_full = _blockdiag_full(G[:, O_PW:O_PW + rows[7]], NB, hd)

    b_mod_sh = lax.dynamic_slice_in_dim(b_mod, q * Cs, Cs, axis=1).reshape(L, 1, Cs)
    modpart = _mod_part(c_all, w_mod, b_mod_sh)
    modq = _exchange_mod(modpart.transpose(1, 0, 2))
    mod = modq.transpose(1, 0, 2).reshape(L, N_MOD, D)
    mod = jnp.pad(mod, ((0, 0), (0, 8 - N_MOD), (0, 0)))

    saved = []
    xcur = x2d
    for i in range(L):
        j = i // 2
        gm = norm_mix_g[i].reshape(1, D)
        gf = norm_ffn_g[i].reshape(1, D)
        if i % 2 == 0:
            h, gb, xr0 = _lru_in_fwd(xcur, mod[i], gm, G, O_WY + j * (D // NQ), O_WIN + j * (D // NQ),
                                     lru_b_y[j].reshape(1, D), lru_b_in[j].reshape(1, D))
            hs, p = _lru_scan_fwd(xr0, gb, conv_w_full[j], lru_conv_b[j].reshape(1, D), wa_full[j], b_a_full[j],
                                  wx_full[j], b_x_full[j], lru_lambda[j].reshape(1, D))
            y, x1, h2, a, z, x2 = _lru_mix_ffn_fwd(xcur, p, mod[i], gf, lru_b_out[j].reshape(1, D), G,
                                                   O_WOUT + j * (D // NQ), O_W1 + i * D, O_W2 + i * f4, F)
            saved.append(dict(x=xcur, h=h, gb=gb, xr0=xr0, hs=hs, p=p, y=y, x1=x1, h2=h2, a=a, z=z))
        else:
            pooled, x1, h2, a, z, x2 = _pool_mix_ffn_fwd(xcur, mod[i], gm, pw_full[j], pool_scale_full[j], gf, G,
                                                         O_W1 + i * D, O_W2 + i * f4, F)
            saved.append(dict(x=xcur, pooled=pooled, x1=x1, h2=h2, a=a, z=z))
        xcur = x2

    dx, lacc = _final_loss_bwd(xcur, final_norm_g.reshape(1, D), tgt)
    loss = lax.psum(0.5 * jnp.sum(lacc[0]) / D, ("x", "y", "c"))
    d_final_g = lacc[1]

    dwp = jnp.zeros((NQ, R, D), BF16)
    dmod_rows = [None] * L
    dg_mix = [None] * L
    dg_ffn = [None] * L
    d_small = {}
    dwa_l, dwx_l, dpw_l = [None] * NA, [None] * NA, [None] * NB
    for i in reversed(range(L)):
        j = i // 2
        sv = saved[i]
        gm = norm_mix_g[i].reshape(1, D)
        gf = norm_ffn_g[i].reshape(1, D)
        dx1, du, dz, facc = _ffn_bwd(dx, sv["x1"], sv["a"], sv["z"], mod[i], gf, G, O_W1 + i * D, O_W2 + i * f4)
        dwp = _matmul_tn(sv["h2"], du, dwp, O_W1 + i * D, 1, NQ, "n", False, "dw1")
        dwp = _matmul_tn(sv["a"], dz, dwp, O_W2 + i * f4, NQ, 1, "m", True, "dw2")
        if i % 2 == 0:
            dyp, dp, oacc = _lru_out_bwd(dx1, sv["y"], mod[i], G, O_WOUT + j * (D // NQ))
            dwp = _matmul_tn(sv["p"], dyp, dwp, O_WOUT + j * (D // NQ), NQ, 1, "m", False, "dwout")
            dgb, dxr, sm, dwa, dwx = _lru_scan_bwd(dp, sv["xr0"], sv["gb"], sv["hs"], conv_w_full[j],
                                                   lru_conv_b[j].reshape(1, D), wa_full[j], b_a_full[j], wx_full[j],
                                                   b_x_full[j], lru_lambda[j].reshape(1, D))
            dx, macc = _lru_in_bwd(dxr, dgb, dx1, sv["x"], mod[i], gm, G, O_WY + j * (D // NQ), O_WIN + j * (D // NQ))
            dwp = _matmul_tn(sv["h"], dgb, dwp, O_WY + j * (D // NQ), NQ, 1, "m", False, "dwy")
            dwp = _matmul_tn(sv["h"], dxr, dwp, O_WIN + j * (D // NQ), NQ, 1, "m", False, "dwin")
            dwa_l[j], dwx_l[j] = dwa, dwx
            d_small[("lru", j)] = (sm, oacc[3])
            dgt_m = oacc[2]
        else:
            dx, macc, dpw = _pool_bwd(dx1, sv["x"], sv["pooled"], mod[i], gm, pw_full[j], pool_scale_full[j])
            dpw_l[j] = dpw
            d_small[("pool", j)] = macc[3]
            dgt_m = macc[2]
        dmod_rows[i] = jnp.stack([macc[0], macc[1], dgt_m, facc[3], facc[4], facc[5]])
        dg_mix[i] = macc[6]
        dg_ffn[i] = facc[7]
    grad_x = dx.reshape(x.shape)

    tiny = jnp.concatenate([_blockdiag_by_chip(jnp.stack(dwa_l), D), _blockdiag_by_chip(jnp.stack(dwx_l), D),
                            _blockdiag_by_chip(jnp.stack(dpw_l), D)], axis=1).astype(BF16)
    dwp = lax.dynamic_update_slice(dwp, tiny, (0, O_WA, 0))

    lru_sm = [d_small[("lru", j)] for j in range(NA)]
    small_rows = [jnp.stack(dmod_rows).reshape(L * N_MOD, D), jnp.stack(dg_mix), jnp.stack(dg_ffn),
                  jnp.stack([s[0][9] for s in lru_sm]), jnp.stack([s[0][8] for s in lru_sm]),
                  jnp.stack([s[0][4] for s in lru_sm]), jnp.stack([s[0][7] for s in lru_sm]),
                  jnp.stack([s[1] for s in lru_sm]),
                  jnp.stack([s[0][0:CONV_W] for s in lru_sm]).reshape(NA * CONV_W, D),
                  jnp.stack([s[0][5] for s in lru_sm]), jnp.stack([s[0][6] for s in lru_sm]),
                  jnp.stack([d_small[("pool", j)] for j in range(NB)]), d_final_g.reshape(1, D)]
    small_g = jnp.concatenate(small_rows, axis=0)
    n_small = small_g.shape[0]
    assert n_small <= SMALL_ROWS
    small_g = jnp.pad(small_g, ((0, SMALL_ROWS - n_small), (0, 0)))

    rb, sg_all, sg_sum = _scatter_grads(dwp, small_g)
    own = lax.dynamic_index_in_dim(dwp, q, axis=0, keepdims=False)
    psum_mine = _sum_partials(own, rb)
    psum_sib = _sibling_exchange(psum_mine)

    def big_update(w, m, v, off, name):
        shp = w.shape
        g, dl, m2, v2 = _adam_rows(w.reshape(-1, D), m.reshape(-1, D), v.reshape(-1, D), psum_mine, psum_sib, off, name)
        return g.reshape(shp), dl.reshape(shp), m2.reshape(shp), v2.reshape(shp)

    res = {}
    res["ffn_w1"] = big_update(ffn_w1, m_ffn_w1, v_ffn_w1, O_W1, "adam_w1")
    res["ffn_w2"] = big_update(ffn_w2, m_ffn_w2, v_ffn_w2, O_W2, "adam_w2")
    res["lru_w_y"] = big_update(lru_w_y, m_lru_w_y, v_lru_w_y, O_WY, "adam_wy")
    res["lru_w_in"] = big_update(lru_w_in, m_lru_w_in, v_lru_w_in, O_WIN, "adam_win")
    res["lru_w_out"] = big_update(lru_w_out, m_lru_w_out, v_lru_w_out, O_WOUT, "adam_wout")
    res["lru_w_a"] = big_update(lru_w_a, m_lru_w_a, v_lru_w_a, O_WA, "adam_wa")
    res["lru_w_x"] = big_update(lru_w_x, m_lru_w_x, v_lru_w_x, O_WX, "adam_wx")
    res["pool_w"] = big_update(pool_w, m_pool_w, v_pool_w, O_PW, "adam_pw")

    dmod_all = sg_all[:, :L * N_MOD, :].reshape(NDEV, L, N_MOD * D)
    dmod_sh = lax.dynamic_slice_in_dim(dmod_all, q * Cs, Cs, axis=2).transpose(1, 0, 2)
    res["w_mod"] = _wmod_update(c_all.T, dmod_sh, w_mod, m_w_mod, v_w_mod)

    r0 = 0

    def take(nrows):
        nonlocal r0
        out = sg_sum[r0:r0 + nrows]
        r0 += nrows
        return out

    g_b_mod = take(L * N_MOD).reshape(L, N_MOD * D)
    g_mix = take(L)
    g_ffn = take(L)
    g_b_y = take(NA)
    g_b_in = take(NA)
    g_conv_b = take(NA)
    g_lam = take(NA)
    g_b_out = take(NA)
    g_conv_w = lax.dynamic_slice_in_dim(take(NA * CONV_W).reshape(NA, CONV_W, D), q * (D // NQ), D // NQ, axis=2)
    g_b_a = lax.dynamic_slice_in_dim(take(NA).reshape(NA, HEADS, hd), q * (hd // NQ), hd // NQ, axis=2)
    g_b_x = lax.dynamic_slice_in_dim(take(NA).reshape(NA, HEADS, hd), q * (hd // NQ), hd // NQ, axis=2)
    g_ps = lax.dynamic_slice_in_dim(take(NB), q * (D // NQ), D // NQ, axis=1)
    g_fin = take(1).reshape(D)

    smalls = [("b_mod", b_mod, m_b_mod, v_b_mod, g_b_mod), ("norm_mix_g", norm_mix_g, m_norm_mix_g, v_norm_mix_g, g_mix),
              ("norm_ffn_g", norm_ffn_g, m_norm_ffn_g, v_norm_ffn_g, g_ffn),
              ("lru_b_y", lru_b_y, m_lru_b_y, v_lru_b_y, g_b_y), ("lru_b_in", lru_b_in, m_lru_b_in, v_lru_b_in, g_b_in),
              ("lru_conv_w", lru_conv_w, m_lru_conv_w, v_lru_conv_w, g_conv_w),
              ("lru_conv_b", lru_conv_b, m_lru_conv_b, v_lru_conv_b, g_conv_b),
              ("lru_b_a", lru_b_a, m_lru_b_a, v_lru_b_a, g_b_a), ("lru_b_x", lru_b_x, m_lru_b_x, v_lru_b_x, g_b_x),
              ("lru_lambda", lru_lambda, m_lru_lambda, v_lru_lambda, g_lam),
              ("lru_b_out", lru_b_out, m_lru_b_out, v_lru_b_out, g_b_out),
              ("pool_scale", pool_scale, m_pool_scale, v_pool_scale, g_ps),
              ("final_norm_g", final_norm_g, m_final_norm_g, v_final_norm_g, g_fin)]
    total = sum(int(s[1].size) for s in smalls)
    prow = -(-total // (8 * 128)) * 8

    def flat(k):
        f = jnp.concatenate([s[k].reshape(-1) for s in smalls])
        return jnp.pad(f, (0, prow * 128 - total)).reshape(prow, 128)

    dl_s, m_s, v_s = _adam_small(flat(1), flat(4), flat(2), flat(3))
    o = 0
    for name, w, _, _, g in smalls:
        sz = int(w.size)
        cut = lambda arr: arr.reshape(-1)[o:o + sz].reshape(w.shape)
        res[name] = (g.reshape(w.shape), cut(dl_s), cut(m_s), cut(v_s))
        o += sz

    order = ["w_mod", "b_mod", "norm_mix_g", "norm_ffn_g", "lru_w_y", "lru_b_y", "lru_w_in", "lru_b_in", "lru_conv_w",
             "lru_conv_b", "lru_w_a", "lru_b_a", "lru_w_x", "lru_b_x", "lru_lambda", "lru_w_out", "lru_b_out", "pool_w",
             "pool_scale", "ffn_w1", "ffn_w2", "final_norm_g"]
    return (loss, grad_x, *[res[n][0] for n in order], *[res[n][1] for n in order],
            *[res[n][2] for n in order], *[res[n][3] for n in order])
```

```python
import functools

import jax
import jax.numpy as jnp
from jax import lax
from jax.experimental import pallas as pl
from jax.experimental.pallas import tpu as pltpu

F32 = jnp.float32
BF16 = jnp.bfloat16
MESH = pl.DeviceIdType.MESH

NQ = 4
NDEV = 8
DEPTH = 4
N_MOD = 6
HEADS = 4
CONV_W = 4
POOL_WINDOWS = (2, 4, 8, 16)
LRU_C = 8.0
EPS = 1e-6
ADAM_LR, ADAM_B1, ADAM_B2, ADAM_EPS, ADAM_WD, ADAM_STEP = 0.001, 0.9, 0.999, 1e-08, 0.01, 10

TM = 512
TT = 256
TP = 256
TK = 512
SMALL_ROWS = 64
VMEM_LIMIT = 60 * 1024 * 1024


def _cparams(*sem):
    return pltpu.CompilerParams(dimension_semantics=tuple(sem), vmem_limit_bytes=VMEM_LIMIT)


def _dot(a, b):
    return jnp.dot(a, b, preferred_element_type=F32)


def _dot_nt(a, b):
    return lax.dot_general(a, b, (((1,), (1,)), ((), ())), preferred_element_type=F32)


def _dot_tn(a, b):
    return lax.dot_general(a, b, (((0,), (0,)), ((), ())), preferred_element_type=F32)


def _resident(shape, index_map):
    return pl.BlockSpec(shape, index_map, pipeline_mode=pl.Buffered(1))


def _rms_fwd(x, g, sc, sh):
    r = lax.rsqrt(jnp.mean(x * x, axis=-1, keepdims=True) + EPS)
    xhat = x * r
    n = xhat * g
    return n * (1.0 + sc) + sh, xhat, r, n


def _rms_bwd(dh, xhat, r, n, g, sc):
    dsh = jnp.sum(dh, axis=0, keepdims=True)
    dsc = jnp.sum(dh * n, axis=0, keepdims=True)
    dn = dh * (1.0 + sc)
    dg = jnp.sum(dn * xhat, axis=0, keepdims=True)
    dxh = dn * g
    dx = r * (dxh - xhat * jnp.mean(dxh * xhat, axis=-1, keepdims=True))
    return dx, dsh, dsc, dg


_GELU_K = 0.7978845608028654
_GELU_C = 0.044715


def _gelu(x):
    t = jnp.tanh(_GELU_K * (x + _GELU_C * x * x * x))
    return 0.5 * x * (1.0 + t), t


def _gelu_grad(x, t):
    return 0.5 * (1.0 + t) + 0.5 * x * (1.0 - t * t) * (_GELU_K * (1.0 + 3.0 * _GELU_C * x * x))


def _neg_expm1(y):
    series = -(y * (1.0 + y * (0.5 + y * (1.0 / 6.0 + y * (1.0 / 24.0 + y * (1.0 / 120.0))))))
    return jnp.where(y > -0.25, series, 1.0 - jnp.exp(y))


def _zero_first(ref):
    @pl.when(pl.program_id(0) == 0)
    def _():
        ref[...] = jnp.zeros_like(ref)


def _my_pos():
    return lax.axis_index("x"), lax.axis_index("y"), lax.axis_index("c")


def _dev_index(x, y, c):
    return 4 * x + 2 * y + c


def _chip_peers(x, y):
    return [(1 - x, y), (x, 1 - y), (1 - x, 1 - y)]


def _all_peers(x, y, c):
    return [(px, py, c) for (px, py) in _chip_peers(x, y)] + [(x, y, 1 - c)] + \
           [(px, py, 1 - c) for (px, py) in _chip_peers(x, y)]


def _comm_copies(x, y, c, gathers, scatters, sib, send, recv, loc):
    q = 2 * x + y
    peers = _chip_peers(x, y)
    remote, local = [], []
    s = 0
    for gi, (src, dst) in enumerate(gathers):
        local.append(pltpu.make_async_copy(src, dst.at[q], loc.at[gi]))
        for (px, py) in peers:
            st = pltpu.make_async_remote_copy(src, dst.at[q], send.at[s], recv.at[s],
                                              device_id=(px, py, c), device_id_type=MESH)
            wt = pltpu.make_async_remote_copy(src, dst.at[2 * px + py], send.at[s], recv.at[s],
                                              device_id=(px, py, c), device_id_type=MESH)
            remote.append((st, wt))
            s += 1
    for (src, dst) in scatters:
        for k, (px, py) in enumerate(peers):
            st = pltpu.make_async_remote_copy(src.at[2 * px + py], dst.at[k], send.at[s], recv.at[s],
                                              device_id=(px, py, c), device_id_type=MESH)
            remote.append((st, st))
            s += 1
    if sib is not None:
        src, dst, ranges = sib
        for (off, rows) in ranges:
            st = pltpu.make_async_remote_copy(src.at[pl.ds(off, rows)], dst.at[pl.ds(off, rows)], send.at[s], recv.at[s],
                                              device_id=(x, y, 1 - c), device_id_type=MESH)
            remote.append((st, st))
            s += 1
    return remote, local


def _comm_start(remote, local):
    for cp in local:
        cp.start()
    for st, _ in remote:
        st.start()


def _comm_wait(remote, local):
    for _, wt in remote:
        wt.wait_recv()
    for st, _ in remote:
        st.wait_send()
    for cp in local:
        cp.wait()


def _comm_shapes(gathers, scatters, sib):
    cin = list(gathers) + list(scatters) + ([sib[0], sib[1]] if sib else [])
    cout = [jax.ShapeDtypeStruct((NQ,) + g.shape, g.dtype) for g in gathers] + \
           [jax.ShapeDtypeStruct((3,) + s.shape[1:], s.dtype) for s in scatters] + \
           ([jax.ShapeDtypeStruct(sib[1].shape, sib[1].dtype)] if sib else [])
    n_rem = 3 * len(gathers) + 3 * len(scatters) + (len(sib[2]) if sib else 0)
    sems = [pltpu.SemaphoreType.DMA((max(n_rem, 1),)), pltpu.SemaphoreType.DMA((max(n_rem, 1),)),
            pltpu.SemaphoreType.DMA((max(len(gathers), 1),))]
    return cin, cout, sems


def _pcall(body, *, name, grid, in_specs, out_specs, out_shape, operands, scratch_shapes=(),
           gathers=(), scatters=(), sib=None):
    assert len(grid) == 1
    out_shape, out_specs = tuple(out_shape), tuple(out_specs)
    if not (gathers or scatters or sib):
        return pl.pallas_call(body, name=name, grid=grid, in_specs=list(in_specs), out_specs=out_specs,
                              out_shape=out_shape, scratch_shapes=list(scratch_shapes),
                              compiler_params=_cparams("arbitrary"))(*operands)
    cin, cout, sems = _comm_shapes(gathers, scatters, sib)
    n_in, n_cin, n_out, n_cout, n_scr = len(operands), len(cin), len(out_shape), len(cout), len(scratch_shapes)
    ng, ns = len(gathers), len(scatters)
    nsteps = grid[0]

    def wrapped(*refs):
        ins = refs[:n_in]
        cins = refs[n_in:n_in + n_cin]
        o0 = n_in + n_cin
        outs = refs[o0:o0 + n_out]
        couts = refs[o0 + n_out:o0 + n_out + n_cout]
        s0 = o0 + n_out + n_cout
        scr = refs[s0:s0 + n_scr]
        send, recv, loc = refs[s0 + n_scr:s0 + n_scr + 3]
        x, y, c = _my_pos()

        def copies():
            g = [(cins[k], couts[k]) for k in range(ng)]
            sc = [(cins[ng + k], couts[ng + k]) for k in range(ns)]
            sb = (cins[ng + ns], couts[ng + ns], sib[2]) if sib else None
            return _comm_copies(x, y, c, g, sc, sb, send, recv, loc)

        @pl.when(pl.program_id(0) == 0)
        def _():
            _comm_start(*copies())

        body(*ins, *outs, *scr)

        @pl.when(pl.program_id(0) == nsteps - 1)
        def _():
            _comm_wait(*copies())

    anyspec = pl.BlockSpec(memory_space=pl.ANY)
    aliases = {n_in + ng + ns + 1: n_out + ng + ns} if sib else {}
    return pl.pallas_call(
        wrapped, name=name, grid=grid,
        in_specs=list(in_specs) + [anyspec] * n_cin, out_specs=out_specs + (anyspec,) * n_cout,
        out_shape=out_shape + tuple(cout), scratch_shapes=list(scratch_shapes) + sems,
        input_output_aliases=aliases,
        compiler_params=pltpu.CompilerParams(dimension_semantics=("arbitrary",), vmem_limit_bytes=VMEM_LIMIT,
                                             has_side_effects=True),
    )(*operands, *cin)


def _comm_only(name, small=None, reduce_small=False, gathers=(), scatters=(), sib=None):
    cin, cout, sems = _comm_shapes(gathers, scatters, sib)
    n_cin, n_cout = len(cin), len(cout)
    ng, ns = len(gathers), len(scatters)
    n_sm_in = 1 if small is not None else 0
    n_sm_out = (2 if reduce_small else 1) if small is not None else 0

    def body(*refs):
        sm_in = refs[:n_sm_in]
        cins = refs[n_sm_in:n_sm_in + n_cin]
        o0 = n_sm_in + n_cin
        sm_out = refs[o0:o0 + n_sm_out]
        couts = refs[o0 + n_sm_out:o0 + n_sm_out + n_cout]
        s0 = o0 + n_sm_out + n_cout
        send, recv, loc = refs[s0:s0 + 3]
        x, y, c = _my_pos()
        g = [(cins[k], couts[k]) for k in range(ng)]
        sc = [(cins[ng + k], couts[ng + k]) for k in range(ns)]
        sb = (cins[ng + ns], couts[ng + ns], sib[2]) if sib else None
        remote, local = _comm_copies(x, y, c, g, sc, sb, send, recv, loc)
        _comm_start(remote, local)
        if small is not None:
            sm_send, sm_recv = refs[s0 + 3:s0 + 5]
            small_ref, sg_ref = sm_in[0], sm_out[0]
            me = _dev_index(x, y, c)
            sg_ref[me] = small_ref[...]
            peers = _all_peers(x, y, c)
            sm = [pltpu.make_async_remote_copy(small_ref, sg_ref.at[me], sm_send.at[k], sm_recv.at[k],
                                               device_id=peer, device_id_type=MESH) for k, peer in enumerate(peers)]
            for cp in sm:
                cp.start()
            for k, (px, py, pc) in enumerate(peers):
                pltpu.make_async_remote_copy(small_ref, sg_ref.at[_dev_index(px, py, pc)], sm_send.at[k], sm_recv.at[k],
                                             device_id=(px, py, pc), device_id_type=MESH).wait_recv()
            if reduce_small:
                acc = sg_ref[0]
                for d in range(1, NDEV):
                    acc = acc + sg_ref[d]
                sm_out[1][...] = acc
            for cp in sm:
                cp.wait_send()
        _comm_wait(remote, local)

    anyspec = pl.BlockSpec(memory_space=pl.ANY)
    vspec = pl.BlockSpec(memory_space=pltpu.VMEM)
    sm_shapes = []
    if small is not None:
        sm_shapes.append(jax.ShapeDtypeStruct((NDEV,) + small.shape, small.dtype))
        if reduce_small:
            sm_shapes.append(jax.ShapeDtypeStruct(small.shape, small.dtype))
        sems = sems + [pltpu.SemaphoreType.DMA((NDEV - 1,)), pltpu.SemaphoreType.DMA((NDEV - 1,))]
    aliases = {n_sm_in + ng + ns + 1: n_sm_out + ng + ns} if sib else {}
    return pl.pallas_call(
        body, name=name,
        in_specs=[vspec] * n_sm_in + [anyspec] * n_cin,
        out_specs=tuple([vspec] * n_sm_out + [anyspec] * n_cout),
        out_shape=tuple(sm_shapes + cout), scratch_shapes=sems, input_output_aliases=aliases,
        compiler_params=pltpu.CompilerParams(has_side_effects=True),
    )(*([small] if small is not None else []), *cin)


def _exchange_mod(modpart):
    _, L, Cs = modpart.shape

    def body(part_ref, out_ref, send, recv):
        x, y, c = _my_pos()
        q = 2 * x + y
        me = _dev_index(x, y, c)
        out_ref[q] = part_ref[me]
        sends = []
        for k, (px, py) in enumerate(_chip_peers(x, y)):
            cp = pltpu.make_async_remote_copy(part_ref.at[_dev_index(px, py, c)], out_ref.at[q], send.at[k], recv.at[k],
                                              device_id=(px, py, c), device_id_type=MESH)
            cp.start()
            sends.append(cp)
        for k, (px, py) in enumerate(_chip_peers(x, y)):
            pltpu.make_async_remote_copy(part_ref.at[me], out_ref.at[2 * px + py], send.at[k], recv.at[k],
                                         device_id=(px, py, c), device_id_type=MESH).wait_recv()
        for cp in sends:
            cp.wait_send()

    return pl.pallas_call(
        body, name="exchange_mod",
        out_shape=jax.ShapeDtypeStruct((NQ, L, Cs), modpart.dtype),
        in_specs=[pl.BlockSpec(memory_space=pltpu.VMEM)],
        out_specs=pl.BlockSpec(memory_space=pltpu.VMEM),
        scratch_shapes=[pltpu.SemaphoreType.DMA((3,)), pltpu.SemaphoreType.DMA((3,))],
        compiler_params=pltpu.CompilerParams(has_side_effects=True),
    )(modpart)


def _mod_part(c_all, w_mod, b_mod_sh):
    L, D, Cs = w_mod.shape
    tn = 512 if Cs % 512 == 0 else Cs

    def body(c_ref, w_ref, b_ref, o_ref):
        cv = c_ref[...]
        cond = cv * jax.nn.sigmoid(cv)
        o_ref[...] = jnp.dot(cond, w_ref[...], preferred_element_type=F32, precision=lax.Precision.HIGHEST) + b_ref[...]

    return pl.pallas_call(
        body, name="mod_part", grid=(L, Cs // tn),
        out_shape=jax.ShapeDtypeStruct((L, NDEV, Cs), F32),
        in_specs=[pl.BlockSpec((NDEV, D), lambda i, j: (0, 0)),
                  pl.BlockSpec((None, D, tn), lambda i, j: (i, 0, j)),
                  pl.BlockSpec((None, 1, tn), lambda i, j: (i, 0, j))],
        out_specs=pl.BlockSpec((None, NDEV, tn), lambda i, j: (i, 0, j)),
        compiler_params=_cparams("parallel", "parallel"),
    )(c_all, w_mod, b_mod_sh)


def _adam(w, g, m, v):
    m2 = ADAM_B1 * m + (1.0 - ADAM_B1) * g
    v2 = ADAM_B2 * v + (1.0 - ADAM_B2) * (g * g)
    m_hat = m2 / (1.0 - ADAM_B1 ** ADAM_STEP)
    v_hat = v2 / (1.0 - ADAM_B2 ** ADAM_STEP)
    delta = -ADAM_LR * (m_hat / (jnp.sqrt(v_hat) + ADAM_EPS) + ADAM_WD * w)
    return delta, m2, v2


def _wmod_update(c_all_t, dmod_sh, w, m, v):
    L, D, Cs = w.shape
    td = 256 if D % 256 == 0 else D

    def body(ct_ref, d_ref, w_ref, m_ref, v_ref, g_ref, dl_ref, m2_ref, v2_ref):
        cv = ct_ref[...]
        cond = cv * jax.nn.sigmoid(cv)
        g = cond[:, 0:1] * d_ref[0:1, :]
        for b in range(1, NDEV):
            g = g + cond[:, b:b + 1] * d_ref[b:b + 1, :]
        g_ref[...] = g
        dl_ref[...], m2_ref[...], v2_ref[...] = _adam(w_ref[...], g, m_ref[...], v_ref[...])

    blk = pl.BlockSpec((None, td, Cs), lambda i, j: (i, j, 0))
    out = jax.ShapeDtypeStruct((L, D, Cs), F32)
    return pl.pallas_call(
        body, name="wmod_update", grid=(L, D // td),
        out_shape=(out, out, out, out),
        in_specs=[pl.BlockSpec((td, NDEV), lambda i, j: (j, 0)),
                  pl.BlockSpec((None, NDEV, Cs), lambda i, j: (i, 0, 0)), blk, blk, blk],
        out_specs=(blk, blk, blk, blk),
        compiler_params=_cparams("parallel", "parallel"),
    )(c_all_t, dmod_sh, w, m, v)


def _adam_rows(w, m, v, pa, pb, row_off, name):
    rows, C = w.shape
    tr = 512 if rows % 512 == 0 else (128 if rows % 128 == 0 else rows)
    assert row_off % tr == 0
    ob = row_off // tr

    def body(w_ref, m_ref, v_ref, pa_ref, pb_ref, g_ref, dl_ref, m2_ref, v2_ref):
        g = pa_ref[...] + pb_ref[...]
        g_ref[...] = g
        dl_ref[...], m2_ref[...], v2_ref[...] = _adam(w_ref[...], g, m_ref[...], v_ref[...])

    blk = pl.BlockSpec((tr, C), lambda i: (i, 0))
    pblk = pl.BlockSpec((tr, C), lambda i: (ob + i, 0))
    out = jax.ShapeDtypeStruct((rows, C), F32)
    return pl.pallas_call(
        body, name=name, grid=(rows // tr,), out_shape=(out, out, out, out),
        in_specs=[blk, blk, blk, pblk, pblk], out_specs=(blk, blk, blk, blk),
        compiler_params=_cparams("parallel"),
    )(w, m, v, pa, pb)


def _adam_small(w, g, m, v):
    def body(w_ref, g_ref, m_ref, v_ref, dl_ref, m2_ref, v2_ref):
        dl_ref[...], m2_ref[...], v2_ref[...] = _adam(w_ref[...], g_ref[...], m_ref[...], v_ref[...])

    out = jax.ShapeDtypeStruct(w.shape, F32)
    return pl.pallas_call(body, name="adam_small", out_shape=(out, out, out))(w, g, m, v)


def _sum_into(ppack, own, rb, off):
    rows, D = own.shape
    tr = 256 if rows % 256 == 0 else 128
    assert rows % tr == 0 and off % tr == 0
    ob = off // tr

    def body(o_ref, r_ref, pin_ref, p_ref):
        acc = o_ref[...].astype(F32)
        for k in range(3):
            acc = acc + r_ref[k].astype(F32)
        p_ref[...] = acc

    return pl.pallas_call(
        body, name="sum_partials", grid=(rows // tr,), out_shape=jax.ShapeDtypeStruct(ppack.shape, ppack.dtype),
        in_specs=[pl.BlockSpec((tr, D), lambda i: (i, 0)), pl.BlockSpec((3, tr, D), lambda i: (0, i, 0)),
                  pl.BlockSpec(memory_space=pl.ANY)],
        out_specs=pl.BlockSpec((tr, D), lambda i: (ob + i, 0)),
        input_output_aliases={2: 0},
        compiler_params=_cparams("parallel"),
    )(own, rb, ppack)


def _wspec(g):
    return _resident(g.shape, lambda i: (0, 0, 0))


def _ffn_fwd_inner(x1, mod_ref, gf_ref, w1_ref, w2_ref, h2_ref, a_ref, z_ref, x2_ref):
    h2 = _rms_fwd(x1, gf_ref[...], mod_ref[4:5, :], mod_ref[3:4, :])[0]
    h2b = h2.astype(BF16)
    h2_ref[...] = h2b
    f4 = w1_ref.shape[2]
    z = jnp.zeros(x1.shape, F32)
    for q in range(NQ):
        a = jnp.maximum(_dot(h2b, w1_ref[q]), 0.0)
        a_ref[:, q * f4:(q + 1) * f4] = a.astype(BF16)
        z = z + _dot((a * a).astype(BF16), w2_ref[q])
    z_ref[...] = z.astype(BF16)
    x2_ref[...] = x1 + mod_ref[5:6, :] * z


def _lru_in_fwd(x, mod_l, g_mix, g_wy, g_win, b_y, b_in, **comm):
    S, D = x.shape
    tm = min(TM, S)

    def body(x_ref, mod_ref, g_ref, wy_ref, win_ref, by_ref, bin_ref, h_ref, gb_ref, xr_ref):
        h = _rms_fwd(x_ref[...], g_ref[...], mod_ref[1:2, :], mod_ref[0:1, :])[0]
        hb = h.astype(BF16)
        h_ref[...] = hb
        gb_ref[...] = _dot(hb, wy_ref[...].reshape(D, D)) + by_ref[...]
        xr_ref[...] = _dot(hb, win_ref[...].reshape(D, D)) + bin_ref[...]

    tile = pl.BlockSpec((tm, D), lambda i: (i, 0))
    row = pl.BlockSpec((1, D), lambda i: (0, 0))
    return _pcall(
        body, name="lru_in_fwd", grid=(S // tm,),
        out_shape=(jax.ShapeDtypeStruct((S, D), BF16), jax.ShapeDtypeStruct((S, D), F32), jax.ShapeDtypeStruct((S, D), F32)),
        in_specs=[tile, pl.BlockSpec((8, D), lambda i: (0, 0)), row, _wspec(g_wy), _wspec(g_win), row, row],
        out_specs=(tile, tile, tile),
        operands=(x, mod_l, g_mix, g_wy, g_win, b_y, b_in), **comm)


def _heads_dot(xb, w_ref, hd, nt=False):
    outs = []
    for h in range(HEADS):
        xs = xb[:, h * hd:(h + 1) * hd]
        outs.append(_dot_nt(xs, w_ref[h]) if nt else _dot(xs, w_ref[h]))
    return jnp.concatenate(outs, axis=1)


def _lru_gates(xc, wa_ref, ba, wx_ref, bx, lam, hd):
    xcb = xc.astype(BF16)
    gate_r = jax.nn.sigmoid(_heads_dot(xcb, wa_ref, hd) + ba)
    gate_i = jax.nn.sigmoid(_heads_dot(xcb, wx_ref, hd) + bx)
    ls = jax.nn.log_sigmoid(lam)
    log_a = LRU_C * gate_r * ls
    a = jnp.exp(log_a)
    mult = jnp.sqrt(_neg_expm1(2.0 * log_a))
    return xcb, gate_r, gate_i, ls, a, mult


def _conv_taps(xext, cw, tt):
    acc = cw[0:1, :] * xext[pl.ds(8 - (CONV_W - 1), tt), :]
    for k in range(1, CONV_W):
        acc = acc + cw[k:k + 1, :] * xext[pl.ds(8 - (CONV_W - 1) + k, tt), :]
    return acc


def _lru_scan_fwd(xr0, gb, cw, cb, wa, ba, wx, bx, lam, **comm):
    S, W = xr0.shape
    tt = min(TT, S)
    hd = W // HEADS

    def body(xr_ref, xrh_ref, gb_ref, cw_ref, cb_ref, wa_ref, ba_ref, wx_ref, bx_ref, lam_ref,
             hs_ref, p_ref, xext, a_s, u_s, carry):
        i = pl.program_id(0)

        @pl.when(i == 0)
        def _():
            carry[...] = jnp.zeros_like(carry)

        xext[0:8, :] = jnp.where(i > 0, xrh_ref[...], 0.0)
        xext[pl.ds(8, tt), :] = xr_ref[...]
        xc = _conv_taps(xext, cw_ref[...], tt) + cb_ref[...]
        _, _, gate_i, _, a, mult = _lru_gates(xc, wa_ref, ba_ref[...], wx_ref, bx_ref[...], lam_ref[...], hd)
        a_s[...] = a
        u_s[...] = mult * (gate_i * xc)
        row = lax.broadcasted_iota(jnp.int32, (8, W), 0)

        def step(k, _):
            off = pl.multiple_of(k * 8, 8)
            A = a_s[pl.ds(off, 8), :]
            U = u_s[pl.ds(off, 8), :]
            for d in (1, 2, 4):
                keep = row >= d
                Us = jnp.where(keep, pltpu.roll(U, d, 0), 0.0)
                As = jnp.where(keep, pltpu.roll(A, d, 0), 1.0)
                U = U + A * Us
                A = A * As
            H = U + A * carry[...]
            hs_ref[pl.ds(off, 8), :] = H
            carry[...] = jnp.broadcast_to(H[7:8, :], (8, W))
            return 0

        lax.fori_loop(0, tt // 8, step, 0)
        p_ref[...] = (hs_ref[...] * _gelu(gb_ref[...])[0]).astype(BF16)

    tile = pl.BlockSpec((tt, W), lambda i: (i, 0))
    halo = pl.BlockSpec((8, W), lambda i: (jnp.maximum(i * (tt // 8) - 1, 0), 0))
    row = pl.BlockSpec((1, W), lambda i: (0, 0))
    wblk = pl.BlockSpec((HEADS, hd, hd), lambda i: (0, 0, 0))
    return _pcall(
        body, name="lru_scan_fwd", grid=(S // tt,),
        out_shape=(jax.ShapeDtypeStruct((S, W), F32), jax.ShapeDtypeStruct((S, W), BF16)),
        in_specs=[tile, halo, tile, pl.BlockSpec((CONV_W, W), lambda i: (0, 0)), row, wblk, row, wblk, row, row],
        out_specs=(tile, tile),
        scratch_shapes=[pltpu.VMEM((tt + 8, W), F32), pltpu.VMEM((tt, W), F32), pltpu.VMEM((tt, W), F32),
                        pltpu.VMEM((8, W), F32)],
        operands=(xr0, xr0, gb, cw, cb, wa, ba, wx, bx, lam), **comm)


def _ffn_out_shapes(S, D, F):
    return (jax.ShapeDtypeStruct((S, D), F32), jax.ShapeDtypeStruct((S, D), BF16), jax.ShapeDtypeStruct((S, F), BF16),
            jax.ShapeDtypeStruct((S, D), BF16), jax.ShapeDtypeStruct((S, D), F32))


def _lru_mix_ffn_fwd(x, p, mod_l, g_ffn, b_out, g_wout, g_w1, g_w2, **comm):
    S, D = x.shape
    tm = min(TM, S)
    F = g_w1.shape[2] * NQ

    def body(x_ref, p_ref, mod_ref, gf_ref, bo_ref, wo_ref, w1_ref, w2_ref,
             y_ref, x1_ref, h2_ref, a_ref, z_ref, x2_ref):
        y = _dot(p_ref[...], wo_ref[...].reshape(D, D)) + bo_ref[...]
        y_ref[...] = y.astype(BF16)
        x1 = x_ref[...] + mod_ref[2:3, :] * y
        x1_ref[...] = x1
        _ffn_fwd_inner(x1, mod_ref, gf_ref, w1_ref, w2_ref, h2_ref, a_ref, z_ref, x2_ref)

    tile = pl.BlockSpec((tm, D), lambda i: (i, 0))
    row = pl.BlockSpec((1, D), lambda i: (0, 0))
    return _pcall(
        body, name="lru_mix_ffn_fwd", grid=(S // tm,),
        out_shape=(jax.ShapeDtypeStruct((S, D), BF16),) + _ffn_out_shapes(S, D, F),
        in_specs=[tile, tile, pl.BlockSpec((8, D), lambda i: (0, 0)), row, row,
                  _wspec(g_wout), _wspec(g_w1), _wspec(g_w2)],
        out_specs=(tile, tile, tile, pl.BlockSpec((tm, F), lambda i: (i, 0)), tile, tile),
        operands=(x, p, mod_l, g_ffn, b_out, g_wout, g_w1, g_w2), **comm)


def _window_vec(D):
    gd = D // len(POOL_WINDOWS)
    lane = lax.broadcasted_iota(jnp.int32, (1, D), 1)
    w = jnp.full((1, D), float(POOL_WINDOWS[0]), F32)
    for g in range(1, len(POOL_WINDOWS)):
        w = jnp.where(lane >= g * gd, float(POOL_WINDOWS[g]), w)
    return w


def _pool_mix_ffn_fwd(x, mod_l, g_mix, pw, ps, g_ffn, g_w1, g_w2, **comm):
    S, D = x.shape
    tm = min(TP, S)
    F = g_w1.shape[2] * NQ
    gd = D // len(POOL_WINDOWS)
    n = tm + 24

    def body(x_ref, xh_ref, mod_ref, gm_ref, pw_ref, ps_ref, gf_ref, w1_ref, w2_ref,
             pl_ref, x1_ref, h2_ref, a_ref, z_ref, x2_ref, ext, b1, b2):
        i = pl.program_id(0)
        g, sc, sh = gm_ref[...], mod_ref[1:2, :], mod_ref[0:1, :]
        xv = x_ref[...]
        h = _rms_fwd(xv, g, sc, sh)[0]
        hh = _rms_fwd(xh_ref[...], g, sc, sh)[0]
        zeros8 = jnp.zeros((8, D), F32)
        ext[0:8, :] = zeros8
        b1[0:8, :] = zeros8
        b2[0:8, :] = zeros8
        ext[8:24, :] = jnp.where(i > 0, hh, 0.0)
        ext[pl.ds(24, tm), :] = h
        m = n - 8
        b1[pl.ds(8, m), :] = ext[pl.ds(8, m), :] + ext[pl.ds(7, m), :]
        b2[pl.ds(8, m), gd:] = b1[pl.ds(8, m), gd:] + b1[pl.ds(6, m), gd:]
        b1[pl.ds(8, m), 2 * gd:] = b2[pl.ds(8, m), 2 * gd:] + b2[pl.ds(4, m), 2 * gd:]
        b2[pl.ds(8, m), 3 * gd:] = b1[pl.ds(8, m), 3 * gd:] + b1[pl.ds(0, m), 3 * gd:]
        wsum = jnp.concatenate([b1[pl.ds(24, tm), 0:gd], b2[pl.ds(24, tm), gd:2 * gd],
                                b1[pl.ds(24, tm), 2 * gd:3 * gd], b2[pl.ds(24, tm), 3 * gd:]], axis=1)
        t1 = (lax.broadcasted_iota(jnp.int32, (tm, 1), 0) + (i * tm + 1)).astype(F32)
        cnt = jnp.minimum(t1, _window_vec(D))
        pooled = (wsum / cnt - h).astype(BF16)
        pl_ref[...] = pooled
        y = _heads_dot(pooled, pw_ref, gd) * ps_ref[...]
        x1 = xv + mod_ref[2:3, :] * y
        x1_ref[...] = x1
        _ffn_fwd_inner(x1, mod_ref, gf_ref, w1_ref, w2_ref, h2_ref, a_ref, z_ref, x2_ref)

    tile = pl.BlockSpec((tm, D), lambda i: (i, 0))
    halo = pl.BlockSpec((16, D), lambda i: (jnp.maximum(i * (tm // 16) - 1, 0), 0))
    row = pl.BlockSpec((1, D), lambda i: (0, 0))
    return _pcall(
        body, name="pool_mix_ffn_fwd", grid=(S // tm,),
        out_shape=(jax.ShapeDtypeStruct((S, D), BF16),) + _ffn_out_shapes(S, D, F),
        in_specs=[tile, halo, pl.BlockSpec((8, D), lambda i: (0, 0)), row,
                  pl.BlockSpec((len(POOL_WINDOWS), gd, gd), lambda i: (0, 0, 0)), row, row,
                  _wspec(g_w1), _wspec(g_w2)],
        out_specs=(tile, tile, tile, pl.BlockSpec((tm, F), lambda i: (i, 0)), tile, tile),
        scratch_shapes=[pltpu.VMEM((n, D), F32), pltpu.VMEM((n, D), F32), pltpu.VMEM((n, D), F32)],
        operands=(x, x, mod_l, g_mix, pw, ps, g_ffn, g_w1, g_w2), **comm)


def _final_loss_bwd(x, g, target):
    S, D = x.shape
    tm = min(TM, S)

    def body(x_ref, g_ref, t_ref, dx_ref, acc_ref):
        _zero_first(acc_ref)
        xv = x_ref[...]
        gv = g_ref[...]
        r = lax.rsqrt(jnp.mean(xv * xv, axis=-1, keepdims=True) + EPS)
        xhat = xv * r
        err = xhat * gv - t_ref[...]
        acc_ref[0:1, :] += jnp.sum(err * err, axis=0, keepdims=True)
        dy = err * (1.0 / D)
        acc_ref[1:2, :] += jnp.sum(dy * xhat, axis=0, keepdims=True)
        dxh = dy * gv
        dx_ref[...] = r * (dxh - xhat * jnp.mean(dxh * xhat, axis=-1, keepdims=True))

    tile = pl.BlockSpec((tm, D), lambda i: (i, 0))
    return pl.pallas_call(
        body, name="final_loss_bwd", grid=(S // tm,),
        out_shape=(jax.ShapeDtypeStruct((S, D), F32), jax.ShapeDtypeStruct((8, D), F32)),
        in_specs=[tile, pl.BlockSpec((1, D), lambda i: (0, 0)), tile],
        out_specs=(tile, pl.BlockSpec((8, D), lambda i: (0, 0))),
        compiler_params=_cparams("arbitrary"),
    )(x, g, target)


def _ffn_bwd(dx2, x1, a, z, mod_l, g_ffn, g_w1, g_w2, **comm):
    S, D = dx2.shape
    F = a.shape[1]
    f4 = F // NQ
    tm = min(TM, S)

    def body(dx2_ref, x1_ref, a_ref, z_ref, mod_ref, gf_ref, w1_ref, w2_ref, dx1_ref, du_ref, dz_ref, acc_ref):
        _zero_first(acc_ref)
        dx2v = dx2_ref[...]
        acc_ref[5:6, :] += jnp.sum(dx2v * z_ref[...].astype(F32), axis=0, keepdims=True)
        dzb = (dx2v * mod_ref[5:6, :]).astype(BF16)
        dz_ref[...] = dzb
        dh2 = jnp.zeros((tm, D), F32)
        for q in range(NQ):
            av = a_ref[:, q * f4:(q + 1) * f4].astype(F32)
            du = (_dot_nt(dzb, w2_ref[q]) * (2.0 * av)).astype(BF16)
            du_ref[:, q * f4:(q + 1) * f4] = du
            dh2 = dh2 + _dot_nt(du, w1_ref[q])
        g, sc = gf_ref[...], mod_ref[4:5, :]
        _, xhat, r, n = _rms_fwd(x1_ref[...], g, sc, mod_ref[3:4, :])
        dx, dsh, dsc, dg = _rms_bwd(dh2, xhat, r, n, g, sc)
        acc_ref[3:4, :] += dsh
        acc_ref[4:5, :] += dsc
        acc_ref[7:8, :] += dg
        dx1_ref[...] = dx2v + dx

    tile = pl.BlockSpec((tm, D), lambda i: (i, 0))
    wide = pl.BlockSpec((tm, F), lambda i: (i, 0))
    return _pcall(
        body, name="ffn_bwd", grid=(S // tm,),
        out_shape=(jax.ShapeDtypeStruct((S, D), F32), jax.ShapeDtypeStruct((S, F), BF16),
                   jax.ShapeDtypeStruct((S, D), BF16), jax.ShapeDtypeStruct((8, D), F32)),
        in_specs=[tile, tile, wide, tile, pl.BlockSpec((8, D), lambda i: (0, 0)), pl.BlockSpec((1, D), lambda i: (0, 0)),
                  _wspec(g_w1), _wspec(g_w2)],
        out_specs=(tile, wide, tile, pl.BlockSpec((8, D), lambda i: (0, 0))),
        operands=(dx2, x1, a, z, mod_l, g_ffn, g_w1, g_w2), **comm)


def _dw_blocked(a, b, by_rows, square_a, name):
    S = a.shape[0]
    tk = min(TK, S)
    nk = S // tk
    if by_rows:
        bm, bn = a.shape[1] // NQ, b.shape[1]
        a_map, b_map = (lambda q, k: (k, q)), (lambda q, k: (k, 0))
    else:
        bm, bn = a.shape[1], b.shape[1] // NQ
        a_map, b_map = (lambda q, k: (k, 0)), (lambda q, k: (k, q))

    def body(a_ref, b_ref, o_ref, acc):
        k = pl.program_id(1)

        @pl.when(k == 0)
        def _():
            acc[...] = jnp.zeros_like(acc)

        av = a_ref[...]
        if square_a:
            af = av.astype(F32)
            av = (af * af).astype(BF16)
        acc[...] += _dot_tn(av, b_ref[...])

        @pl.when(k == nk - 1)
        def _():
            o_ref[...] = acc[...].astype(o_ref.dtype)

    return pl.pallas_call(
        body, name=name, grid=(NQ, nk),
        out_shape=jax.ShapeDtypeStruct((NQ, bm, bn), BF16),
        in_specs=[pl.BlockSpec((tk, bm), a_map), pl.BlockSpec((tk, bn), b_map)],
        out_specs=pl.BlockSpec((None, bm, bn), lambda q, k: (q, 0, 0)),
        scratch_shapes=[pltpu.VMEM((bm, bn), F32)],
        compiler_params=_cparams("parallel", "arbitrary"),
    )(a, b)


def _dw_whole(a, bs, name):
    S, M = a.shape
    N = bs[0].shape[1]
    tk = min(TK, S)
    nk = S // tk
    nb = len(bs)

    def body(*refs):
        a_ref, b_refs, o_refs, accs = refs[0], refs[1:1 + nb], refs[1 + nb:1 + 2 * nb], refs[1 + 2 * nb:]
        k = pl.program_id(0)

        @pl.when(k == 0)
        def _():
            for acc in accs:
                acc[...] = jnp.zeros_like(acc)

        av = a_ref[...]
        for b_ref, acc in zip(b_refs, accs):
            acc[...] += _dot_tn(av, b_ref[...])

        @pl.when(k == nk - 1)
        def _():
            for o_ref, acc in zip(o_refs, accs):
                o_ref[...] = acc[...].reshape(NQ, M // NQ, N).astype(o_ref.dtype)

    return pl.pallas_call(
        body, name=name, grid=(nk,),
        out_shape=tuple(jax.ShapeDtypeStruct((NQ, M // NQ, N), BF16) for _ in bs),
        in_specs=[pl.BlockSpec((tk, M), lambda k: (k, 0))] + [pl.BlockSpec((tk, N), lambda k: (k, 0)) for _ in bs],
        out_specs=tuple(pl.BlockSpec((NQ, M // NQ, N), lambda k: (0, 0, 0)) for _ in bs),
        scratch_shapes=[pltpu.VMEM((M, N), F32) for _ in bs],
        compiler_params=_cparams("arbitrary"),
    )(a, *bs)


def _lru_out_bwd(dx1, y, mod_l, g_wout):
    S, D = dx1.shape
    tm = min(TM, S)

    def body(dx1_ref, y_ref, mod_ref, wo_ref, dy_ref, dp_ref, acc_ref):
        _zero_first(acc_ref)
        dx1v = dx1_ref[...]
        acc_ref[2:3, :] += jnp.sum(dx1v * y_ref[...].astype(F32), axis=0, keepdims=True)
        dy = dx1v * mod_ref[2:3, :]
        acc_ref[3:4, :] += jnp.sum(dy, axis=0, keepdims=True)
        dyb = dy.astype(BF16)
        dy_ref[...] = dyb
        dp_ref[...] = _dot_nt(dyb, wo_ref[...].reshape(D, D))

    tile = pl.BlockSpec((tm, D), lambda i: (i, 0))
    return _pcall(
        body, name="lru_out_bwd", grid=(S // tm,),
        out_shape=(jax.ShapeDtypeStruct((S, D), BF16), jax.ShapeDtypeStruct((S, D), F32), jax.ShapeDtypeStruct((8, D), F32)),
        in_specs=[tile, tile, pl.BlockSpec((8, D), lambda i: (0, 0)), _wspec(g_wout)],
        out_specs=(tile, tile, pl.BlockSpec((8, D), lambda i: (0, 0))),
        operands=(dx1, y, mod_l, g_wout))


def _lru_scan_bwd(dp, xr0, gb, hs, cw, cb, wa, ba, wx, bx, lam, **comm):
    S, W = xr0.shape
    tt = min(TT, S)
    nb = S // tt
    hd = W // HEADS

    def body(dp_ref, xr_ref, xrh_ref, gb_ref, hs_ref, hsh_ref, cw_ref, cb_ref, wa_ref, ba_ref, wx_ref, bx_ref, lam_ref,
             dgb_ref, dxr_ref, sm_ref, dwa_ref, dwx_ref, xext, hext, qext, dext, a_s, b_s, qc, dc):
        i = pl.program_id(0)
        blk = nb - 1 - i

        @pl.when(i == 0)
        def _():
            sm_ref[...] = jnp.zeros_like(sm_ref)
            dwa_ref[...] = jnp.zeros_like(dwa_ref)
            dwx_ref[...] = jnp.zeros_like(dwx_ref)
            qc[...] = jnp.zeros_like(qc)
            dc[...] = jnp.zeros_like(dc)

        xext[0:8, :] = jnp.where(blk > 0, xrh_ref[...], 0.0)
        xext[pl.ds(8, tt), :] = xr_ref[...]
        hext[0:8, :] = jnp.where(blk > 0, hsh_ref[...], 0.0)
        hext[pl.ds(8, tt), :] = hs_ref[...]
        cw = cw_ref[...]
        lam = lam_ref[...]
        xc = _conv_taps(xext, cw, tt) + cb_ref[...]
        xcb, gate_r, gate_i, ls, a, mult = _lru_gates(xc, wa_ref, ba_ref[...], wx_ref, bx_ref[...], lam, hd)

        gbv = gb_ref[...]
        gate, th = _gelu(gbv)
        dpv = dp_ref[...]
        dgb = dpv * hs_ref[...] * _gelu_grad(gbv, th)
        dgb_ref[...] = dgb.astype(BF16)
        sm_ref[9:10, :] += jnp.sum(dgb, axis=0, keepdims=True)
        dhs = dpv * gate

        a_s[...] = a
        b_s[...] = a * dhs
        qext[pl.ds(tt, 8), :] = qc[...]
        row = lax.broadcasted_iota(jnp.int32, (8, W), 0)

        def step(k, _):
            off = pl.multiple_of((tt // 8 - 1 - k) * 8, 8)
            A = a_s[pl.ds(off, 8), :]
            B = b_s[pl.ds(off, 8), :]
            for d in (1, 2, 4):
                keep = row < 8 - d
                Bs = jnp.where(keep, pltpu.roll(B, 8 - d, 0), 0.0)
                As = jnp.where(keep, pltpu.roll(A, 8 - d, 0), 1.0)
                B = B + A * Bs
                A = A * As
            Q = B + A * qc[...]
            qext[pl.ds(off, 8), :] = Q
            qc[...] = jnp.broadcast_to(Q[0:1, :], (8, W))
            return 0

        lax.fori_loop(0, tt // 8, step, 0)
        gsc = dhs + qext[pl.ds(1, tt), :]
        da = gsc * hext[pl.ds(7, tt), :]
        t1 = gsc * xc
        dmult = t1 * gate_i
        dgate_i = t1 * mult
        dxc = gsc * (mult * gate_i)
        dlog_a = da * a - dmult * (a * a) / mult
        dgate_r = dlog_a * (LRU_C * ls)
        sm_ref[7:8, :] += jnp.sum(dlog_a * (LRU_C * gate_r), axis=0, keepdims=True)
        dga = dgate_r * gate_r * (1.0 - gate_r)
        dgx = dgate_i * gate_i * (1.0 - gate_i)
        sm_ref[5:6, :] += jnp.sum(dga, axis=0, keepdims=True)
        sm_ref[6:7, :] += jnp.sum(dgx, axis=0, keepdims=True)
        dgab = dga.astype(BF16)
        dgxb = dgx.astype(BF16)
        dxc = dxc + _heads_dot(dgab, wa_ref, hd, nt=True) + _heads_dot(dgxb, wx_ref, hd, nt=True)
        for h in range(HEADS):
            sl = slice(h * hd, (h + 1) * hd)
            dwa_ref[h] += _dot_tn(xcb[:, sl], dgab[:, sl])
            dwx_ref[h] += _dot_tn(xcb[:, sl], dgxb[:, sl])
        sm_ref[4:5, :] += jnp.sum(dxc, axis=0, keepdims=True)
        for k in range(CONV_W):
            sm_ref[k:k + 1, :] += jnp.sum(dxc * xext[pl.ds(8 - (CONV_W - 1) + k, tt), :], axis=0, keepdims=True)
        dext[pl.ds(0, tt), :] = dxc
        dext[pl.ds(tt, 8), :] = dc[...]
        dxr = cw[0:1, :] * dext[pl.ds(CONV_W - 1, tt), :]
        for k in range(1, CONV_W):
            dxr = dxr + cw[k:k + 1, :] * dext[pl.ds(CONV_W - 1 - k, tt), :]
        dc[...] = dext[0:8, :]
        sm_ref[8:9, :] += jnp.sum(dxr, axis=0, keepdims=True)
        dxr_ref[...] = dxr.astype(BF16)

        @pl.when(i == nb - 1)
        def _():
            sm_ref[7:8, :] = sm_ref[7:8, :] * jax.nn.sigmoid(-lam)

    rev = lambda i: (nb - 1 - i, 0)
    tile = pl.BlockSpec((tt, W), rev)
    halo = pl.BlockSpec((8, W), lambda i: (jnp.maximum((nb - 1 - i) * (tt // 8) - 1, 0), 0))
    row = pl.BlockSpec((1, W), lambda i: (0, 0))
    wblk = pl.BlockSpec((HEADS, hd, hd), lambda i: (0, 0, 0))
    return _pcall(
        body, name="lru_scan_bwd", grid=(nb,),
        out_shape=(jax.ShapeDtypeStruct((S, W), BF16), jax.ShapeDtypeStruct((S, W), BF16),
                   jax.ShapeDtypeStruct((16, W), F32), jax.ShapeDtypeStruct((HEADS, hd, hd), F32),
                   jax.ShapeDtypeStruct((HEADS, hd, hd), F32)),
        in_specs=[tile, tile, halo, tile, tile, halo, pl.BlockSpec((CONV_W, W), lambda i: (0, 0)), row,
                  wblk, row, wblk, row, row],
        out_specs=(tile, tile, pl.BlockSpec((16, W), lambda i: (0, 0)), wblk, wblk),
        scratch_shapes=[pltpu.VMEM((tt + 8, W), F32), pltpu.VMEM((tt + 8, W), F32), pltpu.VMEM((tt + 8, W), F32),
                        pltpu.VMEM((tt + 8, W), F32), pltpu.VMEM((tt, W), F32), pltpu.VMEM((tt, W), F32),
                        pltpu.VMEM((8, W), F32), pltpu.VMEM((8, W), F32)],
        operands=(dp, xr0, xr0, gb, hs, hs, cw, cb, wa, ba, wx, bx, lam), **comm)


def _lru_in_bwd(dxr, dgb, dx1, x, mod_l, g_mix, g_wy, g_win, **comm):
    S, D = x.shape
    tm = min(TM, S)

    def body(dxr_ref, dgb_ref, dx1_ref, x_ref, mod_ref, g_ref, wy_ref, win_ref, dx_ref, acc_ref):
        _zero_first(acc_ref)
        dh = _dot_nt(dxr_ref[...], win_ref[...].reshape(D, D)) + _dot_nt(dgb_ref[...], wy_ref[...].reshape(D, D))
        g, sc = g_ref[...], mod_ref[1:2, :]
        _, xhat, r, n = _rms_fwd(x_ref[...], g, sc, mod_ref[0:1, :])
        dx, dsh, dsc, dg = _rms_bwd(dh, xhat, r, n, g, sc)
        acc_ref[0:1, :] += dsh
        acc_ref[1:2, :] += dsc
        acc_ref[6:7, :] += dg
        dx_ref[...] = dx1_ref[...] + dx

    tile = pl.BlockSpec((tm, D), lambda i: (i, 0))
    return _pcall(
        body, name="lru_in_bwd", grid=(S // tm,),
        out_shape=(jax.ShapeDtypeStruct((S, D), F32), jax.ShapeDtypeStruct((8, D), F32)),
        in_specs=[tile, tile, tile, tile, pl.BlockSpec((8, D), lambda i: (0, 0)), pl.BlockSpec((1, D), lambda i: (0, 0)),
                  _wspec(g_wy), _wspec(g_win)],
        out_specs=(tile, pl.BlockSpec((8, D), lambda i: (0, 0))),
        operands=(dxr, dgb, dx1, x, mod_l, g_mix, g_wy, g_win), **comm)


def _pool_bwd(dx1, x, pooled, mod_l, g_mix, pw, ps):
    S, D = x.shape
    tm = min(TP, S)
    nb = S // tm
    ng = len(POOL_WINDOWS)
    gd = D // ng
    n = tm + 24

    def body(dx1_ref, dxh_ref, x_ref, pl_ref, mod_ref, gm_ref, pw_ref, ps_ref, dx_ref, acc_ref, dpw_ref, ext, b1, b2):
        i = pl.program_id(0)

        @pl.when(i == 0)
        def _():
            acc_ref[...] = jnp.zeros_like(acc_ref)
            dpw_ref[...] = jnp.zeros_like(dpw_ref)

        gt, psv = mod_ref[2:3, :], ps_ref[...]
        wvec = _window_vec(D)
        dx1v = dx1_ref[...]
        pooled = pl_ref[...]
        mixed = _heads_dot(pooled, pw_ref, gd)
        acc_ref[2:3, :] += jnp.sum(dx1v * (mixed * psv), axis=0, keepdims=True)
        dy = dx1v * gt
        acc_ref[3:4, :] += jnp.sum(dy * mixed, axis=0, keepdims=True)
        dmix = (dy * psv).astype(BF16)
        for gi in range(ng):
            sl = slice(gi * gd, (gi + 1) * gd)
            dpw_ref[gi] += _dot_tn(pooled[:, sl], dmix[:, sl])
        dpooled = _heads_dot(dmix, pw_ref, gd, nt=True)
        dmix_h = (dxh_ref[...] * gt * psv).astype(BF16)
        dpooled_h = jnp.where(i < nb - 1, _heads_dot(dmix_h, pw_ref, gd, nt=True), 0.0)
        t1 = (lax.broadcasted_iota(jnp.int32, (tm, 1), 0) + (i * tm + 1)).astype(F32)
        t1h = (lax.broadcasted_iota(jnp.int32, (16, 1), 0) + ((i + 1) * tm + 1)).astype(F32)
        zeros8 = jnp.zeros((8, D), F32)
        ext[pl.ds(0, tm), :] = dpooled / jnp.minimum(t1, wvec)
        ext[pl.ds(tm, 16), :] = dpooled_h / jnp.minimum(t1h, wvec)
        ext[pl.ds(tm + 16, 8), :] = zeros8
        b1[pl.ds(tm + 16, 8), :] = zeros8
        b2[pl.ds(tm + 16, 8), :] = zeros8
        m = n - 8
        b1[pl.ds(0, m), :] = ext[pl.ds(0, m), :] + ext[pl.ds(1, m), :]
        b2[pl.ds(0, m), gd:] = b1[pl.ds(0, m), gd:] + b1[pl.ds(2, m), gd:]
        b1[pl.ds(0, m), 2 * gd:] = b2[pl.ds(0, m), 2 * gd:] + b2[pl.ds(4, m), 2 * gd:]
        b2[pl.ds(0, m), 3 * gd:] = b1[pl.ds(0, m), 3 * gd:] + b1[pl.ds(8, m), 3 * gd:]
        wsum = jnp.concatenate([b1[pl.ds(0, tm), 0:gd], b2[pl.ds(0, tm), gd:2 * gd],
                                b1[pl.ds(0, tm), 2 * gd:3 * gd], b2[pl.ds(0, tm), 3 * gd:]], axis=1)
        dh = wsum - dpooled
        g, sc = gm_ref[...], mod_ref[1:2, :]
        _, xhat, r, nn = _rms_fwd(x_ref[...], g, sc, mod_ref[0:1, :])
        dx, dsh, dsc, dg = _rms_bwd(dh, xhat, r, nn, g, sc)
        acc_ref[0:1, :] += dsh
        acc_ref[1:2, :] += dsc
        acc_ref[6:7, :] += dg
        dx_ref[...] = dx1v + dx

    tile = pl.BlockSpec((tm, D), lambda i: (i, 0))
    halo = pl.BlockSpec((16, D), lambda i: (jnp.minimum((i + 1) * (tm // 16), S // 16 - 1), 0))
    row = pl.BlockSpec((1, D), lambda i: (0, 0))
    wblk = pl.BlockSpec((ng, gd, gd), lambda i: (0, 0, 0))
    return pl.pallas_call(
        body, name="pool_bwd", grid=(nb,),
        out_shape=(jax.ShapeDtypeStruct((S, D), F32), jax.ShapeDtypeStruct((8, D), F32),
                   jax.ShapeDtypeStruct((ng, gd, gd), F32)),
        in_specs=[tile, halo, tile, tile, pl.BlockSpec((8, D), lambda i: (0, 0)), row, wblk, row],
        out_specs=(tile, pl.BlockSpec((8, D), lambda i: (0, 0)), wblk),
        scratch_shapes=[pltpu.VMEM((n, D), F32), pltpu.VMEM((n, D), F32), pltpu.VMEM((n, D), F32)],
        compiler_params=_cparams("arbitrary"),
    )(dx1, dx1, x, pooled, mod_l, g_mix, pw, ps)


def _shard_to_rows(w, D):
    return w.reshape(-1, D)


def _blockdiag_full(gq, na, hd):
    return gq.reshape(NQ, na, HEADS, hd // NQ, hd).transpose(1, 2, 0, 3, 4).reshape(na, HEADS, hd, hd)


def _blockdiag_by_chip(dw, D):
    na, _, hd, _ = dw.shape
    return dw.reshape(na, HEADS, NQ, hd // NQ, hd).transpose(2, 0, 1, 3, 4).reshape(NQ, -1, D)


def kernel(x, c, w_mod, b_mod, norm_mix_g, norm_ffn_g, lru_w_y, lru_b_y, lru_w_in, lru_b_in, lru_conv_w, lru_conv_b, lru_w_a, lru_b_a, lru_w_x, lru_b_x, lru_lambda, lru_w_out, lru_b_out, pool_w, pool_scale, ffn_w1, ffn_w2, final_norm_g, loss_target, m_w_mod, m_b_mod, m_norm_mix_g, m_norm_ffn_g, m_lru_w_y, m_lru_b_y, m_lru_w_in, m_lru_b_in, m_lru_conv_w, m_lru_conv_b, m_lru_w_a, m_lru_b_a, m_lru_w_x, m_lru_b_x, m_lru_lambda, m_lru_w_out, m_lru_b_out, m_pool_w, m_pool_scale, m_ffn_w1, m_ffn_w2, m_final_norm_g, v_w_mod, v_b_mod, v_norm_mix_g, v_norm_ffn_g, v_lru_w_y, v_lru_b_y, v_lru_w_in, v_lru_b_in, v_lru_conv_w, v_lru_conv_b, v_lru_w_a, v_lru_b_a, v_lru_w_x, v_lru_b_x, v_lru_lambda, v_lru_w_out, v_lru_b_out, v_pool_w, v_pool_scale, v_ffn_w1, v_ffn_w2, v_final_norm_g):
    S, D = x.shape[1], x.shape[2]
    L = w_mod.shape[0]
    NA = lru_w_y.shape[0]
    NB = pool_w.shape[0]
    F = ffn_w1.shape[2] * NQ
    f4 = F // NQ
    hd = D // HEADS
    Cs = w_mod.shape[2]
    assert L == DEPTH and Cs * NQ == N_MOD * D and D % 1024 == 0
    x2d = x.reshape(S, D)
    tgt = loss_target.reshape(S, D)
    q = 2 * lax.axis_index("x") + lax.axis_index("y")

    big = [ffn_w1, ffn_w2, lru_w_y, lru_w_in, lru_w_out, lru_w_a, lru_w_x, pool_w]
    rows = [int(w.size) // D for w in big]
    offs = [sum(rows[:k]) for k in range(len(big))]
    O_W1, O_W2, O_WY, O_WIN, O_WOUT, O_WA, O_WX, O_PW = offs
    R = sum(rows)
    dq = D // NQ
    s_w1 = [ffn_w1[i].astype(BF16) for i in range(L)]
    s_w2 = [ffn_w2[i].astype(BF16) for i in range(L)]
    s_wy = [lru_w_y[j].astype(BF16) for j in range(NA)]
    s_win = [lru_w_in[j].astype(BF16) for j in range(NA)]
    s_wout = [lru_w_out[j].astype(BF16) for j in range(NA)]
    s_tiny = jnp.concatenate([_shard_to_rows(w, D) for w in (lru_w_a, lru_w_x, pool_w)], axis=0).astype(BF16)

    cshard = lru_conv_w.reshape(-1)
    small_fwd = jnp.concatenate([c.reshape(-1), cshard, lru_b_a.reshape(-1), lru_b_x.reshape(-1),
                                 pool_scale.reshape(-1)])
    small_fwd = jnp.pad(small_fwd, (0, 8 * D - small_fwd.shape[0])).reshape(8, D)

    g_w1, g_w2 = [None] * L, [None] * L
    g_wy, g_win, g_wout = [None] * NA, [None] * NA, [None] * NA
    SG, g_wy[0], g_win[0], g_tiny = _comm_only("gather_first", small=small_fwd, gathers=(s_wy[0], s_win[0], s_tiny))
    SGf = SG.reshape(NDEV, 8 * D)
    c_all = SGf[:, :D]
    SGq = SGf[0::2]
    o = D
    n_cw = NA * CONV_W * D // NQ
    conv_w_full = SGq[:, o:o + n_cw].reshape(NQ, NA, CONV_W, D // NQ).transpose(1, 2, 0, 3).reshape(NA, CONV_W, D)
    o += n_cw
    n_b = NA * HEADS * hd // NQ
    b_a_full = SGq[:, o:o + n_b].reshape(NQ, NA, HEADS, hd // NQ).transpose(1, 2, 0, 3).reshape(NA, 1, D)
    o += n_b
    b_x_full = SGq[:, o:o + n_b].reshape(NQ, NA, HEADS, hd // NQ).transpose(1, 2, 0, 3).reshape(NA, 1, D)
    o += n_b
    n_ps = NB * D // NQ
    pool_scale_full = SGq[:, o:o + n_ps].reshape(NQ, NB, D // NQ).transpose(1, 0, 2).reshape(NB, 1, D)

    wa_full = _blockdiag_full(g_tiny[:, :rows[5]], NA, hd)
    wx_full = _blockdiag_full(g_tiny[:, rows[5]:rows[5] + rows[6]], NA, hd)
    pw_full = _blockdiag_full(g_tiny[:, rows[5] + rows[6]:], NB, hd)

    b_mod_sh = lax.dynamic_slice_in_dim(b_mod, q * Cs, Cs, axis=1).reshape(L, 1, Cs)
    modpart = _mod_part(c_all, w_mod, b_mod_sh)
    modq = _exchange_mod(modpart.transpose(1, 0, 2))
    mod = modq.transpose(1, 0, 2).reshape(L, N_MOD, D)
    mod = jnp.pad(mod, ((0, 0), (0, 8 - N_MOD), (0, 0)))

    saved = []
    xcur = x2d
    for i in range(L):
        j = i // 2
        gm = norm_mix_g[i].reshape(1, D)
        gf = norm_ffn_g[i].reshape(1, D)
        if i % 2 == 0:
            if g_wout[j] is None:
                h, gb, xr0, g_wout[j] = _lru_in_fwd(xcur, mod[i], gm, g_wy[j], g_win[j], lru_b_y[j].reshape(1, D),
                                                    lru_b_in[j].reshape(1, D), gathers=(s_wout[j],))
            else:
                h, gb, xr0 = _lru_in_fwd(xcur, mod[i], gm, g_wy[j], g_win[j], lru_b_y[j].reshape(1, D),
                                         lru_b_in[j].reshape(1, D))
            hs, p, g_w1[i], g_w2[i] = _lru_scan_fwd(xr0, gb, conv_w_full[j], lru_conv_b[j].reshape(1, D), wa_full[j],
                                                    b_a_full[j], wx_full[j], b_x_full[j], lru_lambda[j].reshape(1, D),
                                                    gathers=(s_w1[i], s_w2[i]))
            y, x1, h2, a, z, x2, g_w1[i + 1], g_w2[i + 1] = _lru_mix_ffn_fwd(
                xcur, p, mod[i], gf, lru_b_out[j].reshape(1, D), g_wout[j], g_w1[i], g_w2[i],
                gathers=(s_w1[i + 1], s_w2[i + 1]))
            saved.append(dict(x=xcur, h=h, gb=gb, xr0=xr0, hs=hs, p=p, y=y, x1=x1, h2=h2, a=a, z=z))
        else:
            if j + 1 < NA:
                pooled, x1, h2, a, z, x2, g_wy[j + 1], g_win[j + 1], g_wout[j + 1] = _pool_mix_ffn_fwd(
                    xcur, mod[i], gm, pw_full[j], pool_scale_full[j], gf, g_w1[i], g_w2[i],
                    gathers=(s_wy[j + 1], s_win[j + 1], s_wout[j + 1]))
            else:
                pooled, x1, h2, a, z, x2 = _pool_mix_ffn_fwd(xcur, mod[i], gm, pw_full[j], pool_scale_full[j], gf,
                                                             g_w1[i], g_w2[i])
            saved.append(dict(x=xcur, pooled=pooled, x1=x1, h2=h2, a=a, z=z))
        xcur = x2

    dx, lacc = _final_loss_bwd(xcur, final_norm_g.reshape(1, D), tgt)
    loss = lax.psum(0.5 * jnp.sum(lacc[0]) / D, ("x", "y", "c"))
    d_final_g = lacc[1]

    ppack = jnp.zeros((R, D), F32)
    psib = jnp.zeros((R, D), F32)
    pending, summed = [], []

    def comm_args():
        kw = {}
        if pending:
            kw["scatters"] = tuple(dw for dw, _ in pending)
        if summed:
            kw["sib"] = (ppack, psib, tuple(summed))
        return kw

    def after_host(extra):
        nonlocal ppack, psib, pending, summed
        had_sib = bool(summed)
        summed = []
        for (dw, off), rb in zip(pending, extra[:len(pending)]):
            own = lax.dynamic_index_in_dim(dw, q, axis=0, keepdims=False)
            ppack = _sum_into(ppack, own, rb, off)
            summed.append((off, dw.shape[1]))
        if had_sib:
            psib = extra[len(pending)]
        pending = []

    dmod_rows = [None] * L
    dg_mix = [None] * L
    dg_ffn = [None] * L
    d_small = {}
    dwa_l, dwx_l, dpw_l = [None] * NA, [None] * NA, [None] * NB
    for i in reversed(range(L)):
        j = i // 2
        sv = saved[i]
        gm = norm_mix_g[i].reshape(1, D)
        gf = norm_ffn_g[i].reshape(1, D)
        outs = _ffn_bwd(dx, sv["x1"], sv["a"], sv["z"], mod[i], gf, g_w1[i], g_w2[i], **comm_args())
        dx1, du, dz, facc = outs[:4]
        after_host(outs[4:])
        pending.append((_dw_blocked(sv["h2"], du, False, False, "dw1"), O_W1 + i * D))
        pending.append((_dw_blocked(sv["a"], dz, True, True, "dw2"), O_W2 + i * f4))
        if i % 2 == 0:
            dyp, dp, oacc = _lru_out_bwd(dx1, sv["y"], mod[i], g_wout[j])
            pending.append((_dw_whole(sv["p"], [dyp], "dwout")[0], O_WOUT + j * dq))
            outs = _lru_scan_bwd(dp, sv["xr0"], sv["gb"], sv["hs"], conv_w_full[j], lru_conv_b[j].reshape(1, D),
                                 wa_full[j], b_a_full[j], wx_full[j], b_x_full[j], lru_lambda[j].reshape(1, D),
                                 **comm_args())
            dgb, dxr, sm, dwa, dwx = outs[:5]
            after_host(outs[5:])
            dwy, dwin = _dw_whole(sv["h"], [dgb, dxr], "dwy_dwin")
            pending.append((dwy, O_WY + j * dq))
            pending.append((dwin, O_WIN + j * dq))
            if i == 0:
                outs = _lru_in_bwd(dxr, dgb, dx1, sv["x"], mod[i], gm, g_wy[j], g_win[j], **comm_args())
                dx, macc = outs[:2]
                after_host(outs[2:])
            else:
                dx, macc = _lru_in_bwd(dxr, dgb, dx1, sv["x"], mod[i], gm, g_wy[j], g_win[j])
            dwa_l[j], dwx_l[j] = dwa, dwx
            d_small[("lru", j)] = (sm, oacc[3])
            dgt_m = oacc[2]
        else:
            dx, macc, dpw = _pool_bwd(dx1, sv["x"], sv["pooled"], mod[i], gm, pw_full[j], pool_scale_full[j])
            dpw_l[j] = dpw
            d_small[("pool", j)] = macc[3]
            dgt_m = macc[2]
        dmod_rows[i] = jnp.stack([macc[0], macc[1], dgt_m, facc[3], facc[4], facc[5]])
        dg_mix[i] = macc[6]
        dg_ffn[i] = facc[7]
    grad_x = dx.reshape(x.shape)

    tiny = jnp.concatenate([_blockdiag_by_chip(jnp.stack(dwa_l), D), _blockdiag_by_chip(jnp.stack(dwx_l), D),
                            _blockdiag_by_chip(jnp.stack(dpw_l), D)], axis=1).astype(BF16)
    pending.append((tiny, O_WA))

    lru_sm = [d_small[("lru", j)] for j in range(NA)]
    small_rows = [jnp.stack(dmod_rows).reshape(L * N_MOD, D), jnp.stack(dg_mix), jnp.stack(dg_ffn),
                  jnp.stack([s[0][9] for s in lru_sm]), jnp.stack([s[0][8] for s in lru_sm]),
                  jnp.stack([s[0][4] for s in lru_sm]), jnp.stack([s[0][7] for s in lru_sm]),
                  jnp.stack([s[1] for s in lru_sm]),
                  jnp.stack([s[0][0:CONV_W] for s in lru_sm]).reshape(NA * CONV_W, D),
                  jnp.stack([s[0][5] for s in lru_sm]), jnp.stack([s[0][6] for s in lru_sm]),
                  jnp.stack([d_small[("pool", j)] for j in range(NB)]), d_final_g.reshape(1, D)]
    small_g = jnp.concatenate(small_rows, axis=0)
    n_small = small_g.shape[0]
    assert n_small <= SMALL_ROWS
    small_g = jnp.pad(small_g, ((0, SMALL_ROWS - n_small), (0, 0)))

    outs = _comm_only("scatter_last", small=small_g, reduce_small=True, **comm_args())
    sg_all, sg_sum = outs[:2]
    after_host(outs[2:])
    psum_mine = ppack
    psum_sib = _comm_only("sibling_last", sib=(ppack, psib, tuple(summed)))[0]

    def big_update(w, m, v, off, name):
        shp = w.shape
        g, dl, m2, v2 = _adam_rows(w.reshape(-1, D), m.reshape(-1, D), v.reshape(-1, D), psum_mine, psum_sib, off, name)
        return g.reshape(shp), dl.reshape(shp), m2.reshape(shp), v2.reshape(shp)

    res = {}
    res["ffn_w1"] = big_update(ffn_w1, m_ffn_w1, v_ffn_w1, O_W1, "adam_w1")
    res["ffn_w2"] = big_update(ffn_w2, m_ffn_w2, v_ffn_w2, O_W2, "adam_w2")
    res["lru_w_y"] = big_update(lru_w_y, m_lru_w_y, v_lru_w_y, O_WY, "adam_wy")
    res["lru_w_in"] = big_update(lru_w_in, m_lru_w_in, v_lru_w_in, O_WIN, "adam_win")
    res["lru_w_out"] = big_update(lru_w_out, m_lru_w_out, v_lru_w_out, O_WOUT, "adam_wout")
    res["lru_w_a"] = big_update(lru_w_a, m_lru_w_a, v_lru_w_a, O_WA, "adam_wa")
    res["lru_w_x"] = big_update(lru_w_x, m_lru_w_x, v_lru_w_x, O_WX, "adam_wx")
    res["pool_w"] = big_update(pool_w, m_pool_w, v_pool_w, O_PW, "adam_pw")

    dmod_all = sg_all[:, :L * N_MOD, :].reshape(NDEV, L, N_MOD * D)
    dmod_sh = lax.dynamic_slice_in_dim(dmod_all, q * Cs, Cs, axis=2).transpose(1, 0, 2)
    res["w_mod"] = _wmod_update(c_all.T, dmod_sh, w_mod, m_w_mod, v_w_mod)

    r0 = 0

    def take(nrows):
        nonlocal r0
        out = sg_sum[r0:r0 + nrows]
        r0 += nrows
        return out

    g_b_mod = take(L * N_MOD).reshape(L, N_MOD * D)
    g_mix = take(L)
    g_ffn = take(L)
    g_b_y = take(NA)
    g_b_in = take(NA)
    g_conv_b = take(NA)
    g_lam = take(NA)
    g_b_out = take(NA)
    g_conv_w = lax.dynamic_slice_in_dim(take(NA * CONV_W).reshape(NA, CONV_W, D), q * (D // NQ), D // NQ, axis=2)
    g_b_a = lax.dynamic_slice_in_dim(take(NA).reshape(NA, HEADS, hd), q * (hd // NQ), hd // NQ, axis=2)
    g_b_x = lax.dynamic_slice_in_dim(take(NA).reshape(NA, HEADS, hd), q * (hd // NQ), hd // NQ, axis=2)
    g_ps = lax.dynamic_slice_in_dim(take(NB), q * (D // NQ), D // NQ, axis=1)
    g_fin = take(1).reshape(D)

    smalls = [("b_mod", b_mod, m_b_mod, v_b_mod, g_b_mod), ("norm_mix_g", norm_mix_g, m_norm_mix_g, v_norm_mix_g, g_mix),
              ("norm_ffn_g", norm_ffn_g, m_norm_ffn_g, v_norm_ffn_g, g_ffn),
              ("lru_b_y", lru_b_y, m_lru_b_y, v_lru_b_y, g_b_y), ("lru_b_in", lru_b_in, m_lru_b_in, v_lru_b_in, g_b_in),
              ("lru_conv_w", lru_conv_w, m_lru_conv_w, v_lru_conv_w, g_conv_w),
              ("lru_conv_b", lru_conv_b, m_lru_conv_b, v_lru_conv_b, g_conv_b),
              ("lru_b_a", lru_b_a, m_lru_b_a, v_lru_b_a, g_b_a), ("lru_b_x", lru_b_x, m_lru_b_x, v_lru_b_x, g_b_x),
              ("lru_lambda", lru_lambda, m_lru_lambda, v_lru_lambda, g_lam),
              ("lru_b_out", lru_b_out, m_lru_b_out, v_lru_b_out, g_b_out),
              ("pool_scale", pool_scale, m_pool_scale, v_pool_scale, g_ps),
              ("final_norm_g", final_norm_g, m_final_norm_g, v_final_norm_g, g_fin)]
    total = sum(int(s[1].size) for s in smalls)
    prow = -(-total // (8 * 128)) * 8

    def flat(k):
        f = jnp.concatenate([s[k].reshape(-1) for s in smalls])
        return jnp.pad(f, (0, prow * 128 - total)).reshape(prow, 128)

    dl_s, m_s, v_s = _adam_small(flat(1), flat(4), flat(2), flat(3))
    o = 0
    for name, w, _, _, g in smalls:
        sz = int(w.size)
        cut = lambda arr: arr.reshape(-1)[o:o + sz].reshape(w.shape)
        res[name] = (g.reshape(w.shape), cut(dl_s), cut(m_s), cut(v_s))
        o += sz

    order = ["w_mod", "b_mod", "norm_mix_g", "norm_ffn_g", "lru_w_y", "lru_b_y", "lru_w_in", "lru_b_in", "lru_conv_w",
             "lru_conv_b", "lru_w_a", "lru_b_a", "lru_w_x", "lru_b_x", "lru_lambda", "lru_w_out", "lru_b_out", "pool_w",
             "pool_scale", "ffn_w1", "ffn_w2", "final_norm_g"]
    return (loss, grad_x, *[res[n][0] for n in order], *[res[n][1] for n in order],
            *[res[n][2] for n in order], *[res[n][3] for n in order])
```

```python
import functools

import jax
import jax.numpy as jnp
from jax import lax
from jax.experimental import pallas as pl
from jax.experimental.pallas import tpu as pltpu

F32 = jnp.float32
BF16 = jnp.bfloat16
MESH = pl.DeviceIdType.MESH

NQ = 4
NDEV = 8
DEPTH = 4
N_MOD = 6
HEADS = 4
CONV_W = 4
POOL_WINDOWS = (2, 4, 8, 16)
LRU_C = 8.0
EPS = 1e-6
ADAM_LR, ADAM_B1, ADAM_B2, ADAM_EPS, ADAM_WD, ADAM_STEP = 0.001, 0.9, 0.999, 1e-08, 0.01, 10

TM = 512
TT = 256
TP = 256
TK = 2048
SMALL_ROWS = 64
VMEM_LIMIT = 60 * 1024 * 1024


def _cparams(*sem):
    return pltpu.CompilerParams(dimension_semantics=tuple(sem), vmem_limit_bytes=VMEM_LIMIT)


def _dot(a, b):
    return jnp.dot(a, b, preferred_element_type=F32)


def _dot_nt(a, b):
    return lax.dot_general(a, b, (((1,), (1,)), ((), ())), preferred_element_type=F32)


def _dot_tn(a, b):
    return lax.dot_general(a, b, (((0,), (0,)), ((), ())), preferred_element_type=F32)


def _resident(shape, index_map):
    return pl.BlockSpec(shape, index_map, pipeline_mode=pl.Buffered(1))


def _rms_fwd(x, g, sc, sh):
    r = lax.rsqrt(jnp.mean(x * x, axis=-1, keepdims=True) + EPS)
    xhat = x * r
    n = xhat * g
    return n * (1.0 + sc) + sh, xhat, r, n


def _rms_bwd(dh, xhat, r, n, g, sc):
    dsh = jnp.sum(dh, axis=0, keepdims=True)
    dsc = jnp.sum(dh * n, axis=0, keepdims=True)
    dn = dh * (1.0 + sc)
    dg = jnp.sum(dn * xhat, axis=0, keepdims=True)
    dxh = dn * g
    dx = r * (dxh - xhat * jnp.mean(dxh * xhat, axis=-1, keepdims=True))
    return dx, dsh, dsc, dg


_GELU_K = 0.7978845608028654
_GELU_C = 0.044715


def _gelu(x):
    t = jnp.tanh(_GELU_K * (x + _GELU_C * x * x * x))
    return 0.5 * x * (1.0 + t), t


def _gelu_grad(x, t):
    return 0.5 * (1.0 + t) + 0.5 * x * (1.0 - t * t) * (_GELU_K * (1.0 + 3.0 * _GELU_C * x * x))


def _neg_expm1(y, exp_y):
    series = -(y * (1.0 + y * (0.5 + y * (1.0 / 6.0))))
    return jnp.where(y > -(1.0 / 64.0), series, 1.0 - exp_y)


def _zero_first(ref):
    @pl.when(pl.program_id(0) == 0)
    def _():
        ref[...] = jnp.zeros_like(ref)


def _my_pos():
    return lax.axis_index("x"), lax.axis_index("y"), lax.axis_index("c")


def _dev_index(x, y, c):
    return 4 * x + 2 * y + c


def _chip_peers(x, y):
    return [(1 - x, y), (x, 1 - y), (1 - x, 1 - y)]


def _all_peers(x, y, c):
    return [(px, py, c) for (px, py) in _chip_peers(x, y)] + [(x, y, 1 - c)] + \
           [(px, py, 1 - c) for (px, py) in _chip_peers(x, y)]


def _comm_copies(x, y, c, gathers, scatters, sib, send, recv, loc):
    q = 2 * x + y
    peers = _chip_peers(x, y)
    remote, local = [], []
    s = 0
    for gi, (src, dst) in enumerate(gathers):
        local.append(pltpu.make_async_copy(src, dst.at[q], loc.at[gi]))
        for (px, py) in peers:
            st = pltpu.make_async_remote_copy(src, dst.at[q], send.at[s], recv.at[s],
                                              device_id=(px, py, c), device_id_type=MESH)
            wt = pltpu.make_async_remote_copy(src, dst.at[2 * px + py], send.at[s], recv.at[s],
                                              device_id=(px, py, c), device_id_type=MESH)
            remote.append((st, wt))
            s += 1
    for (src, dst) in scatters:
        for k, (px, py) in enumerate(peers):
            st = pltpu.make_async_remote_copy(src.at[2 * px + py], dst.at[k], send.at[s], recv.at[s],
                                              device_id=(px, py, c), device_id_type=MESH)
            remote.append((st, st))
            s += 1
    if sib is not None:
        src, dst, ranges = sib
        for (off, rows) in ranges:
            st = pltpu.make_async_remote_copy(src.at[pl.ds(off, rows)], dst.at[pl.ds(off, rows)], send.at[s], recv.at[s],
                                              device_id=(x, y, 1 - c), device_id_type=MESH)
            remote.append((st, st))
            s += 1
    return remote, local


def _comm_start(remote, local):
    for cp in local:
        cp.start()
    for st, _ in remote:
        st.start()


def _comm_wait(remote, local):
    for _, wt in remote:
        wt.wait_recv()
    for st, _ in remote:
        st.wait_send()
    for cp in local:
        cp.wait()


def _comm_shapes(gathers, scatters, sib):
    cin = list(gathers) + list(scatters) + ([sib[0], sib[1]] if sib else [])
    cout = [jax.ShapeDtypeStruct((NQ,) + g.shape, g.dtype) for g in gathers] + \
           [jax.ShapeDtypeStruct((3,) + s.shape[1:], s.dtype) for s in scatters] + \
           ([jax.ShapeDtypeStruct(sib[1].shape, sib[1].dtype)] if sib else [])
    n_rem = 3 * len(gathers) + 3 * len(scatters) + (len(sib[2]) if sib else 0)
    sems = [pltpu.SemaphoreType.DMA((max(n_rem, 1),)), pltpu.SemaphoreType.DMA((max(n_rem, 1),)),
            pltpu.SemaphoreType.DMA((max(len(gathers), 1),))]
    return cin, cout, sems


def _pcall(body, *, name, grid, in_specs, out_specs, out_shape, operands, scratch_shapes=(),
           gathers=(), scatters=(), sib=None):
    assert len(grid) == 1
    out_shape, out_specs = tuple(out_shape), tuple(out_specs)
    if not (gathers or scatters or sib):
        return pl.pallas_call(body, name=name, grid=grid, in_specs=list(in_specs), out_specs=out_specs,
                              out_shape=out_shape, scratch_shapes=list(scratch_shapes),
                              compiler_params=_cparams("arbitrary"))(*operands)
    cin, cout, sems = _comm_shapes(gathers, scatters, sib)
    n_in, n_cin, n_out, n_cout, n_scr = len(operands), len(cin), len(out_shape), len(cout), len(scratch_shapes)
    ng, ns = len(gathers), len(scatters)
    nsteps = grid[0]

    def wrapped(*refs):
        ins = refs[:n_in]
        cins = refs[n_in:n_in + n_cin]
        o0 = n_in + n_cin
        outs = refs[o0:o0 + n_out]
        couts = refs[o0 + n_out:o0 + n_out + n_cout]
        s0 = o0 + n_out + n_cout
        scr = refs[s0:s0 + n_scr]
        send, recv, loc = refs[s0 + n_scr:s0 + n_scr + 3]
        x, y, c = _my_pos()

        def copies():
            g = [(cins[k], couts[k]) for k in range(ng)]
            sc = [(cins[ng + k], couts[ng + k]) for k in range(ns)]
            sb = (cins[ng + ns], couts[ng + ns], sib[2]) if sib else None
            return _comm_copies(x, y, c, g, sc, sb, send, recv, loc)

        @pl.when(pl.program_id(0) == 0)
        def _():
            _comm_start(*copies())

        body(*ins, *outs, *scr)

        @pl.when(pl.program_id(0) == nsteps - 1)
        def _():
            _comm_wait(*copies())

    anyspec = pl.BlockSpec(memory_space=pl.ANY)
    aliases = {n_in + ng + ns + 1: n_out + ng + ns} if sib else {}
    return pl.pallas_call(
        wrapped, name=name, grid=grid,
        in_specs=list(in_specs) + [anyspec] * n_cin, out_specs=out_specs + (anyspec,) * n_cout,
        out_shape=out_shape + tuple(cout), scratch_shapes=list(scratch_shapes) + sems,
        input_output_aliases=aliases,
        compiler_params=pltpu.CompilerParams(dimension_semantics=("arbitrary",), vmem_limit_bytes=VMEM_LIMIT,
                                             has_side_effects=True),
    )(*operands, *cin)


def _comm_only(name, small=None, reduce_small=False, gathers=(), scatters=(), sib=None):
    cin, cout, sems = _comm_shapes(gathers, scatters, sib)
    n_cin, n_cout = len(cin), len(cout)
    ng, ns = len(gathers), len(scatters)
    n_sm_in = 1 if small is not None else 0
    n_sm_out = (2 if reduce_small else 1) if small is not None else 0

    def body(*refs):
        sm_in = refs[:n_sm_in]
        cins = refs[n_sm_in:n_sm_in + n_cin]
        o0 = n_sm_in + n_cin
        sm_out = refs[o0:o0 + n_sm_out]
        couts = refs[o0 + n_sm_out:o0 + n_sm_out + n_cout]
        s0 = o0 + n_sm_out + n_cout
        send, recv, loc = refs[s0:s0 + 3]
        x, y, c = _my_pos()
        g = [(cins[k], couts[k]) for k in range(ng)]
        sc = [(cins[ng + k], couts[ng + k]) for k in range(ns)]
        sb = (cins[ng + ns], couts[ng + ns], sib[2]) if sib else None
        remote, local = _comm_copies(x, y, c, g, sc, sb, send, recv, loc)
        _comm_start(remote, local)
        if small is not None:
            sm_send, sm_recv = refs[s0 + 3:s0 + 5]
            small_ref, sg_ref = sm_in[0], sm_out[0]
            me = _dev_index(x, y, c)
            sg_ref[me] = small_ref[...]
            peers = _all_peers(x, y, c)
            sm = [pltpu.make_async_remote_copy(small_ref, sg_ref.at[me], sm_send.at[k], sm_recv.at[k],
                                               device_id=peer, device_id_type=MESH) for k, peer in enumerate(peers)]
            for cp in sm:
                cp.start()
            for k, (px, py, pc) in enumerate(peers):
                pltpu.make_async_remote_copy(small_ref, sg_ref.at[_dev_index(px, py, pc)], sm_send.at[k], sm_recv.at[k],
                                             device_id=(px, py, pc), device_id_type=MESH).wait_recv()
            if reduce_small:
                acc = sg_ref[0]
                for d in range(1, NDEV):
                    acc = acc + sg_ref[d]
                sm_out[1][...] = acc
            for cp in sm:
                cp.wait_send()
        _comm_wait(remote, local)

    anyspec = pl.BlockSpec(memory_space=pl.ANY)
    vspec = pl.BlockSpec(memory_space=pltpu.VMEM)
    sm_shapes = []
    if small is not None:
        sm_shapes.append(jax.ShapeDtypeStruct((NDEV,) + small.shape, small.dtype))
        if reduce_small:
            sm_shapes.append(jax.ShapeDtypeStruct(small.shape, small.dtype))
        sems = sems + [pltpu.SemaphoreType.DMA((NDEV - 1,)), pltpu.SemaphoreType.DMA((NDEV - 1,))]
    aliases = {n_sm_in + ng + ns + 1: n_sm_out + ng + ns} if sib else {}
    return pl.pallas_call(
        body, name=name,
        in_specs=[vspec] * n_sm_in + [anyspec] * n_cin,
        out_specs=tuple([vspec] * n_sm_out + [anyspec] * n_cout),
        out_shape=tuple(sm_shapes + cout), scratch_shapes=sems, input_output_aliases=aliases,
        compiler_params=pltpu.CompilerParams(has_side_effects=True),
    )(*([small] if small is not None else []), *cin)


def _exchange_mod(modpart):
    _, L, Cs = modpart.shape

    def body(part_ref, out_ref, send, recv):
        x, y, c = _my_pos()
        q = 2 * x + y
        me = _dev_index(x, y, c)
        out_ref[q] = part_ref[me]
        sends = []
        for k, (px, py) in enumerate(_chip_peers(x, y)):
            cp = pltpu.make_async_remote_copy(part_ref.at[_dev_index(px, py, c)], out_ref.at[q], send.at[k], recv.at[k],
                                              device_id=(px, py, c), device_id_type=MESH)
            cp.start()
            sends.append(cp)
        for k, (px, py) in enumerate(_chip_peers(x, y)):
            pltpu.make_async_remote_copy(part_ref.at[me], out_ref.at[2 * px + py], send.at[k], recv.at[k],
                                         device_id=(px, py, c), device_id_type=MESH).wait_recv()
        for cp in sends:
            cp.wait_send()

    return pl.pallas_call(
        body, name="exchange_mod",
        out_shape=jax.ShapeDtypeStruct((NQ, L, Cs), modpart.dtype),
        in_specs=[pl.BlockSpec(memory_space=pltpu.VMEM)],
        out_specs=pl.BlockSpec(memory_space=pltpu.VMEM),
        scratch_shapes=[pltpu.SemaphoreType.DMA((3,)), pltpu.SemaphoreType.DMA((3,))],
        compiler_params=pltpu.CompilerParams(has_side_effects=True),
    )(modpart)


def _mod_part(c_all, w_mod, b_mod_sh):
    L, D, Cs = w_mod.shape
    tn = 512 if Cs % 512 == 0 else Cs

    def body(c_ref, w_ref, b_ref, o_ref):
        cv = c_ref[...]
        cond = cv * jax.nn.sigmoid(cv)
        o_ref[...] = jnp.dot(cond, w_ref[...], preferred_element_type=F32, precision=lax.Precision.HIGHEST) + b_ref[...]

    return pl.pallas_call(
        body, name="mod_part", grid=(L, Cs // tn),
        out_shape=jax.ShapeDtypeStruct((L, NDEV, Cs), F32),
        in_specs=[pl.BlockSpec((NDEV, D), lambda i, j: (0, 0)),
                  pl.BlockSpec((None, D, tn), lambda i, j: (i, 0, j)),
                  pl.BlockSpec((None, 1, tn), lambda i, j: (i, 0, j))],
        out_specs=pl.BlockSpec((None, NDEV, tn), lambda i, j: (i, 0, j)),
        compiler_params=_cparams("parallel", "parallel"),
    )(c_all, w_mod, b_mod_sh)


def _adam(w, g, m, v):
    m2 = ADAM_B1 * m + (1.0 - ADAM_B1) * g
    v2 = ADAM_B2 * v + (1.0 - ADAM_B2) * (g * g)
    m_hat = m2 / (1.0 - ADAM_B1 ** ADAM_STEP)
    v_hat = v2 / (1.0 - ADAM_B2 ** ADAM_STEP)
    delta = -ADAM_LR * (m_hat / (jnp.sqrt(v_hat) + ADAM_EPS) + ADAM_WD * w)
    return delta, m2, v2


def _wmod_update(c_all_t, dmod_sh, w, m, v):
    L, D, Cs = w.shape
    td = 256 if D % 256 == 0 else D

    def body(ct_ref, d_ref, w_ref, m_ref, v_ref, g_ref, dl_ref, m2_ref, v2_ref):
        cv = ct_ref[...]
        cond = cv * jax.nn.sigmoid(cv)
        g = cond[:, 0:1] * d_ref[0:1, :]
        for b in range(1, NDEV):
            g = g + cond[:, b:b + 1] * d_ref[b:b + 1, :]
        g_ref[...] = g
        dl_ref[...], m2_ref[...], v2_ref[...] = _adam(w_ref[...], g, m_ref[...], v_ref[...])

    blk = pl.BlockSpec((None, td, Cs), lambda i, j: (i, j, 0))
    out = jax.ShapeDtypeStruct((L, D, Cs), F32)
    return pl.pallas_call(
        body, name="wmod_update", grid=(L, D // td),
        out_shape=(out, out, out, out),
        in_specs=[pl.BlockSpec((td, NDEV), lambda i, j: (j, 0)),
                  pl.BlockSpec((None, NDEV, Cs), lambda i, j: (i, 0, 0)), blk, blk, blk],
        out_specs=(blk, blk, blk, blk),
        compiler_params=_cparams("parallel", "parallel"),
    )(c_all_t, dmod_sh, w, m, v)


def _adam_rows(w, m, v, pa, pb, row_off, name):
    rows, C = w.shape
    tr = 512 if rows % 512 == 0 else (128 if rows % 128 == 0 else rows)
    assert row_off % tr == 0
    ob = row_off // tr

    def body(w_ref, m_ref, v_ref, pa_ref, pb_ref, g_ref, dl_ref, m2_ref, v2_ref):
        g = pa_ref[...] + pb_ref[...]
        g_ref[...] = g
        dl_ref[...], m2_ref[...], v2_ref[...] = _adam(w_ref[...], g, m_ref[...], v_ref[...])

    blk = pl.BlockSpec((tr, C), lambda i: (i, 0))
    pblk = pl.BlockSpec((tr, C), lambda i: (ob + i, 0))
    out = jax.ShapeDtypeStruct((rows, C), F32)
    return pl.pallas_call(
        body, name=name, grid=(rows // tr,), out_shape=(out, out, out, out),
        in_specs=[blk, blk, blk, pblk, pblk], out_specs=(blk, blk, blk, blk),
        compiler_params=_cparams("parallel"),
    )(w, m, v, pa, pb)


def _adam_small(w, g, m, v):
    def body(w_ref, g_ref, m_ref, v_ref, dl_ref, m2_ref, v2_ref):
        dl_ref[...], m2_ref[...], v2_ref[...] = _adam(w_ref[...], g_ref[...], m_ref[...], v_ref[...])

    out = jax.ShapeDtypeStruct(w.shape, F32)
    return pl.pallas_call(body, name="adam_small", out_shape=(out, out, out))(w, g, m, v)


def _sum_into(ppack, own, rb, off):
    rows, D = own.shape
    tr = 256 if rows % 256 == 0 else 128
    assert rows % tr == 0 and off % tr == 0
    ob = off // tr

    def body(o_ref, r_ref, pin_ref, p_ref):
        acc = o_ref[...].astype(F32)
        for k in range(3):
            acc = acc + r_ref[k].astype(F32)
        p_ref[...] = acc

    return pl.pallas_call(
        body, name="sum_partials", grid=(rows // tr,), out_shape=jax.ShapeDtypeStruct(ppack.shape, ppack.dtype),
        in_specs=[pl.BlockSpec((tr, D), lambda i: (i, 0)), pl.BlockSpec((3, tr, D), lambda i: (0, i, 0)),
                  pl.BlockSpec(memory_space=pl.ANY)],
        out_specs=pl.BlockSpec((tr, D), lambda i: (ob + i, 0)),
        input_output_aliases={2: 0},
        compiler_params=_cparams("parallel"),
    )(own, rb, ppack)


def _wspec(g):
    return _resident(g.shape, lambda i: (0, 0, 0))


def _ffn_fwd_inner(x1, mod_ref, gf_ref, w1_ref, w2_ref, h2_ref, a_ref, z_ref, x2_ref):
    h2 = _rms_fwd(x1, gf_ref[...], mod_ref[4:5, :], mod_ref[3:4, :])[0]
    h2b = h2.astype(BF16)
    h2_ref[...] = h2b
    f4 = w1_ref.shape[2]
    z = jnp.zeros(x1.shape, F32)
    for q in range(NQ):
        a = jnp.maximum(_dot(h2b, w1_ref[q]), 0.0)
        a_ref[:, q * f4:(q + 1) * f4] = a.astype(BF16)
        z = z + _dot((a * a).astype(BF16), w2_ref[q])
    z_ref[...] = z.astype(BF16)
    x2_ref[...] = x1 + mod_ref[5:6, :] * z


def _lru_in_fwd(x, mod_l, g_mix, g_wy, g_win, b_y, b_in, **comm):
    S, D = x.shape
    tm = min(TM, S)

    def body(x_ref, mod_ref, g_ref, wy_ref, win_ref, by_ref, bin_ref, h_ref, gb_ref, xr_ref):
        h = _rms_fwd(x_ref[...], g_ref[...], mod_ref[1:2, :], mod_ref[0:1, :])[0]
        hb = h.astype(BF16)
        h_ref[...] = hb
        gb_ref[...] = _dot(hb, wy_ref[...].reshape(D, D)) + by_ref[...]
        xr_ref[...] = _dot(hb, win_ref[...].reshape(D, D)) + bin_ref[...]

    tile = pl.BlockSpec((tm, D), lambda i: (i, 0))
    row = pl.BlockSpec((1, D), lambda i: (0, 0))
    return _pcall(
        body, name="lru_in_fwd", grid=(S // tm,),
        out_shape=(jax.ShapeDtypeStruct((S, D), BF16), jax.ShapeDtypeStruct((S, D), F32), jax.ShapeDtypeStruct((S, D), F32)),
        in_specs=[tile, pl.BlockSpec((8, D), lambda i: (0, 0)), row, _wspec(g_wy), _wspec(g_win), row, row],
        out_specs=(tile, tile, tile),
        operands=(x, mod_l, g_mix, g_wy, g_win, b_y, b_in), **comm)


def _heads_dot(xb, w_ref, hd, nt=False):
    outs = []
    for h in range(HEADS):
        xs = xb[:, h * hd:(h + 1) * hd]
        outs.append(_dot_nt(xs, w_ref[h]) if nt else _dot(xs, w_ref[h]))
    return jnp.concatenate(outs, axis=1)


def _lru_gates(xc, wa_ref, ba, wx_ref, bx, lam, hd):
    xcb = xc.astype(BF16)
    gate_r = jax.nn.sigmoid(_heads_dot(xcb, wa_ref, hd) + ba)
    gate_i = jax.nn.sigmoid(_heads_dot(xcb, wx_ref, hd) + bx)
    ls = jax.nn.log_sigmoid(lam)
    log_a = LRU_C * gate_r * ls
    a = jnp.exp(log_a)
    mult = jnp.sqrt(_neg_expm1(2.0 * log_a, a * a))
    return xcb, gate_r, gate_i, ls, a, mult


def _conv_taps(xext, cw, tt):
    acc = cw[0:1, :] * xext[pl.ds(8 - (CONV_W - 1), tt), :]
    for k in range(1, CONV_W):
        acc = acc + cw[k:k + 1, :] * xext[pl.ds(8 - (CONV_W - 1) + k, tt), :]
    return acc


def _lru_scan_fwd(xr0, gb, cw, cb, wa, ba, wx, bx, lam, **comm):
    S, W = xr0.shape
    tt = min(TT, S)
    hd = W // HEADS

    def body(xr_ref, xrh_ref, gb_ref, cw_ref, cb_ref, wa_ref, ba_ref, wx_ref, bx_ref, lam_ref,
             hs_ref, p_ref, xext, a_s, u_s, carry):
        i = pl.program_id(0)

        @pl.when(i == 0)
        def _():
            carry[...] = jnp.zeros_like(carry)

        xext[0:8, :] = jnp.where(i > 0, xrh_ref[...], 0.0)
        xext[pl.ds(8, tt), :] = xr_ref[...]
        xc = _conv_taps(xext, cw_ref[...], tt) + cb_ref[...]
        _, _, gate_i, _, a, mult = _lru_gates(xc, wa_ref, ba_ref[...], wx_ref, bx_ref[...], lam_ref[...], hd)
        a_s[...] = a
        u_s[...] = mult * (gate_i * xc)
        row = lax.broadcasted_iota(jnp.int32, (8, W), 0)

        def step(k, _):
            off = pl.multiple_of(k * 8, 8)
            A = a_s[pl.ds(off, 8), :]
            U = u_s[pl.ds(off, 8), :]
            for d in (1, 2, 4):
                keep = row >= d
                Us = jnp.where(keep, pltpu.roll(U, d, 0), 0.0)
                As = jnp.where(keep, pltpu.roll(A, d, 0), 1.0)
                U = U + A * Us
                A = A * As
            H = U + A * carry[...]
            hs_ref[pl.ds(off, 8), :] = H
            carry[...] = jnp.broadcast_to(H[7:8, :], (8, W))
            return 0

        lax.fori_loop(0, tt // 8, step, 0)
        p_ref[...] = (hs_ref[...] * _gelu(gb_ref[...])[0]).astype(BF16)

    tile = pl.BlockSpec((tt, W), lambda i: (i, 0))
    halo = pl.BlockSpec((8, W), lambda i: (jnp.maximum(i * (tt // 8) - 1, 0), 0))
    row = pl.BlockSpec((1, W), lambda i: (0, 0))
    wblk = pl.BlockSpec((HEADS, hd, hd), lambda i: (0, 0, 0))
    return _pcall(
        body, name="lru_scan_fwd", grid=(S // tt,),
        out_shape=(jax.ShapeDtypeStruct((S, W), F32), jax.ShapeDtypeStruct((S, W), BF16)),
        in_specs=[tile, halo, tile, pl.BlockSpec((CONV_W, W), lambda i: (0, 0)), row, wblk, row, wblk, row, row],
        out_specs=(tile, tile),
        scratch_shapes=[pltpu.VMEM((tt + 8, W), F32), pltpu.VMEM((tt, W), F32), pltpu.VMEM((tt, W), F32),
                        pltpu.VMEM((8, W), F32)],
        operands=(xr0, xr0, gb, cw, cb, wa, ba, wx, bx, lam), **comm)


def _ffn_out_shapes(S, D, F):
    return (jax.ShapeDtypeStruct((S, D), F32), jax.ShapeDtypeStruct((S, D), BF16), jax.ShapeDtypeStruct((S, F), BF16),
            jax.ShapeDtypeStruct((S, D), BF16), jax.ShapeDtypeStruct((S, D), F32))


def _lru_mix_ffn_fwd(x, p, mod_l, g_ffn, b_out, g_wout, g_w1, g_w2, **comm):
    S, D = x.shape
    tm = min(TM, S)
    F = g_w1.shape[2] * NQ

    def body(x_ref, p_ref, mod_ref, gf_ref, bo_ref, wo_ref, w1_ref, w2_ref,
             y_ref, x1_ref, h2_ref, a_ref, z_ref, x2_ref):
        y = _dot(p_ref[...], wo_ref[...].reshape(D, D)) + bo_ref[...]
        y_ref[...] = y.astype(BF16)
        x1 = x_ref[...] + mod_ref[2:3, :] * y
        x1_ref[...] = x1
        _ffn_fwd_inner(x1, mod_ref, gf_ref, w1_ref, w2_ref, h2_ref, a_ref, z_ref, x2_ref)

    tile = pl.BlockSpec((tm, D), lambda i: (i, 0))
    row = pl.BlockSpec((1, D), lambda i: (0, 0))
    return _pcall(
        body, name="lru_mix_ffn_fwd", grid=(S // tm,),
        out_shape=(jax.ShapeDtypeStruct((S, D), BF16),) + _ffn_out_shapes(S, D, F),
        in_specs=[tile, tile, pl.BlockSpec((8, D), lambda i: (0, 0)), row, row,
                  _wspec(g_wout), _wspec(g_w1), _wspec(g_w2)],
        out_specs=(tile, tile, tile, pl.BlockSpec((tm, F), lambda i: (i, 0)), tile, tile),
        operands=(x, p, mod_l, g_ffn, b_out, g_wout, g_w1, g_w2), **comm)


def _window_vec(D):
    gd = D // len(POOL_WINDOWS)
    lane = lax.broadcasted_iota(jnp.int32, (1, D), 1)
    w = jnp.full((1, D), float(POOL_WINDOWS[0]), F32)
    for g in range(1, len(POOL_WINDOWS)):
        w = jnp.where(lane >= g * gd, float(POOL_WINDOWS[g]), w)
    return w


def _pool_mix_ffn_fwd(x, mod_l, g_mix, pw, ps, g_ffn, g_w1, g_w2, **comm):
    S, D = x.shape
    tm = min(TP, S)
    F = g_w1.shape[2] * NQ
    gd = D // len(POOL_WINDOWS)
    n = tm + 24

    def body(x_ref, xh_ref, mod_ref, gm_ref, pw_ref, ps_ref, gf_ref, w1_ref, w2_ref,
             pl_ref, x1_ref, h2_ref, a_ref, z_ref, x2_ref, ext, b1, b2):
        i = pl.program_id(0)
        g, sc, sh = gm_ref[...], mod_ref[1:2, :], mod_ref[0:1, :]
        xv = x_ref[...]
        h = _rms_fwd(xv, g, sc, sh)[0]
        hh = _rms_fwd(xh_ref[...], g, sc, sh)[0]
        zeros8 = jnp.zeros((8, D), F32)
        ext[0:8, :] = zeros8
        b1[0:8, :] = zeros8
        b2[0:8, :] = zeros8
        ext[8:24, :] = jnp.where(i > 0, hh, 0.0)
        ext[pl.ds(24, tm), :] = h
        m = n - 8
        b1[pl.ds(8, m), :] = ext[pl.ds(8, m), :] + ext[pl.ds(7, m), :]
        b2[pl.ds(8, m), gd:] = b1[pl.ds(8, m), gd:] + b1[pl.ds(6, m), gd:]
        b1[pl.ds(8, m), 2 * gd:] = b2[pl.ds(8, m), 2 * gd:] + b2[pl.ds(4, m), 2 * gd:]
        b2[pl.ds(8, m), 3 * gd:] = b1[pl.ds(8, m), 3 * gd:] + b1[pl.ds(0, m), 3 * gd:]
        wsum = jnp.concatenate([b1[pl.ds(24, tm), 0:gd], b2[pl.ds(24, tm), gd:2 * gd],
                                b1[pl.ds(24, tm), 2 * gd:3 * gd], b2[pl.ds(24, tm), 3 * gd:]], axis=1)
        t1 = (lax.broadcasted_iota(jnp.int32, (tm, 1), 0) + (i * tm + 1)).astype(F32)
        cnt = jnp.minimum(t1, _window_vec(D))
        pooled = (wsum / cnt - h).astype(BF16)
        pl_ref[...] = pooled
        y = _heads_dot(pooled, pw_ref, gd) * ps_ref[...]
        x1 = xv + mod_ref[2:3, :] * y
        x1_ref[...] = x1
        _ffn_fwd_inner(x1, mod_ref, gf_ref, w1_ref, w2_ref, h2_ref, a_ref, z_ref, x2_ref)

    tile = pl.BlockSpec((tm, D), lambda i: (i, 0))
    halo = pl.BlockSpec((16, D), lambda i: (jnp.maximum(i * (tm // 16) - 1, 0), 0))
    row = pl.BlockSpec((1, D), lambda i: (0, 0))
    return _pcall(
        body, name="pool_mix_ffn_fwd", grid=(S // tm,),
        out_shape=(jax.ShapeDtypeStruct((S, D), BF16),) + _ffn_out_shapes(S, D, F),
        in_specs=[tile, halo, pl.BlockSpec((8, D), lambda i: (0, 0)), row,
                  pl.BlockSpec((len(POOL_WINDOWS), gd, gd), lambda i: (0, 0, 0)), row, row,
                  _wspec(g_w1), _wspec(g_w2)],
        out_specs=(tile, tile, tile, pl.BlockSpec((tm, F), lambda i: (i, 0)), tile, tile),
        scratch_shapes=[pltpu.VMEM((n, D), F32), pltpu.VMEM((n, D), F32), pltpu.VMEM((n, D), F32)],
        operands=(x, x, mod_l, g_mix, pw, ps, g_ffn, g_w1, g_w2), **comm)


def _final_loss_bwd(x, g, target):
    S, D = x.shape
    tm = min(TM, S)

    def body(x_ref, g_ref, t_ref, dx_ref, acc_ref):
        _zero_first(acc_ref)
        xv = x_ref[...]
        gv = g_ref[...]
        r = lax.rsqrt(jnp.mean(xv * xv, axis=-1, keepdims=True) + EPS)
        xhat = xv * r
        err = xhat * gv - t_ref[...]
        acc_ref[0:1, :] += jnp.sum(err * err, axis=0, keepdims=True)
        dy = err * (1.0 / D)
        acc_ref[1:2, :] += jnp.sum(dy * xhat, axis=0, keepdims=True)
        dxh = dy * gv
        dx_ref[...] = r * (dxh - xhat * jnp.mean(dxh * xhat, axis=-1, keepdims=True))

    tile = pl.BlockSpec((tm, D), lambda i: (i, 0))
    return pl.pallas_call(
        body, name="final_loss_bwd", grid=(S // tm,),
        out_shape=(jax.ShapeDtypeStruct((S, D), F32), jax.ShapeDtypeStruct((8, D), F32)),
        in_specs=[tile, pl.BlockSpec((1, D), lambda i: (0, 0)), tile],
        out_specs=(tile, pl.BlockSpec((8, D), lambda i: (0, 0))),
        compiler_params=_cparams("arbitrary"),
    )(x, g, target)


def _ffn_bwd(dx2, x1, a, z, mod_l, g_ffn, g_w1, g_w2, **comm):
    S, D = dx2.shape
    F = a.shape[1]
    f4 = F // NQ
    tm = min(TM, S)

    def body(dx2_ref, x1_ref, a_ref, z_ref, mod_ref, gf_ref, w1_ref, w2_ref, dx1_ref, du_ref, dz_ref, acc_ref):
        _zero_first(acc_ref)
        dx2v = dx2_ref[...]
        acc_ref[5:6, :] += jnp.sum(dx2v * z_ref[...].astype(F32), axis=0, keepdims=True)
        dzb = (dx2v * mod_ref[5:6, :]).astype(BF16)
        dz_ref[...] = dzb
        dh2 = jnp.zeros((tm, D), F32)
        for q in range(NQ):
            av = a_ref[:, q * f4:(q + 1) * f4].astype(F32)
            du = (_dot_nt(dzb, w2_ref[q]) * (2.0 * av)).astype(BF16)
            du_ref[:, q * f4:(q + 1) * f4] = du
            dh2 = dh2 + _dot_nt(du, w1_ref[q])
        g, sc = gf_ref[...], mod_ref[4:5, :]
        _, xhat, r, n = _rms_fwd(x1_ref[...], g, sc, mod_ref[3:4, :])
        dx, dsh, dsc, dg = _rms_bwd(dh2, xhat, r, n, g, sc)
        acc_ref[3:4, :] += dsh
        acc_ref[4:5, :] += dsc
        acc_ref[7:8, :] += dg
        dx1_ref[...] = dx2v + dx

    tile = pl.BlockSpec((tm, D), lambda i: (i, 0))
    wide = pl.BlockSpec((tm, F), lambda i: (i, 0))
    return _pcall(
        body, name="ffn_bwd", grid=(S // tm,),
        out_shape=(jax.ShapeDtypeStruct((S, D), F32), jax.ShapeDtypeStruct((S, F), BF16),
                   jax.ShapeDtypeStruct((S, D), BF16), jax.ShapeDtypeStruct((8, D), F32)),
        in_specs=[tile, tile, wide, tile, pl.BlockSpec((8, D), lambda i: (0, 0)), pl.BlockSpec((1, D), lambda i: (0, 0)),
                  _wspec(g_w1), _wspec(g_w2)],
        out_specs=(tile, wide, tile, pl.BlockSpec((8, D), lambda i: (0, 0))),
        operands=(dx2, x1, a, z, mod_l, g_ffn, g_w1, g_w2), **comm)


def _dw_blocked(a, b, by_rows, square_a, name):
    S = a.shape[0]
    tk = min(TK, S)
    nk = S // tk
    if by_rows:
        bm, bn = a.shape[1] // NQ, b.shape[1]
        a_map, b_map = (lambda q, k: (k, q)), (lambda q, k: (k, 0))
    else:
        bm, bn = a.shape[1], b.shape[1] // NQ
        a_map, b_map = (lambda q, k: (k, 0)), (lambda q, k: (k, q))

    def body(a_ref, b_ref, o_ref, acc):
        k = pl.program_id(1)

        @pl.when(k == 0)
        def _():
            acc[...] = jnp.zeros_like(acc)

        av = a_ref[...]
        if square_a:
            av = av * av
        acc[...] += _dot_tn(av, b_ref[...])

        @pl.when(k == nk - 1)
        def _():
            o_ref[...] = acc[...].astype(o_ref.dtype)

    return pl.pallas_call(
        body, name=name, grid=(NQ, nk),
        out_shape=jax.ShapeDtypeStruct((NQ, bm, bn), BF16),
        in_specs=[pl.BlockSpec((tk, bm), a_map), pl.BlockSpec((tk, bn), b_map)],
        out_specs=pl.BlockSpec((None, bm, bn), lambda q, k: (q, 0, 0)),
        scratch_shapes=[pltpu.VMEM((bm, bn), F32)],
        compiler_params=_cparams("parallel", "arbitrary"),
    )(a, b)


def _dw_whole(a, bs, name):
    S, M = a.shape
    N = bs[0].shape[1]
    tk = min(TK, S)
    nk = S // tk
    nb = len(bs)

    def body(*refs):
        a_ref, b_refs, o_refs, accs = refs[0], refs[1:1 + nb], refs[1 + nb:1 + 2 * nb], refs[1 + 2 * nb:]
        k = pl.program_id(0)

        @pl.when(k == 0)
        def _():
            for acc in accs:
                acc[...] = jnp.zeros_like(acc)

        av = a_ref[...]
        for b_ref, acc in zip(b_refs, accs):
            acc[...] += _dot_tn(av, b_ref[...])

        @pl.when(k == nk - 1)
        def _():
            for o_ref, acc in zip(o_refs, accs):
                o_ref[...] = acc[...].reshape(NQ, M // NQ, N).astype(o_ref.dtype)

    return pl.pallas_call(
        body, name=name, grid=(nk,),
        out_shape=tuple(jax.ShapeDtypeStruct((NQ, M // NQ, N), BF16) for _ in bs),
        in_specs=[pl.BlockSpec((tk, M), lambda k: (k, 0))] + [pl.BlockSpec((tk, N), lambda k: (k, 0)) for _ in bs],
        out_specs=tuple(pl.BlockSpec((NQ, M // NQ, N), lambda k: (0, 0, 0)) for _ in bs),
        scratch_shapes=[pltpu.VMEM((M, N), F32) for _ in bs],
        compiler_params=_cparams("arbitrary"),
    )(a, *bs)


def _lru_out_bwd(dx1, y, mod_l, g_wout):
    S, D = dx1.shape
    tm = min(TM, S)

    def body(dx1_ref, y_ref, mod_ref, wo_ref, dy_ref, dp_ref, acc_ref):
        _zero_first(acc_ref)
        dx1v = dx1_ref[...]
        acc_ref[2:3, :] += jnp.sum(dx1v * y_ref[...].astype(F32), axis=0, keepdims=True)
        dy = dx1v * mod_ref[2:3, :]
        acc_ref[3:4, :] += jnp.sum(dy, axis=0, keepdims=True)
        dyb = dy.astype(BF16)
        dy_ref[...] = dyb
        dp_ref[...] = _dot_nt(dyb, wo_ref[...].reshape(D, D))

    tile = pl.BlockSpec((tm, D), lambda i: (i, 0))
    return _pcall(
        body, name="lru_out_bwd", grid=(S // tm,),
        out_shape=(jax.ShapeDtypeStruct((S, D), BF16), jax.ShapeDtypeStruct((S, D), F32), jax.ShapeDtypeStruct((8, D), F32)),
        in_specs=[tile, tile, pl.BlockSpec((8, D), lambda i: (0, 0)), _wspec(g_wout)],
        out_specs=(tile, tile, pl.BlockSpec((8, D), lambda i: (0, 0))),
        operands=(dx1, y, mod_l, g_wout))


def _lru_scan_bwd(dp, xr0, gb, hs, cw, cb, wa, ba, wx, bx, lam, **comm):
    S, W = xr0.shape
    tt = min(TT, S)
    nb = S // tt
    hd = W // HEADS

    def body(dp_ref, xr_ref, xrh_ref, gb_ref, hs_ref, hsh_ref, cw_ref, cb_ref, wa_ref, ba_ref, wx_ref, bx_ref, lam_ref,
             dgb_ref, dxr_ref, sm_ref, dwa_ref, dwx_ref, xext, hext, qext, dext, a_s, b_s, qc, dc):
        i = pl.program_id(0)
        blk = nb - 1 - i

        @pl.when(i == 0)
        def _():
            sm_ref[...] = jnp.zeros_like(sm_ref)
            dwa_ref[...] = jnp.zeros_like(dwa_ref)
            dwx_ref[...] = jnp.zeros_like(dwx_ref)
            qc[...] = jnp.zeros_like(qc)
            dc[...] = jnp.zeros_like(dc)

        xext[0:8, :] = jnp.where(blk > 0, xrh_ref[...], 0.0)
        xext[pl.ds(8, tt), :] = xr_ref[...]
        hext[0:8, :] = jnp.where(blk > 0, hsh_ref[...], 0.0)
        hext[pl.ds(8, tt), :] = hs_ref[...]
        cw = cw_ref[...]
        lam = lam_ref[...]
        xc = _conv_taps(xext, cw, tt) + cb_ref[...]
        xcb, gate_r, gate_i, ls, a, mult = _lru_gates(xc, wa_ref, ba_ref[...], wx_ref, bx_ref[...], lam, hd)

        gbv = gb_ref[...]
        gate, th = _gelu(gbv)
        dpv = dp_ref[...]
        dgb = dpv * hs_ref[...] * _gelu_grad(gbv, th)
        dgb_ref[...] = dgb.astype(BF16)
        sm_ref[9:10, :] += jnp.sum(dgb, axis=0, keepdims=True)
        dhs = dpv * gate

        a_s[...] = a
        b_s[...] = a * dhs
        qext[pl.ds(tt, 8), :] = qc[...]
        row = lax.broadcasted_iota(jnp.int32, (8, W), 0)

        def step(k, _):
            off = pl.multiple_of((tt // 8 - 1 - k) * 8, 8)
            A = a_s[pl.ds(off, 8), :]
            B = b_s[pl.ds(off, 8), :]
            for d in (1, 2, 4):
                keep = row < 8 - d
                Bs = jnp.where(keep, pltpu.roll(B, 8 - d, 0), 0.0)
                As = jnp.where(keep, pltpu.roll(A, 8 - d, 0), 1.0)
                B = B + A * Bs
                A = A * As
            Q = B + A * qc[...]
            qext[pl.ds(off, 8), :] = Q
            qc[...] = jnp.broadcast_to(Q[0:1, :], (8, W))
            return 0

        lax.fori_loop(0, tt // 8, step, 0)
        gsc = dhs + qext[pl.ds(1, tt), :]
        da = gsc * hext[pl.ds(7, tt), :]
        t1 = gsc * xc
        dmult = t1 * gate_i
        dgate_i = t1 * mult
        dxc = gsc * (mult * gate_i)
        dlog_a = da * a - dmult * (a * a) / mult
        dgate_r = dlog_a * (LRU_C * ls)
        sm_ref[7:8, :] += jnp.sum(dlog_a * (LRU_C * gate_r), axis=0, keepdims=True)
        dga = dgate_r * gate_r * (1.0 - gate_r)
        dgx = dgate_i * gate_i * (1.0 - gate_i)
        sm_ref[5:6, :] += jnp.sum(dga, axis=0, keepdims=True)
        sm_ref[6:7, :] += jnp.sum(dgx, axis=0, keepdims=True)
        dgab = dga.astype(BF16)
        dgxb = dgx.astype(BF16)
        dxc = dxc + _heads_dot(dgab, wa_ref, hd, nt=True) + _heads_dot(dgxb, wx_ref, hd, nt=True)
        for h in range(HEADS):
            sl = slice(h * hd, (h + 1) * hd)
            dwa_ref[h] += _dot_tn(xcb[:, sl], dgab[:, sl])
            dwx_ref[h] += _dot_tn(xcb[:, sl], dgxb[:, sl])
        sm_ref[4:5, :] += jnp.sum(dxc, axis=0, keepdims=True)
        for k in range(CONV_W):
            sm_ref[k:k + 1, :] += jnp.sum(dxc * xext[pl.ds(8 - (CONV_W - 1) + k, tt), :], axis=0, keepdims=True)
        dext[pl.ds(0, tt), :] = dxc
        dext[pl.ds(tt, 8), :] = dc[...]
        dxr = cw[0:1, :] * dext[pl.ds(CONV_W - 1, tt), :]
        for k in range(1, CONV_W):
            dxr = dxr + cw[k:k + 1, :] * dext[pl.ds(CONV_W - 1 - k, tt), :]
        dc[...] = dext[0:8, :]
        sm_ref[8:9, :] += jnp.sum(dxr, axis=0, keepdims=True)
        dxr_ref[...] = dxr.astype(BF16)

        @pl.when(i == nb - 1)
        def _():
            sm_ref[7:8, :] = sm_ref[7:8, :] * jax.nn.sigmoid(-lam)

    rev = lambda i: (nb - 1 - i, 0)
    tile = pl.BlockSpec((tt, W), rev)
    halo = pl.BlockSpec((8, W), lambda i: (jnp.maximum((nb - 1 - i) * (tt // 8) - 1, 0), 0))
    row = pl.BlockSpec((1, W), lambda i: (0, 0))
    wblk = pl.BlockSpec((HEADS, hd, hd), lambda i: (0, 0, 0))
    return _pcall(
        body, name="lru_scan_bwd", grid=(nb,),
        out_shape=(jax.ShapeDtypeStruct((S, W), BF16), jax.ShapeDtypeStruct((S, W), BF16),
                   jax.ShapeDtypeStruct((16, W), F32), jax.ShapeDtypeStruct((HEADS, hd, hd), F32),
                   jax.ShapeDtypeStruct((HEADS, hd, hd), F32)),
        in_specs=[tile, tile, halo, tile, tile, halo, pl.BlockSpec((CONV_W, W), lambda i: (0, 0)), row,
                  wblk, row, wblk, row, row],
        out_specs=(tile, tile, pl.BlockSpec((16, W), lambda i: (0, 0)), wblk, wblk),
        scratch_shapes=[pltpu.VMEM((tt + 8, W), F32), pltpu.VMEM((tt + 8, W), F32), pltpu.VMEM((tt + 8, W), F32),
                        pltpu.VMEM((tt + 8, W), F32), pltpu.VMEM((tt, W), F32), pltpu.VMEM((tt, W), F32),
                        pltpu.VMEM((8, W), F32), pltpu.VMEM((8, W), F32)],
        operands=(dp, xr0, xr0, gb, hs, hs, cw, cb, wa, ba, wx, bx, lam), **comm)


def _lru_in_bwd(dxr, dgb, dx1, x, mod_l, g_mix, g_wy, g_win, **comm):
    S, D = x.shape
    tm = min(TM, S)

    def body(dxr_ref, dgb_ref, dx1_ref, x_ref, mod_ref, g_ref, wy_ref, win_ref, dx_ref, acc_ref):
        _zero_first(acc_ref)
        dh = _dot_nt(dxr_ref[...], win_ref[...].reshape(D, D)) + _dot_nt(dgb_ref[...], wy_ref[...].reshape(D, D))
        g, sc = g_ref[...], mod_ref[1:2, :]
        _, xhat, r, n = _rms_fwd(x_ref[...], g, sc, mod_ref[0:1, :])
        dx, dsh, dsc, dg = _rms_bwd(dh, xhat, r, n, g, sc)
        acc_ref[0:1, :] += dsh
        acc_ref[1:2, :] += dsc
        acc_ref[6:7, :] += dg
        dx_ref[...] = dx1_ref[...] + dx

    tile = pl.BlockSpec((tm, D), lambda i: (i, 0))
    return _pcall(
        body, name="lru_in_bwd", grid=(S // tm,),
        out_shape=(jax.ShapeDtypeStruct((S, D), F32), jax.ShapeDtypeStruct((8, D), F32)),
        in_specs=[tile, tile, tile, tile, pl.BlockSpec((8, D), lambda i: (0, 0)), pl.BlockSpec((1, D), lambda i: (0, 0)),
                  _wspec(g_wy), _wspec(g_win)],
        out_specs=(tile, pl.BlockSpec((8, D), lambda i: (0, 0))),
        operands=(dxr, dgb, dx1, x, mod_l, g_mix, g_wy, g_win), **comm)


def _pool_bwd(dx1, x, pooled, mod_l, g_mix, pw, ps):
    S, D = x.shape
    tm = min(TP, S)
    nb = S // tm
    ng = len(POOL_WINDOWS)
    gd = D // ng
    n = tm + 24

    def body(dx1_ref, dxh_ref, x_ref, pl_ref, mod_ref, gm_ref, pw_ref, ps_ref, dx_ref, acc_ref, dpw_ref, ext, b1, b2):
        i = pl.program_id(0)

        @pl.when(i == 0)
        def _():
            acc_ref[...] = jnp.zeros_like(acc_ref)
            dpw_ref[...] = jnp.zeros_like(dpw_ref)

        gt, psv = mod_ref[2:3, :], ps_ref[...]
        wvec = _window_vec(D)
        dx1v = dx1_ref[...]
        pooled = pl_ref[...]
        mixed = _heads_dot(pooled, pw_ref, gd)
        acc_ref[2:3, :] += jnp.sum(dx1v * (mixed * psv), axis=0, keepdims=True)
        dy = dx1v * gt
        acc_ref[3:4, :] += jnp.sum(dy * mixed, axis=0, keepdims=True)
        dmix = (dy * psv).astype(BF16)
        for gi in range(ng):
            sl = slice(gi * gd, (gi + 1) * gd)
            dpw_ref[gi] += _dot_tn(pooled[:, sl], dmix[:, sl])
        dpooled = _heads_dot(dmix, pw_ref, gd, nt=True)
        dmix_h = (dxh_ref[...] * gt * psv).astype(BF16)
        dpooled_h = jnp.where(i < nb - 1, _heads_dot(dmix_h, pw_ref, gd, nt=True), 0.0)
        t1 = (lax.broadcasted_iota(jnp.int32, (tm, 1), 0) + (i * tm + 1)).astype(F32)
        t1h = (lax.broadcasted_iota(jnp.int32, (16, 1), 0) + ((i + 1) * tm + 1)).astype(F32)
        zeros8 = jnp.zeros((8, D), F32)
        ext[pl.ds(0, tm), :] = dpooled / jnp.minimum(t1, wvec)
        ext[pl.ds(tm, 16), :] = dpooled_h / jnp.minimum(t1h, wvec)
        ext[pl.ds(tm + 16, 8), :] = zeros8
        b1[pl.ds(tm + 16, 8), :] = zeros8
        b2[pl.ds(tm + 16, 8), :] = zeros8
        m = n - 8
        b1[pl.ds(0, m), :] = ext[pl.ds(0, m), :] + ext[pl.ds(1, m), :]
        b2[pl.ds(0, m), gd:] = b1[pl.ds(0, m), gd:] + b1[pl.ds(2, m), gd:]
        b1[pl.ds(0, m), 2 * gd:] = b2[pl.ds(0, m), 2 * gd:] + b2[pl.ds(4, m), 2 * gd:]
        b2[pl.ds(0, m), 3 * gd:] = b1[pl.ds(0, m), 3 * gd:] + b1[pl.ds(8, m), 3 * gd:]
        wsum = jnp.concatenate([b1[pl.ds(0, tm), 0:gd], b2[pl.ds(0, tm), gd:2 * gd],
                                b1[pl.ds(0, tm), 2 * gd:3 * gd], b2[pl.ds(0, tm), 3 * gd:]], axis=1)
        dh = wsum - dpooled
        g, sc = gm_ref[...], mod_ref[1:2, :]
        _, xhat, r, nn = _rms_fwd(x_ref[...], g, sc, mod_ref[0:1, :])
        dx, dsh, dsc, dg = _rms_bwd(dh, xhat, r, nn, g, sc)
        acc_ref[0:1, :] += dsh
        acc_ref[1:2, :] += dsc
        acc_ref[6:7, :] += dg
        dx_ref[...] = dx1v + dx

    tile = pl.BlockSpec((tm, D), lambda i: (i, 0))
    halo = pl.BlockSpec((16, D), lambda i: (jnp.minimum((i + 1) * (tm // 16), S // 16 - 1), 0))
    row = pl.BlockSpec((1, D), lambda i: (0, 0))
    wblk = pl.BlockSpec((ng, gd, gd), lambda i: (0, 0, 0))
    return pl.pallas_call(
        body, name="pool_bwd", grid=(nb,),
        out_shape=(jax.ShapeDtypeStruct((S, D), F32), jax.ShapeDtypeStruct((8, D), F32),
                   jax.ShapeDtypeStruct((ng, gd, gd), F32)),
        in_specs=[tile, halo, tile, tile, pl.BlockSpec((8, D), lambda i: (0, 0)), row, wblk, row],
        out_specs=(tile, pl.BlockSpec((8, D), lambda i: (0, 0)), wblk),
        scratch_shapes=[pltpu.VMEM((n, D), F32), pltpu.VMEM((n, D), F32), pltpu.VMEM((n, D), F32)],
        compiler_params=_cparams("arbitrary"),
    )(dx1, dx1, x, pooled, mod_l, g_mix, pw, ps)


def _shard_to_rows(w, D):
    return w.reshape(-1, D)


def _blockdiag_full(gq, na, hd):
    return gq.reshape(NQ, na, HEADS, hd // NQ, hd).transpose(1, 2, 0, 3, 4).reshape(na, HEADS, hd, hd)


def _blockdiag_by_chip(dw, D):
    na, _, hd, _ = dw.shape
    return dw.reshape(na, HEADS, NQ, hd // NQ, hd).transpose(2, 0, 1, 3, 4).reshape(NQ, -1, D)


def kernel(x, c, w_mod, b_mod, norm_mix_g, norm_ffn_g, lru_w_y, lru_b_y, lru_w_in, lru_b_in, lru_conv_w, lru_conv_b, lru_w_a, lru_b_a, lru_w_x, lru_b_x, lru_lambda, lru_w_out, lru_b_out, pool_w, pool_scale, ffn_w1, ffn_w2, final_norm_g, loss_target, m_w_mod, m_b_mod, m_norm_mix_g, m_norm_ffn_g, m_lru_w_y, m_lru_b_y, m_lru_w_in, m_lru_b_in, m_lru_conv_w, m_lru_conv_b, m_lru_w_a, m_lru_b_a, m_lru_w_x, m_lru_b_x, m_lru_lambda, m_lru_w_out, m_lru_b_out, m_pool_w, m_pool_scale, m_ffn_w1, m_ffn_w2, m_final_norm_g, v_w_mod, v_b_mod, v_norm_mix_g, v_norm_ffn_g, v_lru_w_y, v_lru_b_y, v_lru_w_in, v_lru_b_in, v_lru_conv_w, v_lru_conv_b, v_lru_w_a, v_lru_b_a, v_lru_w_x, v_lru_b_x, v_lru_lambda, v_lru_w_out, v_lru_b_out, v_pool_w, v_pool_scale, v_ffn_w1, v_ffn_w2, v_final_norm_g):
    S, D = x.shape[1], x.shape[2]
    L = w_mod.shape[0]
    NA = lru_w_y.shape[0]
    NB = pool_w.shape[0]
    F = ffn_w1.shape[2] * NQ
    f4 = F // NQ
    hd = D // HEADS
    Cs = w_mod.shape[2]
    assert L == DEPTH and Cs * NQ == N_MOD * D and D % 1024 == 0
    x2d = x.reshape(S, D)
    tgt = loss_target.reshape(S, D)
    q = 2 * lax.axis_index("x") + lax.axis_index("y")

    big = [ffn_w1, ffn_w2, lru_w_y, lru_w_in, lru_w_out, lru_w_a, lru_w_x, pool_w]
    rows = [int(w.size) // D for w in big]
    offs = [sum(rows[:k]) for k in range(len(big))]
    O_W1, O_W2, O_WY, O_WIN, O_WOUT, O_WA, O_WX, O_PW = offs
    R = sum(rows)
    dq = D // NQ
    s_w1 = [ffn_w1[i].astype(BF16) for i in range(L)]
    s_w2 = [ffn_w2[i].astype(BF16) for i in range(L)]
    s_wy = [lru_w_y[j].astype(BF16) for j in range(NA)]
    s_win = [lru_w_in[j].astype(BF16) for j in range(NA)]
    s_wout = [lru_w_out[j].astype(BF16) for j in range(NA)]
    s_tiny = jnp.concatenate([_shard_to_rows(w, D) for w in (lru_w_a, lru_w_x, pool_w)], axis=0).astype(BF16)

    cshard = lru_conv_w.reshape(-1)
    small_fwd = jnp.concatenate([c.reshape(-1), cshard, lru_b_a.reshape(-1), lru_b_x.reshape(-1),
                                 pool_scale.reshape(-1)])
    small_fwd = jnp.pad(small_fwd, (0, 8 * D - small_fwd.shape[0])).reshape(8, D)

    g_w1, g_w2 = [None] * L, [None] * L
    g_wy, g_win, g_wout = [None] * NA, [None] * NA, [None] * NA
    SG, g_wy[0], g_win[0], g_tiny = _comm_only("gather_first", small=small_fwd, gathers=(s_wy[0], s_win[0], s_tiny))
    SGf = SG.reshape(NDEV, 8 * D)
    c_all = SGf[:, :D]
    SGq = SGf.reshape(NQ, 2, 8 * D)[:, 0]
    o = D
    n_cw = NA * CONV_W * D // NQ
    conv_w_full = SGq[:, o:o + n_cw].reshape(NQ, NA, CONV_W, D // NQ).transpose(1, 2, 0, 3).reshape(NA, CONV_W, D)
    o += n_cw
    n_b = NA * HEADS * hd // NQ
    b_a_full = SGq[:, o:o + n_b].reshape(NQ, NA, HEADS, hd // NQ).transpose(1, 2, 0, 3).reshape(NA, 1, D)
    o += n_b
    b_x_full = SGq[:, o:o + n_b].reshape(NQ, NA, HEADS, hd // NQ).transpose(1, 2, 0, 3).reshape(NA, 1, D)
    o += n_b
    n_ps = NB * D // NQ
    pool_scale_full = SGq[:, o:o + n_ps].reshape(NQ, NB, D // NQ).transpose(1, 0, 2).reshape(NB, 1, D)

    wa_full = _blockdiag_full(g_tiny[:, :rows[5]], NA, hd)
    wx_full = _blockdiag_full(g_tiny[:, rows[5]:rows[5] + rows[6]], NA, hd)
    pw_full = _blockdiag_full(g_tiny[:, rows[5] + rows[6]:], NB, hd)

    b_mod_sh = lax.dynamic_slice_in_dim(b_mod, q * Cs, Cs, axis=1).reshape(L, 1, Cs)
    modpart = _mod_part(c_all, w_mod, b_mod_sh)
    modq = _exchange_mod(modpart.transpose(1, 0, 2))
    mod = modq.transpose(1, 0, 2).reshape(L, N_MOD, D)
    mod = jnp.pad(mod, ((0, 0), (0, 8 - N_MOD), (0, 0)))

    saved = []
    xcur = x2d
    for i in range(L):
        j = i // 2
        gm = norm_mix_g[i].reshape(1, D)
        gf = norm_ffn_g[i].reshape(1, D)
        if i % 2 == 0:
            if g_wout[j] is None:
                h, gb, xr0, g_wout[j] = _lru_in_fwd(xcur, mod[i], gm, g_wy[j], g_win[j], lru_b_y[j].reshape(1, D),
                                                    lru_b_in[j].reshape(1, D), gathers=(s_wout[j],))
            else:
                h, gb, xr0 = _lru_in_fwd(xcur, mod[i], gm, g_wy[j], g_win[j], lru_b_y[j].reshape(1, D),
                                         lru_b_in[j].reshape(1, D))
            hs, p, g_w1[i], g_w2[i] = _lru_scan_fwd(xr0, gb, conv_w_full[j], lru_conv_b[j].reshape(1, D), wa_full[j],
                                                    b_a_full[j], wx_full[j], b_x_full[j], lru_lambda[j].reshape(1, D),
                                                    gathers=(s_w1[i], s_w2[i]))
            y, x1, h2, a, z, x2, g_w1[i + 1], g_w2[i + 1] = _lru_mix_ffn_fwd(
                xcur, p, mod[i], gf, lru_b_out[j].reshape(1, D), g_wout[j], g_w1[i], g_w2[i],
                gathers=(s_w1[i + 1], s_w2[i + 1]))
            saved.append(dict(x=xcur, h=h, gb=gb, xr0=xr0, hs=hs, p=p, y=y, x1=x1, h2=h2, a=a, z=z))
        else:
            if j + 1 < NA:
                pooled, x1, h2, a, z, x2, g_wy[j + 1], g_win[j + 1], g_wout[j + 1] = _pool_mix_ffn_fwd(
                    xcur, mod[i], gm, pw_full[j], pool_scale_full[j], gf, g_w1[i], g_w2[i],
                    gathers=(s_wy[j + 1], s_win[j + 1], s_wout[j + 1]))
            else:
                pooled, x1, h2, a, z, x2 = _pool_mix_ffn_fwd(xcur, mod[i], gm, pw_full[j], pool_scale_full[j], gf,
                                                             g_w1[i], g_w2[i])
            saved.append(dict(x=xcur, pooled=pooled, x1=x1, h2=h2, a=a, z=z))
        xcur = x2

    dx, lacc = _final_loss_bwd(xcur, final_norm_g.reshape(1, D), tgt)
    loss = lax.psum(0.5 * jnp.sum(lacc[0]) / D, ("x", "y", "c"))
    d_final_g = lacc[1]

    ppack = lax.empty((R, D), F32)
    psib = lax.empty((R, D), F32)
    pending, summed = [], []

    def comm_args():
        kw = {}
        if pending:
            kw["scatters"] = tuple(dw for dw, _ in pending)
        if summed:
            kw["sib"] = (ppack, psib, tuple(summed))
        return kw

    def after_host(extra):
        nonlocal ppack, psib, pending, summed
        had_sib = bool(summed)
        summed = []
        for (dw, off), rb in zip(pending, extra[:len(pending)]):
            own = lax.dynamic_index_in_dim(dw, q, axis=0, keepdims=False)
            ppack = _sum_into(ppack, own, rb, off)
            summed.append((off, dw.shape[1]))
        if had_sib:
            psib = extra[len(pending)]
        pending = []

    dmod_rows = [None] * L
    dg_mix = [None] * L
    dg_ffn = [None] * L
    d_small = {}
    dwa_l, dwx_l, dpw_l = [None] * NA, [None] * NA, [None] * NB
    for i in reversed(range(L)):
        j = i // 2
        sv = saved[i]
        gm = norm_mix_g[i].reshape(1, D)
        gf = norm_ffn_g[i].reshape(1, D)
        outs = _ffn_bwd(dx, sv["x1"], sv["a"], sv["z"], mod[i], gf, g_w1[i], g_w2[i], **comm_args())
        dx1, du, dz, facc = outs[:4]
        after_host(outs[4:])
        pending.append((_dw_blocked(sv["h2"], du, False, False, "dw1"), O_W1 + i * D))
        pending.append((_dw_blocked(sv["a"], dz, True, True, "dw2"), O_W2 + i * f4))
        if i % 2 == 0:
            dyp, dp, oacc = _lru_out_bwd(dx1, sv["y"], mod[i], g_wout[j])
            pending.append((_dw_whole(sv["p"], [dyp], "dwout")[0], O_WOUT + j * dq))
            outs = _lru_scan_bwd(dp, sv["xr0"], sv["gb"], sv["hs"], conv_w_full[j], lru_conv_b[j].reshape(1, D),
                                 wa_full[j], b_a_full[j], wx_full[j], b_x_full[j], lru_lambda[j].reshape(1, D),
                                 **comm_args())
            dgb, dxr, sm, dwa, dwx = outs[:5]
            after_host(outs[5:])
            dwy, dwin = _dw_whole(sv["h"], [dgb, dxr], "dwy_dwin")
            pending.append((dwy, O_WY + j * dq))
            pending.append((dwin, O_WIN + j * dq))
            if i == 0:
                outs = _lru_in_bwd(dxr, dgb, dx1, sv["x"], mod[i], gm, g_wy[j], g_win[j], **comm_args())
                dx, macc = outs[:2]
                after_host(outs[2:])
            else:
                dx, macc = _lru_in_bwd(dxr, dgb, dx1, sv["x"], mod[i], gm, g_wy[j], g_win[j])
            dwa_l[j], dwx_l[j] = dwa, dwx
            d_small[("lru", j)] = (sm, oacc[3])
            dgt_m = oacc[2]
        else:
            dx, macc, dpw = _pool_bwd(dx1, sv["x"], sv["pooled"], mod[i], gm, pw_full[j], pool_scale_full[j])
            dpw_l[j] = dpw
            d_small[("pool", j)] = macc[3]
            dgt_m = macc[2]
        dmod_rows[i] = jnp.stack([macc[0], macc[1], dgt_m, facc[3], facc[4], facc[5]])
        dg_mix[i] = macc[6]
        dg_ffn[i] = facc[7]
    grad_x = dx.reshape(x.shape)

    tiny = jnp.concatenate([_blockdiag_by_chip(jnp.stack(dwa_l), D), _blockdiag_by_chip(jnp.stack(dwx_l), D),
                            _blockdiag_by_chip(jnp.stack(dpw_l), D)], axis=1).astype(BF16)
    pending.append((tiny, O_WA))

    lru_sm = [d_small[("lru", j)] for j in range(NA)]
    small_rows = [jnp.stack(dmod_rows).reshape(L * N_MOD, D), jnp.stack(dg_mix), jnp.stack(dg_ffn),
                  jnp.stack([s[0][9] for s in lru_sm]), jnp.stack([s[0][8] for s in lru_sm]),
                  jnp.stack([s[0][4] for s in lru_sm]), jnp.stack([s[0][7] for s in lru_sm]),
                  jnp.stack([s[1] for s in lru_sm]),
                  jnp.stack([s[0][0:CONV_W] for s in lru_sm]).reshape(NA * CONV_W, D),
                  jnp.stack([s[0][5] for s in lru_sm]), jnp.stack([s[0][6] for s in lru_sm]),
                  jnp.stack([d_small[("pool", j)] for j in range(NB)]), d_final_g.reshape(1, D)]
    small_g = jnp.concatenate(small_rows, axis=0)
    n_small = small_g.shape[0]
    assert n_small <= SMALL_ROWS
    small_g = jnp.pad(small_g, ((0, SMALL_ROWS - n_small), (0, 0)))

    outs = _comm_only("scatter_last", small=small_g, reduce_small=True, **comm_args())
    sg_all, sg_sum = outs[:2]
    after_host(outs[2:])
    psum_mine = ppack
    psum_sib = _comm_only("sibling_last", sib=(ppack, psib, tuple(summed)))[0]

    def big_update(w, m, v, off, name):
        shp = w.shape
        g, dl, m2, v2 = _adam_rows(w.reshape(-1, D), m.reshape(-1, D), v.reshape(-1, D), psum_mine, psum_sib, off, name)
        return g.reshape(shp), dl.reshape(shp), m2.reshape(shp), v2.reshape(shp)

    res = {}
    res["ffn_w1"] = big_update(ffn_w1, m_ffn_w1, v_ffn_w1, O_W1, "adam_w1")
    res["ffn_w2"] = big_update(ffn_w2, m_ffn_w2, v_ffn_w2, O_W2, "adam_w2")
    res["lru_w_y"] = big_update(lru_w_y, m_lru_w_y, v_lru_w_y, O_WY, "adam_wy")
    res["lru_w_in"] = big_update(lru_w_in, m_lru_w_in, v_lru_w_in, O_WIN, "adam_win")
    res["lru_w_out"] = big_update(lru_w_out, m_lru_w_out, v_lru_w_out, O_WOUT, "adam_wout")
    res["lru_w_a"] = big_update(lru_w_a, m_lru_w_a, v_lru_w_a, O_WA, "adam_wa")
    res["lru_w_x"] = big_update(lru_w_x, m_lru_w_x, v_lru_w_x, O_WX, "adam_wx")
    res["pool_w"] = big_update(pool_w, m_pool_w, v_pool_w, O_PW, "adam_pw")

    dmod_all = sg_all[:, :L * N_MOD, :].reshape(NDEV, L, N_MOD * D)
    dmod_sh = lax.dynamic_slice_in_dim(dmod_all, q * Cs, Cs, axis=2).transpose(1, 0, 2)
    res["w_mod"] = _wmod_update(c_all.T, dmod_sh, w_mod, m_w_mod, v_w_mod)

    r0 = 0

    def take(nrows):
        nonlocal r0
        out = sg_sum[r0:r0 + nrows]
        r0 += nrows
        return out

    g_b_mod = take(L * N_MOD).reshape(L, N_MOD * D)
    g_mix = take(L)
    g_ffn = take(L)
    g_b_y = take(NA)
    g_b_in = take(NA)
    g_conv_b = take(NA)
    g_lam = take(NA)
    g_b_out = take(NA)
    g_conv_w = lax.dynamic_slice_in_dim(take(NA * CONV_W).reshape(NA, CONV_W, D), q * (D // NQ), D // NQ, axis=2)
    g_b_a = lax.dynamic_slice_in_dim(take(NA).reshape(NA, HEADS, hd), q * (hd // NQ), hd // NQ, axis=2)
    g_b_x = lax.dynamic_slice_in_dim(take(NA).reshape(NA, HEADS, hd), q * (hd // NQ), hd // NQ, axis=2)
    g_ps = lax.dynamic_slice_in_dim(take(NB), q * (D // NQ), D // NQ, axis=1)
    g_fin = take(1).reshape(D)

    smalls = [("b_mod", b_mod, m_b_mod, v_b_mod, g_b_mod), ("norm_mix_g", norm_mix_g, m_norm_mix_g, v_norm_mix_g, g_mix),
              ("norm_ffn_g", norm_ffn_g, m_norm_ffn_g, v_norm_ffn_g, g_ffn),
              ("lru_b_y", lru_b_y, m_lru_b_y, v_lru_b_y, g_b_y), ("lru_b_in", lru_b_in, m_lru_b_in, v_lru_b_in, g_b_in),
              ("lru_conv_w", lru_conv_w, m_lru_conv_w, v_lru_conv_w, g_conv_w),
              ("lru_conv_b", lru_conv_b, m_lru_conv_b, v_lru_conv_b, g_conv_b),
              ("lru_b_a", lru_b_a, m_lru_b_a, v_lru_b_a, g_b_a), ("lru_b_x", lru_b_x, m_lru_b_x, v_lru_b_x, g_b_x),
              ("lru_lambda", lru_lambda, m_lru_lambda, v_lru_lambda, g_lam),
              ("lru_b_out", lru_b_out, m_lru_b_out, v_lru_b_out, g_b_out),
              ("pool_scale", pool_scale, m_pool_scale, v_pool_scale, g_ps),
              ("final_norm_g", final_norm_g, m_final_norm_g, v_final_norm_g, g_fin)]
    total = sum(int(s[1].size) for s in smalls)
    prow = -(-total // (8 * 128)) * 8

    def flat(k):
        f = jnp.concatenate([s[k].reshape(-1) for s in smalls])
        return jnp.pad(f, (0, prow * 128 - total)).reshape(prow, 128)

    dl_s, m_s, v_s = _adam_small(flat(1), flat(4), flat(2), flat(3))
    o = 0
    for name, w, _, _, g in smalls:
        sz = int(w.size)
        cut = lambda arr: arr.reshape(-1)[o:o + sz].reshape(w.shape)
        res[name] = (g.reshape(w.shape), cut(dl_s), cut(m_s), cut(v_s))
        o += sz

    order = ["w_mod", "b_mod", "norm_mix_g", "norm_ffn_g", "lru_w_y", "lru_b_y", "lru_w_in", "lru_b_in", "lru_conv_w",
             "lru_conv_b", "lru_w_a", "lru_b_a", "lru_w_x", "lru_b_x", "lru_lambda", "lru_w_out", "lru_b_out", "pool_w",
             "pool_scale", "ffn_w1", "ffn_w2", "final_norm_g"]
    return (loss, grad_x, *[res[n][0] for n in order], *[res[n][1] for n in order],
            *[res[n][2] for n in order], *[res[n][3] for n in order])
```

```python
import functools

import jax
import jax.numpy as jnp
from jax import lax
from jax.experimental import pallas as pl
from jax.experimental.pallas import tpu as pltpu

F32 = jnp.float32
BF16 = jnp.bfloat16
MESH = pl.DeviceIdType.MESH

NQ = 4
NDEV = 8
DEPTH = 4
N_MOD = 6
HEADS = 4
CONV_W = 4
POOL_WINDOWS = (2, 4, 8, 16)
LRU_C = 8.0
EPS = 1e-6
ADAM_LR, ADAM_B1, ADAM_B2, ADAM_EPS, ADAM_WD, ADAM_STEP = 0.001, 0.9, 0.999, 1e-08, 0.01, 10

TM = 512
TT = 256
TP = 256
TK = 2048
SMALL_ROWS = 64
FORWARD_STEPS = 4
VMEM_LIMIT = 60 * 1024 * 1024


def _cparams(*sem):
    return pltpu.CompilerParams(dimension_semantics=tuple(sem), vmem_limit_bytes=VMEM_LIMIT)


def _dot(a, b):
    return jnp.dot(a, b, preferred_element_type=F32)


def _dot_nt(a, b):
    return lax.dot_general(a, b, (((1,), (1,)), ((), ())), preferred_element_type=F32)


def _dot_tn(a, b):
    return lax.dot_general(a, b, (((0,), (0,)), ((), ())), preferred_element_type=F32)


def _resident(shape, index_map):
    return pl.BlockSpec(shape, index_map, pipeline_mode=pl.Buffered(1))


def _rms_fwd(x, g, sc, sh):
    r = lax.rsqrt(jnp.mean(x * x, axis=-1, keepdims=True) + EPS)
    xhat = x * r
    n = xhat * g
    return n * (1.0 + sc) + sh, xhat, r, n


def _rms_bwd(dh, xhat, r, n, g, sc):
    dsh = jnp.sum(dh, axis=0, keepdims=True)
    dsc = jnp.sum(dh * n, axis=0, keepdims=True)
    dn = dh * (1.0 + sc)
    dg = jnp.sum(dn * xhat, axis=0, keepdims=True)
    dxh = dn * g
    dx = r * (dxh - xhat * jnp.mean(dxh * xhat, axis=-1, keepdims=True))
    return dx, dsh, dsc, dg


_GELU_K = 0.7978845608028654
_GELU_C = 0.044715


def _gelu(x):
    t = jnp.tanh(_GELU_K * (x + _GELU_C * x * x * x))
    return 0.5 * x * (1.0 + t), t


def _gelu_grad(x, t):
    return 0.5 * (1.0 + t) + 0.5 * x * (1.0 - t * t) * (_GELU_K * (1.0 + 3.0 * _GELU_C * x * x))


def _neg_expm1(y, exp_y):
    series = -(y * (1.0 + y * (0.5 + y * (1.0 / 6.0))))
    return jnp.where(y > -(1.0 / 64.0), series, 1.0 - exp_y)


def _zero_first(ref):
    @pl.when(pl.program_id(0) == 0)
    def _():
        ref[...] = jnp.zeros_like(ref)


def _my_pos():
    return lax.axis_index("x"), lax.axis_index("y"), lax.axis_index("c")


def _dev_index(x, y, c):
    return 4 * x + 2 * y + c


def _chip_peers(x, y):
    return [(1 - x, y), (x, 1 - y), (1 - x, 1 - y)]


def _all_peers(x, y, c):
    return [(px, py, c) for (px, py) in _chip_peers(x, y)] + [(x, y, 1 - c)] + \
           [(px, py, 1 - c) for (px, py) in _chip_peers(x, y)]


def _comm_run(phase, x, y, c, gathers, scatters, sib, send, recv, loc):
    q = 2 * x + y
    peers = _chip_peers(x, y)
    sibling = (x, y, 1 - c)

    def rcopy(src, dst, s, dev):
        return pltpu.make_async_remote_copy(src, dst, send.at[s], recv.at[s], device_id=dev, device_id_type=MESH)

    s = 0
    for gi, (src, dst) in enumerate(gathers):
        half = src.shape[0] // 2
        mine, other = pl.ds(c * half, half), pl.ds((1 - c) * half, half)
        own = pltpu.make_async_copy(src, dst.at[q], loc.at[gi])
        if phase == "start":
            own.start()
        elif phase == "finish":
            own.wait()
        for (px, py) in peers:
            pq = 2 * px + py
            s_ici, s_fwd = s, s + 1
            s += 2
            if phase == "start":
                rcopy(src.at[mine], dst.at[q].at[mine], s_ici, (px, py, c)).start()
            elif phase == "forward":
                rcopy(src.at[mine], dst.at[pq].at[mine], s_ici, (px, py, c)).wait_recv()
                rcopy(dst.at[pq].at[mine], dst.at[pq].at[mine], s_fwd, sibling).start()
            else:
                rcopy(dst.at[pq].at[other], dst.at[pq].at[other], s_fwd, sibling).wait_recv()
                rcopy(src.at[mine], dst.at[q].at[mine], s_ici, (px, py, c)).wait_send()
                rcopy(dst.at[pq].at[mine], dst.at[pq].at[mine], s_fwd, sibling).wait_send()
    direct = []
    for (src, dst) in scatters:
        for k, (px, py) in enumerate(peers):
            direct.append((src.at[2 * px + py], dst.at[k], (px, py, c)))
    if sib is not None:
        src, dst, ranges = sib
        for (off, rows) in ranges:
            direct.append((src.at[pl.ds(off, rows)], dst.at[pl.ds(off, rows)], sibling))
    if phase == "start":
        for k, (a, b, dev) in enumerate(direct):
            rcopy(a, b, s + k, dev).start()
    elif phase == "finish":
        for k, (a, b, dev) in enumerate(direct):
            rcopy(a, b, s + k, dev).wait_recv()
        for k, (a, b, dev) in enumerate(direct):
            rcopy(a, b, s + k, dev).wait_send()


def _comm_shapes(gathers, scatters, sib):
    assert all(g.shape[0] % 32 == 0 for g in gathers)
    cin = list(gathers) + list(scatters) + ([sib[0], sib[1]] if sib else [])
    cout = [jax.ShapeDtypeStruct((NQ,) + g.shape, g.dtype) for g in gathers] + \
           [jax.ShapeDtypeStruct((3,) + s.shape[1:], s.dtype) for s in scatters] + \
           ([jax.ShapeDtypeStruct(sib[1].shape, sib[1].dtype)] if sib else [])
    n_rem = 6 * len(gathers) + 3 * len(scatters) + (len(sib[2]) if sib else 0)
    sems = [pltpu.SemaphoreType.DMA((max(n_rem, 1),)), pltpu.SemaphoreType.DMA((max(n_rem, 1),)),
            pltpu.SemaphoreType.DMA((max(len(gathers), 1),))]
    return cin, cout, sems


def _pcall(body, *, name, grid, in_specs, out_specs, out_shape, operands, scratch_shapes=(),
           gathers=(), scatters=(), sib=None):
    assert len(grid) == 1
    out_shape, out_specs = tuple(out_shape), tuple(out_specs)
    if not (gathers or scatters or sib):
        return pl.pallas_call(body, name=name, grid=grid, in_specs=list(in_specs), out_specs=out_specs,
                              out_shape=out_shape, scratch_shapes=list(scratch_shapes),
                              compiler_params=_cparams("arbitrary"))(*operands)
    cin, cout, sems = _comm_shapes(gathers, scatters, sib)
    n_in, n_cin, n_out, n_cout, n_scr = len(operands), len(cin), len(out_shape), len(cout), len(scratch_shapes)
    ng, ns = len(gathers), len(scatters)
    nsteps = grid[0]

    def wrapped(*refs):
        ins = refs[:n_in]
        cins = refs[n_in:n_in + n_cin]
        o0 = n_in + n_cin
        outs = refs[o0:o0 + n_out]
        couts = refs[o0 + n_out:o0 + n_out + n_cout]
        s0 = o0 + n_out + n_cout
        scr = refs[s0:s0 + n_scr]
        send, recv, loc = refs[s0 + n_scr:s0 + n_scr + 3]
        x, y, c = _my_pos()

        def run(phase):
            g = [(cins[k], couts[k]) for k in range(ng)]
            sc = [(cins[ng + k], couts[ng + k]) for k in range(ns)]
            sb = (cins[ng + ns], couts[ng + ns], sib[2]) if sib else None
            _comm_run(phase, x, y, c, g, sc, sb, send, recv, loc)

        @pl.when(pl.program_id(0) == 0)
        def _():
            run("start")

        if ng:
            @pl.when(pl.program_id(0) == max(nsteps - FORWARD_STEPS, 0))
            def _():
                run("forward")

        body(*ins, *outs, *scr)

        @pl.when(pl.program_id(0) == nsteps - 1)
        def _():
            run("finish")

    anyspec = pl.BlockSpec(memory_space=pl.ANY)
    aliases = {n_in + ng + ns + 1: n_out + ng + ns} if sib else {}
    return pl.pallas_call(
        wrapped, name=name, grid=grid,
        in_specs=list(in_specs) + [anyspec] * n_cin, out_specs=out_specs + (anyspec,) * n_cout,
        out_shape=out_shape + tuple(cout), scratch_shapes=list(scratch_shapes) + sems,
        input_output_aliases=aliases,
        compiler_params=pltpu.CompilerParams(dimension_semantics=("arbitrary",), vmem_limit_bytes=VMEM_LIMIT,
                                             has_side_effects=True),
    )(*operands, *cin)


def _comm_only(name, small=None, reduce_small=False, gathers=(), scatters=(), sib=None):
    cin, cout, sems = _comm_shapes(gathers, scatters, sib)
    n_cin, n_cout = len(cin), len(cout)
    ng, ns = len(gathers), len(scatters)
    n_sm_in = 1 if small is not None else 0
    n_sm_out = (2 if reduce_small else 1) if small is not None else 0

    def body(*refs):
        sm_in = refs[:n_sm_in]
        cins = refs[n_sm_in:n_sm_in + n_cin]
        o0 = n_sm_in + n_cin
        sm_out = refs[o0:o0 + n_sm_out]
        couts = refs[o0 + n_sm_out:o0 + n_sm_out + n_cout]
        s0 = o0 + n_sm_out + n_cout
        send, recv, loc = refs[s0:s0 + 3]
        x, y, c = _my_pos()
        g = [(cins[k], couts[k]) for k in range(ng)]
        sc = [(cins[ng + k], couts[ng + k]) for k in range(ns)]
        sb = (cins[ng + ns], couts[ng + ns], sib[2]) if sib else None
        _comm_run("start", x, y, c, g, sc, sb, send, recv, loc)
        if small is not None:
            sm_send, sm_recv = refs[s0 + 3:s0 + 5]
            small_ref, sg_ref = sm_in[0], sm_out[0]
            me = _dev_index(x, y, c)
            sg_ref[me] = small_ref[...]
            peers = _all_peers(x, y, c)
            sm = [pltpu.make_async_remote_copy(small_ref, sg_ref.at[me], sm_send.at[k], sm_recv.at[k],
                                               device_id=peer, device_id_type=MESH) for k, peer in enumerate(peers)]
            for cp in sm:
                cp.start()
            for k, (px, py, pc) in enumerate(peers):
                pltpu.make_async_remote_copy(small_ref, sg_ref.at[_dev_index(px, py, pc)], sm_send.at[k], sm_recv.at[k],
                                             device_id=(px, py, pc), device_id_type=MESH).wait_recv()
            if reduce_small:
                acc = sg_ref[0]
                for d in range(1, NDEV):
                    acc = acc + sg_ref[d]
                sm_out[1][...] = acc
            for cp in sm:
                cp.wait_send()
        if ng:
            _comm_run("forward", x, y, c, g, sc, sb, send, recv, loc)
        _comm_run("finish", x, y, c, g, sc, sb, send, recv, loc)

    anyspec = pl.BlockSpec(memory_space=pl.ANY)
    vspec = pl.BlockSpec(memory_space=pltpu.VMEM)
    sm_shapes = []
    if small is not None:
        sm_shapes.append(jax.ShapeDtypeStruct((NDEV,) + small.shape, small.dtype))
        if reduce_small:
            sm_shapes.append(jax.ShapeDtypeStruct(small.shape, small.dtype))
        sems = sems + [pltpu.SemaphoreType.DMA((NDEV - 1,)), pltpu.SemaphoreType.DMA((NDEV - 1,))]
    aliases = {n_sm_in + ng + ns + 1: n_sm_out + ng + ns} if sib else {}
    return pl.pallas_call(
        body, name=name,
        in_specs=[vspec] * n_sm_in + [anyspec] * n_cin,
        out_specs=tuple([vspec] * n_sm_out + [anyspec] * n_cout),
        out_shape=tuple(sm_shapes + cout), scratch_shapes=sems, input_output_aliases=aliases,
        compiler_params=pltpu.CompilerParams(has_side_effects=True),
    )(*([small] if small is not None else []), *cin)


def _exchange_mod(modpart):
    _, L, Cs = modpart.shape

    def body(part_ref, out_ref, send, recv):
        x, y, c = _my_pos()
        q = 2 * x + y
        me = _dev_index(x, y, c)
        out_ref[q] = part_ref[me]
        sends = []
        for k, (px, py) in enumerate(_chip_peers(x, y)):
            cp = pltpu.make_async_remote_copy(part_ref.at[_dev_index(px, py, c)], out_ref.at[q], send.at[k], recv.at[k],
                                              device_id=(px, py, c), device_id_type=MESH)
            cp.start()
            sends.append(cp)
        for k, (px, py) in enumerate(_chip_peers(x, y)):
            pltpu.make_async_remote_copy(part_ref.at[me], out_ref.at[2 * px + py], send.at[k], recv.at[k],
                                         device_id=(px, py, c), device_id_type=MESH).wait_recv()
        for cp in sends:
            cp.wait_send()

    return pl.pallas_call(
        body, name="exchange_mod",
        out_shape=jax.ShapeDtypeStruct((NQ, L, Cs), modpart.dtype),
        in_specs=[pl.BlockSpec(memory_space=pltpu.VMEM)],
        out_specs=pl.BlockSpec(memory_space=pltpu.VMEM),
        scratch_shapes=[pltpu.SemaphoreType.DMA((3,)), pltpu.SemaphoreType.DMA((3,))],
        compiler_params=pltpu.CompilerParams(has_side_effects=True),
    )(modpart)


def _mod_part(c_all, w_mod, b_mod_sh):
    L, D, Cs = w_mod.shape
    tn = 512 if Cs % 512 == 0 else Cs

    def body(c_ref, w_ref, b_ref, o_ref):
        cv = c_ref[...]
        cond = cv * jax.nn.sigmoid(cv)
        o_ref[...] = jnp.dot(cond, w_ref[...], preferred_element_type=F32, precision=lax.Precision.HIGHEST) + b_ref[...]

    return pl.pallas_call(
        body, name="mod_part", grid=(L, Cs // tn),
        out_shape=jax.ShapeDtypeStruct((L, NDEV, Cs), F32),
        in_specs=[pl.BlockSpec((NDEV, D), lambda i, j: (0, 0)),
                  pl.BlockSpec((None, D, tn), lambda i, j: (i, 0, j)),
                  pl.BlockSpec((None, 1, tn), lambda i, j: (i, 0, j))],
        out_specs=pl.BlockSpec((None, NDEV, tn), lambda i, j: (i, 0, j)),
        compiler_params=_cparams("parallel", "parallel"),
    )(c_all, w_mod, b_mod_sh)


def _adam(w, g, m, v):
    m2 = ADAM_B1 * m + (1.0 - ADAM_B1) * g
    v2 = ADAM_B2 * v + (1.0 - ADAM_B2) * (g * g)
    m_hat = m2 / (1.0 - ADAM_B1 ** ADAM_STEP)
    v_hat = v2 / (1.0 - ADAM_B2 ** ADAM_STEP)
    delta = -ADAM_LR * (m_hat / (jnp.sqrt(v_hat) + ADAM_EPS) + ADAM_WD * w)
    return delta, m2, v2


def _wmod_update(c_all_t, dmod_sh, w, m, v):
    L, D, Cs = w.shape
    td = 256 if D % 256 == 0 else D

    def body(ct_ref, d_ref, w_ref, m_ref, v_ref, g_ref, dl_ref, m2_ref, v2_ref):
        cv = ct_ref[...]
        cond = cv * jax.nn.sigmoid(cv)
        g = cond[:, 0:1] * d_ref[0:1, :]
        for b in range(1, NDEV):
            g = g + cond[:, b:b + 1] * d_ref[b:b + 1, :]
        g_ref[...] = g
        dl_ref[...], m2_ref[...], v2_ref[...] = _adam(w_ref[...], g, m_ref[...], v_ref[...])

    blk = pl.BlockSpec((None, td, Cs), lambda i, j: (i, j, 0))
    out = jax.ShapeDtypeStruct((L, D, Cs), F32)
    return pl.pallas_call(
        body, name="wmod_update", grid=(L, D // td),
        out_shape=(out, out, out, out),
        in_specs=[pl.BlockSpec((td, NDEV), lambda i, j: (j, 0)),
                  pl.BlockSpec((None, NDEV, Cs), lambda i, j: (i, 0, 0)), blk, blk, blk],
        out_specs=(blk, blk, blk, blk),
        compiler_params=_cparams("parallel", "parallel"),
    )(c_all_t, dmod_sh, w, m, v)


def _adam_rows(w, m, v, pa, pb, row_off, name):
    rows, C = w.shape
    tr = 512 if rows % 512 == 0 else (128 if rows % 128 == 0 else rows)
    assert row_off % tr == 0
    ob = row_off // tr

    def body(w_ref, m_ref, v_ref, pa_ref, pb_ref, g_ref, dl_ref, m2_ref, v2_ref):
        g = pa_ref[...] + pb_ref[...]
        g_ref[...] = g
        dl_ref[...], m2_ref[...], v2_ref[...] = _adam(w_ref[...], g, m_ref[...], v_ref[...])

    blk = pl.BlockSpec((tr, C), lambda i: (i, 0))
    pblk = pl.BlockSpec((tr, C), lambda i: (ob + i, 0))
    out = jax.ShapeDtypeStruct((rows, C), F32)
    return pl.pallas_call(
        body, name=name, grid=(rows // tr,), out_shape=(out, out, out, out),
        in_specs=[blk, blk, blk, pblk, pblk], out_specs=(blk, blk, blk, blk),
        compiler_params=_cparams("parallel"),
    )(w, m, v, pa, pb)


def _adam_small(w, g, m, v):
    def body(w_ref, g_ref, m_ref, v_ref, dl_ref, m2_ref, v2_ref):
        dl_ref[...], m2_ref[...], v2_ref[...] = _adam(w_ref[...], g_ref[...], m_ref[...], v_ref[...])

    out = jax.ShapeDtypeStruct(w.shape, F32)
    return pl.pallas_call(body, name="adam_small", out_shape=(out, out, out))(w, g, m, v)


def _sum_into(ppack, own, rb, off):
    rows, D = own.shape
    tr = 256 if rows % 256 == 0 else 128
    assert rows % tr == 0 and off % tr == 0
    ob = off // tr

    def body(o_ref, r_ref, pin_ref, p_ref):
        acc = o_ref[...].astype(F32)
        for k in range(3):
            acc = acc + r_ref[k].astype(F32)
        p_ref[...] = acc

    return pl.pallas_call(
        body, name="sum_partials", grid=(rows // tr,), out_shape=jax.ShapeDtypeStruct(ppack.shape, ppack.dtype),
        in_specs=[pl.BlockSpec((tr, D), lambda i: (i, 0)), pl.BlockSpec((3, tr, D), lambda i: (0, i, 0)),
                  pl.BlockSpec(memory_space=pl.ANY)],
        out_specs=pl.BlockSpec((tr, D), lambda i: (ob + i, 0)),
        input_output_aliases={2: 0},
        compiler_params=_cparams("parallel"),
    )(own, rb, ppack)


def _wspec(g):
    return _resident(g.shape, lambda i: (0, 0, 0))


def _ffn_fwd_inner(x1, mod_ref, gf_ref, w1_ref, w2_ref, h2_ref, a_ref, z_ref, x2_ref):
    h2 = _rms_fwd(x1, gf_ref[...], mod_ref[4:5, :], mod_ref[3:4, :])[0]
    h2b = h2.astype(BF16)
    h2_ref[...] = h2b
    f4 = w1_ref.shape[2]
    z = jnp.zeros(x1.shape, F32)
    for q in range(NQ):
        a = jnp.maximum(_dot(h2b, w1_ref[q]), 0.0)
        a_ref[:, q * f4:(q + 1) * f4] = a.astype(BF16)
        z = z + _dot((a * a).astype(BF16), w2_ref[q])
    z_ref[...] = z.astype(BF16)
    x2_ref[...] = x1 + mod_ref[5:6, :] * z


def _lru_in_fwd(x, mod_l, g_mix, g_wy, g_win, b_y, b_in, **comm):
    S, D = x.shape
    tm = min(TM, S)

    def body(x_ref, mod_ref, g_ref, wy_ref, win_ref, by_ref, bin_ref, h_ref, gb_ref, xr_ref):
        h = _rms_fwd(x_ref[...], g_ref[...], mod_ref[1:2, :], mod_ref[0:1, :])[0]
        hb = h.astype(BF16)
        h_ref[...] = hb
        gb_ref[...] = _dot(hb, wy_ref[...].reshape(D, D)) + by_ref[...]
        xr_ref[...] = _dot(hb, win_ref[...].reshape(D, D)) + bin_ref[...]

    tile = pl.BlockSpec((tm, D), lambda i: (i, 0))
    row = pl.BlockSpec((1, D), lambda i: (0, 0))
    return _pcall(
        body, name="lru_in_fwd", grid=(S // tm,),
        out_shape=(jax.ShapeDtypeStruct((S, D), BF16), jax.ShapeDtypeStruct((S, D), F32), jax.ShapeDtypeStruct((S, D), F32)),
        in_specs=[tile, pl.BlockSpec((8, D), lambda i: (0, 0)), row, _wspec(g_wy), _wspec(g_win), row, row],
        out_specs=(tile, tile, tile),
        operands=(x, mod_l, g_mix, g_wy, g_win, b_y, b_in), **comm)


def _heads_dot(xb, w_ref, hd, nt=False):
    outs = []
    for h in range(HEADS):
        xs = xb[:, h * hd:(h + 1) * hd]
        outs.append(_dot_nt(xs, w_ref[h]) if nt else _dot(xs, w_ref[h]))
    return jnp.concatenate(outs, axis=1)


def _lru_gates(xc, wa_ref, ba, wx_ref, bx, lam, hd):
    xcb = xc.astype(BF16)
    gate_r = jax.nn.sigmoid(_heads_dot(xcb, wa_ref, hd) + ba)
    gate_i = jax.nn.sigmoid(_heads_dot(xcb, wx_ref, hd) + bx)
    ls = jax.nn.log_sigmoid(lam)
    log_a = LRU_C * gate_r * ls
    a = jnp.exp(log_a)
    mult = jnp.sqrt(_neg_expm1(2.0 * log_a, a * a))
    return xcb, gate_r, gate_i, ls, a, mult


def _conv_taps(xext, cw, tt):
    acc = cw[0:1, :] * xext[pl.ds(8 - (CONV_W - 1), tt), :]
    for k in range(1, CONV_W):
        acc = acc + cw[k:k + 1, :] * xext[pl.ds(8 - (CONV_W - 1) + k, tt), :]
    return acc


def _lru_scan_fwd(xr0, gb, cw, cb, wa, ba, wx, bx, lam, **comm):
    S, W = xr0.shape
    tt = min(TT, S)
    hd = W // HEADS

    def body(xr_ref, xrh_ref, gb_ref, cw_ref, cb_ref, wa_ref, ba_ref, wx_ref, bx_ref, lam_ref,
             hs_ref, p_ref, xext, a_s, u_s, carry):
        i = pl.program_id(0)

        @pl.when(i == 0)
        def _():
            carry[...] = jnp.zeros_like(carry)

        xext[0:8, :] = jnp.where(i > 0, xrh_ref[...], 0.0)
        xext[pl.ds(8, tt), :] = xr_ref[...]
        xc = _conv_taps(xext, cw_ref[...], tt) + cb_ref[...]
        _, _, gate_i, _, a, mult = _lru_gates(xc, wa_ref, ba_ref[...], wx_ref, bx_ref[...], lam_ref[...], hd)
        a_s[...] = a
        u_s[...] = mult * (gate_i * xc)
        row = lax.broadcasted_iota(jnp.int32, (8, W), 0)

        def step(k, _):
            off = pl.multiple_of(k * 8, 8)
            A = a_s[pl.ds(off, 8), :]
            U = u_s[pl.ds(off, 8), :]
            for d in (1, 2, 4):
                keep = row >= d
                Us = jnp.where(keep, pltpu.roll(U, d, 0), 0.0)
                As = jnp.where(keep, pltpu.roll(A, d, 0), 1.0)
                U = U + A * Us
                A = A * As
            H = U + A * carry[...]
            hs_ref[pl.ds(off, 8), :] = H
            carry[...] = jnp.broadcast_to(H[7:8, :], (8, W))
            return 0

        lax.fori_loop(0, tt // 8, step, 0)
        p_ref[...] = (hs_ref[...] * _gelu(gb_ref[...])[0]).astype(BF16)

    tile = pl.BlockSpec((tt, W), lambda i: (i, 0))
    halo = pl.BlockSpec((8, W), lambda i: (jnp.maximum(i * (tt // 8) - 1, 0), 0))
    row = pl.BlockSpec((1, W), lambda i: (0, 0))
    wblk = pl.BlockSpec((HEADS, hd, hd), lambda i: (0, 0, 0))
    return _pcall(
        body, name="lru_scan_fwd", grid=(S // tt,),
        out_shape=(jax.ShapeDtypeStruct((S, W), F32), jax.ShapeDtypeStruct((S, W), BF16)),
        in_specs=[tile, halo, tile, pl.BlockSpec((CONV_W, W), lambda i: (0, 0)), row, wblk, row, wblk, row, row],
        out_specs=(tile, tile),
        scratch_shapes=[pltpu.VMEM((tt + 8, W), F32), pltpu.VMEM((tt, W), F32), pltpu.VMEM((tt, W), F32),
                        pltpu.VMEM((8, W), F32)],
        operands=(xr0, xr0, gb, cw, cb, wa, ba, wx, bx, lam), **comm)


def _ffn_out_shapes(S, D, F):
    return (jax.ShapeDtypeStruct((S, D), F32), jax.ShapeDtypeStruct((S, D), BF16), jax.ShapeDtypeStruct((S, F), BF16),
            jax.ShapeDtypeStruct((S, D), BF16), jax.ShapeDtypeStruct((S, D), F32))


def _lru_mix_ffn_fwd(x, p, mod_l, g_ffn, b_out, g_wout, g_w1, g_w2, **comm):
    S, D = x.shape
    tm = min(TM, S)
    F = g_w1.shape[2] * NQ

    def body(x_ref, p_ref, mod_ref, gf_ref, bo_ref, wo_ref, w1_ref, w2_ref,
             y_ref, x1_ref, h2_ref, a_ref, z_ref, x2_ref):
        y = _dot(p_ref[...], wo_ref[...].reshape(D, D)) + bo_ref[...]
        y_ref[...] = y.astype(BF16)
        x1 = x_ref[...] + mod_ref[2:3, :] * y
        x1_ref[...] = x1
        _ffn_fwd_inner(x1, mod_ref, gf_ref, w1_ref, w2_ref, h2_ref, a_ref, z_ref, x2_ref)

    tile = pl.BlockSpec((tm, D), lambda i: (i, 0))
    row = pl.BlockSpec((1, D), lambda i: (0, 0))
    return _pcall(
        body, name="lru_mix_ffn_fwd", grid=(S // tm,),
        out_shape=(jax.ShapeDtypeStruct((S, D), BF16),) + _ffn_out_shapes(S, D, F),
        in_specs=[tile, tile, pl.BlockSpec((8, D), lambda i: (0, 0)), row, row,
                  _wspec(g_wout), _wspec(g_w1), _wspec(g_w2)],
        out_specs=(tile, tile, tile, pl.BlockSpec((tm, F), lambda i: (i, 0)), tile, tile),
        operands=(x, p, mod_l, g_ffn, b_out, g_wout, g_w1, g_w2), **comm)


def _window_vec(D):
    gd = D // len(POOL_WINDOWS)
    lane = lax.broadcasted_iota(jnp.int32, (1, D), 1)
    w = jnp.full((1, D), float(POOL_WINDOWS[0]), F32)
    for g in range(1, len(POOL_WINDOWS)):
        w = jnp.where(lane >= g * gd, float(POOL_WINDOWS[g]), w)
    return w


def _pool_mix_ffn_fwd(x, mod_l, g_mix, pw, ps, g_ffn, g_w1, g_w2, **comm):
    S, D = x.shape
    tm = min(TP, S)
    F = g_w1.shape[2] * NQ
    gd = D // len(POOL_WINDOWS)
    n = tm + 24

    def body(x_ref, xh_ref, mod_ref, gm_ref, pw_ref, ps_ref, gf_ref, w1_ref, w2_ref,
             pl_ref, x1_ref, h2_ref, a_ref, z_ref, x2_ref, ext, b1, b2):
        i = pl.program_id(0)
        g, sc, sh = gm_ref[...], mod_ref[1:2, :], mod_ref[0:1, :]
        xv = x_ref[...]
        h = _rms_fwd(xv, g, sc, sh)[0]
        hh = _rms_fwd(xh_ref[...], g, sc, sh)[0]
        zeros8 = jnp.zeros((8, D), F32)
        ext[0:8, :] = zeros8
        b1[0:8, :] = zeros8
        b2[0:8, :] = zeros8
        ext[8:24, :] = jnp.where(i > 0, hh, 0.0)
        ext[pl.ds(24, tm), :] = h
        m = n - 8
        b1[pl.ds(8, m), :] = ext[pl.ds(8, m), :] + ext[pl.ds(7, m), :]
        b2[pl.ds(8, m), gd:] = b1[pl.ds(8, m), gd:] + b1[pl.ds(6, m), gd:]
        b1[pl.ds(8, m), 2 * gd:] = b2[pl.ds(8, m), 2 * gd:] + b2[pl.ds(4, m), 2 * gd:]
        b2[pl.ds(8, m), 3 * gd:] = b1[pl.ds(8, m), 3 * gd:] + b1[pl.ds(0, m), 3 * gd:]
        wsum = jnp.concatenate([b1[pl.ds(24, tm), 0:gd], b2[pl.ds(24, tm), gd:2 * gd],
                                b1[pl.ds(24, tm), 2 * gd:3 * gd], b2[pl.ds(24, tm), 3 * gd:]], axis=1)
        t1 = (lax.broadcasted_iota(jnp.int32, (tm, 1), 0) + (i * tm + 1)).astype(F32)
        cnt = jnp.minimum(t1, _window_vec(D))
        pooled = (wsum / cnt - h).astype(BF16)
        pl_ref[...] = pooled
        y = _heads_dot(pooled, pw_ref, gd) * ps_ref[...]
        x1 = xv + mod_ref[2:3, :] * y
        x1_ref[...] = x1
        _ffn_fwd_inner(x1, mod_ref, gf_ref, w1_ref, w2_ref, h2_ref, a_ref, z_ref, x2_ref)

    tile = pl.BlockSpec((tm, D), lambda i: (i, 0))
    halo = pl.BlockSpec((16, D), lambda i: (jnp.maximum(i * (tm // 16) - 1, 0), 0))
    row = pl.BlockSpec((1, D), lambda i: (0, 0))
    return _pcall(
        body, name="pool_mix_ffn_fwd", grid=(S // tm,),
        out_shape=(jax.ShapeDtypeStruct((S, D), BF16),) + _ffn_out_shapes(S, D, F),
        in_specs=[tile, halo, pl.BlockSpec((8, D), lambda i: (0, 0)), row,
                  pl.BlockSpec((len(POOL_WINDOWS), gd, gd), lambda i: (0, 0, 0)), row, row,
                  _wspec(g_w1), _wspec(g_w2)],
        out_specs=(tile, tile, tile, pl.BlockSpec((tm, F), lambda i: (i, 0)), tile, tile),
        scratch_shapes=[pltpu.VMEM((n, D), F32), pltpu.VMEM((n, D), F32), pltpu.VMEM((n, D), F32)],
        operands=(x, x, mod_l, g_mix, pw, ps, g_ffn, g_w1, g_w2), **comm)


def _final_loss_bwd(x, g, target):
    S, D = x.shape
    tm = min(TM, S)

    def body(x_ref, g_ref, t_ref, dx_ref, acc_ref):
        _zero_first(acc_ref)
        xv = x_ref[...]
        gv = g_ref[...]
        r = lax.rsqrt(jnp.mean(xv * xv, axis=-1, keepdims=True) + EPS)
        xhat = xv * r
        err = xhat * gv - t_ref[...]
        acc_ref[0:1, :] += jnp.sum(err * err, axis=0, keepdims=True)
        dy = err * (1.0 / D)
        acc_ref[1:2, :] += jnp.sum(dy * xhat, axis=0, keepdims=True)
        dxh = dy * gv
        dx_ref[...] = r * (dxh - xhat * jnp.mean(dxh * xhat, axis=-1, keepdims=True))

    tile = pl.BlockSpec((tm, D), lambda i: (i, 0))
    return pl.pallas_call(
        body, name="final_loss_bwd", grid=(S // tm,),
        out_shape=(jax.ShapeDtypeStruct((S, D), F32), jax.ShapeDtypeStruct((8, D), F32)),
        in_specs=[tile, pl.BlockSpec((1, D), lambda i: (0, 0)), tile],
        out_specs=(tile, pl.BlockSpec((8, D), lambda i: (0, 0))),
        compiler_params=_cparams("arbitrary"),
    )(x, g, target)


def _ffn_bwd(dx2, x1, a, z, mod_l, g_ffn, g_w1, g_w2, **comm):
    S, D = dx2.shape
    F = a.shape[1]
    f4 = F // NQ
    tm = min(TM, S)

    def body(dx2_ref, x1_ref, a_ref, z_ref, mod_ref, gf_ref, w1_ref, w2_ref, dx1_ref, du_ref, dz_ref, acc_ref):
        _zero_first(acc_ref)
        dx2v = dx2_ref[...]
        acc_ref[5:6, :] += jnp.sum(dx2v * z_ref[...].astype(F32), axis=0, keepdims=True)
        dzb = (dx2v * mod_ref[5:6, :]).astype(BF16)
        dz_ref[...] = dzb
        dh2 = jnp.zeros((tm, D), F32)
        for q in range(NQ):
            av = a_ref[:, q * f4:(q + 1) * f4].astype(F32)
            du = (_dot_nt(dzb, w2_ref[q]) * (2.0 * av)).astype(BF16)
            du_ref[:, q * f4:(q + 1) * f4] = du
            dh2 = dh2 + _dot_nt(du, w1_ref[q])
        g, sc = gf_ref[...], mod_ref[4:5, :]
        _, xhat, r, n = _rms_fwd(x1_ref[...], g, sc, mod_ref[3:4, :])
        dx, dsh, dsc, dg = _rms_bwd(dh2, xhat, r, n, g, sc)
        acc_ref[3:4, :] += dsh
        acc_ref[4:5, :] += dsc
        acc_ref[7:8, :] += dg
        dx1_ref[...] = dx2v + dx

    tile = pl.BlockSpec((tm, D), lambda i: (i, 0))
    wide = pl.BlockSpec((tm, F), lambda i: (i, 0))
    return _pcall(
        body, name="ffn_bwd", grid=(S // tm,),
        out_shape=(jax.ShapeDtypeStruct((S, D), F32), jax.ShapeDtypeStruct((S, F), BF16),
                   jax.ShapeDtypeStruct((S, D), BF16), jax.ShapeDtypeStruct((8, D), F32)),
        in_specs=[tile, tile, wide, tile, pl.BlockSpec((8, D), lambda i: (0, 0)), pl.BlockSpec((1, D), lambda i: (0, 0)),
                  _wspec(g_w1), _wspec(g_w2)],
        out_specs=(tile, wide, tile, pl.BlockSpec((8, D), lambda i: (0, 0))),
        operands=(dx2, x1, a, z, mod_l, g_ffn, g_w1, g_w2), **comm)


def _dw_blocked(a, b, by_rows, square_a, name):
    S = a.shape[0]
    tk = min(TK, S)
    nk = S // tk
    if by_rows:
        bm, bn = a.shape[1] // NQ, b.shape[1]
        a_map, b_map = (lambda q, k: (k, q)), (lambda q, k: (k, 0))
    else:
        bm, bn = a.shape[1], b.shape[1] // NQ
        a_map, b_map = (lambda q, k: (k, 0)), (lambda q, k: (k, q))

    def body(a_ref, b_ref, o_ref, acc):
        k = pl.program_id(1)

        @pl.when(k == 0)
        def _():
            acc[...] = jnp.zeros_like(acc)

        av = a_ref[...]
        if square_a:
            av = av * av
        acc[...] += _dot_tn(av, b_ref[...])

        @pl.when(k == nk - 1)
        def _():
            o_ref[...] = acc[...].astype(o_ref.dtype)

    return pl.pallas_call(
        body, name=name, grid=(NQ, nk),
        out_shape=jax.ShapeDtypeStruct((NQ, bm, bn), BF16),
        in_specs=[pl.BlockSpec((tk, bm), a_map), pl.BlockSpec((tk, bn), b_map)],
        out_specs=pl.BlockSpec((None, bm, bn), lambda q, k: (q, 0, 0)),
        scratch_shapes=[pltpu.VMEM((bm, bn), F32)],
        compiler_params=_cparams("parallel", "arbitrary"),
    )(a, b)


def _dw_whole(a, bs, name):
    S, M = a.shape
    N = bs[0].shape[1]
    tk = min(TK, S)
    nk = S // tk
    nb = len(bs)

    def body(*refs):
        a_ref, b_refs, o_refs, accs = refs[0], refs[1:1 + nb], refs[1 + nb:1 + 2 * nb], refs[1 + 2 * nb:]
        k = pl.program_id(0)

        @pl.when(k == 0)
        def _():
            for acc in accs:
                acc[...] = jnp.zeros_like(acc)

        av = a_ref[...]
        for b_ref, acc in zip(b_refs, accs):
            acc[...] += _dot_tn(av, b_ref[...])

        @pl.when(k == nk - 1)
        def _():
            for o_ref, acc in zip(o_refs, accs):
                o_ref[...] = acc[...].reshape(NQ, M // NQ, N).astype(o_ref.dtype)

    return pl.pallas_call(
        body, name=name, grid=(nk,),
        out_shape=tuple(jax.ShapeDtypeStruct((NQ, M // NQ, N), BF16) for _ in bs),
        in_specs=[pl.BlockSpec((tk, M), lambda k: (k, 0))] + [pl.BlockSpec((tk, N), lambda k: (k, 0)) for _ in bs],
        out_specs=tuple(pl.BlockSpec((NQ, M // NQ, N), lambda k: (0, 0, 0)) for _ in bs),
        scratch_shapes=[pltpu.VMEM((M, N), F32) for _ in bs],
        compiler_params=_cparams("arbitrary"),
    )(a, *bs)


def _lru_out_bwd(dx1, y, mod_l, g_wout):
    S, D = dx1.shape
    tm = min(TM, S)

    def body(dx1_ref, y_ref, mod_ref, wo_ref, dy_ref, dp_ref, acc_ref):
        _zero_first(acc_ref)
        dx1v = dx1_ref[...]
        acc_ref[2:3, :] += jnp.sum(dx1v * y_ref[...].astype(F32), axis=0, keepdims=True)
        dy = dx1v * mod_ref[2:3, :]
        acc_ref[3:4, :] += jnp.sum(dy, axis=0, keepdims=True)
        dyb = dy.astype(BF16)
        dy_ref[...] = dyb
        dp_ref[...] = _dot_nt(dyb, wo_ref[...].reshape(D, D))

    tile = pl.BlockSpec((tm, D), lambda i: (i, 0))
    return _pcall(
        body, name="lru_out_bwd", grid=(S // tm,),
        out_shape=(jax.ShapeDtypeStruct((S, D), BF16), jax.ShapeDtypeStruct((S, D), F32), jax.ShapeDtypeStruct((8, D), F32)),
        in_specs=[tile, tile, pl.BlockSpec((8, D), lambda i: (0, 0)), _wspec(g_wout)],
        out_specs=(tile, tile, pl.BlockSpec((8, D), lambda i: (0, 0))),
        operands=(dx1, y, mod_l, g_wout))


def _lru_scan_bwd(dp, xr0, gb, hs, cw, cb, wa, ba, wx, bx, lam, **comm):
    S, W = xr0.shape
    tt = min(TT, S)
    nb = S // tt
    hd = W // HEADS

    def body(dp_ref, xr_ref, xrh_ref, gb_ref, hs_ref, hsh_ref, cw_ref, cb_ref, wa_ref, ba_ref, wx_ref, bx_ref, lam_ref,
             dgb_ref, dxr_ref, sm_ref, dwa_ref, dwx_ref, xext, hext, qext, dext, a_s, b_s, qc, dc):
        i = pl.program_id(0)
        blk = nb - 1 - i

        @pl.when(i == 0)
        def _():
            sm_ref[...] = jnp.zeros_like(sm_ref)
            dwa_ref[...] = jnp.zeros_like(dwa_ref)
            dwx_ref[...] = jnp.zeros_like(dwx_ref)
            qc[...] = jnp.zeros_like(qc)
            dc[...] = jnp.zeros_like(dc)

        xext[0:8, :] = jnp.where(blk > 0, xrh_ref[...], 0.0)
        xext[pl.ds(8, tt), :] = xr_ref[...]
        hext[0:8, :] = jnp.where(blk > 0, hsh_ref[...], 0.0)
        hext[pl.ds(8, tt), :] = hs_ref[...]
        cw = cw_ref[...]
        lam = lam_ref[...]
        xc = _conv_taps(xext, cw, tt) + cb_ref[...]
        xcb, gate_r, gate_i, ls, a, mult = _lru_gates(xc, wa_ref, ba_ref[...], wx_ref, bx_ref[...], lam, hd)

        gbv = gb_ref[...]
        gate, th = _gelu(gbv)
        dpv = dp_ref[...]
        dgb = dpv * hs_ref[...] * _gelu_grad(gbv, th)
        dgb_ref[...] = dgb.astype(BF16)
        sm_ref[9:10, :] += jnp.sum(dgb, axis=0, keepdims=True)
        dhs = dpv * gate

        a_s[...] = a
        b_s[...] = a * dhs
        qext[pl.ds(tt, 8), :] = qc[...]
        row = lax.broadcasted_iota(jnp.int32, (8, W), 0)

        def step(k, _):
            off = pl.multiple_of((tt // 8 - 1 - k) * 8, 8)
            A = a_s[pl.ds(off, 8), :]
            B = b_s[pl.ds(off, 8), :]
            for d in (1, 2, 4):
                keep = row < 8 - d
                Bs = jnp.where(keep, pltpu.roll(B, 8 - d, 0), 0.0)
                As = jnp.where(keep, pltpu.roll(A, 8 - d, 0), 1.0)
                B = B + A * Bs
                A = A * As
            Q = B + A * qc[...]
            qext[pl.ds(off, 8), :] = Q
            qc[...] = jnp.broadcast_to(Q[0:1, :], (8, W))
            return 0

        lax.fori_loop(0, tt // 8, step, 0)
        gsc = dhs + qext[pl.ds(1, tt), :]
        da = gsc * hext[pl.ds(7, tt), :]
        t1 = gsc * xc
        dmult = t1 * gate_i
        dgate_i = t1 * mult
        dxc = gsc * (mult * gate_i)
        dlog_a = da * a - dmult * (a * a) / mult
        dgate_r = dlog_a * (LRU_C * ls)
        sm_ref[7:8, :] += jnp.sum(dlog_a * (LRU_C * gate_r), axis=0, keepdims=True)
        dga = dgate_r * gate_r * (1.0 - gate_r)
        dgx = dgate_i * gate_i * (1.0 - gate_i)
        sm_ref[5:6, :] += jnp.sum(dga, axis=0, keepdims=True)
        sm_ref[6:7, :] += jnp.sum(dgx, axis=0, keepdims=True)
        dgab = dga.astype(BF16)
        dgxb = dgx.astype(BF16)
        dxc = dxc + _heads_dot(dgab, wa_ref, hd, nt=True) + _heads_dot(dgxb, wx_ref, hd, nt=True)
        for h in range(HEADS):
            sl = slice(h * hd, (h + 1) * hd)
            dwa_ref[h] += _dot_tn(xcb[:, sl], dgab[:, sl])
            dwx_ref[h] += _dot_tn(xcb[:, sl], dgxb[:, sl])
        sm_ref[4:5, :] += jnp.sum(dxc, axis=0, keepdims=True)
        for k in range(CONV_W):
            sm_ref[k:k + 1, :] += jnp.sum(dxc * xext[pl.ds(8 - (CONV_W - 1) + k, tt), :], axis=0, keepdims=True)
        dext[pl.ds(0, tt), :] = dxc
        dext[pl.ds(tt, 8), :] = dc[...]
        dxr = cw[0:1, :] * dext[pl.ds(CONV_W - 1, tt), :]
        for k in range(1, CONV_W):
            dxr = dxr + cw[k:k + 1, :] * dext[pl.ds(CONV_W - 1 - k, tt), :]
        dc[...] = dext[0:8, :]
        sm_ref[8:9, :] += jnp.sum(dxr, axis=0, keepdims=True)
        dxr_ref[...] = dxr.astype(BF16)

        @pl.when(i == nb - 1)
        def _():
            sm_ref[7:8, :] = sm_ref[7:8, :] * jax.nn.sigmoid(-lam)

    rev = lambda i: (nb - 1 - i, 0)
    tile = pl.BlockSpec((tt, W), rev)
    halo = pl.BlockSpec((8, W), lambda i: (jnp.maximum((nb - 1 - i) * (tt // 8) - 1, 0), 0))
    row = pl.BlockSpec((1, W), lambda i: (0, 0))
    wblk = pl.BlockSpec((HEADS, hd, hd), lambda i: (0, 0, 0))
    return _pcall(
        body, name="lru_scan_bwd", grid=(nb,),
        out_shape=(jax.ShapeDtypeStruct((S, W), BF16), jax.ShapeDtypeStruct((S, W), BF16),
                   jax.ShapeDtypeStruct((16, W), F32), jax.ShapeDtypeStruct((HEADS, hd, hd), F32),
                   jax.ShapeDtypeStruct((HEADS, hd, hd), F32)),
        in_specs=[tile, tile, halo, tile, tile, halo, pl.BlockSpec((CONV_W, W), lambda i: (0, 0)), row,
                  wblk, row, wblk, row, row],
        out_specs=(tile, tile, pl.BlockSpec((16, W), lambda i: (0, 0)), wblk, wblk),
        scratch_shapes=[pltpu.VMEM((tt + 8, W), F32), pltpu.VMEM((tt + 8, W), F32), pltpu.VMEM((tt + 8, W), F32),
                        pltpu.VMEM((tt + 8, W), F32), pltpu.VMEM((tt, W), F32), pltpu.VMEM((tt, W), F32),
                        pltpu.VMEM((8, W), F32), pltpu.VMEM((8, W), F32)],
        operands=(dp, xr0, xr0, gb, hs, hs, cw, cb, wa, ba, wx, bx, lam), **comm)


def _lru_in_bwd(dxr, dgb, dx1, x, mod_l, g_mix, g_wy, g_win, **comm):
    S, D = x.shape
    tm = min(TM, S)

    def body(dxr_ref, dgb_ref, dx1_ref, x_ref, mod_ref, g_ref, wy_ref, win_ref, dx_ref, acc_ref):
        _zero_first(acc_ref)
        dh = _dot_nt(dxr_ref[...], win_ref[...].reshape(D, D)) + _dot_nt(dgb_ref[...], wy_ref[...].reshape(D, D))
        g, sc = g_ref[...], mod_ref[1:2, :]
        _, xhat, r, n = _rms_fwd(x_ref[...], g, sc, mod_ref[0:1, :])
        dx, dsh, dsc, dg = _rms_bwd(dh, xhat, r, n, g, sc)
        acc_ref[0:1, :] += dsh
        acc_ref[1:2, :] += dsc
        acc_ref[6:7, :] += dg
        dx_ref[...] = dx1_ref[...] + dx

    tile = pl.BlockSpec((tm, D), lambda i: (i, 0))
    return _pcall(
        body, name="lru_in_bwd", grid=(S // tm,),
        out_shape=(jax.ShapeDtypeStruct((S, D), F32), jax.ShapeDtypeStruct((8, D), F32)),
        in_specs=[tile, tile, tile, tile, pl.BlockSpec((8, D), lambda i: (0, 0)), pl.BlockSpec((1, D), lambda i: (0, 0)),
                  _wspec(g_wy), _wspec(g_win)],
        out_specs=(tile, pl.BlockSpec((8, D), lambda i: (0, 0))),
        operands=(dxr, dgb, dx1, x, mod_l, g_mix, g_wy, g_win), **comm)


def _pool_bwd(dx1, x, pooled, mod_l, g_mix, pw, ps):
    S, D = x.shape
    tm = min(TP, S)
    nb = S // tm
    ng = len(POOL_WINDOWS)
    gd = D // ng
    n = tm + 24

    def body(dx1_ref, dxh_ref, x_ref, pl_ref, mod_ref, gm_ref, pw_ref, ps_ref, dx_ref, acc_ref, dpw_ref, ext, b1, b2):
        i = pl.program_id(0)

        @pl.when(i == 0)
        def _():
            acc_ref[...] = jnp.zeros_like(acc_ref)
            dpw_ref[...] = jnp.zeros_like(dpw_ref)

        gt, psv = mod_ref[2:3, :], ps_ref[...]
        wvec = _window_vec(D)
        dx1v = dx1_ref[...]
        pooled = pl_ref[...]
        mixed = _heads_dot(pooled, pw_ref, gd)
        acc_ref[2:3, :] += jnp.sum(dx1v * (mixed * psv), axis=0, keepdims=True)
        dy = dx1v * gt
        acc_ref[3:4, :] += jnp.sum(dy * mixed, axis=0, keepdims=True)
        dmix = (dy * psv).astype(BF16)
        for gi in range(ng):
            sl = slice(gi * gd, (gi + 1) * gd)
            dpw_ref[gi] += _dot_tn(pooled[:, sl], dmix[:, sl])
        dpooled = _heads_dot(dmix, pw_ref, gd, nt=True)
        dmix_h = (dxh_ref[...] * gt * psv).astype(BF16)
        dpooled_h = jnp.where(i < nb - 1, _heads_dot(dmix_h, pw_ref, gd, nt=True), 0.0)
        t1 = (lax.broadcasted_iota(jnp.int32, (tm, 1), 0) + (i * tm + 1)).astype(F32)
        t1h = (lax.broadcasted_iota(jnp.int32, (16, 1), 0) + ((i + 1) * tm + 1)).astype(F32)
        zeros8 = jnp.zeros((8, D), F32)
        ext[pl.ds(0, tm), :] = dpooled / jnp.minimum(t1, wvec)
        ext[pl.ds(tm, 16), :] = dpooled_h / jnp.minimum(t1h, wvec)
        ext[pl.ds(tm + 16, 8), :] = zeros8
        b1[pl.ds(tm + 16, 8), :] = zeros8
        b2[pl.ds(tm + 16, 8), :] = zeros8
        m = n - 8
        b1[pl.ds(0, m), :] = ext[pl.ds(0, m), :] + ext[pl.ds(1, m), :]
        b2[pl.ds(0, m), gd:] = b1[pl.ds(0, m), gd:] + b1[pl.ds(2, m), gd:]
        b1[pl.ds(0, m), 2 * gd:] = b2[pl.ds(0, m), 2 * gd:] + b2[pl.ds(4, m), 2 * gd:]
        b2[pl.ds(0, m), 3 * gd:] = b1[pl.ds(0, m), 3 * gd:] + b1[pl.ds(8, m), 3 * gd:]
        wsum = jnp.concatenate([b1[pl.ds(0, tm), 0:gd], b2[pl.ds(0, tm), gd:2 * gd],
                                b1[pl.ds(0, tm), 2 * gd:3 * gd], b2[pl.ds(0, tm), 3 * gd:]], axis=1)
        dh = wsum - dpooled
        g, sc = gm_ref[...], mod_ref[1:2, :]
        _, xhat, r, nn = _rms_fwd(x_ref[...], g, sc, mod_ref[0:1, :])
        dx, dsh, dsc, dg = _rms_bwd(dh, xhat, r, nn, g, sc)
        acc_ref[0:1, :] += dsh
        acc_ref[1:2, :] += dsc
        acc_ref[6:7, :] += dg
        dx_ref[...] = dx1v + dx

    tile = pl.BlockSpec((tm, D), lambda i: (i, 0))
    halo = pl.BlockSpec((16, D), lambda i: (jnp.minimum((i + 1) * (tm // 16), S // 16 - 1), 0))
    row = pl.BlockSpec((1, D), lambda i: (0, 0))
    wblk = pl.BlockSpec((ng, gd, gd), lambda i: (0, 0, 0))
    return pl.pallas_call(
        body, name="pool_bwd", grid=(nb,),
        out_shape=(jax.ShapeDtypeStruct((S, D), F32), jax.ShapeDtypeStruct((8, D), F32),
                   jax.ShapeDtypeStruct((ng, gd, gd), F32)),
        in_specs=[tile, halo, tile, tile, pl.BlockSpec((8, D), lambda i: (0, 0)), row, wblk, row],
        out_specs=(tile, pl.BlockSpec((8, D), lambda i: (0, 0)), wblk),
        scratch_shapes=[pltpu.VMEM((n, D), F32), pltpu.VMEM((n, D), F32), pltpu.VMEM((n, D), F32)],
        compiler_params=_cparams("arbitrary"),
    )(dx1, dx1, x, pooled, mod_l, g_mix, pw, ps)


def _shard_to_rows(w, D):
    return w.reshape(-1, D)


def _blockdiag_full(gq, na, hd):
    return gq.reshape(NQ, na, HEADS, hd // NQ, hd).transpose(1, 2, 0, 3, 4).reshape(na, HEADS, hd, hd)


def _blockdiag_by_chip(dw, D):
    na, _, hd, _ = dw.shape
    return dw.reshape(na, HEADS, NQ, hd // NQ, hd).transpose(2, 0, 1, 3, 4).reshape(NQ, -1, D)


def kernel(x, c, w_mod, b_mod, norm_mix_g, norm_ffn_g, lru_w_y, lru_b_y, lru_w_in, lru_b_in, lru_conv_w, lru_conv_b, lru_w_a, lru_b_a, lru_w_x, lru_b_x, lru_lambda, lru_w_out, lru_b_out, pool_w, pool_scale, ffn_w1, ffn_w2, final_norm_g, loss_target, m_w_mod, m_b_mod, m_norm_mix_g, m_norm_ffn_g, m_lru_w_y, m_lru_b_y, m_lru_w_in, m_lru_b_in, m_lru_conv_w, m_lru_conv_b, m_lru_w_a, m_lru_b_a, m_lru_w_x, m_lru_b_x, m_lru_lambda, m_lru_w_out, m_lru_b_out, m_pool_w, m_pool_scale, m_ffn_w1, m_ffn_w2, m_final_norm_g, v_w_mod, v_b_mod, v_norm_mix_g, v_norm_ffn_g, v_lru_w_y, v_lru_b_y, v_lru_w_in, v_lru_b_in, v_lru_conv_w, v_lru_conv_b, v_lru_w_a, v_lru_b_a, v_lru_w_x, v_lru_b_x, v_lru_lambda, v_lru_w_out, v_lru_b_out, v_pool_w, v_pool_scale, v_ffn_w1, v_ffn_w2, v_final_norm_g):
    S, D = x.shape[1], x.shape[2]
    L = w_mod.shape[0]
    NA = lru_w_y.shape[0]
    NB = pool_w.shape[0]
    F = ffn_w1.shape[2] * NQ
    f4 = F // NQ
    hd = D // HEADS
    Cs = w_mod.shape[2]
    assert L == DEPTH and Cs * NQ == N_MOD * D and D % 1024 == 0
    x2d = x.reshape(S, D)
    tgt = loss_target.reshape(S, D)
    q = 2 * lax.axis_index("x") + lax.axis_index("y")

    big = [ffn_w1, ffn_w2, lru_w_y, lru_w_in, lru_w_out, lru_w_a, lru_w_x, pool_w]
    rows = [int(w.size) // D for w in big]
    offs = [sum(rows[:k]) for k in range(len(big))]
    O_W1, O_W2, O_WY, O_WIN, O_WOUT, O_WA, O_WX, O_PW = offs
    R = sum(rows)
    dq = D // NQ
    s_w1 = [ffn_w1[i].astype(BF16) for i in range(L)]
    s_w2 = [ffn_w2[i].astype(BF16) for i in range(L)]
    s_wy = [lru_w_y[j].astype(BF16) for j in range(NA)]
    s_win = [lru_w_in[j].astype(BF16) for j in range(NA)]
    s_wout = [lru_w_out[j].astype(BF16) for j in range(NA)]
    s_tiny = jnp.concatenate([_shard_to_rows(w, D) for w in (lru_w_a, lru_w_x, pool_w)], axis=0).astype(BF16)

    cshard = lru_conv_w.reshape(-1)
    small_fwd = jnp.concatenate([c.reshape(-1), cshard, lru_b_a.reshape(-1), lru_b_x.reshape(-1),
                                 pool_scale.reshape(-1)])
    small_fwd = jnp.pad(small_fwd, (0, 8 * D - small_fwd.shape[0])).reshape(8, D)

    g_w1, g_w2 = [None] * L, [None] * L
    g_wy, g_win, g_wout = [None] * NA, [None] * NA, [None] * NA
    SG, g_wy[0], g_win[0] = _comm_only("gather_first", small=small_fwd, gathers=(s_wy[0], s_win[0]))
    SGf = SG.reshape(NDEV, 8 * D)
    c_all = SGf[:, :D]
    SGq = SGf.reshape(NQ, 2, 8 * D)[:, 0]
    o = D
    n_cw = NA * CONV_W * D // NQ
    conv_w_full = SGq[:, o:o + n_cw].reshape(NQ, NA, CONV_W, D // NQ).transpose(1, 2, 0, 3).reshape(NA, CONV_W, D)
    o += n_cw
    n_b = NA * HEADS * hd // NQ
    b_a_full = SGq[:, o:o + n_b].reshape(NQ, NA, HEADS, hd // NQ).transpose(1, 2, 0, 3).reshape(NA, 1, D)
    o += n_b
    b_x_full = SGq[:, o:o + n_b].reshape(NQ, NA, HEADS, hd // NQ).transpose(1, 2, 0, 3).reshape(NA, 1, D)
    o += n_b
    n_ps = NB * D // NQ
    pool_scale_full = SGq[:, o:o + n_ps].reshape(NQ, NB, D // NQ).transpose(1, 0, 2).reshape(NB, 1, D)


    b_mod_sh = lax.dynamic_slice_in_dim(b_mod, q * Cs, Cs, axis=1).reshape(L, 1, Cs)
    modpart = _mod_part(c_all, w_mod, b_mod_sh)
    modq = _exchange_mod(modpart.transpose(1, 0, 2))
    mod = modq.transpose(1, 0, 2).reshape(L, N_MOD, D)
    mod = jnp.pad(mod, ((0, 0), (0, 8 - N_MOD), (0, 0)))

    saved = []
    xcur = x2d
    for i in range(L):
        j = i // 2
        gm = norm_mix_g[i].reshape(1, D)
        gf = norm_ffn_g[i].reshape(1, D)
        if i % 2 == 0:
            if g_wout[j] is None:
                h, gb, xr0, g_wout[j], g_tiny = _lru_in_fwd(xcur, mod[i], gm, g_wy[j], g_win[j], lru_b_y[j].reshape(1, D),
                                                            lru_b_in[j].reshape(1, D), gathers=(s_wout[j], s_tiny))
                wa_full = _blockdiag_full(g_tiny[:, :rows[5]], NA, hd)
                wx_full = _blockdiag_full(g_tiny[:, rows[5]:rows[5] + rows[6]], NA, hd)
                pw_full = _blockdiag_full(g_tiny[:, rows[5] + rows[6]:], NB, hd)
            else:
                h, gb, xr0 = _lru_in_fwd(xcur, mod[i], gm, g_wy[j], g_win[j], lru_b_y[j].reshape(1, D),
                                         lru_b_in[j].reshape(1, D))
            hs, p, g_w1[i], g_w2[i] = _lru_scan_fwd(xr0, gb, conv_w_full[j], lru_conv_b[j].reshape(1, D), wa_full[j],
                                                    b_a_full[j], wx_full[j], b_x_full[j], lru_lambda[j].reshape(1, D),
                                                    gathers=(s_w1[i], s_w2[i]))
            y, x1, h2, a, z, x2, g_w1[i + 1], g_w2[i + 1] = _lru_mix_ffn_fwd(
                xcur, p, mod[i], gf, lru_b_out[j].reshape(1, D), g_wout[j], g_w1[i], g_w2[i],
                gathers=(s_w1[i + 1], s_w2[i + 1]))
            saved.append(dict(x=xcur, h=h, gb=gb, xr0=xr0, hs=hs, p=p, y=y, x1=x1, h2=h2, a=a, z=z))
        else:
            if j + 1 < NA:
                pooled, x1, h2, a, z, x2, g_wy[j + 1], g_win[j + 1], g_wout[j + 1] = _pool_mix_ffn_fwd(
                    xcur, mod[i], gm, pw_full[j], pool_scale_full[j], gf, g_w1[i], g_w2[i],
                    gathers=(s_wy[j + 1], s_win[j + 1], s_wout[j + 1]))
            else:
                pooled, x1, h2, a, z, x2 = _pool_mix_ffn_fwd(xcur, mod[i], gm, pw_full[j], pool_scale_full[j], gf,
                                                             g_w1[i], g_w2[i])
            saved.append(dict(x=xcur, pooled=pooled, x1=x1, h2=h2, a=a, z=z))
        xcur = x2

    dx, lacc = _final_loss_bwd(xcur, final_norm_g.reshape(1, D), tgt)
    loss = lax.psum(0.5 * jnp.sum(lacc[0]) / D, ("x", "y", "c"))
    d_final_g = lacc[1]

    ppack = lax.empty((R, D), F32)
    psib = lax.empty((R, D), F32)
    pending, summed = [], []

    def comm_args():
        kw = {}
        if pending:
            kw["scatters"] = tuple(dw for dw, _ in pending)
        if summed:
            kw["sib"] = (ppack, psib, tuple(summed))
        return kw

    def after_host(extra):
        nonlocal ppack, psib, pending, summed
        had_sib = bool(summed)
        summed = []
        for (dw, off), rb in zip(pending, extra[:len(pending)]):
            own = lax.dynamic_index_in_dim(dw, q, axis=0, keepdims=False)
            ppack = _sum_into(ppack, own, rb, off)
            summed.append((off, dw.shape[1]))
        if had_sib:
            psib = extra[len(pending)]
        pending = []

    dmod_rows = [None] * L
    dg_mix = [None] * L
    dg_ffn = [None] * L
    d_small = {}
    dwa_l, dwx_l, dpw_l = [None] * NA, [None] * NA, [None] * NB
    for i in reversed(range(L)):
        j = i // 2
        sv = saved[i]
        gm = norm_mix_g[i].reshape(1, D)
        gf = norm_ffn_g[i].reshape(1, D)
        outs = _ffn_bwd(dx, sv["x1"], sv["a"], sv["z"], mod[i], gf, g_w1[i], g_w2[i], **comm_args())
        dx1, du, dz, facc = outs[:4]
        after_host(outs[4:])
        pending.append((_dw_blocked(sv["h2"], du, False, False, "dw1"), O_W1 + i * D))
        pending.append((_dw_blocked(sv["a"], dz, True, True, "dw2"), O_W2 + i * f4))
        if i % 2 == 0:
            dyp, dp, oacc = _lru_out_bwd(dx1, sv["y"], mod[i], g_wout[j])
            pending.append((_dw_whole(sv["p"], [dyp], "dwout")[0], O_WOUT + j * dq))
            outs = _lru_scan_bwd(dp, sv["xr0"], sv["gb"], sv["hs"], conv_w_full[j], lru_conv_b[j].reshape(1, D),
                                 wa_full[j], b_a_full[j], wx_full[j], b_x_full[j], lru_lambda[j].reshape(1, D),
                                 **comm_args())
            dgb, dxr, sm, dwa, dwx = outs[:5]
            after_host(outs[5:])
            dwy, dwin = _dw_whole(sv["h"], [dgb, dxr], "dwy_dwin")
            pending.append((dwy, O_WY + j * dq))
            pending.append((dwin, O_WIN + j * dq))
            if i == 0:
                outs = _lru_in_bwd(dxr, dgb, dx1, sv["x"], mod[i], gm, g_wy[j], g_win[j], **comm_args())
                dx, macc = outs[:2]
                after_host(outs[2:])
            else:
                dx, macc = _lru_in_bwd(dxr, dgb, dx1, sv["x"], mod[i], gm, g_wy[j], g_win[j])
            dwa_l[j], dwx_l[j] = dwa, dwx
            d_small[("lru", j)] = (sm, oacc[3])
            dgt_m = oacc[2]
        else:
            dx, macc, dpw = _pool_bwd(dx1, sv["x"], sv["pooled"], mod[i], gm, pw_full[j], pool_scale_full[j])
            dpw_l[j] = dpw
            d_small[("pool", j)] = macc[3]
            dgt_m = macc[2]
        dmod_rows[i] = jnp.stack([macc[0], macc[1], dgt_m, facc[3], facc[4], facc[5]])
        dg_mix[i] = macc[6]
        dg_ffn[i] = facc[7]
    grad_x = dx.reshape(x.shape)

    tiny = jnp.concatenate([_blockdiag_by_chip(jnp.stack(dwa_l), D), _blockdiag_by_chip(jnp.stack(dwx_l), D),
                            _blockdiag_by_chip(jnp.stack(dpw_l), D)], axis=1).astype(BF16)
    pending.append((tiny, O_WA))

    lru_sm = [d_small[("lru", j)] for j in range(NA)]
    small_rows = [jnp.stack(dmod_rows).reshape(L * N_MOD, D), jnp.stack(dg_mix), jnp.stack(dg_ffn),
                  jnp.stack([s[0][9] for s in lru_sm]), jnp.stack([s[0][8] for s in lru_sm]),
                  jnp.stack([s[0][4] for s in lru_sm]), jnp.stack([s[0][7] for s in lru_sm]),
                  jnp.stack([s[1] for s in lru_sm]),
                  jnp.stack([s[0][0:CONV_W] for s in lru_sm]).reshape(NA * CONV_W, D),
                  jnp.stack([s[0][5] for s in lru_sm]), jnp.stack([s[0][6] for s in lru_sm]),
                  jnp.stack([d_small[("pool", j)] for j in range(NB)]), d_final_g.reshape(1, D)]
    small_g = jnp.concatenate(small_rows, axis=0)
    n_small = small_g.shape[0]
    assert n_small <= SMALL_ROWS
    small_g = jnp.pad(small_g, ((0, SMALL_ROWS - n_small), (0, 0)))

    outs = _comm_only("scatter_last", small=small_g, reduce_small=True, **comm_args())
    sg_all, sg_sum = outs[:2]
    after_host(outs[2:])
    psum_mine = ppack
    psum_sib = _comm_only("sibling_last", sib=(ppack, psib, tuple(summed)))[0]

    def big_update(w, m, v, off, name):
        shp = w.shape
        g, dl, m2, v2 = _adam_rows(w.reshape(-1, D), m.reshape(-1, D), v.reshape(-1, D), psum_mine, psum_sib, off, name)
        return g.reshape(shp), dl.reshape(shp), m2.reshape(shp), v2.reshape(shp)

    res = {}
    res["ffn_w1"] = big_update(ffn_w1, m_ffn_w1, v_ffn_w1, O_W1, "adam_w1")
    res["ffn_w2"] = big_update(ffn_w2, m_ffn_w2, v_ffn_w2, O_W2, "adam_w2")
    res["lru_w_y"] = big_update(lru_w_y, m_lru_w_y, v_lru_w_y, O_WY, "adam_wy")
    res["lru_w_in"] = big_update(lru_w_in, m_lru_w_in, v_lru_w_in, O_WIN, "adam_win")
    res["lru_w_out"] = big_update(lru_w_out, m_lru_w_out, v_lru_w_out, O_WOUT, "adam_wout")
    res["lru_w_a"] = big_update(lru_w_a, m_lru_w_a, v_lru_w_a, O_WA, "adam_wa")
    res["lru_w_x"] = big_update(lru_w_x, m_lru_w_x, v_lru_w_x, O_WX, "adam_wx")
    res["pool_w"] = big_update(pool_w, m_pool_w, v_pool_w, O_PW, "adam_pw")

    dmod_all = sg_all[:, :L * N_MOD, :].reshape(NDEV, L, N_MOD * D)
    dmod_sh = lax.dynamic_slice_in_dim(dmod_all, q * Cs, Cs, axis=2).transpose(1, 0, 2)
    res["w_mod"] = _wmod_update(c_all.T, dmod_sh, w_mod, m_w_mod, v_w_mod)

    r0 = 0

    def take(nrows):
        nonlocal r0
        out = sg_sum[r0:r0 + nrows]
        r0 += nrows
        return out

    g_b_mod = take(L * N_MOD).reshape(L, N_MOD * D)
    g_mix = take(L)
    g_ffn = take(L)
    g_b_y = take(NA)
    g_b_in = take(NA)
    g_conv_b = take(NA)
    g_lam = take(NA)
    g_b_out = take(NA)
    g_conv_w = lax.dynamic_slice_in_dim(take(NA * CONV_W).reshape(NA, CONV_W, D), q * (D // NQ), D // NQ, axis=2)
    g_b_a = lax.dynamic_slice_in_dim(take(NA).reshape(NA, HEADS, hd), q * (hd // NQ), hd // NQ, axis=2)
    g_b_x = lax.dynamic_slice_in_dim(take(NA).reshape(NA, HEADS, hd), q * (hd // NQ), hd // NQ, axis=2)
    g_ps = lax.dynamic_slice_in_dim(take(NB), q * (D // NQ), D // NQ, axis=1)
    g_fin = take(1).reshape(D)

    smalls = [("b_mod", b_mod, m_b_mod, v_b_mod, g_b_mod), ("norm_mix_g", norm_mix_g, m_norm_mix_g, v_norm_mix_g, g_mix),
              ("norm_ffn_g", norm_ffn_g, m_norm_ffn_g, v_norm_ffn_g, g_ffn),
              ("lru_b_y", lru_b_y, m_lru_b_y, v_lru_b_y, g_b_y), ("lru_b_in", lru_b_in, m_lru_b_in, v_lru_b_in, g_b_in),
              ("lru_conv_w", lru_conv_w, m_lru_conv_w, v_lru_conv_w, g_conv_w),
              ("lru_conv_b", lru_conv_b, m_lru_conv_b, v_lru_conv_b, g_conv_b),
              ("lru_b_a", lru_b_a, m_lru_b_a, v_lru_b_a, g_b_a), ("lru_b_x", lru_b_x, m_lru_b_x, v_lru_b_x, g_b_x),
              ("lru_lambda", lru_lambda, m_lru_lambda, v_lru_lambda, g_lam),
              ("lru_b_out", lru_b_out, m_lru_b_out, v_lru_b_out, g_b_out),
              ("pool_scale", pool_scale, m_pool_scale, v_pool_scale, g_ps),
              ("final_norm_g", final_norm_g, m_final_norm_g, v_final_norm_g, g_fin)]
    total = sum(int(s[1].size) for s in smalls)
    prow = -(-total // (8 * 128)) * 8

    def flat(k):
        f = jnp.concatenate([s[k].reshape(-1) for s in smalls])
        return jnp.pad(f, (0, prow * 128 - total)).reshape(prow, 128)

    dl_s, m_s, v_s = _adam_small(flat(1), flat(4), flat(2), flat(3))
    o = 0
    for name, w, _, _, g in smalls:
        sz = int(w.size)
        cut = lambda arr: arr.reshape(-1)[o:o + sz].reshape(w.shape)
        res[name] = (g.reshape(w.shape), cut(dl_s), cut(m_s), cut(v_s))
        o += sz

    order = ["w_mod", "b_mod", "norm_mix_g", "norm_ffn_g", "lru_w_y", "lru_b_y", "lru_w_in", "lru_b_in", "lru_conv_w",
             "lru_conv_b", "lru_w_a", "lru_b_a", "lru_w_x", "lru_b_x", "lru_lambda", "lru_w_out", "lru_b_out", "pool_w",
             "pool_scale", "ffn_w1", "ffn_w2", "final_norm_g"]
    return (loss, grad_x, *[res[n][0] for n in order], *[res[n][1] for n in order],
            *[res[n][2] for n in order], *[res[n][3] for n in order])
```

```python
import functools

import jax
import jax.numpy as jnp
from jax import lax
from jax.experimental import pallas as pl
from jax.experimental.pallas import tpu as pltpu

F32 = jnp.float32
BF16 = jnp.bfloat16
MESH = pl.DeviceIdType.MESH

NQ = 4
NDEV = 8
DEPTH = 4
N_MOD = 6
HEADS = 4
CONV_W = 4
POOL_WINDOWS = (2, 4, 8, 16)
LRU_C = 8.0
EPS = 1e-6
ADAM_LR, ADAM_B1, ADAM_B2, ADAM_EPS, ADAM_WD, ADAM_STEP = 0.001, 0.9, 0.999, 1e-08, 0.01, 10

TM = 512
TT = 256
TP = 256
TK = 2048
SMALL_ROWS = 64
FORWARD_STEPS = 4
VMEM_LIMIT = 60 * 1024 * 1024


def _cparams(*sem):
    return pltpu.CompilerParams(dimension_semantics=tuple(sem), vmem_limit_bytes=VMEM_LIMIT)


def _dot(a, b):
    return jnp.dot(a, b, preferred_element_type=F32)


def _dot_nt(a, b):
    return lax.dot_general(a, b, (((1,), (1,)), ((), ())), preferred_element_type=F32)


def _dot_tn(a, b):
    return lax.dot_general(a, b, (((0,), (0,)), ((), ())), preferred_element_type=F32)


def _resident(shape, index_map):
    return pl.BlockSpec(shape, index_map, pipeline_mode=pl.Buffered(1))


def _rms_fwd(x, g, sc, sh):
    r = lax.rsqrt(jnp.mean(x * x, axis=-1, keepdims=True) + EPS)
    xhat = x * r
    n = xhat * g
    return n * (1.0 + sc) + sh, xhat, r, n


def _rms_bwd(dh, xhat, r, n, g, sc):
    dsh = jnp.sum(dh, axis=0, keepdims=True)
    dsc = jnp.sum(dh * n, axis=0, keepdims=True)
    dn = dh * (1.0 + sc)
    dg = jnp.sum(dn * xhat, axis=0, keepdims=True)
    dxh = dn * g
    dx = r * (dxh - xhat * jnp.mean(dxh * xhat, axis=-1, keepdims=True))
    return dx, dsh, dsc, dg


_GELU_K = 0.7978845608028654
_GELU_C = 0.044715


def _gelu(x):
    t = jnp.tanh(_GELU_K * (x + _GELU_C * x * x * x))
    return 0.5 * x * (1.0 + t), t


def _gelu_grad(x, t):
    return 0.5 * (1.0 + t) + 0.5 * x * (1.0 - t * t) * (_GELU_K * (1.0 + 3.0 * _GELU_C * x * x))


def _neg_expm1(y, exp_y):
    series = -(y * (1.0 + y * (0.5 + y * (1.0 / 6.0))))
    return jnp.where(y > -(1.0 / 64.0), series, 1.0 - exp_y)


def _zero_first(ref):
    @pl.when(pl.program_id(0) == 0)
    def _():
        ref[...] = jnp.zeros_like(ref)


def _my_pos():
    return lax.axis_index("x"), lax.axis_index("y"), lax.axis_index("c")


def _dev_index(x, y, c):
    return 4 * x + 2 * y + c


def _chip_peers(x, y):
    return [(1 - x, y), (x, 1 - y), (1 - x, 1 - y)]


def _all_peers(x, y, c):
    return [(px, py, c) for (px, py) in _chip_peers(x, y)] + [(x, y, 1 - c)] + \
           [(px, py, 1 - c) for (px, py) in _chip_peers(x, y)]


def _comm_run(phase, x, y, c, gathers, scatters, sib, send, recv, loc):
    q = 2 * x + y
    peers = _chip_peers(x, y)
    sibling = (x, y, 1 - c)

    def rcopy(src, dst, s, dev):
        return pltpu.make_async_remote_copy(src, dst, send.at[s], recv.at[s], device_id=dev, device_id_type=MESH)

    s = 0
    for gi, (src, dst) in enumerate(gathers):
        half = src.shape[0] // 2
        mine, other = pl.ds(c * half, half), pl.ds((1 - c) * half, half)
        own = pltpu.make_async_copy(src, dst.at[q], loc.at[gi])
        if phase == "start":
            own.start()
        elif phase == "finish":
            own.wait()
        for (px, py) in peers:
            pq = 2 * px + py
            s_ici, s_fwd = s, s + 1
            s += 2
            if phase == "start":
                rcopy(src.at[mine], dst.at[q].at[mine], s_ici, (px, py, c)).start()
            elif phase == "forward":
                rcopy(src.at[mine], dst.at[pq].at[mine], s_ici, (px, py, c)).wait_recv()
                rcopy(dst.at[pq].at[mine], dst.at[pq].at[mine], s_fwd, sibling).start()
            else:
                rcopy(dst.at[pq].at[other], dst.at[pq].at[other], s_fwd, sibling).wait_recv()
                rcopy(src.at[mine], dst.at[q].at[mine], s_ici, (px, py, c)).wait_send()
                rcopy(dst.at[pq].at[mine], dst.at[pq].at[mine], s_fwd, sibling).wait_send()
    direct = []
    for (src, dst) in scatters:
        for k, (px, py) in enumerate(peers):
            direct.append((src.at[2 * px + py], dst.at[k], (px, py, c)))
    if sib is not None:
        src, dst, ranges = sib
        for (off, rows) in ranges:
            direct.append((src.at[pl.ds(off, rows)], dst.at[pl.ds(off, rows)], sibling))
    if phase == "start":
        for k, (a, b, dev) in enumerate(direct):
            rcopy(a, b, s + k, dev).start()
    elif phase == "finish":
        for k, (a, b, dev) in enumerate(direct):
            rcopy(a, b, s + k, dev).wait_recv()
        for k, (a, b, dev) in enumerate(direct):
            rcopy(a, b, s + k, dev).wait_send()


def _comm_shapes(gathers, scatters, sib):
    assert all(g.shape[0] % 32 == 0 for g in gathers)
    cin = list(gathers) + list(scatters) + ([sib[0], sib[1]] if sib else [])
    cout = [jax.ShapeDtypeStruct((NQ,) + g.shape, g.dtype) for g in gathers] + \
           [jax.ShapeDtypeStruct((3,) + s.shape[1:], s.dtype) for s in scatters] + \
           ([jax.ShapeDtypeStruct(sib[1].shape, sib[1].dtype)] if sib else [])
    n_rem = 6 * len(gathers) + 3 * len(scatters) + (len(sib[2]) if sib else 0)
    sems = [pltpu.SemaphoreType.DMA((max(n_rem, 1),)), pltpu.SemaphoreType.DMA((max(n_rem, 1),)),
            pltpu.SemaphoreType.DMA((max(len(gathers), 1),))]
    return cin, cout, sems


def _pcall(body, *, name, grid, in_specs, out_specs, out_shape, operands, scratch_shapes=(),
           gathers=(), scatters=(), sib=None):
    assert len(grid) == 1
    out_shape, out_specs = tuple(out_shape), tuple(out_specs)
    if not (gathers or scatters or sib):
        return pl.pallas_call(body, name=name, grid=grid, in_specs=list(in_specs), out_specs=out_specs,
                              out_shape=out_shape, scratch_shapes=list(scratch_shapes),
                              compiler_params=_cparams("arbitrary"))(*operands)
    cin, cout, sems = _comm_shapes(gathers, scatters, sib)
    n_in, n_cin, n_out, n_cout, n_scr = len(operands), len(cin), len(out_shape), len(cout), len(scratch_shapes)
    ng, ns = len(gathers), len(scatters)
    nsteps = grid[0]

    def wrapped(*refs):
        ins = refs[:n_in]
        cins = refs[n_in:n_in + n_cin]
        o0 = n_in + n_cin
        outs = refs[o0:o0 + n_out]
        couts = refs[o0 + n_out:o0 + n_out + n_cout]
        s0 = o0 + n_out + n_cout
        scr = refs[s0:s0 + n_scr]
        send, recv, loc = refs[s0 + n_scr:s0 + n_scr + 3]
        x, y, c = _my_pos()

        def run(phase):
            g = [(cins[k], couts[k]) for k in range(ng)]
            sc = [(cins[ng + k], couts[ng + k]) for k in range(ns)]
            sb = (cins[ng + ns], couts[ng + ns], sib[2]) if sib else None
            _comm_run(phase, x, y, c, g, sc, sb, send, recv, loc)

        @pl.when(pl.program_id(0) == 0)
        def _():
            run("start")

        if ng:
            @pl.when(pl.program_id(0) == max(nsteps - FORWARD_STEPS, 0))
            def _():
                run("forward")

        body(*ins, *outs, *scr)

        @pl.when(pl.program_id(0) == nsteps - 1)
        def _():
            run("finish")

    anyspec = pl.BlockSpec(memory_space=pl.ANY)
    aliases = {n_in + ng + ns + 1: n_out + ng + ns} if sib else {}
    return pl.pallas_call(
        wrapped, name=name, grid=grid,
        in_specs=list(in_specs) + [anyspec] * n_cin, out_specs=out_specs + (anyspec,) * n_cout,
        out_shape=out_shape + tuple(cout), scratch_shapes=list(scratch_shapes) + sems,
        input_output_aliases=aliases,
        compiler_params=pltpu.CompilerParams(dimension_semantics=("arbitrary",), vmem_limit_bytes=VMEM_LIMIT,
                                             has_side_effects=True),
    )(*operands, *cin)


def _comm_only(name, small=None, reduce_small=False, gathers=(), scatters=(), sib=None):
    cin, cout, sems = _comm_shapes(gathers, scatters, sib)
    n_cin, n_cout = len(cin), len(cout)
    ng, ns = len(gathers), len(scatters)
    n_sm_in = 1 if small is not None else 0
    n_sm_out = (2 if reduce_small else 1) if small is not None else 0

    def body(*refs):
        sm_in = refs[:n_sm_in]
        cins = refs[n_sm_in:n_sm_in + n_cin]
        o0 = n_sm_in + n_cin
        sm_out = refs[o0:o0 + n_sm_out]
        couts = refs[o0 + n_sm_out:o0 + n_sm_out + n_cout]
        s0 = o0 + n_sm_out + n_cout
        send, recv, loc = refs[s0:s0 + 3]
        x, y, c = _my_pos()
        g = [(cins[k], couts[k]) for k in range(ng)]
        sc = [(cins[ng + k], couts[ng + k]) for k in range(ns)]
        sb = (cins[ng + ns], couts[ng + ns], sib[2]) if sib else None
        _comm_run("start", x, y, c, g, sc, sb, send, recv, loc)
        if small is not None:
            sm_send, sm_recv = refs[s0 + 3:s0 + 5]
            small_ref, sg_ref = sm_in[0], sm_out[0]
            me = _dev_index(x, y, c)
            sg_ref[me] = small_ref[...]
            peers = _all_peers(x, y, c)
            sm = [pltpu.make_async_remote_copy(small_ref, sg_ref.at[me], sm_send.at[k], sm_recv.at[k],
                                               device_id=peer, device_id_type=MESH) for k, peer in enumerate(peers)]
            for cp in sm:
                cp.start()
            for k, (px, py, pc) in enumerate(peers):
                pltpu.make_async_remote_copy(small_ref, sg_ref.at[_dev_index(px, py, pc)], sm_send.at[k], sm_recv.at[k],
                                             device_id=(px, py, pc), device_id_type=MESH).wait_recv()
            if reduce_small:
                acc = sg_ref[0]
                for d in range(1, NDEV):
                    acc = acc + sg_ref[d]
                sm_out[1][...] = acc
            for cp in sm:
                cp.wait_send()
        if ng:
            _comm_run("forward", x, y, c, g, sc, sb, send, recv, loc)
        _comm_run("finish", x, y, c, g, sc, sb, send, recv, loc)

    anyspec = pl.BlockSpec(memory_space=pl.ANY)
    vspec = pl.BlockSpec(memory_space=pltpu.VMEM)
    sm_shapes = []
    if small is not None:
        sm_shapes.append(jax.ShapeDtypeStruct((NDEV,) + small.shape, small.dtype))
        if reduce_small:
            sm_shapes.append(jax.ShapeDtypeStruct(small.shape, small.dtype))
        sems = sems + [pltpu.SemaphoreType.DMA((NDEV - 1,)), pltpu.SemaphoreType.DMA((NDEV - 1,))]
    aliases = {n_sm_in + ng + ns + 1: n_sm_out + ng + ns} if sib else {}
    return pl.pallas_call(
        body, name=name,
        in_specs=[vspec] * n_sm_in + [anyspec] * n_cin,
        out_specs=tuple([vspec] * n_sm_out + [anyspec] * n_cout),
        out_shape=tuple(sm_shapes + cout), scratch_shapes=sems, input_output_aliases=aliases,
        compiler_params=pltpu.CompilerParams(has_side_effects=True),
    )(*([small] if small is not None else []), *cin)


def _exchange_mod(modpart):
    _, L, Cs = modpart.shape

    def body(part_ref, out_ref, send, recv):
        x, y, c = _my_pos()
        q = 2 * x + y
        me = _dev_index(x, y, c)
        out_ref[q] = part_ref[me]
        sends = []
        for k, (px, py) in enumerate(_chip_peers(x, y)):
            cp = pltpu.make_async_remote_copy(part_ref.at[_dev_index(px, py, c)], out_ref.at[q], send.at[k], recv.at[k],
                                              device_id=(px, py, c), device_id_type=MESH)
            cp.start()
            sends.append(cp)
        for k, (px, py) in enumerate(_chip_peers(x, y)):
            pltpu.make_async_remote_copy(part_ref.at[me], out_ref.at[2 * px + py], send.at[k], recv.at[k],
                                         device_id=(px, py, c), device_id_type=MESH).wait_recv()
        for cp in sends:
            cp.wait_send()

    return pl.pallas_call(
        body, name="exchange_mod",
        out_shape=jax.ShapeDtypeStruct((NQ, L, Cs), modpart.dtype),
        in_specs=[pl.BlockSpec(memory_space=pltpu.VMEM)],
        out_specs=pl.BlockSpec(memory_space=pltpu.VMEM),
        scratch_shapes=[pltpu.SemaphoreType.DMA((3,)), pltpu.SemaphoreType.DMA((3,))],
        compiler_params=pltpu.CompilerParams(has_side_effects=True),
    )(modpart)


def _mod_part(c_all, w_mod, b_mod_sh):
    L, D, Cs = w_mod.shape
    tn = 512 if Cs % 512 == 0 else Cs

    def body(c_ref, w_ref, b_ref, o_ref):
        cv = c_ref[...]
        cond = cv * jax.nn.sigmoid(cv)
        o_ref[...] = jnp.dot(cond, w_ref[...], preferred_element_type=F32, precision=lax.Precision.HIGHEST) + b_ref[...]

    return pl.pallas_call(
        body, name="mod_part", grid=(L, Cs // tn),
        out_shape=jax.ShapeDtypeStruct((L, NDEV, Cs), F32),
        in_specs=[pl.BlockSpec((NDEV, D), lambda i, j: (0, 0)),
                  pl.BlockSpec((None, D, tn), lambda i, j: (i, 0, j)),
                  pl.BlockSpec((None, 1, tn), lambda i, j: (i, 0, j))],
        out_specs=pl.BlockSpec((None, NDEV, tn), lambda i, j: (i, 0, j)),
        compiler_params=_cparams("parallel", "parallel"),
    )(c_all, w_mod, b_mod_sh)


def _adam(w, g, m, v):
    m2 = ADAM_B1 * m + (1.0 - ADAM_B1) * g
    v2 = ADAM_B2 * v + (1.0 - ADAM_B2) * (g * g)
    m_hat = m2 / (1.0 - ADAM_B1 ** ADAM_STEP)
    v_hat = v2 / (1.0 - ADAM_B2 ** ADAM_STEP)
    delta = -ADAM_LR * (m_hat / (jnp.sqrt(v_hat) + ADAM_EPS) + ADAM_WD * w)
    return delta, m2, v2


def _wmod_update(c_all_t, dmod_sh, w, m, v):
    L, D, Cs = w.shape
    td = 256 if D % 256 == 0 else D

    def body(ct_ref, d_ref, w_ref, m_ref, v_ref, g_ref, dl_ref, m2_ref, v2_ref):
        cv = ct_ref[...]
        cond = cv * jax.nn.sigmoid(cv)
        g = cond[:, 0:1] * d_ref[0:1, :]
        for b in range(1, NDEV):
            g = g + cond[:, b:b + 1] * d_ref[b:b + 1, :]
        g_ref[...] = g
        dl_ref[...], m2_ref[...], v2_ref[...] = _adam(w_ref[...], g, m_ref[...], v_ref[...])

    blk = pl.BlockSpec((None, td, Cs), lambda i, j: (i, j, 0))
    out = jax.ShapeDtypeStruct((L, D, Cs), F32)
    return pl.pallas_call(
        body, name="wmod_update", grid=(L, D // td),
        out_shape=(out, out, out, out),
        in_specs=[pl.BlockSpec((td, NDEV), lambda i, j: (j, 0)),
                  pl.BlockSpec((None, NDEV, Cs), lambda i, j: (i, 0, 0)), blk, blk, blk],
        out_specs=(blk, blk, blk, blk),
        compiler_params=_cparams("parallel", "parallel"),
    )(c_all_t, dmod_sh, w, m, v)


def _adam_rows(w, m, v, pa, pb, row_off, name):
    rows, C = w.shape
    tr = 512 if rows % 512 == 0 else (128 if rows % 128 == 0 else rows)
    assert row_off % tr == 0
    ob = row_off // tr

    def body(w_ref, m_ref, v_ref, pa_ref, pb_ref, g_ref, dl_ref, m2_ref, v2_ref):
        g = pa_ref[...] + pb_ref[...]
        g_ref[...] = g
        dl_ref[...], m2_ref[...], v2_ref[...] = _adam(w_ref[...], g, m_ref[...], v_ref[...])

    blk = pl.BlockSpec((tr, C), lambda i: (i, 0))
    pblk = pl.BlockSpec((tr, C), lambda i: (ob + i, 0))
    out = jax.ShapeDtypeStruct((rows, C), F32)
    return pl.pallas_call(
        body, name=name, grid=(rows // tr,), out_shape=(out, out, out, out),
        in_specs=[blk, blk, blk, pblk, pblk], out_specs=(blk, blk, blk, blk),
        compiler_params=_cparams("parallel"),
    )(w, m, v, pa, pb)


def _adam_small(w, g, m, v):
    def body(w_ref, g_ref, m_ref, v_ref, dl_ref, m2_ref, v2_ref):
        dl_ref[...], m2_ref[...], v2_ref[...] = _adam(w_ref[...], g_ref[...], m_ref[...], v_ref[...])

    out = jax.ShapeDtypeStruct(w.shape, F32)
    return pl.pallas_call(body, name="adam_small", out_shape=(out, out, out))(w, g, m, v)


def _sum_into(ppack, dw, rb, off, qv):
    _, rows, D = dw.shape
    tr = 256 if rows % 256 == 0 else 128
    assert rows % tr == 0 and off % tr == 0
    ob = off // tr

    def body(q_ref, o_ref, r_ref, pin_ref, p_ref):
        acc = o_ref[...].astype(F32)
        for k in range(3):
            acc = acc + r_ref[k].astype(F32)
        p_ref[...] = acc

    return pl.pallas_call(
        body, name="sum_partials", out_shape=jax.ShapeDtypeStruct(ppack.shape, ppack.dtype),
        grid_spec=pltpu.PrefetchScalarGridSpec(
            num_scalar_prefetch=1, grid=(rows // tr,),
            in_specs=[pl.BlockSpec((None, tr, D), lambda i, q_ref: (q_ref[0], i, 0)),
                      pl.BlockSpec((3, tr, D), lambda i, q_ref: (0, i, 0)),
                      pl.BlockSpec(memory_space=pl.ANY)],
            out_specs=pl.BlockSpec((tr, D), lambda i, q_ref: (ob + i, 0))),
        input_output_aliases={3: 0},
        compiler_params=_cparams("parallel"),
    )(qv, dw, rb, ppack)


def _wspec(g):
    return _resident(g.shape, lambda i: (0, 0, 0))


def _ffn_fwd_inner(x1, mod_ref, gf_ref, w1_ref, w2_ref, h2_ref, a_ref, z_ref, x2_ref):
    h2 = _rms_fwd(x1, gf_ref[...], mod_ref[4:5, :], mod_ref[3:4, :])[0]
    h2b = h2.astype(BF16)
    h2_ref[...] = h2b
    f4 = w1_ref.shape[2]
    z = jnp.zeros(x1.shape, F32)
    for q in range(NQ):
        a = jnp.maximum(_dot(h2b, w1_ref[q]), 0.0)
        a_ref[:, q * f4:(q + 1) * f4] = a.astype(BF16)
        z = z + _dot((a * a).astype(BF16), w2_ref[q])
    z_ref[...] = z.astype(BF16)
    x2_ref[...] = x1 + mod_ref[5:6, :] * z


def _sigmoid(x):
    return 0.5 + 0.5 * jnp.tanh(0.5 * x)


def _heads_dot(xb, w_ref, hd, nt=False):
    outs = []
    for h in range(HEADS):
        xs = xb[:, h * hd:(h + 1) * hd]
        outs.append(_dot_nt(xs, w_ref[h]) if nt else _dot(xs, w_ref[h]))
    return jnp.concatenate(outs, axis=1)


def _lru_gates(xc, wa_ref, ba, wx_ref, bx, lam, hd):
    xcb = xc.astype(BF16)
    gate_r = _sigmoid(_heads_dot(xcb, wa_ref, hd) + ba)
    gate_i = _sigmoid(_heads_dot(xcb, wx_ref, hd) + bx)
    ls = jax.nn.log_sigmoid(lam)
    log_a = gate_r * (LRU_C * ls)
    a = jnp.exp(log_a)
    mult = jnp.sqrt(_neg_expm1(2.0 * log_a, a * a))
    return xcb, gate_r, gate_i, ls, a, mult


def _conv_taps(xext, cw, tt):
    acc = cw[0:1, :] * xext[pl.ds(8 - (CONV_W - 1), tt), :]
    for k in range(1, CONV_W):
        acc = acc + cw[k:k + 1, :] * xext[pl.ds(8 - (CONV_W - 1) + k, tt), :]
    return acc


def _lru_fwd(x, mod_l, g_mix, g_wy, g_win, b_y, b_in, cw, cb, wa, ba, wx, bx, lam, g_wout, b_out, **comm):
    S, W = x.shape
    tt = min(TT, S)
    hd = W // HEADS

    def body(x_ref, mod_ref, g_ref, wy_ref, win_ref, by_ref, bin_ref, cw_ref, cb_ref, wa_ref, ba_ref, wx_ref, bx_ref,
             lam_ref, wo_ref, bo_ref, h_ref, gb_ref, xr_ref, hs_ref, p_ref, y_ref, x1_ref, xext, a_s, u_s, carry):
        i = pl.program_id(0)

        @pl.when(i == 0)
        def _():
            carry[...] = jnp.zeros_like(carry)
            xext[0:8, :] = jnp.zeros((8, W), F32)

        @pl.when(i > 0)
        def _():
            xext[0:8, :] = xext[pl.ds(tt, 8), :]

        xv = x_ref[...]
        hb = _rms_fwd(xv, g_ref[...], mod_ref[1:2, :], mod_ref[0:1, :])[0].astype(BF16)
        h_ref[...] = hb
        gbv = _dot(hb, wy_ref[...].reshape(W, W)) + by_ref[...]
        gb_ref[...] = gbv
        xr = _dot(hb, win_ref[...].reshape(W, W)) + bin_ref[...]
        xr_ref[...] = xr
        xext[pl.ds(8, tt), :] = xr
        xc = _conv_taps(xext, cw_ref[...], tt) + cb_ref[...]
        _, _, gate_i, _, a, mult = _lru_gates(xc, wa_ref, ba_ref[...], wx_ref, bx_ref[...], lam_ref[...], hd)
        a_s[...] = a
        u_s[...] = mult * (gate_i * xc)
        row = lax.broadcasted_iota(jnp.int32, (8, W), 0)

        def step(k, _):
            off = pl.multiple_of(k * 8, 8)
            A = a_s[pl.ds(off, 8), :]
            U = u_s[pl.ds(off, 8), :]
            for d in (1, 2, 4):
                keep = row >= d
                Us = jnp.where(keep, pltpu.roll(U, d, 0), 0.0)
                As = jnp.where(keep, pltpu.roll(A, d, 0), 1.0)
                U = U + A * Us
                A = A * As
            H = U + A * carry[...]
            hs_ref[pl.ds(off, 8), :] = H
            carry[...] = jnp.broadcast_to(H[7:8, :], (8, W))
            return 0

        lax.fori_loop(0, tt // 8, step, 0)
        pb = (hs_ref[...] * _gelu(gbv)[0]).astype(BF16)
        p_ref[...] = pb
        y = _dot(pb, wo_ref[...].reshape(W, W)) + bo_ref[...]
        y_ref[...] = y.astype(BF16)
        x1_ref[...] = xv + mod_ref[2:3, :] * y

    tile = pl.BlockSpec((tt, W), lambda i: (i, 0))
    row = pl.BlockSpec((1, W), lambda i: (0, 0))
    wblk = pl.BlockSpec((HEADS, hd, hd), lambda i: (0, 0, 0))
    f32o, bf16o = jax.ShapeDtypeStruct((S, W), F32), jax.ShapeDtypeStruct((S, W), BF16)
    return _pcall(
        body, name="lru_fwd", grid=(S // tt,),
        out_shape=(bf16o, f32o, f32o, f32o, bf16o, bf16o, f32o),
        in_specs=[tile, pl.BlockSpec((8, W), lambda i: (0, 0)), row, _wspec(g_wy), _wspec(g_win), row, row,
                  pl.BlockSpec((CONV_W, W), lambda i: (0, 0)), row, wblk, row, wblk, row, row, _wspec(g_wout), row],
        out_specs=(tile,) * 7,
        scratch_shapes=[pltpu.VMEM((tt + 8, W), F32), pltpu.VMEM((tt, W), F32), pltpu.VMEM((tt, W), F32),
                        pltpu.VMEM((8, W), F32)],
        operands=(x, mod_l, g_mix, g_wy, g_win, b_y, b_in, cw, cb, wa, ba, wx, bx, lam, g_wout, b_out), **comm)


def _ffn_out_shapes(S, D, F):
    return (jax.ShapeDtypeStruct((S, D), BF16), jax.ShapeDtypeStruct((S, F), BF16),
            jax.ShapeDtypeStruct((S, D), BF16), jax.ShapeDtypeStruct((S, D), F32))


def _ffn_fwd(x1, mod_l, g_ffn, g_w1, g_w2, **comm):
    S, D = x1.shape
    tm = min(TM, S)
    F = g_w1.shape[2] * NQ

    def body(x1_ref, mod_ref, gf_ref, w1_ref, w2_ref, h2_ref, a_ref, z_ref, x2_ref):
        _ffn_fwd_inner(x1_ref[...], mod_ref, gf_ref, w1_ref, w2_ref, h2_ref, a_ref, z_ref, x2_ref)

    tile = pl.BlockSpec((tm, D), lambda i: (i, 0))
    row = pl.BlockSpec((1, D), lambda i: (0, 0))
    return _pcall(
        body, name="ffn_fwd", grid=(S // tm,),
        out_shape=_ffn_out_shapes(S, D, F),
        in_specs=[tile, pl.BlockSpec((8, D), lambda i: (0, 0)), row, _wspec(g_w1), _wspec(g_w2)],
        out_specs=(tile, pl.BlockSpec((tm, F), lambda i: (i, 0)), tile, tile),
        operands=(x1, mod_l, g_ffn, g_w1, g_w2), **comm)


def _window_vec(D):
    gd = D // len(POOL_WINDOWS)
    lane = lax.broadcasted_iota(jnp.int32, (1, D), 1)
    w = jnp.full((1, D), float(POOL_WINDOWS[0]), F32)
    for g in range(1, len(POOL_WINDOWS)):
        w = jnp.where(lane >= g * gd, float(POOL_WINDOWS[g]), w)
    return w


def _pool_mix_ffn_fwd(x, mod_l, g_mix, pw, ps, g_ffn, g_w1, g_w2, **comm):
    S, D = x.shape
    tm = min(TP, S)
    F = g_w1.shape[2] * NQ
    gd = D // len(POOL_WINDOWS)
    n = tm + 24

    def body(x_ref, xh_ref, mod_ref, gm_ref, pw_ref, ps_ref, gf_ref, w1_ref, w2_ref,
             pl_ref, x1_ref, h2_ref, a_ref, z_ref, x2_ref, ext, b1, b2):
        i = pl.program_id(0)
        g, sc, sh = gm_ref[...], mod_ref[1:2, :], mod_ref[0:1, :]
        xv = x_ref[...]
        h = _rms_fwd(xv, g, sc, sh)[0]
        hh = _rms_fwd(xh_ref[...], g, sc, sh)[0]
        zeros8 = jnp.zeros((8, D), F32)
        ext[0:8, :] = zeros8
        b1[0:8, :] = zeros8
        b2[0:8, :] = zeros8
        ext[8:24, :] = jnp.where(i > 0, hh, 0.0)
        ext[pl.ds(24, tm), :] = h
        m = n - 8
        b1[pl.ds(8, m), :] = ext[pl.ds(8, m), :] + ext[pl.ds(7, m), :]
        b2[pl.ds(8, m), gd:] = b1[pl.ds(8, m), gd:] + b1[pl.ds(6, m), gd:]
        b1[pl.ds(8, m), 2 * gd:] = b2[pl.ds(8, m), 2 * gd:] + b2[pl.ds(4, m), 2 * gd:]
        b2[pl.ds(8, m), 3 * gd:] = b1[pl.ds(8, m), 3 * gd:] + b1[pl.ds(0, m), 3 * gd:]
        wsum = jnp.concatenate([b1[pl.ds(24, tm), 0:gd], b2[pl.ds(24, tm), gd:2 * gd],
                                b1[pl.ds(24, tm), 2 * gd:3 * gd], b2[pl.ds(24, tm), 3 * gd:]], axis=1)
        t1 = (lax.broadcasted_iota(jnp.int32, (tm, 1), 0) + (i * tm + 1)).astype(F32)
        cnt = jnp.minimum(t1, _window_vec(D))
        pooled = (wsum / cnt - h).astype(BF16)
        pl_ref[...] = pooled
        y = _heads_dot(pooled, pw_ref, gd) * ps_ref[...]
        x1 = xv + mod_ref[2:3, :] * y
        x1_ref[...] = x1
        _ffn_fwd_inner(x1, mod_ref, gf_ref, w1_ref, w2_ref, h2_ref, a_ref, z_ref, x2_ref)

    tile = pl.BlockSpec((tm, D), lambda i: (i, 0))
    halo = pl.BlockSpec((16, D), lambda i: (jnp.maximum(i * (tm // 16) - 1, 0), 0))
    row = pl.BlockSpec((1, D), lambda i: (0, 0))
    return _pcall(
        body, name="pool_mix_ffn_fwd", grid=(S // tm,),
        out_shape=(jax.ShapeDtypeStruct((S, D), BF16), jax.ShapeDtypeStruct((S, D), F32)) + _ffn_out_shapes(S, D, F),
        in_specs=[tile, halo, pl.BlockSpec((8, D), lambda i: (0, 0)), row,
                  pl.BlockSpec((len(POOL_WINDOWS), gd, gd), lambda i: (0, 0, 0)), row, row,
                  _wspec(g_w1), _wspec(g_w2)],
        out_specs=(tile, tile, tile, pl.BlockSpec((tm, F), lambda i: (i, 0)), tile, tile),
        scratch_shapes=[pltpu.VMEM((n, D), F32), pltpu.VMEM((n, D), F32), pltpu.VMEM((n, D), F32)],
        operands=(x, x, mod_l, g_mix, pw, ps, g_ffn, g_w1, g_w2), **comm)


def _loss_head(xv, gv, tv, acc_ref):
    D = xv.shape[1]
    r = lax.rsqrt(jnp.mean(xv * xv, axis=-1, keepdims=True) + EPS)
    xhat = xv * r
    err = xhat * gv - tv
    acc_ref[0:1, :] += jnp.sum(err * err, axis=0, keepdims=True)
    dy = err * (1.0 / D)
    acc_ref[1:2, :] += jnp.sum(dy * xhat, axis=0, keepdims=True)
    dxh = dy * gv
    return r * (dxh - xhat * jnp.mean(dxh * xhat, axis=-1, keepdims=True))


def _ffn_bwd(dx2, x1, a, z, mod_l, g_ffn, g_w1, g_w2, head=None, **comm):
    S, D = dx2.shape
    F = a.shape[1]
    f4 = F // NQ
    tm = min(TM, S)
    nh = 2 if head else 0

    def body(*refs):
        dx2_ref, x1_ref, a_ref, z_ref, mod_ref, gf_ref, w1_ref, w2_ref = refs[:8]
        dx1_ref, du_ref, dz_ref, acc_ref = refs[8 + nh:]
        _zero_first(acc_ref)
        dx2v = dx2_ref[...]
        if head:
            dx2v = _loss_head(dx2v, refs[8][...], refs[9][...], acc_ref)
        acc_ref[5:6, :] +=jnp.sum(dx2v * z_ref[...].astype(F32), axis=0, keepdims=True)
        dzb = (dx2v * mod_ref[5:6, :]).astype(BF16)
        dz_ref[...] = dzb
        dh2 = jnp.zeros((tm, D), F32)
        for q in range(NQ):
            av = a_ref[:, q * f4:(q + 1) * f4].astype(F32)
            du = (_dot_nt(dzb, w2_ref[q]) * (2.0 * av)).astype(BF16)
            du_ref[:, q * f4:(q + 1) * f4] = du
            dh2 = dh2 + _dot_nt(du, w1_ref[q])
        g, sc = gf_ref[...], mod_ref[4:5, :]
        _, xhat, r, n = _rms_fwd(x1_ref[...], g, sc, mod_ref[3:4, :])
        dx, dsh, dsc, dg = _rms_bwd(dh2, xhat, r, n, g, sc)
        acc_ref[3:4, :] += dsh
        acc_ref[4:5, :] += dsc
        acc_ref[7:8, :] += dg
        dx1_ref[...] = dx2v + dx

    tile = pl.BlockSpec((tm, D), lambda i: (i, 0))
    wide = pl.BlockSpec((tm, F), lambda i: (i, 0))
    return _pcall(
        body, name="ffn_bwd", grid=(S // tm,),
        out_shape=(jax.ShapeDtypeStruct((S, D), F32), jax.ShapeDtypeStruct((S, F), BF16),
                   jax.ShapeDtypeStruct((S, D), BF16), jax.ShapeDtypeStruct((8, D), F32)),
        in_specs=[tile, tile, wide, tile, pl.BlockSpec((8, D), lambda i: (0, 0)), pl.BlockSpec((1, D), lambda i: (0, 0)),
                  _wspec(g_w1), _wspec(g_w2)] + ([pl.BlockSpec((1, D), lambda i: (0, 0)), tile] if head else []),
        out_specs=(tile, wide, tile, pl.BlockSpec((8, D), lambda i: (0, 0))),
        operands=(dx2, x1, a, z, mod_l, g_ffn, g_w1, g_w2) + (tuple(head) if head else ()), **comm)


def _dw_blocked(a, b, by_rows, square_a, name):
    S = a.shape[0]
    tk = min(TK, S)
    nk = S // tk
    if by_rows:
        bm, bn = a.shape[1] // NQ, b.shape[1]
        a_map, b_map = (lambda q, k: (k, q)), (lambda q, k: (k, 0))
    else:
        bm, bn = a.shape[1], b.shape[1] // NQ
        a_map, b_map = (lambda q, k: (k, 0)), (lambda q, k: (k, q))

    def body(a_ref, b_ref, o_ref, acc):
        k = pl.program_id(1)

        @pl.when(k == 0)
        def _():
            acc[...] = jnp.zeros_like(acc)

        av = a_ref[...]
        if square_a:
            av = av * av
        acc[...] += _dot_tn(av, b_ref[...])

        @pl.when(k == nk - 1)
        def _():
            o_ref[...] = acc[...].astype(o_ref.dtype)

    return pl.pallas_call(
        body, name=name, grid=(NQ, nk),
        out_shape=jax.ShapeDtypeStruct((NQ, bm, bn), BF16),
        in_specs=[pl.BlockSpec((tk, bm), a_map), pl.BlockSpec((tk, bn), b_map)],
        out_specs=pl.BlockSpec((None, bm, bn), lambda q, k: (q, 0, 0)),
        scratch_shapes=[pltpu.VMEM((bm, bn), F32)],
        compiler_params=_cparams("parallel", "arbitrary"),
    )(a, b)


def _dw_whole(a, bs, name):
    S, M = a.shape
    N = bs[0].shape[1]
    tk = min(TK, S)
    nk = S // tk
    nb = len(bs)

    def body(*refs):
        a_ref, b_refs, o_refs, accs = refs[0], refs[1:1 + nb], refs[1 + nb:1 + 2 * nb], refs[1 + 2 * nb:]
        k = pl.program_id(0)

        @pl.when(k == 0)
        def _():
            for acc in accs:
                acc[...] = jnp.zeros_like(acc)

        av = a_ref[...]
        for b_ref, acc in zip(b_refs, accs):
            acc[...] += _dot_tn(av, b_ref[...])

        @pl.when(k == nk - 1)
        def _():
            for o_ref, acc in zip(o_refs, accs):
                o_ref[...] = acc[...].reshape(NQ, M // NQ, N).astype(o_ref.dtype)

    return pl.pallas_call(
        body, name=name, grid=(nk,),
        out_shape=tuple(jax.ShapeDtypeStruct((NQ, M // NQ, N), BF16) for _ in bs),
        in_specs=[pl.BlockSpec((tk, M), lambda k: (k, 0))] + [pl.BlockSpec((tk, N), lambda k: (k, 0)) for _ in bs],
        out_specs=tuple(pl.BlockSpec((NQ, M // NQ, N), lambda k: (0, 0, 0)) for _ in bs),
        scratch_shapes=[pltpu.VMEM((M, N), F32) for _ in bs],
        compiler_params=_cparams("arbitrary"),
    )(a, *bs)


def _lru_bwd(dx1, y, x, xr0, gb, hs, mod_l, g_mix, g_wout, g_wy, g_win, cw, cb, wa, ba, wx, bx, lam, **comm):
    S, W = xr0.shape
    tt = min(TT, S)
    nb = S // tt
    hd = W // HEADS

    def body(dx1_ref, y_ref, x_ref, xr_ref, xrh_ref, gb_ref, hs_ref, hsh_ref, mod_ref, gm_ref, wo_ref, wy_ref, win_ref,
             cw_ref, cb_ref, wa_ref, ba_ref, wx_ref, bx_ref, lam_ref,
             dy_ref, dgb_ref, dxr_ref, dx_ref, sm_ref, dwa_ref, dwx_ref, acc_ref,
             xext, hext, qext, dext, a_s, b_s, qc, dc):
        i = pl.program_id(0)
        blk = nb - 1 - i

        @pl.when(i == 0)
        def _():
            sm_ref[...] = jnp.zeros_like(sm_ref)
            dwa_ref[...] = jnp.zeros_like(dwa_ref)
            dwx_ref[...] = jnp.zeros_like(dwx_ref)
            acc_ref[...] = jnp.zeros_like(acc_ref)
            qc[...] = jnp.zeros_like(qc)
            dc[...] = jnp.zeros_like(dc)

        dx1v = dx1_ref[...]
        acc_ref[2:3, :] += jnp.sum(dx1v * y_ref[...].astype(F32), axis=0, keepdims=True)
        dy = dx1v * mod_ref[2:3, :]
        acc_ref[3:4, :] += jnp.sum(dy, axis=0, keepdims=True)
        dyb = dy.astype(BF16)
        dy_ref[...] = dyb
        dpv = _dot_nt(dyb, wo_ref[...].reshape(W, W))

        xext[0:8, :] = jnp.where(blk > 0, xrh_ref[...], 0.0)
        xext[pl.ds(8, tt), :] = xr_ref[...]
        hext[0:8, :] = jnp.where(blk > 0, hsh_ref[...], 0.0)
        hext[pl.ds(8, tt), :] = hs_ref[...]
        cw = cw_ref[...]
        lam = lam_ref[...]
        xc = _conv_taps(xext, cw, tt) + cb_ref[...]
        xcb, gate_r, gate_i, ls, a, mult = _lru_gates(xc, wa_ref, ba_ref[...], wx_ref, bx_ref[...], lam, hd)

        gbv = gb_ref[...]
        gate, th = _gelu(gbv)
        dgb = dpv * hs_ref[...] * _gelu_grad(gbv, th)
        dgbb = dgb.astype(BF16)
        dgb_ref[...] = dgbb
        sm_ref[9:10, :] += jnp.sum(dgb, axis=0, keepdims=True)
        dhs = dpv * gate

        a_s[...] = a
        b_s[...] = a * dhs
        qext[pl.ds(tt, 8), :] = qc[...]
        row = lax.broadcasted_iota(jnp.int32, (8, W), 0)

        def step(k, _):
            off = pl.multiple_of((tt // 8 - 1 - k) * 8, 8)
            A = a_s[pl.ds(off, 8), :]
            B = b_s[pl.ds(off, 8), :]
            for d in (1, 2, 4):
                keep = row < 8 - d
                Bs = jnp.where(keep, pltpu.roll(B, 8 - d, 0), 0.0)
                As = jnp.where(keep, pltpu.roll(A, 8 - d, 0), 1.0)
                B = B + A * Bs
                A = A * As
            Q = B + A * qc[...]
            qext[pl.ds(off, 8), :] = Q
            qc[...] = jnp.broadcast_to(Q[0:1, :], (8, W))
            return 0

        lax.fori_loop(0, tt // 8, step, 0)
        gsc = dhs + qext[pl.ds(1, tt), :]
        da = gsc * hext[pl.ds(7, tt), :]
        t1 = gsc * xc
        dmult = t1 * gate_i
        dgate_i = t1 * mult
        dxc = gsc * (mult * gate_i)
        dlog_a = da * a - dmult * (a * a) / mult
        dgate_r = dlog_a * (LRU_C * ls)
        sm_ref[7:8, :] += jnp.sum(dlog_a * (LRU_C * gate_r), axis=0, keepdims=True)
        dga = dgate_r * gate_r * (1.0 - gate_r)
        dgx = dgate_i * gate_i * (1.0 - gate_i)
        sm_ref[5:6, :] += jnp.sum(dga, axis=0, keepdims=True)
        sm_ref[6:7, :] += jnp.sum(dgx, axis=0, keepdims=True)
        dgab = dga.astype(BF16)
        dgxb = dgx.astype(BF16)
        dxc = dxc + _heads_dot(dgab, wa_ref, hd, nt=True) + _heads_dot(dgxb, wx_ref, hd, nt=True)
        for h in range(HEADS):
            sl = slice(h * hd, (h + 1) * hd)
            dwa_ref[h] += _dot_tn(xcb[:, sl], dgab[:, sl])
            dwx_ref[h] += _dot_tn(xcb[:, sl], dgxb[:, sl])
        sm_ref[4:5, :] += jnp.sum(dxc, axis=0, keepdims=True)
        for k in range(CONV_W):
            sm_ref[k:k + 1, :] += jnp.sum(dxc * xext[pl.ds(8 - (CONV_W - 1) + k, tt), :], axis=0, keepdims=True)
        dext[pl.ds(0, tt), :] = dxc
        dext[pl.ds(tt, 8), :] = dc[...]
        dxr = cw[0:1, :] * dext[pl.ds(CONV_W - 1, tt), :]
        for k in range(1, CONV_W):
            dxr = dxr + cw[k:k + 1, :] * dext[pl.ds(CONV_W - 1 - k, tt), :]
        dc[...] = dext[0:8, :]
        sm_ref[8:9, :] += jnp.sum(dxr, axis=0, keepdims=True)
        dxrb = dxr.astype(BF16)
        dxr_ref[...] = dxrb

        dh = _dot_nt(dxrb, win_ref[...].reshape(W, W)) + _dot_nt(dgbb, wy_ref[...].reshape(W, W))
        g, sc = gm_ref[...], mod_ref[1:2, :]
        _, xhat, r, n = _rms_fwd(x_ref[...], g, sc, mod_ref[0:1, :])
        dx, dsh, dsc, dg = _rms_bwd(dh, xhat, r, n, g, sc)
        acc_ref[0:1, :] += dsh
        acc_ref[1:2, :] += dsc
        acc_ref[6:7, :] += dg
        dx_ref[...] = dx1v + dx

        @pl.when(i == nb - 1)
        def _():
            sm_ref[7:8, :] = sm_ref[7:8, :] * jax.nn.sigmoid(-lam)

    rev = lambda i: (nb - 1 - i, 0)
    tile = pl.BlockSpec((tt, W), rev)
    halo = pl.BlockSpec((8, W), lambda i: (jnp.maximum((nb - 1 - i) * (tt // 8) - 1, 0), 0))
    row = pl.BlockSpec((1, W), lambda i: (0, 0))
    wblk = pl.BlockSpec((HEADS, hd, hd), lambda i: (0, 0, 0))
    bf16o = jax.ShapeDtypeStruct((S, W), BF16)
    return _pcall(
        body, name="lru_bwd", grid=(nb,),
        out_shape=(bf16o, bf16o, bf16o, jax.ShapeDtypeStruct((S, W), F32),
                   jax.ShapeDtypeStruct((16, W), F32), jax.ShapeDtypeStruct((HEADS, hd, hd), F32),
                   jax.ShapeDtypeStruct((HEADS, hd, hd), F32), jax.ShapeDtypeStruct((8, W), F32)),
        in_specs=[tile, tile, tile, tile, halo, tile, tile, halo, pl.BlockSpec((8, W), lambda i: (0, 0)), row,
                  _wspec(g_wout), _wspec(g_wy), _wspec(g_win), pl.BlockSpec((CONV_W, W), lambda i: (0, 0)), row,
                  wblk, row, wblk, row, row],
        out_specs=(tile, tile, tile, tile, pl.BlockSpec((16, W), lambda i: (0, 0)), wblk, wblk,
                   pl.BlockSpec((8, W), lambda i: (0, 0))),
        scratch_shapes=[pltpu.VMEM((tt + 8, W), F32), pltpu.VMEM((tt + 8, W), F32), pltpu.VMEM((tt + 8, W), F32),
                        pltpu.VMEM((tt + 8, W), F32), pltpu.VMEM((tt, W), F32), pltpu.VMEM((tt, W), F32),
                        pltpu.VMEM((8, W), F32), pltpu.VMEM((8, W), F32)],
        operands=(dx1, y, x, xr0, xr0, gb, hs, hs, mod_l, g_mix, g_wout, g_wy, g_win, cw, cb, wa, ba, wx, bx, lam),
        **comm)


def _pool_bwd(dx1, x, pooled, mod_l, g_mix, pw, ps):
    S, D = x.shape
    tm = min(TP, S)
    nb = S // tm
    ng = len(POOL_WINDOWS)
    gd = D // ng
    n = tm + 24

    def body(dx1_ref, dxh_ref, x_ref, pl_ref, mod_ref, gm_ref, pw_ref, ps_ref, dx_ref, acc_ref, dpw_ref, ext, b1, b2):
        i = pl.program_id(0)

        @pl.when(i == 0)
        def _():
            acc_ref[...] = jnp.zeros_like(acc_ref)
            dpw_ref[...] = jnp.zeros_like(dpw_ref)

        gt, psv = mod_ref[2:3, :], ps_ref[...]
        wvec = _window_vec(D)
        dx1v = dx1_ref[...]
        pooled = pl_ref[...]
        mixed = _heads_dot(pooled, pw_ref, gd)
        acc_ref[2:3, :] += jnp.sum(dx1v * (mixed * psv), axis=0, keepdims=True)
        dy = dx1v * gt
        acc_ref[3:4, :] += jnp.sum(dy * mixed, axis=0, keepdims=True)
        dmix = (dy * psv).astype(BF16)
        for gi in range(ng):
            sl = slice(gi * gd, (gi + 1) * gd)
            dpw_ref[gi] += _dot_tn(pooled[:, sl], dmix[:, sl])
        dpooled = _heads_dot(dmix, pw_ref, gd, nt=True)
        dmix_h = (dxh_ref[...] * gt * psv).astype(BF16)
        dpooled_h = jnp.where(i < nb - 1, _heads_dot(dmix_h, pw_ref, gd, nt=True), 0.0)
        t1 = (lax.broadcasted_iota(jnp.int32, (tm, 1), 0) + (i * tm + 1)).astype(F32)
        t1h = (lax.broadcasted_iota(jnp.int32, (16, 1), 0) + ((i + 1) * tm + 1)).astype(F32)
        zeros8 = jnp.zeros((8, D), F32)
        ext[pl.ds(0, tm), :] = dpooled / jnp.minimum(t1, wvec)
        ext[pl.ds(tm, 16), :] = dpooled_h / jnp.minimum(t1h, wvec)
        ext[pl.ds(tm + 16, 8), :] = zeros8
        b1[pl.ds(tm + 16, 8), :] = zeros8
        b2[pl.ds(tm + 16, 8), :] = zeros8
        m = n - 8
        b1[pl.ds(0, m), :] = ext[pl.ds(0, m), :] + ext[pl.ds(1, m), :]
        b2[pl.ds(0, m), gd:] = b1[pl.ds(0, m), gd:] + b1[pl.ds(2, m), gd:]
        b1[pl.ds(0, m), 2 * gd:] = b2[pl.ds(0, m), 2 * gd:] + b2[pl.ds(4, m), 2 * gd:]
        b2[pl.ds(0, m), 3 * gd:] = b1[pl.ds(0, m), 3 * gd:] + b1[pl.ds(8, m), 3 * gd:]
        wsum = jnp.concatenate([b1[pl.ds(0, tm), 0:gd], b2[pl.ds(0, tm), gd:2 * gd],
                                b1[pl.ds(0, tm), 2 * gd:3 * gd], b2[pl.ds(0, tm), 3 * gd:]], axis=1)
        dh = wsum - dpooled
        g, sc = gm_ref[...], mod_ref[1:2, :]
        _, xhat, r, nn = _rms_fwd(x_ref[...], g, sc, mod_ref[0:1, :])
        dx, dsh, dsc, dg = _rms_bwd(dh, xhat, r, nn, g, sc)
        acc_ref[0:1, :] += dsh
        acc_ref[1:2, :] += dsc
        acc_ref[6:7, :] += dg
        dx_ref[...] = dx1v + dx

    tile = pl.BlockSpec((tm, D), lambda i: (i, 0))
    halo = pl.BlockSpec((16, D), lambda i: (jnp.minimum((i + 1) * (tm // 16), S // 16 - 1), 0))
    row = pl.BlockSpec((1, D), lambda i: (0, 0))
    wblk = pl.BlockSpec((ng, gd, gd), lambda i: (0, 0, 0))
    return pl.pallas_call(
        body, name="pool_bwd", grid=(nb,),
        out_shape=(jax.ShapeDtypeStruct((S, D), F32), jax.ShapeDtypeStruct((8, D), F32),
                   jax.ShapeDtypeStruct((ng, gd, gd), F32)),
        in_specs=[tile, halo, tile, tile, pl.BlockSpec((8, D), lambda i: (0, 0)), row, wblk, row],
        out_specs=(tile, pl.BlockSpec((8, D), lambda i: (0, 0)), wblk),
        scratch_shapes=[pltpu.VMEM((n, D), F32), pltpu.VMEM((n, D), F32), pltpu.VMEM((n, D), F32)],
        compiler_params=_cparams("arbitrary"),
    )(dx1, dx1, x, pooled, mod_l, g_mix, pw, ps)


def _shard_to_rows(w, D):
    return w.reshape(-1, D)


def _blockdiag_full(gq, na, hd):
    return gq.reshape(NQ, na, HEADS, hd // NQ, hd).transpose(1, 2, 0, 3, 4).reshape(na, HEADS, hd, hd)


def _blockdiag_by_chip(dw, D):
    na, _, hd, _ = dw.shape
    return dw.reshape(na, HEADS, NQ, hd // NQ, hd).transpose(2, 0, 1, 3, 4).reshape(NQ, -1, D)


def kernel(x, c, w_mod, b_mod, norm_mix_g, norm_ffn_g, lru_w_y, lru_b_y, lru_w_in, lru_b_in, lru_conv_w, lru_conv_b, lru_w_a, lru_b_a, lru_w_x, lru_b_x, lru_lambda, lru_w_out, lru_b_out, pool_w, pool_scale, ffn_w1, ffn_w2, final_norm_g, loss_target, m_w_mod, m_b_mod, m_norm_mix_g, m_norm_ffn_g, m_lru_w_y, m_lru_b_y, m_lru_w_in, m_lru_b_in, m_lru_conv_w, m_lru_conv_b, m_lru_w_a, m_lru_b_a, m_lru_w_x, m_lru_b_x, m_lru_lambda, m_lru_w_out, m_lru_b_out, m_pool_w, m_pool_scale, m_ffn_w1, m_ffn_w2, m_final_norm_g, v_w_mod, v_b_mod, v_norm_mix_g, v_norm_ffn_g, v_lru_w_y, v_lru_b_y, v_lru_w_in, v_lru_b_in, v_lru_conv_w, v_lru_conv_b, v_lru_w_a, v_lru_b_a, v_lru_w_x, v_lru_b_x, v_lru_lambda, v_lru_w_out, v_lru_b_out, v_pool_w, v_pool_scale, v_ffn_w1, v_ffn_w2, v_final_norm_g):
    S, D = x.shape[1], x.shape[2]
    L = w_mod.shape[0]
    NA = lru_w_y.shape[0]
    NB = pool_w.shape[0]
    F = ffn_w1.shape[2] * NQ
    f4 = F // NQ
    hd = D // HEADS
    Cs = w_mod.shape[2]
    assert L == DEPTH and Cs * NQ == N_MOD * D and D % 1024 == 0
    x2d = x.reshape(S, D)
    tgt = loss_target.reshape(S, D)
    q = 2 * lax.axis_index("x") + lax.axis_index("y")

    big = [ffn_w1, ffn_w2, lru_w_y, lru_w_in, lru_w_out, lru_w_a, lru_w_x, pool_w]
    rows = [int(w.size) // D for w in big]
    offs = [sum(rows[:k]) for k in range(len(big))]
    O_W1, O_W2, O_WY, O_WIN, O_WOUT, O_WA, O_WX, O_PW = offs
    R = sum(rows)
    dq = D // NQ
    s_w1 = [ffn_w1[i].astype(BF16) for i in range(L)]
    s_w2 = [ffn_w2[i].astype(BF16) for i in range(L)]
    s_wy = [lru_w_y[j].astype(BF16) for j in range(NA)]
    s_win = [lru_w_in[j].astype(BF16) for j in range(NA)]
    s_wout = [lru_w_out[j].astype(BF16) for j in range(NA)]
    s_tiny = jnp.concatenate([_shard_to_rows(w, D) for w in (lru_w_a, lru_w_x, pool_w)], axis=0).astype(BF16)

    cshard = lru_conv_w.reshape(-1)
    small_fwd = jnp.concatenate([c.reshape(-1), cshard, lru_b_a.reshape(-1), lru_b_x.reshape(-1),
                                 pool_scale.reshape(-1)])
    small_fwd = jnp.pad(small_fwd, (0, 8 * D - small_fwd.shape[0])).reshape(8, D)

    g_w1, g_w2 = [None] * L, [None] * L
    g_wy, g_win, g_wout = [None] * NA, [None] * NA, [None] * NA
    SG, g_wy[0], g_win[0], g_wout[0], g_tiny = _comm_only("gather_first", small=small_fwd,
                                                         gathers=(s_wy[0], s_win[0], s_wout[0], s_tiny))
    wa_full = _blockdiag_full(g_tiny[:, :rows[5]], NA, hd)
    wx_full = _blockdiag_full(g_tiny[:, rows[5]:rows[5] + rows[6]], NA, hd)
    pw_full = _blockdiag_full(g_tiny[:, rows[5] + rows[6]:], NB, hd)
    SGf = SG.reshape(NDEV, 8 * D)
    c_all = SGf[:, :D]
    SGq = SGf.reshape(NQ, 2, 8 * D)[:, 0]
    o = D
    n_cw = NA * CONV_W * D // NQ
    conv_w_full = SGq[:, o:o + n_cw].reshape(NQ, NA, CONV_W, D // NQ).transpose(1, 2, 0, 3).reshape(NA, CONV_W, D)
    o += n_cw
    n_b = NA * HEADS * hd // NQ
    b_a_full = SGq[:, o:o + n_b].reshape(NQ, NA, HEADS, hd // NQ).transpose(1, 2, 0, 3).reshape(NA, 1, D)
    o += n_b
    b_x_full = SGq[:, o:o + n_b].reshape(NQ, NA, HEADS, hd // NQ).transpose(1, 2, 0, 3).reshape(NA, 1, D)
    o += n_b
    n_ps = NB * D // NQ
    pool_scale_full = SGq[:, o:o + n_ps].reshape(NQ, NB, D // NQ).transpose(1, 0, 2).reshape(NB, 1, D)


    b_mod_sh = lax.dynamic_slice_in_dim(b_mod, q * Cs, Cs, axis=1).reshape(L, 1, Cs)
    modpart = _mod_part(c_all, w_mod, b_mod_sh)
    modq = _exchange_mod(modpart.transpose(1, 0, 2))
    mod = modq.transpose(1, 0, 2).reshape(L, N_MOD, D)
    mod = jnp.pad(mod, ((0, 0), (0, 8 - N_MOD), (0, 0)))

    saved = []
    xcur = x2d
    for i in range(L):
        j = i // 2
        gm = norm_mix_g[i].reshape(1, D)
        gf = norm_ffn_g[i].reshape(1, D)
        if i % 2 == 0:
            h, gb, xr0, hs, p, y, x1, g_w1[i], g_w2[i] = _lru_fwd(
                xcur, mod[i], gm, g_wy[j], g_win[j], lru_b_y[j].reshape(1, D), lru_b_in[j].reshape(1, D),
                conv_w_full[j], lru_conv_b[j].reshape(1, D), wa_full[j], b_a_full[j], wx_full[j], b_x_full[j],
                lru_lambda[j].reshape(1, D), g_wout[j], lru_b_out[j].reshape(1, D), gathers=(s_w1[i], s_w2[i]))
            h2, a, z, x2, g_w1[i + 1], g_w2[i + 1] = _ffn_fwd(x1, mod[i], gf, g_w1[i], g_w2[i],
                                                              gathers=(s_w1[i + 1], s_w2[i + 1]))
            saved.append(dict(x=xcur, h=h, gb=gb, xr0=xr0, hs=hs, p=p, y=y, x1=x1, h2=h2, a=a, z=z))
        else:
            if j + 1 < NA:
                pooled, x1, h2, a, z, x2, g_wy[j + 1], g_win[j + 1], g_wout[j + 1] = _pool_mix_ffn_fwd(
                    xcur, mod[i], gm, pw_full[j], pool_scale_full[j], gf, g_w1[i], g_w2[i],
                    gathers=(s_wy[j + 1], s_win[j + 1], s_wout[j + 1]))
            else:
                pooled, x1, h2, a, z, x2 = _pool_mix_ffn_fwd(xcur, mod[i], gm, pw_full[j], pool_scale_full[j], gf,
                                                             g_w1[i], g_w2[i])
            saved.append(dict(x=xcur, pooled=pooled, x1=x1, h2=h2, a=a, z=z))
        xcur = x2

    dx = xcur
    qv = q.reshape(1).astype(jnp.int32)
    ppack = lax.empty((R, D), F32)
    psib = lax.empty((R, D), F32)
    pending, summed = [], []

    def comm_args():
        kw = {}
        if pending:
            kw["scatters"] = tuple(dw for dw, _ in pending)
        if summed:
            kw["sib"] = (ppack, psib, tuple(summed))
        return kw

    def after_host(extra):
        nonlocal ppack, psib, pending, summed
        had_sib = bool(summed)
        summed = []
        for (dw, off), rb in zip(pending, extra[:len(pending)]):
            ppack = _sum_into(ppack, dw, rb, off, qv)
            summed.append((off, dw.shape[1]))
        if had_sib:
            psib = extra[len(pending)]
        pending = []

    dmod_rows = [None] * L
    dg_mix = [None] * L
    dg_ffn = [None] * L
    d_small = {}
    dwa_l, dwx_l, dpw_l = [None] * NA, [None] * NA, [None] * NB
    for i in reversed(range(L)):
        j = i // 2
        sv = saved[i]
        gm = norm_mix_g[i].reshape(1, D)
        gf = norm_ffn_g[i].reshape(1, D)
        head = (final_norm_g.reshape(1, D), tgt) if i == L - 1 else None
        outs = _ffn_bwd(dx, sv["x1"], sv["a"], sv["z"], mod[i], gf, g_w1[i], g_w2[i], head=head, **comm_args())
        dx1, du, dz, facc = outs[:4]
        after_host(outs[4:])
        if head:
            loss = lax.psum(0.5 * jnp.sum(facc[0]) / D, ("x", "y", "c"))
            d_final_g = facc[1]
        pending.append((_dw_blocked(sv["h2"], du, False, False, "dw1"), O_W1 + i * D))
        pending.append((_dw_blocked(sv["a"], dz, True, True, "dw2"), O_W2 + i * f4))
        if i % 2 == 0:
            outs = _lru_bwd(dx1, sv["y"], sv["x"], sv["xr0"], sv["gb"], sv["hs"], mod[i], gm, g_wout[j], g_wy[j],
                            g_win[j], conv_w_full[j], lru_conv_b[j].reshape(1, D), wa_full[j], b_a_full[j], wx_full[j],
                            b_x_full[j], lru_lambda[j].reshape(1, D), **comm_args())
            dyp, dgb, dxr, dx, sm, dwa, dwx, macc = outs[:8]
            after_host(outs[8:])
            pending.append((_dw_whole(sv["p"], [dyp], "dwout")[0], O_WOUT + j * dq))
            dwy, dwin = _dw_whole(sv["h"], [dgb, dxr], "dwy_dwin")
            pending.append((dwy, O_WY + j * dq))
            pending.append((dwin, O_WIN + j * dq))
            dwa_l[j], dwx_l[j] = dwa, dwx
            d_small[("lru", j)] = (sm, macc[3])
            dgt_m = macc[2]
        else:
            dx, macc, dpw = _pool_bwd(dx1, sv["x"], sv["pooled"], mod[i], gm, pw_full[j], pool_scale_full[j])
            dpw_l[j] = dpw
            d_small[("pool", j)] = macc[3]
            dgt_m = macc[2]
        dmod_rows[i] = jnp.stack([macc[0], macc[1], dgt_m, facc[3], facc[4], facc[5]])
        dg_mix[i] = macc[6]
        dg_ffn[i] = facc[7]
    grad_x = dx.reshape(x.shape)

    tiny = jnp.concatenate([_blockdiag_by_chip(jnp.stack(dwa_l), D), _blockdiag_by_chip(jnp.stack(dwx_l), D),
                            _blockdiag_by_chip(jnp.stack(dpw_l), D)], axis=1).astype(BF16)
    pending.append((tiny, O_WA))

    lru_sm = [d_small[("lru", j)] for j in range(NA)]
    small_rows = [jnp.stack(dmod_rows).reshape(L * N_MOD, D), jnp.stack(dg_mix), jnp.stack(dg_ffn),
                  jnp.stack([s[0][9] for s in lru_sm]), jnp.stack([s[0][8] for s in lru_sm]),
                  jnp.stack([s[0][4] for s in lru_sm]), jnp.stack([s[0][7] for s in lru_sm]),
                  jnp.stack([s[1] for s in lru_sm]),
                  jnp.stack([s[0][0:CONV_W] for s in lru_sm]).reshape(NA * CONV_W, D),
                  jnp.stack([s[0][5] for s in lru_sm]), jnp.stack([s[0][6] for s in lru_sm]),
                  jnp.stack([d_small[("pool", j)] for j in range(NB)]), d_final_g.reshape(1, D)]
    small_g = jnp.concatenate(small_rows, axis=0)
    n_small = small_g.shape[0]
    assert n_small <= SMALL_ROWS
    small_g = jnp.pad(small_g, ((0, SMALL_ROWS - n_small), (0, 0)))

    outs = _comm_only("scatter_last", small=small_g, reduce_small=True, **comm_args())
    sg_all, sg_sum = outs[:2]
    after_host(outs[2:])
    psum_mine = ppack
    psum_sib = _comm_only("sibling_last", sib=(ppack, psib, tuple(summed)))[0]

    def big_update(w, m, v, off, name):
        shp = w.shape
        g, dl, m2, v2 = _adam_rows(w.reshape(-1, D), m.reshape(-1, D), v.reshape(-1, D), psum_mine, psum_sib, off, name)
        return g.reshape(shp), dl.reshape(shp), m2.reshape(shp), v2.reshape(shp)

    res = {}
    res["ffn_w1"] = big_update(ffn_w1, m_ffn_w1, v_ffn_w1, O_W1, "adam_w1")
    res["ffn_w2"] = big_update(ffn_w2, m_ffn_w2, v_ffn_w2, O_W2, "adam_w2")
    res["lru_w_y"] = big_update(lru_w_y, m_lru_w_y, v_lru_w_y, O_WY, "adam_wy")
    res["lru_w_in"] = big_update(lru_w_in, m_lru_w_in, v_lru_w_in, O_WIN, "adam_win")
    res["lru_w_out"] = big_update(lru_w_out, m_lru_w_out, v_lru_w_out, O_WOUT, "adam_wout")
    res["lru_w_a"] = big_update(lru_w_a, m_lru_w_a, v_lru_w_a, O_WA, "adam_wa")
    res["lru_w_x"] = big_update(lru_w_x, m_lru_w_x, v_lru_w_x, O_WX, "adam_wx")
    res["pool_w"] = big_update(pool_w, m_pool_w, v_pool_w, O_PW, "adam_pw")

    dmod_all = sg_all[:, :L * N_MOD, :].reshape(NDEV, L, N_MOD * D)
    dmod_sh = lax.dynamic_slice_in_dim(dmod_all, q * Cs, Cs, axis=2).transpose(1, 0, 2)
    res["w_mod"] = _wmod_update(c_all.T, dmod_sh, w_mod, m_w_mod, v_w_mod)

    r0 = 0

    def take(nrows):
        nonlocal r0
        out = sg_sum[r0:r0 + nrows]
        r0 += nrows
        return out

    g_b_mod = take(L * N_MOD).reshape(L, N_MOD * D)
    g_mix = take(L)
    g_ffn = take(L)
    g_b_y = take(NA)
    g_b_in = take(NA)
    g_conv_b = take(NA)
    g_lam = take(NA)
    g_b_out = take(NA)
    g_conv_w = lax.dynamic_slice_in_dim(take(NA * CONV_W).reshape(NA, CONV_W, D), q * (D // NQ), D // NQ, axis=2)
    g_b_a = lax.dynamic_slice_in_dim(take(NA).reshape(NA, HEADS, hd), q * (hd // NQ), hd // NQ, axis=2)
    g_b_x = lax.dynamic_slice_in_dim(take(NA).reshape(NA, HEADS, hd), q * (hd // NQ), hd // NQ, axis=2)
    g_ps = lax.dynamic_slice_in_dim(take(NB), q * (D // NQ), D // NQ, axis=1)
    g_fin = take(1).reshape(D)

    smalls = [("b_mod", b_mod, m_b_mod, v_b_mod, g_b_mod), ("norm_mix_g", norm_mix_g, m_norm_mix_g, v_norm_mix_g, g_mix),
              ("norm_ffn_g", norm_ffn_g, m_norm_ffn_g, v_norm_ffn_g, g_ffn),
              ("lru_b_y", lru_b_y, m_lru_b_y, v_lru_b_y, g_b_y), ("lru_b_in", lru_b_in, m_lru_b_in, v_lru_b_in, g_b_in),
              ("lru_conv_w", lru_conv_w, m_lru_conv_w, v_lru_conv_w, g_conv_w),
              ("lru_conv_b", lru_conv_b, m_lru_conv_b, v_lru_conv_b, g_conv_b),
              ("lru_b_a", lru_b_a, m_lru_b_a, v_lru_b_a, g_b_a), ("lru_b_x", lru_b_x, m_lru_b_x, v_lru_b_x, g_b_x),
              ("lru_lambda", lru_lambda, m_lru_lambda, v_lru_lambda, g_lam),
              ("lru_b_out", lru_b_out, m_lru_b_out, v_lru_b_out, g_b_out),
              ("pool_scale", pool_scale, m_pool_scale, v_pool_scale, g_ps),
              ("final_norm_g", final_norm_g, m_final_norm_g, v_final_norm_g, g_fin)]
    total = sum(int(s[1].size) for s in smalls)
    prow = -(-total // (8 * 128)) * 8

    def flat(k):
        f = jnp.concatenate([s[k].reshape(-1) for s in smalls])
        return jnp.pad(f, (0, prow * 128 - total)).reshape(prow, 128)

    dl_s, m_s, v_s = _adam_small(flat(1), flat(4), flat(2), flat(3))
    o = 0
    for name, w, _, _, g in smalls:
        sz = int(w.size)
        cut = lambda arr: arr.reshape(-1)[o:o + sz].reshape(w.shape)
        res[name] = (g.reshape(w.shape), cut(dl_s), cut(m_s), cut(v_s))
        o += sz

    order = ["w_mod", "b_mod", "norm_mix_g", "norm_ffn_g", "lru_w_y", "lru_b_y", "lru_w_in", "lru_b_in", "lru_conv_w",
             "lru_conv_b", "lru_w_a", "lru_b_a", "lru_w_x", "lru_b_x", "lru_lambda", "lru_w_out", "lru_b_out", "pool_w",
             "pool_scale", "ffn_w1", "ffn_w2", "final_norm_g"]
    return (loss, grad_x, *[res[n][0] for n in order], *[res[n][1] for n in order],
            *[res[n][2] for n in order], *[res[n][3] for n in order])
```

```python
import functools

import jax
import jax.numpy as jnp
from jax import lax
from jax.experimental import pallas as pl
from jax.experimental.pallas import tpu as pltpu

F32 = jnp.float32
BF16 = jnp.bfloat16
MESH = pl.DeviceIdType.MESH

NQ = 4
NDEV = 8
DEPTH = 4
N_MOD = 6
HEADS = 4
CONV_W = 4
POOL_WINDOWS = (2, 4, 8, 16)
LRU_C = 8.0
EPS = 1e-6
ADAM_LR, ADAM_B1, ADAM_B2, ADAM_EPS, ADAM_WD, ADAM_STEP = 0.001, 0.9, 0.999, 1e-08, 0.01, 10

TM = 512
TT = 256
TP = 256
TK = 2048
SMALL_ROWS = 64
FORWARD_STEPS = 4
VMEM_LIMIT = 60 * 1024 * 1024


def _cparams(*sem):
    return pltpu.CompilerParams(dimension_semantics=tuple(sem), vmem_limit_bytes=VMEM_LIMIT)


def _dot(a, b):
    return jnp.dot(a, b, preferred_element_type=F32)


def _dot_nt(a, b):
    return lax.dot_general(a, b, (((1,), (1,)), ((), ())), preferred_element_type=F32)


def _dot_tn(a, b):
    return lax.dot_general(a, b, (((0,), (0,)), ((), ())), preferred_element_type=F32)


def _resident(shape, index_map):
    return pl.BlockSpec(shape, index_map, pipeline_mode=pl.Buffered(1))


def _rms_fwd(x, g, sc, sh):
    r = lax.rsqrt(jnp.mean(x * x, axis=-1, keepdims=True) + EPS)
    xhat = x * r
    n = xhat * g
    return n * (1.0 + sc) + sh, xhat, r, n


def _rms_bwd(dh, xhat, r, n, g, sc):
    dsh = jnp.sum(dh, axis=0, keepdims=True)
    dsc = jnp.sum(dh * n, axis=0, keepdims=True)
    dn = dh * (1.0 + sc)
    dg = jnp.sum(dn * xhat, axis=0, keepdims=True)
    dxh = dn * g
    dx = r * (dxh - xhat * jnp.mean(dxh * xhat, axis=-1, keepdims=True))
    return dx, dsh, dsc, dg


_GELU_K = 0.7978845608028654
_GELU_C = 0.044715


def _gelu(x):
    t = jnp.tanh(_GELU_K * (x + _GELU_C * x * x * x))
    return 0.5 * x * (1.0 + t), t


def _gelu_grad(x, t):
    return 0.5 * (1.0 + t) + 0.5 * x * (1.0 - t * t) * (_GELU_K * (1.0 + 3.0 * _GELU_C * x * x))


def _neg_expm1(y, exp_y):
    series = -(y * (1.0 + y * (0.5 + y * (1.0 / 6.0))))
    return jnp.where(y > -(1.0 / 64.0), series, 1.0 - exp_y)


def _zero_first(ref):
    @pl.when(pl.program_id(0) == 0)
    def _():
        ref[...] = jnp.zeros_like(ref)


def _my_pos():
    return lax.axis_index("x"), lax.axis_index("y"), lax.axis_index("c")


def _dev_index(x, y, c):
    return 4 * x + 2 * y + c


def _chip_peers(x, y):
    return [(1 - x, y), (x, 1 - y), (1 - x, 1 - y)]


def _all_peers(x, y, c):
    return [(px, py, c) for (px, py) in _chip_peers(x, y)] + [(x, y, 1 - c)] + \
           [(px, py, 1 - c) for (px, py) in _chip_peers(x, y)]


def _comm_run(phase, x, y, c, gathers, scatters, sib, send, recv, loc):
    q = 2 * x + y
    peers = _chip_peers(x, y)
    sibling = (x, y, 1 - c)

    def rcopy(src, dst, s, dev):
        return pltpu.make_async_remote_copy(src, dst, send.at[s], recv.at[s], device_id=dev, device_id_type=MESH)

    s = 0
    for gi, (src, dst) in enumerate(gathers):
        half = src.shape[0] // 2
        mine, other = pl.ds(c * half, half), pl.ds((1 - c) * half, half)
        own = pltpu.make_async_copy(src, dst.at[q], loc.at[gi])
        if phase == "start":
            own.start()
        elif phase == "finish":
            own.wait()
        for (px, py) in peers:
            pq = 2 * px + py
            s_ici, s_fwd = s, s + 1
            s += 2
            if phase == "start":
                rcopy(src.at[mine], dst.at[q].at[mine], s_ici, (px, py, c)).start()
            elif phase == "forward":
                rcopy(src.at[mine], dst.at[pq].at[mine], s_ici, (px, py, c)).wait_recv()
                rcopy(dst.at[pq].at[mine], dst.at[pq].at[mine], s_fwd, sibling).start()
            else:
                rcopy(dst.at[pq].at[other], dst.at[pq].at[other], s_fwd, sibling).wait_recv()
                rcopy(src.at[mine], dst.at[q].at[mine], s_ici, (px, py, c)).wait_send()
                rcopy(dst.at[pq].at[mine], dst.at[pq].at[mine], s_fwd, sibling).wait_send()
    direct = []
    for (src, dst) in scatters:
        for k, (px, py) in enumerate(peers):
            direct.append((src.at[2 * px + py], dst.at[k], (px, py, c)))
    if sib is not None:
        src, dst, ranges = sib
        for (off, rows) in ranges:
            direct.append((src.at[pl.ds(off, rows)], dst.at[pl.ds(off, rows)], sibling))
    if phase == "start":
        for k, (a, b, dev) in enumerate(direct):
            rcopy(a, b, s + k, dev).start()
    elif phase == "finish":
        for k, (a, b, dev) in enumerate(direct):
            rcopy(a, b, s + k, dev).wait_recv()
        for k, (a, b, dev) in enumerate(direct):
            rcopy(a, b, s + k, dev).wait_send()


def _comm_shapes(gathers, scatters, sib):
    assert all(g.shape[0] % 32 == 0 for g in gathers)
    cin = list(gathers) + list(scatters) + ([sib[0], sib[1]] if sib else [])
    cout = [jax.ShapeDtypeStruct((NQ,) + g.shape, g.dtype) for g in gathers] + \
           [jax.ShapeDtypeStruct((3,) + s.shape[1:], s.dtype) for s in scatters] + \
           ([jax.ShapeDtypeStruct(sib[1].shape, sib[1].dtype)] if sib else [])
    n_rem = 6 * len(gathers) + 3 * len(scatters) + (len(sib[2]) if sib else 0)
    sems = [pltpu.SemaphoreType.DMA((max(n_rem, 1),)), pltpu.SemaphoreType.DMA((max(n_rem, 1),)),
            pltpu.SemaphoreType.DMA((max(len(gathers), 1),))]
    return cin, cout, sems


def _pcall(body, *, name, grid, in_specs, out_specs, out_shape, operands, scratch_shapes=(),
           gathers=(), scatters=(), sib=None):
    assert len(grid) == 1
    out_shape, out_specs = tuple(out_shape), tuple(out_specs)
    if not (gathers or scatters or sib):
        return pl.pallas_call(body, name=name, grid=grid, in_specs=list(in_specs), out_specs=out_specs,
                              out_shape=out_shape, scratch_shapes=list(scratch_shapes),
                              compiler_params=_cparams("arbitrary"))(*operands)
    cin, cout, sems = _comm_shapes(gathers, scatters, sib)
    n_in, n_cin, n_out, n_cout, n_scr = len(operands), len(cin), len(out_shape), len(cout), len(scratch_shapes)
    ng, ns = len(gathers), len(scatters)
    nsteps = grid[0]

    def wrapped(*refs):
        ins = refs[:n_in]
        cins = refs[n_in:n_in + n_cin]
        o0 = n_in + n_cin
        outs = refs[o0:o0 + n_out]
        couts = refs[o0 + n_out:o0 + n_out + n_cout]
        s0 = o0 + n_out + n_cout
        scr = refs[s0:s0 + n_scr]
        send, recv, loc = refs[s0 + n_scr:s0 + n_scr + 3]
        x, y, c = _my_pos()

        def run(phase):
            g = [(cins[k], couts[k]) for k in range(ng)]
            sc = [(cins[ng + k], couts[ng + k]) for k in range(ns)]
            sb = (cins[ng + ns], couts[ng + ns], sib[2]) if sib else None
            _comm_run(phase, x, y, c, g, sc, sb, send, recv, loc)

        @pl.when(pl.program_id(0) == 0)
        def _():
            run("start")

        if ng:
            @pl.when(pl.program_id(0) == max(nsteps - FORWARD_STEPS, 0))
            def _():
                run("forward")

        body(*ins, *outs, *scr)

        @pl.when(pl.program_id(0) == nsteps - 1)
        def _():
            run("finish")

    anyspec = pl.BlockSpec(memory_space=pl.ANY)
    aliases = {n_in + ng + ns + 1: n_out + ng + ns} if sib else {}
    return pl.pallas_call(
        wrapped, name=name, grid=grid,
        in_specs=list(in_specs) + [anyspec] * n_cin, out_specs=out_specs + (anyspec,) * n_cout,
        out_shape=out_shape + tuple(cout), scratch_shapes=list(scratch_shapes) + sems,
        input_output_aliases=aliases,
        compiler_params=pltpu.CompilerParams(dimension_semantics=("arbitrary",), vmem_limit_bytes=VMEM_LIMIT,
                                             has_side_effects=True),
    )(*operands, *cin)


def _comm_only(name, small=None, reduce_small=False, gathers=(), scatters=(), sib=None):
    cin, cout, sems = _comm_shapes(gathers, scatters, sib)
    n_cin, n_cout = len(cin), len(cout)
    ng, ns = len(gathers), len(scatters)
    n_sm_in = 1 if small is not None else 0
    n_sm_out = (2 if reduce_small else 1) if small is not None else 0

    def body(*refs):
        sm_in = refs[:n_sm_in]
        cins = refs[n_sm_in:n_sm_in + n_cin]
        o0 = n_sm_in + n_cin
        sm_out = refs[o0:o0 + n_sm_out]
        couts = refs[o0 + n_sm_out:o0 + n_sm_out + n_cout]
        s0 = o0 + n_sm_out + n_cout
        send, recv, loc = refs[s0:s0 + 3]
        x, y, c = _my_pos()
        g = [(cins[k], couts[k]) for k in range(ng)]
        sc = [(cins[ng + k], couts[ng + k]) for k in range(ns)]
        sb = (cins[ng + ns], couts[ng + ns], sib[2]) if sib else None
        _comm_run("start", x, y, c, g, sc, sb, send, recv, loc)
        if small is not None:
            sm_send, sm_recv = refs[s0 + 3:s0 + 5]
            small_ref, sg_ref = sm_in[0], sm_out[0]
            me = _dev_index(x, y, c)
            sg_ref[me] = small_ref[...]
            peers = _all_peers(x, y, c)
            sm = [pltpu.make_async_remote_copy(small_ref, sg_ref.at[me], sm_send.at[k], sm_recv.at[k],
                                               device_id=peer, device_id_type=MESH) for k, peer in enumerate(peers)]
            for cp in sm:
                cp.start()
            for k, (px, py, pc) in enumerate(peers):
                pltpu.make_async_remote_copy(small_ref, sg_ref.at[_dev_index(px, py, pc)], sm_send.at[k], sm_recv.at[k],
                                             device_id=(px, py, pc), device_id_type=MESH).wait_recv()
            if reduce_small:
                acc = sg_ref[0]
                for d in range(1, NDEV):
                    acc = acc + sg_ref[d]
                sm_out[1][...] = acc
            for cp in sm:
                cp.wait_send()
        if ng:
            _comm_run("forward", x, y, c, g, sc, sb, send, recv, loc)
        _comm_run("finish", x, y, c, g, sc, sb, send, recv, loc)

    anyspec = pl.BlockSpec(memory_space=pl.ANY)
    vspec = pl.BlockSpec(memory_space=pltpu.VMEM)
    sm_shapes = []
    if small is not None:
        sm_shapes.append(jax.ShapeDtypeStruct((NDEV,) + small.shape, small.dtype))
        if reduce_small:
            sm_shapes.append(jax.ShapeDtypeStruct(small.shape, small.dtype))
        sems = sems + [pltpu.SemaphoreType.DMA((NDEV - 1,)), pltpu.SemaphoreType.DMA((NDEV - 1,))]
    aliases = {n_sm_in + ng + ns + 1: n_sm_out + ng + ns} if sib else {}
    return pl.pallas_call(
        body, name=name,
        in_specs=[vspec] * n_sm_in + [anyspec] * n_cin,
        out_specs=tuple([vspec] * n_sm_out + [anyspec] * n_cout),
        out_shape=tuple(sm_shapes + cout), scratch_shapes=sems, input_output_aliases=aliases,
        compiler_params=pltpu.CompilerParams(has_side_effects=True),
    )(*([small] if small is not None else []), *cin)


def _exchange_mod(modpart):
    _, L, Cs = modpart.shape

    def body(part_ref, out_ref, send, recv):
        x, y, c = _my_pos()
        q = 2 * x + y
        me = _dev_index(x, y, c)
        out_ref[q] = part_ref[me]
        sends = []
        for k, (px, py) in enumerate(_chip_peers(x, y)):
            cp = pltpu.make_async_remote_copy(part_ref.at[_dev_index(px, py, c)], out_ref.at[q], send.at[k], recv.at[k],
                                              device_id=(px, py, c), device_id_type=MESH)
            cp.start()
            sends.append(cp)
        for k, (px, py) in enumerate(_chip_peers(x, y)):
            pltpu.make_async_remote_copy(part_ref.at[me], out_ref.at[2 * px + py], send.at[k], recv.at[k],
                                         device_id=(px, py, c), device_id_type=MESH).wait_recv()
        for cp in sends:
            cp.wait_send()

    return pl.pallas_call(
        body, name="exchange_mod",
        out_shape=jax.ShapeDtypeStruct((NQ, L, Cs), modpart.dtype),
        in_specs=[pl.BlockSpec(memory_space=pltpu.VMEM)],
        out_specs=pl.BlockSpec(memory_space=pltpu.VMEM),
        scratch_shapes=[pltpu.SemaphoreType.DMA((3,)), pltpu.SemaphoreType.DMA((3,))],
        compiler_params=pltpu.CompilerParams(has_side_effects=True),
    )(modpart)


def _mod_part(c_all, w_mod, b_mod_sh):
    L, D, Cs = w_mod.shape
    tn = 512 if Cs % 512 == 0 else Cs

    def body(c_ref, w_ref, b_ref, o_ref):
        cv = c_ref[...]
        cond = cv * jax.nn.sigmoid(cv)
        o_ref[...] = jnp.dot(cond, w_ref[...], preferred_element_type=F32, precision=lax.Precision.HIGHEST) + b_ref[...]

    return pl.pallas_call(
        body, name="mod_part", grid=(L, Cs // tn),
        out_shape=jax.ShapeDtypeStruct((L, NDEV, Cs), F32),
        in_specs=[pl.BlockSpec((NDEV, D), lambda i, j: (0, 0)),
                  pl.BlockSpec((None, D, tn), lambda i, j: (i, 0, j)),
                  pl.BlockSpec((None, 1, tn), lambda i, j: (i, 0, j))],
        out_specs=pl.BlockSpec((None, NDEV, tn), lambda i, j: (i, 0, j)),
        compiler_params=_cparams("parallel", "parallel"),
    )(c_all, w_mod, b_mod_sh)


def _adam(w, g, m, v):
    m2 = ADAM_B1 * m + (1.0 - ADAM_B1) * g
    v2 = ADAM_B2 * v + (1.0 - ADAM_B2) * (g * g)
    m_hat = m2 / (1.0 - ADAM_B1 ** ADAM_STEP)
    v_hat = v2 / (1.0 - ADAM_B2 ** ADAM_STEP)
    delta = -ADAM_LR * (m_hat / (jnp.sqrt(v_hat) + ADAM_EPS) + ADAM_WD * w)
    return delta, m2, v2


def _wmod_update(c_all_t, dmod_sh, w, m, v):
    L, D, Cs = w.shape
    td = 256 if D % 256 == 0 else D

    def body(ct_ref, d_ref, w_ref, m_ref, v_ref, g_ref, dl_ref, m2_ref, v2_ref):
        cv = ct_ref[...]
        cond = cv * jax.nn.sigmoid(cv)
        g = cond[:, 0:1] * d_ref[0:1, :]
        for b in range(1, NDEV):
            g = g + cond[:, b:b + 1] * d_ref[b:b + 1, :]
        g_ref[...] = g
        dl_ref[...], m2_ref[...], v2_ref[...] = _adam(w_ref[...], g, m_ref[...], v_ref[...])

    blk = pl.BlockSpec((None, td, Cs), lambda i, j: (i, j, 0))
    out = jax.ShapeDtypeStruct((L, D, Cs), F32)
    return pl.pallas_call(
        body, name="wmod_update", grid=(L, D // td),
        out_shape=(out, out, out, out),
        in_specs=[pl.BlockSpec((td, NDEV), lambda i, j: (j, 0)),
                  pl.BlockSpec((None, NDEV, Cs), lambda i, j: (i, 0, 0)), blk, blk, blk],
        out_specs=(blk, blk, blk, blk),
        compiler_params=_cparams("parallel", "parallel"),
    )(c_all_t, dmod_sh, w, m, v)


def _adam_rows(w, m, v, pa, pb, row_off, name):
    rows, C = w.shape
    tr = 512 if rows % 512 == 0 else (128 if rows % 128 == 0 else rows)
    assert row_off % tr == 0
    ob = row_off // tr

    def body(w_ref, m_ref, v_ref, pa_ref, pb_ref, g_ref, dl_ref, m2_ref, v2_ref):
        g = pa_ref[...] + pb_ref[...]
        g_ref[...] = g
        dl_ref[...], m2_ref[...], v2_ref[...] = _adam(w_ref[...], g, m_ref[...], v_ref[...])

    blk = pl.BlockSpec((tr, C), lambda i: (i, 0))
    pblk = pl.BlockSpec((tr, C), lambda i: (ob + i, 0))
    out = jax.ShapeDtypeStruct((rows, C), F32)
    return pl.pallas_call(
        body, name=name, grid=(rows // tr,), out_shape=(out, out, out, out),
        in_specs=[blk, blk, blk, pblk, pblk], out_specs=(blk, blk, blk, blk),
        compiler_params=_cparams("parallel"),
    )(w, m, v, pa, pb)


def _adam_small(w, g, m, v):
    def body(w_ref, g_ref, m_ref, v_ref, dl_ref, m2_ref, v2_ref):
        dl_ref[...], m2_ref[...], v2_ref[...] = _adam(w_ref[...], g_ref[...], m_ref[...], v_ref[...])

    out = jax.ShapeDtypeStruct(w.shape, F32)
    return pl.pallas_call(body, name="adam_small", out_shape=(out, out, out))(w, g, m, v)


def _sum_into(ppack, dw, rb, off, qv):
    _, rows, D = dw.shape
    tr = 256 if rows % 256 == 0 else 128
    assert rows % tr == 0 and off % tr == 0
    ob = off // tr

    def body(q_ref, o_ref, r_ref, pin_ref, p_ref):
        acc = o_ref[...].astype(F32)
        for k in range(3):
            acc = acc + r_ref[k].astype(F32)
        p_ref[...] = acc

    return pl.pallas_call(
        body, name="sum_partials", out_shape=jax.ShapeDtypeStruct(ppack.shape, ppack.dtype),
        grid_spec=pltpu.PrefetchScalarGridSpec(
            num_scalar_prefetch=1, grid=(rows // tr,),
            in_specs=[pl.BlockSpec((None, tr, D), lambda i, q_ref: (q_ref[0], i, 0)),
                      pl.BlockSpec((3, tr, D), lambda i, q_ref: (0, i, 0)),
                      pl.BlockSpec(memory_space=pl.ANY)],
            out_specs=pl.BlockSpec((tr, D), lambda i, q_ref: (ob + i, 0))),
        input_output_aliases={3: 0},
        compiler_params=_cparams("parallel"),
    )(qv, dw, rb, ppack)


def _wspec(g):
    return _resident(g.shape, lambda i: (0, 0, 0))


def _ffn_fwd_inner(x1, mod_ref, gf_ref, w1_ref, w2_ref, h2_ref, a_ref, z_ref, x2_ref):
    h2 = _rms_fwd(x1, gf_ref[...], mod_ref[4:5, :], mod_ref[3:4, :])[0]
    h2b = h2.astype(BF16)
    h2_ref[...] = h2b
    f4 = w1_ref.shape[2]
    z = jnp.zeros(x1.shape, F32)
    for q in range(NQ):
        a = jnp.maximum(_dot(h2b, w1_ref[q]), 0.0)
        a_ref[:, q * f4:(q + 1) * f4] = a.astype(BF16)
        z = z + _dot((a * a).astype(BF16), w2_ref[q])
    z_ref[...] = z.astype(BF16)
    x2_ref[...] = x1 + mod_ref[5:6, :] * z


def _sigmoid(x):
    return 0.5 + 0.5 * jnp.tanh(0.5 * x)


def _heads_dot(xb, w_ref, hd, nt=False):
    outs = []
    for h in range(HEADS):
        xs = xb[:, h * hd:(h + 1) * hd]
        outs.append(_dot_nt(xs, w_ref[h]) if nt else _dot(xs, w_ref[h]))
    return jnp.concatenate(outs, axis=1)


def _lru_gates(xc, wa_ref, ba, wx_ref, bx, lam, hd):
    xcb = xc.astype(BF16)
    gate_r = _sigmoid(_heads_dot(xcb, wa_ref, hd) + ba)
    gate_i = _sigmoid(_heads_dot(xcb, wx_ref, hd) + bx)
    ls = jax.nn.log_sigmoid(lam)
    log_a = gate_r * (LRU_C * ls)
    a = jnp.exp(log_a)
    mult = jnp.sqrt(_neg_expm1(2.0 * log_a, a * a))
    return xcb, gate_r, gate_i, ls, a, mult


def _conv_taps(xext, cw, tt):
    acc = cw[0:1, :] * xext[pl.ds(8 - (CONV_W - 1), tt), :]
    for k in range(1, CONV_W):
        acc = acc + cw[k:k + 1, :] * xext[pl.ds(8 - (CONV_W - 1) + k, tt), :]
    return acc


def _lru_fwd(x, mod_l, g_mix, g_wy, g_win, b_y, b_in, cw, cb, wa, ba, wx, bx, lam, g_wout, b_out, **comm):
    S, W = x.shape
    tt = min(TT, S)
    hd = W // HEADS

    def body(x_ref, mod_ref, g_ref, wy_ref, win_ref, by_ref, bin_ref, cw_ref, cb_ref, wa_ref, ba_ref, wx_ref, bx_ref,
             lam_ref, wo_ref, bo_ref, h_ref, gb_ref, xr_ref, hs_ref, p_ref, y_ref, x1_ref, xext, a_s, u_s, carry):
        i = pl.program_id(0)

        @pl.when(i == 0)
        def _():
            carry[...] = jnp.zeros_like(carry)
            xext[0:8, :] = jnp.zeros((8, W), F32)

        @pl.when(i > 0)
        def _():
            xext[0:8, :] = xext[pl.ds(tt, 8), :]

        xv = x_ref[...]
        hb = _rms_fwd(xv, g_ref[...], mod_ref[1:2, :], mod_ref[0:1, :])[0].astype(BF16)
        h_ref[...] = hb
        gbv = _dot(hb, wy_ref[...].reshape(W, W)) + by_ref[...]
        gb_ref[...] = gbv
        xr = _dot(hb, win_ref[...].reshape(W, W)) + bin_ref[...]
        xr_ref[...] = xr
        xext[pl.ds(8, tt), :] = xr
        xc = _conv_taps(xext, cw_ref[...], tt) + cb_ref[...]
        _, _, gate_i, _, a, mult = _lru_gates(xc, wa_ref, ba_ref[...], wx_ref, bx_ref[...], lam_ref[...], hd)
        a_s[...] = a
        u_s[...] = mult * (gate_i * xc)
        row = lax.broadcasted_iota(jnp.int32, (8, W), 0)

        def step(k, _):
            off = pl.multiple_of(k * 8, 8)
            A = a_s[pl.ds(off, 8), :]
            U = u_s[pl.ds(off, 8), :]
            for d in (1, 2, 4):
                keep = row >= d
                Us = jnp.where(keep, pltpu.roll(U, d, 0), 0.0)
                As = jnp.where(keep, pltpu.roll(A, d, 0), 1.0)
                U = U + A * Us
                A = A * As
            H = U + A * carry[...]
            hs_ref[pl.ds(off, 8), :] = H
            carry[...] = jnp.broadcast_to(H[7:8, :], (8, W))
            return 0

        lax.fori_loop(0, tt // 8, step, 0)
        pb = (hs_ref[...] * _gelu(gbv)[0]).astype(BF16)
        p_ref[...] = pb
        y = _dot(pb, wo_ref[...].reshape(W, W)) + bo_ref[...]
        y_ref[...] = y.astype(BF16)
        x1_ref[...] = xv + mod_ref[2:3, :] * y

    tile = pl.BlockSpec((tt, W), lambda i: (i, 0))
    row = pl.BlockSpec((1, W), lambda i: (0, 0))
    wblk = pl.BlockSpec((HEADS, hd, hd), lambda i: (0, 0, 0))
    f32o, bf16o = jax.ShapeDtypeStruct((S, W), F32), jax.ShapeDtypeStruct((S, W), BF16)
    return _pcall(
        body, name="lru_fwd", grid=(S // tt,),
        out_shape=(bf16o, f32o, f32o, f32o, bf16o, bf16o, f32o),
        in_specs=[tile, pl.BlockSpec((8, W), lambda i: (0, 0)), row, _wspec(g_wy), _wspec(g_win), row, row,
                  pl.BlockSpec((CONV_W, W), lambda i: (0, 0)), row, wblk, row, wblk, row, row, _wspec(g_wout), row],
        out_specs=(tile,) * 7,
        scratch_shapes=[pltpu.VMEM((tt + 8, W), F32), pltpu.VMEM((tt, W), F32), pltpu.VMEM((tt, W), F32),
                        pltpu.VMEM((8, W), F32)],
        operands=(x, mod_l, g_mix, g_wy, g_win, b_y, b_in, cw, cb, wa, ba, wx, bx, lam, g_wout, b_out), **comm)


def _ffn_out_shapes(S, D, F):
    return (jax.ShapeDtypeStruct((S, D), BF16), jax.ShapeDtypeStruct((S, F), BF16),
            jax.ShapeDtypeStruct((S, D), BF16), jax.ShapeDtypeStruct((S, D), F32))


def _ffn_fwd(x1, mod_l, g_ffn, g_w1, g_w2, **comm):
    S, D = x1.shape
    tm = min(TM, S)
    F = g_w1.shape[2] * NQ

    def body(x1_ref, mod_ref, gf_ref, w1_ref, w2_ref, h2_ref, a_ref, z_ref, x2_ref):
        _ffn_fwd_inner(x1_ref[...], mod_ref, gf_ref, w1_ref, w2_ref, h2_ref, a_ref, z_ref, x2_ref)

    tile = pl.BlockSpec((tm, D), lambda i: (i, 0))
    row = pl.BlockSpec((1, D), lambda i: (0, 0))
    return _pcall(
        body, name="ffn_fwd", grid=(S // tm,),
        out_shape=_ffn_out_shapes(S, D, F),
        in_specs=[tile, pl.BlockSpec((8, D), lambda i: (0, 0)), row, _wspec(g_w1), _wspec(g_w2)],
        out_specs=(tile, pl.BlockSpec((tm, F), lambda i: (i, 0)), tile, tile),
        operands=(x1, mod_l, g_ffn, g_w1, g_w2), **comm)


def _window_vec(D):
    gd = D // len(POOL_WINDOWS)
    lane = lax.broadcasted_iota(jnp.int32, (1, D), 1)
    w = jnp.full((1, D), float(POOL_WINDOWS[0]), F32)
    for g in range(1, len(POOL_WINDOWS)):
        w = jnp.where(lane >= g * gd, float(POOL_WINDOWS[g]), w)
    return w


def _pool_mix_ffn_fwd(x, mod_l, g_mix, pw, ps, g_ffn, g_w1, g_w2, **comm):
    S, D = x.shape
    tm = min(TP, S)
    F = g_w1.shape[2] * NQ
    gd = D // len(POOL_WINDOWS)
    n = tm + 24

    def body(x_ref, xh_ref, mod_ref, gm_ref, pw_ref, ps_ref, gf_ref, w1_ref, w2_ref,
             pl_ref, x1_ref, h2_ref, a_ref, z_ref, x2_ref, ext, b1, b2):
        i = pl.program_id(0)
        g, sc, sh = gm_ref[...], mod_ref[1:2, :], mod_ref[0:1, :]
        xv = x_ref[...]
        h = _rms_fwd(xv, g, sc, sh)[0]
        hh = _rms_fwd(xh_ref[...], g, sc, sh)[0]
        zeros8 = jnp.zeros((8, D), F32)
        ext[0:8, :] = zeros8
        b1[0:8, :] = zeros8
        b2[0:8, :] = zeros8
        ext[8:24, :] = jnp.where(i > 0, hh, 0.0)
        ext[pl.ds(24, tm), :] = h
        m = n - 8
        b1[pl.ds(8, m), :] = ext[pl.ds(8, m), :] + ext[pl.ds(7, m), :]
        b2[pl.ds(8, m), gd:] = b1[pl.ds(8, m), gd:] + b1[pl.ds(6, m), gd:]
        b1[pl.ds(8, m), 2 * gd:] = b2[pl.ds(8, m), 2 * gd:] + b2[pl.ds(4, m), 2 * gd:]
        b2[pl.ds(8, m), 3 * gd:] = b1[pl.ds(8, m), 3 * gd:] + b1[pl.ds(0, m), 3 * gd:]
        wsum = jnp.concatenate([b1[pl.ds(24, tm), 0:gd], b2[pl.ds(24, tm), gd:2 * gd],
                                b1[pl.ds(24, tm), 2 * gd:3 * gd], b2[pl.ds(24, tm), 3 * gd:]], axis=1)
        t1 = (lax.broadcasted_iota(jnp.int32, (tm, 1), 0) + (i * tm + 1)).astype(F32)
        cnt = jnp.minimum(t1, _window_vec(D))
        pooled = (wsum / cnt - h).astype(BF16)
        pl_ref[...] = pooled
        y = _heads_dot(pooled, pw_ref, gd) * ps_ref[...]
        x1 = xv + mod_ref[2:3, :] * y
        x1_ref[...] = x1
        _ffn_fwd_inner(x1, mod_ref, gf_ref, w1_ref, w2_ref, h2_ref, a_ref, z_ref, x2_ref)

    tile = pl.BlockSpec((tm, D), lambda i: (i, 0))
    halo = pl.BlockSpec((16, D), lambda i: (jnp.maximum(i * (tm // 16) - 1, 0), 0))
    row = pl.BlockSpec((1, D), lambda i: (0, 0))
    return _pcall(
        body, name="pool_mix_ffn_fwd", grid=(S // tm,),
        out_shape=(jax.ShapeDtypeStruct((S, D), BF16), jax.ShapeDtypeStruct((S, D), F32)) + _ffn_out_shapes(S, D, F),
        in_specs=[tile, halo, pl.BlockSpec((8, D), lambda i: (0, 0)), row,
                  pl.BlockSpec((len(POOL_WINDOWS), gd, gd), lambda i: (0, 0, 0)), row, row,
                  _wspec(g_w1), _wspec(g_w2)],
        out_specs=(tile, tile, tile, pl.BlockSpec((tm, F), lambda i: (i, 0)), tile, tile),
        scratch_shapes=[pltpu.VMEM((n, D), F32), pltpu.VMEM((n, D), F32), pltpu.VMEM((n, D), F32)],
        operands=(x, x, mod_l, g_mix, pw, ps, g_ffn, g_w1, g_w2), **comm)


def _loss_head(xv, gv, tv, acc_ref):
    D = xv.shape[1]
    r = lax.rsqrt(jnp.mean(xv * xv, axis=-1, keepdims=True) + EPS)
    xhat = xv * r
    err = xhat * gv - tv
    acc_ref[0:1, :] += jnp.sum(err * err, axis=0, keepdims=True)
    dy = err * (1.0 / D)
    acc_ref[1:2, :] += jnp.sum(dy * xhat, axis=0, keepdims=True)
    dxh = dy * gv
    return r * (dxh - xhat * jnp.mean(dxh * xhat, axis=-1, keepdims=True))


def _ffn_bwd(dx2, x1, a, z, mod_l, g_ffn, g_w1, g_w2, head=None, **comm):
    S, D = dx2.shape
    F = a.shape[1]
    f4 = F // NQ
    tm = min(TM, S)
    nh = 2 if head else 0

    def body(*refs):
        dx2_ref, x1_ref, a_ref, z_ref, mod_ref, gf_ref, w1_ref, w2_ref = refs[:8]
        dx1_ref, du_ref, dz_ref, acc_ref = refs[8 + nh:]
        _zero_first(acc_ref)
        dx2v = dx2_ref[...]
        if head:
            dx2v = _loss_head(dx2v, refs[8][...], refs[9][...], acc_ref)
        acc_ref[5:6, :] +=jnp.sum(dx2v * z_ref[...].astype(F32), axis=0, keepdims=True)
        dzb = (dx2v * mod_ref[5:6, :]).astype(BF16)
        dz_ref[...] = dzb
        dh2 = jnp.zeros((tm, D), F32)
        for q in range(NQ):
            av = a_ref[:, q * f4:(q + 1) * f4].astype(F32)
            du = (_dot_nt(dzb, w2_ref[q]) * (2.0 * av)).astype(BF16)
            du_ref[:, q * f4:(q + 1) * f4] = du
            dh2 = dh2 + _dot_nt(du, w1_ref[q])
        g, sc = gf_ref[...], mod_ref[4:5, :]
        _, xhat, r, n = _rms_fwd(x1_ref[...], g, sc, mod_ref[3:4, :])
        dx, dsh, dsc, dg = _rms_bwd(dh2, xhat, r, n, g, sc)
        acc_ref[3:4, :] += dsh
        acc_ref[4:5, :] += dsc
        acc_ref[7:8, :] += dg
        dx1_ref[...] = dx2v + dx

    tile = pl.BlockSpec((tm, D), lambda i: (i, 0))
    wide = pl.BlockSpec((tm, F), lambda i: (i, 0))
    return _pcall(
        body, name="ffn_bwd", grid=(S // tm,),
        out_shape=(jax.ShapeDtypeStruct((S, D), F32), jax.ShapeDtypeStruct((S, F), BF16),
                   jax.ShapeDtypeStruct((S, D), BF16), jax.ShapeDtypeStruct((8, D), F32)),
        in_specs=[tile, tile, wide, tile, pl.BlockSpec((8, D), lambda i: (0, 0)), pl.BlockSpec((1, D), lambda i: (0, 0)),
                  _wspec(g_w1), _wspec(g_w2)] + ([pl.BlockSpec((1, D), lambda i: (0, 0)), tile] if head else []),
        out_specs=(tile, wide, tile, pl.BlockSpec((8, D), lambda i: (0, 0))),
        operands=(dx2, x1, a, z, mod_l, g_ffn, g_w1, g_w2) + (tuple(head) if head else ()), **comm)


def _dw_blocked(a, b, by_rows, square_a, name):
    S = a.shape[0]
    tk = min(TK, S)
    nk = S // tk
    if by_rows:
        bm, bn = a.shape[1] // NQ, b.shape[1]
        a_map, b_map = (lambda q, k: (k, q)), (lambda q, k: (k, 0))
    else:
        bm, bn = a.shape[1], b.shape[1] // NQ
        a_map, b_map = (lambda q, k: (k, 0)), (lambda q, k: (k, q))

    def body(a_ref, b_ref, o_ref, acc):
        k = pl.program_id(1)

        @pl.when(k == 0)
        def _():
            acc[...] = jnp.zeros_like(acc)

        av = a_ref[...]
        if square_a:
            av = av * av
        acc[...] += _dot_tn(av, b_ref[...])

        @pl.when(k == nk - 1)
        def _():
            o_ref[...] = acc[...].astype(o_ref.dtype)

    return pl.pallas_call(
        body, name=name, grid=(NQ, nk),
        out_shape=jax.ShapeDtypeStruct((NQ, bm, bn), BF16),
        in_specs=[pl.BlockSpec((tk, bm), a_map), pl.BlockSpec((tk, bn), b_map)],
        out_specs=pl.BlockSpec((None, bm, bn), lambda q, k: (q, 0, 0)),
        scratch_shapes=[pltpu.VMEM((bm, bn), F32)],
        compiler_params=_cparams("parallel", "arbitrary"),
    )(a, b)


def _dw_whole(a, bs, name, **comm):
    S, M = a.shape
    N = bs[0].shape[1]
    tk = min(TK, S)
    nk = S // tk
    nb = len(bs)

    def body(*refs):
        a_ref, b_refs, o_refs, accs = refs[0], refs[1:1 + nb], refs[1 + nb:1 + 2 * nb], refs[1 + 2 * nb:]
        k = pl.program_id(0)

        @pl.when(k == 0)
        def _():
            for acc in accs:
                acc[...] = jnp.zeros_like(acc)

        av = a_ref[...]
        for b_ref, acc in zip(b_refs, accs):
            acc[...] += _dot_tn(av, b_ref[...])

        @pl.when(k == nk - 1)
        def _():
            for o_ref, acc in zip(o_refs, accs):
                o_ref[...] = acc[...].reshape(NQ, M // NQ, N).astype(o_ref.dtype)

    return _pcall(
        body, name=name, grid=(nk,),
        out_shape=tuple(jax.ShapeDtypeStruct((NQ, M // NQ, N), BF16) for _ in bs),
        in_specs=[pl.BlockSpec((tk, M), lambda k: (k, 0))] + [pl.BlockSpec((tk, N), lambda k: (k, 0)) for _ in bs],
        out_specs=tuple(pl.BlockSpec((NQ, M // NQ, N), lambda k: (0, 0, 0)) for _ in bs),
        scratch_shapes=[pltpu.VMEM((M, N), F32) for _ in bs],
        operands=(a, *bs), **comm)


def _lru_bwd(dx1, y, x, xr0, gb, hs, mod_l, g_mix, g_wout, g_wy, g_win, cw, cb, wa, ba, wx, bx, lam, **comm):
    S, W = xr0.shape
    tt = min(TT, S)
    nb = S // tt
    hd = W // HEADS

    def body(dx1_ref, y_ref, x_ref, xr_ref, xrh_ref, gb_ref, hs_ref, hsh_ref, mod_ref, gm_ref, wo_ref, wy_ref, win_ref,
             cw_ref, cb_ref, wa_ref, ba_ref, wx_ref, bx_ref, lam_ref,
             dy_ref, dgb_ref, dxr_ref, dx_ref, sm_ref, dwa_ref, dwx_ref, acc_ref,
             xext, hext, qext, dext, a_s, b_s, qc, dc):
        i = pl.program_id(0)
        blk = nb - 1 - i

        @pl.when(i == 0)
        def _():
            sm_ref[...] = jnp.zeros_like(sm_ref)
            dwa_ref[...] = jnp.zeros_like(dwa_ref)
            dwx_ref[...] = jnp.zeros_like(dwx_ref)
            acc_ref[...] = jnp.zeros_like(acc_ref)
            qc[...] = jnp.zeros_like(qc)
            dc[...] = jnp.zeros_like(dc)

        dx1v = dx1_ref[...]
        acc_ref[2:3, :] += jnp.sum(dx1v * y_ref[...].astype(F32), axis=0, keepdims=True)
        dy = dx1v * mod_ref[2:3, :]
        acc_ref[3:4, :] += jnp.sum(dy, axis=0, keepdims=True)
        dyb = dy.astype(BF16)
        dy_ref[...] = dyb
        dpv = _dot_nt(dyb, wo_ref[...].reshape(W, W))

        xext[0:8, :] = jnp.where(blk > 0, xrh_ref[...], 0.0)
        xext[pl.ds(8, tt), :] = xr_ref[...]
        hext[0:8, :] = jnp.where(blk > 0, hsh_ref[...], 0.0)
        hext[pl.ds(8, tt), :] = hs_ref[...]
        cw = cw_ref[...]
        lam = lam_ref[...]
        xc = _conv_taps(xext, cw, tt) + cb_ref[...]
        xcb, gate_r, gate_i, ls, a, mult = _lru_gates(xc, wa_ref, ba_ref[...], wx_ref, bx_ref[...], lam, hd)

        gbv = gb_ref[...]
        gate, th = _gelu(gbv)
        dgb = dpv * hs_ref[...] * _gelu_grad(gbv, th)
        dgbb = dgb.astype(BF16)
        dgb_ref[...] = dgbb
        sm_ref[9:10, :] += jnp.sum(dgb, axis=0, keepdims=True)
        dhs = dpv * gate

        a_s[...] = a
        b_s[...] = a * dhs
        qext[pl.ds(tt, 8), :] = qc[...]
        row = lax.broadcasted_iota(jnp.int32, (8, W), 0)

        def step(k, _):
            off = pl.multiple_of((tt // 8 - 1 - k) * 8, 8)
            A = a_s[pl.ds(off, 8), :]
            B = b_s[pl.ds(off, 8), :]
            for d in (1, 2, 4):
                keep = row < 8 - d
                Bs = jnp.where(keep, pltpu.roll(B, 8 - d, 0), 0.0)
                As = jnp.where(keep, pltpu.roll(A, 8 - d, 0), 1.0)
                B = B + A * Bs
                A = A * As
            Q = B + A * qc[...]
            qext[pl.ds(off, 8), :] = Q
            qc[...] = jnp.broadcast_to(Q[0:1, :], (8, W))
            return 0

        lax.fori_loop(0, tt // 8, step, 0)
        gsc = dhs + qext[pl.ds(1, tt), :]
        da = gsc * hext[pl.ds(7, tt), :]
        t1 = gsc * xc
        dmult = t1 * gate_i
        dgate_i = t1 * mult
        dxc = gsc * (mult * gate_i)
        dlog_a = da * a - dmult * (a * a) / mult
        dgate_r = dlog_a * (LRU_C * ls)
        sm_ref[7:8, :] += jnp.sum(dlog_a * (LRU_C * gate_r), axis=0, keepdims=True)
        dga = dgate_r * gate_r * (1.0 - gate_r)
        dgx = dgate_i * gate_i * (1.0 - gate_i)
        sm_ref[5:6, :] += jnp.sum(dga, axis=0, keepdims=True)
        sm_ref[6:7, :] += jnp.sum(dgx, axis=0, keepdims=True)
        dgab = dga.astype(BF16)
        dgxb = dgx.astype(BF16)
        dxc = dxc + _heads_dot(dgab, wa_ref, hd, nt=True) + _heads_dot(dgxb, wx_ref, hd, nt=True)
        for h in range(HEADS):
            sl = slice(h * hd, (h + 1) * hd)
            dwa_ref[h] += _dot_tn(xcb[:, sl], dgab[:, sl])
            dwx_ref[h] += _dot_tn(xcb[:, sl], dgxb[:, sl])
        sm_ref[4:5, :] += jnp.sum(dxc, axis=0, keepdims=True)
        dext[pl.ds(0, tt), :] = dxc
        dext[pl.ds(tt, 8), :] = dc[...]
        xrv = xr_ref[...]
        dxr = None
        for k in range(CONV_W):
            up = dext[pl.ds(CONV_W - 1 - k, tt), :]
            sm_ref[k:k + 1, :] += jnp.sum(up * xrv, axis=0, keepdims=True)
            dxr = cw[k:k + 1, :] * up if dxr is None else dxr + cw[k:k + 1, :] * up
        dc[...] = dext[0:8, :]
        sm_ref[8:9, :] += jnp.sum(dxr, axis=0, keepdims=True)
        dxrb = dxr.astype(BF16)
        dxr_ref[...] = dxrb

        dh = _dot_nt(dxrb, win_ref[...].reshape(W, W)) + _dot_nt(dgbb, wy_ref[...].reshape(W, W))
        g, sc = gm_ref[...], mod_ref[1:2, :]
        _, xhat, r, n = _rms_fwd(x_ref[...], g, sc, mod_ref[0:1, :])
        dx, dsh, dsc, dg = _rms_bwd(dh, xhat, r, n, g, sc)
        acc_ref[0:1, :] += dsh
        acc_ref[1:2, :] += dsc
        acc_ref[6:7, :] += dg
        dx_ref[...] = dx1v + dx

        @pl.when(i == nb - 1)
        def _():
            sm_ref[7:8, :] = sm_ref[7:8, :] * jax.nn.sigmoid(-lam)

    rev = lambda i: (nb - 1 - i, 0)
    tile = pl.BlockSpec((tt, W), rev)
    halo = pl.BlockSpec((8, W), lambda i: (jnp.maximum((nb - 1 - i) * (tt // 8) - 1, 0), 0))
    row = pl.BlockSpec((1, W), lambda i: (0, 0))
    wblk = pl.BlockSpec((HEADS, hd, hd), lambda i: (0, 0, 0))
    bf16o = jax.ShapeDtypeStruct((S, W), BF16)
    return _pcall(
        body, name="lru_bwd", grid=(nb,),
        out_shape=(bf16o, bf16o, bf16o, jax.ShapeDtypeStruct((S, W), F32),
                   jax.ShapeDtypeStruct((16, W), F32), jax.ShapeDtypeStruct((HEADS, hd, hd), F32),
                   jax.ShapeDtypeStruct((HEADS, hd, hd), F32), jax.ShapeDtypeStruct((8, W), F32)),
        in_specs=[tile, tile, tile, tile, halo, tile, tile, halo, pl.BlockSpec((8, W), lambda i: (0, 0)), row,
                  _wspec(g_wout), _wspec(g_wy), _wspec(g_win), pl.BlockSpec((CONV_W, W), lambda i: (0, 0)), row,
                  wblk, row, wblk, row, row],
        out_specs=(tile, tile, tile, tile, pl.BlockSpec((16, W), lambda i: (0, 0)), wblk, wblk,
                   pl.BlockSpec((8, W), lambda i: (0, 0))),
        scratch_shapes=[pltpu.VMEM((tt + 8, W), F32), pltpu.VMEM((tt + 8, W), F32), pltpu.VMEM((tt + 8, W), F32),
                        pltpu.VMEM((tt + 8, W), F32), pltpu.VMEM((tt, W), F32), pltpu.VMEM((tt, W), F32),
                        pltpu.VMEM((8, W), F32), pltpu.VMEM((8, W), F32)],
        operands=(dx1, y, x, xr0, xr0, gb, hs, hs, mod_l, g_mix, g_wout, g_wy, g_win, cw, cb, wa, ba, wx, bx, lam),
        **comm)


def _pool_bwd(dx1, x, pooled, mod_l, g_mix, pw, ps):
    S, D = x.shape
    tm = min(TP, S)
    nb = S // tm
    ng = len(POOL_WINDOWS)
    gd = D // ng
    n = tm + 24

    def body(dx1_ref, dxh_ref, x_ref, pl_ref, mod_ref, gm_ref, pw_ref, ps_ref, dx_ref, acc_ref, dpw_ref, ext, b1, b2):
        i = pl.program_id(0)

        @pl.when(i == 0)
        def _():
            acc_ref[...] = jnp.zeros_like(acc_ref)
            dpw_ref[...] = jnp.zeros_like(dpw_ref)

        gt, psv = mod_ref[2:3, :], ps_ref[...]
        wvec = _window_vec(D)
        dx1v = dx1_ref[...]
        pooled = pl_ref[...]
        mixed = _heads_dot(pooled, pw_ref, gd)
        acc_ref[2:3, :] += jnp.sum(dx1v * (mixed * psv), axis=0, keepdims=True)
        dy = dx1v * gt
        acc_ref[3:4, :] += jnp.sum(dy * mixed, axis=0, keepdims=True)
        dmix = (dy * psv).astype(BF16)
        for gi in range(ng):
            sl = slice(gi * gd, (gi + 1) * gd)
            dpw_ref[gi] += _dot_tn(pooled[:, sl], dmix[:, sl])
        dpooled = _heads_dot(dmix, pw_ref, gd, nt=True)
        dmix_h = (dxh_ref[...] * gt * psv).astype(BF16)
        dpooled_h = jnp.where(i < nb - 1, _heads_dot(dmix_h, pw_ref, gd, nt=True), 0.0)
        t1 = (lax.broadcasted_iota(jnp.int32, (tm, 1), 0) + (i * tm + 1)).astype(F32)
        t1h = (lax.broadcasted_iota(jnp.int32, (16, 1), 0) + ((i + 1) * tm + 1)).astype(F32)
        zeros8 = jnp.zeros((8, D), F32)
        ext[pl.ds(0, tm), :] = dpooled / jnp.minimum(t1, wvec)
        ext[pl.ds(tm, 16), :] = dpooled_h / jnp.minimum(t1h, wvec)
        ext[pl.ds(tm + 16, 8), :] = zeros8
        b1[pl.ds(tm + 16, 8), :] = zeros8
        b2[pl.ds(tm + 16, 8), :] = zeros8
        m = n - 8
        b1[pl.ds(0, m), :] = ext[pl.ds(0, m), :] + ext[pl.ds(1, m), :]
        b2[pl.ds(0, m), gd:] = b1[pl.ds(0, m), gd:] + b1[pl.ds(2, m), gd:]
        b1[pl.ds(0, m), 2 * gd:] = b2[pl.ds(0, m), 2 * gd:] + b2[pl.ds(4, m), 2 * gd:]
        b2[pl.ds(0, m), 3 * gd:] = b1[pl.ds(0, m), 3 * gd:] + b1[pl.ds(8, m), 3 * gd:]
        wsum = jnp.concatenate([b1[pl.ds(0, tm), 0:gd], b2[pl.ds(0, tm), gd:2 * gd],
                                b1[pl.ds(0, tm), 2 * gd:3 * gd], b2[pl.ds(0, tm), 3 * gd:]], axis=1)
        dh = wsum - dpooled
        g, sc = gm_ref[...], mod_ref[1:2, :]
        _, xhat, r, nn = _rms_fwd(x_ref[...], g, sc, mod_ref[0:1, :])
        dx, dsh, dsc, dg = _rms_bwd(dh, xhat, r, nn, g, sc)
        acc_ref[0:1, :] += dsh
        acc_ref[1:2, :] += dsc
        acc_ref[6:7, :] += dg
        dx_ref[...] = dx1v + dx

    tile = pl.BlockSpec((tm, D), lambda i: (i, 0))
    halo = pl.BlockSpec((16, D), lambda i: (jnp.minimum((i + 1) * (tm // 16), S // 16 - 1), 0))
    row = pl.BlockSpec((1, D), lambda i: (0, 0))
    wblk = pl.BlockSpec((ng, gd, gd), lambda i: (0, 0, 0))
    return pl.pallas_call(
        body, name="pool_bwd", grid=(nb,),
        out_shape=(jax.ShapeDtypeStruct((S, D), F32), jax.ShapeDtypeStruct((8, D), F32),
                   jax.ShapeDtypeStruct((ng, gd, gd), F32)),
        in_specs=[tile, halo, tile, tile, pl.BlockSpec((8, D), lambda i: (0, 0)), row, wblk, row],
        out_specs=(tile, pl.BlockSpec((8, D), lambda i: (0, 0)), wblk),
        scratch_shapes=[pltpu.VMEM((n, D), F32), pltpu.VMEM((n, D), F32), pltpu.VMEM((n, D), F32)],
        compiler_params=_cparams("arbitrary"),
    )(dx1, dx1, x, pooled, mod_l, g_mix, pw, ps)


def _shard_to_rows(w, D):
    return w.reshape(-1, D)


def _blockdiag_full(gq, na, hd):
    return gq.reshape(NQ, na, HEADS, hd // NQ, hd).transpose(1, 2, 0, 3, 4).reshape(na, HEADS, hd, hd)


def _blockdiag_by_chip(dw, D):
    na, _, hd, _ = dw.shape
    return dw.reshape(na, HEADS, NQ, hd // NQ, hd).transpose(2, 0, 1, 3, 4).reshape(NQ, -1, D)


def kernel(x, c, w_mod, b_mod, norm_mix_g, norm_ffn_g, lru_w_y, lru_b_y, lru_w_in, lru_b_in, lru_conv_w, lru_conv_b, lru_w_a, lru_b_a, lru_w_x, lru_b_x, lru_lambda, lru_w_out, lru_b_out, pool_w, pool_scale, ffn_w1, ffn_w2, final_norm_g, loss_target, m_w_mod, m_b_mod, m_norm_mix_g, m_norm_ffn_g, m_lru_w_y, m_lru_b_y, m_lru_w_in, m_lru_b_in, m_lru_conv_w, m_lru_conv_b, m_lru_w_a, m_lru_b_a, m_lru_w_x, m_lru_b_x, m_lru_lambda, m_lru_w_out, m_lru_b_out, m_pool_w, m_pool_scale, m_ffn_w1, m_ffn_w2, m_final_norm_g, v_w_mod, v_b_mod, v_norm_mix_g, v_norm_ffn_g, v_lru_w_y, v_lru_b_y, v_lru_w_in, v_lru_b_in, v_lru_conv_w, v_lru_conv_b, v_lru_w_a, v_lru_b_a, v_lru_w_x, v_lru_b_x, v_lru_lambda, v_lru_w_out, v_lru_b_out, v_pool_w, v_pool_scale, v_ffn_w1, v_ffn_w2, v_final_norm_g):
    S, D = x.shape[1], x.shape[2]
    L = w_mod.shape[0]
    NA = lru_w_y.shape[0]
    NB = pool_w.shape[0]
    F = ffn_w1.shape[2] * NQ
    f4 = F // NQ
    hd = D // HEADS
    Cs = w_mod.shape[2]
    assert L == DEPTH and Cs * NQ == N_MOD * D and D % 1024 == 0
    x2d = x.reshape(S, D)
    tgt = loss_target.reshape(S, D)
    q = 2 * lax.axis_index("x") + lax.axis_index("y")

    big = [ffn_w1, ffn_w2, lru_w_y, lru_w_in, lru_w_out, lru_w_a, lru_w_x, pool_w]
    rows = [int(w.size) // D for w in big]
    offs = [sum(rows[:k]) for k in range(len(big))]
    O_W1, O_W2, O_WY, O_WIN, O_WOUT, O_WA, O_WX, O_PW = offs
    R = sum(rows)
    dq = D // NQ
    s_w1 = [ffn_w1[i].astype(BF16) for i in range(L)]
    s_w2 = [ffn_w2[i].astype(BF16) for i in range(L)]
    s_wy = [lru_w_y[j].astype(BF16) for j in range(NA)]
    s_win = [lru_w_in[j].astype(BF16) for j in range(NA)]
    s_wout = [lru_w_out[j].astype(BF16) for j in range(NA)]
    s_tiny = jnp.concatenate([_shard_to_rows(w, D) for w in (lru_w_a, lru_w_x, pool_w)], axis=0).astype(BF16)

    cshard = lru_conv_w.reshape(-1)
    small_fwd = jnp.concatenate([c.reshape(-1), cshard, lru_b_a.reshape(-1), lru_b_x.reshape(-1),
                                 pool_scale.reshape(-1)])
    small_fwd = jnp.pad(small_fwd, (0, 8 * D - small_fwd.shape[0])).reshape(8, D)

    g_w1, g_w2 = [None] * L, [None] * L
    g_wy, g_win, g_wout = [None] * NA, [None] * NA, [None] * NA
    SG, g_wy[0], g_win[0], g_wout[0], g_tiny = _comm_only("gather_first", small=small_fwd,
                                                         gathers=(s_wy[0], s_win[0], s_wout[0], s_tiny))
    wa_full = _blockdiag_full(g_tiny[:, :rows[5]], NA, hd)
    wx_full = _blockdiag_full(g_tiny[:, rows[5]:rows[5] + rows[6]], NA, hd)
    pw_full = _blockdiag_full(g_tiny[:, rows[5] + rows[6]:], NB, hd)
    SGf = SG.reshape(NDEV, 8 * D)
    c_all = SGf[:, :D]
    SGq = SGf.reshape(NQ, 2, 8 * D)[:, 0]
    o = D
    n_cw = NA * CONV_W * D // NQ
    conv_w_full = SGq[:, o:o + n_cw].reshape(NQ, NA, CONV_W, D // NQ).transpose(1, 2, 0, 3).reshape(NA, CONV_W, D)
    o += n_cw
    n_b = NA * HEADS * hd // NQ
    b_a_full = SGq[:, o:o + n_b].reshape(NQ, NA, HEADS, hd // NQ).transpose(1, 2, 0, 3).reshape(NA, 1, D)
    o += n_b
    b_x_full = SGq[:, o:o + n_b].reshape(NQ, NA, HEADS, hd // NQ).transpose(1, 2, 0, 3).reshape(NA, 1, D)
    o += n_b
    n_ps = NB * D // NQ
    pool_scale_full = SGq[:, o:o + n_ps].reshape(NQ, NB, D // NQ).transpose(1, 0, 2).reshape(NB, 1, D)


    b_mod_sh = lax.dynamic_slice_in_dim(b_mod, q * Cs, Cs, axis=1).reshape(L, 1, Cs)
    modpart = _mod_part(c_all, w_mod, b_mod_sh)
    modq = _exchange_mod(modpart.transpose(1, 0, 2))
    mod = modq.transpose(1, 0, 2).reshape(L, N_MOD, D)
    mod = jnp.pad(mod, ((0, 0), (0, 8 - N_MOD), (0, 0)))

    saved = []
    xcur = x2d
    for i in range(L):
        j = i // 2
        gm = norm_mix_g[i].reshape(1, D)
        gf = norm_ffn_g[i].reshape(1, D)
        if i % 2 == 0:
            h, gb, xr0, hs, p, y, x1, g_w1[i], g_w2[i] = _lru_fwd(
                xcur, mod[i], gm, g_wy[j], g_win[j], lru_b_y[j].reshape(1, D), lru_b_in[j].reshape(1, D),
                conv_w_full[j], lru_conv_b[j].reshape(1, D), wa_full[j], b_a_full[j], wx_full[j], b_x_full[j],
                lru_lambda[j].reshape(1, D), g_wout[j], lru_b_out[j].reshape(1, D), gathers=(s_w1[i], s_w2[i]))
            h2, a, z, x2, g_w1[i + 1], g_w2[i + 1] = _ffn_fwd(x1, mod[i], gf, g_w1[i], g_w2[i],
                                                              gathers=(s_w1[i + 1], s_w2[i + 1]))
            saved.append(dict(x=xcur, h=h, gb=gb, xr0=xr0, hs=hs, p=p, y=y, x1=x1, h2=h2, a=a, z=z))
        else:
            if j + 1 < NA:
                pooled, x1, h2, a, z, x2, g_wy[j + 1], g_win[j + 1], g_wout[j + 1] = _pool_mix_ffn_fwd(
                    xcur, mod[i], gm, pw_full[j], pool_scale_full[j], gf, g_w1[i], g_w2[i],
                    gathers=(s_wy[j + 1], s_win[j + 1], s_wout[j + 1]))
            else:
                pooled, x1, h2, a, z, x2 = _pool_mix_ffn_fwd(xcur, mod[i], gm, pw_full[j], pool_scale_full[j], gf,
                                                             g_w1[i], g_w2[i])
            saved.append(dict(x=xcur, pooled=pooled, x1=x1, h2=h2, a=a, z=z))
        xcur = x2

    dx = xcur
    qv = q.reshape(1).astype(jnp.int32)
    ppack = lax.empty((R, D), F32)
    psib = lax.empty((R, D), F32)
    pending, summed = [], []

    def comm_args():
        kw = {}
        if pending:
            kw["scatters"] = tuple(dw for dw, _ in pending)
        if summed:
            kw["sib"] = (ppack, psib, tuple(summed))
        return kw

    def after_host(extra):
        nonlocal ppack, psib, pending, summed
        had_sib = bool(summed)
        summed = []
        for (dw, off), rb in zip(pending, extra[:len(pending)]):
            ppack = _sum_into(ppack, dw, rb, off, qv)
            summed.append((off, dw.shape[1]))
        if had_sib:
            psib = extra[len(pending)]
        pending = []

    dmod_rows = [None] * L
    dg_mix = [None] * L
    dg_ffn = [None] * L
    d_small = {}
    dwa_l, dwx_l, dpw_l = [None] * NA, [None] * NA, [None] * NB
    for i in reversed(range(L)):
        j = i // 2
        sv = saved[i]
        gm = norm_mix_g[i].reshape(1, D)
        gf = norm_ffn_g[i].reshape(1, D)
        head = (final_norm_g.reshape(1, D), tgt) if i == L - 1 else None
        outs = _ffn_bwd(dx, sv["x1"], sv["a"], sv["z"], mod[i], gf, g_w1[i], g_w2[i], head=head, **comm_args())
        dx1, du, dz, facc = outs[:4]
        after_host(outs[4:])
        if head:
            loss = lax.psum(0.5 * jnp.sum(facc[0]) / D, ("x", "y", "c"))
            d_final_g = facc[1]
        pending.append((_dw_blocked(sv["h2"], du, False, False, "dw1"), O_W1 + i * D))
        pending.append((_dw_blocked(sv["a"], dz, True, True, "dw2"), O_W2 + i * f4))
        if i % 2 == 0:
            outs = _lru_bwd(dx1, sv["y"], sv["x"], sv["xr0"], sv["gb"], sv["hs"], mod[i], gm, g_wout[j], g_wy[j],
                            g_win[j], conv_w_full[j], lru_conv_b[j].reshape(1, D), wa_full[j], b_a_full[j], wx_full[j],
                            b_x_full[j], lru_lambda[j].reshape(1, D), **comm_args())
            dyp, dgb, dxr, dx, sm, dwa, dwx, macc = outs[:8]
            after_host(outs[8:])
            dwa_l[j], dwx_l[j] = dwa, dwx
            if i == 0:
                tiny = jnp.concatenate([_blockdiag_by_chip(jnp.stack(dwa_l), D), _blockdiag_by_chip(jnp.stack(dwx_l), D),
                                        _blockdiag_by_chip(jnp.stack(dpw_l), D)], axis=1).astype(BF16)
                pending.append((tiny, O_WA))
            outs = _dw_whole(sv["p"], [dyp], "dwout", **comm_args())
            after_host(outs[1:])
            pending.append((outs[0], O_WOUT + j * dq))
            outs = _dw_whole(sv["h"], [dgb, dxr], "dwy_dwin", **comm_args())
            after_host(outs[2:])
            pending.append((outs[0], O_WY + j * dq))
            pending.append((outs[1], O_WIN + j * dq))
            d_small[("lru", j)] = (sm, macc[3])
            dgt_m = macc[2]
        else:
            dx, macc, dpw = _pool_bwd(dx1, sv["x"], sv["pooled"], mod[i], gm, pw_full[j], pool_scale_full[j])
            dpw_l[j] = dpw
            d_small[("pool", j)] = macc[3]
            dgt_m = macc[2]
        dmod_rows[i] = jnp.stack([macc[0], macc[1], dgt_m, facc[3], facc[4], facc[5]])
        dg_mix[i] = macc[6]
        dg_ffn[i] = facc[7]
    grad_x = dx.reshape(x.shape)

    lru_sm = [d_small[("lru", j)] for j in range(NA)]
    small_rows = [jnp.stack(dmod_rows).reshape(L * N_MOD, D), jnp.stack(dg_mix), jnp.stack(dg_ffn),
                  jnp.stack([s[0][9] for s in lru_sm]), jnp.stack([s[0][8] for s in lru_sm]),
                  jnp.stack([s[0][4] for s in lru_sm]), jnp.stack([s[0][7] for s in lru_sm]),
                  jnp.stack([s[1] for s in lru_sm]),
                  jnp.stack([s[0][0:CONV_W] for s in lru_sm]).reshape(NA * CONV_W, D),
                  jnp.stack([s[0][5] for s in lru_sm]), jnp.stack([s[0][6] for s in lru_sm]),
                  jnp.stack([d_small[("pool", j)] for j in range(NB)]), d_final_g.reshape(1, D)]
    small_g = jnp.concatenate(small_rows, axis=0)
    n_small = small_g.shape[0]
    assert n_small <= SMALL_ROWS
    small_g = jnp.pad(small_g, ((0, SMALL_ROWS - n_small), (0, 0)))

    outs = _comm_only("scatter_last", small=small_g, reduce_small=True, **comm_args())
    sg_all, sg_sum = outs[:2]
    after_host(outs[2:])
    psum_mine = ppack
    psum_sib = _comm_only("sibling_last", sib=(ppack, psib, tuple(summed)))[0]

    def big_update(w, m, v, off, name):
        shp = w.shape
        g, dl, m2, v2 = _adam_rows(w.reshape(-1, D), m.reshape(-1, D), v.reshape(-1, D), psum_mine, psum_sib, off, name)
        return g.reshape(shp), dl.reshape(shp), m2.reshape(shp), v2.reshape(shp)

    res = {}
    res["ffn_w1"] = big_update(ffn_w1, m_ffn_w1, v_ffn_w1, O_W1, "adam_w1")
    res["ffn_w2"] = big_update(ffn_w2, m_ffn_w2, v_ffn_w2, O_W2, "adam_w2")
    res["lru_w_y"] = big_update(lru_w_y, m_lru_w_y, v_lru_w_y, O_WY, "adam_wy")
    res["lru_w_in"] = big_update(lru_w_in, m_lru_w_in, v_lru_w_in, O_WIN, "adam_win")
    res["lru_w_out"] = big_update(lru_w_out, m_lru_w_out, v_lru_w_out, O_WOUT, "adam_wout")
    res["lru_w_a"] = big_update(lru_w_a, m_lru_w_a, v_lru_w_a, O_WA, "adam_wa")
    res["lru_w_x"] = big_update(lru_w_x, m_lru_w_x, v_lru_w_x, O_WX, "adam_wx")
    res["pool_w"] = big_update(pool_w, m_pool_w, v_pool_w, O_PW, "adam_pw")

    dmod_all = sg_all[:, :L * N_MOD, :].reshape(NDEV, L, N_MOD * D)
    dmod_sh = lax.dynamic_slice_in_dim(dmod_all, q * Cs, Cs, axis=2).transpose(1, 0, 2)
    res["w_mod"] = _wmod_update(c_all.T, dmod_sh, w_mod, m_w_mod, v_w_mod)

    r0 = 0

    def take(nrows):
        nonlocal r0
        out = sg_sum[r0:r0 + nrows]
        r0 += nrows
        return out

    g_b_mod = take(L * N_MOD).reshape(L, N_MOD * D)
    g_mix = take(L)
    g_ffn = take(L)
    g_b_y = take(NA)
    g_b_in = take(NA)
    g_conv_b = take(NA)
    g_lam = take(NA)
    g_b_out = take(NA)
    g_conv_w = lax.dynamic_slice_in_dim(take(NA * CONV_W).reshape(NA, CONV_W, D), q * (D // NQ), D // NQ, axis=2)
    g_b_a = lax.dynamic_slice_in_dim(take(NA).reshape(NA, HEADS, hd), q * (hd // NQ), hd // NQ, axis=2)
    g_b_x = lax.dynamic_slice_in_dim(take(NA).reshape(NA, HEADS, hd), q * (hd // NQ), hd // NQ, axis=2)
    g_ps = lax.dynamic_slice_in_dim(take(NB), q * (D // NQ), D // NQ, axis=1)
    g_fin = take(1).reshape(D)

    smalls = [("b_mod", b_mod, m_b_mod, v_b_mod, g_b_mod), ("norm_mix_g", norm_mix_g, m_norm_mix_g, v_norm_mix_g, g_mix),
              ("norm_ffn_g", norm_ffn_g, m_norm_ffn_g, v_norm_ffn_g, g_ffn),
              ("lru_b_y", lru_b_y, m_lru_b_y, v_lru_b_y, g_b_y), ("lru_b_in", lru_b_in, m_lru_b_in, v_lru_b_in, g_b_in),
              ("lru_conv_w", lru_conv_w, m_lru_conv_w, v_lru_conv_w, g_conv_w),
              ("lru_conv_b", lru_conv_b, m_lru_conv_b, v_lru_conv_b, g_conv_b),
              ("lru_b_a", lru_b_a, m_lru_b_a, v_lru_b_a, g_b_a), ("lru_b_x", lru_b_x, m_lru_b_x, v_lru_b_x, g_b_x),
              ("lru_lambda", lru_lambda, m_lru_lambda, v_lru_lambda, g_lam),
              ("lru_b_out", lru_b_out, m_lru_b_out, v_lru_b_out, g_b_out),
              ("pool_scale", pool_scale, m_pool_scale, v_pool_scale, g_ps),
              ("final_norm_g", final_norm_g, m_final_norm_g, v_final_norm_g, g_fin)]
    total = sum(int(s[1].size) for s in smalls)
    prow = -(-total // (8 * 128)) * 8

    def flat(k):
        f = jnp.concatenate([s[k].reshape(-1) for s in smalls])
        return jnp.pad(f, (0, prow * 128 - total)).reshape(prow, 128)

    dl_s, m_s, v_s = _adam_small(flat(1), flat(4), flat(2), flat(3))
    o = 0
    for name, w, _, _, g in smalls:
        sz = int(w.size)
        cut = lambda arr: arr.reshape(-1)[o:o + sz].reshape(w.shape)
        res[name] = (g.reshape(w.shape), cut(dl_s), cut(m_s), cut(v_s))
        o += sz

    order = ["w_mod", "b_mod", "norm_mix_g", "norm_ffn_g", "lru_w_y", "lru_b_y", "lru_w_in", "lru_b_in", "lru_conv_w",
             "lru_conv_b", "lru_w_a", "lru_b_a", "lru_w_x", "lru_b_x", "lru_lambda", "lru_w_out", "lru_b_out", "pool_w",
             "pool_scale", "ffn_w1", "ffn_w2", "final_norm_g"]
    return (loss, grad_x, *[res[n][0] for n in order], *[res[n][1] for n in order],
            *[res[n][2] for n in order], *[res[n][3] for n in order])
```

```python
import functools

import jax
import jax.numpy as jnp
from jax import lax
from jax.experimental import pallas as pl
from jax.experimental.pallas import tpu as pltpu

F32 = jnp.float32
BF16 = jnp.bfloat16
MESH = pl.DeviceIdType.MESH

NQ = 4
NDEV = 8
DEPTH = 4
N_MOD = 6
HEADS = 4
CONV_W = 4
POOL_WINDOWS = (2, 4, 8, 16)
LRU_C = 8.0
EPS = 1e-6
ADAM_LR, ADAM_B1, ADAM_B2, ADAM_EPS, ADAM_WD, ADAM_STEP = 0.001, 0.9, 0.999, 1e-08, 0.01, 10

TM = 512
TT = 256
TP = 256
TK = 2048
SMALL_ROWS = 64
FORWARD_STEPS = 4
VMEM_LIMIT = 60 * 1024 * 1024


def _cparams(*sem):
    return pltpu.CompilerParams(dimension_semantics=tuple(sem), vmem_limit_bytes=VMEM_LIMIT)


def _dot(a, b):
    return jnp.dot(a, b, preferred_element_type=F32)


def _dot_nt(a, b):
    return lax.dot_general(a, b, (((1,), (1,)), ((), ())), preferred_element_type=F32)


def _dot_tn(a, b):
    return lax.dot_general(a, b, (((0,), (0,)), ((), ())), preferred_element_type=F32)


def _resident(shape, index_map):
    return pl.BlockSpec(shape, index_map, pipeline_mode=pl.Buffered(1))


def _rms_fwd(x, g, sc, sh):
    r = lax.rsqrt(jnp.mean(x * x, axis=-1, keepdims=True) + EPS)
    xhat = x * r
    n = xhat * g
    return n * (1.0 + sc) + sh, xhat, r, n


def _rms_bwd(dh, xhat, r, n, g, sc):
    dsh = jnp.sum(dh, axis=0, keepdims=True)
    dsc = jnp.sum(dh * n, axis=0, keepdims=True)
    dn = dh * (1.0 + sc)
    dg = jnp.sum(dn * xhat, axis=0, keepdims=True)
    dxh = dn * g
    dx = r * (dxh - xhat * jnp.mean(dxh * xhat, axis=-1, keepdims=True))
    return dx, dsh, dsc, dg


_GELU_K = 0.7978845608028654
_GELU_C = 0.044715


def _gelu(x):
    t = jnp.tanh(_GELU_K * (x + _GELU_C * x * x * x))
    return 0.5 * x * (1.0 + t), t


def _gelu_grad(x, t):
    return 0.5 * (1.0 + t) + 0.5 * x * (1.0 - t * t) * (_GELU_K * (1.0 + 3.0 * _GELU_C * x * x))


def _neg_expm1(y, exp_y):
    series = -(y * (1.0 + y * (0.5 + y * (1.0 / 6.0))))
    return jnp.where(y > -(1.0 / 64.0), series, 1.0 - exp_y)


def _zero_first(ref):
    @pl.when(pl.program_id(0) == 0)
    def _():
        ref[...] = jnp.zeros_like(ref)


def _my_pos():
    return lax.axis_index("x"), lax.axis_index("y"), lax.axis_index("c")


def _dev_index(x, y, c):
    return 4 * x + 2 * y + c


def _chip_peers(x, y):
    return [(1 - x, y), (x, 1 - y), (1 - x, 1 - y)]


def _all_peers(x, y, c):
    return [(px, py, c) for (px, py) in _chip_peers(x, y)] + [(x, y, 1 - c)] + \
           [(px, py, 1 - c) for (px, py) in _chip_peers(x, y)]


def _comm_run(phase, x, y, c, gathers, scatters, sib, send, recv, loc):
    q = 2 * x + y
    peers = _chip_peers(x, y)
    sibling = (x, y, 1 - c)

    def rcopy(src, dst, s, dev):
        return pltpu.make_async_remote_copy(src, dst, send.at[s], recv.at[s], device_id=dev, device_id_type=MESH)

    s = 0
    for gi, (src, dst) in enumerate(gathers):
        half = src.shape[0] // 2
        mine, other = pl.ds(c * half, half), pl.ds((1 - c) * half, half)
        own = pltpu.make_async_copy(src, dst.at[q], loc.at[gi])
        if phase == "start":
            own.start()
        elif phase == "finish":
            own.wait()
        for (px, py) in peers:
            pq = 2 * px + py
            s_ici, s_fwd = s, s + 1
            s += 2
            if phase == "start":
                rcopy(src.at[mine], dst.at[q].at[mine], s_ici, (px, py, c)).start()
            elif phase == "forward":
                rcopy(src.at[mine], dst.at[pq].at[mine], s_ici, (px, py, c)).wait_recv()
                rcopy(dst.at[pq].at[mine], dst.at[pq].at[mine], s_fwd, sibling).start()
            else:
                rcopy(dst.at[pq].at[other], dst.at[pq].at[other], s_fwd, sibling).wait_recv()
                rcopy(src.at[mine], dst.at[q].at[mine], s_ici, (px, py, c)).wait_send()
                rcopy(dst.at[pq].at[mine], dst.at[pq].at[mine], s_fwd, sibling).wait_send()
    direct = []
    for (src, dst) in scatters:
        for k, (px, py) in enumerate(peers):
            direct.append((src.at[2 * px + py], dst.at[k], (px, py, c)))
    if sib is not None:
        src, dst, ranges = sib
        for (off, rows) in ranges:
            direct.append((src.at[pl.ds(off, rows)], dst.at[pl.ds(off, rows)], sibling))
    if phase == "start":
        for k, (a, b, dev) in enumerate(direct):
            rcopy(a, b, s + k, dev).start()
    elif phase == "finish":
        for k, (a, b, dev) in enumerate(direct):
            rcopy(a, b, s + k, dev).wait_recv()
        for k, (a, b, dev) in enumerate(direct):
            rcopy(a, b, s + k, dev).wait_send()


def _comm_shapes(gathers, scatters, sib):
    assert all(g.shape[0] % 32 == 0 for g in gathers)
    cin = list(gathers) + list(scatters) + ([sib[0], sib[1]] if sib else [])
    cout = [jax.ShapeDtypeStruct((NQ,) + g.shape, g.dtype) for g in gathers] + \
           [jax.ShapeDtypeStruct((3,) + s.shape[1:], s.dtype) for s in scatters] + \
           ([jax.ShapeDtypeStruct(sib[1].shape, sib[1].dtype)] if sib else [])
    n_rem = 6 * len(gathers) + 3 * len(scatters) + (len(sib[2]) if sib else 0)
    sems = [pltpu.SemaphoreType.DMA((max(n_rem, 1),)), pltpu.SemaphoreType.DMA((max(n_rem, 1),)),
            pltpu.SemaphoreType.DMA((max(len(gathers), 1),))]
    return cin, cout, sems


def _pcall(body, *, name, grid, in_specs, out_specs, out_shape, operands, scratch_shapes=(),
           gathers=(), scatters=(), sib=None):
    assert len(grid) == 1
    out_shape, out_specs = tuple(out_shape), tuple(out_specs)
    if not (gathers or scatters or sib):
        return pl.pallas_call(body, name=name, grid=grid, in_specs=list(in_specs), out_specs=out_specs,
                              out_shape=out_shape, scratch_shapes=list(scratch_shapes),
                              compiler_params=_cparams("arbitrary"))(*operands)
    cin, cout, sems = _comm_shapes(gathers, scatters, sib)
    n_in, n_cin, n_out, n_cout, n_scr = len(operands), len(cin), len(out_shape), len(cout), len(scratch_shapes)
    ng, ns = len(gathers), len(scatters)
    nsteps = grid[0]

    def wrapped(*refs):
        ins = refs[:n_in]
        cins = refs[n_in:n_in + n_cin]
        o0 = n_in + n_cin
        outs = refs[o0:o0 + n_out]
        couts = refs[o0 + n_out:o0 + n_out + n_cout]
        s0 = o0 + n_out + n_cout
        scr = refs[s0:s0 + n_scr]
        send, recv, loc = refs[s0 + n_scr:s0 + n_scr + 3]
        x, y, c = _my_pos()

        def run(phase):
            g = [(cins[k], couts[k]) for k in range(ng)]
            sc = [(cins[ng + k], couts[ng + k]) for k in range(ns)]
            sb = (cins[ng + ns], couts[ng + ns], sib[2]) if sib else None
            _comm_run(phase, x, y, c, g, sc, sb, send, recv, loc)

        @pl.when(pl.program_id(0) == 0)
        def _():
            run("start")

        if ng:
            @pl.when(pl.program_id(0) == max(nsteps - FORWARD_STEPS, 0))
            def _():
                run("forward")

        body(*ins, *outs, *scr)

        @pl.when(pl.program_id(0) == nsteps - 1)
        def _():
            run("finish")

    anyspec = pl.BlockSpec(memory_space=pl.ANY)
    aliases = {n_in + ng + ns + 1: n_out + ng + ns} if sib else {}
    return pl.pallas_call(
        wrapped, name=name, grid=grid,
        in_specs=list(in_specs) + [anyspec] * n_cin, out_specs=out_specs + (anyspec,) * n_cout,
        out_shape=out_shape + tuple(cout), scratch_shapes=list(scratch_shapes) + sems,
        input_output_aliases=aliases,
        compiler_params=pltpu.CompilerParams(dimension_semantics=("arbitrary",), vmem_limit_bytes=VMEM_LIMIT,
                                             has_side_effects=True),
    )(*operands, *cin)


def _comm_only(name, small=None, reduce_small=False, gathers=(), scatters=(), sib=None):
    cin, cout, sems = _comm_shapes(gathers, scatters, sib)
    n_cin, n_cout = len(cin), len(cout)
    ng, ns = len(gathers), len(scatters)
    n_sm_in = 1 if small is not None else 0
    n_sm_out = (2 if reduce_small else 1) if small is not None else 0

    def body(*refs):
        sm_in = refs[:n_sm_in]
        cins = refs[n_sm_in:n_sm_in + n_cin]
        o0 = n_sm_in + n_cin
        sm_out = refs[o0:o0 + n_sm_out]
        couts = refs[o0 + n_sm_out:o0 + n_sm_out + n_cout]
        s0 = o0 + n_sm_out + n_cout
        send, recv, loc = refs[s0:s0 + 3]
        x, y, c = _my_pos()
        g = [(cins[k], couts[k]) for k in range(ng)]
        sc = [(cins[ng + k], couts[ng + k]) for k in range(ns)]
        sb = (cins[ng + ns], couts[ng + ns], sib[2]) if sib else None
        _comm_run("start", x, y, c, g, sc, sb, send, recv, loc)
        if small is not None:
            sm_send, sm_recv = refs[s0 + 3:s0 + 5]
            small_ref, sg_ref = sm_in[0], sm_out[0]
            me = _dev_index(x, y, c)
            sg_ref[me] = small_ref[...]
            peers = _all_peers(x, y, c)
            sm = [pltpu.make_async_remote_copy(small_ref, sg_ref.at[me], sm_send.at[k], sm_recv.at[k],
                                               device_id=peer, device_id_type=MESH) for k, peer in enumerate(peers)]
            for cp in sm:
                cp.start()
            for k, (px, py, pc) in enumerate(peers):
                pltpu.make_async_remote_copy(small_ref, sg_ref.at[_dev_index(px, py, pc)], sm_send.at[k], sm_recv.at[k],
                                             device_id=(px, py, pc), device_id_type=MESH).wait_recv()
            if reduce_small:
                acc = sg_ref[0]
                for d in range(1, NDEV):
                    acc = acc + sg_ref[d]
                sm_out[1][...] = acc
            for cp in sm:
                cp.wait_send()
        if ng:
            _comm_run("forward", x, y, c, g, sc, sb, send, recv, loc)
        _comm_run("finish", x, y, c, g, sc, sb, send, recv, loc)

    anyspec = pl.BlockSpec(memory_space=pl.ANY)
    vspec = pl.BlockSpec(memory_space=pltpu.VMEM)
    sm_shapes = []
    if small is not None:
        sm_shapes.append(jax.ShapeDtypeStruct((NDEV,) + small.shape, small.dtype))
        if reduce_small:
            sm_shapes.append(jax.ShapeDtypeStruct(small.shape, small.dtype))
        sems = sems + [pltpu.SemaphoreType.DMA((NDEV - 1,)), pltpu.SemaphoreType.DMA((NDEV - 1,))]
    aliases = {n_sm_in + ng + ns + 1: n_sm_out + ng + ns} if sib else {}
    return pl.pallas_call(
        body, name=name,
        in_specs=[vspec] * n_sm_in + [anyspec] * n_cin,
        out_specs=tuple([vspec] * n_sm_out + [anyspec] * n_cout),
        out_shape=tuple(sm_shapes + cout), scratch_shapes=sems, input_output_aliases=aliases,
        compiler_params=pltpu.CompilerParams(has_side_effects=True),
    )(*([small] if small is not None else []), *cin)


def _exchange_mod(modpart):
    _, L, Cs = modpart.shape

    def body(part_ref, out_ref, send, recv):
        x, y, c = _my_pos()
        q = 2 * x + y
        me = _dev_index(x, y, c)
        out_ref[q] = part_ref[me]
        sends = []
        for k, (px, py) in enumerate(_chip_peers(x, y)):
            cp = pltpu.make_async_remote_copy(part_ref.at[_dev_index(px, py, c)], out_ref.at[q], send.at[k], recv.at[k],
                                              device_id=(px, py, c), device_id_type=MESH)
            cp.start()
            sends.append(cp)
        for k, (px, py) in enumerate(_chip_peers(x, y)):
            pltpu.make_async_remote_copy(part_ref.at[me], out_ref.at[2 * px + py], send.at[k], recv.at[k],
                                         device_id=(px, py, c), device_id_type=MESH).wait_recv()
        for cp in sends:
            cp.wait_send()

    return pl.pallas_call(
        body, name="exchange_mod",
        out_shape=jax.ShapeDtypeStruct((NQ, L, Cs), modpart.dtype),
        in_specs=[pl.BlockSpec(memory_space=pltpu.VMEM)],
        out_specs=pl.BlockSpec(memory_space=pltpu.VMEM),
        scratch_shapes=[pltpu.SemaphoreType.DMA((3,)), pltpu.SemaphoreType.DMA((3,))],
        compiler_params=pltpu.CompilerParams(has_side_effects=True),
    )(modpart)


def _mod_part(c_all, w_mod, b_mod_sh):
    L, D, Cs = w_mod.shape
    tn = 512 if Cs % 512 == 0 else Cs

    def body(c_ref, w_ref, b_ref, o_ref):
        cv = c_ref[...]
        cond = cv * jax.nn.sigmoid(cv)
        o_ref[...] = jnp.dot(cond, w_ref[...], preferred_element_type=F32, precision=lax.Precision.HIGHEST) + b_ref[...]

    return pl.pallas_call(
        body, name="mod_part", grid=(L, Cs // tn),
        out_shape=jax.ShapeDtypeStruct((L, NDEV, Cs), F32),
        in_specs=[pl.BlockSpec((NDEV, D), lambda i, j: (0, 0)),
                  pl.BlockSpec((None, D, tn), lambda i, j: (i, 0, j)),
                  pl.BlockSpec((None, 1, tn), lambda i, j: (i, 0, j))],
        out_specs=pl.BlockSpec((None, NDEV, tn), lambda i, j: (i, 0, j)),
        compiler_params=_cparams("parallel", "parallel"),
    )(c_all, w_mod, b_mod_sh)


def _adam(w, g, m, v):
    m2 = ADAM_B1 * m + (1.0 - ADAM_B1) * g
    v2 = ADAM_B2 * v + (1.0 - ADAM_B2) * (g * g)
    m_hat = m2 / (1.0 - ADAM_B1 ** ADAM_STEP)
    v_hat = v2 / (1.0 - ADAM_B2 ** ADAM_STEP)
    delta = -ADAM_LR * (m_hat / (jnp.sqrt(v_hat) + ADAM_EPS) + ADAM_WD * w)
    return delta, m2, v2


def _wmod_update(c_all_t, dmod_sh, w, m, v):
    L, D, Cs = w.shape
    td = 256 if D % 256 == 0 else D

    def body(ct_ref, d_ref, w_ref, m_ref, v_ref, g_ref, dl_ref, m2_ref, v2_ref):
        cv = ct_ref[...]
        cond = cv * jax.nn.sigmoid(cv)
        g = cond[:, 0:1] * d_ref[0:1, :]
        for b in range(1, NDEV):
            g = g + cond[:, b:b + 1] * d_ref[b:b + 1, :]
        g_ref[...] = g
        dl_ref[...], m2_ref[...], v2_ref[...] = _adam(w_ref[...], g, m_ref[...], v_ref[...])

    blk = pl.BlockSpec((None, td, Cs), lambda i, j: (i, j, 0))
    out = jax.ShapeDtypeStruct((L, D, Cs), F32)
    return pl.pallas_call(
        body, name="wmod_update", grid=(L, D // td),
        out_shape=(out, out, out, out),
        in_specs=[pl.BlockSpec((td, NDEV), lambda i, j: (j, 0)),
                  pl.BlockSpec((None, NDEV, Cs), lambda i, j: (i, 0, 0)), blk, blk, blk],
        out_specs=(blk, blk, blk, blk),
        compiler_params=_cparams("parallel", "parallel"),
    )(c_all_t, dmod_sh, w, m, v)


def _adam_rows(w, m, v, pa, pb, row_off, name):
    rows, C = w.shape
    tr = 512 if rows % 512 == 0 else (128 if rows % 128 == 0 else rows)
    assert row_off % tr == 0
    ob = row_off // tr

    def body(w_ref, m_ref, v_ref, pa_ref, pb_ref, g_ref, dl_ref, m2_ref, v2_ref):
        g = pa_ref[...] + pb_ref[...]
        g_ref[...] = g
        dl_ref[...], m2_ref[...], v2_ref[...] = _adam(w_ref[...], g, m_ref[...], v_ref[...])

    blk = pl.BlockSpec((tr, C), lambda i: (i, 0))
    pblk = pl.BlockSpec((tr, C), lambda i: (ob + i, 0))
    out = jax.ShapeDtypeStruct((rows, C), F32)
    return pl.pallas_call(
        body, name=name, grid=(rows // tr,), out_shape=(out, out, out, out),
        in_specs=[blk, blk, blk, pblk, pblk], out_specs=(blk, blk, blk, blk),
        compiler_params=_cparams("parallel"),
    )(w, m, v, pa, pb)


def _adam_small(sg_sum, by_rows, sliced):
    D = sg_sum.shape[1]
    na, nb = len(by_rows), len(sliced)

    def body(*refs):
        sg = refs[0]
        ins_a = [refs[1 + 3 * t:4 + 3 * t] for t in range(na)]
        p = 1 + 3 * na
        ins_b = [refs[p + 4 * t:p + 4 * t + 4] for t in range(nb)]
        p += 4 * nb
        outs_a = [refs[p + 4 * t:p + 4 * t + 4] for t in range(na)]
        p += 4 * na
        outs_b = [refs[p + 3 * t:p + 3 * t + 3] for t in range(nb)]
        for (w_ref, m_ref, v_ref), (g_ref, dl_ref, m2_ref, v2_ref), (w, _, _, row0) in zip(ins_a, outs_a, by_rows):
            n, k = w.shape[0], w.shape[1] // D
            pieces = [(slice(0, n), slice(0, D), slice(row0, row0 + n))] if k == 1 else \
                     [(slice(i, i + 1), slice(kk * D, (kk + 1) * D), slice(row0 + i * k + kk, row0 + i * k + kk + 1))
                      for i in range(n) for kk in range(k)]
            for rs, cs, gs in pieces:
                g = sg[gs, :]
                g_ref[rs, cs] = g
                dl_ref[rs, cs], m2_ref[rs, cs], v2_ref[rs, cs] = _adam(w_ref[rs, cs], g, m_ref[rs, cs], v_ref[rs, cs])
        for (w_ref, m_ref, v_ref, g_ref), (dl_ref, m2_ref, v2_ref) in zip(ins_b, outs_b):
            dl_ref[...], m2_ref[...], v2_ref[...] = _adam(w_ref[...], g_ref[...], m_ref[...], v_ref[...])

    operands = [sg_sum] + [a for t in by_rows for a in t[:3]] + [a for t in sliced for a in t]
    out_shape = [jax.ShapeDtypeStruct(t[0].shape, F32) for t in by_rows for _ in range(4)] + \
                [jax.ShapeDtypeStruct(t[0].shape, F32) for t in sliced for _ in range(3)]
    outs = pl.pallas_call(body, name="adam_small", out_shape=tuple(out_shape))(*operands)
    res_a = [tuple(outs[4 * t:4 * t + 4]) for t in range(na)]
    res_b = [tuple(outs[4 * na + 3 * t:4 * na + 3 * t + 3]) for t in range(nb)]
    return res_a, res_b


def _sum_into(ppack, dw, rb, off, qv):
    _, rows, D = dw.shape
    tr = 256 if rows % 256 == 0 else 128
    assert rows % tr == 0 and off % tr == 0
    ob = off // tr

    def body(q_ref, o_ref, r_ref, pin_ref, p_ref):
        acc = o_ref[...].astype(F32)
        for k in range(3):
            acc = acc + r_ref[k].astype(F32)
        p_ref[...] = acc

    return pl.pallas_call(
        body, name="sum_partials", out_shape=jax.ShapeDtypeStruct(ppack.shape, ppack.dtype),
        grid_spec=pltpu.PrefetchScalarGridSpec(
            num_scalar_prefetch=1, grid=(rows // tr,),
            in_specs=[pl.BlockSpec((None, tr, D), lambda i, q_ref: (q_ref[0], i, 0)),
                      pl.BlockSpec((3, tr, D), lambda i, q_ref: (0, i, 0)),
                      pl.BlockSpec(memory_space=pl.ANY)],
            out_specs=pl.BlockSpec((tr, D), lambda i, q_ref: (ob + i, 0))),
        input_output_aliases={3: 0},
        compiler_params=_cparams("parallel"),
    )(qv, dw, rb, ppack)


def _wspec(g):
    return _resident(g.shape, lambda i: (0, 0, 0))


def _ffn_fwd_inner(x1, mod_ref, gf_ref, w1_ref, w2_ref, h2_ref, a_ref, z_ref, x2_ref):
    h2 = _rms_fwd(x1, gf_ref[...], mod_ref[4:5, :], mod_ref[3:4, :])[0]
    h2b = h2.astype(BF16)
    h2_ref[...] = h2b
    f4 = w1_ref.shape[2]
    z = jnp.zeros(x1.shape, F32)
    for q in range(NQ):
        a = jnp.maximum(_dot(h2b, w1_ref[q]), 0.0)
        a_ref[:, q * f4:(q + 1) * f4] = a.astype(BF16)
        z = z + _dot((a * a).astype(BF16), w2_ref[q])
    z_ref[...] = z.astype(BF16)
    x2_ref[...] = x1 + mod_ref[5:6, :] * z


def _sigmoid(x):
    return 0.5 + 0.5 * jnp.tanh(0.5 * x)


def _heads_dot(xb, w_ref, hd, nt=False):
    outs = []
    for h in range(HEADS):
        xs = xb[:, h * hd:(h + 1) * hd]
        outs.append(_dot_nt(xs, w_ref[h]) if nt else _dot(xs, w_ref[h]))
    return jnp.concatenate(outs, axis=1)


def _lru_gates(xc, wa_ref, ba, wx_ref, bx, lam, hd):
    xcb = xc.astype(BF16)
    gate_r = _sigmoid(_heads_dot(xcb, wa_ref, hd) + ba)
    gate_i = _sigmoid(_heads_dot(xcb, wx_ref, hd) + bx)
    ls = jax.nn.log_sigmoid(lam)
    log_a = gate_r * (LRU_C * ls)
    a = jnp.exp(log_a)
    mult = jnp.sqrt(_neg_expm1(2.0 * log_a, a * a))
    return xcb, gate_r, gate_i, ls, a, mult


def _conv_taps(xext, cw, tt):
    acc = cw[0:1, :] * xext[pl.ds(8 - (CONV_W - 1), tt), :]
    for k in range(1, CONV_W):
        acc = acc + cw[k:k + 1, :] * xext[pl.ds(8 - (CONV_W - 1) + k, tt), :]
    return acc


def _lru_fwd(x, mod_l, g_mix, g_wy, g_win, b_y, b_in, cw, cb, wa, ba, wx, bx, lam, g_wout, b_out, **comm):
    S, W = x.shape
    tt = min(TT, S)
    hd = W // HEADS

    def body(x_ref, mod_ref, g_ref, wy_ref, win_ref, by_ref, bin_ref, cw_ref, cb_ref, wa_ref, ba_ref, wx_ref, bx_ref,
             lam_ref, wo_ref, bo_ref, h_ref, gb_ref, xr_ref, hs_ref, p_ref, y_ref, x1_ref, xext, a_s, u_s, carry):
        i = pl.program_id(0)

        @pl.when(i == 0)
        def _():
            carry[...] = jnp.zeros_like(carry)
            xext[0:8, :] = jnp.zeros((8, W), F32)

        @pl.when(i > 0)
        def _():
            xext[0:8, :] = xext[pl.ds(tt, 8), :]

        xv = x_ref[...]
        hb = _rms_fwd(xv, g_ref[...], mod_ref[1:2, :], mod_ref[0:1, :])[0].astype(BF16)
        h_ref[...] = hb
        gbv = _dot(hb, wy_ref[...].reshape(W, W)) + by_ref[...]
        gb_ref[...] = gbv
        xr = _dot(hb, win_ref[...].reshape(W, W)) + bin_ref[...]
        xr_ref[...] = xr
        xext[pl.ds(8, tt), :] = xr
        xc = _conv_taps(xext, cw_ref[...], tt) + cb_ref[...]
        _, _, gate_i, _, a, mult = _lru_gates(xc, wa_ref, ba_ref[...], wx_ref, bx_ref[...], lam_ref[...], hd)
        a_s[...] = a
        u_s[...] = mult * (gate_i * xc)
        row = lax.broadcasted_iota(jnp.int32, (8, W), 0)

        def step(k, _):
            off = pl.multiple_of(k * 8, 8)
            A = a_s[pl.ds(off, 8), :]
            U = u_s[pl.ds(off, 8), :]
            for d in (1, 2, 4):
                keep = row >= d
                Us = jnp.where(keep, pltpu.roll(U, d, 0), 0.0)
                As = jnp.where(keep, pltpu.roll(A, d, 0), 1.0)
                U = U + A * Us
                A = A * As
            H = U + A * carry[...]
            hs_ref[pl.ds(off, 8), :] = H
            carry[...] = jnp.broadcast_to(H[7:8, :], (8, W))
            return 0

        lax.fori_loop(0, tt // 8, step, 0)
        pb = (hs_ref[...] * _gelu(gbv)[0]).astype(BF16)
        p_ref[...] = pb
        y = _dot(pb, wo_ref[...].reshape(W, W)) + bo_ref[...]
        y_ref[...] = y.astype(BF16)
        x1_ref[...] = xv + mod_ref[2:3, :] * y

    tile = pl.BlockSpec((tt, W), lambda i: (i, 0))
    row = pl.BlockSpec((1, W), lambda i: (0, 0))
    wblk = pl.BlockSpec((HEADS, hd, hd), lambda i: (0, 0, 0))
    f32o, bf16o = jax.ShapeDtypeStruct((S, W), F32), jax.ShapeDtypeStruct((S, W), BF16)
    return _pcall(
        body, name="lru_fwd", grid=(S // tt,),
        out_shape=(bf16o, f32o, f32o, f32o, bf16o, bf16o, f32o),
        in_specs=[tile, pl.BlockSpec((8, W), lambda i: (0, 0)), row, _wspec(g_wy), _wspec(g_win), row, row,
                  pl.BlockSpec((CONV_W, W), lambda i: (0, 0)), row, wblk, row, wblk, row, row, _wspec(g_wout), row],
        out_specs=(tile,) * 7,
        scratch_shapes=[pltpu.VMEM((tt + 8, W), F32), pltpu.VMEM((tt, W), F32), pltpu.VMEM((tt, W), F32),
                        pltpu.VMEM((8, W), F32)],
        operands=(x, mod_l, g_mix, g_wy, g_win, b_y, b_in, cw, cb, wa, ba, wx, bx, lam, g_wout, b_out), **comm)


def _ffn_out_shapes(S, D, F):
    return (jax.ShapeDtypeStruct((S, D), BF16), jax.ShapeDtypeStruct((S, F), BF16),
            jax.ShapeDtypeStruct((S, D), BF16), jax.ShapeDtypeStruct((S, D), F32))


def _ffn_fwd(x1, mod_l, g_ffn, g_w1, g_w2, **comm):
    S, D = x1.shape
    tm = min(TM, S)
    F = g_w1.shape[2] * NQ

    def body(x1_ref, mod_ref, gf_ref, w1_ref, w2_ref, h2_ref, a_ref, z_ref, x2_ref):
        _ffn_fwd_inner(x1_ref[...], mod_ref, gf_ref, w1_ref, w2_ref, h2_ref, a_ref, z_ref, x2_ref)

    tile = pl.BlockSpec((tm, D), lambda i: (i, 0))
    row = pl.BlockSpec((1, D), lambda i: (0, 0))
    return _pcall(
        body, name="ffn_fwd", grid=(S // tm,),
        out_shape=_ffn_out_shapes(S, D, F),
        in_specs=[tile, pl.BlockSpec((8, D), lambda i: (0, 0)), row, _wspec(g_w1), _wspec(g_w2)],
        out_specs=(tile, pl.BlockSpec((tm, F), lambda i: (i, 0)), tile, tile),
        operands=(x1, mod_l, g_ffn, g_w1, g_w2), **comm)


def _window_vec(D):
    gd = D // len(POOL_WINDOWS)
    lane = lax.broadcasted_iota(jnp.int32, (1, D), 1)
    w = jnp.full((1, D), float(POOL_WINDOWS[0]), F32)
    for g in range(1, len(POOL_WINDOWS)):
        w = jnp.where(lane >= g * gd, float(POOL_WINDOWS[g]), w)
    return w


def _pool_mix_ffn_fwd(x, mod_l, g_mix, pw, ps, g_ffn, g_w1, g_w2, **comm):
    S, D = x.shape
    tm = min(TP, S)
    F = g_w1.shape[2] * NQ
    gd = D // len(POOL_WINDOWS)
    n = tm + 24

    def body(x_ref, xh_ref, mod_ref, gm_ref, pw_ref, ps_ref, gf_ref, w1_ref, w2_ref,
             pl_ref, x1_ref, h2_ref, a_ref, z_ref, x2_ref, ext, b1, b2):
        i = pl.program_id(0)
        g, sc, sh = gm_ref[...], mod_ref[1:2, :], mod_ref[0:1, :]
        xv = x_ref[...]
        h = _rms_fwd(xv, g, sc, sh)[0]
        hh = _rms_fwd(xh_ref[...], g, sc, sh)[0]
        zeros8 = jnp.zeros((8, D), F32)
        ext[0:8, :] = zeros8
        b1[0:8, :] = zeros8
        b2[0:8, :] = zeros8
        ext[8:24, :] = jnp.where(i > 0, hh, 0.0)
        ext[pl.ds(24, tm), :] = h
        m = n - 8
        b1[pl.ds(8, m), :] = ext[pl.ds(8, m), :] + ext[pl.ds(7, m), :]
        b2[pl.ds(8, m), gd:] = b1[pl.ds(8, m), gd:] + b1[pl.ds(6, m), gd:]
        b1[pl.ds(8, m), 2 * gd:] = b2[pl.ds(8, m), 2 * gd:] + b2[pl.ds(4, m), 2 * gd:]
        b2[pl.ds(8, m), 3 * gd:] = b1[pl.ds(8, m), 3 * gd:] + b1[pl.ds(0, m), 3 * gd:]
        wsum = jnp.concatenate([b1[pl.ds(24, tm), 0:gd], b2[pl.ds(24, tm), gd:2 * gd],
                                b1[pl.ds(24, tm), 2 * gd:3 * gd], b2[pl.ds(24, tm), 3 * gd:]], axis=1)
        t1 = (lax.broadcasted_iota(jnp.int32, (tm, 1), 0) + (i * tm + 1)).astype(F32)
        cnt = jnp.minimum(t1, _window_vec(D))
        pooled = (wsum / cnt - h).astype(BF16)
        pl_ref[...] = pooled
        y = _heads_dot(pooled, pw_ref, gd) * ps_ref[...]
        x1 = xv + mod_ref[2:3, :] * y
        x1_ref[...] = x1
        _ffn_fwd_inner(x1, mod_ref, gf_ref, w1_ref, w2_ref, h2_ref, a_ref, z_ref, x2_ref)

    tile = pl.BlockSpec((tm, D), lambda i: (i, 0))
    halo = pl.BlockSpec((16, D), lambda i: (jnp.maximum(i * (tm // 16) - 1, 0), 0))
    row = pl.BlockSpec((1, D), lambda i: (0, 0))
    return _pcall(
        body, name="pool_mix_ffn_fwd", grid=(S // tm,),
        out_shape=(jax.ShapeDtypeStruct((S, D), BF16), jax.ShapeDtypeStruct((S, D), F32)) + _ffn_out_shapes(S, D, F),
        in_specs=[tile, halo, pl.BlockSpec((8, D), lambda i: (0, 0)), row,
                  pl.BlockSpec((len(POOL_WINDOWS), gd, gd), lambda i: (0, 0, 0)), row, row,
                  _wspec(g_w1), _wspec(g_w2)],
        out_specs=(tile, tile, tile, pl.BlockSpec((tm, F), lambda i: (i, 0)), tile, tile),
        scratch_shapes=[pltpu.VMEM((n, D), F32), pltpu.VMEM((n, D), F32), pltpu.VMEM((n, D), F32)],
        operands=(x, x, mod_l, g_mix, pw, ps, g_ffn, g_w1, g_w2), **comm)


def _loss_head(xv, gv, tv, acc_ref):
    D = xv.shape[1]
    r = lax.rsqrt(jnp.mean(xv * xv, axis=-1, keepdims=True) + EPS)
    xhat = xv * r
    err = xhat * gv - tv
    acc_ref[0:1, :] += jnp.sum(err * err, axis=0, keepdims=True)
    dy = err * (1.0 / D)
    acc_ref[1:2, :] += jnp.sum(dy * xhat, axis=0, keepdims=True)
    dxh = dy * gv
    return r * (dxh - xhat * jnp.mean(dxh * xhat, axis=-1, keepdims=True))


def _ffn_bwd(dx2, x1, a, z, mod_l, g_ffn, g_w1, g_w2, head=None, **comm):
    S, D = dx2.shape
    F = a.shape[1]
    f4 = F // NQ
    tm = min(TM, S)
    nh = 2 if head else 0

    def body(*refs):
        dx2_ref, x1_ref, a_ref, z_ref, mod_ref, gf_ref, w1_ref, w2_ref = refs[:8]
        dx1_ref, du_ref, dz_ref, acc_ref = refs[8 + nh:]
        _zero_first(acc_ref)
        dx2v = dx2_ref[...]
        if head:
            dx2v = _loss_head(dx2v, refs[8][...], refs[9][...], acc_ref)
        acc_ref[5:6, :] +=jnp.sum(dx2v * z_ref[...].astype(F32), axis=0, keepdims=True)
        dzb = (dx2v * mod_ref[5:6, :]).astype(BF16)
        dz_ref[...] = dzb
        dh2 = jnp.zeros((tm, D), F32)
        for q in range(NQ):
            av = a_ref[:, q * f4:(q + 1) * f4].astype(F32)
            du = (_dot_nt(dzb, w2_ref[q]) * (2.0 * av)).astype(BF16)
            du_ref[:, q * f4:(q + 1) * f4] = du
            dh2 = dh2 + _dot_nt(du, w1_ref[q])
        g, sc = gf_ref[...], mod_ref[4:5, :]
        _, xhat, r, n = _rms_fwd(x1_ref[...], g, sc, mod_ref[3:4, :])
        dx, dsh, dsc, dg = _rms_bwd(dh2, xhat, r, n, g, sc)
        acc_ref[3:4, :] += dsh
        acc_ref[4:5, :] += dsc
        acc_ref[7:8, :] += dg
        dx1_ref[...] = dx2v + dx

    tile = pl.BlockSpec((tm, D), lambda i: (i, 0))
    wide = pl.BlockSpec((tm, F), lambda i: (i, 0))
    return _pcall(
        body, name="ffn_bwd", grid=(S // tm,),
        out_shape=(jax.ShapeDtypeStruct((S, D), F32), jax.ShapeDtypeStruct((S, F), BF16),
                   jax.ShapeDtypeStruct((S, D), BF16), jax.ShapeDtypeStruct((8, D), F32)),
        in_specs=[tile, tile, wide, tile, pl.BlockSpec((8, D), lambda i: (0, 0)), pl.BlockSpec((1, D), lambda i: (0, 0)),
                  _wspec(g_w1), _wspec(g_w2)] + ([pl.BlockSpec((1, D), lambda i: (0, 0)), tile] if head else []),
        out_specs=(tile, wide, tile, pl.BlockSpec((8, D), lambda i: (0, 0))),
        operands=(dx2, x1, a, z, mod_l, g_ffn, g_w1, g_w2) + (tuple(head) if head else ()), **comm)


def _dw_blocked(a, b, by_rows, square_a, name):
    S = a.shape[0]
    tk = min(TK, S)
    nk = S // tk
    if by_rows:
        bm, bn = a.shape[1] // NQ, b.shape[1]
        a_map, b_map = (lambda q, k: (k, q)), (lambda q, k: (k, 0))
    else:
        bm, bn = a.shape[1], b.shape[1] // NQ
        a_map, b_map = (lambda q, k: (k, 0)), (lambda q, k: (k, q))

    def body(a_ref, b_ref, o_ref, acc):
        k = pl.program_id(1)

        @pl.when(k == 0)
        def _():
            acc[...] = jnp.zeros_like(acc)

        av = a_ref[...]
        if square_a:
            av = av * av
        acc[...] += _dot_tn(av, b_ref[...])

        @pl.when(k == nk - 1)
        def _():
            o_ref[...] = acc[...].astype(o_ref.dtype)

    return pl.pallas_call(
        body, name=name, grid=(NQ, nk),
        out_shape=jax.ShapeDtypeStruct((NQ, bm, bn), BF16),
        in_specs=[pl.BlockSpec((tk, bm), a_map), pl.BlockSpec((tk, bn), b_map)],
        out_specs=pl.BlockSpec((None, bm, bn), lambda q, k: (q, 0, 0)),
        scratch_shapes=[pltpu.VMEM((bm, bn), F32)],
        compiler_params=_cparams("parallel", "arbitrary"),
    )(a, b)


def _dw_whole(a, bs, name, **comm):
    S, M = a.shape
    N = bs[0].shape[1]
    tk = min(TK, S)
    nk = S // tk
    nb = len(bs)

    def body(*refs):
        a_ref, b_refs, o_refs, accs = refs[0], refs[1:1 + nb], refs[1 + nb:1 + 2 * nb], refs[1 + 2 * nb:]
        k = pl.program_id(0)

        @pl.when(k == 0)
        def _():
            for acc in accs:
                acc[...] = jnp.zeros_like(acc)

        av = a_ref[...]
        for b_ref, acc in zip(b_refs, accs):
            acc[...] += _dot_tn(av, b_ref[...])

        @pl.when(k == nk - 1)
        def _():
            for o_ref, acc in zip(o_refs, accs):
                o_ref[...] = acc[...].reshape(NQ, M // NQ, N).astype(o_ref.dtype)

    return _pcall(
        body, name=name, grid=(nk,),
        out_shape=tuple(jax.ShapeDtypeStruct((NQ, M // NQ, N), BF16) for _ in bs),
        in_specs=[pl.BlockSpec((tk, M), lambda k: (k, 0))] + [pl.BlockSpec((tk, N), lambda k: (k, 0)) for _ in bs],
        out_specs=tuple(pl.BlockSpec((NQ, M // NQ, N), lambda k: (0, 0, 0)) for _ in bs),
        scratch_shapes=[pltpu.VMEM((M, N), F32) for _ in bs],
        operands=(a, *bs), **comm)


def _lru_bwd(dx1, y, x, xr0, gb, hs, mod_l, g_mix, g_wout, g_wy, g_win, cw, cb, wa, ba, wx, bx, lam, **comm):
    S, W = xr0.shape
    tt = min(TT, S)
    nb = S // tt
    hd = W // HEADS

    def body(dx1_ref, y_ref, x_ref, xr_ref, xrh_ref, gb_ref, hs_ref, hsh_ref, mod_ref, gm_ref, wo_ref, wy_ref, win_ref,
             cw_ref, cb_ref, wa_ref, ba_ref, wx_ref, bx_ref, lam_ref,
             dy_ref, dgb_ref, dxr_ref, dx_ref, sm_ref, dwa_ref, dwx_ref, acc_ref,
             xext, hext, qext, dext, a_s, b_s, qc, dc):
        i = pl.program_id(0)
        blk = nb - 1 - i

        @pl.when(i == 0)
        def _():
            sm_ref[...] = jnp.zeros_like(sm_ref)
            dwa_ref[...] = jnp.zeros_like(dwa_ref)
            dwx_ref[...] = jnp.zeros_like(dwx_ref)
            acc_ref[...] = jnp.zeros_like(acc_ref)
            qc[...] = jnp.zeros_like(qc)
            dc[...] = jnp.zeros_like(dc)

        dx1v = dx1_ref[...]
        acc_ref[2:3, :] += jnp.sum(dx1v * y_ref[...].astype(F32), axis=0, keepdims=True)
        dy = dx1v * mod_ref[2:3, :]
        acc_ref[3:4, :] += jnp.sum(dy, axis=0, keepdims=True)
        dyb = dy.astype(BF16)
        dy_ref[...] = dyb
        dpv = _dot_nt(dyb, wo_ref[...].reshape(W, W))

        xext[0:8, :] = jnp.where(blk > 0, xrh_ref[...], 0.0)
        xext[pl.ds(8, tt), :] = xr_ref[...]
        hext[0:8, :] = jnp.where(blk > 0, hsh_ref[...], 0.0)
        hext[pl.ds(8, tt), :] = hs_ref[...]
        cw = cw_ref[...]
        lam = lam_ref[...]
        xc = _conv_taps(xext, cw, tt) + cb_ref[...]
        xcb, gate_r, gate_i, ls, a, mult = _lru_gates(xc, wa_ref, ba_ref[...], wx_ref, bx_ref[...], lam, hd)

        gbv = gb_ref[...]
        gate, th = _gelu(gbv)
        dgb = dpv * hs_ref[...] * _gelu_grad(gbv, th)
        dgbb = dgb.astype(BF16)
        dgb_ref[...] = dgbb
        sm_ref[9:10, :] += jnp.sum(dgb, axis=0, keepdims=True)
        dhs = dpv * gate

        a_s[...] = a
        b_s[...] = a * dhs
        qext[pl.ds(tt, 8), :] = qc[...]
        row = lax.broadcasted_iota(jnp.int32, (8, W), 0)

        def step(k, _):
            off = pl.multiple_of((tt // 8 - 1 - k) * 8, 8)
            A = a_s[pl.ds(off, 8), :]
            B = b_s[pl.ds(off, 8), :]
            for d in (1, 2, 4):
                keep = row < 8 - d
                Bs = jnp.where(keep, pltpu.roll(B, 8 - d, 0), 0.0)
                As = jnp.where(keep, pltpu.roll(A, 8 - d, 0), 1.0)
                B = B + A * Bs
                A = A * As
            Q = B + A * qc[...]
            qext[pl.ds(off, 8), :] = Q
            qc[...] = jnp.broadcast_to(Q[0:1, :], (8, W))
            return 0

        lax.fori_loop(0, tt // 8, step, 0)
        gsc = dhs + qext[pl.ds(1, tt), :]
        da = gsc * hext[pl.ds(7, tt), :]
        t1 = gsc * xc
        dmult = t1 * gate_i
        dgate_i = t1 * mult
        dxc = gsc * (mult * gate_i)
        dlog_a = da * a - dmult * (a * a) / mult
        dgate_r = dlog_a * (LRU_C * ls)
        sm_ref[7:8, :] += jnp.sum(dlog_a * (LRU_C * gate_r), axis=0, keepdims=True)
        dga = dgate_r * gate_r * (1.0 - gate_r)
        dgx = dgate_i * gate_i * (1.0 - gate_i)
        sm_ref[5:6, :] += jnp.sum(dga, axis=0, keepdims=True)
        sm_ref[6:7, :] += jnp.sum(dgx, axis=0, keepdims=True)
        dgab = dga.astype(BF16)
        dgxb = dgx.astype(BF16)
        dxc = dxc + _heads_dot(dgab, wa_ref, hd, nt=True) + _heads_dot(dgxb, wx_ref, hd, nt=True)
        for h in range(HEADS):
            sl = slice(h * hd, (h + 1) * hd)
            dwa_ref[h] += _dot_tn(xcb[:, sl], dgab[:, sl])
            dwx_ref[h] += _dot_tn(xcb[:, sl], dgxb[:, sl])
        sm_ref[4:5, :] += jnp.sum(dxc, axis=0, keepdims=True)
        for k in range(CONV_W):
            sm_ref[k:k + 1, :] += jnp.sum(dxc * xext[pl.ds(8 - (CONV_W - 1) + k, tt), :], axis=0, keepdims=True)
        dext[pl.ds(0, tt), :] = dxc
        dext[pl.ds(tt, 8), :] = dc[...]
        dxr = cw[0:1, :] * dext[pl.ds(CONV_W - 1, tt), :]
        for k in range(1, CONV_W):
            dxr = dxr + cw[k:k + 1, :] * dext[pl.ds(CONV_W - 1 - k, tt), :]
        dc[...] = dext[0:8, :]
        sm_ref[8:9, :] += jnp.sum(dxr, axis=0, keepdims=True)
        dxrb = dxr.astype(BF16)
        dxr_ref[...] = dxrb

        dh = _dot_nt(dxrb, win_ref[...].reshape(W, W)) + _dot_nt(dgbb, wy_ref[...].reshape(W, W))
        g, sc = gm_ref[...], mod_ref[1:2, :]
        _, xhat, r, n = _rms_fwd(x_ref[...], g, sc, mod_ref[0:1, :])
        dx, dsh, dsc, dg = _rms_bwd(dh, xhat, r, n, g, sc)
        acc_ref[0:1, :] += dsh
        acc_ref[1:2, :] += dsc
        acc_ref[6:7, :] += dg
        dx_ref[...] = dx1v + dx

        @pl.when(i == nb - 1)
        def _():
            sm_ref[7:8, :] = sm_ref[7:8, :] * jax.nn.sigmoid(-lam)

    rev = lambda i: (nb - 1 - i, 0)
    tile = pl.BlockSpec((tt, W), rev)
    halo = pl.BlockSpec((8, W), lambda i: (jnp.maximum((nb - 1 - i) * (tt // 8) - 1, 0), 0))
    row = pl.BlockSpec((1, W), lambda i: (0, 0))
    wblk = pl.BlockSpec((HEADS, hd, hd), lambda i: (0, 0, 0))
    bf16o = jax.ShapeDtypeStruct((S, W), BF16)
    return _pcall(
        body, name="lru_bwd", grid=(nb,),
        out_shape=(bf16o, bf16o, bf16o, jax.ShapeDtypeStruct((S, W), F32),
                   jax.ShapeDtypeStruct((16, W), F32), jax.ShapeDtypeStruct((HEADS, hd, hd), F32),
                   jax.ShapeDtypeStruct((HEADS, hd, hd), F32), jax.ShapeDtypeStruct((8, W), F32)),
        in_specs=[tile, tile, tile, tile, halo, tile, tile, halo, pl.BlockSpec((8, W), lambda i: (0, 0)), row,
                  _wspec(g_wout), _wspec(g_wy), _wspec(g_win), pl.BlockSpec((CONV_W, W), lambda i: (0, 0)), row,
                  wblk, row, wblk, row, row],
        out_specs=(tile, tile, tile, tile, pl.BlockSpec((16, W), lambda i: (0, 0)), wblk, wblk,
                   pl.BlockSpec((8, W), lambda i: (0, 0))),
        scratch_shapes=[pltpu.VMEM((tt + 8, W), F32), pltpu.VMEM((tt + 8, W), F32), pltpu.VMEM((tt + 8, W), F32),
                        pltpu.VMEM((tt + 8, W), F32), pltpu.VMEM((tt, W), F32), pltpu.VMEM((tt, W), F32),
                        pltpu.VMEM((8, W), F32), pltpu.VMEM((8, W), F32)],
        operands=(dx1, y, x, xr0, xr0, gb, hs, hs, mod_l, g_mix, g_wout, g_wy, g_win, cw, cb, wa, ba, wx, bx, lam),
        **comm)


def _pool_bwd(dx1, x, pooled, mod_l, g_mix, pw, ps):
    S, D = x.shape
    tm = min(TP, S)
    nb = S // tm
    ng = len(POOL_WINDOWS)
    gd = D // ng
    n = tm + 24

    def body(dx1_ref, dxh_ref, x_ref, pl_ref, mod_ref, gm_ref, pw_ref, ps_ref, dx_ref, acc_ref, dpw_ref, ext, b1, b2):
        i = pl.program_id(0)

        @pl.when(i == 0)
        def _():
            acc_ref[...] = jnp.zeros_like(acc_ref)
            dpw_ref[...] = jnp.zeros_like(dpw_ref)

        gt, psv = mod_ref[2:3, :], ps_ref[...]
        wvec = _window_vec(D)
        dx1v = dx1_ref[...]
        pooled = pl_ref[...]
        mixed = _heads_dot(pooled, pw_ref, gd)
        acc_ref[2:3, :] += jnp.sum(dx1v * (mixed * psv), axis=0, keepdims=True)
        dy = dx1v * gt
        acc_ref[3:4, :] += jnp.sum(dy * mixed, axis=0, keepdims=True)
        dmix = (dy * psv).astype(BF16)
        for gi in range(ng):
            sl = slice(gi * gd, (gi + 1) * gd)
            dpw_ref[gi] += _dot_tn(pooled[:, sl], dmix[:, sl])
        dpooled = _heads_dot(dmix, pw_ref, gd, nt=True)
        dmix_h = (dxh_ref[...] * gt * psv).astype(BF16)
        dpooled_h = jnp.where(i < nb - 1, _heads_dot(dmix_h, pw_ref, gd, nt=True), 0.0)
        t1 = (lax.broadcasted_iota(jnp.int32, (tm, 1), 0) + (i * tm + 1)).astype(F32)
        t1h = (lax.broadcasted_iota(jnp.int32, (16, 1), 0) + ((i + 1) * tm + 1)).astype(F32)
        zeros8 = jnp.zeros((8, D), F32)
        ext[pl.ds(0, tm), :] = dpooled / jnp.minimum(t1, wvec)
        ext[pl.ds(tm, 16), :] = dpooled_h / jnp.minimum(t1h, wvec)
        ext[pl.ds(tm + 16, 8), :] = zeros8
        b1[pl.ds(tm + 16, 8), :] = zeros8
        b2[pl.ds(tm + 16, 8), :] = zeros8
        m = n - 8
        b1[pl.ds(0, m), :] = ext[pl.ds(0, m), :] + ext[pl.ds(1, m), :]
        b2[pl.ds(0, m), gd:] = b1[pl.ds(0, m), gd:] + b1[pl.ds(2, m), gd:]
        b1[pl.ds(0, m), 2 * gd:] = b2[pl.ds(0, m), 2 * gd:] + b2[pl.ds(4, m), 2 * gd:]
        b2[pl.ds(0, m), 3 * gd:] = b1[pl.ds(0, m), 3 * gd:] + b1[pl.ds(8, m), 3 * gd:]
        wsum = jnp.concatenate([b1[pl.ds(0, tm), 0:gd], b2[pl.ds(0, tm), gd:2 * gd],
                                b1[pl.ds(0, tm), 2 * gd:3 * gd], b2[pl.ds(0, tm), 3 * gd:]], axis=1)
        dh = wsum - dpooled
        g, sc = gm_ref[...], mod_ref[1:2, :]
        _, xhat, r, nn = _rms_fwd(x_ref[...], g, sc, mod_ref[0:1, :])
        dx, dsh, dsc, dg = _rms_bwd(dh, xhat, r, nn, g, sc)
        acc_ref[0:1, :] += dsh
        acc_ref[1:2, :] += dsc
        acc_ref[6:7, :] += dg
        dx_ref[...] = dx1v + dx

    tile = pl.BlockSpec((tm, D), lambda i: (i, 0))
    halo = pl.BlockSpec((16, D), lambda i: (jnp.minimum((i + 1) * (tm // 16), S // 16 - 1), 0))
    row = pl.BlockSpec((1, D), lambda i: (0, 0))
    wblk = pl.BlockSpec((ng, gd, gd), lambda i: (0, 0, 0))
    return pl.pallas_call(
        body, name="pool_bwd", grid=(nb,),
        out_shape=(jax.ShapeDtypeStruct((S, D), F32), jax.ShapeDtypeStruct((8, D), F32),
                   jax.ShapeDtypeStruct((ng, gd, gd), F32)),
        in_specs=[tile, halo, tile, tile, pl.BlockSpec((8, D), lambda i: (0, 0)), row, wblk, row],
        out_specs=(tile, pl.BlockSpec((8, D), lambda i: (0, 0)), wblk),
        scratch_shapes=[pltpu.VMEM((n, D), F32), pltpu.VMEM((n, D), F32), pltpu.VMEM((n, D), F32)],
        compiler_params=_cparams("arbitrary"),
    )(dx1, dx1, x, pooled, mod_l, g_mix, pw, ps)


def _shard_to_rows(w, D):
    return w.reshape(-1, D)


def _blockdiag_full(gq, na, hd):
    return gq.reshape(NQ, na, HEADS, hd // NQ, hd).transpose(1, 2, 0, 3, 4).reshape(na, HEADS, hd, hd)


def _blockdiag_by_chip(dw, D):
    na, _, hd, _ = dw.shape
    return dw.reshape(na, HEADS, NQ, hd // NQ, hd).transpose(2, 0, 1, 3, 4).reshape(NQ, -1, D)


def kernel(x, c, w_mod, b_mod, norm_mix_g, norm_ffn_g, lru_w_y, lru_b_y, lru_w_in, lru_b_in, lru_conv_w, lru_conv_b, lru_w_a, lru_b_a, lru_w_x, lru_b_x, lru_lambda, lru_w_out, lru_b_out, pool_w, pool_scale, ffn_w1, ffn_w2, final_norm_g, loss_target, m_w_mod, m_b_mod, m_norm_mix_g, m_norm_ffn_g, m_lru_w_y, m_lru_b_y, m_lru_w_in, m_lru_b_in, m_lru_conv_w, m_lru_conv_b, m_lru_w_a, m_lru_b_a, m_lru_w_x, m_lru_b_x, m_lru_lambda, m_lru_w_out, m_lru_b_out, m_pool_w, m_pool_scale, m_ffn_w1, m_ffn_w2, m_final_norm_g, v_w_mod, v_b_mod, v_norm_mix_g, v_norm_ffn_g, v_lru_w_y, v_lru_b_y, v_lru_w_in, v_lru_b_in, v_lru_conv_w, v_lru_conv_b, v_lru_w_a, v_lru_b_a, v_lru_w_x, v_lru_b_x, v_lru_lambda, v_lru_w_out, v_lru_b_out, v_pool_w, v_pool_scale, v_ffn_w1, v_ffn_w2, v_final_norm_g):
    S, D = x.shape[1], x.shape[2]
    L = w_mod.shape[0]
    NA = lru_w_y.shape[0]
    NB = pool_w.shape[0]
    F = ffn_w1.shape[2] * NQ
    f4 = F // NQ
    hd = D // HEADS
    Cs = w_mod.shape[2]
    assert L == DEPTH and Cs * NQ == N_MOD * D and D % 1024 == 0
    x2d = x.reshape(S, D)
    tgt = loss_target.reshape(S, D)
    q = 2 * lax.axis_index("x") + lax.axis_index("y")

    big = [ffn_w1, ffn_w2, lru_w_y, lru_w_in, lru_w_out, lru_w_a, lru_w_x, pool_w]
    rows = [int(w.size) // D for w in big]
    offs = [sum(rows[:k]) for k in range(len(big))]
    O_W1, O_W2, O_WY, O_WIN, O_WOUT, O_WA, O_WX, O_PW = offs
    R = sum(rows)
    dq = D // NQ
    s_w1 = [ffn_w1[i].astype(BF16) for i in range(L)]
    s_w2 = [ffn_w2[i].astype(BF16) for i in range(L)]
    s_wy = [lru_w_y[j].astype(BF16) for j in range(NA)]
    s_win = [lru_w_in[j].astype(BF16) for j in range(NA)]
    s_wout = [lru_w_out[j].astype(BF16) for j in range(NA)]
    s_tiny = jnp.concatenate([_shard_to_rows(w, D) for w in (lru_w_a, lru_w_x, pool_w)], axis=0).astype(BF16)

    cshard = lru_conv_w.reshape(-1)
    small_fwd = jnp.concatenate([c.reshape(-1), cshard, lru_b_a.reshape(-1), lru_b_x.reshape(-1),
                                 pool_scale.reshape(-1)])
    small_fwd = jnp.pad(small_fwd, (0, 8 * D - small_fwd.shape[0])).reshape(8, D)

    g_w1, g_w2 = [None] * L, [None] * L
    g_wy, g_win, g_wout = [None] * NA, [None] * NA, [None] * NA
    SG, g_wy[0], g_win[0], g_wout[0], g_tiny = _comm_only("gather_first", small=small_fwd,
                                                         gathers=(s_wy[0], s_win[0], s_wout[0], s_tiny))
    wa_full = _blockdiag_full(g_tiny[:, :rows[5]], NA, hd)
    wx_full = _blockdiag_full(g_tiny[:, rows[5]:rows[5] + rows[6]], NA, hd)
    pw_full = _blockdiag_full(g_tiny[:, rows[5] + rows[6]:], NB, hd)
    SGf = SG.reshape(NDEV, 8 * D)
    c_all = SGf[:, :D]
    SGq = SGf.reshape(NQ, 2, 8 * D)[:, 0]
    o = D
    n_cw = NA * CONV_W * D // NQ
    conv_w_full = SGq[:, o:o + n_cw].reshape(NQ, NA, CONV_W, D // NQ).transpose(1, 2, 0, 3).reshape(NA, CONV_W, D)
    o += n_cw
    n_b = NA * HEADS * hd // NQ
    b_a_full = SGq[:, o:o + n_b].reshape(NQ, NA, HEADS, hd // NQ).transpose(1, 2, 0, 3).reshape(NA, 1, D)
    o += n_b
    b_x_full = SGq[:, o:o + n_b].reshape(NQ, NA, HEADS, hd // NQ).transpose(1, 2, 0, 3).reshape(NA, 1, D)
    o += n_b
    n_ps = NB * D // NQ
    pool_scale_full = SGq[:, o:o + n_ps].reshape(NQ, NB, D // NQ).transpose(1, 0, 2).reshape(NB, 1, D)


    b_mod_sh = lax.dynamic_slice_in_dim(b_mod, q * Cs, Cs, axis=1).reshape(L, 1, Cs)
    modpart = _mod_part(c_all, w_mod, b_mod_sh)
    modq = _exchange_mod(modpart.transpose(1, 0, 2))
    mod = modq.transpose(1, 0, 2).reshape(L, N_MOD, D)
    mod = jnp.pad(mod, ((0, 0), (0, 8 - N_MOD), (0, 0)))

    saved = []
    xcur = x2d
    for i in range(L):
        j = i // 2
        gm = norm_mix_g[i].reshape(1, D)
        gf = norm_ffn_g[i].reshape(1, D)
        if i % 2 == 0:
            h, gb, xr0, hs, p, y, x1, g_w1[i], g_w2[i] = _lru_fwd(
                xcur, mod[i], gm, g_wy[j], g_win[j], lru_b_y[j].reshape(1, D), lru_b_in[j].reshape(1, D),
                conv_w_full[j], lru_conv_b[j].reshape(1, D), wa_full[j], b_a_full[j], wx_full[j], b_x_full[j],
                lru_lambda[j].reshape(1, D), g_wout[j], lru_b_out[j].reshape(1, D), gathers=(s_w1[i], s_w2[i]))
            h2, a, z, x2, g_w1[i + 1], g_w2[i + 1] = _ffn_fwd(x1, mod[i], gf, g_w1[i], g_w2[i],
                                                              gathers=(s_w1[i + 1], s_w2[i + 1]))
            saved.append(dict(x=xcur, h=h, gb=gb, xr0=xr0, hs=hs, p=p, y=y, x1=x1, h2=h2, a=a, z=z))
        else:
            if j + 1 < NA:
                pooled, x1, h2, a, z, x2, g_wy[j + 1], g_win[j + 1], g_wout[j + 1] = _pool_mix_ffn_fwd(
                    xcur, mod[i], gm, pw_full[j], pool_scale_full[j], gf, g_w1[i], g_w2[i],
                    gathers=(s_wy[j + 1], s_win[j + 1], s_wout[j + 1]))
            else:
                pooled, x1, h2, a, z, x2 = _pool_mix_ffn_fwd(xcur, mod[i], gm, pw_full[j], pool_scale_full[j], gf,
                                                             g_w1[i], g_w2[i])
            saved.append(dict(x=xcur, pooled=pooled, x1=x1, h2=h2, a=a, z=z))
        xcur = x2

    dx = xcur
    qv = q.reshape(1).astype(jnp.int32)
    ppack = lax.empty((R, D), F32)
    psib = lax.empty((R, D), F32)
    pending, summed = [], []

    def comm_args():
        kw = {}
        if pending:
            kw["scatters"] = tuple(dw for dw, _ in pending)
        if summed:
            kw["sib"] = (ppack, psib, tuple(summed))
        return kw

    def after_host(extra):
        nonlocal ppack, psib, pending, summed
        had_sib = bool(summed)
        summed = []
        for (dw, off), rb in zip(pending, extra[:len(pending)]):
            ppack = _sum_into(ppack, dw, rb, off, qv)
            summed.append((off, dw.shape[1]))
        if had_sib:
            psib = extra[len(pending)]
        pending = []

    dmod_rows = [None] * L
    dg_mix = [None] * L
    dg_ffn = [None] * L
    d_small = {}
    dwa_l, dwx_l, dpw_l = [None] * NA, [None] * NA, [None] * NB
    for i in reversed(range(L)):
        j = i // 2
        sv = saved[i]
        gm = norm_mix_g[i].reshape(1, D)
        gf = norm_ffn_g[i].reshape(1, D)
        head = (final_norm_g.reshape(1, D), tgt) if i == L - 1 else None
        outs = _ffn_bwd(dx, sv["x1"], sv["a"], sv["z"], mod[i], gf, g_w1[i], g_w2[i], head=head, **comm_args())
        dx1, du, dz, facc = outs[:4]
        after_host(outs[4:])
        if head:
            loss = lax.psum(0.5 * jnp.sum(facc[0]) / D, ("x", "y", "c"))
            d_final_g = facc[1]
        pending.append((_dw_blocked(sv["h2"], du, False, False, "dw1"), O_W1 + i * D))
        pending.append((_dw_blocked(sv["a"], dz, True, True, "dw2"), O_W2 + i * f4))
        if i % 2 == 0:
            outs = _lru_bwd(dx1, sv["y"], sv["x"], sv["xr0"], sv["gb"], sv["hs"], mod[i], gm, g_wout[j], g_wy[j],
                            g_win[j], conv_w_full[j], lru_conv_b[j].reshape(1, D), wa_full[j], b_a_full[j], wx_full[j],
                            b_x_full[j], lru_lambda[j].reshape(1, D), **comm_args())
            dyp, dgb, dxr, dx, sm, dwa, dwx, macc = outs[:8]
            after_host(outs[8:])
            dwa_l[j], dwx_l[j] = dwa, dwx
            if i == 0:
                tiny = jnp.concatenate([_blockdiag_by_chip(jnp.stack(dwa_l), D), _blockdiag_by_chip(jnp.stack(dwx_l), D),
                                        _blockdiag_by_chip(jnp.stack(dpw_l), D)], axis=1).astype(BF16)
                pending.append((tiny, O_WA))
            outs = _dw_whole(sv["p"], [dyp], "dwout", **comm_args())
            after_host(outs[1:])
            pending.append((outs[0], O_WOUT + j * dq))
            outs = _dw_whole(sv["h"], [dgb, dxr], "dwy_dwin", **comm_args())
            after_host(outs[2:])
            pending.append((outs[0], O_WY + j * dq))
            pending.append((outs[1], O_WIN + j * dq))
            d_small[("lru", j)] = (sm, macc[3])
            dgt_m = macc[2]
        else:
            dx, macc, dpw = _pool_bwd(dx1, sv["x"], sv["pooled"], mod[i], gm, pw_full[j], pool_scale_full[j])
            dpw_l[j] = dpw
            d_small[("pool", j)] = macc[3]
            dgt_m = macc[2]
        dmod_rows[i] = jnp.stack([macc[0], macc[1], dgt_m, facc[3], facc[4], facc[5]])
        dg_mix[i] = macc[6]
        dg_ffn[i] = facc[7]
    grad_x = dx.reshape(x.shape)

    lru_sm = [d_small[("lru", j)] for j in range(NA)]
    small_rows = [jnp.stack(dmod_rows).reshape(L * N_MOD, D), jnp.stack(dg_mix), jnp.stack(dg_ffn),
                  jnp.stack([s[0][9] for s in lru_sm]), jnp.stack([s[0][8] for s in lru_sm]),
                  jnp.stack([s[0][4] for s in lru_sm]), jnp.stack([s[0][7] for s in lru_sm]),
                  jnp.stack([s[1] for s in lru_sm]),
                  jnp.stack([s[0][0:CONV_W] for s in lru_sm]).reshape(NA * CONV_W, D),
                  jnp.stack([s[0][5] for s in lru_sm]), jnp.stack([s[0][6] for s in lru_sm]),
                  jnp.stack([d_small[("pool", j)] for j in range(NB)]), d_final_g.reshape(1, D)]
    small_g = jnp.concatenate(small_rows, axis=0)
    n_small = small_g.shape[0]
    assert n_small <= SMALL_ROWS
    small_g = jnp.pad(small_g, ((0, SMALL_ROWS - n_small), (0, 0)))

    outs = _comm_only("scatter_last", small=small_g, reduce_small=True, **comm_args())
    sg_all, sg_sum = outs[:2]
    after_host(outs[2:])
    psum_mine = ppack
    psum_sib = _comm_only("sibling_last", sib=(ppack, psib, tuple(summed)))[0]

    def big_update(w, m, v, off, name):
        shp = w.shape
        g, dl, m2, v2 = _adam_rows(w.reshape(-1, D), m.reshape(-1, D), v.reshape(-1, D), psum_mine, psum_sib, off, name)
        return g.reshape(shp), dl.reshape(shp), m2.reshape(shp), v2.reshape(shp)

    res = {}
    res["ffn_w1"] = big_update(ffn_w1, m_ffn_w1, v_ffn_w1, O_W1, "adam_w1")
    res["ffn_w2"] = big_update(ffn_w2, m_ffn_w2, v_ffn_w2, O_W2, "adam_w2")
    res["lru_w_y"] = big_update(lru_w_y, m_lru_w_y, v_lru_w_y, O_WY, "adam_wy")
    res["lru_w_in"] = big_update(lru_w_in, m_lru_w_in, v_lru_w_in, O_WIN, "adam_win")
    res["lru_w_out"] = big_update(lru_w_out, m_lru_w_out, v_lru_w_out, O_WOUT, "adam_wout")
    res["lru_w_a"] = big_update(lru_w_a, m_lru_w_a, v_lru_w_a, O_WA, "adam_wa")
    res["lru_w_x"] = big_update(lru_w_x, m_lru_w_x, v_lru_w_x, O_WX, "adam_wx")
    res["pool_w"] = big_update(pool_w, m_pool_w, v_pool_w, O_PW, "adam_pw")

    dmod_all = sg_all[:, :L * N_MOD, :].reshape(NDEV, L, N_MOD * D)
    dmod_sh = lax.dynamic_slice_in_dim(dmod_all, q * Cs, Cs, axis=2).transpose(1, 0, 2)
    res["w_mod"] = _wmod_update(c_all.T, dmod_sh, w_mod, m_w_mod, v_w_mod)

    r0 = 0
    by_rows, names_a = [], []
    for name, w, m, v in (("b_mod", b_mod, m_b_mod, v_b_mod), ("norm_mix_g", norm_mix_g, m_norm_mix_g, v_norm_mix_g),
                          ("norm_ffn_g", norm_ffn_g, m_norm_ffn_g, v_norm_ffn_g),
                          ("lru_b_y", lru_b_y, m_lru_b_y, v_lru_b_y), ("lru_b_in", lru_b_in, m_lru_b_in, v_lru_b_in),
                          ("lru_conv_b", lru_conv_b, m_lru_conv_b, v_lru_conv_b),
                          ("lru_lambda", lru_lambda, m_lru_lambda, v_lru_lambda),
                          ("lru_b_out", lru_b_out, m_lru_b_out, v_lru_b_out)):
        by_rows.append((w, m, v, r0))
        names_a.append(name)
        r0 += int(w.size) // D
    g_conv_w = lax.dynamic_slice_in_dim(sg_sum[r0:r0 + NA * CONV_W].reshape(NA, CONV_W, D), q * dq, dq, axis=2)
    r0 += NA * CONV_W
    g_b_a = lax.dynamic_slice_in_dim(sg_sum[r0:r0 + NA].reshape(NA, HEADS, hd), q * (hd // NQ), hd // NQ, axis=2)
    r0 += NA
    g_b_x = lax.dynamic_slice_in_dim(sg_sum[r0:r0 + NA].reshape(NA, HEADS, hd), q * (hd // NQ), hd // NQ, axis=2)
    r0 += NA
    g_ps = lax.dynamic_slice_in_dim(sg_sum[r0:r0 + NB], q * dq, dq, axis=1)
    r0 += NB
    by_rows.append((final_norm_g.reshape(1, D), m_final_norm_g.reshape(1, D), v_final_norm_g.reshape(1, D), r0))
    names_a.append("final_norm_g")
    sliced = [(lru_conv_w, m_lru_conv_w, v_lru_conv_w, g_conv_w), (lru_b_a, m_lru_b_a, v_lru_b_a, g_b_a),
              (lru_b_x, m_lru_b_x, v_lru_b_x, g_b_x), (pool_scale, m_pool_scale, v_pool_scale, g_ps)]
    res_a, res_b = _adam_small(sg_sum, by_rows, sliced)
    for name, r in zip(names_a, res_a):
        res[name] = r
    res["final_norm_g"] = tuple(a.reshape(D) for a in res["final_norm_g"])
    for name, (_, _, _, g), r in zip(("lru_conv_w", "lru_b_a", "lru_b_x", "pool_scale"), sliced, res_b):
        res[name] = (g,) + r

    order = ["w_mod", "b_mod", "norm_mix_g", "norm_ffn_g", "lru_w_y", "lru_b_y", "lru_w_in", "lru_b_in", "lru_conv_w",
             "lru_conv_b", "lru_w_a", "lru_b_a", "lru_w_x", "lru_b_x", "lru_lambda", "lru_w_out", "lru_b_out", "pool_w",
             "pool_scale", "ffn_w1", "ffn_w2", "final_norm_g"]
    return (loss, grad_x, *[res[n][0] for n in order], *[res[n][1] for n in order],
            *[res[n][2] for n in order], *[res[n][3] for n in order])
```

```python
import functools

import jax
import jax.numpy as jnp
from jax import lax
from jax.experimental import pallas as pl
from jax.experimental.pallas import tpu as pltpu

F32 = jnp.float32
BF16 = jnp.bfloat16
MESH = pl.DeviceIdType.MESH

NQ = 4
NDEV = 8
DEPTH = 4
N_MOD = 6
HEADS = 4
CONV_W = 4
POOL_WINDOWS = (2, 4, 8, 16)
LRU_C = 8.0
EPS = 1e-6
ADAM_LR, ADAM_B1, ADAM_B2, ADAM_EPS, ADAM_WD, ADAM_STEP = 0.001, 0.9, 0.999, 1e-08, 0.01, 10

TM = 512
TT = 256
TP = 256
TK = 2048
SMALL_ROWS = 64
FORWARD_STEPS = 4
VMEM_LIMIT = 60 * 1024 * 1024


def _cparams(*sem):
    return pltpu.CompilerParams(dimension_semantics=tuple(sem), vmem_limit_bytes=VMEM_LIMIT)


def _dot(a, b):
    return jnp.dot(a, b, preferred_element_type=F32)


def _dot_nt(a, b):
    return lax.dot_general(a, b, (((1,), (1,)), ((), ())), preferred_element_type=F32)


def _dot_tn(a, b):
    return lax.dot_general(a, b, (((0,), (0,)), ((), ())), preferred_element_type=F32)


def _resident(shape, index_map):
    return pl.BlockSpec(shape, index_map, pipeline_mode=pl.Buffered(1))


def _rms_fwd(x, g, sc, sh):
    r = lax.rsqrt(jnp.mean(x * x, axis=-1, keepdims=True) + EPS)
    xhat = x * r
    n = xhat * g
    return n * (1.0 + sc) + sh, xhat, r, n


def _rms_bwd(dh, xhat, r, n, g, sc):
    dsh = jnp.sum(dh, axis=0, keepdims=True)
    dsc = jnp.sum(dh * n, axis=0, keepdims=True)
    dn = dh * (1.0 + sc)
    dg = jnp.sum(dn * xhat, axis=0, keepdims=True)
    dxh = dn * g
    dx = r * (dxh - xhat * jnp.mean(dxh * xhat, axis=-1, keepdims=True))
    return dx, dsh, dsc, dg


_GELU_K = 0.7978845608028654
_GELU_C = 0.044715


def _gelu(x):
    t = jnp.tanh(_GELU_K * (x + _GELU_C * x * x * x))
    return 0.5 * x * (1.0 + t), t


def _gelu_grad(x, t):
    return 0.5 * (1.0 + t) + 0.5 * x * (1.0 - t * t) * (_GELU_K * (1.0 + 3.0 * _GELU_C * x * x))


def _neg_expm1(y, exp_y):
    series = -(y * (1.0 + y * (0.5 + y * (1.0 / 6.0))))
    return jnp.where(y > -(1.0 / 64.0), series, 1.0 - exp_y)


def _zero_first(ref):
    @pl.when(pl.program_id(0) == 0)
    def _():
        ref[...] = jnp.zeros_like(ref)


def _my_pos():
    return lax.axis_index("x"), lax.axis_index("y"), lax.axis_index("c")


def _dev_index(x, y, c):
    return 4 * x + 2 * y + c


def _chip_peers(x, y):
    return [(1 - x, y), (x, 1 - y), (1 - x, 1 - y)]


def _all_peers(x, y, c):
    return [(px, py, c) for (px, py) in _chip_peers(x, y)] + [(x, y, 1 - c)] + \
           [(px, py, 1 - c) for (px, py) in _chip_peers(x, y)]


def _comm_run(phase, x, y, c, gathers, scatters, sib, send, recv, loc):
    q = 2 * x + y
    peers = _chip_peers(x, y)
    sibling = (x, y, 1 - c)

    def rcopy(src, dst, s, dev):
        return pltpu.make_async_remote_copy(src, dst, send.at[s], recv.at[s], device_id=dev, device_id_type=MESH)

    s = 0
    for gi, (src, dst) in enumerate(gathers):
        half = src.shape[0] // 2
        mine, other = pl.ds(c * half, half), pl.ds((1 - c) * half, half)
        own = pltpu.make_async_copy(src, dst.at[q], loc.at[gi])
        if phase == "start":
            own.start()
        elif phase == "finish":
            own.wait()
        for (px, py) in peers:
            pq = 2 * px + py
            s_ici, s_fwd = s, s + 1
            s += 2
            if phase == "start":
                rcopy(src.at[mine], dst.at[q].at[mine], s_ici, (px, py, c)).start()
            elif phase == "forward":
                rcopy(src.at[mine], dst.at[pq].at[mine], s_ici, (px, py, c)).wait_recv()
                rcopy(dst.at[pq].at[mine], dst.at[pq].at[mine], s_fwd, sibling).start()
            else:
                rcopy(dst.at[pq].at[other], dst.at[pq].at[other], s_fwd, sibling).wait_recv()
                rcopy(src.at[mine], dst.at[q].at[mine], s_ici, (px, py, c)).wait_send()
                rcopy(dst.at[pq].at[mine], dst.at[pq].at[mine], s_fwd, sibling).wait_send()
    direct = []
    for (src, dst) in scatters:
        for k, (px, py) in enumerate(peers):
            direct.append((src.at[2 * px + py], dst.at[k], (px, py, c)))
    if sib is not None:
        src, dst, ranges = sib
        for (off, rows) in ranges:
            direct.append((src.at[pl.ds(off, rows)], dst.at[pl.ds(off, rows)], sibling))
    if phase == "start":
        for k, (a, b, dev) in enumerate(direct):
            rcopy(a, b, s + k, dev).start()
    elif phase == "finish":
        for k, (a, b, dev) in enumerate(direct):
            rcopy(a, b, s + k, dev).wait_recv()
        for k, (a, b, dev) in enumerate(direct):
            rcopy(a, b, s + k, dev).wait_send()


def _comm_shapes(gathers, scatters, sib):
    assert all(g.shape[0] % 32 == 0 for g in gathers)
    cin = list(gathers) + list(scatters) + ([sib[0], sib[1]] if sib else [])
    cout = [jax.ShapeDtypeStruct((NQ,) + g.shape, g.dtype) for g in gathers] + \
           [jax.ShapeDtypeStruct((3,) + s.shape[1:], s.dtype) for s in scatters] + \
           ([jax.ShapeDtypeStruct(sib[1].shape, sib[1].dtype)] if sib else [])
    n_rem = 6 * len(gathers) + 3 * len(scatters) + (len(sib[2]) if sib else 0)
    sems = [pltpu.SemaphoreType.DMA((max(n_rem, 1),)), pltpu.SemaphoreType.DMA((max(n_rem, 1),)),
            pltpu.SemaphoreType.DMA((max(len(gathers), 1),))]
    return cin, cout, sems


def _pcall(body, *, name, grid, in_specs, out_specs, out_shape, operands, scratch_shapes=(),
           gathers=(), scatters=(), sib=None):
    assert len(grid) == 1
    out_shape, out_specs = tuple(out_shape), tuple(out_specs)
    if not (gathers or scatters or sib):
        return pl.pallas_call(body, name=name, grid=grid, in_specs=list(in_specs), out_specs=out_specs,
                              out_shape=out_shape, scratch_shapes=list(scratch_shapes),
                              compiler_params=_cparams("arbitrary"))(*operands)
    cin, cout, sems = _comm_shapes(gathers, scatters, sib)
    n_in, n_cin, n_out, n_cout, n_scr = len(operands), len(cin), len(out_shape), len(cout), len(scratch_shapes)
    ng, ns = len(gathers), len(scatters)
    nsteps = grid[0]

    def wrapped(*refs):
        ins = refs[:n_in]
        cins = refs[n_in:n_in + n_cin]
        o0 = n_in + n_cin
        outs = refs[o0:o0 + n_out]
        couts = refs[o0 + n_out:o0 + n_out + n_cout]
        s0 = o0 + n_out + n_cout
        scr = refs[s0:s0 + n_scr]
        send, recv, loc = refs[s0 + n_scr:s0 + n_scr + 3]
        x, y, c = _my_pos()

        def run(phase):
            g = [(cins[k], couts[k]) for k in range(ng)]
            sc = [(cins[ng + k], couts[ng + k]) for k in range(ns)]
            sb = (cins[ng + ns], couts[ng + ns], sib[2]) if sib else None
            _comm_run(phase, x, y, c, g, sc, sb, send, recv, loc)

        @pl.when(pl.program_id(0) == 0)
        def _():
            run("start")

        if ng:
            @pl.when(pl.program_id(0) == max(nsteps - FORWARD_STEPS, 0))
            def _():
                run("forward")

        body(*ins, *outs, *scr)

        @pl.when(pl.program_id(0) == nsteps - 1)
        def _():
            run("finish")

    anyspec = pl.BlockSpec(memory_space=pl.ANY)
    aliases = {n_in + ng + ns + 1: n_out + ng + ns} if sib else {}
    return pl.pallas_call(
        wrapped, name=name, grid=grid,
        in_specs=list(in_specs) + [anyspec] * n_cin, out_specs=out_specs + (anyspec,) * n_cout,
        out_shape=out_shape + tuple(cout), scratch_shapes=list(scratch_shapes) + sems,
        input_output_aliases=aliases,
        compiler_params=pltpu.CompilerParams(dimension_semantics=("arbitrary",), vmem_limit_bytes=VMEM_LIMIT,
                                             has_side_effects=True),
    )(*operands, *cin)


def _comm_only(name, small=None, reduce_small=False, gathers=(), scatters=(), sib=None):
    cin, cout, sems = _comm_shapes(gathers, scatters, sib)
    n_cin, n_cout = len(cin), len(cout)
    ng, ns = len(gathers), len(scatters)
    n_sm_in = 1 if small is not None else 0
    n_sm_out = (2 if reduce_small else 1) if small is not None else 0

    def body(*refs):
        sm_in = refs[:n_sm_in]
        cins = refs[n_sm_in:n_sm_in + n_cin]
        o0 = n_sm_in + n_cin
        sm_out = refs[o0:o0 + n_sm_out]
        couts = refs[o0 + n_sm_out:o0 + n_sm_out + n_cout]
        s0 = o0 + n_sm_out + n_cout
        send, recv, loc = refs[s0:s0 + 3]
        x, y, c = _my_pos()
        g = [(cins[k], couts[k]) for k in range(ng)]
        sc = [(cins[ng + k], couts[ng + k]) for k in range(ns)]
        sb = (cins[ng + ns], couts[ng + ns], sib[2]) if sib else None
        _comm_run("start", x, y, c, g, sc, sb, send, recv, loc)
        if small is not None:
            sm_send, sm_recv = refs[s0 + 3:s0 + 5]
            small_ref, sg_ref = sm_in[0], sm_out[0]
            me = _dev_index(x, y, c)
            sg_ref[me] = small_ref[...]
            peers = _all_peers(x, y, c)
            sm = [pltpu.make_async_remote_copy(small_ref, sg_ref.at[me], sm_send.at[k], sm_recv.at[k],
                                               device_id=peer, device_id_type=MESH) for k, peer in enumerate(peers)]
            for cp in sm:
                cp.start()
            for k, (px, py, pc) in enumerate(peers):
                pltpu.make_async_remote_copy(small_ref, sg_ref.at[_dev_index(px, py, pc)], sm_send.at[k], sm_recv.at[k],
                                             device_id=(px, py, pc), device_id_type=MESH).wait_recv()
            if reduce_small:
                acc = sg_ref[0]
                for d in range(1, NDEV):
                    acc = acc + sg_ref[d]
                sm_out[1][...] = acc
            for cp in sm:
                cp.wait_send()
        if ng:
            _comm_run("forward", x, y, c, g, sc, sb, send, recv, loc)
        _comm_run("finish", x, y, c, g, sc, sb, send, recv, loc)

    anyspec = pl.BlockSpec(memory_space=pl.ANY)
    vspec = pl.BlockSpec(memory_space=pltpu.VMEM)
    sm_shapes = []
    if small is not None:
        sm_shapes.append(jax.ShapeDtypeStruct((NDEV,) + small.shape, small.dtype))
        if reduce_small:
            sm_shapes.append(jax.ShapeDtypeStruct(small.shape, small.dtype))
        sems = sems + [pltpu.SemaphoreType.DMA((NDEV - 1,)), pltpu.SemaphoreType.DMA((NDEV - 1,))]
    aliases = {n_sm_in + ng + ns + 1: n_sm_out + ng + ns} if sib else {}
    return pl.pallas_call(
        body, name=name,
        in_specs=[vspec] * n_sm_in + [anyspec] * n_cin,
        out_specs=tuple([vspec] * n_sm_out + [anyspec] * n_cout),
        out_shape=tuple(sm_shapes + cout), scratch_shapes=sems, input_output_aliases=aliases,
        compiler_params=pltpu.CompilerParams(has_side_effects=True),
    )(*([small] if small is not None else []), *cin)


def _exchange_mod(modpart):
    _, L, Cs = modpart.shape

    def body(part_ref, out_ref, send, recv):
        x, y, c = _my_pos()
        q = 2 * x + y
        me = _dev_index(x, y, c)
        out_ref[q] = part_ref[me]
        sends = []
        for k, (px, py) in enumerate(_chip_peers(x, y)):
            cp = pltpu.make_async_remote_copy(part_ref.at[_dev_index(px, py, c)], out_ref.at[q], send.at[k], recv.at[k],
                                              device_id=(px, py, c), device_id_type=MESH)
            cp.start()
            sends.append(cp)
        for k, (px, py) in enumerate(_chip_peers(x, y)):
            pltpu.make_async_remote_copy(part_ref.at[me], out_ref.at[2 * px + py], send.at[k], recv.at[k],
                                         device_id=(px, py, c), device_id_type=MESH).wait_recv()
        for cp in sends:
            cp.wait_send()

    return pl.pallas_call(
        body, name="exchange_mod",
        out_shape=jax.ShapeDtypeStruct((NQ, L, Cs), modpart.dtype),
        in_specs=[pl.BlockSpec(memory_space=pltpu.VMEM)],
        out_specs=pl.BlockSpec(memory_space=pltpu.VMEM),
        scratch_shapes=[pltpu.SemaphoreType.DMA((3,)), pltpu.SemaphoreType.DMA((3,))],
        compiler_params=pltpu.CompilerParams(has_side_effects=True),
    )(modpart)


def _mod_part(c_all, w_mod, b_mod_sh):
    L, D, Cs = w_mod.shape
    tn = 512 if Cs % 512 == 0 else Cs

    def body(c_ref, w_ref, b_ref, o_ref):
        cv = c_ref[...]
        cond = cv * jax.nn.sigmoid(cv)
        o_ref[...] = jnp.dot(cond, w_ref[...], preferred_element_type=F32, precision=lax.Precision.HIGHEST) + b_ref[...]

    return pl.pallas_call(
        body, name="mod_part", grid=(L, Cs // tn),
        out_shape=jax.ShapeDtypeStruct((L, NDEV, Cs), F32),
        in_specs=[pl.BlockSpec((NDEV, D), lambda i, j: (0, 0)),
                  pl.BlockSpec((None, D, tn), lambda i, j: (i, 0, j)),
                  pl.BlockSpec((None, 1, tn), lambda i, j: (i, 0, j))],
        out_specs=pl.BlockSpec((None, NDEV, tn), lambda i, j: (i, 0, j)),
        compiler_params=_cparams("parallel", "parallel"),
    )(c_all, w_mod, b_mod_sh)


def _adam(w, g, m, v):
    m2 = ADAM_B1 * m + (1.0 - ADAM_B1) * g
    v2 = ADAM_B2 * v + (1.0 - ADAM_B2) * (g * g)
    m_hat = m2 / (1.0 - ADAM_B1 ** ADAM_STEP)
    v_hat = v2 / (1.0 - ADAM_B2 ** ADAM_STEP)
    delta = -ADAM_LR * (m_hat / (jnp.sqrt(v_hat) + ADAM_EPS) + ADAM_WD * w)
    return delta, m2, v2


def _wmod_update(c_all_t, dmod_sh, w, m, v):
    L, D, Cs = w.shape
    td = 256 if D % 256 == 0 else D

    def body(ct_ref, d_ref, w_ref, m_ref, v_ref, g_ref, dl_ref, m2_ref, v2_ref):
        cv = ct_ref[...]
        cond = cv * jax.nn.sigmoid(cv)
        g = cond[:, 0:1] * d_ref[0:1, :]
        for b in range(1, NDEV):
            g = g + cond[:, b:b + 1] * d_ref[b:b + 1, :]
        g_ref[...] = g
        dl_ref[...], m2_ref[...], v2_ref[...] = _adam(w_ref[...], g, m_ref[...], v_ref[...])

    blk = pl.BlockSpec((None, td, Cs), lambda i, j: (i, j, 0))
    out = jax.ShapeDtypeStruct((L, D, Cs), F32)
    return pl.pallas_call(
        body, name="wmod_update", grid=(L, D // td),
        out_shape=(out, out, out, out),
        in_specs=[pl.BlockSpec((td, NDEV), lambda i, j: (j, 0)),
                  pl.BlockSpec((None, NDEV, Cs), lambda i, j: (i, 0, 0)), blk, blk, blk],
        out_specs=(blk, blk, blk, blk),
        compiler_params=_cparams("parallel", "parallel"),
    )(c_all_t, dmod_sh, w, m, v)


def _adam_rows(w, m, v, pa, pb, row_off, name):
    rows, C = w.shape
    tr = 512 if rows % 512 == 0 else (128 if rows % 128 == 0 else rows)
    assert row_off % tr == 0
    ob = row_off // tr

    def body(w_ref, m_ref, v_ref, pa_ref, pb_ref, g_ref, dl_ref, m2_ref, v2_ref):
        g = pa_ref[...] + pb_ref[...]
        g_ref[...] = g
        dl_ref[...], m2_ref[...], v2_ref[...] = _adam(w_ref[...], g, m_ref[...], v_ref[...])

    blk = pl.BlockSpec((tr, C), lambda i: (i, 0))
    pblk = pl.BlockSpec((tr, C), lambda i: (ob + i, 0))
    out = jax.ShapeDtypeStruct((rows, C), F32)
    return pl.pallas_call(
        body, name=name, grid=(rows // tr,), out_shape=(out, out, out, out),
        in_specs=[blk, blk, blk, pblk, pblk], out_specs=(blk, blk, blk, blk),
        compiler_params=_cparams("parallel"),
    )(w, m, v, pa, pb)


def _adam_small(sg_sum, by_rows, sliced, pairs):
    D = sg_sum.shape[1]
    na, nb, nc = len(by_rows), len(sliced), len(pairs)

    def body(*refs):
        sg = refs[0]
        ins_a = [refs[1 + 3 * t:4 + 3 * t] for t in range(na)]
        p = 1 + 3 * na
        ins_b = [refs[p + 4 * t:p + 4 * t + 4] for t in range(nb)]
        p += 4 * nb
        ins_c = [refs[p + 5 * t:p + 5 * t + 5] for t in range(nc)]
        p += 5 * nc
        outs_a = [refs[p + 4 * t:p + 4 * t + 4] for t in range(na)]
        p += 4 * na
        outs_b = [refs[p + 3 * t:p + 3 * t + 3] for t in range(nb)]
        p += 3 * nb
        outs_c = [refs[p + 4 * t:p + 4 * t + 4] for t in range(nc)]
        for (w_ref, m_ref, v_ref, ga_ref, gb_ref), (g_ref, dl_ref, m2_ref, v2_ref) in zip(ins_c, outs_c):
            g = ga_ref[...] + gb_ref[...]
            g_ref[...] = g
            dl_ref[...], m2_ref[...], v2_ref[...] = _adam(w_ref[...], g, m_ref[...], v_ref[...])
        for (w_ref, m_ref, v_ref), (g_ref, dl_ref, m2_ref, v2_ref), (w, _, _, row0) in zip(ins_a, outs_a, by_rows):
            n, k = w.shape[0], w.shape[1] // D
            pieces = [(slice(0, n), slice(0, D), slice(row0, row0 + n))] if k == 1 else \
                     [(slice(i, i + 1), slice(kk * D, (kk + 1) * D), slice(row0 + i * k + kk, row0 + i * k + kk + 1))
                      for i in range(n) for kk in range(k)]
            for rs, cs, gs in pieces:
                g = sg[gs, :]
                g_ref[rs, cs] = g
                dl_ref[rs, cs], m2_ref[rs, cs], v2_ref[rs, cs] = _adam(w_ref[rs, cs], g, m_ref[rs, cs], v_ref[rs, cs])
        for (w_ref, m_ref, v_ref, g_ref), (dl_ref, m2_ref, v2_ref) in zip(ins_b, outs_b):
            dl_ref[...], m2_ref[...], v2_ref[...] = _adam(w_ref[...], g_ref[...], m_ref[...], v_ref[...])

    operands = [sg_sum] + [a for t in by_rows for a in t[:3]] + [a for t in sliced for a in t] + \
               [a for t in pairs for a in t]
    out_shape = [jax.ShapeDtypeStruct(t[0].shape, F32) for t in by_rows for _ in range(4)] + \
                [jax.ShapeDtypeStruct(t[0].shape, F32) for t in sliced for _ in range(3)] + \
                [jax.ShapeDtypeStruct(t[0].shape, F32) for t in pairs for _ in range(4)]
    outs = pl.pallas_call(body, name="adam_small", out_shape=tuple(out_shape))(*operands)
    res_a = [tuple(outs[4 * t:4 * t + 4]) for t in range(na)]
    o = 4 * na
    res_b = [tuple(outs[o + 3 * t:o + 3 * t + 3]) for t in range(nb)]
    o += 3 * nb
    res_c = [tuple(outs[o + 4 * t:o + 4 * t + 4]) for t in range(nc)]
    return res_a, res_b, res_c


def _sum_into(ppack, dw, rb, off, qv):
    _, rows, D = dw.shape
    tr = 256 if rows % 256 == 0 else 128
    assert rows % tr == 0 and off % tr == 0
    ob = off // tr

    def body(q_ref, o_ref, r_ref, pin_ref, p_ref):
        acc = o_ref[...].astype(F32)
        for k in range(3):
            acc = acc + r_ref[k].astype(F32)
        p_ref[...] = acc

    return pl.pallas_call(
        body, name="sum_partials", out_shape=jax.ShapeDtypeStruct(ppack.shape, ppack.dtype),
        grid_spec=pltpu.PrefetchScalarGridSpec(
            num_scalar_prefetch=1, grid=(rows // tr,),
            in_specs=[pl.BlockSpec((None, tr, D), lambda i, q_ref: (q_ref[0], i, 0)),
                      pl.BlockSpec((3, tr, D), lambda i, q_ref: (0, i, 0)),
                      pl.BlockSpec(memory_space=pl.ANY)],
            out_specs=pl.BlockSpec((tr, D), lambda i, q_ref: (ob + i, 0))),
        input_output_aliases={3: 0},
        compiler_params=_cparams("parallel"),
    )(qv, dw, rb, ppack)


def _wspec(g):
    return _resident(g.shape, lambda i: (0, 0, 0))


def _ffn_fwd_inner(x1, mod_ref, gf_ref, w1_ref, w2_ref, h2_ref, a_ref, z_ref, x2_ref):
    h2 = _rms_fwd(x1, gf_ref[...], mod_ref[4:5, :], mod_ref[3:4, :])[0]
    h2b = h2.astype(BF16)
    h2_ref[...] = h2b
    f4 = w1_ref.shape[2]
    z = jnp.zeros(x1.shape, F32)
    for q in range(NQ):
        a = jnp.maximum(_dot(h2b, w1_ref[q]), 0.0)
        a_ref[:, q * f4:(q + 1) * f4] = a.astype(BF16)
        z = z + _dot((a * a).astype(BF16), w2_ref[q])
    z_ref[...] = z.astype(BF16)
    x2_ref[...] = x1 + mod_ref[5:6, :] * z


def _sigmoid(x):
    return 0.5 + 0.5 * jnp.tanh(0.5 * x)


def _heads_dot(xb, w_ref, hd, nt=False):
    outs = []
    for h in range(HEADS):
        xs = xb[:, h * hd:(h + 1) * hd]
        outs.append(_dot_nt(xs, w_ref[h]) if nt else _dot(xs, w_ref[h]))
    return jnp.concatenate(outs, axis=1)


def _lru_gates(xc, wa_ref, ba, wx_ref, bx, lam, hd):
    xcb = xc.astype(BF16)
    gate_r = _sigmoid(_heads_dot(xcb, wa_ref, hd) + ba)
    gate_i = _sigmoid(_heads_dot(xcb, wx_ref, hd) + bx)
    ls = jax.nn.log_sigmoid(lam)
    log_a = gate_r * (LRU_C * ls)
    a = jnp.exp(log_a)
    mult = jnp.sqrt(_neg_expm1(2.0 * log_a, a * a))
    return xcb, gate_r, gate_i, ls, a, mult


def _conv_taps(xext, cw, tt):
    acc = cw[0:1, :] * xext[pl.ds(8 - (CONV_W - 1), tt), :]
    for k in range(1, CONV_W):
        acc = acc + cw[k:k + 1, :] * xext[pl.ds(8 - (CONV_W - 1) + k, tt), :]
    return acc


def _lru_fwd(x, mod_l, g_mix, g_wy, g_win, b_y, b_in, cw, cb, wa, ba, wx, bx, lam, g_wout, b_out, **comm):
    S, W = x.shape
    tt = min(TT, S)
    hd = W // HEADS

    def body(x_ref, mod_ref, g_ref, wy_ref, win_ref, by_ref, bin_ref, cw_ref, cb_ref, wa_ref, ba_ref, wx_ref, bx_ref,
             lam_ref, wo_ref, bo_ref, h_ref, gb_ref, xr_ref, hs_ref, p_ref, y_ref, x1_ref, xext, a_s, u_s, carry):
        i = pl.program_id(0)

        @pl.when(i == 0)
        def _():
            carry[...] = jnp.zeros_like(carry)
            xext[0:8, :] = jnp.zeros((8, W), F32)

        @pl.when(i > 0)
        def _():
            xext[0:8, :] = xext[pl.ds(tt, 8), :]

        xv = x_ref[...]
        hb = _rms_fwd(xv, g_ref[...], mod_ref[1:2, :], mod_ref[0:1, :])[0].astype(BF16)
        h_ref[...] = hb
        gbv = _dot(hb, wy_ref[...].reshape(W, W)) + by_ref[...]
        gb_ref[...] = gbv
        xr = _dot(hb, win_ref[...].reshape(W, W)) + bin_ref[...]
        xr_ref[...] = xr
        xext[pl.ds(8, tt), :] = xr
        xc = _conv_taps(xext, cw_ref[...], tt) + cb_ref[...]
        _, _, gate_i, _, a, mult = _lru_gates(xc, wa_ref, ba_ref[...], wx_ref, bx_ref[...], lam_ref[...], hd)
        a_s[...] = a
        u_s[...] = mult * (gate_i * xc)
        row = lax.broadcasted_iota(jnp.int32, (8, W), 0)

        def step(k, _):
            off = pl.multiple_of(k * 8, 8)
            A = a_s[pl.ds(off, 8), :]
            U = u_s[pl.ds(off, 8), :]
            for d in (1, 2, 4):
                keep = row >= d
                Us = jnp.where(keep, pltpu.roll(U, d, 0), 0.0)
                As = jnp.where(keep, pltpu.roll(A, d, 0), 1.0)
                U = U + A * Us
                A = A * As
            H = U + A * carry[...]
            hs_ref[pl.ds(off, 8), :] = H
            carry[...] = jnp.broadcast_to(H[7:8, :], (8, W))
            return 0

        lax.fori_loop(0, tt // 8, step, 0)
        pb = (hs_ref[...] * _gelu(gbv)[0]).astype(BF16)
        p_ref[...] = pb
        y = _dot(pb, wo_ref[...].reshape(W, W)) + bo_ref[...]
        y_ref[...] = y.astype(BF16)
        x1_ref[...] = xv + mod_ref[2:3, :] * y

    tile = pl.BlockSpec((tt, W), lambda i: (i, 0))
    row = pl.BlockSpec((1, W), lambda i: (0, 0))
    wblk = pl.BlockSpec((HEADS, hd, hd), lambda i: (0, 0, 0))
    f32o, bf16o = jax.ShapeDtypeStruct((S, W), F32), jax.ShapeDtypeStruct((S, W), BF16)
    return _pcall(
        body, name="lru_fwd", grid=(S // tt,),
        out_shape=(bf16o, f32o, f32o, f32o, bf16o, bf16o, f32o),
        in_specs=[tile, pl.BlockSpec((8, W), lambda i: (0, 0)), row, _wspec(g_wy), _wspec(g_win), row, row,
                  pl.BlockSpec((CONV_W, W), lambda i: (0, 0)), row, wblk, row, wblk, row, row, _wspec(g_wout), row],
        out_specs=(tile,) * 7,
        scratch_shapes=[pltpu.VMEM((tt + 8, W), F32), pltpu.VMEM((tt, W), F32), pltpu.VMEM((tt, W), F32),
                        pltpu.VMEM((8, W), F32)],
        operands=(x, mod_l, g_mix, g_wy, g_win, b_y, b_in, cw, cb, wa, ba, wx, bx, lam, g_wout, b_out), **comm)


def _ffn_out_shapes(S, D, F):
    return (jax.ShapeDtypeStruct((S, D), BF16), jax.ShapeDtypeStruct((S, F), BF16),
            jax.ShapeDtypeStruct((S, D), BF16), jax.ShapeDtypeStruct((S, D), F32))


def _ffn_fwd(x1, mod_l, g_ffn, g_w1, g_w2, **comm):
    S, D = x1.shape
    tm = min(TM, S)
    F = g_w1.shape[2] * NQ

    def body(x1_ref, mod_ref, gf_ref, w1_ref, w2_ref, h2_ref, a_ref, z_ref, x2_ref):
        _ffn_fwd_inner(x1_ref[...], mod_ref, gf_ref, w1_ref, w2_ref, h2_ref, a_ref, z_ref, x2_ref)

    tile = pl.BlockSpec((tm, D), lambda i: (i, 0))
    row = pl.BlockSpec((1, D), lambda i: (0, 0))
    return _pcall(
        body, name="ffn_fwd", grid=(S // tm,),
        out_shape=_ffn_out_shapes(S, D, F),
        in_specs=[tile, pl.BlockSpec((8, D), lambda i: (0, 0)), row, _wspec(g_w1), _wspec(g_w2)],
        out_specs=(tile, pl.BlockSpec((tm, F), lambda i: (i, 0)), tile, tile),
        operands=(x1, mod_l, g_ffn, g_w1, g_w2), **comm)


def _window_vec(D):
    gd = D // len(POOL_WINDOWS)
    lane = lax.broadcasted_iota(jnp.int32, (1, D), 1)
    w = jnp.full((1, D), float(POOL_WINDOWS[0]), F32)
    for g in range(1, len(POOL_WINDOWS)):
        w = jnp.where(lane >= g * gd, float(POOL_WINDOWS[g]), w)
    return w


def _pool_mix_ffn_fwd(x, mod_l, g_mix, pw, ps, g_ffn, g_w1, g_w2, **comm):
    S, D = x.shape
    tm = min(TP, S)
    F = g_w1.shape[2] * NQ
    gd = D // len(POOL_WINDOWS)
    n = tm + 24

    def body(x_ref, xh_ref, mod_ref, gm_ref, pw_ref, ps_ref, gf_ref, w1_ref, w2_ref,
             pl_ref, x1_ref, h2_ref, a_ref, z_ref, x2_ref, ext, b1, b2):
        i = pl.program_id(0)
        g, sc, sh = gm_ref[...], mod_ref[1:2, :], mod_ref[0:1, :]
        xv = x_ref[...]
        h = _rms_fwd(xv, g, sc, sh)[0]
        hh = _rms_fwd(xh_ref[...], g, sc, sh)[0]
        zeros8 = jnp.zeros((8, D), F32)
        ext[0:8, :] = zeros8
        b1[0:8, :] = zeros8
        b2[0:8, :] = zeros8
        ext[8:24, :] = jnp.where(i > 0, hh, 0.0)
        ext[pl.ds(24, tm), :] = h
        m = n - 8
        b1[pl.ds(8, m), :] = ext[pl.ds(8, m), :] + ext[pl.ds(7, m), :]
        b2[pl.ds(8, m), gd:] = b1[pl.ds(8, m), gd:] + b1[pl.ds(6, m), gd:]
        b1[pl.ds(8, m), 2 * gd:] = b2[pl.ds(8, m), 2 * gd:] + b2[pl.ds(4, m), 2 * gd:]
        b2[pl.ds(8, m), 3 * gd:] = b1[pl.ds(8, m), 3 * gd:] + b1[pl.ds(0, m), 3 * gd:]
        wsum = jnp.concatenate([b1[pl.ds(24, tm), 0:gd], b2[pl.ds(24, tm), gd:2 * gd],
                                b1[pl.ds(24, tm), 2 * gd:3 * gd], b2[pl.ds(24, tm), 3 * gd:]], axis=1)
        t1 = (lax.broadcasted_iota(jnp.int32, (tm, 1), 0) + (i * tm + 1)).astype(F32)
        cnt = jnp.minimum(t1, _window_vec(D))
        pooled = (wsum / cnt - h).astype(BF16)
        pl_ref[...] = pooled
        y = _heads_dot(pooled, pw_ref, gd) * ps_ref[...]
        x1 = xv + mod_ref[2:3, :] * y
        x1_ref[...] = x1
        _ffn_fwd_inner(x1, mod_ref, gf_ref, w1_ref, w2_ref, h2_ref, a_ref, z_ref, x2_ref)

    tile = pl.BlockSpec((tm, D), lambda i: (i, 0))
    halo = pl.BlockSpec((16, D), lambda i: (jnp.maximum(i * (tm // 16) - 1, 0), 0))
    row = pl.BlockSpec((1, D), lambda i: (0, 0))
    return _pcall(
        body, name="pool_mix_ffn_fwd", grid=(S // tm,),
        out_shape=(jax.ShapeDtypeStruct((S, D), BF16), jax.ShapeDtypeStruct((S, D), F32)) + _ffn_out_shapes(S, D, F),
        in_specs=[tile, halo, pl.BlockSpec((8, D), lambda i: (0, 0)), row,
                  pl.BlockSpec((len(POOL_WINDOWS), gd, gd), lambda i: (0, 0, 0)), row, row,
                  _wspec(g_w1), _wspec(g_w2)],
        out_specs=(tile, tile, tile, pl.BlockSpec((tm, F), lambda i: (i, 0)), tile, tile),
        scratch_shapes=[pltpu.VMEM((n, D), F32), pltpu.VMEM((n, D), F32), pltpu.VMEM((n, D), F32)],
        operands=(x, x, mod_l, g_mix, pw, ps, g_ffn, g_w1, g_w2), **comm)


def _loss_head(xv, gv, tv, acc_ref):
    D = xv.shape[1]
    r = lax.rsqrt(jnp.mean(xv * xv, axis=-1, keepdims=True) + EPS)
    xhat = xv * r
    err = xhat * gv - tv
    acc_ref[0:1, :] += jnp.sum(err * err, axis=0, keepdims=True)
    dy = err * (1.0 / D)
    acc_ref[1:2, :] += jnp.sum(dy * xhat, axis=0, keepdims=True)
    dxh = dy * gv
    return r * (dxh - xhat * jnp.mean(dxh * xhat, axis=-1, keepdims=True))


def _ffn_bwd(dx2, x1, a, z, mod_l, g_ffn, g_w1, g_w2, head=None, **comm):
    S, D = dx2.shape
    F = a.shape[1]
    f4 = F // NQ
    tm = min(TM, S)
    nh = 2 if head else 0

    def body(*refs):
        dx2_ref, x1_ref, a_ref, z_ref, mod_ref, gf_ref, w1_ref, w2_ref = refs[:8]
        dx1_ref, du_ref, dz_ref, acc_ref = refs[8 + nh:]
        _zero_first(acc_ref)
        dx2v = dx2_ref[...]
        if head:
            dx2v = _loss_head(dx2v, refs[8][...], refs[9][...], acc_ref)
        acc_ref[5:6, :] +=jnp.sum(dx2v * z_ref[...].astype(F32), axis=0, keepdims=True)
        dzb = (dx2v * mod_ref[5:6, :]).astype(BF16)
        dz_ref[...] = dzb
        dh2 = jnp.zeros((tm, D), F32)
        for q in range(NQ):
            av = a_ref[:, q * f4:(q + 1) * f4].astype(F32)
            du = (_dot_nt(dzb, w2_ref[q]) * (2.0 * av)).astype(BF16)
            du_ref[:, q * f4:(q + 1) * f4] = du
            dh2 = dh2 + _dot_nt(du, w1_ref[q])
        g, sc = gf_ref[...], mod_ref[4:5, :]
        _, xhat, r, n = _rms_fwd(x1_ref[...], g, sc, mod_ref[3:4, :])
        dx, dsh, dsc, dg = _rms_bwd(dh2, xhat, r, n, g, sc)
        acc_ref[3:4, :] += dsh
        acc_ref[4:5, :] += dsc
        acc_ref[7:8, :] += dg
        dx1_ref[...] = dx2v + dx

    tile = pl.BlockSpec((tm, D), lambda i: (i, 0))
    wide = pl.BlockSpec((tm, F), lambda i: (i, 0))
    return _pcall(
        body, name="ffn_bwd", grid=(S // tm,),
        out_shape=(jax.ShapeDtypeStruct((S, D), F32), jax.ShapeDtypeStruct((S, F), BF16),
                   jax.ShapeDtypeStruct((S, D), BF16), jax.ShapeDtypeStruct((8, D), F32)),
        in_specs=[tile, tile, wide, tile, pl.BlockSpec((8, D), lambda i: (0, 0)), pl.BlockSpec((1, D), lambda i: (0, 0)),
                  _wspec(g_w1), _wspec(g_w2)] + ([pl.BlockSpec((1, D), lambda i: (0, 0)), tile] if head else []),
        out_specs=(tile, wide, tile, pl.BlockSpec((8, D), lambda i: (0, 0))),
        operands=(dx2, x1, a, z, mod_l, g_ffn, g_w1, g_w2) + (tuple(head) if head else ()), **comm)


def _dw_blocked(a, b, by_rows, square_a, name):
    S = a.shape[0]
    tk = min(TK, S)
    nk = S // tk
    if by_rows:
        bm, bn = a.shape[1] // NQ, b.shape[1]
        a_map, b_map = (lambda q, k: (k, q)), (lambda q, k: (k, 0))
    else:
        bm, bn = a.shape[1], b.shape[1] // NQ
        a_map, b_map = (lambda q, k: (k, 0)), (lambda q, k: (k, q))

    def body(a_ref, b_ref, o_ref, acc):
        k = pl.program_id(1)

        @pl.when(k == 0)
        def _():
            acc[...] = jnp.zeros_like(acc)

        av = a_ref[...]
        if square_a:
            av = av * av
        acc[...] += _dot_tn(av, b_ref[...])

        @pl.when(k == nk - 1)
        def _():
            o_ref[...] = acc[...].astype(o_ref.dtype)

    return pl.pallas_call(
        body, name=name, grid=(NQ, nk),
        out_shape=jax.ShapeDtypeStruct((NQ, bm, bn), BF16),
        in_specs=[pl.BlockSpec((tk, bm), a_map), pl.BlockSpec((tk, bn), b_map)],
        out_specs=pl.BlockSpec((None, bm, bn), lambda q, k: (q, 0, 0)),
        scratch_shapes=[pltpu.VMEM((bm, bn), F32)],
        compiler_params=_cparams("parallel", "arbitrary"),
    )(a, b)


def _dw_whole(a, bs, name, **comm):
    S, M = a.shape
    N = bs[0].shape[1]
    tk = min(TK, S)
    nk = S // tk
    nb = len(bs)

    def body(*refs):
        a_ref, b_refs, o_refs, accs = refs[0], refs[1:1 + nb], refs[1 + nb:1 + 2 * nb], refs[1 + 2 * nb:]
        k = pl.program_id(0)

        @pl.when(k == 0)
        def _():
            for acc in accs:
                acc[...] = jnp.zeros_like(acc)

        av = a_ref[...]
        for b_ref, acc in zip(b_refs, accs):
            acc[...] += _dot_tn(av, b_ref[...])

        @pl.when(k == nk - 1)
        def _():
            for o_ref, acc in zip(o_refs, accs):
                o_ref[...] = acc[...].reshape(NQ, M // NQ, N).astype(o_ref.dtype)

    return _pcall(
        body, name=name, grid=(nk,),
        out_shape=tuple(jax.ShapeDtypeStruct((NQ, M // NQ, N), BF16) for _ in bs),
        in_specs=[pl.BlockSpec((tk, M), lambda k: (k, 0))] + [pl.BlockSpec((tk, N), lambda k: (k, 0)) for _ in bs],
        out_specs=tuple(pl.BlockSpec((NQ, M // NQ, N), lambda k: (0, 0, 0)) for _ in bs),
        scratch_shapes=[pltpu.VMEM((M, N), F32) for _ in bs],
        operands=(a, *bs), **comm)


def _lru_bwd(dx1, y, x, xr0, gb, hs, mod_l, g_mix, g_wout, g_wy, g_win, cw, cb, wa, ba, wx, bx, lam, **comm):
    S, W = xr0.shape
    tt = min(TT, S)
    nb = S // tt
    hd = W // HEADS

    def body(dx1_ref, y_ref, x_ref, xr_ref, xrh_ref, gb_ref, hs_ref, hsh_ref, mod_ref, gm_ref, wo_ref, wy_ref, win_ref,
             cw_ref, cb_ref, wa_ref, ba_ref, wx_ref, bx_ref, lam_ref,
             dy_ref, dgb_ref, dxr_ref, dx_ref, sm_ref, dwa_ref, dwx_ref, acc_ref,
             xext, hext, qext, dext, a_s, b_s, qc, dc):
        i = pl.program_id(0)
        blk = nb - 1 - i

        @pl.when(i == 0)
        def _():
            sm_ref[...] = jnp.zeros_like(sm_ref)
            dwa_ref[...] = jnp.zeros_like(dwa_ref)
            dwx_ref[...] = jnp.zeros_like(dwx_ref)
            acc_ref[...] = jnp.zeros_like(acc_ref)
            qc[...] = jnp.zeros_like(qc)
            dc[...] = jnp.zeros_like(dc)

        dx1v = dx1_ref[...]
        acc_ref[2:3, :] += jnp.sum(dx1v * y_ref[...].astype(F32), axis=0, keepdims=True)
        dy = dx1v * mod_ref[2:3, :]
        acc_ref[3:4, :] += jnp.sum(dy, axis=0, keepdims=True)
        dyb = dy.astype(BF16)
        dy_ref[...] = dyb
        dpv = _dot_nt(dyb, wo_ref[...].reshape(W, W))

        xext[0:8, :] = jnp.where(blk > 0, xrh_ref[...], 0.0)
        xext[pl.ds(8, tt), :] = xr_ref[...]
        hext[0:8, :] = jnp.where(blk > 0, hsh_ref[...], 0.0)
        hext[pl.ds(8, tt), :] = hs_ref[...]
        cw = cw_ref[...]
        lam = lam_ref[...]
        xc = _conv_taps(xext, cw, tt) + cb_ref[...]
        xcb, gate_r, gate_i, ls, a, mult = _lru_gates(xc, wa_ref, ba_ref[...], wx_ref, bx_ref[...], lam, hd)

        gbv = gb_ref[...]
        gate, th = _gelu(gbv)
        dgb = dpv * hs_ref[...] * _gelu_grad(gbv, th)
        dgbb = dgb.astype(BF16)
        dgb_ref[...] = dgbb
        sm_ref[9:10, :] += jnp.sum(dgb, axis=0, keepdims=True)
        dhs = dpv * gate

        a_s[...] = a
        b_s[...] = a * dhs
        qext[pl.ds(tt, 8), :] = qc[...]
        row = lax.broadcasted_iota(jnp.int32, (8, W), 0)

        def step(k, _):
            off = pl.multiple_of((tt // 8 - 1 - k) * 8, 8)
            A = a_s[pl.ds(off, 8), :]
            B = b_s[pl.ds(off, 8), :]
            for d in (1, 2, 4):
                keep = row < 8 - d
                Bs = jnp.where(keep, pltpu.roll(B, 8 - d, 0), 0.0)
                As = jnp.where(keep, pltpu.roll(A, 8 - d, 0), 1.0)
                B = B + A * Bs
                A = A * As
            Q = B + A * qc[...]
            qext[pl.ds(off, 8), :] = Q
            qc[...] = jnp.broadcast_to(Q[0:1, :], (8, W))
            return 0

        lax.fori_loop(0, tt // 8, step, 0)
        gsc = dhs + qext[pl.ds(1, tt), :]
        da = gsc * hext[pl.ds(7, tt), :]
        t1 = gsc * xc
        dmult = t1 * gate_i
        dgate_i = t1 * mult
        dxc = gsc * (mult * gate_i)
        dlog_a = da * a - dmult * (a * a) / mult
        dgate_r = dlog_a * (LRU_C * ls)
        sm_ref[7:8, :] += jnp.sum(dlog_a * (LRU_C * gate_r), axis=0, keepdims=True)
        dga = dgate_r * gate_r * (1.0 - gate_r)
        dgx = dgate_i * gate_i * (1.0 - gate_i)
        sm_ref[5:6, :] += jnp.sum(dga, axis=0, keepdims=True)
        sm_ref[6:7, :] += jnp.sum(dgx, axis=0, keepdims=True)
        dgab = dga.astype(BF16)
        dgxb = dgx.astype(BF16)
        dxc = dxc + _heads_dot(dgab, wa_ref, hd, nt=True) + _heads_dot(dgxb, wx_ref, hd, nt=True)
        for h in range(HEADS):
            sl = slice(h * hd, (h + 1) * hd)
            dwa_ref[h] += _dot_tn(xcb[:, sl], dgab[:, sl])
            dwx_ref[h] += _dot_tn(xcb[:, sl], dgxb[:, sl])
        sm_ref[4:5, :] += jnp.sum(dxc, axis=0, keepdims=True)
        for k in range(CONV_W):
            sm_ref[k:k + 1, :] += jnp.sum(dxc * xext[pl.ds(8 - (CONV_W - 1) + k, tt), :], axis=0, keepdims=True)
        dext[pl.ds(0, tt), :] = dxc
        dext[pl.ds(tt, 8), :] = dc[...]
        dxr = cw[0:1, :] * dext[pl.ds(CONV_W - 1, tt), :]
        for k in range(1, CONV_W):
            dxr = dxr + cw[k:k + 1, :] * dext[pl.ds(CONV_W - 1 - k, tt), :]
        dc[...] = dext[0:8, :]
        sm_ref[8:9, :] += jnp.sum(dxr, axis=0, keepdims=True)
        dxrb = dxr.astype(BF16)
        dxr_ref[...] = dxrb

        dh = _dot_nt(dxrb, win_ref[...].reshape(W, W)) + _dot_nt(dgbb, wy_ref[...].reshape(W, W))
        g, sc = gm_ref[...], mod_ref[1:2, :]
        _, xhat, r, n = _rms_fwd(x_ref[...], g, sc, mod_ref[0:1, :])
        dx, dsh, dsc, dg = _rms_bwd(dh, xhat, r, n, g, sc)
        acc_ref[0:1, :] += dsh
        acc_ref[1:2, :] += dsc
        acc_ref[6:7, :] += dg
        dx_ref[...] = dx1v + dx

        @pl.when(i == nb - 1)
        def _():
            sm_ref[7:8, :] = sm_ref[7:8, :] * jax.nn.sigmoid(-lam)

    rev = lambda i: (nb - 1 - i, 0)
    tile = pl.BlockSpec((tt, W), rev)
    halo = pl.BlockSpec((8, W), lambda i: (jnp.maximum((nb - 1 - i) * (tt // 8) - 1, 0), 0))
    row = pl.BlockSpec((1, W), lambda i: (0, 0))
    wblk = pl.BlockSpec((HEADS, hd, hd), lambda i: (0, 0, 0))
    bf16o = jax.ShapeDtypeStruct((S, W), BF16)
    return _pcall(
        body, name="lru_bwd", grid=(nb,),
        out_shape=(bf16o, bf16o, bf16o, jax.ShapeDtypeStruct((S, W), F32),
                   jax.ShapeDtypeStruct((16, W), F32), jax.ShapeDtypeStruct((HEADS, hd, hd), F32),
                   jax.ShapeDtypeStruct((HEADS, hd, hd), F32), jax.ShapeDtypeStruct((8, W), F32)),
        in_specs=[tile, tile, tile, tile, halo, tile, tile, halo, pl.BlockSpec((8, W), lambda i: (0, 0)), row,
                  _wspec(g_wout), _wspec(g_wy), _wspec(g_win), pl.BlockSpec((CONV_W, W), lambda i: (0, 0)), row,
                  wblk, row, wblk, row, row],
        out_specs=(tile, tile, tile, tile, pl.BlockSpec((16, W), lambda i: (0, 0)), wblk, wblk,
                   pl.BlockSpec((8, W), lambda i: (0, 0))),
        scratch_shapes=[pltpu.VMEM((tt + 8, W), F32), pltpu.VMEM((tt + 8, W), F32), pltpu.VMEM((tt + 8, W), F32),
                        pltpu.VMEM((tt + 8, W), F32), pltpu.VMEM((tt, W), F32), pltpu.VMEM((tt, W), F32),
                        pltpu.VMEM((8, W), F32), pltpu.VMEM((8, W), F32)],
        operands=(dx1, y, x, xr0, xr0, gb, hs, hs, mod_l, g_mix, g_wout, g_wy, g_win, cw, cb, wa, ba, wx, bx, lam),
        **comm)


def _pool_bwd(dx1, x, pooled, mod_l, g_mix, pw, ps, h2, du, a, dz):
    S, D = x.shape
    tm = min(TP, S)
    nb = S // tm
    ng = len(POOL_WINDOWS)
    gd = D // ng
    n = tm + 24
    f4 = du.shape[1] // NQ
    assert nb % NQ == 0
    kch = nb // NQ
    kr = S // kch

    def body(dx1_ref, dxh_ref, x_ref, pl_ref, mod_ref, gm_ref, pw_ref, ps_ref, h2_ref, du_ref, a_ref, dz_ref,
             dx_ref, acc_ref, dpw_ref, dw1_ref, dw2_ref, ext, b1, b2, acc1, acc2):
        i = pl.program_id(0)

        @pl.when(i == 0)
        def _():
            acc_ref[...] = jnp.zeros_like(acc_ref)
            dpw_ref[...] = jnp.zeros_like(dpw_ref)

        @pl.when(i % kch == 0)
        def _():
            acc1[...] = jnp.zeros_like(acc1)
            acc2[...] = jnp.zeros_like(acc2)

        acc1[...] += _dot_tn(h2_ref[...], du_ref[...])
        av = a_ref[...]
        acc2[...] += _dot_tn(av * av, dz_ref[...])

        gt, psv = mod_ref[2:3, :], ps_ref[...]
        wvec = _window_vec(D)
        dx1v = dx1_ref[...]
        pooled = pl_ref[...]
        mixed = _heads_dot(pooled, pw_ref, gd)
        acc_ref[2:3, :] += jnp.sum(dx1v * (mixed * psv), axis=0, keepdims=True)
        dy = dx1v * gt
        acc_ref[3:4, :] += jnp.sum(dy * mixed, axis=0, keepdims=True)
        dmix = (dy * psv).astype(BF16)
        for gi in range(ng):
            sl = slice(gi * gd, (gi + 1) * gd)
            dpw_ref[gi] += _dot_tn(pooled[:, sl], dmix[:, sl])
        dpooled = _heads_dot(dmix, pw_ref, gd, nt=True)
        dmix_h = (dxh_ref[...] * gt * psv).astype(BF16)
        dpooled_h = jnp.where(i < nb - 1, _heads_dot(dmix_h, pw_ref, gd, nt=True), 0.0)
        t1 = (lax.broadcasted_iota(jnp.int32, (tm, 1), 0) + (i * tm + 1)).astype(F32)
        t1h = (lax.broadcasted_iota(jnp.int32, (16, 1), 0) + ((i + 1) * tm + 1)).astype(F32)
        zeros8 = jnp.zeros((8, D), F32)
        ext[pl.ds(0, tm), :] = dpooled / jnp.minimum(t1, wvec)
        ext[pl.ds(tm, 16), :] = dpooled_h / jnp.minimum(t1h, wvec)
        ext[pl.ds(tm + 16, 8), :] = zeros8
        b1[pl.ds(tm + 16, 8), :] = zeros8
        b2[pl.ds(tm + 16, 8), :] = zeros8
        m = n - 8
        b1[pl.ds(0, m), :] = ext[pl.ds(0, m), :] + ext[pl.ds(1, m), :]
        b2[pl.ds(0, m), gd:] = b1[pl.ds(0, m), gd:] + b1[pl.ds(2, m), gd:]
        b1[pl.ds(0, m), 2 * gd:] = b2[pl.ds(0, m), 2 * gd:] + b2[pl.ds(4, m), 2 * gd:]
        b2[pl.ds(0, m), 3 * gd:] = b1[pl.ds(0, m), 3 * gd:] + b1[pl.ds(8, m), 3 * gd:]
        wsum = jnp.concatenate([b1[pl.ds(0, tm), 0:gd], b2[pl.ds(0, tm), gd:2 * gd],
                                b1[pl.ds(0, tm), 2 * gd:3 * gd], b2[pl.ds(0, tm), 3 * gd:]], axis=1)
        dh = wsum - dpooled
        g, sc = gm_ref[...], mod_ref[1:2, :]
        _, xhat, r, nn = _rms_fwd(x_ref[...], g, sc, mod_ref[0:1, :])
        dx, dsh, dsc, dg = _rms_bwd(dh, xhat, r, nn, g, sc)
        acc_ref[0:1, :] += dsh
        acc_ref[1:2, :] += dsc
        acc_ref[6:7, :] += dg
        dx_ref[...] = dx1v + dx

        @pl.when(i % kch == kch - 1)
        def _():
            dw1_ref[...] = acc1[...].astype(BF16)
            dw2_ref[...] = acc2[...].astype(BF16)

    tile = pl.BlockSpec((tm, D), lambda i: (i, 0))
    halo = pl.BlockSpec((16, D), lambda i: (jnp.minimum((i + 1) * (tm // 16), S // 16 - 1), 0))
    row = pl.BlockSpec((1, D), lambda i: (0, 0))
    wblk = pl.BlockSpec((ng, gd, gd), lambda i: (0, 0, 0))
    full_k = pl.BlockSpec((kr, D), lambda i: (i % kch, 0))
    part_k = pl.BlockSpec((kr, f4), lambda i: (i % kch, i // kch))
    return pl.pallas_call(
        body, name="pool_bwd", grid=(nb,),
        out_shape=(jax.ShapeDtypeStruct((S, D), F32), jax.ShapeDtypeStruct((8, D), F32),
                   jax.ShapeDtypeStruct((ng, gd, gd), F32), jax.ShapeDtypeStruct((NQ, D, f4), BF16),
                   jax.ShapeDtypeStruct((NQ, f4, D), BF16)),
        in_specs=[tile, halo, tile, tile, pl.BlockSpec((8, D), lambda i: (0, 0)), row, wblk, row,
                  full_k, part_k, part_k, full_k],
        out_specs=(tile, pl.BlockSpec((8, D), lambda i: (0, 0)), wblk,
                   pl.BlockSpec((None, D, f4), lambda i: (i // kch, 0, 0)),
                   pl.BlockSpec((None, f4, D), lambda i: (i // kch, 0, 0))),
        scratch_shapes=[pltpu.VMEM((n, D), F32), pltpu.VMEM((n, D), F32), pltpu.VMEM((n, D), F32),
                        pltpu.VMEM((D, f4), F32), pltpu.VMEM((f4, D), F32)],
        compiler_params=_cparams("arbitrary"),
    )(dx1, dx1, x, pooled, mod_l, g_mix, pw, ps, h2, du, a, dz)


def _shard_to_rows(w, D):
    return w.reshape(-1, D)


def _blockdiag_full(gq, na, hd):
    return gq.reshape(NQ, na, HEADS, hd // NQ, hd).transpose(1, 2, 0, 3, 4).reshape(na, HEADS, hd, hd)


def _blockdiag_by_chip(dw, D):
    na, _, hd, _ = dw.shape
    return dw.reshape(na, HEADS, NQ, hd // NQ, hd).transpose(2, 0, 1, 3, 4).reshape(NQ, -1, D)


def kernel(x, c, w_mod, b_mod, norm_mix_g, norm_ffn_g, lru_w_y, lru_b_y, lru_w_in, lru_b_in, lru_conv_w, lru_conv_b, lru_w_a, lru_b_a, lru_w_x, lru_b_x, lru_lambda, lru_w_out, lru_b_out, pool_w, pool_scale, ffn_w1, ffn_w2, final_norm_g, loss_target, m_w_mod, m_b_mod, m_norm_mix_g, m_norm_ffn_g, m_lru_w_y, m_lru_b_y, m_lru_w_in, m_lru_b_in, m_lru_conv_w, m_lru_conv_b, m_lru_w_a, m_lru_b_a, m_lru_w_x, m_lru_b_x, m_lru_lambda, m_lru_w_out, m_lru_b_out, m_pool_w, m_pool_scale, m_ffn_w1, m_ffn_w2, m_final_norm_g, v_w_mod, v_b_mod, v_norm_mix_g, v_norm_ffn_g, v_lru_w_y, v_lru_b_y, v_lru_w_in, v_lru_b_in, v_lru_conv_w, v_lru_conv_b, v_lru_w_a, v_lru_b_a, v_lru_w_x, v_lru_b_x, v_lru_lambda, v_lru_w_out, v_lru_b_out, v_pool_w, v_pool_scale, v_ffn_w1, v_ffn_w2, v_final_norm_g):
    S, D = x.shape[1], x.shape[2]
    L = w_mod.shape[0]
    NA = lru_w_y.shape[0]
    NB = pool_w.shape[0]
    F = ffn_w1.shape[2] * NQ
    f4 = F // NQ
    hd = D // HEADS
    Cs = w_mod.shape[2]
    assert L == DEPTH and Cs * NQ == N_MOD * D and D % 1024 == 0
    x2d = x.reshape(S, D)
    tgt = loss_target.reshape(S, D)
    q = 2 * lax.axis_index("x") + lax.axis_index("y")

    big = [ffn_w1, ffn_w2, lru_w_y, lru_w_in, lru_w_out, lru_w_a, lru_w_x, pool_w]
    rows = [int(w.size) // D for w in big]
    offs = [sum(rows[:k]) for k in range(len(big))]
    O_W1, O_W2, O_WY, O_WIN, O_WOUT, O_WA, O_WX, O_PW = offs
    R = sum(rows)
    dq = D // NQ
    s_w1 = [ffn_w1[i].astype(BF16) for i in range(L)]
    s_w2 = [ffn_w2[i].astype(BF16) for i in range(L)]
    s_wy = [lru_w_y[j].astype(BF16) for j in range(NA)]
    s_win = [lru_w_in[j].astype(BF16) for j in range(NA)]
    s_wout = [lru_w_out[j].astype(BF16) for j in range(NA)]
    s_tiny = jnp.concatenate([_shard_to_rows(w, D) for w in (lru_w_a, lru_w_x, pool_w)], axis=0).astype(BF16)

    cshard = lru_conv_w.reshape(-1)
    small_fwd = jnp.concatenate([c.reshape(-1), cshard, lru_b_a.reshape(-1), lru_b_x.reshape(-1),
                                 pool_scale.reshape(-1)])
    small_fwd = jnp.pad(small_fwd, (0, 8 * D - small_fwd.shape[0])).reshape(8, D)

    g_w1, g_w2 = [None] * L, [None] * L
    g_wy, g_win, g_wout = [None] * NA, [None] * NA, [None] * NA
    SG, g_wy[0], g_win[0], g_wout[0], g_tiny = _comm_only("gather_first", small=small_fwd,
                                                         gathers=(s_wy[0], s_win[0], s_wout[0], s_tiny))
    wa_full = _blockdiag_full(g_tiny[:, :rows[5]], NA, hd)
    wx_full = _blockdiag_full(g_tiny[:, rows[5]:rows[5] + rows[6]], NA, hd)
    pw_full = _blockdiag_full(g_tiny[:, rows[5] + rows[6]:], NB, hd)
    SGf = SG.reshape(NDEV, 8 * D)
    c_all = SGf[:, :D]
    SGq = SGf.reshape(NQ, 2, 8 * D)[:, 0]
    o = D
    n_cw = NA * CONV_W * D // NQ
    conv_w_full = SGq[:, o:o + n_cw].reshape(NQ, NA, CONV_W, D // NQ).transpose(1, 2, 0, 3).reshape(NA, CONV_W, D)
    o += n_cw
    n_b = NA * HEADS * hd // NQ
    b_a_full = SGq[:, o:o + n_b].reshape(NQ, NA, HEADS, hd // NQ).transpose(1, 2, 0, 3).reshape(NA, 1, D)
    o += n_b
    b_x_full = SGq[:, o:o + n_b].reshape(NQ, NA, HEADS, hd // NQ).transpose(1, 2, 0, 3).reshape(NA, 1, D)
    o += n_b
    n_ps = NB * D // NQ
    pool_scale_full = SGq[:, o:o + n_ps].reshape(NQ, NB, D // NQ).transpose(1, 0, 2).reshape(NB, 1, D)


    b_mod_sh = lax.dynamic_slice_in_dim(b_mod, q * Cs, Cs, axis=1).reshape(L, 1, Cs)
    modpart = _mod_part(c_all, w_mod, b_mod_sh)
    modq = _exchange_mod(modpart.transpose(1, 0, 2))
    mod = modq.transpose(1, 0, 2).reshape(L, N_MOD, D)
    mod = jnp.pad(mod, ((0, 0), (0, 8 - N_MOD), (0, 0)))

    saved = []
    xcur = x2d
    for i in range(L):
        j = i // 2
        gm = norm_mix_g[i].reshape(1, D)
        gf = norm_ffn_g[i].reshape(1, D)
        if i % 2 == 0:
            h, gb, xr0, hs, p, y, x1, g_w1[i], g_w2[i] = _lru_fwd(
                xcur, mod[i], gm, g_wy[j], g_win[j], lru_b_y[j].reshape(1, D), lru_b_in[j].reshape(1, D),
                conv_w_full[j], lru_conv_b[j].reshape(1, D), wa_full[j], b_a_full[j], wx_full[j], b_x_full[j],
                lru_lambda[j].reshape(1, D), g_wout[j], lru_b_out[j].reshape(1, D), gathers=(s_w1[i], s_w2[i]))
            h2, a, z, x2, g_w1[i + 1], g_w2[i + 1] = _ffn_fwd(x1, mod[i], gf, g_w1[i], g_w2[i],
                                                              gathers=(s_w1[i + 1], s_w2[i + 1]))
            saved.append(dict(x=xcur, h=h, gb=gb, xr0=xr0, hs=hs, p=p, y=y, x1=x1, h2=h2, a=a, z=z))
        else:
            if j + 1 < NA:
                pooled, x1, h2, a, z, x2, g_wy[j + 1], g_win[j + 1], g_wout[j + 1] = _pool_mix_ffn_fwd(
                    xcur, mod[i], gm, pw_full[j], pool_scale_full[j], gf, g_w1[i], g_w2[i],
                    gathers=(s_wy[j + 1], s_win[j + 1], s_wout[j + 1]))
            else:
                pooled, x1, h2, a, z, x2 = _pool_mix_ffn_fwd(xcur, mod[i], gm, pw_full[j], pool_scale_full[j], gf,
                                                             g_w1[i], g_w2[i])
            saved.append(dict(x=xcur, pooled=pooled, x1=x1, h2=h2, a=a, z=z))
        xcur = x2

    dx = xcur
    qv = q.reshape(1).astype(jnp.int32)
    ppack = lax.empty((R, D), F32)
    psib = lax.empty((R, D), F32)
    pending, summed = [], []

    def comm_args():
        kw = {}
        if pending:
            kw["scatters"] = tuple(dw for dw, _ in pending)
        if summed:
            kw["sib"] = (ppack, psib, tuple(summed))
        return kw

    def after_host(extra):
        nonlocal ppack, psib, pending, summed
        had_sib = bool(summed)
        summed = []
        for (dw, off), rb in zip(pending, extra[:len(pending)]):
            ppack = _sum_into(ppack, dw, rb, off, qv)
            summed.append((off, dw.shape[1]))
        if had_sib:
            psib = extra[len(pending)]
        pending = []

    dmod_rows = [None] * L
    dg_mix = [None] * L
    dg_ffn = [None] * L
    d_small = {}
    dwa_l, dwx_l, dpw_l = [None] * NA, [None] * NA, [None] * NB
    for i in reversed(range(L)):
        j = i // 2
        sv = saved[i]
        gm = norm_mix_g[i].reshape(1, D)
        gf = norm_ffn_g[i].reshape(1, D)
        head = (final_norm_g.reshape(1, D), tgt) if i == L - 1 else None
        outs = _ffn_bwd(dx, sv["x1"], sv["a"], sv["z"], mod[i], gf, g_w1[i], g_w2[i], head=head, **comm_args())
        dx1, du, dz, facc = outs[:4]
        after_host(outs[4:])
        if head:
            loss = lax.psum(0.5 * jnp.sum(facc[0]) / D, ("x", "y", "c"))
            d_final_g = facc[1]
        if i % 2 == 0:
            pending.append((_dw_blocked(sv["h2"], du, False, False, "dw1"), O_W1 + i * D))
            pending.append((_dw_blocked(sv["a"], dz, True, True, "dw2"), O_W2 + i * f4))
            outs = _lru_bwd(dx1, sv["y"], sv["x"], sv["xr0"], sv["gb"], sv["hs"], mod[i], gm, g_wout[j], g_wy[j],
                            g_win[j], conv_w_full[j], lru_conv_b[j].reshape(1, D), wa_full[j], b_a_full[j], wx_full[j],
                            b_x_full[j], lru_lambda[j].reshape(1, D), **comm_args())
            dyp, dgb, dxr, dx, sm, dwa, dwx, macc = outs[:8]
            after_host(outs[8:])
            dwa_l[j], dwx_l[j] = dwa, dwx
            if i == 0:
                tiny = jnp.concatenate([_blockdiag_by_chip(jnp.stack(dwa_l), D), _blockdiag_by_chip(jnp.stack(dwx_l), D),
                                        _blockdiag_by_chip(jnp.stack(dpw_l), D)], axis=1).astype(BF16)
                pending.append((tiny, O_WA))
            outs = _dw_whole(sv["p"], [dyp], "dwout", **comm_args())
            after_host(outs[1:])
            pending.append((outs[0], O_WOUT + j * dq))
            outs = _dw_whole(sv["h"], [dgb, dxr], "dwy_dwin", **comm_args())
            after_host(outs[2:])
            pending.append((outs[0], O_WY + j * dq))
            pending.append((outs[1], O_WIN + j * dq))
            d_small[("lru", j)] = (sm, macc[3])
            dgt_m = macc[2]
        else:
            dx, macc, dpw, dw1, dw2 = _pool_bwd(dx1, sv["x"], sv["pooled"], mod[i], gm, pw_full[j], pool_scale_full[j],
                                                sv["h2"], du, sv["a"], dz)
            pending.append((dw1, O_W1 + i * D))
            pending.append((dw2, O_W2 + i * f4))
            dpw_l[j] = dpw
            d_small[("pool", j)] = macc[3]
            dgt_m = macc[2]
        dmod_rows[i] = jnp.stack([macc[0], macc[1], dgt_m, facc[3], facc[4], facc[5]])
        dg_mix[i] = macc[6]
        dg_ffn[i] = facc[7]
    grad_x = dx.reshape(x.shape)

    lru_sm = [d_small[("lru", j)] for j in range(NA)]
    small_rows = [jnp.stack(dmod_rows).reshape(L * N_MOD, D), jnp.stack(dg_mix), jnp.stack(dg_ffn),
                  jnp.stack([s[0][9] for s in lru_sm]), jnp.stack([s[0][8] for s in lru_sm]),
                  jnp.stack([s[0][4] for s in lru_sm]), jnp.stack([s[0][7] for s in lru_sm]),
                  jnp.stack([s[1] for s in lru_sm]),
                  jnp.stack([s[0][0:CONV_W] for s in lru_sm]).reshape(NA * CONV_W, D),
                  jnp.stack([s[0][5] for s in lru_sm]), jnp.stack([s[0][6] for s in lru_sm]),
                  jnp.stack([d_small[("pool", j)] for j in range(NB)]), d_final_g.reshape(1, D)]
    small_g = jnp.concatenate(small_rows, axis=0)
    n_small = small_g.shape[0]
    assert n_small <= SMALL_ROWS
    small_g = jnp.pad(small_g, ((0, SMALL_ROWS - n_small), (0, 0)))

    outs = _comm_only("scatter_last", small=small_g, reduce_small=True, **comm_args())
    sg_all, sg_sum = outs[:2]
    after_host(outs[2:])
    psum_mine = ppack
    psum_sib = _comm_only("sibling_last", sib=(ppack, psib, tuple(summed)))[0]

    def big_update(w, m, v, off, name):
        shp = w.shape
        g, dl, m2, v2 = _adam_rows(w.reshape(-1, D), m.reshape(-1, D), v.reshape(-1, D), psum_mine, psum_sib, off, name)
        return g.reshape(shp), dl.reshape(shp), m2.reshape(shp), v2.reshape(shp)

    res = {}
    res["ffn_w1"] = big_update(ffn_w1, m_ffn_w1, v_ffn_w1, O_W1, "adam_w1")
    res["ffn_w2"] = big_update(ffn_w2, m_ffn_w2, v_ffn_w2, O_W2, "adam_w2")
    res["lru_w_y"] = big_update(lru_w_y, m_lru_w_y, v_lru_w_y, O_WY, "adam_wy")
    res["lru_w_in"] = big_update(lru_w_in, m_lru_w_in, v_lru_w_in, O_WIN, "adam_win")
    res["lru_w_out"] = big_update(lru_w_out, m_lru_w_out, v_lru_w_out, O_WOUT, "adam_wout")

    def tiny_parts(w, off):
        n = int(w.size) // D
        return psum_mine[off:off + n].reshape(w.shape), psum_sib[off:off + n].reshape(w.shape)

    tiny_items = [("lru_w_a", lru_w_a, m_lru_w_a, v_lru_w_a) + tiny_parts(lru_w_a, O_WA),
                  ("lru_w_x", lru_w_x, m_lru_w_x, v_lru_w_x) + tiny_parts(lru_w_x, O_WX),
                  ("pool_w", pool_w, m_pool_w, v_pool_w) + tiny_parts(pool_w, O_PW)]

    dmod_all = sg_all[:, :L * N_MOD, :].reshape(NDEV, L, N_MOD * D)
    dmod_sh = lax.dynamic_slice_in_dim(dmod_all, q * Cs, Cs, axis=2).transpose(1, 0, 2)
    res["w_mod"] = _wmod_update(c_all.T, dmod_sh, w_mod, m_w_mod, v_w_mod)

    r0 = 0
    by_rows, names_a = [], []
    for name, w, m, v in (("b_mod", b_mod, m_b_mod, v_b_mod), ("norm_mix_g", norm_mix_g, m_norm_mix_g, v_norm_mix_g),
                          ("norm_ffn_g", norm_ffn_g, m_norm_ffn_g, v_norm_ffn_g),
                          ("lru_b_y", lru_b_y, m_lru_b_y, v_lru_b_y), ("lru_b_in", lru_b_in, m_lru_b_in, v_lru_b_in),
                          ("lru_conv_b", lru_conv_b, m_lru_conv_b, v_lru_conv_b),
                          ("lru_lambda", lru_lambda, m_lru_lambda, v_lru_lambda),
                          ("lru_b_out", lru_b_out, m_lru_b_out, v_lru_b_out)):
        by_rows.append((w, m, v, r0))
        names_a.append(name)
        r0 += int(w.size) // D
    g_conv_w = lax.dynamic_slice_in_dim(sg_sum[r0:r0 + NA * CONV_W].reshape(NA, CONV_W, D), q * dq, dq, axis=2)
    r0 += NA * CONV_W
    g_b_a = lax.dynamic_slice_in_dim(sg_sum[r0:r0 + NA].reshape(NA, HEADS, hd), q * (hd // NQ), hd // NQ, axis=2)
    r0 += NA
    g_b_x = lax.dynamic_slice_in_dim(sg_sum[r0:r0 + NA].reshape(NA, HEADS, hd), q * (hd // NQ), hd // NQ, axis=2)
    r0 += NA
    g_ps = lax.dynamic_slice_in_dim(sg_sum[r0:r0 + NB], q * dq, dq, axis=1)
    r0 += NB
    by_rows.append((final_norm_g.reshape(1, D), m_final_norm_g.reshape(1, D), v_final_norm_g.reshape(1, D), r0))
    names_a.append("final_norm_g")
    sliced = [(lru_conv_w, m_lru_conv_w, v_lru_conv_w, g_conv_w), (lru_b_a, m_lru_b_a, v_lru_b_a, g_b_a),
              (lru_b_x, m_lru_b_x, v_lru_b_x, g_b_x), (pool_scale, m_pool_scale, v_pool_scale, g_ps)]
    res_a, res_b, res_c = _adam_small(sg_sum, by_rows, sliced, [t[1:] for t in tiny_items])
    for name, r in zip(names_a, res_a):
        res[name] = r
    for t, r in zip(tiny_items, res_c):
        res[t[0]] = r
    res["final_norm_g"] = tuple(a.reshape(D) for a in res["final_norm_g"])
    for name, (_, _, _, g), r in zip(("lru_conv_w", "lru_b_a", "lru_b_x", "pool_scale"), sliced, res_b):
        res[name] = (g,) + r

    order = ["w_mod", "b_mod", "norm_mix_g", "norm_ffn_g", "lru_w_y", "lru_b_y", "lru_w_in", "lru_b_in", "lru_conv_w",
             "lru_conv_b", "lru_w_a", "lru_b_a", "lru_w_x", "lru_b_x", "lru_lambda", "lru_w_out", "lru_b_out", "pool_w",
             "pool_scale", "ffn_w1", "ffn_w2", "final_norm_g"]
    return (loss, grad_x, *[res[n][0] for n in order], *[res[n][1] for n in order],
            *[res[n][2] for n in order], *[res[n][3] for n in order])
```

```python
import functools

import jax
import jax.numpy as jnp
from jax import lax
from jax.experimental import pallas as pl
from jax.experimental.pallas import tpu as pltpu

F32 = jnp.float32
BF16 = jnp.bfloat16
MESH = pl.DeviceIdType.MESH

NQ = 4
NDEV = 8
DEPTH = 4
N_MOD = 6
HEADS = 4
CONV_W = 4
POOL_WINDOWS = (2, 4, 8, 16)
LRU_C = 8.0
EPS = 1e-6
ADAM_LR, ADAM_B1, ADAM_B2, ADAM_EPS, ADAM_WD, ADAM_STEP = 0.001, 0.9, 0.999, 1e-08, 0.01, 10

TM = 512
TT = 256
TP = 256
TK = 2048
SMALL_ROWS = 64
FORWARD_STEPS = 4
VMEM_LIMIT = 60 * 1024 * 1024


def _cparams(*sem):
    return pltpu.CompilerParams(dimension_semantics=tuple(sem), vmem_limit_bytes=VMEM_LIMIT)


def _dot(a, b):
    return jnp.dot(a, b, preferred_element_type=F32)


def _dot_nt(a, b):
    return lax.dot_general(a, b, (((1,), (1,)), ((), ())), preferred_element_type=F32)


def _dot_tn(a, b):
    return lax.dot_general(a, b, (((0,), (0,)), ((), ())), preferred_element_type=F32)


def _resident(shape, index_map):
    return pl.BlockSpec(shape, index_map, pipeline_mode=pl.Buffered(1))


def _rms_fwd(x, g, sc, sh):
    r = lax.rsqrt(jnp.mean(x * x, axis=-1, keepdims=True) + EPS)
    xhat = x * r
    n = xhat * g
    return n * (1.0 + sc) + sh, xhat, r, n


def _rms_bwd(dh, xhat, r, n, g, sc):
    dsh = jnp.sum(dh, axis=0, keepdims=True)
    dsc = jnp.sum(dh * n, axis=0, keepdims=True)
    dn = dh * (1.0 + sc)
    dg = jnp.sum(dn * xhat, axis=0, keepdims=True)
    dxh = dn * g
    dx = r * (dxh - xhat * jnp.mean(dxh * xhat, axis=-1, keepdims=True))
    return dx, dsh, dsc, dg


_GELU_K = 0.7978845608028654
_GELU_C = 0.044715


def _gelu(x):
    t = jnp.tanh(_GELU_K * (x + _GELU_C * x * x * x))
    return 0.5 * x * (1.0 + t), t


def _gelu_grad(x, t):
    return 0.5 * (1.0 + t) + 0.5 * x * (1.0 - t * t) * (_GELU_K * (1.0 + 3.0 * _GELU_C * x * x))


def _neg_expm1(y, exp_y):
    series = -(y * (1.0 + y * (0.5 + y * (1.0 / 6.0))))
    return jnp.where(y > -(1.0 / 64.0), series, 1.0 - exp_y)


def _zero_first(ref):
    @pl.when(pl.program_id(0) == 0)
    def _():
        ref[...] = jnp.zeros_like(ref)


def _my_pos():
    return lax.axis_index("x"), lax.axis_index("y"), lax.axis_index("c")


def _dev_index(x, y, c):
    return 4 * x + 2 * y + c


def _chip_peers(x, y):
    return [(1 - x, y), (x, 1 - y), (1 - x, 1 - y)]


def _all_peers(x, y, c):
    return [(px, py, c) for (px, py) in _chip_peers(x, y)] + [(x, y, 1 - c)] + \
           [(px, py, 1 - c) for (px, py) in _chip_peers(x, y)]


def _comm_run(phase, x, y, c, gathers, scatters, sib, send, recv, loc):
    q = 2 * x + y
    peers = _chip_peers(x, y)
    sibling = (x, y, 1 - c)

    def rcopy(src, dst, s, dev):
        return pltpu.make_async_remote_copy(src, dst, send.at[s], recv.at[s], device_id=dev, device_id_type=MESH)

    s = 0
    for gi, (src, dst) in enumerate(gathers):
        half = src.shape[0] // 2
        mine, other = pl.ds(c * half, half), pl.ds((1 - c) * half, half)
        own = pltpu.make_async_copy(src, dst.at[q], loc.at[gi])
        if phase == "start":
            own.start()
        elif phase == "finish":
            own.wait()
        for (px, py) in peers:
            pq = 2 * px + py
            s_ici, s_fwd = s, s + 1
            s += 2
            if phase == "start":
                rcopy(src.at[mine], dst.at[q].at[mine], s_ici, (px, py, c)).start()
            elif phase == "forward":
                rcopy(src.at[mine], dst.at[pq].at[mine], s_ici, (px, py, c)).wait_recv()
                rcopy(dst.at[pq].at[mine], dst.at[pq].at[mine], s_fwd, sibling).start()
            else:
                rcopy(dst.at[pq].at[other], dst.at[pq].at[other], s_fwd, sibling).wait_recv()
                rcopy(src.at[mine], dst.at[q].at[mine], s_ici, (px, py, c)).wait_send()
                rcopy(dst.at[pq].at[mine], dst.at[pq].at[mine], s_fwd, sibling).wait_send()
    direct = []
    for (src, dst) in scatters:
        for k, (px, py) in enumerate(peers):
            direct.append((src.at[2 * px + py], dst.at[k], (px, py, c)))
    if sib is not None:
        src, dst, ranges = sib
        for (off, rows) in ranges:
            direct.append((src.at[pl.ds(off, rows)], dst.at[pl.ds(off, rows)], sibling))
    if phase == "start":
        for k, (a, b, dev) in enumerate(direct):
            rcopy(a, b, s + k, dev).start()
    elif phase == "finish":
        for k, (a, b, dev) in enumerate(direct):
            rcopy(a, b, s + k, dev).wait_recv()
        for k, (a, b, dev) in enumerate(direct):
            rcopy(a, b, s + k, dev).wait_send()


def _comm_shapes(gathers, scatters, sib):
    assert all(g.shape[0] % 32 == 0 for g in gathers)
    cin = list(gathers) + list(scatters) + ([sib[0], sib[1]] if sib else [])
    cout = [jax.ShapeDtypeStruct((NQ,) + g.shape, g.dtype) for g in gathers] + \
           [jax.ShapeDtypeStruct((3,) + s.shape[1:], s.dtype) for s in scatters] + \
           ([jax.ShapeDtypeStruct(sib[1].shape, sib[1].dtype)] if sib else [])
    n_rem = 6 * len(gathers) + 3 * len(scatters) + (len(sib[2]) if sib else 0)
    sems = [pltpu.SemaphoreType.DMA((max(n_rem, 1),)), pltpu.SemaphoreType.DMA((max(n_rem, 1),)),
            pltpu.SemaphoreType.DMA((max(len(gathers), 1),))]
    return cin, cout, sems


def _pcall(body, *, name, grid, in_specs, out_specs, out_shape, operands, scratch_shapes=(),
           gathers=(), scatters=(), sib=None):
    assert len(grid) == 1
    out_shape, out_specs = tuple(out_shape), tuple(out_specs)
    if not (gathers or scatters or sib):
        return pl.pallas_call(body, name=name, grid=grid, in_specs=list(in_specs), out_specs=out_specs,
                              out_shape=out_shape, scratch_shapes=list(scratch_shapes),
                              compiler_params=_cparams("arbitrary"))(*operands)
    cin, cout, sems = _comm_shapes(gathers, scatters, sib)
    n_in, n_cin, n_out, n_cout, n_scr = len(operands), len(cin), len(out_shape), len(cout), len(scratch_shapes)
    ng, ns = len(gathers), len(scatters)
    nsteps = grid[0]

    def wrapped(*refs):
        ins = refs[:n_in]
        cins = refs[n_in:n_in + n_cin]
        o0 = n_in + n_cin
        outs = refs[o0:o0 + n_out]
        couts = refs[o0 + n_out:o0 + n_out + n_cout]
        s0 = o0 + n_out + n_cout
        scr = refs[s0:s0 + n_scr]
        send, recv, loc = refs[s0 + n_scr:s0 + n_scr + 3]
        x, y, c = _my_pos()

        def run(phase):
            g = [(cins[k], couts[k]) for k in range(ng)]
            sc = [(cins[ng + k], couts[ng + k]) for k in range(ns)]
            sb = (cins[ng + ns], couts[ng + ns], sib[2]) if sib else None
            _comm_run(phase, x, y, c, g, sc, sb, send, recv, loc)

        @pl.when(pl.program_id(0) == 0)
        def _():
            run("start")

        if ng:
            @pl.when(pl.program_id(0) == max(nsteps - FORWARD_STEPS, 0))
            def _():
                run("forward")

        body(*ins, *outs, *scr)

        @pl.when(pl.program_id(0) == nsteps - 1)
        def _():
            run("finish")

    anyspec = pl.BlockSpec(memory_space=pl.ANY)
    aliases = {n_in + ng + ns + 1: n_out + ng + ns} if sib else {}
    return pl.pallas_call(
        wrapped, name=name, grid=grid,
        in_specs=list(in_specs) + [anyspec] * n_cin, out_specs=out_specs + (anyspec,) * n_cout,
        out_shape=out_shape + tuple(cout), scratch_shapes=list(scratch_shapes) + sems,
        input_output_aliases=aliases,
        compiler_params=pltpu.CompilerParams(dimension_semantics=("arbitrary",), vmem_limit_bytes=VMEM_LIMIT,
                                             has_side_effects=True),
    )(*operands, *cin)


def _comm_only(name, small=None, reduce_small=False, gathers=(), scatters=(), sib=None):
    cin, cout, sems = _comm_shapes(gathers, scatters, sib)
    n_cin, n_cout = len(cin), len(cout)
    ng, ns = len(gathers), len(scatters)
    n_sm_in = 1 if small is not None else 0
    n_sm_out = (2 if reduce_small else 1) if small is not None else 0

    def body(*refs):
        sm_in = refs[:n_sm_in]
        cins = refs[n_sm_in:n_sm_in + n_cin]
        o0 = n_sm_in + n_cin
        sm_out = refs[o0:o0 + n_sm_out]
        couts = refs[o0 + n_sm_out:o0 + n_sm_out + n_cout]
        s0 = o0 + n_sm_out + n_cout
        send, recv, loc = refs[s0:s0 + 3]
        x, y, c = _my_pos()
        g = [(cins[k], couts[k]) for k in range(ng)]
        sc = [(cins[ng + k], couts[ng + k]) for k in range(ns)]
        sb = (cins[ng + ns], couts[ng + ns], sib[2]) if sib else None
        _comm_run("start", x, y, c, g, sc, sb, send, recv, loc)
        if small is not None:
            sm_send, sm_recv = refs[s0 + 3:s0 + 5]
            small_ref, sg_ref = sm_in[0], sm_out[0]
            me = _dev_index(x, y, c)
            sg_ref[me] = small_ref[...]
            peers = _all_peers(x, y, c)
            sm = [pltpu.make_async_remote_copy(small_ref, sg_ref.at[me], sm_send.at[k], sm_recv.at[k],
                                               device_id=peer, device_id_type=MESH) for k, peer in enumerate(peers)]
            for cp in sm:
                cp.start()
            for k, (px, py, pc) in enumerate(peers):
                pltpu.make_async_remote_copy(small_ref, sg_ref.at[_dev_index(px, py, pc)], sm_send.at[k], sm_recv.at[k],
                                             device_id=(px, py, pc), device_id_type=MESH).wait_recv()
            if reduce_small:
                acc = sg_ref[0]
                for d in range(1, NDEV):
                    acc = acc + sg_ref[d]
                sm_out[1][...] = acc
            for cp in sm:
                cp.wait_send()
        if ng:
            _comm_run("forward", x, y, c, g, sc, sb, send, recv, loc)
        _comm_run("finish", x, y, c, g, sc, sb, send, recv, loc)

    anyspec = pl.BlockSpec(memory_space=pl.ANY)
    vspec = pl.BlockSpec(memory_space=pltpu.VMEM)
    sm_shapes = []
    if small is not None:
        sm_shapes.append(jax.ShapeDtypeStruct((NDEV,) + small.shape, small.dtype))
        if reduce_small:
            sm_shapes.append(jax.ShapeDtypeStruct(small.shape, small.dtype))
        sems = sems + [pltpu.SemaphoreType.DMA((NDEV - 1,)), pltpu.SemaphoreType.DMA((NDEV - 1,))]
    aliases = {n_sm_in + ng + ns + 1: n_sm_out + ng + ns} if sib else {}
    return pl.pallas_call(
        body, name=name,
        in_specs=[vspec] * n_sm_in + [anyspec] * n_cin,
        out_specs=tuple([vspec] * n_sm_out + [anyspec] * n_cout),
        out_shape=tuple(sm_shapes + cout), scratch_shapes=sems, input_output_aliases=aliases,
        compiler_params=pltpu.CompilerParams(has_side_effects=True),
    )(*([small] if small is not None else []), *cin)


def _exchange_mod(modpart):
    _, L, Cs = modpart.shape

    def body(part_ref, out_ref, send, recv):
        x, y, c = _my_pos()
        q = 2 * x + y
        me = _dev_index(x, y, c)
        out_ref[q] = part_ref[me]
        sends = []
        for k, (px, py) in enumerate(_chip_peers(x, y)):
            cp = pltpu.make_async_remote_copy(part_ref.at[_dev_index(px, py, c)], out_ref.at[q], send.at[k], recv.at[k],
                                              device_id=(px, py, c), device_id_type=MESH)
            cp.start()
            sends.append(cp)
        for k, (px, py) in enumerate(_chip_peers(x, y)):
            pltpu.make_async_remote_copy(part_ref.at[me], out_ref.at[2 * px + py], send.at[k], recv.at[k],
                                         device_id=(px, py, c), device_id_type=MESH).wait_recv()
        for cp in sends:
            cp.wait_send()

    return pl.pallas_call(
        body, name="exchange_mod",
        out_shape=jax.ShapeDtypeStruct((NQ, L, Cs), modpart.dtype),
        in_specs=[pl.BlockSpec(memory_space=pltpu.VMEM)],
        out_specs=pl.BlockSpec(memory_space=pltpu.VMEM),
        scratch_shapes=[pltpu.SemaphoreType.DMA((3,)), pltpu.SemaphoreType.DMA((3,))],
        compiler_params=pltpu.CompilerParams(has_side_effects=True),
    )(modpart)


def _mod_part(c_all, w_mod, b_mod_sh):
    L, D, Cs = w_mod.shape
    tn = 512 if Cs % 512 == 0 else Cs

    def body(c_ref, w_ref, b_ref, o_ref):
        cv = c_ref[...]
        cond = cv * jax.nn.sigmoid(cv)
        o_ref[...] = jnp.dot(cond, w_ref[...], preferred_element_type=F32, precision=lax.Precision.HIGHEST) + b_ref[...]

    return pl.pallas_call(
        body, name="mod_part", grid=(L, Cs // tn),
        out_shape=jax.ShapeDtypeStruct((L, NDEV, Cs), F32),
        in_specs=[pl.BlockSpec((NDEV, D), lambda i, j: (0, 0)),
                  pl.BlockSpec((None, D, tn), lambda i, j: (i, 0, j)),
                  pl.BlockSpec((None, 1, tn), lambda i, j: (i, 0, j))],
        out_specs=pl.BlockSpec((None, NDEV, tn), lambda i, j: (i, 0, j)),
        compiler_params=_cparams("parallel", "parallel"),
    )(c_all, w_mod, b_mod_sh)


def _adam(w, g, m, v):
    m2 = ADAM_B1 * m + (1.0 - ADAM_B1) * g
    v2 = ADAM_B2 * v + (1.0 - ADAM_B2) * (g * g)
    m_hat = m2 / (1.0 - ADAM_B1 ** ADAM_STEP)
    v_hat = v2 / (1.0 - ADAM_B2 ** ADAM_STEP)
    delta = -ADAM_LR * (m_hat / (jnp.sqrt(v_hat) + ADAM_EPS) + ADAM_WD * w)
    return delta, m2, v2


def _wmod_update(c_all_t, dmod_sh, w, m, v):
    L, D, Cs = w.shape
    td = 256 if D % 256 == 0 else D

    def body(ct_ref, d_ref, w_ref, m_ref, v_ref, g_ref, dl_ref, m2_ref, v2_ref):
        cv = ct_ref[...]
        cond = cv * jax.nn.sigmoid(cv)
        g = cond[:, 0:1] * d_ref[0:1, :]
        for b in range(1, NDEV):
            g = g + cond[:, b:b + 1] * d_ref[b:b + 1, :]
        g_ref[...] = g
        dl_ref[...], m2_ref[...], v2_ref[...] = _adam(w_ref[...], g, m_ref[...], v_ref[...])

    blk = pl.BlockSpec((None, td, Cs), lambda i, j: (i, j, 0))
    out = jax.ShapeDtypeStruct((L, D, Cs), F32)
    return pl.pallas_call(
        body, name="wmod_update", grid=(L, D // td),
        out_shape=(out, out, out, out),
        in_specs=[pl.BlockSpec((td, NDEV), lambda i, j: (j, 0)),
                  pl.BlockSpec((None, NDEV, Cs), lambda i, j: (i, 0, 0)), blk, blk, blk],
        out_specs=(blk, blk, blk, blk),
        compiler_params=_cparams("parallel", "parallel"),
    )(c_all_t, dmod_sh, w, m, v)


def _adam_rows(w, m, v, pa, pb, row_off, name):
    rows, C = w.shape
    tr = 512 if rows % 512 == 0 else (128 if rows % 128 == 0 else rows)
    assert row_off % tr == 0
    ob = row_off // tr

    def body(w_ref, m_ref, v_ref, pa_ref, pb_ref, g_ref, dl_ref, m2_ref, v2_ref):
        g = pa_ref[...] + pb_ref[...]
        g_ref[...] = g
        dl_ref[...], m2_ref[...], v2_ref[...] = _adam(w_ref[...], g, m_ref[...], v_ref[...])

    blk = pl.BlockSpec((tr, C), lambda i: (i, 0))
    pblk = pl.BlockSpec((tr, C), lambda i: (ob + i, 0))
    out = jax.ShapeDtypeStruct((rows, C), F32)
    return pl.pallas_call(
        body, name=name, grid=(rows // tr,), out_shape=(out, out, out, out),
        in_specs=[blk, blk, blk, pblk, pblk], out_specs=(blk, blk, blk, blk),
        compiler_params=_cparams("parallel"),
    )(w, m, v, pa, pb)


def _adam_small(sg_sum, by_rows, sliced, pairs):
    D = sg_sum.shape[1]
    na, nb, nc = len(by_rows), len(sliced), len(pairs)

    def body(*refs):
        sg = refs[0]
        ins_a = [refs[1 + 3 * t:4 + 3 * t] for t in range(na)]
        p = 1 + 3 * na
        ins_b = [refs[p + 4 * t:p + 4 * t + 4] for t in range(nb)]
        p += 4 * nb
        ins_c = [refs[p + 5 * t:p + 5 * t + 5] for t in range(nc)]
        p += 5 * nc
        outs_a = [refs[p + 4 * t:p + 4 * t + 4] for t in range(na)]
        p += 4 * na
        outs_b = [refs[p + 3 * t:p + 3 * t + 3] for t in range(nb)]
        p += 3 * nb
        outs_c = [refs[p + 4 * t:p + 4 * t + 4] for t in range(nc)]
        for (w_ref, m_ref, v_ref, ga_ref, gb_ref), (g_ref, dl_ref, m2_ref, v2_ref) in zip(ins_c, outs_c):
            g = ga_ref[...] + gb_ref[...]
            g_ref[...] = g
            dl_ref[...], m2_ref[...], v2_ref[...] = _adam(w_ref[...], g, m_ref[...], v_ref[...])
        for (w_ref, m_ref, v_ref), (g_ref, dl_ref, m2_ref, v2_ref), (w, _, _, row0) in zip(ins_a, outs_a, by_rows):
            n, k = w.shape[0], w.shape[1] // D
            pieces = [(slice(0, n), slice(0, D), slice(row0, row0 + n))] if k == 1 else \
                     [(slice(i, i + 1), slice(kk * D, (kk + 1) * D), slice(row0 + i * k + kk, row0 + i * k + kk + 1))
                      for i in range(n) for kk in range(k)]
            for rs, cs, gs in pieces:
                g = sg[gs, :]
                g_ref[rs, cs] = g
                dl_ref[rs, cs], m2_ref[rs, cs], v2_ref[rs, cs] = _adam(w_ref[rs, cs], g, m_ref[rs, cs], v_ref[rs, cs])
        for (w_ref, m_ref, v_ref, g_ref), (dl_ref, m2_ref, v2_ref) in zip(ins_b, outs_b):
            dl_ref[...], m2_ref[...], v2_ref[...] = _adam(w_ref[...], g_ref[...], m_ref[...], v_ref[...])

    operands = [sg_sum] + [a for t in by_rows for a in t[:3]] + [a for t in sliced for a in t] + \
               [a for t in pairs for a in t]
    out_shape = [jax.ShapeDtypeStruct(t[0].shape, F32) for t in by_rows for _ in range(4)] + \
                [jax.ShapeDtypeStruct(t[0].shape, F32) for t in sliced for _ in range(3)] + \
                [jax.ShapeDtypeStruct(t[0].shape, F32) for t in pairs for _ in range(4)]
    outs = pl.pallas_call(body, name="adam_small", out_shape=tuple(out_shape))(*operands)
    res_a = [tuple(outs[4 * t:4 * t + 4]) for t in range(na)]
    o = 4 * na
    res_b = [tuple(outs[o + 3 * t:o + 3 * t + 3]) for t in range(nb)]
    o += 3 * nb
    res_c = [tuple(outs[o + 4 * t:o + 4 * t + 4]) for t in range(nc)]
    return res_a, res_b, res_c


def _sum_into(ppack, dw, rb, off, qv):
    _, rows, D = dw.shape
    tr = 256 if rows % 256 == 0 else 128
    assert rows % tr == 0 and off % tr == 0
    ob = off // tr

    def body(q_ref, o_ref, r_ref, pin_ref, p_ref):
        acc = o_ref[...].astype(F32)
        for k in range(3):
            acc = acc + r_ref[k].astype(F32)
        p_ref[...] = acc

    return pl.pallas_call(
        body, name="sum_partials", out_shape=jax.ShapeDtypeStruct(ppack.shape, ppack.dtype),
        grid_spec=pltpu.PrefetchScalarGridSpec(
            num_scalar_prefetch=1, grid=(rows // tr,),
            in_specs=[pl.BlockSpec((None, tr, D), lambda i, q_ref: (q_ref[0], i, 0)),
                      pl.BlockSpec((3, tr, D), lambda i, q_ref: (0, i, 0)),
                      pl.BlockSpec(memory_space=pl.ANY)],
            out_specs=pl.BlockSpec((tr, D), lambda i, q_ref: (ob + i, 0))),
        input_output_aliases={3: 0},
        compiler_params=_cparams("parallel"),
    )(qv, dw, rb, ppack)


def _wspec(g):
    return _resident(g.shape, lambda i: (0, 0, 0))


def _ffn_fwd_inner(x1, mod_ref, gf_ref, w1_ref, w2_ref, h2_ref, a_ref, z_ref, x2_ref):
    h2 = _rms_fwd(x1, gf_ref[...], mod_ref[4:5, :], mod_ref[3:4, :])[0]
    h2b = h2.astype(BF16)
    h2_ref[...] = h2b
    f4 = w1_ref.shape[2]
    z = jnp.zeros(x1.shape, F32)
    for q in range(NQ):
        a = jnp.maximum(_dot(h2b, w1_ref[q]), 0.0)
        a_ref[:, q * f4:(q + 1) * f4] = a.astype(BF16)
        z = z + _dot((a * a).astype(BF16), w2_ref[q])
    z_ref[...] = z.astype(BF16)
    x2_ref[...] = x1 + mod_ref[5:6, :] * z


def _sigmoid(x):
    return 0.5 + 0.5 * jnp.tanh(0.5 * x)


def _heads_dot(xb, w_ref, hd, nt=False):
    outs = []
    for h in range(HEADS):
        xs = xb[:, h * hd:(h + 1) * hd]
        outs.append(_dot_nt(xs, w_ref[h]) if nt else _dot(xs, w_ref[h]))
    return jnp.concatenate(outs, axis=1)


def _lru_gates(xc, wa_ref, ba, wx_ref, bx, lam, hd):
    xcb = xc.astype(BF16)
    gate_r = _sigmoid(_heads_dot(xcb, wa_ref, hd) + ba)
    gate_i = _sigmoid(_heads_dot(xcb, wx_ref, hd) + bx)
    ls = jax.nn.log_sigmoid(lam)
    log_a = gate_r * (LRU_C * ls)
    a = jnp.exp(log_a)
    mult = jnp.sqrt(_neg_expm1(2.0 * log_a, a * a))
    return xcb, gate_r, gate_i, ls, a, mult


def _conv_taps(xext, cw, tt):
    acc = cw[0:1, :] * xext[pl.ds(8 - (CONV_W - 1), tt), :]
    for k in range(1, CONV_W):
        acc = acc + cw[k:k + 1, :] * xext[pl.ds(8 - (CONV_W - 1) + k, tt), :]
    return acc


def _lru_fwd(x, mod_l, g_mix, g_wy, g_win, b_y, b_in, cw, cb, wa, ba, wx, bx, lam, g_wout, b_out, **comm):
    S, W = x.shape
    tt = min(TT, S)
    hd = W // HEADS

    def body(x_ref, mod_ref, g_ref, wy_ref, win_ref, by_ref, bin_ref, cw_ref, cb_ref, wa_ref, ba_ref, wx_ref, bx_ref,
             lam_ref, wo_ref, bo_ref, h_ref, gb_ref, xr_ref, hs_ref, p_ref, y_ref, x1_ref,
             xc_ref, gr_ref, gi_ref, a_s, mu_ref, xext, u_s, carry):
        i = pl.program_id(0)

        @pl.when(i == 0)
        def _():
            carry[...] = jnp.zeros_like(carry)
            xext[0:8, :] = jnp.zeros((8, W), F32)

        @pl.when(i > 0)
        def _():
            xext[0:8, :] = xext[pl.ds(tt, 8), :]

        xv = x_ref[...]
        hb = _rms_fwd(xv, g_ref[...], mod_ref[1:2, :], mod_ref[0:1, :])[0].astype(BF16)
        h_ref[...] = hb
        gbv = _dot(hb, wy_ref[...].reshape(W, W)) + by_ref[...]
        gb_ref[...] = gbv
        xr = _dot(hb, win_ref[...].reshape(W, W)) + bin_ref[...]
        xr_ref[...] = xr
        xext[pl.ds(8, tt), :] = xr
        xc = _conv_taps(xext, cw_ref[...], tt) + cb_ref[...]
        _, gate_r, gate_i, _, a, mult = _lru_gates(xc, wa_ref, ba_ref[...], wx_ref, bx_ref[...], lam_ref[...], hd)
        xc_ref[...] = xc
        gr_ref[...] = gate_r
        gi_ref[...] = gate_i
        mu_ref[...] = mult
        a_s[...] = a
        u_s[...] = mult * (gate_i * xc)
        row = lax.broadcasted_iota(jnp.int32, (8, W), 0)

        def step(k, _):
            off = pl.multiple_of(k * 8, 8)
            A = a_s[pl.ds(off, 8), :]
            U = u_s[pl.ds(off, 8), :]
            for d in (1, 2, 4):
                keep = row >= d
                Us = jnp.where(keep, pltpu.roll(U, d, 0), 0.0)
                As = jnp.where(keep, pltpu.roll(A, d, 0), 1.0)
                U = U + A * Us
                A = A * As
            H = U + A * carry[...]
            hs_ref[pl.ds(off, 8), :] = H
            carry[...] = jnp.broadcast_to(H[7:8, :], (8, W))
            return 0

        lax.fori_loop(0, tt // 8, step, 0)
        pb = (hs_ref[...] * _gelu(gbv)[0]).astype(BF16)
        p_ref[...] = pb
        y = _dot(pb, wo_ref[...].reshape(W, W)) + bo_ref[...]
        y_ref[...] = y.astype(BF16)
        x1_ref[...] = xv + mod_ref[2:3, :] * y

    tile = pl.BlockSpec((tt, W), lambda i: (i, 0))
    row = pl.BlockSpec((1, W), lambda i: (0, 0))
    wblk = pl.BlockSpec((HEADS, hd, hd), lambda i: (0, 0, 0))
    f32o, bf16o = jax.ShapeDtypeStruct((S, W), F32), jax.ShapeDtypeStruct((S, W), BF16)
    return _pcall(
        body, name="lru_fwd", grid=(S // tt,),
        out_shape=(bf16o, f32o, f32o, f32o, bf16o, bf16o, f32o, f32o, f32o, f32o, f32o, f32o),
        in_specs=[tile, pl.BlockSpec((8, W), lambda i: (0, 0)), row, _wspec(g_wy), _wspec(g_win), row, row,
                  pl.BlockSpec((CONV_W, W), lambda i: (0, 0)), row, wblk, row, wblk, row, row, _wspec(g_wout), row],
        out_specs=(tile,) * 12,
        scratch_shapes=[pltpu.VMEM((tt + 8, W), F32), pltpu.VMEM((tt, W), F32), pltpu.VMEM((8, W), F32)],
        operands=(x, mod_l, g_mix, g_wy, g_win, b_y, b_in, cw, cb, wa, ba, wx, bx, lam, g_wout, b_out), **comm)


def _ffn_out_shapes(S, D, F):
    return (jax.ShapeDtypeStruct((S, D), BF16), jax.ShapeDtypeStruct((S, F), BF16),
            jax.ShapeDtypeStruct((S, D), BF16), jax.ShapeDtypeStruct((S, D), F32))


def _ffn_fwd(x1, mod_l, g_ffn, g_w1, g_w2, **comm):
    S, D = x1.shape
    tm = min(TM, S)
    F = g_w1.shape[2] * NQ

    def body(x1_ref, mod_ref, gf_ref, w1_ref, w2_ref, h2_ref, a_ref, z_ref, x2_ref):
        _ffn_fwd_inner(x1_ref[...], mod_ref, gf_ref, w1_ref, w2_ref, h2_ref, a_ref, z_ref, x2_ref)

    tile = pl.BlockSpec((tm, D), lambda i: (i, 0))
    row = pl.BlockSpec((1, D), lambda i: (0, 0))
    return _pcall(
        body, name="ffn_fwd", grid=(S // tm,),
        out_shape=_ffn_out_shapes(S, D, F),
        in_specs=[tile, pl.BlockSpec((8, D), lambda i: (0, 0)), row, _wspec(g_w1), _wspec(g_w2)],
        out_specs=(tile, pl.BlockSpec((tm, F), lambda i: (i, 0)), tile, tile),
        operands=(x1, mod_l, g_ffn, g_w1, g_w2), **comm)


def _window_vec(D):
    gd = D // len(POOL_WINDOWS)
    lane = lax.broadcasted_iota(jnp.int32, (1, D), 1)
    w = jnp.full((1, D), float(POOL_WINDOWS[0]), F32)
    for g in range(1, len(POOL_WINDOWS)):
        w = jnp.where(lane >= g * gd, float(POOL_WINDOWS[g]), w)
    return w


def _pool_mix_ffn_fwd(x, mod_l, g_mix, pw, ps, g_ffn, g_w1, g_w2, **comm):
    S, D = x.shape
    tm = min(TP, S)
    F = g_w1.shape[2] * NQ
    gd = D // len(POOL_WINDOWS)
    n = tm + 24

    def body(x_ref, xh_ref, mod_ref, gm_ref, pw_ref, ps_ref, gf_ref, w1_ref, w2_ref,
             pl_ref, x1_ref, h2_ref, a_ref, z_ref, x2_ref, ext, b1, b2):
        i = pl.program_id(0)
        g, sc, sh = gm_ref[...], mod_ref[1:2, :], mod_ref[0:1, :]
        xv = x_ref[...]
        h = _rms_fwd(xv, g, sc, sh)[0]
        hh = _rms_fwd(xh_ref[...], g, sc, sh)[0]
        zeros8 = jnp.zeros((8, D), F32)
        ext[0:8, :] = zeros8
        b1[0:8, :] = zeros8
        b2[0:8, :] = zeros8
        ext[8:24, :] = jnp.where(i > 0, hh, 0.0)
        ext[pl.ds(24, tm), :] = h
        m = n - 8
        b1[pl.ds(8, m), :] = ext[pl.ds(8, m), :] + ext[pl.ds(7, m), :]
        b2[pl.ds(8, m), gd:] = b1[pl.ds(8, m), gd:] + b1[pl.ds(6, m), gd:]
        b1[pl.ds(8, m), 2 * gd:] = b2[pl.ds(8, m), 2 * gd:] + b2[pl.ds(4, m), 2 * gd:]
        b2[pl.ds(8, m), 3 * gd:] = b1[pl.ds(8, m), 3 * gd:] + b1[pl.ds(0, m), 3 * gd:]
        wsum = jnp.concatenate([b1[pl.ds(24, tm), 0:gd], b2[pl.ds(24, tm), gd:2 * gd],
                                b1[pl.ds(24, tm), 2 * gd:3 * gd], b2[pl.ds(24, tm), 3 * gd:]], axis=1)
        t1 = (lax.broadcasted_iota(jnp.int32, (tm, 1), 0) + (i * tm + 1)).astype(F32)
        cnt = jnp.minimum(t1, _window_vec(D))
        pooled = (wsum / cnt - h).astype(BF16)
        pl_ref[...] = pooled
        y = _heads_dot(pooled, pw_ref, gd) * ps_ref[...]
        x1 = xv + mod_ref[2:3, :] * y
        x1_ref[...] = x1
        _ffn_fwd_inner(x1, mod_ref, gf_ref, w1_ref, w2_ref, h2_ref, a_ref, z_ref, x2_ref)

    tile = pl.BlockSpec((tm, D), lambda i: (i, 0))
    halo = pl.BlockSpec((16, D), lambda i: (jnp.maximum(i * (tm // 16) - 1, 0), 0))
    row = pl.BlockSpec((1, D), lambda i: (0, 0))
    return _pcall(
        body, name="pool_mix_ffn_fwd", grid=(S // tm,),
        out_shape=(jax.ShapeDtypeStruct((S, D), BF16), jax.ShapeDtypeStruct((S, D), F32)) + _ffn_out_shapes(S, D, F),
        in_specs=[tile, halo, pl.BlockSpec((8, D), lambda i: (0, 0)), row,
                  pl.BlockSpec((len(POOL_WINDOWS), gd, gd), lambda i: (0, 0, 0)), row, row,
                  _wspec(g_w1), _wspec(g_w2)],
        out_specs=(tile, tile, tile, pl.BlockSpec((tm, F), lambda i: (i, 0)), tile, tile),
        scratch_shapes=[pltpu.VMEM((n, D), F32), pltpu.VMEM((n, D), F32), pltpu.VMEM((n, D), F32)],
        operands=(x, x, mod_l, g_mix, pw, ps, g_ffn, g_w1, g_w2), **comm)


def _loss_head(xv, gv, tv, acc_ref):
    D = xv.shape[1]
    r = lax.rsqrt(jnp.mean(xv * xv, axis=-1, keepdims=True) + EPS)
    xhat = xv * r
    err = xhat * gv - tv
    acc_ref[0:1, :] += jnp.sum(err * err, axis=0, keepdims=True)
    dy = err * (1.0 / D)
    acc_ref[1:2, :] += jnp.sum(dy * xhat, axis=0, keepdims=True)
    dxh = dy * gv
    return r * (dxh - xhat * jnp.mean(dxh * xhat, axis=-1, keepdims=True))


def _ffn_bwd(dx2, x1, a, z, mod_l, g_ffn, g_w1, g_w2, head=None, **comm):
    S, D = dx2.shape
    F = a.shape[1]
    f4 = F // NQ
    tm = min(TM, S)
    nh = 2 if head else 0

    def body(*refs):
        dx2_ref, x1_ref, a_ref, z_ref, mod_ref, gf_ref, w1_ref, w2_ref = refs[:8]
        dx1_ref, du_ref, dz_ref, acc_ref = refs[8 + nh:]
        _zero_first(acc_ref)
        dx2v = dx2_ref[...]
        if head:
            dx2v = _loss_head(dx2v, refs[8][...], refs[9][...], acc_ref)
        acc_ref[5:6, :] +=jnp.sum(dx2v * z_ref[...].astype(F32), axis=0, keepdims=True)
        dzb = (dx2v * mod_ref[5:6, :]).astype(BF16)
        dz_ref[...] = dzb
        dh2 = jnp.zeros((tm, D), F32)
        for q in range(NQ):
            av = a_ref[:, q * f4:(q + 1) * f4].astype(F32)
            du = (_dot_nt(dzb, w2_ref[q]) * (2.0 * av)).astype(BF16)
            du_ref[:, q * f4:(q + 1) * f4] = du
            dh2 = dh2 + _dot_nt(du, w1_ref[q])
        g, sc = gf_ref[...], mod_ref[4:5, :]
        _, xhat, r, n = _rms_fwd(x1_ref[...], g, sc, mod_ref[3:4, :])
        dx, dsh, dsc, dg = _rms_bwd(dh2, xhat, r, n, g, sc)
        acc_ref[3:4, :] += dsh
        acc_ref[4:5, :] += dsc
        acc_ref[7:8, :] += dg
        dx1_ref[...] = dx2v + dx

    tile = pl.BlockSpec((tm, D), lambda i: (i, 0))
    wide = pl.BlockSpec((tm, F), lambda i: (i, 0))
    return _pcall(
        body, name="ffn_bwd", grid=(S // tm,),
        out_shape=(jax.ShapeDtypeStruct((S, D), F32), jax.ShapeDtypeStruct((S, F), BF16),
                   jax.ShapeDtypeStruct((S, D), BF16), jax.ShapeDtypeStruct((8, D), F32)),
        in_specs=[tile, tile, wide, tile, pl.BlockSpec((8, D), lambda i: (0, 0)), pl.BlockSpec((1, D), lambda i: (0, 0)),
                  _wspec(g_w1), _wspec(g_w2)] + ([pl.BlockSpec((1, D), lambda i: (0, 0)), tile] if head else []),
        out_specs=(tile, wide, tile, pl.BlockSpec((8, D), lambda i: (0, 0))),
        operands=(dx2, x1, a, z, mod_l, g_ffn, g_w1, g_w2) + (tuple(head) if head else ()), **comm)


def _dw_blocked(a, b, by_rows, square_a, name):
    S = a.shape[0]
    tk = min(TK, S)
    nk = S // tk
    if by_rows:
        bm, bn = a.shape[1] // NQ, b.shape[1]
        a_map, b_map = (lambda q, k: (k, q)), (lambda q, k: (k, 0))
    else:
        bm, bn = a.shape[1], b.shape[1] // NQ
        a_map, b_map = (lambda q, k: (k, 0)), (lambda q, k: (k, q))

    def body(a_ref, b_ref, o_ref, acc):
        k = pl.program_id(1)

        @pl.when(k == 0)
        def _():
            acc[...] = jnp.zeros_like(acc)

        av = a_ref[...]
        if square_a:
            av = av * av
        acc[...] += _dot_tn(av, b_ref[...])

        @pl.when(k == nk - 1)
        def _():
            o_ref[...] = acc[...].astype(o_ref.dtype)

    return pl.pallas_call(
        body, name=name, grid=(NQ, nk),
        out_shape=jax.ShapeDtypeStruct((NQ, bm, bn), BF16),
        in_specs=[pl.BlockSpec((tk, bm), a_map), pl.BlockSpec((tk, bn), b_map)],
        out_specs=pl.BlockSpec((None, bm, bn), lambda q, k: (q, 0, 0)),
        scratch_shapes=[pltpu.VMEM((bm, bn), F32)],
        compiler_params=_cparams("parallel", "arbitrary"),
    )(a, b)


def _dw_whole(a, bs, name, **comm):
    S, M = a.shape
    N = bs[0].shape[1]
    tk = min(TK, S)
    nk = S // tk
    nb = len(bs)

    def body(*refs):
        a_ref, b_refs, o_refs, accs = refs[0], refs[1:1 + nb], refs[1 + nb:1 + 2 * nb], refs[1 + 2 * nb:]
        k = pl.program_id(0)

        @pl.when(k == 0)
        def _():
            for acc in accs:
                acc[...] = jnp.zeros_like(acc)

        av = a_ref[...]
        for b_ref, acc in zip(b_refs, accs):
            acc[...] += _dot_tn(av, b_ref[...])

        @pl.when(k == nk - 1)
        def _():
            for o_ref, acc in zip(o_refs, accs):
                o_ref[...] = acc[...].reshape(NQ, M // NQ, N).astype(o_ref.dtype)

    return _pcall(
        body, name=name, grid=(nk,),
        out_shape=tuple(jax.ShapeDtypeStruct((NQ, M // NQ, N), BF16) for _ in bs),
        in_specs=[pl.BlockSpec((tk, M), lambda k: (k, 0))] + [pl.BlockSpec((tk, N), lambda k: (k, 0)) for _ in bs],
        out_specs=tuple(pl.BlockSpec((NQ, M // NQ, N), lambda k: (0, 0, 0)) for _ in bs),
        scratch_shapes=[pltpu.VMEM((M, N), F32) for _ in bs],
        operands=(a, *bs), **comm)


def _lru_bwd(dx1, y, x, xr0, gb, hs, xc_, gate_r_, gate_i_, a_, mult_, mod_l, g_mix, g_wout, g_wy, g_win, cw, wa, wx,
             lam, **comm):
    S, W = xr0.shape
    tt = min(TT, S)
    nb = S // tt
    hd = W // HEADS

    def body(dx1_ref, y_ref, x_ref, xr_ref, gb_ref, hs_ref, hsh_ref, xc_ref, gr_ref, gi_ref, a_s, mu_ref,
             mod_ref, gm_ref, wo_ref, wy_ref, win_ref, cw_ref, wa_ref, wx_ref, lam_ref,
             dy_ref, dgb_ref, dxr_ref, dx_ref, sm_ref, dwa_ref, dwx_ref, acc_ref,
             hext, qext, dext, b_s, qc, dc):
        i = pl.program_id(0)
        blk = nb - 1 - i

        @pl.when(i == 0)
        def _():
            sm_ref[...] = jnp.zeros_like(sm_ref)
            dwa_ref[...] = jnp.zeros_like(dwa_ref)
            dwx_ref[...] = jnp.zeros_like(dwx_ref)
            acc_ref[...] = jnp.zeros_like(acc_ref)
            qc[...] = jnp.zeros_like(qc)
            dc[...] = jnp.zeros_like(dc)

        dx1v = dx1_ref[...]
        acc_ref[2:3, :] += jnp.sum(dx1v * y_ref[...].astype(F32), axis=0, keepdims=True)
        dy = dx1v * mod_ref[2:3, :]
        acc_ref[3:4, :] += jnp.sum(dy, axis=0, keepdims=True)
        dyb = dy.astype(BF16)
        dy_ref[...] = dyb
        dpv = _dot_nt(dyb, wo_ref[...].reshape(W, W))

        hext[0:8, :] = jnp.where(blk > 0, hsh_ref[...], 0.0)
        hext[pl.ds(8, tt), :] = hs_ref[...]
        cw = cw_ref[...]
        lam = lam_ref[...]
        xc, gate_r, gate_i, a, mult = xc_ref[...], gr_ref[...], gi_ref[...], a_s[...], mu_ref[...]
        xcb = xc.astype(BF16)
        ls = jax.nn.log_sigmoid(lam)

        gbv = gb_ref[...]
        gate, th = _gelu(gbv)
        dgb = dpv * hs_ref[...] * _gelu_grad(gbv, th)
        dgbb = dgb.astype(BF16)
        dgb_ref[...] = dgbb
        sm_ref[9:10, :] += jnp.sum(dgb, axis=0, keepdims=True)
        dhs = dpv * gate

        b_s[...] = a * dhs
        qext[pl.ds(tt, 8), :] = qc[...]
        row = lax.broadcasted_iota(jnp.int32, (8, W), 0)

        def step(k, _):
            off = pl.multiple_of((tt // 8 - 1 - k) * 8, 8)
            A = a_s[pl.ds(off, 8), :]
            B = b_s[pl.ds(off, 8), :]
            for d in (1, 2, 4):
                keep = row < 8 - d
                Bs = jnp.where(keep, pltpu.roll(B, 8 - d, 0), 0.0)
                As = jnp.where(keep, pltpu.roll(A, 8 - d, 0), 1.0)
                B = B + A * Bs
                A = A * As
            Q = B + A * qc[...]
            qext[pl.ds(off, 8), :] = Q
            qc[...] = jnp.broadcast_to(Q[0:1, :], (8, W))
            return 0

        lax.fori_loop(0, tt // 8, step, 0)
        gsc = dhs + qext[pl.ds(1, tt), :]
        da = gsc * hext[pl.ds(7, tt), :]
        t1 = gsc * xc
        dmult = t1 * gate_i
        dgate_i = t1 * mult
        dxc = gsc * (mult * gate_i)
        dlog_a = da * a - dmult * (a * a) / mult
        dgate_r = dlog_a * (LRU_C * ls)
        sm_ref[7:8, :] += jnp.sum(dlog_a * (LRU_C * gate_r), axis=0, keepdims=True)
        dga = dgate_r * gate_r * (1.0 - gate_r)
        dgx = dgate_i * gate_i * (1.0 - gate_i)
        sm_ref[5:6, :] += jnp.sum(dga, axis=0, keepdims=True)
        sm_ref[6:7, :] += jnp.sum(dgx, axis=0, keepdims=True)
        dgab = dga.astype(BF16)
        dgxb = dgx.astype(BF16)
        dxc = dxc + _heads_dot(dgab, wa_ref, hd, nt=True) + _heads_dot(dgxb, wx_ref, hd, nt=True)
        for h in range(HEADS):
            sl = slice(h * hd, (h + 1) * hd)
            dwa_ref[h] += _dot_tn(xcb[:, sl], dgab[:, sl])
            dwx_ref[h] += _dot_tn(xcb[:, sl], dgxb[:, sl])
        sm_ref[4:5, :] += jnp.sum(dxc, axis=0, keepdims=True)
        dext[pl.ds(0, tt), :] = dxc
        dext[pl.ds(tt, 8), :] = dc[...]
        xrv = xr_ref[...]
        dxr = None
        for k in range(CONV_W):
            up = dext[pl.ds(CONV_W - 1 - k, tt), :]
            sm_ref[k:k + 1, :] += jnp.sum(up * xrv, axis=0, keepdims=True)
            dxr = cw[k:k + 1, :] * up if dxr is None else dxr + cw[k:k + 1, :] * up
        dc[...] = dext[0:8, :]
        sm_ref[8:9, :] += jnp.sum(dxr, axis=0, keepdims=True)
        dxrb = dxr.astype(BF16)
        dxr_ref[...] = dxrb

        dh = _dot_nt(dxrb, win_ref[...].reshape(W, W)) + _dot_nt(dgbb, wy_ref[...].reshape(W, W))
        g, sc = gm_ref[...], mod_ref[1:2, :]
        _, xhat, r, n = _rms_fwd(x_ref[...], g, sc, mod_ref[0:1, :])
        dx, dsh, dsc, dg = _rms_bwd(dh, xhat, r, n, g, sc)
        acc_ref[0:1, :] += dsh
        acc_ref[1:2, :] += dsc
        acc_ref[6:7, :] += dg
        dx_ref[...] = dx1v + dx

        @pl.when(i == nb - 1)
        def _():
            sm_ref[7:8, :] = sm_ref[7:8, :] * jax.nn.sigmoid(-lam)

    rev = lambda i: (nb - 1 - i, 0)
    tile = pl.BlockSpec((tt, W), rev)
    halo = pl.BlockSpec((8, W), lambda i: (jnp.maximum((nb - 1 - i) * (tt // 8) - 1, 0), 0))
    row = pl.BlockSpec((1, W), lambda i: (0, 0))
    wblk = pl.BlockSpec((HEADS, hd, hd), lambda i: (0, 0, 0))
    bf16o = jax.ShapeDtypeStruct((S, W), BF16)
    return _pcall(
        body, name="lru_bwd", grid=(nb,),
        out_shape=(bf16o, bf16o, bf16o, jax.ShapeDtypeStruct((S, W), F32),
                   jax.ShapeDtypeStruct((16, W), F32), jax.ShapeDtypeStruct((HEADS, hd, hd), F32),
                   jax.ShapeDtypeStruct((HEADS, hd, hd), F32), jax.ShapeDtypeStruct((8, W), F32)),
        in_specs=[tile, tile, tile, tile, tile, tile, halo, tile, tile, tile, tile, tile,
                  pl.BlockSpec((8, W), lambda i: (0, 0)), row,
                  _wspec(g_wout), _wspec(g_wy), _wspec(g_win), pl.BlockSpec((CONV_W, W), lambda i: (0, 0)),
                  wblk, wblk, row],
        out_specs=(tile, tile, tile, tile, pl.BlockSpec((16, W), lambda i: (0, 0)), wblk, wblk,
                   pl.BlockSpec((8, W), lambda i: (0, 0))),
        scratch_shapes=[pltpu.VMEM((tt + 8, W), F32), pltpu.VMEM((tt + 8, W), F32), pltpu.VMEM((tt + 8, W), F32),
                        pltpu.VMEM((tt, W), F32), pltpu.VMEM((8, W), F32), pltpu.VMEM((8, W), F32)],
        operands=(dx1, y, x, xr0, gb, hs, hs, xc_, gate_r_, gate_i_, a_, mult_, mod_l, g_mix, g_wout, g_wy, g_win,
                  cw, wa, wx, lam),
        **comm)


def _pool_bwd(dx1, x, pooled, mod_l, g_mix, pw, ps, h2, du, a, dz):
    S, D = x.shape
    tm = min(TP, S)
    nb = S // tm
    ng = len(POOL_WINDOWS)
    gd = D // ng
    n = tm + 24
    f4 = du.shape[1] // NQ
    assert nb % NQ == 0
    kch = nb // NQ
    kr = S // kch

    def body(dx1_ref, dxh_ref, x_ref, pl_ref, mod_ref, gm_ref, pw_ref, ps_ref, h2_ref, du_ref, a_ref, dz_ref,
             dx_ref, acc_ref, dpw_ref, dw1_ref, dw2_ref, ext, b1, b2, acc1, acc2):
        i = pl.program_id(0)

        @pl.when(i == 0)
        def _():
            acc_ref[...] = jnp.zeros_like(acc_ref)
            dpw_ref[...] = jnp.zeros_like(dpw_ref)

        @pl.when(i % kch == 0)
        def _():
            acc1[...] = jnp.zeros_like(acc1)
            acc2[...] = jnp.zeros_like(acc2)

        acc1[...] += _dot_tn(h2_ref[...], du_ref[...])
        av = a_ref[...]
        acc2[...] += _dot_tn(av * av, dz_ref[...])

        gt, psv = mod_ref[2:3, :], ps_ref[...]
        wvec = _window_vec(D)
        dx1v = dx1_ref[...]
        pooled = pl_ref[...]
        mixed = _heads_dot(pooled, pw_ref, gd)
        acc_ref[2:3, :] += jnp.sum(dx1v * (mixed * psv), axis=0, keepdims=True)
        dy = dx1v * gt
        acc_ref[3:4, :] += jnp.sum(dy * mixed, axis=0, keepdims=True)
        dmix = (dy * psv).astype(BF16)
        for gi in range(ng):
            sl = slice(gi * gd, (gi + 1) * gd)
            dpw_ref[gi] += _dot_tn(pooled[:, sl], dmix[:, sl])
        dpooled = _heads_dot(dmix, pw_ref, gd, nt=True)
        dmix_h = (dxh_ref[...] * gt * psv).astype(BF16)
        dpooled_h = jnp.where(i < nb - 1, _heads_dot(dmix_h, pw_ref, gd, nt=True), 0.0)
        t1 = (lax.broadcasted_iota(jnp.int32, (tm, 1), 0) + (i * tm + 1)).astype(F32)
        t1h = (lax.broadcasted_iota(jnp.int32, (16, 1), 0) + ((i + 1) * tm + 1)).astype(F32)
        zeros8 = jnp.zeros((8, D), F32)
        ext[pl.ds(0, tm), :] = dpooled / jnp.minimum(t1, wvec)
        ext[pl.ds(tm, 16), :] = dpooled_h / jnp.minimum(t1h, wvec)
        ext[pl.ds(tm + 16, 8), :] = zeros8
        b1[pl.ds(tm + 16, 8), :] = zeros8
        b2[pl.ds(tm + 16, 8), :] = zeros8
        m = n - 8
        b1[pl.ds(0, m), :] = ext[pl.ds(0, m), :] + ext[pl.ds(1, m), :]
        b2[pl.ds(0, m), gd:] = b1[pl.ds(0, m), gd:] + b1[pl.ds(2, m), gd:]
        b1[pl.ds(0, m), 2 * gd:] = b2[pl.ds(0, m), 2 * gd:] + b2[pl.ds(4, m), 2 * gd:]
        b2[pl.ds(0, m), 3 * gd:] = b1[pl.ds(0, m), 3 * gd:] + b1[pl.ds(8, m), 3 * gd:]
        wsum = jnp.concatenate([b1[pl.ds(0, tm), 0:gd], b2[pl.ds(0, tm), gd:2 * gd],
                                b1[pl.ds(0, tm), 2 * gd:3 * gd], b2[pl.ds(0, tm), 3 * gd:]], axis=1)
        dh = wsum - dpooled
        g, sc = gm_ref[...], mod_ref[1:2, :]
        _, xhat, r, nn = _rms_fwd(x_ref[...], g, sc, mod_ref[0:1, :])
        dx, dsh, dsc, dg = _rms_bwd(dh, xhat, r, nn, g, sc)
        acc_ref[0:1, :] += dsh
        acc_ref[1:2, :] += dsc
        acc_ref[6:7, :] += dg
        dx_ref[...] = dx1v + dx

        @pl.when(i % kch == kch - 1)
        def _():
            dw1_ref[...] = acc1[...].astype(BF16)
            dw2_ref[...] = acc2[...].astype(BF16)

    tile = pl.BlockSpec((tm, D), lambda i: (i, 0))
    halo = pl.BlockSpec((16, D), lambda i: (jnp.minimum((i + 1) * (tm // 16), S // 16 - 1), 0))
    row = pl.BlockSpec((1, D), lambda i: (0, 0))
    wblk = pl.BlockSpec((ng, gd, gd), lambda i: (0, 0, 0))
    full_k = pl.BlockSpec((kr, D), lambda i: (i % kch, 0))
    part_k = pl.BlockSpec((kr, f4), lambda i: (i % kch, i // kch))
    return pl.pallas_call(
        body, name="pool_bwd", grid=(nb,),
        out_shape=(jax.ShapeDtypeStruct((S, D), F32), jax.ShapeDtypeStruct((8, D), F32),
                   jax.ShapeDtypeStruct((ng, gd, gd), F32), jax.ShapeDtypeStruct((NQ, D, f4), BF16),
                   jax.ShapeDtypeStruct((NQ, f4, D), BF16)),
        in_specs=[tile, halo, tile, tile, pl.BlockSpec((8, D), lambda i: (0, 0)), row, wblk, row,
                  full_k, part_k, part_k, full_k],
        out_specs=(tile, pl.BlockSpec((8, D), lambda i: (0, 0)), wblk,
                   pl.BlockSpec((None, D, f4), lambda i: (i // kch, 0, 0)),
                   pl.BlockSpec((None, f4, D), lambda i: (i // kch, 0, 0))),
        scratch_shapes=[pltpu.VMEM((n, D), F32), pltpu.VMEM((n, D), F32), pltpu.VMEM((n, D), F32),
                        pltpu.VMEM((D, f4), F32), pltpu.VMEM((f4, D), F32)],
        compiler_params=_cparams("arbitrary"),
    )(dx1, dx1, x, pooled, mod_l, g_mix, pw, ps, h2, du, a, dz)


def _shard_to_rows(w, D):
    return w.reshape(-1, D)


def _blockdiag_full(gq, na, hd):
    return gq.reshape(NQ, na, HEADS, hd // NQ, hd).transpose(1, 2, 0, 3, 4).reshape(na, HEADS, hd, hd)


def _blockdiag_by_chip(dw, D):
    na, _, hd, _ = dw.shape
    return dw.reshape(na, HEADS, NQ, hd // NQ, hd).transpose(2, 0, 1, 3, 4).reshape(NQ, -1, D)


def kernel(x, c, w_mod, b_mod, norm_mix_g, norm_ffn_g, lru_w_y, lru_b_y, lru_w_in, lru_b_in, lru_conv_w, lru_conv_b, lru_w_a, lru_b_a, lru_w_x, lru_b_x, lru_lambda, lru_w_out, lru_b_out, pool_w, pool_scale, ffn_w1, ffn_w2, final_norm_g, loss_target, m_w_mod, m_b_mod, m_norm_mix_g, m_norm_ffn_g, m_lru_w_y, m_lru_b_y, m_lru_w_in, m_lru_b_in, m_lru_conv_w, m_lru_conv_b, m_lru_w_a, m_lru_b_a, m_lru_w_x, m_lru_b_x, m_lru_lambda, m_lru_w_out, m_lru_b_out, m_pool_w, m_pool_scale, m_ffn_w1, m_ffn_w2, m_final_norm_g, v_w_mod, v_b_mod, v_norm_mix_g, v_norm_ffn_g, v_lru_w_y, v_lru_b_y, v_lru_w_in, v_lru_b_in, v_lru_conv_w, v_lru_conv_b, v_lru_w_a, v_lru_b_a, v_lru_w_x, v_lru_b_x, v_lru_lambda, v_lru_w_out, v_lru_b_out, v_pool_w, v_pool_scale, v_ffn_w1, v_ffn_w2, v_final_norm_g):
    S, D = x.shape[1], x.shape[2]
    L = w_mod.shape[0]
    NA = lru_w_y.shape[0]
    NB = pool_w.shape[0]
    F = ffn_w1.shape[2] * NQ
    f4 = F // NQ
    hd = D // HEADS
    Cs = w_mod.shape[2]
    assert L == DEPTH and Cs * NQ == N_MOD * D and D % 1024 == 0
    x2d = x.reshape(S, D)
    tgt = loss_target.reshape(S, D)
    q = 2 * lax.axis_index("x") + lax.axis_index("y")

    big = [ffn_w1, ffn_w2, lru_w_y, lru_w_in, lru_w_out, lru_w_a, lru_w_x, pool_w]
    rows = [int(w.size) // D for w in big]
    offs = [sum(rows[:k]) for k in range(len(big))]
    O_W1, O_W2, O_WY, O_WIN, O_WOUT, O_WA, O_WX, O_PW = offs
    R = sum(rows)
    dq = D // NQ
    s_w1 = [ffn_w1[i].astype(BF16) for i in range(L)]
    s_w2 = [ffn_w2[i].astype(BF16) for i in range(L)]
    s_wy = [lru_w_y[j].astype(BF16) for j in range(NA)]
    s_win = [lru_w_in[j].astype(BF16) for j in range(NA)]
    s_wout = [lru_w_out[j].astype(BF16) for j in range(NA)]
    s_tiny = jnp.concatenate([_shard_to_rows(w, D) for w in (lru_w_a, lru_w_x, pool_w)], axis=0).astype(BF16)

    cshard = lru_conv_w.reshape(-1)
    small_fwd = jnp.concatenate([c.reshape(-1), cshard, lru_b_a.reshape(-1), lru_b_x.reshape(-1),
                                 pool_scale.reshape(-1)])
    small_fwd = jnp.pad(small_fwd, (0, 8 * D - small_fwd.shape[0])).reshape(8, D)

    g_w1, g_w2 = [None] * L, [None] * L
    g_wy, g_win, g_wout = [None] * NA, [None] * NA, [None] * NA
    SG, g_wy[0], g_win[0], g_wout[0], g_tiny = _comm_only("gather_first", small=small_fwd,
                                                         gathers=(s_wy[0], s_win[0], s_wout[0], s_tiny))
    wa_full = _blockdiag_full(g_tiny[:, :rows[5]], NA, hd)
    wx_full = _blockdiag_full(g_tiny[:, rows[5]:rows[5] + rows[6]], NA, hd)
    pw_full = _blockdiag_full(g_tiny[:, rows[5] + rows[6]:], NB, hd)
    SGf = SG.reshape(NDEV, 8 * D)
    c_all = SGf[:, :D]
    SGq = SGf.reshape(NQ, 2, 8 * D)[:, 0]
    o = D
    n_cw = NA * CONV_W * D // NQ
    conv_w_full = SGq[:, o:o + n_cw].reshape(NQ, NA, CONV_W, D // NQ).transpose(1, 2, 0, 3).reshape(NA, CONV_W, D)
    o += n_cw
    n_b = NA * HEADS * hd // NQ
    b_a_full = SGq[:, o:o + n_b].reshape(NQ, NA, HEADS, hd // NQ).transpose(1, 2, 0, 3).reshape(NA, 1, D)
    o += n_b
    b_x_full = SGq[:, o:o + n_b].reshape(NQ, NA, HEADS, hd // NQ).transpose(1, 2, 0, 3).reshape(NA, 1, D)
    o += n_b
    n_ps = NB * D // NQ
    pool_scale_full = SGq[:, o:o + n_ps].reshape(NQ, NB, D // NQ).transpose(1, 0, 2).reshape(NB, 1, D)


    b_mod_sh = lax.dynamic_slice_in_dim(b_mod, q * Cs, Cs, axis=1).reshape(L, 1, Cs)
    modpart = _mod_part(c_all, w_mod, b_mod_sh)
    modq = _exchange_mod(modpart.transpose(1, 0, 2))
    mod = modq.transpose(1, 0, 2).reshape(L, N_MOD, D)
    mod = jnp.pad(mod, ((0, 0), (0, 8 - N_MOD), (0, 0)))

    saved = []
    xcur = x2d
    for i in range(L):
        j = i // 2
        gm = norm_mix_g[i].reshape(1, D)
        gf = norm_ffn_g[i].reshape(1, D)
        if i % 2 == 0:
            h, gb, xr0, hs, p, y, x1, xc_s, gr_s, gi_s, a_sv, mu_s, g_w1[i], g_w2[i] = _lru_fwd(
                xcur, mod[i], gm, g_wy[j], g_win[j], lru_b_y[j].reshape(1, D), lru_b_in[j].reshape(1, D),
                conv_w_full[j], lru_conv_b[j].reshape(1, D), wa_full[j], b_a_full[j], wx_full[j], b_x_full[j],
                lru_lambda[j].reshape(1, D), g_wout[j], lru_b_out[j].reshape(1, D), gathers=(s_w1[i], s_w2[i]))
            h2, a, z, x2, g_w1[i + 1], g_w2[i + 1] = _ffn_fwd(x1, mod[i], gf, g_w1[i], g_w2[i],
                                                              gathers=(s_w1[i + 1], s_w2[i + 1]))
            saved.append(dict(x=xcur, h=h, gb=gb, xr0=xr0, hs=hs, p=p, y=y, x1=x1, h2=h2, a=a, z=z,
                              lru=(xc_s, gr_s, gi_s, a_sv, mu_s)))
        else:
            if j + 1 < NA:
                pooled, x1, h2, a, z, x2, g_wy[j + 1], g_win[j + 1], g_wout[j + 1] = _pool_mix_ffn_fwd(
                    xcur, mod[i], gm, pw_full[j], pool_scale_full[j], gf, g_w1[i], g_w2[i],
                    gathers=(s_wy[j + 1], s_win[j + 1], s_wout[j + 1]))
            else:
                pooled, x1, h2, a, z, x2 = _pool_mix_ffn_fwd(xcur, mod[i], gm, pw_full[j], pool_scale_full[j], gf,
                                                             g_w1[i], g_w2[i])
            saved.append(dict(x=xcur, pooled=pooled, x1=x1, h2=h2, a=a, z=z))
        xcur = x2

    dx = xcur
    qv = q.reshape(1).astype(jnp.int32)
    ppack = lax.empty((R, D), F32)
    psib = lax.empty((R, D), F32)
    pending, summed = [], []

    def comm_args():
        kw = {}
        if pending:
            kw["scatters"] = tuple(dw for dw, _ in pending)
        if summed:
            kw["sib"] = (ppack, psib, tuple(summed))
        return kw

    def after_host(extra):
        nonlocal ppack, psib, pending, summed
        had_sib = bool(summed)
        summed = []
        for (dw, off), rb in zip(pending, extra[:len(pending)]):
            ppack = _sum_into(ppack, dw, rb, off, qv)
            summed.append((off, dw.shape[1]))
        if had_sib:
            psib = extra[len(pending)]
        pending = []

    dmod_rows = [None] * L
    dg_mix = [None] * L
    dg_ffn = [None] * L
    d_small = {}
    dwa_l, dwx_l, dpw_l = [None] * NA, [None] * NA, [None] * NB
    for i in reversed(range(L)):
        j = i // 2
        sv = saved[i]
        gm = norm_mix_g[i].reshape(1, D)
        gf = norm_ffn_g[i].reshape(1, D)
        head = (final_norm_g.reshape(1, D), tgt) if i == L - 1 else None
        outs = _ffn_bwd(dx, sv["x1"], sv["a"], sv["z"], mod[i], gf, g_w1[i], g_w2[i], head=head, **comm_args())
        dx1, du, dz, facc = outs[:4]
        after_host(outs[4:])
        if head:
            loss = lax.psum(0.5 * jnp.sum(facc[0]) / D, ("x", "y", "c"))
            d_final_g = facc[1]
        if i % 2 == 0:
            pending.append((_dw_blocked(sv["h2"], du, False, False, "dw1"), O_W1 + i * D))
            pending.append((_dw_blocked(sv["a"], dz, True, True, "dw2"), O_W2 + i * f4))
            outs = _lru_bwd(dx1, sv["y"], sv["x"], sv["xr0"], sv["gb"], sv["hs"], *sv["lru"], mod[i], gm, g_wout[j],
                            g_wy[j], g_win[j], conv_w_full[j], wa_full[j], wx_full[j], lru_lambda[j].reshape(1, D),
                            **comm_args())
            dyp, dgb, dxr, dx, sm, dwa, dwx, macc = outs[:8]
            after_host(outs[8:])
            dwa_l[j], dwx_l[j] = dwa, dwx
            if i == 0:
                tiny = jnp.concatenate([_blockdiag_by_chip(jnp.stack(dwa_l), D), _blockdiag_by_chip(jnp.stack(dwx_l), D),
                                        _blockdiag_by_chip(jnp.stack(dpw_l), D)], axis=1).astype(BF16)
                pending.append((tiny, O_WA))
            outs = _dw_whole(sv["p"], [dyp], "dwout", **comm_args())
            after_host(outs[1:])
            pending.append((outs[0], O_WOUT + j * dq))
            outs = _dw_whole(sv["h"], [dgb, dxr], "dwy_dwin", **comm_args())
            after_host(outs[2:])
            pending.append((outs[0], O_WY + j * dq))
            pending.append((outs[1], O_WIN + j * dq))
            d_small[("lru", j)] = (sm, macc[3])
            dgt_m = macc[2]
        else:
            dx, macc, dpw, dw1, dw2 = _pool_bwd(dx1, sv["x"], sv["pooled"], mod[i], gm, pw_full[j], pool_scale_full[j],
                                                sv["h2"], du, sv["a"], dz)
            pending.append((dw1, O_W1 + i * D))
            pending.append((dw2, O_W2 + i * f4))
            dpw_l[j] = dpw
            d_small[("pool", j)] = macc[3]
            dgt_m = macc[2]
        dmod_rows[i] = jnp.stack([macc[0], macc[1], dgt_m, facc[3], facc[4], facc[5]])
        dg_mix[i] = macc[6]
        dg_ffn[i] = facc[7]
    grad_x = dx.reshape(x.shape)

    lru_sm = [d_small[("lru", j)] for j in range(NA)]
    small_rows = [jnp.stack(dmod_rows).reshape(L * N_MOD, D), jnp.stack(dg_mix), jnp.stack(dg_ffn),
                  jnp.stack([s[0][9] for s in lru_sm]), jnp.stack([s[0][8] for s in lru_sm]),
                  jnp.stack([s[0][4] for s in lru_sm]), jnp.stack([s[0][7] for s in lru_sm]),
                  jnp.stack([s[1] for s in lru_sm]),
                  jnp.stack([s[0][0:CONV_W] for s in lru_sm]).reshape(NA * CONV_W, D),
                  jnp.stack([s[0][5] for s in lru_sm]), jnp.stack([s[0][6] for s in lru_sm]),
                  jnp.stack([d_small[("pool", j)] for j in range(NB)]), d_final_g.reshape(1, D)]
    small_g = jnp.concatenate(small_rows, axis=0)
    n_small = small_g.shape[0]
    assert n_small <= SMALL_ROWS
    small_g = jnp.pad(small_g, ((0, SMALL_ROWS - n_small), (0, 0)))

    outs = _comm_only("scatter_last", small=small_g, reduce_small=True, **comm_args())
    sg_all, sg_sum = outs[:2]
    after_host(outs[2:])
    psum_mine = ppack
    psum_sib = _comm_only("sibling_last", sib=(ppack, psib, tuple(summed)))[0]

    def big_update(w, m, v, off, name):
        shp = w.shape
        g, dl, m2, v2 = _adam_rows(w.reshape(-1, D), m.reshape(-1, D), v.reshape(-1, D), psum_mine, psum_sib, off, name)
        return g.reshape(shp), dl.reshape(shp), m2.reshape(shp), v2.reshape(shp)

    res = {}
    res["ffn_w1"] = big_update(ffn_w1, m_ffn_w1, v_ffn_w1, O_W1, "adam_w1")
    res["ffn_w2"] = big_update(ffn_w2, m_ffn_w2, v_ffn_w2, O_W2, "adam_w2")
    res["lru_w_y"] = big_update(lru_w_y, m_lru_w_y, v_lru_w_y, O_WY, "adam_wy")
    res["lru_w_in"] = big_update(lru_w_in, m_lru_w_in, v_lru_w_in, O_WIN, "adam_win")
    res["lru_w_out"] = big_update(lru_w_out, m_lru_w_out, v_lru_w_out, O_WOUT, "adam_wout")

    def tiny_parts(w, off):
        n = int(w.size) // D
        return psum_mine[off:off + n].reshape(w.shape), psum_sib[off:off + n].reshape(w.shape)

    tiny_items = [("lru_w_a", lru_w_a, m_lru_w_a, v_lru_w_a) + tiny_parts(lru_w_a, O_WA),
                  ("lru_w_x", lru_w_x, m_lru_w_x, v_lru_w_x) + tiny_parts(lru_w_x, O_WX),
                  ("pool_w", pool_w, m_pool_w, v_pool_w) + tiny_parts(pool_w, O_PW)]

    dmod_all = sg_all[:, :L * N_MOD, :].reshape(NDEV, L, N_MOD * D)
    dmod_sh = lax.dynamic_slice_in_dim(dmod_all, q * Cs, Cs, axis=2).transpose(1, 0, 2)
    res["w_mod"] = _wmod_update(c_all.T, dmod_sh, w_mod, m_w_mod, v_w_mod)

    r0 = 0
    by_rows, names_a = [], []
    for name, w, m, v in (("b_mod", b_mod, m_b_mod, v_b_mod), ("norm_mix_g", norm_mix_g, m_norm_mix_g, v_norm_mix_g),
                          ("norm_ffn_g", norm_ffn_g, m_norm_ffn_g, v_norm_ffn_g),
                          ("lru_b_y", lru_b_y, m_lru_b_y, v_lru_b_y), ("lru_b_in", lru_b_in, m_lru_b_in, v_lru_b_in),
                          ("lru_conv_b", lru_conv_b, m_lru_conv_b, v_lru_conv_b),
                          ("lru_lambda", lru_lambda, m_lru_lambda, v_lru_lambda),
                          ("lru_b_out", lru_b_out, m_lru_b_out, v_lru_b_out)):
        by_rows.append((w, m, v, r0))
        names_a.append(name)
        r0 += int(w.size) // D
    g_conv_w = lax.dynamic_slice_in_dim(sg_sum[r0:r0 + NA * CONV_W].reshape(NA, CONV_W, D), q * dq, dq, axis=2)
    r0 += NA * CONV_W
    g_b_a = lax.dynamic_slice_in_dim(sg_sum[r0:r0 + NA].reshape(NA, HEADS, hd), q * (hd // NQ), hd // NQ, axis=2)
    r0 += NA
    g_b_x = lax.dynamic_slice_in_dim(sg_sum[r0:r0 + NA].reshape(NA, HEADS, hd), q * (hd // NQ), hd // NQ, axis=2)
    r0 += NA
    g_ps = lax.dynamic_slice_in_dim(sg_sum[r0:r0 + NB], q * dq, dq, axis=1)
    r0 += NB
    by_rows.append((final_norm_g.reshape(1, D), m_final_norm_g.reshape(1, D), v_final_norm_g.reshape(1, D), r0))
    names_a.append("final_norm_g")
    sliced = [(lru_conv_w, m_lru_conv_w, v_lru_conv_w, g_conv_w), (lru_b_a, m_lru_b_a, v_lru_b_a, g_b_a),
              (lru_b_x, m_lru_b_x, v_lru_b_x, g_b_x), (pool_scale, m_pool_scale, v_pool_scale, g_ps)]
    res_a, res_b, res_c = _adam_small(sg_sum, by_rows, sliced, [t[1:] for t in tiny_items])
    for name, r in zip(names_a, res_a):
        res[name] = r
    for t, r in zip(tiny_items, res_c):
        res[t[0]] = r
    res["final_norm_g"] = tuple(a.reshape(D) for a in res["final_norm_g"])
    for name, (_, _, _, g), r in zip(("lru_conv_w", "lru_b_a", "lru_b_x", "pool_scale"), sliced, res_b):
        res[name] = (g,) + r

    order = ["w_mod", "b_mod", "norm_mix_g", "norm_ffn_g", "lru_w_y", "lru_b_y", "lru_w_in", "lru_b_in", "lru_conv_w",
             "lru_conv_b", "lru_w_a", "lru_b_a", "lru_w_x", "lru_b_x", "lru_lambda", "lru_w_out", "lru_b_out", "pool_w",
             "pool_scale", "ffn_w1", "ffn_w2", "final_norm_g"]
    return (loss, grad_x, *[res[n][0] for n in order], *[res[n][1] for n in order],
            *[res[n][2] for n in order], *[res[n][3] for n in order])
```

```python
import functools

import jax
import jax.numpy as jnp
from jax import lax
from jax.experimental import pallas as pl
from jax.experimental.pallas import tpu as pltpu

F32 = jnp.float32
BF16 = jnp.bfloat16
MESH = pl.DeviceIdType.MESH

NQ = 4
NDEV = 8
DEPTH = 4
N_MOD = 6
HEADS = 4
CONV_W = 4
POOL_WINDOWS = (2, 4, 8, 16)
LRU_C = 8.0
EPS = 1e-6
ADAM_LR, ADAM_B1, ADAM_B2, ADAM_EPS, ADAM_WD, ADAM_STEP = 0.001, 0.9, 0.999, 1e-08, 0.01, 10

TM = 512
TT = 256
TP = 256
TPF = 512
TK = 2048
SMALL_ROWS = 64
FORWARD_STEPS = 4
VMEM_LIMIT = 60 * 1024 * 1024


def _cparams(*sem):
    return pltpu.CompilerParams(dimension_semantics=tuple(sem), vmem_limit_bytes=VMEM_LIMIT)


def _dot(a, b):
    return jnp.dot(a, b, preferred_element_type=F32)


def _dot_nt(a, b):
    return lax.dot_general(a, b, (((1,), (1,)), ((), ())), preferred_element_type=F32)


def _dot_tn(a, b):
    return lax.dot_general(a, b, (((0,), (0,)), ((), ())), preferred_element_type=F32)


def _resident(shape, index_map):
    return pl.BlockSpec(shape, index_map, pipeline_mode=pl.Buffered(1))


def _rms_fwd(x, g, sc, sh):
    r = lax.rsqrt(jnp.mean(x * x, axis=-1, keepdims=True) + EPS)
    xhat = x * r
    n = xhat * g
    return n * (1.0 + sc) + sh, xhat, r, n


def _rms_bwd(dh, xhat, r, n, g, sc):
    dsh = jnp.sum(dh, axis=0, keepdims=True)
    dsc = jnp.sum(dh * n, axis=0, keepdims=True)
    dn = dh * (1.0 + sc)
    dg = jnp.sum(dn * xhat, axis=0, keepdims=True)
    dxh = dn * g
    dx = r * (dxh - xhat * jnp.mean(dxh * xhat, axis=-1, keepdims=True))
    return dx, dsh, dsc, dg


_GELU_K = 0.7978845608028654
_GELU_C = 0.044715


def _gelu(x):
    t = jnp.tanh(_GELU_K * (x + _GELU_C * x * x * x))
    return 0.5 * x * (1.0 + t), t


def _gelu_grad(x, t):
    return 0.5 * (1.0 + t) + 0.5 * x * (1.0 - t * t) * (_GELU_K * (1.0 + 3.0 * _GELU_C * x * x))


def _neg_expm1(y, exp_y):
    series = -(y * (1.0 + y * (0.5 + y * (1.0 / 6.0))))
    return jnp.where(y > -(1.0 / 64.0), series, 1.0 - exp_y)


def _zero_first(ref):
    @pl.when(pl.program_id(0) == 0)
    def _():
        ref[...] = jnp.zeros_like(ref)


def _my_pos():
    return lax.axis_index("x"), lax.axis_index("y"), lax.axis_index("c")


def _dev_index(x, y, c):
    return 4 * x + 2 * y + c


def _chip_peers(x, y):
    return [(1 - x, y), (x, 1 - y), (1 - x, 1 - y)]


def _all_peers(x, y, c):
    return [(px, py, c) for (px, py) in _chip_peers(x, y)] + [(x, y, 1 - c)] + \
           [(px, py, 1 - c) for (px, py) in _chip_peers(x, y)]


def _comm_run(phase, x, y, c, gathers, scatters, sib, send, recv, loc):
    q = 2 * x + y
    peers = _chip_peers(x, y)
    sibling = (x, y, 1 - c)

    def rcopy(src, dst, s, dev):
        return pltpu.make_async_remote_copy(src, dst, send.at[s], recv.at[s], device_id=dev, device_id_type=MESH)

    s = 0
    for gi, (src, dst) in enumerate(gathers):
        half = src.shape[0] // 2
        mine, other = pl.ds(c * half, half), pl.ds((1 - c) * half, half)
        own = pltpu.make_async_copy(src, dst.at[q], loc.at[gi])
        if phase == "start":
            own.start()
        elif phase == "finish":
            own.wait()
        for (px, py) in peers:
            pq = 2 * px + py
            s_ici, s_fwd = s, s + 1
            s += 2
            if phase == "start":
                rcopy(src.at[mine], dst.at[q].at[mine], s_ici, (px, py, c)).start()
            elif phase == "forward":
                rcopy(src.at[mine], dst.at[pq].at[mine], s_ici, (px, py, c)).wait_recv()
                rcopy(dst.at[pq].at[mine], dst.at[pq].at[mine], s_fwd, sibling).start()
            else:
                rcopy(dst.at[pq].at[other], dst.at[pq].at[other], s_fwd, sibling).wait_recv()
                rcopy(src.at[mine], dst.at[q].at[mine], s_ici, (px, py, c)).wait_send()
                rcopy(dst.at[pq].at[mine], dst.at[pq].at[mine], s_fwd, sibling).wait_send()
    direct = []
    for (src, dst) in scatters:
        for k, (px, py) in enumerate(peers):
            direct.append((src.at[2 * px + py], dst.at[k], (px, py, c)))
    if sib is not None:
        src, dst, ranges = sib
        for (off, rows) in ranges:
            direct.append((src.at[pl.ds(off, rows)], dst.at[pl.ds(off, rows)], sibling))
    if phase == "start":
        for k, (a, b, dev) in enumerate(direct):
            rcopy(a, b, s + k, dev).start()
    elif phase == "finish":
        for k, (a, b, dev) in enumerate(direct):
            rcopy(a, b, s + k, dev).wait_recv()
        for k, (a, b, dev) in enumerate(direct):
            rcopy(a, b, s + k, dev).wait_send()


def _comm_shapes(gathers, scatters, sib):
    assert all(g.shape[0] % 32 == 0 for g in gathers)
    cin = list(gathers) + list(scatters) + ([sib[0], sib[1]] if sib else [])
    cout = [jax.ShapeDtypeStruct((NQ,) + g.shape, g.dtype) for g in gathers] + \
           [jax.ShapeDtypeStruct((3,) + s.shape[1:], s.dtype) for s in scatters] + \
           ([jax.ShapeDtypeStruct(sib[1].shape, sib[1].dtype)] if sib else [])
    n_rem = 6 * len(gathers) + 3 * len(scatters) + (len(sib[2]) if sib else 0)
    sems = [pltpu.SemaphoreType.DMA((max(n_rem, 1),)), pltpu.SemaphoreType.DMA((max(n_rem, 1),)),
            pltpu.SemaphoreType.DMA((max(len(gathers), 1),))]
    return cin, cout, sems


def _pcall(body, *, name, grid, in_specs, out_specs, out_shape, operands, scratch_shapes=(),
           gathers=(), scatters=(), sib=None):
    assert len(grid) == 1
    out_shape, out_specs = tuple(out_shape), tuple(out_specs)
    if not (gathers or scatters or sib):
        return pl.pallas_call(body, name=name, grid=grid, in_specs=list(in_specs), out_specs=out_specs,
                              out_shape=out_shape, scratch_shapes=list(scratch_shapes),
                              compiler_params=_cparams("arbitrary"))(*operands)
    cin, cout, sems = _comm_shapes(gathers, scatters, sib)
    n_in, n_cin, n_out, n_cout, n_scr = len(operands), len(cin), len(out_shape), len(cout), len(scratch_shapes)
    ng, ns = len(gathers), len(scatters)
    nsteps = grid[0]

    def wrapped(*refs):
        ins = refs[:n_in]
        cins = refs[n_in:n_in + n_cin]
        o0 = n_in + n_cin
        outs = refs[o0:o0 + n_out]
        couts = refs[o0 + n_out:o0 + n_out + n_cout]
        s0 = o0 + n_out + n_cout
        scr = refs[s0:s0 + n_scr]
        send, recv, loc = refs[s0 + n_scr:s0 + n_scr + 3]
        x, y, c = _my_pos()

        def run(phase):
            g = [(cins[k], couts[k]) for k in range(ng)]
            sc = [(cins[ng + k], couts[ng + k]) for k in range(ns)]
            sb = (cins[ng + ns], couts[ng + ns], sib[2]) if sib else None
            _comm_run(phase, x, y, c, g, sc, sb, send, recv, loc)

        @pl.when(pl.program_id(0) == 0)
        def _():
            run("start")

        if ng:
            @pl.when(pl.program_id(0) == max(nsteps - FORWARD_STEPS, 0))
            def _():
                run("forward")

        body(*ins, *outs, *scr)

        @pl.when(pl.program_id(0) == nsteps - 1)
        def _():
            run("finish")

    anyspec = pl.BlockSpec(memory_space=pl.ANY)
    aliases = {n_in + ng + ns + 1: n_out + ng + ns} if sib else {}
    return pl.pallas_call(
        wrapped, name=name, grid=grid,
        in_specs=list(in_specs) + [anyspec] * n_cin, out_specs=out_specs + (anyspec,) * n_cout,
        out_shape=out_shape + tuple(cout), scratch_shapes=list(scratch_shapes) + sems,
        input_output_aliases=aliases,
        compiler_params=pltpu.CompilerParams(dimension_semantics=("arbitrary",), vmem_limit_bytes=VMEM_LIMIT,
                                             has_side_effects=True),
    )(*operands, *cin)


def _comm_only(name, small=None, reduce_small=False, gathers=(), scatters=(), sib=None):
    cin, cout, sems = _comm_shapes(gathers, scatters, sib)
    n_cin, n_cout = len(cin), len(cout)
    ng, ns = len(gathers), len(scatters)
    n_sm_in = 1 if small is not None else 0
    n_sm_out = (2 if reduce_small else 1) if small is not None else 0

    def body(*refs):
        sm_in = refs[:n_sm_in]
        cins = refs[n_sm_in:n_sm_in + n_cin]
        o0 = n_sm_in + n_cin
        sm_out = refs[o0:o0 + n_sm_out]
        couts = refs[o0 + n_sm_out:o0 + n_sm_out + n_cout]
        s0 = o0 + n_sm_out + n_cout
        send, recv, loc = refs[s0:s0 + 3]
        x, y, c = _my_pos()
        g = [(cins[k], couts[k]) for k in range(ng)]
        sc = [(cins[ng + k], couts[ng + k]) for k in range(ns)]
        sb = (cins[ng + ns], couts[ng + ns], sib[2]) if sib else None
        _comm_run("start", x, y, c, g, sc, sb, send, recv, loc)
        if small is not None:
            sm_send, sm_recv = refs[s0 + 3:s0 + 5]
            small_ref, sg_ref = sm_in[0], sm_out[0]
            me = _dev_index(x, y, c)
            sg_ref[me] = small_ref[...]
            peers = _all_peers(x, y, c)
            sm = [pltpu.make_async_remote_copy(small_ref, sg_ref.at[me], sm_send.at[k], sm_recv.at[k],
                                               device_id=peer, device_id_type=MESH) for k, peer in enumerate(peers)]
            for cp in sm:
                cp.start()
            for k, (px, py, pc) in enumerate(peers):
                pltpu.make_async_remote_copy(small_ref, sg_ref.at[_dev_index(px, py, pc)], sm_send.at[k], sm_recv.at[k],
                                             device_id=(px, py, pc), device_id_type=MESH).wait_recv()
            if reduce_small:
                acc = sg_ref[0]
                for d in range(1, NDEV):
                    acc = acc + sg_ref[d]
                sm_out[1][...] = acc
            for cp in sm:
                cp.wait_send()
        if ng:
            _comm_run("forward", x, y, c, g, sc, sb, send, recv, loc)
        _comm_run("finish", x, y, c, g, sc, sb, send, recv, loc)

    anyspec = pl.BlockSpec(memory_space=pl.ANY)
    vspec = pl.BlockSpec(memory_space=pltpu.VMEM)
    sm_shapes = []
    if small is not None:
        sm_shapes.append(jax.ShapeDtypeStruct((NDEV,) + small.shape, small.dtype))
        if reduce_small:
            sm_shapes.append(jax.ShapeDtypeStruct(small.shape, small.dtype))
        sems = sems + [pltpu.SemaphoreType.DMA((NDEV - 1,)), pltpu.SemaphoreType.DMA((NDEV - 1,))]
    aliases = {n_sm_in + ng + ns + 1: n_sm_out + ng + ns} if sib else {}
    return pl.pallas_call(
        body, name=name,
        in_specs=[vspec] * n_sm_in + [anyspec] * n_cin,
        out_specs=tuple([vspec] * n_sm_out + [anyspec] * n_cout),
        out_shape=tuple(sm_shapes + cout), scratch_shapes=sems, input_output_aliases=aliases,
        compiler_params=pltpu.CompilerParams(has_side_effects=True),
    )(*([small] if small is not None else []), *cin)


def _exchange_mod(modpart):
    _, L, Cs = modpart.shape

    def body(part_ref, out_ref, send, recv):
        x, y, c = _my_pos()
        q = 2 * x + y
        me = _dev_index(x, y, c)
        out_ref[q] = part_ref[me]
        sends = []
        for k, (px, py) in enumerate(_chip_peers(x, y)):
            cp = pltpu.make_async_remote_copy(part_ref.at[_dev_index(px, py, c)], out_ref.at[q], send.at[k], recv.at[k],
                                              device_id=(px, py, c), device_id_type=MESH)
            cp.start()
            sends.append(cp)
        for k, (px, py) in enumerate(_chip_peers(x, y)):
            pltpu.make_async_remote_copy(part_ref.at[me], out_ref.at[2 * px + py], send.at[k], recv.at[k],
                                         device_id=(px, py, c), device_id_type=MESH).wait_recv()
        for cp in sends:
            cp.wait_send()

    return pl.pallas_call(
        body, name="exchange_mod",
        out_shape=jax.ShapeDtypeStruct((NQ, L, Cs), modpart.dtype),
        in_specs=[pl.BlockSpec(memory_space=pltpu.VMEM)],
        out_specs=pl.BlockSpec(memory_space=pltpu.VMEM),
        scratch_shapes=[pltpu.SemaphoreType.DMA((3,)), pltpu.SemaphoreType.DMA((3,))],
        compiler_params=pltpu.CompilerParams(has_side_effects=True),
    )(modpart)


def _mod_part(c_all, w_mod, b_mod_sh):
    L, D, Cs = w_mod.shape
    tn = 512 if Cs % 512 == 0 else Cs

    def body(c_ref, w_ref, b_ref, o_ref):
        cv = c_ref[...]
        cond = cv * jax.nn.sigmoid(cv)
        o_ref[...] = jnp.dot(cond, w_ref[...], preferred_element_type=F32, precision=lax.Precision.HIGHEST) + b_ref[...]

    return pl.pallas_call(
        body, name="mod_part", grid=(L, Cs // tn),
        out_shape=jax.ShapeDtypeStruct((L, NDEV, Cs), F32),
        in_specs=[pl.BlockSpec((NDEV, D), lambda i, j: (0, 0)),
                  pl.BlockSpec((None, D, tn), lambda i, j: (i, 0, j)),
                  pl.BlockSpec((None, 1, tn), lambda i, j: (i, 0, j))],
        out_specs=pl.BlockSpec((None, NDEV, tn), lambda i, j: (i, 0, j)),
        compiler_params=_cparams("parallel", "parallel"),
    )(c_all, w_mod, b_mod_sh)


def _adam(w, g, m, v):
    m2 = ADAM_B1 * m + (1.0 - ADAM_B1) * g
    v2 = ADAM_B2 * v + (1.0 - ADAM_B2) * (g * g)
    m_hat = m2 / (1.0 - ADAM_B1 ** ADAM_STEP)
    v_hat = v2 / (1.0 - ADAM_B2 ** ADAM_STEP)
    delta = -ADAM_LR * (m_hat / (jnp.sqrt(v_hat) + ADAM_EPS) + ADAM_WD * w)
    return delta, m2, v2


def _wmod_update(c_all_t, dmod_sh, w, m, v):
    L, D, Cs = w.shape
    td = 256 if D % 256 == 0 else D

    def body(ct_ref, d_ref, w_ref, m_ref, v_ref, g_ref, dl_ref, m2_ref, v2_ref):
        cv = ct_ref[...]
        cond = cv * jax.nn.sigmoid(cv)
        g = cond[:, 0:1] * d_ref[0:1, :]
        for b in range(1, NDEV):
            g = g + cond[:, b:b + 1] * d_ref[b:b + 1, :]
        g_ref[...] = g
        dl_ref[...], m2_ref[...], v2_ref[...] = _adam(w_ref[...], g, m_ref[...], v_ref[...])

    blk = pl.BlockSpec((None, td, Cs), lambda i, j: (i, j, 0))
    out = jax.ShapeDtypeStruct((L, D, Cs), F32)
    return pl.pallas_call(
        body, name="wmod_update", grid=(L, D // td),
        out_shape=(out, out, out, out),
        in_specs=[pl.BlockSpec((td, NDEV), lambda i, j: (j, 0)),
                  pl.BlockSpec((None, NDEV, Cs), lambda i, j: (i, 0, 0)), blk, blk, blk],
        out_specs=(blk, blk, blk, blk),
        compiler_params=_cparams("parallel", "parallel"),
    )(c_all_t, dmod_sh, w, m, v)


def _adam_rows(w, m, v, pa, pb, row_off, name):
    rows, C = w.shape
    tr = 512 if rows % 512 == 0 else (128 if rows % 128 == 0 else rows)
    assert row_off % tr == 0
    ob = row_off // tr

    def body(w_ref, m_ref, v_ref, pa_ref, pb_ref, g_ref, dl_ref, m2_ref, v2_ref):
        g = pa_ref[...] + pb_ref[...]
        g_ref[...] = g
        dl_ref[...], m2_ref[...], v2_ref[...] = _adam(w_ref[...], g, m_ref[...], v_ref[...])

    blk = pl.BlockSpec((tr, C), lambda i: (i, 0))
    pblk = pl.BlockSpec((tr, C), lambda i: (ob + i, 0))
    out = jax.ShapeDtypeStruct((rows, C), F32)
    return pl.pallas_call(
        body, name=name, grid=(rows // tr,), out_shape=(out, out, out, out),
        in_specs=[blk, blk, blk, pblk, pblk], out_specs=(blk, blk, blk, blk),
        compiler_params=_cparams("parallel"),
    )(w, m, v, pa, pb)


def _adam_small(sg_sum, by_rows, sliced, pairs):
    D = sg_sum.shape[1]
    na, nb, nc = len(by_rows), len(sliced), len(pairs)

    def body(*refs):
        sg = refs[0]
        ins_a = [refs[1 + 3 * t:4 + 3 * t] for t in range(na)]
        p = 1 + 3 * na
        ins_b = [refs[p + 4 * t:p + 4 * t + 4] for t in range(nb)]
        p += 4 * nb
        ins_c = [refs[p + 5 * t:p + 5 * t + 5] for t in range(nc)]
        p += 5 * nc
        outs_a = [refs[p + 4 * t:p + 4 * t + 4] for t in range(na)]
        p += 4 * na
        outs_b = [refs[p + 3 * t:p + 3 * t + 3] for t in range(nb)]
        p += 3 * nb
        outs_c = [refs[p + 4 * t:p + 4 * t + 4] for t in range(nc)]
        for (w_ref, m_ref, v_ref, ga_ref, gb_ref), (g_ref, dl_ref, m2_ref, v2_ref) in zip(ins_c, outs_c):
            g = ga_ref[...] + gb_ref[...]
            g_ref[...] = g
            dl_ref[...], m2_ref[...], v2_ref[...] = _adam(w_ref[...], g, m_ref[...], v_ref[...])
        for (w_ref, m_ref, v_ref), (g_ref, dl_ref, m2_ref, v2_ref), (w, _, _, row0) in zip(ins_a, outs_a, by_rows):
            n, k = w.shape[0], w.shape[1] // D
            pieces = [(slice(0, n), slice(0, D), slice(row0, row0 + n))] if k == 1 else \
                     [(slice(i, i + 1), slice(kk * D, (kk + 1) * D), slice(row0 + i * k + kk, row0 + i * k + kk + 1))
                      for i in range(n) for kk in range(k)]
            for rs, cs, gs in pieces:
                g = sg[gs, :]
                g_ref[rs, cs] = g
                dl_ref[rs, cs], m2_ref[rs, cs], v2_ref[rs, cs] = _adam(w_ref[rs, cs], g, m_ref[rs, cs], v_ref[rs, cs])
        for (w_ref, m_ref, v_ref, g_ref), (dl_ref, m2_ref, v2_ref) in zip(ins_b, outs_b):
            dl_ref[...], m2_ref[...], v2_ref[...] = _adam(w_ref[...], g_ref[...], m_ref[...], v_ref[...])

    operands = [sg_sum] + [a for t in by_rows for a in t[:3]] + [a for t in sliced for a in t] + \
               [a for t in pairs for a in t]
    out_shape = [jax.ShapeDtypeStruct(t[0].shape, F32) for t in by_rows for _ in range(4)] + \
                [jax.ShapeDtypeStruct(t[0].shape, F32) for t in sliced for _ in range(3)] + \
                [jax.ShapeDtypeStruct(t[0].shape, F32) for t in pairs for _ in range(4)]
    outs = pl.pallas_call(body, name="adam_small", out_shape=tuple(out_shape))(*operands)
    res_a = [tuple(outs[4 * t:4 * t + 4]) for t in range(na)]
    o = 4 * na
    res_b = [tuple(outs[o + 3 * t:o + 3 * t + 3]) for t in range(nb)]
    o += 3 * nb
    res_c = [tuple(outs[o + 4 * t:o + 4 * t + 4]) for t in range(nc)]
    return res_a, res_b, res_c


def _sum_into(ppack, dw, rb, off, qv):
    _, rows, D = dw.shape
    tr = 256 if rows % 256 == 0 else 128
    assert rows % tr == 0 and off % tr == 0
    ob = off // tr

    def body(q_ref, o_ref, r_ref, pin_ref, p_ref):
        acc = o_ref[...].astype(F32)
        for k in range(3):
            acc = acc + r_ref[k].astype(F32)
        p_ref[...] = acc

    return pl.pallas_call(
        body, name="sum_partials", out_shape=jax.ShapeDtypeStruct(ppack.shape, ppack.dtype),
        grid_spec=pltpu.PrefetchScalarGridSpec(
            num_scalar_prefetch=1, grid=(rows // tr,),
            in_specs=[pl.BlockSpec((None, tr, D), lambda i, q_ref: (q_ref[0], i, 0)),
                      pl.BlockSpec((3, tr, D), lambda i, q_ref: (0, i, 0)),
                      pl.BlockSpec(memory_space=pl.ANY)],
            out_specs=pl.BlockSpec((tr, D), lambda i, q_ref: (ob + i, 0))),
        input_output_aliases={3: 0},
        compiler_params=_cparams("parallel"),
    )(qv, dw, rb, ppack)


def _wspec(g):
    return _resident(g.shape, lambda i: (0, 0, 0))


def _ffn_fwd_inner(x1, mod_ref, gf_ref, w1_ref, w2_ref, h2_ref, a_ref, z_ref, x2_ref):
    h2 = _rms_fwd(x1, gf_ref[...], mod_ref[4:5, :], mod_ref[3:4, :])[0]
    h2b = h2.astype(BF16)
    h2_ref[...] = h2b
    f4 = w1_ref.shape[2]
    z = jnp.zeros(x1.shape, F32)
    for q in range(NQ):
        a = jnp.maximum(_dot(h2b, w1_ref[q]), 0.0)
        a_ref[:, q * f4:(q + 1) * f4] = a.astype(BF16)
        z = z + _dot((a * a).astype(BF16), w2_ref[q])
    z_ref[...] = z.astype(BF16)
    x2_ref[...] = x1 + mod_ref[5:6, :] * z


def _sigmoid(x):
    return 0.5 + 0.5 * jnp.tanh(0.5 * x)


def _heads_dot(xb, w_ref, hd, nt=False):
    outs = []
    for h in range(HEADS):
        xs = xb[:, h * hd:(h + 1) * hd]
        outs.append(_dot_nt(xs, w_ref[h]) if nt else _dot(xs, w_ref[h]))
    return jnp.concatenate(outs, axis=1)


def _lru_gates(xc, wa_ref, ba, wx_ref, bx, lam, hd):
    xcb = xc.astype(BF16)
    gate_r = _sigmoid(_heads_dot(xcb, wa_ref, hd) + ba)
    gate_i = _sigmoid(_heads_dot(xcb, wx_ref, hd) + bx)
    ls = jax.nn.log_sigmoid(lam)
    log_a = gate_r * (LRU_C * ls)
    a = jnp.exp(log_a)
    mult = jnp.sqrt(_neg_expm1(2.0 * log_a, a * a))
    return xcb, gate_r, gate_i, ls, a, mult


def _conv_taps(xext, cw, tt):
    acc = cw[0:1, :] * xext[pl.ds(8 - (CONV_W - 1), tt), :]
    for k in range(1, CONV_W):
        acc = acc + cw[k:k + 1, :] * xext[pl.ds(8 - (CONV_W - 1) + k, tt), :]
    return acc


def _lru_fwd(x, mod_l, g_mix, g_wy, g_win, b_y, b_in, cw, cb, wa, ba, wx, bx, lam, g_wout, b_out, **comm):
    S, W = x.shape
    tt = min(TT, S)
    hd = W // HEADS

    def body(x_ref, mod_ref, g_ref, wy_ref, win_ref, by_ref, bin_ref, cw_ref, cb_ref, wa_ref, ba_ref, wx_ref, bx_ref,
             lam_ref, wo_ref, bo_ref, h_ref, gb_ref, xr_ref, hs_ref, p_ref, y_ref, x1_ref,
             xc_ref, gr_ref, gi_ref, a_s, mu_ref, xext, u_s, carry):
        i = pl.program_id(0)

        @pl.when(i == 0)
        def _():
            carry[...] = jnp.zeros_like(carry)
            xext[0:8, :] = jnp.zeros((8, W), F32)

        @pl.when(i > 0)
        def _():
            xext[0:8, :] = xext[pl.ds(tt, 8), :]

        xv = x_ref[...]
        hb = _rms_fwd(xv, g_ref[...], mod_ref[1:2, :], mod_ref[0:1, :])[0].astype(BF16)
        h_ref[...] = hb
        gbv = _dot(hb, wy_ref[...].reshape(W, W)) + by_ref[...]
        gb_ref[...] = gbv
        xr = _dot(hb, win_ref[...].reshape(W, W)) + bin_ref[...]
        xr_ref[...] = xr
        xext[pl.ds(8, tt), :] = xr
        xc = _conv_taps(xext, cw_ref[...], tt) + cb_ref[...]
        _, gate_r, gate_i, _, a, mult = _lru_gates(xc, wa_ref, ba_ref[...], wx_ref, bx_ref[...], lam_ref[...], hd)
        xc_ref[...] = xc
        gr_ref[...] = gate_r
        gi_ref[...] = gate_i
        mu_ref[...] = mult
        a_s[...] = a
        u_s[...] = mult * (gate_i * xc)
        row = lax.broadcasted_iota(jnp.int32, (8, W), 0)

        def step(k, _):
            off = pl.multiple_of(k * 8, 8)
            A = a_s[pl.ds(off, 8), :]
            U = u_s[pl.ds(off, 8), :]
            for d in (1, 2, 4):
                keep = row >= d
                Us = jnp.where(keep, pltpu.roll(U, d, 0), 0.0)
                As = jnp.where(keep, pltpu.roll(A, d, 0), 1.0)
                U = U + A * Us
                A = A * As
            H = U + A * carry[...]
            hs_ref[pl.ds(off, 8), :] = H
            carry[...] = jnp.broadcast_to(H[7:8, :], (8, W))
            return 0

        lax.fori_loop(0, tt // 8, step, 0)
        pb = (hs_ref[...] * _gelu(gbv)[0]).astype(BF16)
        p_ref[...] = pb
        y = _dot(pb, wo_ref[...].reshape(W, W)) + bo_ref[...]
        y_ref[...] = y.astype(BF16)
        x1_ref[...] = xv + mod_ref[2:3, :] * y

    tile = pl.BlockSpec((tt, W), lambda i: (i, 0))
    row = pl.BlockSpec((1, W), lambda i: (0, 0))
    wblk = pl.BlockSpec((HEADS, hd, hd), lambda i: (0, 0, 0))
    f32o, bf16o = jax.ShapeDtypeStruct((S, W), F32), jax.ShapeDtypeStruct((S, W), BF16)
    return _pcall(
        body, name="lru_fwd", grid=(S // tt,),
        out_shape=(bf16o, f32o, f32o, f32o, bf16o, bf16o, f32o, f32o, f32o, f32o, f32o, f32o),
        in_specs=[tile, pl.BlockSpec((8, W), lambda i: (0, 0)), row, _wspec(g_wy), _wspec(g_win), row, row,
                  pl.BlockSpec((CONV_W, W), lambda i: (0, 0)), row, wblk, row, wblk, row, row, _wspec(g_wout), row],
        out_specs=(tile,) * 12,
        scratch_shapes=[pltpu.VMEM((tt + 8, W), F32), pltpu.VMEM((tt, W), F32), pltpu.VMEM((8, W), F32)],
        operands=(x, mod_l, g_mix, g_wy, g_win, b_y, b_in, cw, cb, wa, ba, wx, bx, lam, g_wout, b_out), **comm)


def _ffn_out_shapes(S, D, F):
    return (jax.ShapeDtypeStruct((S, D), BF16), jax.ShapeDtypeStruct((S, F), BF16),
            jax.ShapeDtypeStruct((S, D), BF16), jax.ShapeDtypeStruct((S, D), F32))


def _ffn_fwd(x1, mod_l, g_ffn, g_w1, g_w2, **comm):
    S, D = x1.shape
    tm = min(TM, S)
    F = g_w1.shape[2] * NQ

    def body(x1_ref, mod_ref, gf_ref, w1_ref, w2_ref, h2_ref, a_ref, z_ref, x2_ref):
        _ffn_fwd_inner(x1_ref[...], mod_ref, gf_ref, w1_ref, w2_ref, h2_ref, a_ref, z_ref, x2_ref)

    tile = pl.BlockSpec((tm, D), lambda i: (i, 0))
    row = pl.BlockSpec((1, D), lambda i: (0, 0))
    return _pcall(
        body, name="ffn_fwd", grid=(S // tm,),
        out_shape=_ffn_out_shapes(S, D, F),
        in_specs=[tile, pl.BlockSpec((8, D), lambda i: (0, 0)), row, _wspec(g_w1), _wspec(g_w2)],
        out_specs=(tile, pl.BlockSpec((tm, F), lambda i: (i, 0)), tile, tile),
        operands=(x1, mod_l, g_ffn, g_w1, g_w2), **comm)


def _window_vec(D):
    gd = D // len(POOL_WINDOWS)
    lane = lax.broadcasted_iota(jnp.int32, (1, D), 1)
    w = jnp.full((1, D), float(POOL_WINDOWS[0]), F32)
    for g in range(1, len(POOL_WINDOWS)):
        w = jnp.where(lane >= g * gd, float(POOL_WINDOWS[g]), w)
    return w


def _pool_mix_ffn_fwd(x, mod_l, g_mix, pw, ps, g_ffn, g_w1, g_w2, **comm):
    S, D = x.shape
    tm = min(TPF, S)
    F = g_w1.shape[2] * NQ
    gd = D // len(POOL_WINDOWS)
    n = tm + 24

    def body(x_ref, xh_ref, mod_ref, gm_ref, pw_ref, ps_ref, gf_ref, w1_ref, w2_ref,
             pl_ref, x1_ref, h2_ref, a_ref, z_ref, x2_ref, ext, b1, b2):
        i = pl.program_id(0)
        g, sc, sh = gm_ref[...], mod_ref[1:2, :], mod_ref[0:1, :]
        xv = x_ref[...]
        h = _rms_fwd(xv, g, sc, sh)[0]
        hh = _rms_fwd(xh_ref[...], g, sc, sh)[0]
        zeros8 = jnp.zeros((8, D), F32)
        ext[0:8, :] = zeros8
        b1[0:8, :] = zeros8
        b2[0:8, :] = zeros8
        ext[8:24, :] = jnp.where(i > 0, hh, 0.0)
        ext[pl.ds(24, tm), :] = h
        m = n - 8
        b1[pl.ds(8, m), :] = ext[pl.ds(8, m), :] + ext[pl.ds(7, m), :]
        b2[pl.ds(8, m), gd:] = b1[pl.ds(8, m), gd:] + b1[pl.ds(6, m), gd:]
        b1[pl.ds(8, m), 2 * gd:] = b2[pl.ds(8, m), 2 * gd:] + b2[pl.ds(4, m), 2 * gd:]
        b2[pl.ds(8, m), 3 * gd:] = b1[pl.ds(8, m), 3 * gd:] + b1[pl.ds(0, m), 3 * gd:]
        wsum = jnp.concatenate([b1[pl.ds(24, tm), 0:gd], b2[pl.ds(24, tm), gd:2 * gd],
                                b1[pl.ds(24, tm), 2 * gd:3 * gd], b2[pl.ds(24, tm), 3 * gd:]], axis=1)
        t1 = (lax.broadcasted_iota(jnp.int32, (tm, 1), 0) + (i * tm + 1)).astype(F32)
        cnt = jnp.minimum(t1, _window_vec(D))
        pooled = (wsum / cnt - h).astype(BF16)
        pl_ref[...] = pooled
        y = _heads_dot(pooled, pw_ref, gd) * ps_ref[...]
        x1 = xv + mod_ref[2:3, :] * y
        x1_ref[...] = x1
        _ffn_fwd_inner(x1, mod_ref, gf_ref, w1_ref, w2_ref, h2_ref, a_ref, z_ref, x2_ref)

    tile = pl.BlockSpec((tm, D), lambda i: (i, 0))
    halo = pl.BlockSpec((16, D), lambda i: (jnp.maximum(i * (tm // 16) - 1, 0), 0))
    row = pl.BlockSpec((1, D), lambda i: (0, 0))
    return _pcall(
        body, name="pool_mix_ffn_fwd", grid=(S // tm,),
        out_shape=(jax.ShapeDtypeStruct((S, D), BF16), jax.ShapeDtypeStruct((S, D), F32)) + _ffn_out_shapes(S, D, F),
        in_specs=[tile, halo, pl.BlockSpec((8, D), lambda i: (0, 0)), row,
                  pl.BlockSpec((len(POOL_WINDOWS), gd, gd), lambda i: (0, 0, 0)), row, row,
                  _wspec(g_w1), _wspec(g_w2)],
        out_specs=(tile, tile, tile, pl.BlockSpec((tm, F), lambda i: (i, 0)), tile, tile),
        scratch_shapes=[pltpu.VMEM((n, D), F32), pltpu.VMEM((n, D), F32), pltpu.VMEM((n, D), F32)],
        operands=(x, x, mod_l, g_mix, pw, ps, g_ffn, g_w1, g_w2), **comm)


def _loss_head(xv, gv, tv, acc_ref):
    D = xv.shape[1]
    r = lax.rsqrt(jnp.mean(xv * xv, axis=-1, keepdims=True) + EPS)
    xhat = xv * r
    err = xhat * gv - tv
    acc_ref[0:1, :] += jnp.sum(err * err, axis=0, keepdims=True)
    dy = err * (1.0 / D)
    acc_ref[1:2, :] += jnp.sum(dy * xhat, axis=0, keepdims=True)
    dxh = dy * gv
    return r * (dxh - xhat * jnp.mean(dxh * xhat, axis=-1, keepdims=True))


def _ffn_bwd(dx2, x1, a, z, mod_l, g_ffn, g_w1, g_w2, head=None, **comm):
    S, D = dx2.shape
    F = a.shape[1]
    f4 = F // NQ
    tm = min(TM, S)
    nh = 2 if head else 0

    def body(*refs):
        dx2_ref, x1_ref, a_ref, z_ref, mod_ref, gf_ref, w1_ref, w2_ref = refs[:8]
        dx1_ref, du_ref, dz_ref, acc_ref = refs[8 + nh:]
        _zero_first(acc_ref)
        dx2v = dx2_ref[...]
        if head:
            dx2v = _loss_head(dx2v, refs[8][...], refs[9][...], acc_ref)
        acc_ref[5:6, :] +=jnp.sum(dx2v * z_ref[...].astype(F32), axis=0, keepdims=True)
        dzb = (dx2v * mod_ref[5:6, :]).astype(BF16)
        dz_ref[...] = dzb
        dh2 = jnp.zeros((tm, D), F32)
        for q in range(NQ):
            av = a_ref[:, q * f4:(q + 1) * f4].astype(F32)
            du = (_dot_nt(dzb, w2_ref[q]) * (2.0 * av)).astype(BF16)
            du_ref[:, q * f4:(q + 1) * f4] = du
            dh2 = dh2 + _dot_nt(du, w1_ref[q])
        g, sc = gf_ref[...], mod_ref[4:5, :]
        _, xhat, r, n = _rms_fwd(x1_ref[...], g, sc, mod_ref[3:4, :])
        dx, dsh, dsc, dg = _rms_bwd(dh2, xhat, r, n, g, sc)
        acc_ref[3:4, :] += dsh
        acc_ref[4:5, :] += dsc
        acc_ref[7:8, :] += dg
        dx1_ref[...] = dx2v + dx

    tile = pl.BlockSpec((tm, D), lambda i: (i, 0))
    wide = pl.BlockSpec((tm, F), lambda i: (i, 0))
    return _pcall(
        body, name="ffn_bwd", grid=(S // tm,),
        out_shape=(jax.ShapeDtypeStruct((S, D), F32), jax.ShapeDtypeStruct((S, F), BF16),
                   jax.ShapeDtypeStruct((S, D), BF16), jax.ShapeDtypeStruct((8, D), F32)),
        in_specs=[tile, tile, wide, tile, pl.BlockSpec((8, D), lambda i: (0, 0)), pl.BlockSpec((1, D), lambda i: (0, 0)),
                  _wspec(g_w1), _wspec(g_w2)] + ([pl.BlockSpec((1, D), lambda i: (0, 0)), tile] if head else []),
        out_specs=(tile, wide, tile, pl.BlockSpec((8, D), lambda i: (0, 0))),
        operands=(dx2, x1, a, z, mod_l, g_ffn, g_w1, g_w2) + (tuple(head) if head else ()), **comm)


def _dw_blocked(a, b, by_rows, square_a, name):
    S = a.shape[0]
    tk = min(TK, S)
    nk = S // tk
    if by_rows:
        bm, bn = a.shape[1] // NQ, b.shape[1]
        a_map, b_map = (lambda q, k: (k, q)), (lambda q, k: (k, 0))
    else:
        bm, bn = a.shape[1], b.shape[1] // NQ
        a_map, b_map = (lambda q, k: (k, 0)), (lambda q, k: (k, q))

    def body(a_ref, b_ref, o_ref, acc):
        k = pl.program_id(1)

        @pl.when(k == 0)
        def _():
            acc[...] = jnp.zeros_like(acc)

        av = a_ref[...]
        if square_a:
            av = av * av
        acc[...] += _dot_tn(av, b_ref[...])

        @pl.when(k == nk - 1)
        def _():
            o_ref[...] = acc[...].astype(o_ref.dtype)

    return pl.pallas_call(
        body, name=name, grid=(NQ, nk),
        out_shape=jax.ShapeDtypeStruct((NQ, bm, bn), BF16),
        in_specs=[pl.BlockSpec((tk, bm), a_map), pl.BlockSpec((tk, bn), b_map)],
        out_specs=pl.BlockSpec((None, bm, bn), lambda q, k: (q, 0, 0)),
        scratch_shapes=[pltpu.VMEM((bm, bn), F32)],
        compiler_params=_cparams("parallel", "arbitrary"),
    )(a, b)


def _dw_whole(a, bs, name, **comm):
    S, M = a.shape
    N = bs[0].shape[1]
    tk = min(TK, S)
    nk = S // tk
    nb = len(bs)

    def body(*refs):
        a_ref, b_refs, o_refs, accs = refs[0], refs[1:1 + nb], refs[1 + nb:1 + 2 * nb], refs[1 + 2 * nb:]
        k = pl.program_id(0)

        @pl.when(k == 0)
        def _():
            for acc in accs:
                acc[...] = jnp.zeros_like(acc)

        av = a_ref[...]
        for b_ref, acc in zip(b_refs, accs):
            acc[...] += _dot_tn(av, b_ref[...])

        @pl.when(k == nk - 1)
        def _():
            for o_ref, acc in zip(o_refs, accs):
                o_ref[...] = acc[...].reshape(NQ, M // NQ, N).astype(o_ref.dtype)

    return _pcall(
        body, name=name, grid=(nk,),
        out_shape=tuple(jax.ShapeDtypeStruct((NQ, M // NQ, N), BF16) for _ in bs),
        in_specs=[pl.BlockSpec((tk, M), lambda k: (k, 0))] + [pl.BlockSpec((tk, N), lambda k: (k, 0)) for _ in bs],
        out_specs=tuple(pl.BlockSpec((NQ, M // NQ, N), lambda k: (0, 0, 0)) for _ in bs),
        scratch_shapes=[pltpu.VMEM((M, N), F32) for _ in bs],
        operands=(a, *bs), **comm)


def _lru_bwd(dx1, y, x, xr0, gb, hs, xc_, gate_r_, gate_i_, a_, mult_, mod_l, g_mix, g_wout, g_wy, g_win, cw, wa, wx,
             lam, **comm):
    S, W = xr0.shape
    tt = min(TT, S)
    nb = S // tt
    hd = W // HEADS

    def body(dx1_ref, y_ref, x_ref, xr_ref, gb_ref, hs_ref, hsh_ref, xc_ref, gr_ref, gi_ref, a_s, mu_ref,
             mod_ref, gm_ref, wo_ref, wy_ref, win_ref, cw_ref, wa_ref, wx_ref, lam_ref,
             dy_ref, dgb_ref, dxr_ref, dx_ref, sm_ref, dwa_ref, dwx_ref, acc_ref,
             hext, qext, dext, b_s, qc, dc):
        i = pl.program_id(0)
        blk = nb - 1 - i

        @pl.when(i == 0)
        def _():
            sm_ref[...] = jnp.zeros_like(sm_ref)
            dwa_ref[...] = jnp.zeros_like(dwa_ref)
            dwx_ref[...] = jnp.zeros_like(dwx_ref)
            acc_ref[...] = jnp.zeros_like(acc_ref)
            qc[...] = jnp.zeros_like(qc)
            dc[...] = jnp.zeros_like(dc)

        dx1v = dx1_ref[...]
        acc_ref[2:3, :] += jnp.sum(dx1v * y_ref[...].astype(F32), axis=0, keepdims=True)
        dy = dx1v * mod_ref[2:3, :]
        acc_ref[3:4, :] += jnp.sum(dy, axis=0, keepdims=True)
        dyb = dy.astype(BF16)
        dy_ref[...] = dyb
        dpv = _dot_nt(dyb, wo_ref[...].reshape(W, W))

        hext[0:8, :] = jnp.where(blk > 0, hsh_ref[...], 0.0)
        hext[pl.ds(8, tt), :] = hs_ref[...]
        cw = cw_ref[...]
        lam = lam_ref[...]
        xc, gate_r, gate_i, a, mult = xc_ref[...], gr_ref[...], gi_ref[...], a_s[...], mu_ref[...]
        xcb = xc.astype(BF16)
        ls = jax.nn.log_sigmoid(lam)

        gbv = gb_ref[...]
        gate, th = _gelu(gbv)
        dgb = dpv * hs_ref[...] * _gelu_grad(gbv, th)
        dgbb = dgb.astype(BF16)
        dgb_ref[...] = dgbb
        sm_ref[9:10, :] += jnp.sum(dgb, axis=0, keepdims=True)
        dhs = dpv * gate

        b_s[...] = a * dhs
        qext[pl.ds(tt, 8), :] = qc[...]
        row = lax.broadcasted_iota(jnp.int32, (8, W), 0)

        def step(k, _):
            off = pl.multiple_of((tt // 8 - 1 - k) * 8, 8)
            A = a_s[pl.ds(off, 8), :]
            B = b_s[pl.ds(off, 8), :]
            for d in (1, 2, 4):
                keep = row < 8 - d
                Bs = jnp.where(keep, pltpu.roll(B, 8 - d, 0), 0.0)
                As = jnp.where(keep, pltpu.roll(A, 8 - d, 0), 1.0)
                B = B + A * Bs
                A = A * As
            Q = B + A * qc[...]
            qext[pl.ds(off, 8), :] = Q
            qc[...] = jnp.broadcast_to(Q[0:1, :], (8, W))
            return 0

        lax.fori_loop(0, tt // 8, step, 0)
        gsc = dhs + qext[pl.ds(1, tt), :]
        da = gsc * hext[pl.ds(7, tt), :]
        t1 = gsc * xc
        dmult = t1 * gate_i
        dgate_i = t1 * mult
        dxc = gsc * (mult * gate_i)
        dlog_a = da * a - dmult * (a * a) / mult
        dgate_r = dlog_a * (LRU_C * ls)
        sm_ref[7:8, :] += jnp.sum(dlog_a * (LRU_C * gate_r), axis=0, keepdims=True)
        dga = dgate_r * gate_r * (1.0 - gate_r)
        dgx = dgate_i * gate_i * (1.0 - gate_i)
        sm_ref[5:6, :] += jnp.sum(dga, axis=0, keepdims=True)
        sm_ref[6:7, :] += jnp.sum(dgx, axis=0, keepdims=True)
        dgab = dga.astype(BF16)
        dgxb = dgx.astype(BF16)
        dxc = dxc + _heads_dot(dgab, wa_ref, hd, nt=True) + _heads_dot(dgxb, wx_ref, hd, nt=True)
        for h in range(HEADS):
            sl = slice(h * hd, (h + 1) * hd)
            dwa_ref[h] += _dot_tn(xcb[:, sl], dgab[:, sl])
            dwx_ref[h] += _dot_tn(xcb[:, sl], dgxb[:, sl])
        sm_ref[4:5, :] += jnp.sum(dxc, axis=0, keepdims=True)
        dext[pl.ds(0, tt), :] = dxc
        dext[pl.ds(tt, 8), :] = dc[...]
        xrv = xr_ref[...]
        dxr = None
        for k in range(CONV_W):
            up = dext[pl.ds(CONV_W - 1 - k, tt), :]
            sm_ref[k:k + 1, :] += jnp.sum(up * xrv, axis=0, keepdims=True)
            dxr = cw[k:k + 1, :] * up if dxr is None else dxr + cw[k:k + 1, :] * up
        dc[...] = dext[0:8, :]
        sm_ref[8:9, :] += jnp.sum(dxr, axis=0, keepdims=True)
        dxrb = dxr.astype(BF16)
        dxr_ref[...] = dxrb

        dh = _dot_nt(dxrb, win_ref[...].reshape(W, W)) + _dot_nt(dgbb, wy_ref[...].reshape(W, W))
        g, sc = gm_ref[...], mod_ref[1:2, :]
        _, xhat, r, n = _rms_fwd(x_ref[...], g, sc, mod_ref[0:1, :])
        dx, dsh, dsc, dg = _rms_bwd(dh, xhat, r, n, g, sc)
        acc_ref[0:1, :] += dsh
        acc_ref[1:2, :] += dsc
        acc_ref[6:7, :] += dg
        dx_ref[...] = dx1v + dx

        @pl.when(i == nb - 1)
        def _():
            sm_ref[7:8, :] = sm_ref[7:8, :] * jax.nn.sigmoid(-lam)

    rev = lambda i: (nb - 1 - i, 0)
    tile = pl.BlockSpec((tt, W), rev)
    halo = pl.BlockSpec((8, W), lambda i: (jnp.maximum((nb - 1 - i) * (tt // 8) - 1, 0), 0))
    row = pl.BlockSpec((1, W), lambda i: (0, 0))
    wblk = pl.BlockSpec((HEADS, hd, hd), lambda i: (0, 0, 0))
    bf16o = jax.ShapeDtypeStruct((S, W), BF16)
    return _pcall(
        body, name="lru_bwd", grid=(nb,),
        out_shape=(bf16o, bf16o, bf16o, jax.ShapeDtypeStruct((S, W), F32),
                   jax.ShapeDtypeStruct((16, W), F32), jax.ShapeDtypeStruct((HEADS, hd, hd), F32),
                   jax.ShapeDtypeStruct((HEADS, hd, hd), F32), jax.ShapeDtypeStruct((8, W), F32)),
        in_specs=[tile, tile, tile, tile, tile, tile, halo, tile, tile, tile, tile, tile,
                  pl.BlockSpec((8, W), lambda i: (0, 0)), row,
                  _wspec(g_wout), _wspec(g_wy), _wspec(g_win), pl.BlockSpec((CONV_W, W), lambda i: (0, 0)),
                  wblk, wblk, row],
        out_specs=(tile, tile, tile, tile, pl.BlockSpec((16, W), lambda i: (0, 0)), wblk, wblk,
                   pl.BlockSpec((8, W), lambda i: (0, 0))),
        scratch_shapes=[pltpu.VMEM((tt + 8, W), F32), pltpu.VMEM((tt + 8, W), F32), pltpu.VMEM((tt + 8, W), F32),
                        pltpu.VMEM((tt, W), F32), pltpu.VMEM((8, W), F32), pltpu.VMEM((8, W), F32)],
        operands=(dx1, y, x, xr0, gb, hs, hs, xc_, gate_r_, gate_i_, a_, mult_, mod_l, g_mix, g_wout, g_wy, g_win,
                  cw, wa, wx, lam),
        **comm)


def _pool_bwd(dx1, x, pooled, mod_l, g_mix, pw, ps, h2, du, a, dz):
    S, D = x.shape
    tm = min(TP, S)
    nb = S // tm
    ng = len(POOL_WINDOWS)
    gd = D // ng
    n = tm + 24
    f4 = du.shape[1] // NQ
    assert nb % NQ == 0
    kch = nb // NQ
    kr = S // kch

    def body(dx1_ref, dxh_ref, x_ref, pl_ref, mod_ref, gm_ref, pw_ref, ps_ref, h2_ref, du_ref, a_ref, dz_ref,
             dx_ref, acc_ref, dpw_ref, dw1_ref, dw2_ref, ext, b1, b2, acc1, acc2):
        i = pl.program_id(0)

        @pl.when(i == 0)
        def _():
            acc_ref[...] = jnp.zeros_like(acc_ref)
            dpw_ref[...] = jnp.zeros_like(dpw_ref)

        @pl.when(i % kch == 0)
        def _():
            acc1[...] = jnp.zeros_like(acc1)
            acc2[...] = jnp.zeros_like(acc2)

        acc1[...] += _dot_tn(h2_ref[...], du_ref[...])
        av = a_ref[...]
        acc2[...] += _dot_tn(av * av, dz_ref[...])

        gt, psv = mod_ref[2:3, :], ps_ref[...]
        wvec = _window_vec(D)
        dx1v = dx1_ref[...]
        pooled = pl_ref[...]
        mixed = _heads_dot(pooled, pw_ref, gd)
        acc_ref[2:3, :] += jnp.sum(dx1v * (mixed * psv), axis=0, keepdims=True)
        dy = dx1v * gt
        acc_ref[3:4, :] += jnp.sum(dy * mixed, axis=0, keepdims=True)
        dmix = (dy * psv).astype(BF16)
        for gi in range(ng):
            sl = slice(gi * gd, (gi + 1) * gd)
            dpw_ref[gi] += _dot_tn(pooled[:, sl], dmix[:, sl])
        dpooled = _heads_dot(dmix, pw_ref, gd, nt=True)
        dmix_h = (dxh_ref[...] * gt * psv).astype(BF16)
        dpooled_h = jnp.where(i < nb - 1, _heads_dot(dmix_h, pw_ref, gd, nt=True), 0.0)
        t1 = (lax.broadcasted_iota(jnp.int32, (tm, 1), 0) + (i * tm + 1)).astype(F32)
        t1h = (lax.broadcasted_iota(jnp.int32, (16, 1), 0) + ((i + 1) * tm + 1)).astype(F32)
        zeros8 = jnp.zeros((8, D), F32)
        ext[pl.ds(0, tm), :] = dpooled / jnp.minimum(t1, wvec)
        ext[pl.ds(tm, 16), :] = dpooled_h / jnp.minimum(t1h, wvec)
        ext[pl.ds(tm + 16, 8), :] = zeros8
        b1[pl.ds(tm + 16, 8), :] = zeros8
        b2[pl.ds(tm + 16, 8), :] = zeros8
        m = n - 8
        b1[pl.ds(0, m), :] = ext[pl.ds(0, m), :] + ext[pl.ds(1, m), :]
        b2[pl.ds(0, m), gd:] = b1[pl.ds(0, m), gd:] + b1[pl.ds(2, m), gd:]
        b1[pl.ds(0, m), 2 * gd:] = b2[pl.ds(0, m), 2 * gd:] + b2[pl.ds(4, m), 2 * gd:]
        b2[pl.ds(0, m), 3 * gd:] = b1[pl.ds(0, m), 3 * gd:] + b1[pl.ds(8, m), 3 * gd:]
        wsum = jnp.concatenate([b1[pl.ds(0, tm), 0:gd], b2[pl.ds(0, tm), gd:2 * gd],
                                b1[pl.ds(0, tm), 2 * gd:3 * gd], b2[pl.ds(0, tm), 3 * gd:]], axis=1)
        dh = wsum - dpooled
        g, sc = gm_ref[...], mod_ref[1:2, :]
        _, xhat, r, nn = _rms_fwd(x_ref[...], g, sc, mod_ref[0:1, :])
        dx, dsh, dsc, dg = _rms_bwd(dh, xhat, r, nn, g, sc)
        acc_ref[0:1, :] += dsh
        acc_ref[1:2, :] += dsc
        acc_ref[6:7, :] += dg
        dx_ref[...] = dx1v + dx

        @pl.when(i % kch == kch - 1)
        def _():
            dw1_ref[...] = acc1[...].astype(BF16)
            dw2_ref[...] = acc2[...].astype(BF16)

    tile = pl.BlockSpec((tm, D), lambda i: (i, 0))
    halo = pl.BlockSpec((16, D), lambda i: (jnp.minimum((i + 1) * (tm // 16), S // 16 - 1), 0))
    row = pl.BlockSpec((1, D), lambda i: (0, 0))
    wblk = pl.BlockSpec((ng, gd, gd), lambda i: (0, 0, 0))
    full_k = pl.BlockSpec((kr, D), lambda i: (i % kch, 0))
    part_k = pl.BlockSpec((kr, f4), lambda i: (i % kch, i // kch))
    return pl.pallas_call(
        body, name="pool_bwd", grid=(nb,),
        out_shape=(jax.ShapeDtypeStruct((S, D), F32), jax.ShapeDtypeStruct((8, D), F32),
                   jax.ShapeDtypeStruct((ng, gd, gd), F32), jax.ShapeDtypeStruct((NQ, D, f4), BF16),
                   jax.ShapeDtypeStruct((NQ, f4, D), BF16)),
        in_specs=[tile, halo, tile, tile, pl.BlockSpec((8, D), lambda i: (0, 0)), row, wblk, row,
                  full_k, part_k, part_k, full_k],
        out_specs=(tile, pl.BlockSpec((8, D), lambda i: (0, 0)), wblk,
                   pl.BlockSpec((None, D, f4), lambda i: (i // kch, 0, 0)),
                   pl.BlockSpec((None, f4, D), lambda i: (i // kch, 0, 0))),
        scratch_shapes=[pltpu.VMEM((n, D), F32), pltpu.VMEM((n, D), F32), pltpu.VMEM((n, D), F32),
                        pltpu.VMEM((D, f4), F32), pltpu.VMEM((f4, D), F32)],
        compiler_params=_cparams("arbitrary"),
    )(dx1, dx1, x, pooled, mod_l, g_mix, pw, ps, h2, du, a, dz)


def _shard_to_rows(w, D):
    return w.reshape(-1, D)


def _blockdiag_full(gq, na, hd):
    return gq.reshape(NQ, na, HEADS, hd // NQ, hd).transpose(1, 2, 0, 3, 4).reshape(na, HEADS, hd, hd)


def _blockdiag_by_chip(dw, D):
    na, _, hd, _ = dw.shape
    return dw.reshape(na, HEADS, NQ, hd // NQ, hd).transpose(2, 0, 1, 3, 4).reshape(NQ, -1, D)


def kernel(x, c, w_mod, b_mod, norm_mix_g, norm_ffn_g, lru_w_y, lru_b_y, lru_w_in, lru_b_in, lru_conv_w, lru_conv_b, lru_w_a, lru_b_a, lru_w_x, lru_b_x, lru_lambda, lru_w_out, lru_b_out, pool_w, pool_scale, ffn_w1, ffn_w2, final_norm_g, loss_target, m_w_mod, m_b_mod, m_norm_mix_g, m_norm_ffn_g, m_lru_w_y, m_lru_b_y, m_lru_w_in, m_lru_b_in, m_lru_conv_w, m_lru_conv_b, m_lru_w_a, m_lru_b_a, m_lru_w_x, m_lru_b_x, m_lru_lambda, m_lru_w_out, m_lru_b_out, m_pool_w, m_pool_scale, m_ffn_w1, m_ffn_w2, m_final_norm_g, v_w_mod, v_b_mod, v_norm_mix_g, v_norm_ffn_g, v_lru_w_y, v_lru_b_y, v_lru_w_in, v_lru_b_in, v_lru_conv_w, v_lru_conv_b, v_lru_w_a, v_lru_b_a, v_lru_w_x, v_lru_b_x, v_lru_lambda, v_lru_w_out, v_lru_b_out, v_pool_w, v_pool_scale, v_ffn_w1, v_ffn_w2, v_final_norm_g):
    S, D = x.shape[1], x.shape[2]
    L = w_mod.shape[0]
    NA = lru_w_y.shape[0]
    NB = pool_w.shape[0]
    F = ffn_w1.shape[2] * NQ
    f4 = F // NQ
    hd = D // HEADS
    Cs = w_mod.shape[2]
    assert L == DEPTH and Cs * NQ == N_MOD * D and D % 1024 == 0
    x2d = x.reshape(S, D)
    tgt = loss_target.reshape(S, D)
    q = 2 * lax.axis_index("x") + lax.axis_index("y")

    big = [ffn_w1, ffn_w2, lru_w_y, lru_w_in, lru_w_out, lru_w_a, lru_w_x, pool_w]
    rows = [int(w.size) // D for w in big]
    offs = [sum(rows[:k]) for k in range(len(big))]
    O_W1, O_W2, O_WY, O_WIN, O_WOUT, O_WA, O_WX, O_PW = offs
    R = sum(rows)
    dq = D // NQ
    s_w1 = [ffn_w1[i].astype(BF16) for i in range(L)]
    s_w2 = [ffn_w2[i].astype(BF16) for i in range(L)]
    s_wy = [lru_w_y[j].astype(BF16) for j in range(NA)]
    s_win = [lru_w_in[j].astype(BF16) for j in range(NA)]
    s_wout = [lru_w_out[j].astype(BF16) for j in range(NA)]
    s_tiny = jnp.concatenate([_shard_to_rows(w, D) for w in (lru_w_a, lru_w_x, pool_w)], axis=0).astype(BF16)

    cshard = lru_conv_w.reshape(-1)
    small_fwd = jnp.concatenate([c.reshape(-1), cshard, lru_b_a.reshape(-1), lru_b_x.reshape(-1),
                                 pool_scale.reshape(-1)])
    small_fwd = jnp.pad(small_fwd, (0, 8 * D - small_fwd.shape[0])).reshape(8, D)

    g_w1, g_w2 = [None] * L, [None] * L
    g_wy, g_win, g_wout = [None] * NA, [None] * NA, [None] * NA
    SG, g_wy[0], g_win[0], g_wout[0], g_tiny = _comm_only("gather_first", small=small_fwd,
                                                         gathers=(s_wy[0], s_win[0], s_wout[0], s_tiny))
    wa_full = _blockdiag_full(g_tiny[:, :rows[5]], NA, hd)
    wx_full = _blockdiag_full(g_tiny[:, rows[5]:rows[5] + rows[6]], NA, hd)
    pw_full = _blockdiag_full(g_tiny[:, rows[5] + rows[6]:], NB, hd)
    SGf = SG.reshape(NDEV, 8 * D)
    c_all = SGf[:, :D]
    SGq = SGf.reshape(NQ, 2, 8 * D)[:, 0]
    o = D
    n_cw = NA * CONV_W * D // NQ
    conv_w_full = SGq[:, o:o + n_cw].reshape(NQ, NA, CONV_W, D // NQ).transpose(1, 2, 0, 3).reshape(NA, CONV_W, D)
    o += n_cw
    n_b = NA * HEADS * hd // NQ
    b_a_full = SGq[:, o:o + n_b].reshape(NQ, NA, HEADS, hd // NQ).transpose(1, 2, 0, 3).reshape(NA, 1, D)
    o += n_b
    b_x_full = SGq[:, o:o + n_b].reshape(NQ, NA, HEADS, hd // NQ).transpose(1, 2, 0, 3).reshape(NA, 1, D)
    o += n_b
    n_ps = NB * D // NQ
    pool_scale_full = SGq[:, o:o + n_ps].reshape(NQ, NB, D // NQ).transpose(1, 0, 2).reshape(NB, 1, D)


    b_mod_sh = lax.dynamic_slice_in_dim(b_mod, q * Cs, Cs, axis=1).reshape(L, 1, Cs)
    modpart = _mod_part(c_all, w_mod, b_mod_sh)
    modq = _exchange_mod(modpart.transpose(1, 0, 2))
    mod = modq.transpose(1, 0, 2).reshape(L, N_MOD, D)
    mod = jnp.pad(mod, ((0, 0), (0, 8 - N_MOD), (0, 0)))

    saved = []
    xcur = x2d
    for i in range(L):
        j = i // 2
        gm = norm_mix_g[i].reshape(1, D)
        gf = norm_ffn_g[i].reshape(1, D)
        if i % 2 == 0:
            h, gb, xr0, hs, p, y, x1, xc_s, gr_s, gi_s, a_sv, mu_s, g_w1[i], g_w2[i] = _lru_fwd(
                xcur, mod[i], gm, g_wy[j], g_win[j], lru_b_y[j].reshape(1, D), lru_b_in[j].reshape(1, D),
                conv_w_full[j], lru_conv_b[j].reshape(1, D), wa_full[j], b_a_full[j], wx_full[j], b_x_full[j],
                lru_lambda[j].reshape(1, D), g_wout[j], lru_b_out[j].reshape(1, D), gathers=(s_w1[i], s_w2[i]))
            h2, a, z, x2, g_w1[i + 1], g_w2[i + 1] = _ffn_fwd(x1, mod[i], gf, g_w1[i], g_w2[i],
                                                              gathers=(s_w1[i + 1], s_w2[i + 1]))
            saved.append(dict(x=xcur, h=h, gb=gb, xr0=xr0, hs=hs, p=p, y=y, x1=x1, h2=h2, a=a, z=z,
                              lru=(xc_s, gr_s, gi_s, a_sv, mu_s)))
        else:
            if j + 1 < NA:
                pooled, x1, h2, a, z, x2, g_wy[j + 1], g_win[j + 1], g_wout[j + 1] = _pool_mix_ffn_fwd(
                    xcur, mod[i], gm, pw_full[j], pool_scale_full[j], gf, g_w1[i], g_w2[i],
                    gathers=(s_wy[j + 1], s_win[j + 1], s_wout[j + 1]))
            else:
                pooled, x1, h2, a, z, x2 = _pool_mix_ffn_fwd(xcur, mod[i], gm, pw_full[j], pool_scale_full[j], gf,
                                                             g_w1[i], g_w2[i])
            saved.append(dict(x=xcur, pooled=pooled, x1=x1, h2=h2, a=a, z=z))
        xcur = x2

    dx = xcur
    qv = q.reshape(1).astype(jnp.int32)
    ppack = lax.empty((R, D), F32)
    psib = lax.empty((R, D), F32)
    pending, summed = [], []

    def comm_args():
        kw = {}
        if pending:
            kw["scatters"] = tuple(dw for dw, _ in pending)
        if summed:
            kw["sib"] = (ppack, psib, tuple(summed))
        return kw

    def after_host(extra):
        nonlocal ppack, psib, pending, summed
        had_sib = bool(summed)
        summed = []
        for (dw, off), rb in zip(pending, extra[:len(pending)]):
            ppack = _sum_into(ppack, dw, rb, off, qv)
            summed.append((off, dw.shape[1]))
        if had_sib:
            psib = extra[len(pending)]
        pending = []

    dmod_rows = [None] * L
    dg_mix = [None] * L
    dg_ffn = [None] * L
    d_small = {}
    dwa_l, dwx_l, dpw_l = [None] * NA, [None] * NA, [None] * NB
    for i in reversed(range(L)):
        j = i // 2
        sv = saved[i]
        gm = norm_mix_g[i].reshape(1, D)
        gf = norm_ffn_g[i].reshape(1, D)
        head = (final_norm_g.reshape(1, D), tgt) if i == L - 1 else None
        outs = _ffn_bwd(dx, sv["x1"], sv["a"], sv["z"], mod[i], gf, g_w1[i], g_w2[i], head=head, **comm_args())
        dx1, du, dz, facc = outs[:4]
        after_host(outs[4:])
        if head:
            loss = lax.psum(0.5 * jnp.sum(facc[0]) / D, ("x", "y", "c"))
            d_final_g = facc[1]
        if i % 2 == 0:
            pending.append((_dw_blocked(sv["h2"], du, False, False, "dw1"), O_W1 + i * D))
            pending.append((_dw_blocked(sv["a"], dz, True, True, "dw2"), O_W2 + i * f4))
            outs = _lru_bwd(dx1, sv["y"], sv["x"], sv["xr0"], sv["gb"], sv["hs"], *sv["lru"], mod[i], gm, g_wout[j],
                            g_wy[j], g_win[j], conv_w_full[j], wa_full[j], wx_full[j], lru_lambda[j].reshape(1, D),
                            **comm_args())
            dyp, dgb, dxr, dx, sm, dwa, dwx, macc = outs[:8]
            after_host(outs[8:])
            dwa_l[j], dwx_l[j] = dwa, dwx
            if i == 0:
                tiny = jnp.concatenate([_blockdiag_by_chip(jnp.stack(dwa_l), D), _blockdiag_by_chip(jnp.stack(dwx_l), D),
                                        _blockdiag_by_chip(jnp.stack(dpw_l), D)], axis=1).astype(BF16)
                pending.append((tiny, O_WA))
            outs = _dw_whole(sv["p"], [dyp], "dwout", **comm_args())
            after_host(outs[1:])
            pending.append((outs[0], O_WOUT + j * dq))
            outs = _dw_whole(sv["h"], [dgb, dxr], "dwy_dwin", **comm_args())
            after_host(outs[2:])
            pending.append((outs[0], O_WY + j * dq))
            pending.append((outs[1], O_WIN + j * dq))
            d_small[("lru", j)] = (sm, macc[3])
            dgt_m = macc[2]
        else:
            dx, macc, dpw, dw1, dw2 = _pool_bwd(dx1, sv["x"], sv["pooled"], mod[i], gm, pw_full[j], pool_scale_full[j],
                                                sv["h2"], du, sv["a"], dz)
            pending.append((dw1, O_W1 + i * D))
            pending.append((dw2, O_W2 + i * f4))
            dpw_l[j] = dpw
            d_small[("pool", j)] = macc[3]
            dgt_m = macc[2]
        dmod_rows[i] = jnp.stack([macc[0], macc[1], dgt_m, facc[3], facc[4], facc[5]])
        dg_mix[i] = macc[6]
        dg_ffn[i] = facc[7]
    grad_x = dx.reshape(x.shape)

    lru_sm = [d_small[("lru", j)] for j in range(NA)]
    small_rows = [jnp.stack(dmod_rows).reshape(L * N_MOD, D), jnp.stack(dg_mix), jnp.stack(dg_ffn),
                  jnp.stack([s[0][9] for s in lru_sm]), jnp.stack([s[0][8] for s in lru_sm]),
                  jnp.stack([s[0][4] for s in lru_sm]), jnp.stack([s[0][7] for s in lru_sm]),
                  jnp.stack([s[1] for s in lru_sm]),
                  jnp.stack([s[0][0:CONV_W] for s in lru_sm]).reshape(NA * CONV_W, D),
                  jnp.stack([s[0][5] for s in lru_sm]), jnp.stack([s[0][6] for s in lru_sm]),
                  jnp.stack([d_small[("pool", j)] for j in range(NB)]), d_final_g.reshape(1, D)]
    small_g = jnp.concatenate(small_rows, axis=0)
    n_small = small_g.shape[0]
    assert n_small <= SMALL_ROWS
    small_g = jnp.pad(small_g, ((0, SMALL_ROWS - n_small), (0, 0)))

    outs = _comm_only("scatter_last", small=small_g, reduce_small=True, **comm_args())
    sg_all, sg_sum = outs[:2]
    after_host(outs[2:])
    psum_mine = ppack
    psum_sib = _comm_only("sibling_last", sib=(ppack, psib, tuple(summed)))[0]

    def big_update(w, m, v, off, name):
        shp = w.shape
        g, dl, m2, v2 = _adam_rows(w.reshape(-1, D), m.reshape(-1, D), v.reshape(-1, D), psum_mine, psum_sib, off, name)
        return g.reshape(shp), dl.reshape(shp), m2.reshape(shp), v2.reshape(shp)

    res = {}
    res["ffn_w1"] = big_update(ffn_w1, m_ffn_w1, v_ffn_w1, O_W1, "adam_w1")
    res["ffn_w2"] = big_update(ffn_w2, m_ffn_w2, v_ffn_w2, O_W2, "adam_w2")
    res["lru_w_y"] = big_update(lru_w_y, m_lru_w_y, v_lru_w_y, O_WY, "adam_wy")
    res["lru_w_in"] = big_update(lru_w_in, m_lru_w_in, v_lru_w_in, O_WIN, "adam_win")
    res["lru_w_out"] = big_update(lru_w_out, m_lru_w_out, v_lru_w_out, O_WOUT, "adam_wout")

    def tiny_parts(w, off):
        n = int(w.size) // D
        return psum_mine[off:off + n].reshape(w.shape), psum_sib[off:off + n].reshape(w.shape)

    tiny_items = [("lru_w_a", lru_w_a, m_lru_w_a, v_lru_w_a) + tiny_parts(lru_w_a, O_WA),
                  ("lru_w_x", lru_w_x, m_lru_w_x, v_lru_w_x) + tiny_parts(lru_w_x, O_WX),
                  ("pool_w", pool_w, m_pool_w, v_pool_w) + tiny_parts(pool_w, O_PW)]

    dmod_all = sg_all[:, :L * N_MOD, :].reshape(NDEV, L, N_MOD * D)
    dmod_sh = lax.dynamic_slice_in_dim(dmod_all, q * Cs, Cs, axis=2).transpose(1, 0, 2)
    res["w_mod"] = _wmod_update(c_all.T, dmod_sh, w_mod, m_w_mod, v_w_mod)

    r0 = 0
    by_rows, names_a = [], []
    for name, w, m, v in (("b_mod", b_mod, m_b_mod, v_b_mod), ("norm_mix_g", norm_mix_g, m_norm_mix_g, v_norm_mix_g),
                          ("norm_ffn_g", norm_ffn_g, m_norm_ffn_g, v_norm_ffn_g),
                          ("lru_b_y", lru_b_y, m_lru_b_y, v_lru_b_y), ("lru_b_in", lru_b_in, m_lru_b_in, v_lru_b_in),
                          ("lru_conv_b", lru_conv_b, m_lru_conv_b, v_lru_conv_b),
                          ("lru_lambda", lru_lambda, m_lru_lambda, v_lru_lambda),
                          ("lru_b_out", lru_b_out, m_lru_b_out, v_lru_b_out)):
        by_rows.append((w, m, v, r0))
        names_a.append(name)
        r0 += int(w.size) // D
    g_conv_w = lax.dynamic_slice_in_dim(sg_sum[r0:r0 + NA * CONV_W].reshape(NA, CONV_W, D), q * dq, dq, axis=2)
    r0 += NA * CONV_W
    g_b_a = lax.dynamic_slice_in_dim(sg_sum[r0:r0 + NA].reshape(NA, HEADS, hd), q * (hd // NQ), hd // NQ, axis=2)
    r0 += NA
    g_b_x = lax.dynamic_slice_in_dim(sg_sum[r0:r0 + NA].reshape(NA, HEADS, hd), q * (hd // NQ), hd // NQ, axis=2)
    r0 += NA
    g_ps = lax.dynamic_slice_in_dim(sg_sum[r0:r0 + NB], q * dq, dq, axis=1)
    r0 += NB
    by_rows.append((final_norm_g.reshape(1, D), m_final_norm_g.reshape(1, D), v_final_norm_g.reshape(1, D), r0))
    names_a.append("final_norm_g")
    sliced = [(lru_conv_w, m_lru_conv_w, v_lru_conv_w, g_conv_w), (lru_b_a, m_lru_b_a, v_lru_b_a, g_b_a),
              (lru_b_x, m_lru_b_x, v_lru_b_x, g_b_x), (pool_scale, m_pool_scale, v_pool_scale, g_ps)]
    res_a, res_b, res_c = _adam_small(sg_sum, by_rows, sliced, [t[1:] for t in tiny_items])
    for name, r in zip(names_a, res_a):
        res[name] = r
    for t, r in zip(tiny_items, res_c):
        res[t[0]] = r
    res["final_norm_g"] = tuple(a.reshape(D) for a in res["final_norm_g"])
    for name, (_, _, _, g), r in zip(("lru_conv_w", "lru_b_a", "lru_b_x", "pool_scale"), sliced, res_b):
        res[name] = (g,) + r

    order = ["w_mod", "b_mod", "norm_mix_g", "norm_ffn_g", "lru_w_y", "lru_b_y", "lru_w_in", "lru_b_in", "lru_conv_w",
             "lru_conv_b", "lru_w_a", "lru_b_a", "lru_w_x", "lru_b_x", "lru_lambda", "lru_w_out", "lru_b_out", "pool_w",
             "pool_scale", "ffn_w1", "ffn_w2", "final_norm_g"]
    return (loss, grad_x, *[res[n][0] for n in order], *[res[n][1] for n in order],
            *[res[n][2] for n in order], *[res[n][3] for n in order])
```

```python
import jax
import jax.numpy as jnp
from jax import lax
from jax.experimental import pallas as pl
from jax.experimental.pallas import tpu as pltpu

F32 = jnp.float32
BF16 = jnp.bfloat16
MESH = pl.DeviceIdType.MESH

NQ = 4
NDEV = 8
DEPTH = 4
N_MOD = 6
HEADS = 4
CONV_W = 4
POOL_WINDOWS = (2, 4, 8, 16)
LRU_C = 8.0
EPS = 1e-6
ADAM_LR, ADAM_B1, ADAM_B2, ADAM_EPS, ADAM_WD, ADAM_STEP = 0.001, 0.9, 0.999, 1e-08, 0.01, 10

TM = 512
TT = 256
TP = 256
TPF = 512
TK = 2048
SMALL_ROWS = 64
FORWARD_STEPS = 4
VMEM_LIMIT = 60 * 1024 * 1024


def _cparams(*sem):
    return pltpu.CompilerParams(dimension_semantics=tuple(sem), vmem_limit_bytes=VMEM_LIMIT)


def _dot(a, b):
    return jnp.dot(a, b, preferred_element_type=F32)


def _dot_nt(a, b):
    return lax.dot_general(a, b, (((1,), (1,)), ((), ())), preferred_element_type=F32)


def _dot_tn(a, b):
    return lax.dot_general(a, b, (((0,), (0,)), ((), ())), preferred_element_type=F32)


def _resident(shape, index_map):
    return pl.BlockSpec(shape, index_map, pipeline_mode=pl.Buffered(1))


def _rms_fwd(x, g, sc, sh):
    r = lax.rsqrt(jnp.mean(x * x, axis=-1, keepdims=True) + EPS)
    xhat = x * r
    n = xhat * g
    return n * (1.0 + sc) + sh, xhat, r, n


def _rms_bwd(dh, xhat, r, n, g, sc):
    dsh = jnp.sum(dh, axis=0, keepdims=True)
    dsc = jnp.sum(dh * n, axis=0, keepdims=True)
    dn = dh * (1.0 + sc)
    dg = jnp.sum(dn * xhat, axis=0, keepdims=True)
    dxh = dn * g
    dx = r * (dxh - xhat * jnp.mean(dxh * xhat, axis=-1, keepdims=True))
    return dx, dsh, dsc, dg


_GELU_K = 0.7978845608028654
_GELU_C = 0.044715


def _gelu(x):
    t = jnp.tanh(_GELU_K * (x + _GELU_C * x * x * x))
    return 0.5 * x * (1.0 + t), t


def _gelu_grad(x, t):
    return 0.5 * (1.0 + t) + 0.5 * x * (1.0 - t * t) * (_GELU_K * (1.0 + 3.0 * _GELU_C * x * x))


def _neg_expm1(y, exp_y):
    series = -(y * (1.0 + y * (0.5 + y * (1.0 / 6.0))))
    return jnp.where(y > -(1.0 / 64.0), series, 1.0 - exp_y)


def _zero_first(ref):
    @pl.when(pl.program_id(0) == 0)
    def _():
        ref[...] = jnp.zeros_like(ref)


def _my_pos():
    return lax.axis_index("x"), lax.axis_index("y"), lax.axis_index("c")


def _dev_index(x, y, c):
    return 4 * x + 2 * y + c


def _chip_peers(x, y):
    return [(1 - x, y), (x, 1 - y), (1 - x, 1 - y)]


def _all_peers(x, y, c):
    return [(px, py, c) for (px, py) in _chip_peers(x, y)] + [(x, y, 1 - c)] + \
           [(px, py, 1 - c) for (px, py) in _chip_peers(x, y)]


def _comm_run(phase, x, y, c, gathers, scatters, sib, send, recv, loc):
    q = 2 * x + y
    peers = _chip_peers(x, y)
    sibling = (x, y, 1 - c)

    def rcopy(src, dst, s, dev):
        return pltpu.make_async_remote_copy(src, dst, send.at[s], recv.at[s], device_id=dev, device_id_type=MESH)

    s = 0
    for gi, (src, dst) in enumerate(gathers):
        half = src.shape[0] // 2
        mine, other = pl.ds(c * half, half), pl.ds((1 - c) * half, half)
        own = pltpu.make_async_copy(src, dst.at[q], loc.at[gi])
        if phase == "start":
            own.start()
        elif phase == "finish":
            own.wait()
        for (px, py) in peers:
            pq = 2 * px + py
            s_ici, s_fwd = s, s + 1
            s += 2
            if phase == "start":
                rcopy(src.at[mine], dst.at[q].at[mine], s_ici, (px, py, c)).start()
            elif phase == "forward":
                rcopy(src.at[mine], dst.at[pq].at[mine], s_ici, (px, py, c)).wait_recv()
                rcopy(dst.at[pq].at[mine], dst.at[pq].at[mine], s_fwd, sibling).start()
            else:
                rcopy(dst.at[pq].at[other], dst.at[pq].at[other], s_fwd, sibling).wait_recv()
                rcopy(src.at[mine], dst.at[q].at[mine], s_ici, (px, py, c)).wait_send()
                rcopy(dst.at[pq].at[mine], dst.at[pq].at[mine], s_fwd, sibling).wait_send()
    direct = []
    for (src, dst) in scatters:
        for k, (px, py) in enumerate(peers):
            direct.append((src.at[2 * px + py], dst.at[k], (px, py, c)))
    if sib is not None:
        src, dst, ranges = sib
        for (off, rows) in ranges:
            direct.append((src.at[pl.ds(off, rows)], dst.at[pl.ds(off, rows)], sibling))
    if phase == "start":
        for k, (a, b, dev) in enumerate(direct):
            rcopy(a, b, s + k, dev).start()
    elif phase == "finish":
        for k, (a, b, dev) in enumerate(direct):
            rcopy(a, b, s + k, dev).wait_recv()
        for k, (a, b, dev) in enumerate(direct):
            rcopy(a, b, s + k, dev).wait_send()


def _comm_shapes(gathers, scatters, sib):
    assert all(g.shape[0] % 32 == 0 for g in gathers)
    cin = list(gathers) + list(scatters) + ([sib[0], sib[1]] if sib else [])
    cout = [jax.ShapeDtypeStruct((NQ,) + g.shape, g.dtype) for g in gathers] + \
           [jax.ShapeDtypeStruct((3,) + s.shape[1:], s.dtype) for s in scatters] + \
           ([jax.ShapeDtypeStruct(sib[1].shape, sib[1].dtype)] if sib else [])
    n_rem = 6 * len(gathers) + 3 * len(scatters) + (len(sib[2]) if sib else 0)
    sems = [pltpu.SemaphoreType.DMA((max(n_rem, 1),)), pltpu.SemaphoreType.DMA((max(n_rem, 1),)),
            pltpu.SemaphoreType.DMA((max(len(gathers), 1),))]
    return cin, cout, sems


def _pcall(body, *, name, grid, in_specs, out_specs, out_shape, operands, scratch_shapes=(),
           gathers=(), scatters=(), sib=None):
    assert len(grid) == 1
    out_shape, out_specs = tuple(out_shape), tuple(out_specs)
    if not (gathers or scatters or sib):
        return pl.pallas_call(body, name=name, grid=grid, in_specs=list(in_specs), out_specs=out_specs,
                              out_shape=out_shape, scratch_shapes=list(scratch_shapes),
                              compiler_params=_cparams("arbitrary"))(*operands)
    cin, cout, sems = _comm_shapes(gathers, scatters, sib)
    n_in, n_cin, n_out, n_cout, n_scr = len(operands), len(cin), len(out_shape), len(cout), len(scratch_shapes)
    ng, ns = len(gathers), len(scatters)
    nsteps = grid[0]

    def wrapped(*refs):
        ins = refs[:n_in]
        cins = refs[n_in:n_in + n_cin]
        o0 = n_in + n_cin
        outs = refs[o0:o0 + n_out]
        couts = refs[o0 + n_out:o0 + n_out + n_cout]
        s0 = o0 + n_out + n_cout
        scr = refs[s0:s0 + n_scr]
        send, recv, loc = refs[s0 + n_scr:s0 + n_scr + 3]
        x, y, c = _my_pos()

        def run(phase):
            g = [(cins[k], couts[k]) for k in range(ng)]
            sc = [(cins[ng + k], couts[ng + k]) for k in range(ns)]
            sb = (cins[ng + ns], couts[ng + ns], sib[2]) if sib else None
            _comm_run(phase, x, y, c, g, sc, sb, send, recv, loc)

        @pl.when(pl.program_id(0) == 0)
        def _():
            run("start")

        if ng:
            @pl.when(pl.program_id(0) == max(nsteps - FORWARD_STEPS, 0))
            def _():
                run("forward")

        body(*ins, *outs, *scr)

        @pl.when(pl.program_id(0) == nsteps - 1)
        def _():
            run("finish")

    anyspec = pl.BlockSpec(memory_space=pl.ANY)
    aliases = {n_in + ng + ns + 1: n_out + ng + ns} if sib else {}
    return pl.pallas_call(
        wrapped, name=name, grid=grid,
        in_specs=list(in_specs) + [anyspec] * n_cin, out_specs=out_specs + (anyspec,) * n_cout,
        out_shape=out_shape + tuple(cout), scratch_shapes=list(scratch_shapes) + sems,
        input_output_aliases=aliases,
        compiler_params=pltpu.CompilerParams(dimension_semantics=("arbitrary",), vmem_limit_bytes=VMEM_LIMIT,
                                             has_side_effects=True),
    )(*operands, *cin)


def _comm_only(name, small=None, reduce_small=False, gathers=(), scatters=(), sib=None):
    cin, cout, sems = _comm_shapes(gathers, scatters, sib)
    n_cin, n_cout = len(cin), len(cout)
    ng, ns = len(gathers), len(scatters)
    n_sm_in = 1 if small is not None else 0
    n_sm_out = (2 if reduce_small else 1) if small is not None else 0

    def body(*refs):
        sm_in = refs[:n_sm_in]
        cins = refs[n_sm_in:n_sm_in + n_cin]
        o0 = n_sm_in + n_cin
        sm_out = refs[o0:o0 + n_sm_out]
        couts = refs[o0 + n_sm_out:o0 + n_sm_out + n_cout]
        s0 = o0 + n_sm_out + n_cout
        send, recv, loc = refs[s0:s0 + 3]
        x, y, c = _my_pos()
        g = [(cins[k], couts[k]) for k in range(ng)]
        sc = [(cins[ng + k], couts[ng + k]) for k in range(ns)]
        sb = (cins[ng + ns], couts[ng + ns], sib[2]) if sib else None
        _comm_run("start", x, y, c, g, sc, sb, send, recv, loc)
        if small is not None:
            sm_send, sm_recv = refs[s0 + 3:s0 + 5]
            small_ref, sg_ref = sm_in[0], sm_out[0]
            me = _dev_index(x, y, c)
            sg_ref[me] = small_ref[...]
            peers = _all_peers(x, y, c)
            sm = [pltpu.make_async_remote_copy(small_ref, sg_ref.at[me], sm_send.at[k], sm_recv.at[k],
                                               device_id=peer, device_id_type=MESH) for k, peer in enumerate(peers)]
            for cp in sm:
                cp.start()
            for k, (px, py, pc) in enumerate(peers):
                pltpu.make_async_remote_copy(small_ref, sg_ref.at[_dev_index(px, py, pc)], sm_send.at[k], sm_recv.at[k],
                                             device_id=(px, py, pc), device_id_type=MESH).wait_recv()
            if reduce_small:
                acc = sg_ref[0]
                for d in range(1, NDEV):
                    acc = acc + sg_ref[d]
                sm_out[1][...] = acc
            for cp in sm:
                cp.wait_send()
        if ng:
            _comm_run("forward", x, y, c, g, sc, sb, send, recv, loc)
        _comm_run("finish", x, y, c, g, sc, sb, send, recv, loc)

    anyspec = pl.BlockSpec(memory_space=pl.ANY)
    vspec = pl.BlockSpec(memory_space=pltpu.VMEM)
    sm_shapes = []
    if small is not None:
        sm_shapes.append(jax.ShapeDtypeStruct((NDEV,) + small.shape, small.dtype))
        if reduce_small:
            sm_shapes.append(jax.ShapeDtypeStruct(small.shape, small.dtype))
        sems = sems + [pltpu.SemaphoreType.DMA((NDEV - 1,)), pltpu.SemaphoreType.DMA((NDEV - 1,))]
    aliases = {n_sm_in + ng + ns + 1: n_sm_out + ng + ns} if sib else {}
    return pl.pallas_call(
        body, name=name,
        in_specs=[vspec] * n_sm_in + [anyspec] * n_cin,
        out_specs=tuple([vspec] * n_sm_out + [anyspec] * n_cout),
        out_shape=tuple(sm_shapes + cout), scratch_shapes=sems, input_output_aliases=aliases,
        compiler_params=pltpu.CompilerParams(has_side_effects=True),
    )(*([small] if small is not None else []), *cin)


def _exchange_mod(modpart):
    _, L, Cs = modpart.shape

    def body(part_ref, out_ref, send, recv):
        x, y, c = _my_pos()
        q = 2 * x + y
        me = _dev_index(x, y, c)
        out_ref[q] = part_ref[me]
        sends = []
        for k, (px, py) in enumerate(_chip_peers(x, y)):
            cp = pltpu.make_async_remote_copy(part_ref.at[_dev_index(px, py, c)], out_ref.at[q], send.at[k], recv.at[k],
                                              device_id=(px, py, c), device_id_type=MESH)
            cp.start()
            sends.append(cp)
        for k, (px, py) in enumerate(_chip_peers(x, y)):
            pltpu.make_async_remote_copy(part_ref.at[me], out_ref.at[2 * px + py], send.at[k], recv.at[k],
                                         device_id=(px, py, c), device_id_type=MESH).wait_recv()
        for cp in sends:
            cp.wait_send()

    return pl.pallas_call(
        body, name="exchange_mod",
        out_shape=jax.ShapeDtypeStruct((NQ, L, Cs), modpart.dtype),
        in_specs=[pl.BlockSpec(memory_space=pltpu.VMEM)],
        out_specs=pl.BlockSpec(memory_space=pltpu.VMEM),
        scratch_shapes=[pltpu.SemaphoreType.DMA((3,)), pltpu.SemaphoreType.DMA((3,))],
        compiler_params=pltpu.CompilerParams(has_side_effects=True),
    )(modpart)


def _mod_part(c_all, w_mod, b_mod_sh):
    L, D, Cs = w_mod.shape
    tn = 512 if Cs % 512 == 0 else Cs

    def body(c_ref, w_ref, b_ref, o_ref):
        cv = c_ref[...]
        cond = cv * jax.nn.sigmoid(cv)
        o_ref[...] = jnp.dot(cond, w_ref[...], preferred_element_type=F32, precision=lax.Precision.HIGHEST) + b_ref[...]

    return pl.pallas_call(
        body, name="mod_part", grid=(L, Cs // tn),
        out_shape=jax.ShapeDtypeStruct((L, NDEV, Cs), F32),
        in_specs=[pl.BlockSpec((NDEV, D), lambda i, j: (0, 0)),
                  pl.BlockSpec((None, D, tn), lambda i, j: (i, 0, j)),
                  pl.BlockSpec((None, 1, tn), lambda i, j: (i, 0, j))],
        out_specs=pl.BlockSpec((None, NDEV, tn), lambda i, j: (i, 0, j)),
        compiler_params=_cparams("parallel", "parallel"),
    )(c_all, w_mod, b_mod_sh)


def _adam(w, g, m, v):
    m2 = ADAM_B1 * m + (1.0 - ADAM_B1) * g
    v2 = ADAM_B2 * v + (1.0 - ADAM_B2) * (g * g)
    m_hat = m2 / (1.0 - ADAM_B1 ** ADAM_STEP)
    v_hat = v2 / (1.0 - ADAM_B2 ** ADAM_STEP)
    delta = -ADAM_LR * (m_hat / (jnp.sqrt(v_hat) + ADAM_EPS) + ADAM_WD * w)
    return delta, m2, v2


def _wmod_update(c_all_t, dmod_sh, w, m, v):
    L, D, Cs = w.shape
    td = 256 if D % 256 == 0 else D

    def body(ct_ref, d_ref, w_ref, m_ref, v_ref, g_ref, dl_ref, m2_ref, v2_ref):
        cv = ct_ref[...]
        cond = cv * jax.nn.sigmoid(cv)
        g = cond[:, 0:1] * d_ref[0:1, :]
        for b in range(1, NDEV):
            g = g + cond[:, b:b + 1] * d_ref[b:b + 1, :]
        g_ref[...] = g
        dl_ref[...], m2_ref[...], v2_ref[...] = _adam(w_ref[...], g, m_ref[...], v_ref[...])

    blk = pl.BlockSpec((None, td, Cs), lambda i, j: (i, j, 0))
    out = jax.ShapeDtypeStruct((L, D, Cs), F32)
    return pl.pallas_call(
        body, name="wmod_update", grid=(L, D // td),
        out_shape=(out, out, out, out),
        in_specs=[pl.BlockSpec((td, NDEV), lambda i, j: (j, 0)),
                  pl.BlockSpec((None, NDEV, Cs), lambda i, j: (i, 0, 0)), blk, blk, blk],
        out_specs=(blk, blk, blk, blk),
        compiler_params=_cparams("parallel", "parallel"),
    )(c_all_t, dmod_sh, w, m, v)


def _adam_rows(w, m, v, pa, pb, row_off, name):
    rows, C = w.shape
    tr = 512 if rows % 512 == 0 else (128 if rows % 128 == 0 else rows)
    assert row_off % tr == 0
    ob = row_off // tr

    def body(w_ref, m_ref, v_ref, pa_ref, pb_ref, g_ref, dl_ref, m2_ref, v2_ref):
        g = pa_ref[...] + pb_ref[...]
        g_ref[...] = g
        dl_ref[...], m2_ref[...], v2_ref[...] = _adam(w_ref[...], g, m_ref[...], v_ref[...])

    blk = pl.BlockSpec((tr, C), lambda i: (i, 0))
    pblk = pl.BlockSpec((tr, C), lambda i: (ob + i, 0))
    out = jax.ShapeDtypeStruct((rows, C), F32)
    return pl.pallas_call(
        body, name=name, grid=(rows // tr,), out_shape=(out, out, out, out),
        in_specs=[blk, blk, blk, pblk, pblk], out_specs=(blk, blk, blk, blk),
        compiler_params=_cparams("parallel"),
    )(w, m, v, pa, pb)


def _adam_small(sg_sum, by_rows, sliced, pairs):
    D = sg_sum.shape[1]
    na, nb, nc = len(by_rows), len(sliced), len(pairs)

    def body(*refs):
        sg = refs[0]
        ins_a = [refs[1 + 3 * t:4 + 3 * t] for t in range(na)]
        p = 1 + 3 * na
        ins_b = [refs[p + 4 * t:p + 4 * t + 4] for t in range(nb)]
        p += 4 * nb
        ins_c = [refs[p + 5 * t:p + 5 * t + 5] for t in range(nc)]
        p += 5 * nc
        outs_a = [refs[p + 4 * t:p + 4 * t + 4] for t in range(na)]
        p += 4 * na
        outs_b = [refs[p + 3 * t:p + 3 * t + 3] for t in range(nb)]
        p += 3 * nb
        outs_c = [refs[p + 4 * t:p + 4 * t + 4] for t in range(nc)]
        for (w_ref, m_ref, v_ref, ga_ref, gb_ref), (g_ref, dl_ref, m2_ref, v2_ref) in zip(ins_c, outs_c):
            g = ga_ref[...] + gb_ref[...]
            g_ref[...] = g
            dl_ref[...], m2_ref[...], v2_ref[...] = _adam(w_ref[...], g, m_ref[...], v_ref[...])
        for (w_ref, m_ref, v_ref), (g_ref, dl_ref, m2_ref, v2_ref), (w, _, _, row0) in zip(ins_a, outs_a, by_rows):
            n, k = w.shape[0], w.shape[1] // D
            pieces = [(slice(0, n), slice(0, D), slice(row0, row0 + n))] if k == 1 else \
                     [(slice(i, i + 1), slice(kk * D, (kk + 1) * D), slice(row0 + i * k + kk, row0 + i * k + kk + 1))
                      for i in range(n) for kk in range(k)]
            for rs, cs, gs in pieces:
                g = sg[gs, :]
                g_ref[rs, cs] = g
                dl_ref[rs, cs], m2_ref[rs, cs], v2_ref[rs, cs] = _adam(w_ref[rs, cs], g, m_ref[rs, cs], v_ref[rs, cs])
        for (w_ref, m_ref, v_ref, g_ref), (dl_ref, m2_ref, v2_ref) in zip(ins_b, outs_b):
            dl_ref[...], m2_ref[...], v2_ref[...] = _adam(w_ref[...], g_ref[...], m_ref[...], v_ref[...])

    operands = [sg_sum] + [a for t in by_rows for a in t[:3]] + [a for t in sliced for a in t] + \
               [a for t in pairs for a in t]
    out_shape = [jax.ShapeDtypeStruct(t[0].shape, F32) for t in by_rows for _ in range(4)] + \
                [jax.ShapeDtypeStruct(t[0].shape, F32) for t in sliced for _ in range(3)] + \
                [jax.ShapeDtypeStruct(t[0].shape, F32) for t in pairs for _ in range(4)]
    outs = pl.pallas_call(body, name="adam_small", out_shape=tuple(out_shape))(*operands)
    res_a = [tuple(outs[4 * t:4 * t + 4]) for t in range(na)]
    o = 4 * na
    res_b = [tuple(outs[o + 3 * t:o + 3 * t + 3]) for t in range(nb)]
    o += 3 * nb
    res_c = [tuple(outs[o + 4 * t:o + 4 * t + 4]) for t in range(nc)]
    return res_a, res_b, res_c


def _sum_into(ppack, dw, rb, off, qv):
    _, rows, D = dw.shape
    tr = 256 if rows % 256 == 0 else 128
    assert rows % tr == 0 and off % tr == 0
    ob = off // tr

    def body(q_ref, o_ref, r_ref, pin_ref, p_ref):
        acc = o_ref[...].astype(F32)
        for k in range(3):
            acc = acc + r_ref[k].astype(F32)
        p_ref[...] = acc

    return pl.pallas_call(
        body, name="sum_partials", out_shape=jax.ShapeDtypeStruct(ppack.shape, ppack.dtype),
        grid_spec=pltpu.PrefetchScalarGridSpec(
            num_scalar_prefetch=1, grid=(rows // tr,),
            in_specs=[pl.BlockSpec((None, tr, D), lambda i, q_ref: (q_ref[0], i, 0)),
                      pl.BlockSpec((3, tr, D), lambda i, q_ref: (0, i, 0)),
                      pl.BlockSpec(memory_space=pl.ANY)],
            out_specs=pl.BlockSpec((tr, D), lambda i, q_ref: (ob + i, 0))),
        input_output_aliases={3: 0},
        compiler_params=_cparams("parallel"),
    )(qv, dw, rb, ppack)


def _wspec(g):
    return _resident(g.shape, lambda i: (0, 0, 0))


def _ffn_fwd_inner(x1, mod_ref, gf_ref, w1_ref, w2_ref, h2_ref, a_ref, z_ref, x2_ref):
    h2 = _rms_fwd(x1, gf_ref[...], mod_ref[4:5, :], mod_ref[3:4, :])[0]
    h2b = h2.astype(BF16)
    h2_ref[...] = h2b
    f4 = w1_ref.shape[2]
    z = jnp.zeros(x1.shape, F32)
    for q in range(NQ):
        a = jnp.maximum(_dot(h2b, w1_ref[q]), 0.0)
        a_ref[:, q * f4:(q + 1) * f4] = a.astype(BF16)
        z = z + _dot((a * a).astype(BF16), w2_ref[q])
    z_ref[...] = z.astype(BF16)
    x2_ref[...] = x1 + mod_ref[5:6, :] * z


def _sigmoid(x):
    return 0.5 + 0.5 * jnp.tanh(0.5 * x)


def _heads_dot(xb, w_ref, hd, nt=False):
    outs = []
    for h in range(HEADS):
        xs = xb[:, h * hd:(h + 1) * hd]
        outs.append(_dot_nt(xs, w_ref[h]) if nt else _dot(xs, w_ref[h]))
    return jnp.concatenate(outs, axis=1)


def _lru_gates(xc, wa_ref, ba, wx_ref, bx, lam, hd):
    xcb = xc.astype(BF16)
    gate_r = _sigmoid(_heads_dot(xcb, wa_ref, hd) + ba)
    gate_i = _sigmoid(_heads_dot(xcb, wx_ref, hd) + bx)
    ls = jax.nn.log_sigmoid(lam)
    log_a = gate_r * (LRU_C * ls)
    a = jnp.exp(log_a)
    mult = jnp.sqrt(_neg_expm1(2.0 * log_a, a * a))
    return xcb, gate_r, gate_i, ls, a, mult


def _conv_taps(xext, cw, tt):
    acc = cw[0:1, :] * xext[pl.ds(8 - (CONV_W - 1), tt), :]
    for k in range(1, CONV_W):
        acc = acc + cw[k:k + 1, :] * xext[pl.ds(8 - (CONV_W - 1) + k, tt), :]
    return acc


def _lru_fwd(x, mod_l, g_mix, g_wy, g_win, b_y, b_in, cw, cb, wa, ba, wx, bx, lam, g_wout, b_out, **comm):
    S, W = x.shape
    tt = min(TT, S)
    hd = W // HEADS

    def body(x_ref, mod_ref, g_ref, wy_ref, win_ref, by_ref, bin_ref, cw_ref, cb_ref, wa_ref, ba_ref, wx_ref, bx_ref,
             lam_ref, wo_ref, bo_ref, h_ref, gb_ref, xr_ref, hs_ref, p_ref, y_ref, x1_ref,
             xc_ref, gr_ref, gi_ref, a_s, mu_ref, dgelu_ref, xext, u_s, carry):
        i = pl.program_id(0)

        @pl.when(i == 0)
        def _():
            carry[...] = jnp.zeros_like(carry)
            xext[0:8, :] = jnp.zeros((8, W), F32)

        @pl.when(i > 0)
        def _():
            xext[0:8, :] = xext[pl.ds(tt, 8), :]

        xv = x_ref[...]
        hb = _rms_fwd(xv, g_ref[...], mod_ref[1:2, :], mod_ref[0:1, :])[0].astype(BF16)
        h_ref[...] = hb
        gbv = _dot(hb, wy_ref[...].reshape(W, W)) + by_ref[...]
        gate, th = _gelu(gbv)
        gb_ref[...] = gate
        dgelu_ref[...] = _gelu_grad(gbv, th)
        xr = _dot(hb, win_ref[...].reshape(W, W)) + bin_ref[...]
        xr_ref[...] = xr
        xext[pl.ds(8, tt), :] = xr
        xc = _conv_taps(xext, cw_ref[...], tt) + cb_ref[...]
        _, gate_r, gate_i, _, a, mult = _lru_gates(xc, wa_ref, ba_ref[...], wx_ref, bx_ref[...], lam_ref[...], hd)
        xc_ref[...] = xc
        gr_ref[...] = gate_r
        gi_ref[...] = gate_i
        mu_ref[...] = mult
        a_s[...] = a
        u_s[...] = mult * (gate_i * xc)
        row = lax.broadcasted_iota(jnp.int32, (8, W), 0)

        def step(k, _):
            off = pl.multiple_of(k * 8, 8)
            A = a_s[pl.ds(off, 8), :]
            U = u_s[pl.ds(off, 8), :]
            for d in (1, 2, 4):
                keep = row >= d
                Us = jnp.where(keep, pltpu.roll(U, d, 0), 0.0)
                As = jnp.where(keep, pltpu.roll(A, d, 0), 1.0)
                U = U + A * Us
                A = A * As
            H = U + A * carry[...]
            hs_ref[pl.ds(off, 8), :] = H
            carry[...] = jnp.broadcast_to(H[7:8, :], (8, W))
            return 0

        lax.fori_loop(0, tt // 8, step, 0)
        pb = (hs_ref[...] * gate).astype(BF16)
        p_ref[...] = pb
        y = _dot(pb, wo_ref[...].reshape(W, W)) + bo_ref[...]
        y_ref[...] = y.astype(BF16)
        x1_ref[...] = xv + mod_ref[2:3, :] * y

    tile = pl.BlockSpec((tt, W), lambda i: (i, 0))
    row = pl.BlockSpec((1, W), lambda i: (0, 0))
    wblk = pl.BlockSpec((HEADS, hd, hd), lambda i: (0, 0, 0))
    f32o, bf16o = jax.ShapeDtypeStruct((S, W), F32), jax.ShapeDtypeStruct((S, W), BF16)
    return _pcall(
        body, name="lru_fwd", grid=(S // tt,),
        out_shape=(bf16o, f32o, f32o, f32o, bf16o, bf16o, f32o, f32o, f32o, f32o, f32o, f32o, f32o),
        in_specs=[tile, pl.BlockSpec((8, W), lambda i: (0, 0)), row, _wspec(g_wy), _wspec(g_win), row, row,
                  pl.BlockSpec((CONV_W, W), lambda i: (0, 0)), row, wblk, row, wblk, row, row, _wspec(g_wout), row],
        out_specs=(tile,) * 13,
        scratch_shapes=[pltpu.VMEM((tt + 8, W), F32), pltpu.VMEM((tt, W), F32), pltpu.VMEM((8, W), F32)],
        operands=(x, mod_l, g_mix, g_wy, g_win, b_y, b_in, cw, cb, wa, ba, wx, bx, lam, g_wout, b_out), **comm)


def _ffn_out_shapes(S, D, F):
    return (jax.ShapeDtypeStruct((S, D), BF16), jax.ShapeDtypeStruct((S, F), BF16),
            jax.ShapeDtypeStruct((S, D), BF16), jax.ShapeDtypeStruct((S, D), F32))


def _ffn_fwd(x1, mod_l, g_ffn, g_w1, g_w2, **comm):
    S, D = x1.shape
    tm = min(TM, S)
    F = g_w1.shape[2] * NQ

    def body(x1_ref, mod_ref, gf_ref, w1_ref, w2_ref, h2_ref, a_ref, z_ref, x2_ref):
        _ffn_fwd_inner(x1_ref[...], mod_ref, gf_ref, w1_ref, w2_ref, h2_ref, a_ref, z_ref, x2_ref)

    tile = pl.BlockSpec((tm, D), lambda i: (i, 0))
    row = pl.BlockSpec((1, D), lambda i: (0, 0))
    return _pcall(
        body, name="ffn_fwd", grid=(S // tm,),
        out_shape=_ffn_out_shapes(S, D, F),
        in_specs=[tile, pl.BlockSpec((8, D), lambda i: (0, 0)), row, _wspec(g_w1), _wspec(g_w2)],
        out_specs=(tile, pl.BlockSpec((tm, F), lambda i: (i, 0)), tile, tile),
        operands=(x1, mod_l, g_ffn, g_w1, g_w2), **comm)


def _window_vec(D):
    gd = D // len(POOL_WINDOWS)
    lane = lax.broadcasted_iota(jnp.int32, (1, D), 1)
    w = jnp.full((1, D), float(POOL_WINDOWS[0]), F32)
    for g in range(1, len(POOL_WINDOWS)):
        w = jnp.where(lane >= g * gd, float(POOL_WINDOWS[g]), w)
    return w


def _pool_mix_ffn_fwd(x, mod_l, g_mix, pw, ps, g_ffn, g_w1, g_w2, **comm):
    S, D = x.shape
    tm = min(TPF, S)
    F = g_w1.shape[2] * NQ
    gd = D // len(POOL_WINDOWS)
    n = tm + 24

    def body(x_ref, xh_ref, mod_ref, gm_ref, pw_ref, ps_ref, gf_ref, w1_ref, w2_ref,
             pl_ref, x1_ref, h2_ref, a_ref, z_ref, x2_ref, ext, b1, b2):
        i = pl.program_id(0)
        g, sc, sh = gm_ref[...], mod_ref[1:2, :], mod_ref[0:1, :]
        xv = x_ref[...]
        h = _rms_fwd(xv, g, sc, sh)[0]
        hh = _rms_fwd(xh_ref[...], g, sc, sh)[0]
        zeros8 = jnp.zeros((8, D), F32)
        ext[0:8, :] = zeros8
        b1[0:8, :] = zeros8
        b2[0:8, :] = zeros8
        ext[8:24, :] = jnp.where(i > 0, hh, 0.0)
        ext[pl.ds(24, tm), :] = h
        m = n - 8
        b1[pl.ds(8, m), :] = ext[pl.ds(8, m), :] + ext[pl.ds(7, m), :]
        b2[pl.ds(8, m), gd:] = b1[pl.ds(8, m), gd:] + b1[pl.ds(6, m), gd:]
        b1[pl.ds(8, m), 2 * gd:] = b2[pl.ds(8, m), 2 * gd:] + b2[pl.ds(4, m), 2 * gd:]
        b2[pl.ds(8, m), 3 * gd:] = b1[pl.ds(8, m), 3 * gd:] + b1[pl.ds(0, m), 3 * gd:]
        wsum = jnp.concatenate([b1[pl.ds(24, tm), 0:gd], b2[pl.ds(24, tm), gd:2 * gd],
                                b1[pl.ds(24, tm), 2 * gd:3 * gd], b2[pl.ds(24, tm), 3 * gd:]], axis=1)
        t1 = (lax.broadcasted_iota(jnp.int32, (tm, 1), 0) + (i * tm + 1)).astype(F32)
        cnt = jnp.minimum(t1, _window_vec(D))
        pooled = (wsum / cnt - h).astype(BF16)
        pl_ref[...] = pooled
        y = _heads_dot(pooled, pw_ref, gd) * ps_ref[...]
        x1 = xv + mod_ref[2:3, :] * y
        x1_ref[...] = x1
        _ffn_fwd_inner(x1, mod_ref, gf_ref, w1_ref, w2_ref, h2_ref, a_ref, z_ref, x2_ref)

    tile = pl.BlockSpec((tm, D), lambda i: (i, 0))
    halo = pl.BlockSpec((16, D), lambda i: (jnp.maximum(i * (tm // 16) - 1, 0), 0))
    row = pl.BlockSpec((1, D), lambda i: (0, 0))
    return _pcall(
        body, name="pool_mix_ffn_fwd", grid=(S // tm,),
        out_shape=(jax.ShapeDtypeStruct((S, D), BF16), jax.ShapeDtypeStruct((S, D), F32)) + _ffn_out_shapes(S, D, F),
        in_specs=[tile, halo, pl.BlockSpec((8, D), lambda i: (0, 0)), row,
                  pl.BlockSpec((len(POOL_WINDOWS), gd, gd), lambda i: (0, 0, 0)), row, row,
                  _wspec(g_w1), _wspec(g_w2)],
        out_specs=(tile, tile, tile, pl.BlockSpec((tm, F), lambda i: (i, 0)), tile, tile),
        scratch_shapes=[pltpu.VMEM((n, D), F32), pltpu.VMEM((n, D), F32), pltpu.VMEM((n, D), F32)],
        operands=(x, x, mod_l, g_mix, pw, ps, g_ffn, g_w1, g_w2), **comm)


def _loss_head(xv, gv, tv, acc_ref):
    D = xv.shape[1]
    r = lax.rsqrt(jnp.mean(xv * xv, axis=-1, keepdims=True) + EPS)
    xhat = xv * r
    err = xhat * gv - tv
    acc_ref[0:1, :] += jnp.sum(err * err, axis=0, keepdims=True)
    dy = err * (1.0 / D)
    acc_ref[1:2, :] += jnp.sum(dy * xhat, axis=0, keepdims=True)
    dxh = dy * gv
    return r * (dxh - xhat * jnp.mean(dxh * xhat, axis=-1, keepdims=True))


def _ffn_bwd(dx2, x1, a, z, mod_l, g_ffn, g_w1, g_w2, head=None, **comm):
    S, D = dx2.shape
    F = a.shape[1]
    f4 = F // NQ
    tm = min(TM, S)
    nh = 2 if head else 0

    def body(*refs):
        dx2_ref, x1_ref, a_ref, z_ref, mod_ref, gf_ref, w1_ref, w2_ref = refs[:8]
        dx1_ref, du_ref, dz_ref, acc_ref = refs[8 + nh:]
        _zero_first(acc_ref)
        dx2v = dx2_ref[...]
        if head:
            dx2v = _loss_head(dx2v, refs[8][...], refs[9][...], acc_ref)
        acc_ref[5:6, :] +=jnp.sum(dx2v * z_ref[...].astype(F32), axis=0, keepdims=True)
        dzb = (dx2v * mod_ref[5:6, :]).astype(BF16)
        dz_ref[...] = dzb
        dh2 = jnp.zeros((tm, D), F32)
        for q in range(NQ):
            av = a_ref[:, q * f4:(q + 1) * f4].astype(F32)
            du = (_dot_nt(dzb, w2_ref[q]) * (2.0 * av)).astype(BF16)
            du_ref[:, q * f4:(q + 1) * f4] = du
            dh2 = dh2 + _dot_nt(du, w1_ref[q])
        g, sc = gf_ref[...], mod_ref[4:5, :]
        _, xhat, r, n = _rms_fwd(x1_ref[...], g, sc, mod_ref[3:4, :])
        dx, dsh, dsc, dg = _rms_bwd(dh2, xhat, r, n, g, sc)
        acc_ref[3:4, :] += dsh
        acc_ref[4:5, :] += dsc
        acc_ref[7:8, :] += dg
        dx1_ref[...] = dx2v + dx

    tile = pl.BlockSpec((tm, D), lambda i: (i, 0))
    wide = pl.BlockSpec((tm, F), lambda i: (i, 0))
    return _pcall(
        body, name="ffn_bwd", grid=(S // tm,),
        out_shape=(jax.ShapeDtypeStruct((S, D), F32), jax.ShapeDtypeStruct((S, F), BF16),
                   jax.ShapeDtypeStruct((S, D), BF16), jax.ShapeDtypeStruct((8, D), F32)),
        in_specs=[tile, tile, wide, tile, pl.BlockSpec((8, D), lambda i: (0, 0)), pl.BlockSpec((1, D), lambda i: (0, 0)),
                  _wspec(g_w1), _wspec(g_w2)] + ([pl.BlockSpec((1, D), lambda i: (0, 0)), tile] if head else []),
        out_specs=(tile, wide, tile, pl.BlockSpec((8, D), lambda i: (0, 0))),
        operands=(dx2, x1, a, z, mod_l, g_ffn, g_w1, g_w2) + (tuple(head) if head else ()), **comm)


def _dw_blocked(a, b, by_rows, square_a, name):
    S = a.shape[0]
    tk = min(TK, S)
    nk = S // tk
    if by_rows:
        bm, bn = a.shape[1] // NQ, b.shape[1]
        a_map, b_map = (lambda q, k: (k, q)), (lambda q, k: (k, 0))
    else:
        bm, bn = a.shape[1], b.shape[1] // NQ
        a_map, b_map = (lambda q, k: (k, 0)), (lambda q, k: (k, q))

    def body(a_ref, b_ref, o_ref, acc):
        k = pl.program_id(1)

        @pl.when(k == 0)
        def _():
            acc[...] = jnp.zeros_like(acc)

        av = a_ref[...]
        if square_a:
            av = av * av
        acc[...] += _dot_tn(av, b_ref[...])

        @pl.when(k == nk - 1)
        def _():
            o_ref[...] = acc[...].astype(o_ref.dtype)

    return pl.pallas_call(
        body, name=name, grid=(NQ, nk),
        out_shape=jax.ShapeDtypeStruct((NQ, bm, bn), BF16),
        in_specs=[pl.BlockSpec((tk, bm), a_map), pl.BlockSpec((tk, bn), b_map)],
        out_specs=pl.BlockSpec((None, bm, bn), lambda q, k: (q, 0, 0)),
        scratch_shapes=[pltpu.VMEM((bm, bn), F32)],
        compiler_params=_cparams("parallel", "arbitrary"),
    )(a, b)


def _dw_whole(a, bs, name, **comm):
    S, M = a.shape
    N = bs[0].shape[1]
    tk = min(TK, S)
    nk = S // tk
    nb = len(bs)

    def body(*refs):
        a_ref, b_refs, o_refs, accs = refs[0], refs[1:1 + nb], refs[1 + nb:1 + 2 * nb], refs[1 + 2 * nb:]
        k = pl.program_id(0)

        @pl.when(k == 0)
        def _():
            for acc in accs:
                acc[...] = jnp.zeros_like(acc)

        av = a_ref[...]
        for b_ref, acc in zip(b_refs, accs):
            acc[...] += _dot_tn(av, b_ref[...])

        @pl.when(k == nk - 1)
        def _():
            for o_ref, acc in zip(o_refs, accs):
                o_ref[...] = acc[...].reshape(NQ, M // NQ, N).astype(o_ref.dtype)

    return _pcall(
        body, name=name, grid=(nk,),
        out_shape=tuple(jax.ShapeDtypeStruct((NQ, M // NQ, N), BF16) for _ in bs),
        in_specs=[pl.BlockSpec((tk, M), lambda k: (k, 0))] + [pl.BlockSpec((tk, N), lambda k: (k, 0)) for _ in bs],
        out_specs=tuple(pl.BlockSpec((NQ, M // NQ, N), lambda k: (0, 0, 0)) for _ in bs),
        scratch_shapes=[pltpu.VMEM((M, N), F32) for _ in bs],
        operands=(a, *bs), **comm)


def _lru_bwd(dx1, y, x, xr0, gate_, hs, xc_, gate_r_, gate_i_, a_, mult_, dgelu_, mod_l, g_mix, g_wout, g_wy, g_win,
             cw, wa, wx, lam, **comm):
    S, W = xr0.shape
    tt = min(TT, S)
    nb = S // tt
    hd = W // HEADS

    def body(dx1_ref, y_ref, x_ref, xr_ref, gate_ref, hs_ref, hsh_ref, xc_ref, gr_ref, gi_ref, a_s, mu_ref, dgelu_ref,
             mod_ref, gm_ref, wo_ref, wy_ref, win_ref, cw_ref, wa_ref, wx_ref, lam_ref,
             dy_ref, dgb_ref, dxr_ref, dx_ref, sm_ref, dwa_ref, dwx_ref, acc_ref,
             hext, qext, dext, b_s, qc, dc):
        i = pl.program_id(0)
        blk = nb - 1 - i

        @pl.when(i == 0)
        def _():
            sm_ref[...] = jnp.zeros_like(sm_ref)
            dwa_ref[...] = jnp.zeros_like(dwa_ref)
            dwx_ref[...] = jnp.zeros_like(dwx_ref)
            acc_ref[...] = jnp.zeros_like(acc_ref)
            qc[...] = jnp.zeros_like(qc)
            dc[...] = jnp.zeros_like(dc)

        dx1v = dx1_ref[...]
        acc_ref[2:3, :] += jnp.sum(dx1v * y_ref[...].astype(F32), axis=0, keepdims=True)
        dy = dx1v * mod_ref[2:3, :]
        acc_ref[3:4, :] += jnp.sum(dy, axis=0, keepdims=True)
        dyb = dy.astype(BF16)
        dy_ref[...] = dyb
        dpv = _dot_nt(dyb, wo_ref[...].reshape(W, W))

        hext[0:8, :] = jnp.where(blk > 0, hsh_ref[...], 0.0)
        hext[pl.ds(8, tt), :] = hs_ref[...]
        cw = cw_ref[...]
        lam = lam_ref[...]
        xc, gate_r, gate_i, a, mult = xc_ref[...], gr_ref[...], gi_ref[...], a_s[...], mu_ref[...]
        xcb = xc.astype(BF16)
        ls = jax.nn.log_sigmoid(lam)

        gate = gate_ref[...]
        dgb = dpv * hs_ref[...] * dgelu_ref[...]
        dgbb = dgb.astype(BF16)
        dgb_ref[...] = dgbb
        sm_ref[9:10, :] += jnp.sum(dgb, axis=0, keepdims=True)
        dhs = dpv * gate

        b_s[...] = a * dhs
        qext[pl.ds(tt, 8), :] = qc[...]
        row = lax.broadcasted_iota(jnp.int32, (8, W), 0)

        def step(k, _):
            off = pl.multiple_of((tt // 8 - 1 - k) * 8, 8)
            A = a_s[pl.ds(off, 8), :]
            B = b_s[pl.ds(off, 8), :]
            for d in (1, 2, 4):
                keep = row < 8 - d
                Bs = jnp.where(keep, pltpu.roll(B, 8 - d, 0), 0.0)
                As = jnp.where(keep, pltpu.roll(A, 8 - d, 0), 1.0)
                B = B + A * Bs
                A = A * As
            Q = B + A * qc[...]
            qext[pl.ds(off, 8), :] = Q
            qc[...] = jnp.broadcast_to(Q[0:1, :], (8, W))
            return 0

        lax.fori_loop(0, tt // 8, step, 0)
        gsc = dhs + qext[pl.ds(1, tt), :]
        da = gsc * hext[pl.ds(7, tt), :]
        t1 = gsc * xc
        dmult = t1 * gate_i
        dgate_i = t1 * mult
        dxc = gsc * (mult * gate_i)
        dlog_a = da * a - dmult * (a * a) / mult
        dgate_r = dlog_a * (LRU_C * ls)
        sm_ref[7:8, :] += jnp.sum(dlog_a * (LRU_C * gate_r), axis=0, keepdims=True)
        dga = dgate_r * gate_r * (1.0 - gate_r)
        dgx = dgate_i * gate_i * (1.0 - gate_i)
        sm_ref[5:6, :] += jnp.sum(dga, axis=0, keepdims=True)
        sm_ref[6:7, :] += jnp.sum(dgx, axis=0, keepdims=True)
        dgab = dga.astype(BF16)
        dgxb = dgx.astype(BF16)
        dxc = dxc + _heads_dot(dgab, wa_ref, hd, nt=True) + _heads_dot(dgxb, wx_ref, hd, nt=True)
        for h in range(HEADS):
            sl = slice(h * hd, (h + 1) * hd)
            dwa_ref[h] += _dot_tn(xcb[:, sl], dgab[:, sl])
            dwx_ref[h] += _dot_tn(xcb[:, sl], dgxb[:, sl])
        sm_ref[4:5, :] += jnp.sum(dxc, axis=0, keepdims=True)
        dext[pl.ds(0, tt), :] = dxc
        dext[pl.ds(tt, 8), :] = dc[...]
        xrv = xr_ref[...]
        dxr = None
        for k in range(CONV_W):
            up = dext[pl.ds(CONV_W - 1 - k, tt), :]
            sm_ref[k:k + 1, :] += jnp.sum(up * xrv, axis=0, keepdims=True)
            dxr = cw[k:k + 1, :] * up if dxr is None else dxr + cw[k:k + 1, :] * up
        dc[...] = dext[0:8, :]
        sm_ref[8:9, :] += jnp.sum(dxr, axis=0, keepdims=True)
        dxrb = dxr.astype(BF16)
        dxr_ref[...] = dxrb

        dh = _dot_nt(dxrb, win_ref[...].reshape(W, W)) + _dot_nt(dgbb, wy_ref[...].reshape(W, W))
        g, sc = gm_ref[...], mod_ref[1:2, :]
        _, xhat, r, n = _rms_fwd(x_ref[...], g, sc, mod_ref[0:1, :])
        dx, dsh, dsc, dg = _rms_bwd(dh, xhat, r, n, g, sc)
        acc_ref[0:1, :] += dsh
        acc_ref[1:2, :] += dsc
        acc_ref[6:7, :] += dg
        dx_ref[...] = dx1v + dx

        @pl.when(i == nb - 1)
        def _():
            sm_ref[7:8, :] = sm_ref[7:8, :] * jax.nn.sigmoid(-lam)

    rev = lambda i: (nb - 1 - i, 0)
    tile = pl.BlockSpec((tt, W), rev)
    halo = pl.BlockSpec((8, W), lambda i: (jnp.maximum((nb - 1 - i) * (tt // 8) - 1, 0), 0))
    row = pl.BlockSpec((1, W), lambda i: (0, 0))
    wblk = pl.BlockSpec((HEADS, hd, hd), lambda i: (0, 0, 0))
    bf16o = jax.ShapeDtypeStruct((S, W), BF16)
    return _pcall(
        body, name="lru_bwd", grid=(nb,),
        out_shape=(bf16o, bf16o, bf16o, jax.ShapeDtypeStruct((S, W), F32),
                   jax.ShapeDtypeStruct((16, W), F32), jax.ShapeDtypeStruct((HEADS, hd, hd), F32),
                   jax.ShapeDtypeStruct((HEADS, hd, hd), F32), jax.ShapeDtypeStruct((8, W), F32)),
        in_specs=[tile, tile, tile, tile, tile, tile, halo, tile, tile, tile, tile, tile, tile,
                  pl.BlockSpec((8, W), lambda i: (0, 0)), row,
                  _wspec(g_wout), _wspec(g_wy), _wspec(g_win), pl.BlockSpec((CONV_W, W), lambda i: (0, 0)),
                  wblk, wblk, row],
        out_specs=(tile, tile, tile, tile, pl.BlockSpec((16, W), lambda i: (0, 0)), wblk, wblk,
                   pl.BlockSpec((8, W), lambda i: (0, 0))),
        scratch_shapes=[pltpu.VMEM((tt + 8, W), F32), pltpu.VMEM((tt + 8, W), F32), pltpu.VMEM((tt + 8, W), F32),
                        pltpu.VMEM((tt, W), F32), pltpu.VMEM((8, W), F32), pltpu.VMEM((8, W), F32)],
        operands=(dx1, y, x, xr0, gate_, hs, hs, xc_, gate_r_, gate_i_, a_, mult_, dgelu_, mod_l, g_mix, g_wout, g_wy,
                  g_win, cw, wa, wx, lam),
        **comm)


def _pool_bwd(dx1, x, pooled, mod_l, g_mix, pw, ps, h2, du, a, dz):
    S, D = x.shape
    tm = min(TP, S)
    nb = S // tm
    ng = len(POOL_WINDOWS)
    gd = D // ng
    n = tm + 24
    f4 = du.shape[1] // NQ
    assert nb % NQ == 0
    kch = nb // NQ
    kr = S // kch

    def body(dx1_ref, dxh_ref, x_ref, pl_ref, mod_ref, gm_ref, pw_ref, ps_ref, h2_ref, du_ref, a_ref, dz_ref,
             dx_ref, acc_ref, dpw_ref, dw1_ref, dw2_ref, ext, b1, b2, acc1, acc2):
        i = pl.program_id(0)

        @pl.when(i == 0)
        def _():
            acc_ref[...] = jnp.zeros_like(acc_ref)
            dpw_ref[...] = jnp.zeros_like(dpw_ref)

        @pl.when(i % kch == 0)
        def _():
            acc1[...] = jnp.zeros_like(acc1)
            acc2[...] = jnp.zeros_like(acc2)

        acc1[...] += _dot_tn(h2_ref[...], du_ref[...])
        av = a_ref[...]
        acc2[...] += _dot_tn(av * av, dz_ref[...])

        gt, psv = mod_ref[2:3, :], ps_ref[...]
        wvec = _window_vec(D)
        dx1v = dx1_ref[...]
        pooled = pl_ref[...]
        mixed = _heads_dot(pooled, pw_ref, gd)
        acc_ref[2:3, :] += jnp.sum(dx1v * (mixed * psv), axis=0, keepdims=True)
        dy = dx1v * gt
        acc_ref[3:4, :] += jnp.sum(dy * mixed, axis=0, keepdims=True)
        dmix = (dy * psv).astype(BF16)
        for gi in range(ng):
            sl = slice(gi * gd, (gi + 1) * gd)
            dpw_ref[gi] += _dot_tn(pooled[:, sl], dmix[:, sl])
        dpooled = _heads_dot(dmix, pw_ref, gd, nt=True)
        dmix_h = (dxh_ref[...] * gt * psv).astype(BF16)
        dpooled_h = jnp.where(i < nb - 1, _heads_dot(dmix_h, pw_ref, gd, nt=True), 0.0)
        t1 = (lax.broadcasted_iota(jnp.int32, (tm, 1), 0) + (i * tm + 1)).astype(F32)
        t1h = (lax.broadcasted_iota(jnp.int32, (16, 1), 0) + ((i + 1) * tm + 1)).astype(F32)
        zeros8 = jnp.zeros((8, D), F32)
        ext[pl.ds(0, tm), :] = dpooled / jnp.minimum(t1, wvec)
        ext[pl.ds(tm, 16), :] = dpooled_h / jnp.minimum(t1h, wvec)
        ext[pl.ds(tm + 16, 8), :] = zeros8
        b1[pl.ds(tm + 16, 8), :] = zeros8
        b2[pl.ds(tm + 16, 8), :] = zeros8
        m = n - 8
        b1[pl.ds(0, m), :] = ext[pl.ds(0, m), :] + ext[pl.ds(1, m), :]
        b2[pl.ds(0, m), gd:] = b1[pl.ds(0, m), gd:] + b1[pl.ds(2, m), gd:]
        b1[pl.ds(0, m), 2 * gd:] = b2[pl.ds(0, m), 2 * gd:] + b2[pl.ds(4, m), 2 * gd:]
        b2[pl.ds(0, m), 3 * gd:] = b1[pl.ds(0, m), 3 * gd:] + b1[pl.ds(8, m), 3 * gd:]
        wsum = jnp.concatenate([b1[pl.ds(0, tm), 0:gd], b2[pl.ds(0, tm), gd:2 * gd],
                                b1[pl.ds(0, tm), 2 * gd:3 * gd], b2[pl.ds(0, tm), 3 * gd:]], axis=1)
        dh = wsum - dpooled
        g, sc = gm_ref[...], mod_ref[1:2, :]
        _, xhat, r, nn = _rms_fwd(x_ref[...], g, sc, mod_ref[0:1, :])
        dx, dsh, dsc, dg = _rms_bwd(dh, xhat, r, nn, g, sc)
        acc_ref[0:1, :] += dsh
        acc_ref[1:2, :] += dsc
        acc_ref[6:7, :] += dg
        dx_ref[...] = dx1v + dx

        @pl.when(i % kch == kch - 1)
        def _():
            dw1_ref[...] = acc1[...].astype(BF16)
            dw2_ref[...] = acc2[...].astype(BF16)

    tile = pl.BlockSpec((tm, D), lambda i: (i, 0))
    halo = pl.BlockSpec((16, D), lambda i: (jnp.minimum((i + 1) * (tm // 16), S // 16 - 1), 0))
    row = pl.BlockSpec((1, D), lambda i: (0, 0))
    wblk = pl.BlockSpec((ng, gd, gd), lambda i: (0, 0, 0))
    full_k = pl.BlockSpec((kr, D), lambda i: (i % kch, 0))
    part_k = pl.BlockSpec((kr, f4), lambda i: (i % kch, i // kch))
    return pl.pallas_call(
        body, name="pool_bwd", grid=(nb,),
        out_shape=(jax.ShapeDtypeStruct((S, D), F32), jax.ShapeDtypeStruct((8, D), F32),
                   jax.ShapeDtypeStruct((ng, gd, gd), F32), jax.ShapeDtypeStruct((NQ, D, f4), BF16),
                   jax.ShapeDtypeStruct((NQ, f4, D), BF16)),
        in_specs=[tile, halo, tile, tile, pl.BlockSpec((8, D), lambda i: (0, 0)), row, wblk, row,
                  full_k, part_k, part_k, full_k],
        out_specs=(tile, pl.BlockSpec((8, D), lambda i: (0, 0)), wblk,
                   pl.BlockSpec((None, D, f4), lambda i: (i // kch, 0, 0)),
                   pl.BlockSpec((None, f4, D), lambda i: (i // kch, 0, 0))),
        scratch_shapes=[pltpu.VMEM((n, D), F32), pltpu.VMEM((n, D), F32), pltpu.VMEM((n, D), F32),
                        pltpu.VMEM((D, f4), F32), pltpu.VMEM((f4, D), F32)],
        compiler_params=_cparams("arbitrary"),
    )(dx1, dx1, x, pooled, mod_l, g_mix, pw, ps, h2, du, a, dz)


def _shard_to_rows(w, D):
    return w.reshape(-1, D)


def _blockdiag_full(gq, na, hd):
    return gq.reshape(NQ, na, HEADS, hd // NQ, hd).transpose(1, 2, 0, 3, 4).reshape(na, HEADS, hd, hd)


def _blockdiag_by_chip(dw, D):
    na, _, hd, _ = dw.shape
    return dw.reshape(na, HEADS, NQ, hd // NQ, hd).transpose(2, 0, 1, 3, 4).reshape(NQ, -1, D)


def kernel(x, c, w_mod, b_mod, norm_mix_g, norm_ffn_g, lru_w_y, lru_b_y, lru_w_in, lru_b_in, lru_conv_w, lru_conv_b, lru_w_a, lru_b_a, lru_w_x, lru_b_x, lru_lambda, lru_w_out, lru_b_out, pool_w, pool_scale, ffn_w1, ffn_w2, final_norm_g, loss_target, m_w_mod, m_b_mod, m_norm_mix_g, m_norm_ffn_g, m_lru_w_y, m_lru_b_y, m_lru_w_in, m_lru_b_in, m_lru_conv_w, m_lru_conv_b, m_lru_w_a, m_lru_b_a, m_lru_w_x, m_lru_b_x, m_lru_lambda, m_lru_w_out, m_lru_b_out, m_pool_w, m_pool_scale, m_ffn_w1, m_ffn_w2, m_final_norm_g, v_w_mod, v_b_mod, v_norm_mix_g, v_norm_ffn_g, v_lru_w_y, v_lru_b_y, v_lru_w_in, v_lru_b_in, v_lru_conv_w, v_lru_conv_b, v_lru_w_a, v_lru_b_a, v_lru_w_x, v_lru_b_x, v_lru_lambda, v_lru_w_out, v_lru_b_out, v_pool_w, v_pool_scale, v_ffn_w1, v_ffn_w2, v_final_norm_g):
    S, D = x.shape[1], x.shape[2]
    L = w_mod.shape[0]
    NA = lru_w_y.shape[0]
    NB = pool_w.shape[0]
    F = ffn_w1.shape[2] * NQ
    f4 = F // NQ
    hd = D // HEADS
    Cs = w_mod.shape[2]
    assert L == DEPTH and Cs * NQ == N_MOD * D and D % 1024 == 0
    x2d = x.reshape(S, D)
    tgt = loss_target.reshape(S, D)
    q = 2 * lax.axis_index("x") + lax.axis_index("y")

    big = [ffn_w1, ffn_w2, lru_w_y, lru_w_in, lru_w_out, lru_w_a, lru_w_x, pool_w]
    rows = [int(w.size) // D for w in big]
    offs = [sum(rows[:k]) for k in range(len(big))]
    O_W1, O_W2, O_WY, O_WIN, O_WOUT, O_WA, O_WX, O_PW = offs
    R = sum(rows)
    dq = D // NQ
    s_w1 = [ffn_w1[i].astype(BF16) for i in range(L)]
    s_w2 = [ffn_w2[i].astype(BF16) for i in range(L)]
    s_wy = [lru_w_y[j].astype(BF16) for j in range(NA)]
    s_win = [lru_w_in[j].astype(BF16) for j in range(NA)]
    s_wout = [lru_w_out[j].astype(BF16) for j in range(NA)]
    s_tiny = jnp.concatenate([_shard_to_rows(w, D) for w in (lru_w_a, lru_w_x, pool_w)], axis=0).astype(BF16)

    cshard = lru_conv_w.reshape(-1)
    small_fwd = jnp.concatenate([c.reshape(-1), cshard, lru_b_a.reshape(-1), lru_b_x.reshape(-1),
                                 pool_scale.reshape(-1)])
    small_fwd = jnp.pad(small_fwd, (0, 8 * D - small_fwd.shape[0])).reshape(8, D)

    g_w1, g_w2 = [None] * L, [None] * L
    g_wy, g_win, g_wout = [None] * NA, [None] * NA, [None] * NA
    SG, g_wy[0], g_win[0], g_wout[0], g_tiny = _comm_only("gather_first", small=small_fwd,
                                                         gathers=(s_wy[0], s_win[0], s_wout[0], s_tiny))
    wa_full = _blockdiag_full(g_tiny[:, :rows[5]], NA, hd)
    wx_full = _blockdiag_full(g_tiny[:, rows[5]:rows[5] + rows[6]], NA, hd)
    pw_full = _blockdiag_full(g_tiny[:, rows[5] + rows[6]:], NB, hd)
    SGf = SG.reshape(NDEV, 8 * D)
    c_all = SGf[:, :D]
    SGq = SGf.reshape(NQ, 2, 8 * D)[:, 0]
    o = D
    n_cw = NA * CONV_W * D // NQ
    conv_w_full = SGq[:, o:o + n_cw].reshape(NQ, NA, CONV_W, D // NQ).transpose(1, 2, 0, 3).reshape(NA, CONV_W, D)
    o += n_cw
    n_b = NA * HEADS * hd // NQ
    b_a_full = SGq[:, o:o + n_b].reshape(NQ, NA, HEADS, hd // NQ).transpose(1, 2, 0, 3).reshape(NA, 1, D)
    o += n_b
    b_x_full = SGq[:, o:o + n_b].reshape(NQ, NA, HEADS, hd // NQ).transpose(1, 2, 0, 3).reshape(NA, 1, D)
    o += n_b
    n_ps = NB * D // NQ
    pool_scale_full = SGq[:, o:o + n_ps].reshape(NQ, NB, D // NQ).transpose(1, 0, 2).reshape(NB, 1, D)


    b_mod_sh = lax.dynamic_slice_in_dim(b_mod, q * Cs, Cs, axis=1).reshape(L, 1, Cs)
    modpart = _mod_part(c_all, w_mod, b_mod_sh)
    modq = _exchange_mod(modpart.transpose(1, 0, 2))
    mod = modq.transpose(1, 0, 2).reshape(L, N_MOD, D)
    mod = jnp.pad(mod, ((0, 0), (0, 8 - N_MOD), (0, 0)))

    saved = []
    xcur = x2d
    for i in range(L):
        j = i // 2
        gm = norm_mix_g[i].reshape(1, D)
        gf = norm_ffn_g[i].reshape(1, D)
        if i % 2 == 0:
            h, gb, xr0, hs, p, y, x1, xc_s, gr_s, gi_s, a_sv, mu_s, dge_s, g_w1[i], g_w2[i] = _lru_fwd(
                xcur, mod[i], gm, g_wy[j], g_win[j], lru_b_y[j].reshape(1, D), lru_b_in[j].reshape(1, D),
                conv_w_full[j], lru_conv_b[j].reshape(1, D), wa_full[j], b_a_full[j], wx_full[j], b_x_full[j],
                lru_lambda[j].reshape(1, D), g_wout[j], lru_b_out[j].reshape(1, D), gathers=(s_w1[i], s_w2[i]))
            h2, a, z, x2, g_w1[i + 1], g_w2[i + 1] = _ffn_fwd(x1, mod[i], gf, g_w1[i], g_w2[i],
                                                              gathers=(s_w1[i + 1], s_w2[i + 1]))
            saved.append(dict(x=xcur, h=h, gb=gb, xr0=xr0, hs=hs, p=p, y=y, x1=x1, h2=h2, a=a, z=z,
                              lru=(xc_s, gr_s, gi_s, a_sv, mu_s, dge_s)))
        else:
            if j + 1 < NA:
                pooled, x1, h2, a, z, x2, g_wy[j + 1], g_win[j + 1], g_wout[j + 1] = _pool_mix_ffn_fwd(
                    xcur, mod[i], gm, pw_full[j], pool_scale_full[j], gf, g_w1[i], g_w2[i],
                    gathers=(s_wy[j + 1], s_win[j + 1], s_wout[j + 1]))
            else:
                pooled, x1, h2, a, z, x2 = _pool_mix_ffn_fwd(xcur, mod[i], gm, pw_full[j], pool_scale_full[j], gf,
                                                             g_w1[i], g_w2[i])
            saved.append(dict(x=xcur, pooled=pooled, x1=x1, h2=h2, a=a, z=z))
        xcur = x2

    dx = xcur
    qv = q.reshape(1).astype(jnp.int32)
    ppack = lax.empty((R, D), F32)
    psib = lax.empty((R, D), F32)
    pending, summed = [], []

    def comm_args():
        kw = {}
        if pending:
            kw["scatters"] = tuple(dw for dw, _ in pending)
        if summed:
            kw["sib"] = (ppack, psib, tuple(summed))
        return kw

    def after_host(extra):
        nonlocal ppack, psib, pending, summed
        had_sib = bool(summed)
        summed = []
        for (dw, off), rb in zip(pending, extra[:len(pending)]):
            ppack = _sum_into(ppack, dw, rb, off, qv)
            summed.append((off, dw.shape[1]))
        if had_sib:
            psib = extra[len(pending)]
        pending = []

    dmod_rows = [None] * L
    dg_mix = [None] * L
    dg_ffn = [None] * L
    d_small = {}
    dwa_l, dwx_l, dpw_l = [None] * NA, [None] * NA, [None] * NB
    for i in reversed(range(L)):
        j = i // 2
        sv = saved[i]
        gm = norm_mix_g[i].reshape(1, D)
        gf = norm_ffn_g[i].reshape(1, D)
        head = (final_norm_g.reshape(1, D), tgt) if i == L - 1 else None
        outs = _ffn_bwd(dx, sv["x1"], sv["a"], sv["z"], mod[i], gf, g_w1[i], g_w2[i], head=head, **comm_args())
        dx1, du, dz, facc = outs[:4]
        after_host(outs[4:])
        if head:
            loss = lax.psum(0.5 * jnp.sum(facc[0]) / D, ("x", "y", "c"))
            d_final_g = facc[1]
        if i % 2 == 0:
            pending.append((_dw_blocked(sv["h2"], du, False, False, "dw1"), O_W1 + i * D))
            pending.append((_dw_blocked(sv["a"], dz, True, True, "dw2"), O_W2 + i * f4))
            outs = _lru_bwd(dx1, sv["y"], sv["x"], sv["xr0"], sv["gb"], sv["hs"], *sv["lru"], mod[i], gm, g_wout[j],
                            g_wy[j], g_win[j], conv_w_full[j], wa_full[j], wx_full[j], lru_lambda[j].reshape(1, D),
                            **comm_args())
            dyp, dgb, dxr, dx, sm, dwa, dwx, macc = outs[:8]
            after_host(outs[8:])
            dwa_l[j], dwx_l[j] = dwa, dwx
            if i == 0:
                tiny = jnp.concatenate([_blockdiag_by_chip(jnp.stack(dwa_l), D), _blockdiag_by_chip(jnp.stack(dwx_l), D),
                                        _blockdiag_by_chip(jnp.stack(dpw_l), D)], axis=1).astype(BF16)
                pending.append((tiny, O_WA))
            outs = _dw_whole(sv["p"], [dyp], "dwout", **comm_args())
            after_host(outs[1:])
            pending.append((outs[0], O_WOUT + j * dq))
            outs = _dw_whole(sv["h"], [dgb, dxr], "dwy_dwin", **comm_args())
            after_host(outs[2:])
            pending.append((outs[0], O_WY + j * dq))
            pending.append((outs[1], O_WIN + j * dq))
            d_small[("lru", j)] = (sm, macc[3])
            dgt_m = macc[2]
        else:
            dx, macc, dpw, dw1, dw2 = _pool_bwd(dx1, sv["x"], sv["pooled"], mod[i], gm, pw_full[j], pool_scale_full[j],
                                                sv["h2"], du, sv["a"], dz)
            pending.append((dw1, O_W1 + i * D))
            pending.append((dw2, O_W2 + i * f4))
            dpw_l[j] = dpw
            d_small[("pool", j)] = macc[3]
            dgt_m = macc[2]
        dmod_rows[i] = jnp.stack([macc[0], macc[1], dgt_m, facc[3], facc[4], facc[5]])
        dg_mix[i] = macc[6]
        dg_ffn[i] = facc[7]
    grad_x = dx.reshape(x.shape)

    lru_sm = [d_small[("lru", j)] for j in range(NA)]
    small_rows = [jnp.stack(dmod_rows).reshape(L * N_MOD, D), jnp.stack(dg_mix), jnp.stack(dg_ffn),
                  jnp.stack([s[0][9] for s in lru_sm]), jnp.stack([s[0][8] for s in lru_sm]),
                  jnp.stack([s[0][4] for s in lru_sm]), jnp.stack([s[0][7] for s in lru_sm]),
                  jnp.stack([s[1] for s in lru_sm]),
                  jnp.stack([s[0][0:CONV_W] for s in lru_sm]).reshape(NA * CONV_W, D),
                  jnp.stack([s[0][5] for s in lru_sm]), jnp.stack([s[0][6] for s in lru_sm]),
                  jnp.stack([d_small[("pool", j)] for j in range(NB)]), d_final_g.reshape(1, D)]
    small_g = jnp.concatenate(small_rows, axis=0)
    n_small = small_g.shape[0]
    assert n_small <= SMALL_ROWS
    small_g = jnp.pad(small_g, ((0, SMALL_ROWS - n_small), (0, 0)))

    outs = _comm_only("scatter_last", small=small_g, reduce_small=True, **comm_args())
    sg_all, sg_sum = outs[:2]
    after_host(outs[2:])
    psum_mine = ppack
    psum_sib = _comm_only("sibling_last", sib=(ppack, psib, tuple(summed)))[0]

    def big_update(w, m, v, off, name):
        shp = w.shape
        g, dl, m2, v2 = _adam_rows(w.reshape(-1, D), m.reshape(-1, D), v.reshape(-1, D), psum_mine, psum_sib, off, name)
        return g.reshape(shp), dl.reshape(shp), m2.reshape(shp), v2.reshape(shp)

    res = {}
    res["ffn_w1"] = big_update(ffn_w1, m_ffn_w1, v_ffn_w1, O_W1, "adam_w1")
    res["ffn_w2"] = big_update(ffn_w2, m_ffn_w2, v_ffn_w2, O_W2, "adam_w2")
    res["lru_w_y"] = big_update(lru_w_y, m_lru_w_y, v_lru_w_y, O_WY, "adam_wy")
    res["lru_w_in"] = big_update(lru_w_in, m_lru_w_in, v_lru_w_in, O_WIN, "adam_win")
    res["lru_w_out"] = big_update(lru_w_out, m_lru_w_out, v_lru_w_out, O_WOUT, "adam_wout")

    def tiny_parts(w, off):
        n = int(w.size) // D
        return psum_mine[off:off + n].reshape(w.shape), psum_sib[off:off + n].reshape(w.shape)

    tiny_items = [("lru_w_a", lru_w_a, m_lru_w_a, v_lru_w_a) + tiny_parts(lru_w_a, O_WA),
                  ("lru_w_x", lru_w_x, m_lru_w_x, v_lru_w_x) + tiny_parts(lru_w_x, O_WX),
                  ("pool_w", pool_w, m_pool_w, v_pool_w) + tiny_parts(pool_w, O_PW)]

    dmod_all = sg_all[:, :L * N_MOD, :].reshape(NDEV, L, N_MOD * D)
    dmod_sh = lax.dynamic_slice_in_dim(dmod_all, q * Cs, Cs, axis=2).transpose(1, 0, 2)
    res["w_mod"] = _wmod_update(c_all.T, dmod_sh, w_mod, m_w_mod, v_w_mod)

    r0 = 0
    by_rows, names_a = [], []
    for name, w, m, v in (("b_mod", b_mod, m_b_mod, v_b_mod), ("norm_mix_g", norm_mix_g, m_norm_mix_g, v_norm_mix_g),
                          ("norm_ffn_g", norm_ffn_g, m_norm_ffn_g, v_norm_ffn_g),
                          ("lru_b_y", lru_b_y, m_lru_b_y, v_lru_b_y), ("lru_b_in", lru_b_in, m_lru_b_in, v_lru_b_in),
                          ("lru_conv_b", lru_conv_b, m_lru_conv_b, v_lru_conv_b),
                          ("lru_lambda", lru_lambda, m_lru_lambda, v_lru_lambda),
                          ("lru_b_out", lru_b_out, m_lru_b_out, v_lru_b_out)):
        by_rows.append((w, m, v, r0))
        names_a.append(name)
        r0 += int(w.size) // D
    g_conv_w = lax.dynamic_slice_in_dim(sg_sum[r0:r0 + NA * CONV_W].reshape(NA, CONV_W, D), q * dq, dq, axis=2)
    r0 += NA * CONV_W
    g_b_a = lax.dynamic_slice_in_dim(sg_sum[r0:r0 + NA].reshape(NA, HEADS, hd), q * (hd // NQ), hd // NQ, axis=2)
    r0 += NA
    g_b_x = lax.dynamic_slice_in_dim(sg_sum[r0:r0 + NA].reshape(NA, HEADS, hd), q * (hd // NQ), hd // NQ, axis=2)
    r0 += NA
    g_ps = lax.dynamic_slice_in_dim(sg_sum[r0:r0 + NB], q * dq, dq, axis=1)
    r0 += NB
    by_rows.append((final_norm_g.reshape(1, D), m_final_norm_g.reshape(1, D), v_final_norm_g.reshape(1, D), r0))
    names_a.append("final_norm_g")
    sliced = [(lru_conv_w, m_lru_conv_w, v_lru_conv_w, g_conv_w), (lru_b_a, m_lru_b_a, v_lru_b_a, g_b_a),
              (lru_b_x, m_lru_b_x, v_lru_b_x, g_b_x), (pool_scale, m_pool_scale, v_pool_scale, g_ps)]
    res_a, res_b, res_c = _adam_small(sg_sum, by_rows, sliced, [t[1:] for t in tiny_items])
    for name, r in zip(names_a, res_a):
        res[name] = r
    for t, r in zip(tiny_items, res_c):
        res[t[0]] = r
    res["final_norm_g"] = tuple(a.reshape(D) for a in res["final_norm_g"])
    for name, (_, _, _, g), r in zip(("lru_conv_w", "lru_b_a", "lru_b_x", "pool_scale"), sliced, res_b):
        res[name] = (g,) + r

    order = ["w_mod", "b_mod", "norm_mix_g", "norm_ffn_g", "lru_w_y", "lru_b_y", "lru_w_in", "lru_b_in", "lru_conv_w",
             "lru_conv_b", "lru_w_a", "lru_b_a", "lru_w_x", "lru_b_x", "lru_lambda", "lru_w_out", "lru_b_out", "pool_w",
             "pool_scale", "ffn_w1", "ffn_w2", "final_norm_g"]
    return (loss, grad_x, *[res[n][0] for n in order], *[res[n][1] for n in order],
            *[res[n][2] for n in order], *[res[n][3] for n in order])
```

```python
import jax
import jax.numpy as jnp
from jax import lax
from jax.experimental import pallas as pl
from jax.experimental.pallas import tpu as pltpu

F32 = jnp.float32
BF16 = jnp.bfloat16
MESH = pl.DeviceIdType.MESH

NQ = 4
NDEV = 8
DEPTH = 4
N_MOD = 6
HEADS = 4
CONV_W = 4
POOL_WINDOWS = (2, 4, 8, 16)
LRU_C = 8.0
EPS = 1e-6
ADAM_LR, ADAM_B1, ADAM_B2, ADAM_EPS, ADAM_WD, ADAM_STEP = 0.001, 0.9, 0.999, 1e-08, 0.01, 10

TM = 512
TT = 256
TP = 256
TPF = 512
TK = 2048
SMALL_ROWS = 64
FORWARD_STEPS = 4
VMEM_LIMIT = 60 * 1024 * 1024


def _cparams(*sem):
    return pltpu.CompilerParams(dimension_semantics=tuple(sem), vmem_limit_bytes=VMEM_LIMIT)


def _dot(a, b):
    return jnp.dot(a, b, preferred_element_type=F32)


def _dot_nt(a, b):
    return lax.dot_general(a, b, (((1,), (1,)), ((), ())), preferred_element_type=F32)


def _dot_tn(a, b):
    return lax.dot_general(a, b, (((0,), (0,)), ((), ())), preferred_element_type=F32)


def _resident(shape, index_map):
    return pl.BlockSpec(shape, index_map, pipeline_mode=pl.Buffered(1))


def _rms_fwd(x, g, sc, sh):
    r = lax.rsqrt(jnp.mean(x * x, axis=-1, keepdims=True) + EPS)
    xhat = x * r
    n = xhat * g
    return n * (1.0 + sc) + sh, xhat, r, n


def _rms_bwd(dh, xhat, r, n, g, sc):
    dsh = jnp.sum(dh, axis=0, keepdims=True)
    dsc = jnp.sum(dh * n, axis=0, keepdims=True)
    dn = dh * (1.0 + sc)
    dg = jnp.sum(dn * xhat, axis=0, keepdims=True)
    dxh = dn * g
    dx = r * (dxh - xhat * jnp.mean(dxh * xhat, axis=-1, keepdims=True))
    return dx, dsh, dsc, dg


_GELU_K = 0.7978845608028654
_GELU_C = 0.044715


def _gelu(x):
    t = jnp.tanh(_GELU_K * (x + _GELU_C * x * x * x))
    return 0.5 * x * (1.0 + t), t


def _gelu_grad(x, t):
    return 0.5 * (1.0 + t) + 0.5 * x * (1.0 - t * t) * (_GELU_K * (1.0 + 3.0 * _GELU_C * x * x))


def _neg_expm1(y, exp_y):
    series = -(y * (1.0 + y * (0.5 + y * (1.0 / 6.0))))
    return jnp.where(y > -(1.0 / 64.0), series, 1.0 - exp_y)


def _zero_first(ref):
    @pl.when(pl.program_id(0) == 0)
    def _():
        ref[...] = jnp.zeros_like(ref)


def _my_pos():
    return lax.axis_index("x"), lax.axis_index("y"), lax.axis_index("c")


def _dev_index(x, y, c):
    return 4 * x + 2 * y + c


def _chip_peers(x, y):
    return [(1 - x, y), (x, 1 - y), (1 - x, 1 - y)]


def _all_peers(x, y, c):
    return [(px, py, c) for (px, py) in _chip_peers(x, y)] + [(x, y, 1 - c)] + \
           [(px, py, 1 - c) for (px, py) in _chip_peers(x, y)]


def _comm_run(phase, x, y, c, gathers, scatters, sib, send, recv, loc):
    q = 2 * x + y
    peers = _chip_peers(x, y)
    sibling = (x, y, 1 - c)

    def rcopy(src, dst, s, dev):
        return pltpu.make_async_remote_copy(src, dst, send.at[s], recv.at[s], device_id=dev, device_id_type=MESH)

    s = 0
    for gi, (src, dst) in enumerate(gathers):
        half = src.shape[0] // 2
        mine, other = pl.ds(c * half, half), pl.ds((1 - c) * half, half)
        own = pltpu.make_async_copy(src, dst.at[q], loc.at[gi])
        if phase == "start":
            own.start()
        elif phase == "finish":
            own.wait()
        for (px, py) in peers:
            pq = 2 * px + py
            s_ici, s_fwd = s, s + 1
            s += 2
            if phase == "start":
                rcopy(src.at[mine], dst.at[q].at[mine], s_ici, (px, py, c)).start()
            elif phase == "forward":
                rcopy(src.at[mine], dst.at[pq].at[mine], s_ici, (px, py, c)).wait_recv()
                rcopy(dst.at[pq].at[mine], dst.at[pq].at[mine], s_fwd, sibling).start()
            else:
                rcopy(dst.at[pq].at[other], dst.at[pq].at[other], s_fwd, sibling).wait_recv()
                rcopy(src.at[mine], dst.at[q].at[mine], s_ici, (px, py, c)).wait_send()
                rcopy(dst.at[pq].at[mine], dst.at[pq].at[mine], s_fwd, sibling).wait_send()
    direct = []
    for (src, dst) in scatters:
        for k, (px, py) in enumerate(peers):
            direct.append((src.at[2 * px + py], dst.at[k], (px, py, c)))
    if sib is not None:
        src, dst, ranges = sib
        for (off, rows) in ranges:
            direct.append((src.at[pl.ds(off, rows)], dst.at[pl.ds(off, rows)], sibling))
    if phase == "start":
        for k, (a, b, dev) in enumerate(direct):
            rcopy(a, b, s + k, dev).start()
    elif phase == "finish":
        for k, (a, b, dev) in enumerate(direct):
            rcopy(a, b, s + k, dev).wait_recv()
        for k, (a, b, dev) in enumerate(direct):
            rcopy(a, b, s + k, dev).wait_send()


def _comm_shapes(gathers, scatters, sib):
    assert all(g.shape[0] % 32 == 0 for g in gathers)
    cin = list(gathers) + list(scatters) + ([sib[0], sib[1]] if sib else [])
    cout = [jax.ShapeDtypeStruct((NQ,) + g.shape, g.dtype) for g in gathers] + \
           [jax.ShapeDtypeStruct((3,) + s.shape[1:], s.dtype) for s in scatters] + \
           ([jax.ShapeDtypeStruct(sib[1].shape, sib[1].dtype)] if sib else [])
    n_rem = 6 * len(gathers) + 3 * len(scatters) + (len(sib[2]) if sib else 0)
    sems = [pltpu.SemaphoreType.DMA((max(n_rem, 1),)), pltpu.SemaphoreType.DMA((max(n_rem, 1),)),
            pltpu.SemaphoreType.DMA((max(len(gathers), 1),))]
    return cin, cout, sems


def _pcall(body, *, name, grid, in_specs, out_specs, out_shape, operands, scratch_shapes=(),
           gathers=(), scatters=(), sib=None):
    assert len(grid) == 1
    out_shape, out_specs = tuple(out_shape), tuple(out_specs)
    if not (gathers or scatters or sib):
        return pl.pallas_call(body, name=name, grid=grid, in_specs=list(in_specs), out_specs=out_specs,
                              out_shape=out_shape, scratch_shapes=list(scratch_shapes),
                              compiler_params=_cparams("arbitrary"))(*operands)
    cin, cout, sems = _comm_shapes(gathers, scatters, sib)
    n_in, n_cin, n_out, n_cout, n_scr = len(operands), len(cin), len(out_shape), len(cout), len(scratch_shapes)
    ng, ns = len(gathers), len(scatters)
    nsteps = grid[0]

    def wrapped(*refs):
        ins = refs[:n_in]
        cins = refs[n_in:n_in + n_cin]
        o0 = n_in + n_cin
        outs = refs[o0:o0 + n_out]
        couts = refs[o0 + n_out:o0 + n_out + n_cout]
        s0 = o0 + n_out + n_cout
        scr = refs[s0:s0 + n_scr]
        send, recv, loc = refs[s0 + n_scr:s0 + n_scr + 3]
        x, y, c = _my_pos()

        def run(phase):
            g = [(cins[k], couts[k]) for k in range(ng)]
            sc = [(cins[ng + k], couts[ng + k]) for k in range(ns)]
            sb = (cins[ng + ns], couts[ng + ns], sib[2]) if sib else None
            _comm_run(phase, x, y, c, g, sc, sb, send, recv, loc)

        @pl.when(pl.program_id(0) == 0)
        def _():
            run("start")

        if ng:
            @pl.when(pl.program_id(0) == max(nsteps - FORWARD_STEPS, 0))
            def _():
                run("forward")

        body(*ins, *outs, *scr)

        @pl.when(pl.program_id(0) == nsteps - 1)
        def _():
            run("finish")

    anyspec = pl.BlockSpec(memory_space=pl.ANY)
    aliases = {n_in + ng + ns + 1: n_out + ng + ns} if sib else {}
    return pl.pallas_call(
        wrapped, name=name, grid=grid,
        in_specs=list(in_specs) + [anyspec] * n_cin, out_specs=out_specs + (anyspec,) * n_cout,
        out_shape=out_shape + tuple(cout), scratch_shapes=list(scratch_shapes) + sems,
        input_output_aliases=aliases,
        compiler_params=pltpu.CompilerParams(dimension_semantics=("arbitrary",), vmem_limit_bytes=VMEM_LIMIT,
                                             has_side_effects=True),
    )(*operands, *cin)


def _comm_only(name, small=None, reduce_small=False, gathers=(), scatters=(), sib=None):
    cin, cout, sems = _comm_shapes(gathers, scatters, sib)
    n_cin, n_cout = len(cin), len(cout)
    ng, ns = len(gathers), len(scatters)
    n_sm_in = 1 if small is not None else 0
    n_sm_out = (2 if reduce_small else 1) if small is not None else 0

    def body(*refs):
        sm_in = refs[:n_sm_in]
        cins = refs[n_sm_in:n_sm_in + n_cin]
        o0 = n_sm_in + n_cin
        sm_out = refs[o0:o0 + n_sm_out]
        couts = refs[o0 + n_sm_out:o0 + n_sm_out + n_cout]
        s0 = o0 + n_sm_out + n_cout
        send, recv, loc = refs[s0:s0 + 3]
        x, y, c = _my_pos()
        g = [(cins[k], couts[k]) for k in range(ng)]
        sc = [(cins[ng + k], couts[ng + k]) for k in range(ns)]
        sb = (cins[ng + ns], couts[ng + ns], sib[2]) if sib else None
        _comm_run("start", x, y, c, g, sc, sb, send, recv, loc)
        if small is not None:
            sm_send, sm_recv = refs[s0 + 3:s0 + 5]
            small_ref, sg_ref = sm_in[0], sm_out[0]
            me = _dev_index(x, y, c)
            sg_ref[me] = small_ref[...]
            peers = _all_peers(x, y, c)
            sm = [pltpu.make_async_remote_copy(small_ref, sg_ref.at[me], sm_send.at[k], sm_recv.at[k],
                                               device_id=peer, device_id_type=MESH) for k, peer in enumerate(peers)]
            for cp in sm:
                cp.start()
            for k, (px, py, pc) in enumerate(peers):
                pltpu.make_async_remote_copy(small_ref, sg_ref.at[_dev_index(px, py, pc)], sm_send.at[k], sm_recv.at[k],
                                             device_id=(px, py, pc), device_id_type=MESH).wait_recv()
            if reduce_small:
                acc = sg_ref[0]
                for d in range(1, NDEV):
                    acc = acc + sg_ref[d]
                sm_out[1][...] = acc
            for cp in sm:
                cp.wait_send()
        if ng:
            _comm_run("forward", x, y, c, g, sc, sb, send, recv, loc)
        _comm_run("finish", x, y, c, g, sc, sb, send, recv, loc)

    anyspec = pl.BlockSpec(memory_space=pl.ANY)
    vspec = pl.BlockSpec(memory_space=pltpu.VMEM)
    sm_shapes = []
    if small is not None:
        sm_shapes.append(jax.ShapeDtypeStruct((NDEV,) + small.shape, small.dtype))
        if reduce_small:
            sm_shapes.append(jax.ShapeDtypeStruct(small.shape, small.dtype))
        sems = sems + [pltpu.SemaphoreType.DMA((NDEV - 1,)), pltpu.SemaphoreType.DMA((NDEV - 1,))]
    aliases = {n_sm_in + ng + ns + 1: n_sm_out + ng + ns} if sib else {}
    return pl.pallas_call(
        body, name=name,
        in_specs=[vspec] * n_sm_in + [anyspec] * n_cin,
        out_specs=tuple([vspec] * n_sm_out + [anyspec] * n_cout),
        out_shape=tuple(sm_shapes + cout), scratch_shapes=sems, input_output_aliases=aliases,
        compiler_params=pltpu.CompilerParams(has_side_effects=True),
    )(*([small] if small is not None else []), *cin)


def _exchange_mod(modpart):
    _, L, Cs = modpart.shape

    def body(part_ref, out_ref, send, recv):
        x, y, c = _my_pos()
        q = 2 * x + y
        me = _dev_index(x, y, c)
        out_ref[q] = part_ref[me]
        sends = []
        for k, (px, py) in enumerate(_chip_peers(x, y)):
            cp = pltpu.make_async_remote_copy(part_ref.at[_dev_index(px, py, c)], out_ref.at[q], send.at[k], recv.at[k],
                                              device_id=(px, py, c), device_id_type=MESH)
            cp.start()
            sends.append(cp)
        for k, (px, py) in enumerate(_chip_peers(x, y)):
            pltpu.make_async_remote_copy(part_ref.at[me], out_ref.at[2 * px + py], send.at[k], recv.at[k],
                                         device_id=(px, py, c), device_id_type=MESH).wait_recv()
        for cp in sends:
            cp.wait_send()

    return pl.pallas_call(
        body, name="exchange_mod",
        out_shape=jax.ShapeDtypeStruct((NQ, L, Cs), modpart.dtype),
        in_specs=[pl.BlockSpec(memory_space=pltpu.VMEM)],
        out_specs=pl.BlockSpec(memory_space=pltpu.VMEM),
        scratch_shapes=[pltpu.SemaphoreType.DMA((3,)), pltpu.SemaphoreType.DMA((3,))],
        compiler_params=pltpu.CompilerParams(has_side_effects=True),
    )(modpart)


def _mod_part(c_all, w_mod, b_mod_sh):
    L, D, Cs = w_mod.shape
    tn = 512 if Cs % 512 == 0 else Cs

    def body(c_ref, w_ref, b_ref, o_ref):
        cv = c_ref[...]
        cond = cv * jax.nn.sigmoid(cv)
        o_ref[...] = jnp.dot(cond, w_ref[...], preferred_element_type=F32, precision=lax.Precision.HIGHEST) + b_ref[...]

    return pl.pallas_call(
        body, name="mod_part", grid=(L, Cs // tn),
        out_shape=jax.ShapeDtypeStruct((L, NDEV, Cs), F32),
        in_specs=[pl.BlockSpec((NDEV, D), lambda i, j: (0, 0)),
                  pl.BlockSpec((None, D, tn), lambda i, j: (i, 0, j)),
                  pl.BlockSpec((None, 1, tn), lambda i, j: (i, 0, j))],
        out_specs=pl.BlockSpec((None, NDEV, tn), lambda i, j: (i, 0, j)),
        compiler_params=_cparams("parallel", "parallel"),
    )(c_all, w_mod, b_mod_sh)


def _adam(w, g, m, v):
    m2 = ADAM_B1 * m + (1.0 - ADAM_B1) * g
    v2 = ADAM_B2 * v + (1.0 - ADAM_B2) * (g * g)
    m_hat = m2 / (1.0 - ADAM_B1 ** ADAM_STEP)
    v_hat = v2 / (1.0 - ADAM_B2 ** ADAM_STEP)
    delta = -ADAM_LR * (m_hat / (jnp.sqrt(v_hat) + ADAM_EPS) + ADAM_WD * w)
    return delta, m2, v2


def _wmod_update(c_all_t, dmod_sh, w, m, v):
    L, D, Cs = w.shape
    td = 256 if D % 256 == 0 else D

    def body(ct_ref, d_ref, w_ref, m_ref, v_ref, g_ref, dl_ref, m2_ref, v2_ref):
        cv = ct_ref[...]
        cond = cv * jax.nn.sigmoid(cv)
        g = cond[:, 0:1] * d_ref[0:1, :]
        for b in range(1, NDEV):
            g = g + cond[:, b:b + 1] * d_ref[b:b + 1, :]
        g_ref[...] = g
        dl_ref[...], m2_ref[...], v2_ref[...] = _adam(w_ref[...], g, m_ref[...], v_ref[...])

    blk = pl.BlockSpec((None, td, Cs), lambda i, j: (i, j, 0))
    out = jax.ShapeDtypeStruct((L, D, Cs), F32)
    return pl.pallas_call(
        body, name="wmod_update", grid=(L, D // td),
        out_shape=(out, out, out, out),
        in_specs=[pl.BlockSpec((td, NDEV), lambda i, j: (j, 0)),
                  pl.BlockSpec((None, NDEV, Cs), lambda i, j: (i, 0, 0)), blk, blk, blk],
        out_specs=(blk, blk, blk, blk),
        compiler_params=_cparams("parallel", "parallel"),
    )(c_all_t, dmod_sh, w, m, v)


def _adam_rows(w, m, v, pa, pb, row_off, name):
    rows, C = w.shape
    tr = 512 if rows % 512 == 0 else (128 if rows % 128 == 0 else rows)
    assert row_off % tr == 0
    ob = row_off // tr

    def body(w_ref, m_ref, v_ref, pa_ref, pb_ref, g_ref, dl_ref, m2_ref, v2_ref):
        g = pa_ref[...] + pb_ref[...]
        g_ref[...] = g
        dl_ref[...], m2_ref[...], v2_ref[...] = _adam(w_ref[...], g, m_ref[...], v_ref[...])

    blk = pl.BlockSpec((tr, C), lambda i: (i, 0))
    pblk = pl.BlockSpec((tr, C), lambda i: (ob + i, 0))
    out = jax.ShapeDtypeStruct((rows, C), F32)
    return pl.pallas_call(
        body, name=name, grid=(rows // tr,), out_shape=(out, out, out, out),
        in_specs=[blk, blk, blk, pblk, pblk], out_specs=(blk, blk, blk, blk),
        compiler_params=_cparams("parallel"),
    )(w, m, v, pa, pb)


def _adam_small(sg_sum, by_rows, sliced, pairs):
    D = sg_sum.shape[1]
    na, nb, nc = len(by_rows), len(sliced), len(pairs)

    def body(*refs):
        sg = refs[0]
        ins_a = [refs[1 + 3 * t:4 + 3 * t] for t in range(na)]
        p = 1 + 3 * na
        ins_b = [refs[p + 4 * t:p + 4 * t + 4] for t in range(nb)]
        p += 4 * nb
        ins_c = [refs[p + 5 * t:p + 5 * t + 5] for t in range(nc)]
        p += 5 * nc
        outs_a = [refs[p + 4 * t:p + 4 * t + 4] for t in range(na)]
        p += 4 * na
        outs_b = [refs[p + 3 * t:p + 3 * t + 3] for t in range(nb)]
        p += 3 * nb
        outs_c = [refs[p + 4 * t:p + 4 * t + 4] for t in range(nc)]
        for (w_ref, m_ref, v_ref, ga_ref, gb_ref), (g_ref, dl_ref, m2_ref, v2_ref) in zip(ins_c, outs_c):
            g = ga_ref[...] + gb_ref[...]
            g_ref[...] = g
            dl_ref[...], m2_ref[...], v2_ref[...] = _adam(w_ref[...], g, m_ref[...], v_ref[...])
        for (w_ref, m_ref, v_ref), (g_ref, dl_ref, m2_ref, v2_ref), (w, _, _, row0) in zip(ins_a, outs_a, by_rows):
            n, k = w.shape[0], w.shape[1] // D
            pieces = [(slice(0, n), slice(0, D), slice(row0, row0 + n))] if k == 1 else \
                     [(slice(i, i + 1), slice(kk * D, (kk + 1) * D), slice(row0 + i * k + kk, row0 + i * k + kk + 1))
                      for i in range(n) for kk in range(k)]
            for rs, cs, gs in pieces:
                g = sg[gs, :]
                g_ref[rs, cs] = g
                dl_ref[rs, cs], m2_ref[rs, cs], v2_ref[rs, cs] = _adam(w_ref[rs, cs], g, m_ref[rs, cs], v_ref[rs, cs])
        for (w_ref, m_ref, v_ref, g_ref), (dl_ref, m2_ref, v2_ref) in zip(ins_b, outs_b):
            dl_ref[...], m2_ref[...], v2_ref[...] = _adam(w_ref[...], g_ref[...], m_ref[...], v_ref[...])

    operands = [sg_sum] + [a for t in by_rows for a in t[:3]] + [a for t in sliced for a in t] + \
               [a for t in pairs for a in t]
    out_shape = [jax.ShapeDtypeStruct(t[0].shape, F32) for t in by_rows for _ in range(4)] + \
                [jax.ShapeDtypeStruct(t[0].shape, F32) for t in sliced for _ in range(3)] + \
                [jax.ShapeDtypeStruct(t[0].shape, F32) for t in pairs for _ in range(4)]
    outs = pl.pallas_call(body, name="adam_small", out_shape=tuple(out_shape))(*operands)
    res_a = [tuple(outs[4 * t:4 * t + 4]) for t in range(na)]
    o = 4 * na
    res_b = [tuple(outs[o + 3 * t:o + 3 * t + 3]) for t in range(nb)]
    o += 3 * nb
    res_c = [tuple(outs[o + 4 * t:o + 4 * t + 4]) for t in range(nc)]
    return res_a, res_b, res_c


def _sum_into(ppack, dw, rb, off, qv):
    _, rows, D = dw.shape
    tr = 256 if rows % 256 == 0 else 128
    assert rows % tr == 0 and off % tr == 0
    ob = off // tr

    def body(q_ref, o_ref, r_ref, pin_ref, p_ref):
        acc = o_ref[...].astype(F32)
        for k in range(3):
            acc = acc + r_ref[k].astype(F32)
        p_ref[...] = acc

    return pl.pallas_call(
        body, name="sum_partials", out_shape=jax.ShapeDtypeStruct(ppack.shape, ppack.dtype),
        grid_spec=pltpu.PrefetchScalarGridSpec(
            num_scalar_prefetch=1, grid=(rows // tr,),
            in_specs=[pl.BlockSpec((None, tr, D), lambda i, q_ref: (q_ref[0], i, 0)),
                      pl.BlockSpec((3, tr, D), lambda i, q_ref: (0, i, 0)),
                      pl.BlockSpec(memory_space=pl.ANY)],
            out_specs=pl.BlockSpec((tr, D), lambda i, q_ref: (ob + i, 0))),
        input_output_aliases={3: 0},
        compiler_params=_cparams("parallel"),
    )(qv, dw, rb, ppack)


def _wspec(g):
    return _resident(g.shape, lambda i: (0, 0, 0))


def _ffn_fwd_inner(x1, mod_ref, gf_ref, w1_ref, w2_ref, h2_ref, a_ref, z_ref, x2_ref):
    h2 = _rms_fwd(x1, gf_ref[...], mod_ref[4:5, :], mod_ref[3:4, :])[0]
    h2b = h2.astype(BF16)
    h2_ref[...] = h2b
    f4 = w1_ref.shape[2]
    z = jnp.zeros(x1.shape, F32)
    for q in range(NQ):
        a = jnp.maximum(_dot(h2b, w1_ref[q]), 0.0)
        a_ref[:, q * f4:(q + 1) * f4] = a.astype(BF16)
        z = z + _dot((a * a).astype(BF16), w2_ref[q])
    z_ref[...] = z.astype(BF16)
    x2_ref[...] = x1 + mod_ref[5:6, :] * z


def _sigmoid(x):
    return 0.5 + 0.5 * jnp.tanh(0.5 * x)


def _heads_dot(xb, w_ref, hd, nt=False):
    outs = []
    for h in range(HEADS):
        xs = xb[:, h * hd:(h + 1) * hd]
        outs.append(_dot_nt(xs, w_ref[h]) if nt else _dot(xs, w_ref[h]))
    return jnp.concatenate(outs, axis=1)


def _lru_gates(xc, wa_ref, ba, wx_ref, bx, lam, hd):
    xcb = xc.astype(BF16)
    gate_r = _sigmoid(_heads_dot(xcb, wa_ref, hd) + ba)
    gate_i = _sigmoid(_heads_dot(xcb, wx_ref, hd) + bx)
    ls = jax.nn.log_sigmoid(lam)
    log_a = gate_r * (LRU_C * ls)
    a = jnp.exp(log_a)
    mult = jnp.sqrt(_neg_expm1(2.0 * log_a, a * a))
    return xcb, gate_r, gate_i, ls, a, mult


def _conv_taps(xext, cw, tt):
    acc = cw[0:1, :] * xext[pl.ds(8 - (CONV_W - 1), tt), :]
    for k in range(1, CONV_W):
        acc = acc + cw[k:k + 1, :] * xext[pl.ds(8 - (CONV_W - 1) + k, tt), :]
    return acc


def _lru_fwd(x, mod_l, g_mix, g_wy, g_win, b_y, b_in, cw, cb, wa, ba, wx, bx, lam, g_wout, b_out, **comm):
    S, W = x.shape
    tt = min(TT, S)
    hd = W // HEADS

    def body(x_ref, mod_ref, g_ref, wy_ref, win_ref, by_ref, bin_ref, cw_ref, cb_ref, wa_ref, ba_ref, wx_ref, bx_ref,
             lam_ref, wo_ref, bo_ref, h_ref, gb_ref, xr_ref, hs_ref, p_ref, y_ref, x1_ref,
             xc_ref, gr_ref, gi_ref, a_s, mu_ref, xext, u_s, carry):
        i = pl.program_id(0)

        @pl.when(i == 0)
        def _():
            carry[...] = jnp.zeros_like(carry)
            xext[0:8, :] = jnp.zeros((8, W), F32)

        @pl.when(i > 0)
        def _():
            xext[0:8, :] = xext[pl.ds(tt, 8), :]

        xv = x_ref[...]
        hb = _rms_fwd(xv, g_ref[...], mod_ref[1:2, :], mod_ref[0:1, :])[0].astype(BF16)
        h_ref[...] = hb
        gbv = _dot(hb, wy_ref[...].reshape(W, W)) + by_ref[...]
        gb_ref[...] = gbv
        xr = _dot(hb, win_ref[...].reshape(W, W)) + bin_ref[...]
        xr_ref[...] = xr
        xext[pl.ds(8, tt), :] = xr
        xc = _conv_taps(xext, cw_ref[...], tt) + cb_ref[...]
        _, gate_r, gate_i, _, a, mult = _lru_gates(xc, wa_ref, ba_ref[...], wx_ref, bx_ref[...], lam_ref[...], hd)
        xc_ref[...] = xc
        gr_ref[...] = gate_r
        gi_ref[...] = gate_i
        mu_ref[...] = mult
        a_s[...] = a
        u_s[...] = mult * (gate_i * xc)
        row = lax.broadcasted_iota(jnp.int32, (8, W), 0)

        def step(k, _):
            off = pl.multiple_of(k * 8, 8)
            A = a_s[pl.ds(off, 8), :]
            U = u_s[pl.ds(off, 8), :]
            for d in (1, 2, 4):
                keep = row >= d
                Us = jnp.where(keep, pltpu.roll(U, d, 0), 0.0)
                As = jnp.where(keep, pltpu.roll(A, d, 0), 1.0)
                U = U + A * Us
                A = A * As
            H = U + A * carry[...]
            hs_ref[pl.ds(off, 8), :] = H
            carry[...] = jnp.broadcast_to(H[7:8, :], (8, W))
            return 0

        lax.fori_loop(0, tt // 8, step, 0)
        pb = (hs_ref[...] * _gelu(gbv)[0]).astype(BF16)
        p_ref[...] = pb
        y = _dot(pb, wo_ref[...].reshape(W, W)) + bo_ref[...]
        y_ref[...] = y.astype(BF16)
        x1_ref[...] = xv + mod_ref[2:3, :] * y

    tile = pl.BlockSpec((tt, W), lambda i: (i, 0))
    row = pl.BlockSpec((1, W), lambda i: (0, 0))
    wblk = pl.BlockSpec((HEADS, hd, hd), lambda i: (0, 0, 0))
    f32o, bf16o = jax.ShapeDtypeStruct((S, W), F32), jax.ShapeDtypeStruct((S, W), BF16)
    return _pcall(
        body, name="lru_fwd", grid=(S // tt,),
        out_shape=(bf16o, f32o, f32o, f32o, bf16o, bf16o, f32o, f32o, f32o, f32o, f32o, f32o),
        in_specs=[tile, pl.BlockSpec((8, W), lambda i: (0, 0)), row, _wspec(g_wy), _wspec(g_win), row, row,
                  pl.BlockSpec((CONV_W, W), lambda i: (0, 0)), row, wblk, row, wblk, row, row, _wspec(g_wout), row],
        out_specs=(tile,) * 12,
        scratch_shapes=[pltpu.VMEM((tt + 8, W), F32), pltpu.VMEM((tt, W), F32), pltpu.VMEM((8, W), F32)],
        operands=(x, mod_l, g_mix, g_wy, g_win, b_y, b_in, cw, cb, wa, ba, wx, bx, lam, g_wout, b_out), **comm)


def _ffn_out_shapes(S, D, F):
    return (jax.ShapeDtypeStruct((S, D), BF16), jax.ShapeDtypeStruct((S, F), BF16),
            jax.ShapeDtypeStruct((S, D), BF16), jax.ShapeDtypeStruct((S, D), F32))


def _ffn_fwd(x1, mod_l, g_ffn, g_w1, g_w2, **comm):
    S, D = x1.shape
    tm = min(TM, S)
    F = g_w1.shape[2] * NQ

    def body(x1_ref, mod_ref, gf_ref, w1_ref, w2_ref, h2_ref, a_ref, z_ref, x2_ref):
        _ffn_fwd_inner(x1_ref[...], mod_ref, gf_ref, w1_ref, w2_ref, h2_ref, a_ref, z_ref, x2_ref)

    tile = pl.BlockSpec((tm, D), lambda i: (i, 0))
    row = pl.BlockSpec((1, D), lambda i: (0, 0))
    return _pcall(
        body, name="ffn_fwd", grid=(S // tm,),
        out_shape=_ffn_out_shapes(S, D, F),
        in_specs=[tile, pl.BlockSpec((8, D), lambda i: (0, 0)), row, _wspec(g_w1), _wspec(g_w2)],
        out_specs=(tile, pl.BlockSpec((tm, F), lambda i: (i, 0)), tile, tile),
        operands=(x1, mod_l, g_ffn, g_w1, g_w2), **comm)


def _window_vec(D):
    gd = D // len(POOL_WINDOWS)
    lane = lax.broadcasted_iota(jnp.int32, (1, D), 1)
    w = jnp.full((1, D), float(POOL_WINDOWS[0]), F32)
    for g in range(1, len(POOL_WINDOWS)):
        w = jnp.where(lane >= g * gd, float(POOL_WINDOWS[g]), w)
    return w


def _pool_mix_ffn_fwd(x, mod_l, g_mix, pw, ps, g_ffn, g_w1, g_w2, **comm):
    S, D = x.shape
    tm = min(TPF, S)
    F = g_w1.shape[2] * NQ
    gd = D // len(POOL_WINDOWS)
    n = tm + 24

    def body(x_ref, xh_ref, mod_ref, gm_ref, pw_ref, ps_ref, gf_ref, w1_ref, w2_ref,
             pl_ref, x1_ref, h2_ref, a_ref, z_ref, x2_ref, ext, b1, b2):
        i = pl.program_id(0)
        g, sc, sh = gm_ref[...], mod_ref[1:2, :], mod_ref[0:1, :]
        xv = x_ref[...]
        h = _rms_fwd(xv, g, sc, sh)[0]
        hh = _rms_fwd(xh_ref[...], g, sc, sh)[0]
        zeros8 = jnp.zeros((8, D), F32)
        ext[0:8, :] = zeros8
        b1[0:8, :] = zeros8
        b2[0:8, :] = zeros8
        ext[8:24, :] = jnp.where(i > 0, hh, 0.0)
        ext[pl.ds(24, tm), :] = h
        m = n - 8
        b1[pl.ds(8, m), :] = ext[pl.ds(8, m), :] + ext[pl.ds(7, m), :]
        b2[pl.ds(8, m), gd:] = b1[pl.ds(8, m), gd:] + b1[pl.ds(6, m), gd:]
        b1[pl.ds(8, m), 2 * gd:] = b2[pl.ds(8, m), 2 * gd:] + b2[pl.ds(4, m), 2 * gd:]
        b2[pl.ds(8, m), 3 * gd:] = b1[pl.ds(8, m), 3 * gd:] + b1[pl.ds(0, m), 3 * gd:]
        wsum = jnp.concatenate([b1[pl.ds(24, tm), 0:gd], b2[pl.ds(24, tm), gd:2 * gd],
                                b1[pl.ds(24, tm), 2 * gd:3 * gd], b2[pl.ds(24, tm), 3 * gd:]], axis=1)
        t1 = (lax.broadcasted_iota(jnp.int32, (tm, 1), 0) + (i * tm + 1)).astype(F32)
        cnt = jnp.minimum(t1, _window_vec(D))
        pooled = (wsum / cnt - h).astype(BF16)
        pl_ref[...] = pooled
        y = _heads_dot(pooled, pw_ref, gd) * ps_ref[...]
        x1 = xv + mod_ref[2:3, :] * y
        x1_ref[...] = x1
        _ffn_fwd_inner(x1, mod_ref, gf_ref, w1_ref, w2_ref, h2_ref, a_ref, z_ref, x2_ref)

    tile = pl.BlockSpec((tm, D), lambda i: (i, 0))
    halo = pl.BlockSpec((16, D), lambda i: (jnp.maximum(i * (tm // 16) - 1, 0), 0))
    row = pl.BlockSpec((1, D), lambda i: (0, 0))
    return _pcall(
        body, name="pool_mix_ffn_fwd", grid=(S // tm,),
        out_shape=(jax.ShapeDtypeStruct((S, D), BF16), jax.ShapeDtypeStruct((S, D), F32)) + _ffn_out_shapes(S, D, F),
        in_specs=[tile, halo, pl.BlockSpec((8, D), lambda i: (0, 0)), row,
                  pl.BlockSpec((len(POOL_WINDOWS), gd, gd), lambda i: (0, 0, 0)), row, row,
                  _wspec(g_w1), _wspec(g_w2)],
        out_specs=(tile, tile, tile, pl.BlockSpec((tm, F), lambda i: (i, 0)), tile, tile),
        scratch_shapes=[pltpu.VMEM((n, D), F32), pltpu.VMEM((n, D), F32), pltpu.VMEM((n, D), F32)],
        operands=(x, x, mod_l, g_mix, pw, ps, g_ffn, g_w1, g_w2), **comm)


def _loss_head(xv, gv, tv, acc_ref):
    D = xv.shape[1]
    r = lax.rsqrt(jnp.mean(xv * xv, axis=-1, keepdims=True) + EPS)
    xhat = xv * r
    err = xhat * gv - tv
    acc_ref[0:1, :] += jnp.sum(err * err, axis=0, keepdims=True)
    dy = err * (1.0 / D)
    acc_ref[1:2, :] += jnp.sum(dy * xhat, axis=0, keepdims=True)
    dxh = dy * gv
    return r * (dxh - xhat * jnp.mean(dxh * xhat, axis=-1, keepdims=True))


def _ffn_bwd(dx2, x1, a, z, mod_l, g_ffn, g_w1, g_w2, head=None, **comm):
    S, D = dx2.shape
    F = a.shape[1]
    f4 = F // NQ
    tm = min(TM, S)
    nh = 2 if head else 0

    def body(*refs):
        dx2_ref, x1_ref, a_ref, z_ref, mod_ref, gf_ref, w1_ref, w2_ref = refs[:8]
        dx1_ref, du_ref, dz_ref, acc_ref = refs[8 + nh:]
        _zero_first(acc_ref)
        dx2v = dx2_ref[...]
        if head:
            dx2v = _loss_head(dx2v, refs[8][...], refs[9][...], acc_ref)
        acc_ref[5:6, :] +=jnp.sum(dx2v * z_ref[...].astype(F32), axis=0, keepdims=True)
        dzb = (dx2v * mod_ref[5:6, :]).astype(BF16)
        dz_ref[...] = dzb
        dh2 = jnp.zeros((tm, D), F32)
        for q in range(NQ):
            av = a_ref[:, q * f4:(q + 1) * f4].astype(F32)
            du = (_dot_nt(dzb, w2_ref[q]) * (2.0 * av)).astype(BF16)
            du_ref[:, q * f4:(q + 1) * f4] = du
            dh2 = dh2 + _dot_nt(du, w1_ref[q])
        g, sc = gf_ref[...], mod_ref[4:5, :]
        _, xhat, r, n = _rms_fwd(x1_ref[...], g, sc, mod_ref[3:4, :])
        dx, dsh, dsc, dg = _rms_bwd(dh2, xhat, r, n, g, sc)
        acc_ref[3:4, :] += dsh
        acc_ref[4:5, :] += dsc
        acc_ref[7:8, :] += dg
        dx1_ref[...] = dx2v + dx

    tile = pl.BlockSpec((tm, D), lambda i: (i, 0))
    wide = pl.BlockSpec((tm, F), lambda i: (i, 0))
    return _pcall(
        body, name="ffn_bwd", grid=(S // tm,),
        out_shape=(jax.ShapeDtypeStruct((S, D), F32), jax.ShapeDtypeStruct((S, F), BF16),
                   jax.ShapeDtypeStruct((S, D), BF16), jax.ShapeDtypeStruct((8, D), F32)),
        in_specs=[tile, tile, wide, tile, pl.BlockSpec((8, D), lambda i: (0, 0)), pl.BlockSpec((1, D), lambda i: (0, 0)),
                  _wspec(g_w1), _wspec(g_w2)] + ([pl.BlockSpec((1, D), lambda i: (0, 0)), tile] if head else []),
        out_specs=(tile, wide, tile, pl.BlockSpec((8, D), lambda i: (0, 0))),
        operands=(dx2, x1, a, z, mod_l, g_ffn, g_w1, g_w2) + (tuple(head) if head else ()), **comm)


def _dw_blocked(a, b, by_rows, square_a, name):
    S = a.shape[0]
    tk = min(TK, S)
    nk = S // tk
    if by_rows:
        bm, bn = a.shape[1] // NQ, b.shape[1]
        a_map, b_map = (lambda q, k: (k, q)), (lambda q, k: (k, 0))
    else:
        bm, bn = a.shape[1], b.shape[1] // NQ
        a_map, b_map = (lambda q, k: (k, 0)), (lambda q, k: (k, q))

    def body(a_ref, b_ref, o_ref, acc):
        k = pl.program_id(1)

        @pl.when(k == 0)
        def _():
            acc[...] = jnp.zeros_like(acc)

        av = a_ref[...]
        if square_a:
            av = av * av
        acc[...] += _dot_tn(av, b_ref[...])

        @pl.when(k == nk - 1)
        def _():
            o_ref[...] = acc[...].astype(o_ref.dtype)

    return pl.pallas_call(
        body, name=name, grid=(NQ, nk),
        out_shape=jax.ShapeDtypeStruct((NQ, bm, bn), BF16),
        in_specs=[pl.BlockSpec((tk, bm), a_map), pl.BlockSpec((tk, bn), b_map)],
        out_specs=pl.BlockSpec((None, bm, bn), lambda q, k: (q, 0, 0)),
        scratch_shapes=[pltpu.VMEM((bm, bn), F32)],
        compiler_params=_cparams("parallel", "arbitrary"),
    )(a, b)


def _dw_whole(a, bs, name, **comm):
    S, M = a.shape
    N = bs[0].shape[1]
    tk = min(TK, S)
    nk = S // tk
    nb = len(bs)

    def body(*refs):
        a_ref, b_refs, o_refs, accs = refs[0], refs[1:1 + nb], refs[1 + nb:1 + 2 * nb], refs[1 + 2 * nb:]
        k = pl.program_id(0)

        @pl.when(k == 0)
        def _():
            for acc in accs:
                acc[...] = jnp.zeros_like(acc)

        av = a_ref[...]
        for b_ref, acc in zip(b_refs, accs):
            acc[...] += _dot_tn(av, b_ref[...])

        @pl.when(k == nk - 1)
        def _():
            for o_ref, acc in zip(o_refs, accs):
                o_ref[...] = acc[...].reshape(NQ, M // NQ, N).astype(o_ref.dtype)

    return _pcall(
        body, name=name, grid=(nk,),
        out_shape=tuple(jax.ShapeDtypeStruct((NQ, M // NQ, N), BF16) for _ in bs),
        in_specs=[pl.BlockSpec((tk, M), lambda k: (k, 0))] + [pl.BlockSpec((tk, N), lambda k: (k, 0)) for _ in bs],
        out_specs=tuple(pl.BlockSpec((NQ, M // NQ, N), lambda k: (0, 0, 0)) for _ in bs),
        scratch_shapes=[pltpu.VMEM((M, N), F32) for _ in bs],
        operands=(a, *bs), **comm)


def _lru_bwd(dx1, y, x, xr0, gb, hs, xc_, gate_r_, gate_i_, a_, mult_, mod_l, g_mix, g_wout, g_wy, g_win, cw, wa, wx,
             lam, **comm):
    S, W = xr0.shape
    tt = min(TT, S)
    nb = S // tt
    hd = W // HEADS

    def body(dx1_ref, y_ref, x_ref, xr_ref, gb_ref, hs_ref, hsh_ref, xc_ref, gr_ref, gi_ref, a_s, mu_ref,
             mod_ref, gm_ref, wo_ref, wy_ref, win_ref, cw_ref, wa_ref, wx_ref, lam_ref,
             dy_ref, dgb_ref, dxr_ref, dx_ref, sm_ref, dwa_ref, dwx_ref, acc_ref,
             hext, qext, dext, b_s, qc, dc):
        i = pl.program_id(0)
        blk = nb - 1 - i

        @pl.when(i == 0)
        def _():
            sm_ref[...] = jnp.zeros_like(sm_ref)
            dwa_ref[...] = jnp.zeros_like(dwa_ref)
            dwx_ref[...] = jnp.zeros_like(dwx_ref)
            acc_ref[...] = jnp.zeros_like(acc_ref)
            qc[...] = jnp.zeros_like(qc)
            dc[...] = jnp.zeros_like(dc)

        dx1v = dx1_ref[...]
        acc_ref[2:3, :] += jnp.sum(dx1v * y_ref[...].astype(F32), axis=0, keepdims=True)
        dy = dx1v * mod_ref[2:3, :]
        acc_ref[3:4, :] += jnp.sum(dy, axis=0, keepdims=True)
        dyb = dy.astype(BF16)
        dy_ref[...] = dyb
        dpv = _dot_nt(dyb, wo_ref[...].reshape(W, W))

        hext[0:8, :] = jnp.where(blk > 0, hsh_ref[...], 0.0)
        hext[pl.ds(8, tt), :] = hs_ref[...]
        cw = cw_ref[...]
        lam = lam_ref[...]
        xc, gate_r, gate_i, a, mult = xc_ref[...], gr_ref[...], gi_ref[...], a_s[...], mu_ref[...]
        xcb = xc.astype(BF16)
        ls = jax.nn.log_sigmoid(lam)

        gbv = gb_ref[...]
        gate, th = _gelu(gbv)
        dgb = dpv * hs_ref[...] * _gelu_grad(gbv, th)
        dgbb = dgb.astype(BF16)
        dgb_ref[...] = dgbb
        sm_ref[9:10, :] += jnp.sum(dgb, axis=0, keepdims=True)
        dhs = dpv * gate

        b_s[...] = a * dhs
        qext[pl.ds(tt, 8), :] = qc[...]
        row = lax.broadcasted_iota(jnp.int32, (8, W), 0)

        def step(k, _):
            off = pl.multiple_of((tt // 8 - 1 - k) * 8, 8)
            A = a_s[pl.ds(off, 8), :]
            B = b_s[pl.ds(off, 8), :]
            for d in (1, 2, 4):
                keep = row < 8 - d
                Bs = jnp.where(keep, pltpu.roll(B, 8 - d, 0), 0.0)
                As = jnp.where(keep, pltpu.roll(A, 8 - d, 0), 1.0)
                B = B + A * Bs
                A = A * As
            Q = B + A * qc[...]
            qext[pl.ds(off, 8), :] = Q
            qc[...] = jnp.broadcast_to(Q[0:1, :], (8, W))
            return 0

        lax.fori_loop(0, tt // 8, step, 0)
        gsc = dhs + qext[pl.ds(1, tt), :]
        da = gsc * hext[pl.ds(7, tt), :]
        t1 = gsc * xc
        dmult = t1 * gate_i
        dgate_i = t1 * mult
        dxc = gsc * (mult * gate_i)
        dlog_a = da * a - dmult * (a * a) / mult
        dgate_r = dlog_a * (LRU_C * ls)
        sm_ref[7:8, :] += jnp.sum(dlog_a * (LRU_C * gate_r), axis=0, keepdims=True)
        dga = dgate_r * gate_r * (1.0 - gate_r)
        dgx = dgate_i * gate_i * (1.0 - gate_i)
        sm_ref[5:6, :] += jnp.sum(dga, axis=0, keepdims=True)
        sm_ref[6:7, :] += jnp.sum(dgx, axis=0, keepdims=True)
        dgab = dga.astype(BF16)
        dgxb = dgx.astype(BF16)
        dxc = dxc + _heads_dot(dgab, wa_ref, hd, nt=True) + _heads_dot(dgxb, wx_ref, hd, nt=True)
        for h in range(HEADS):
            sl = slice(h * hd, (h + 1) * hd)
            dwa_ref[h] += _dot_tn(xcb[:, sl], dgab[:, sl])
            dwx_ref[h] += _dot_tn(xcb[:, sl], dgxb[:, sl])
        sm_ref[4:5, :] += jnp.sum(dxc, axis=0, keepdims=True)
        dext[pl.ds(0, tt), :] = dxc
        dext[pl.ds(tt, 8), :] = dc[...]
        xrv = xr_ref[...]
        dxr = None
        for k in range(CONV_W):
            up = dext[pl.ds(CONV_W - 1 - k, tt), :]
            sm_ref[k:k + 1, :] += jnp.sum(up * xrv, axis=0, keepdims=True)
            dxr = cw[k:k + 1, :] * up if dxr is None else dxr + cw[k:k + 1, :] * up
        dc[...] = dext[0:8, :]
        sm_ref[8:9, :] += jnp.sum(dxr, axis=0, keepdims=True)
        dxrb = dxr.astype(BF16)
        dxr_ref[...] = dxrb

        dh = _dot_nt(dxrb, win_ref[...].reshape(W, W)) + _dot_nt(dgbb, wy_ref[...].reshape(W, W))
        g, sc = gm_ref[...], mod_ref[1:2, :]
        _, xhat, r, n = _rms_fwd(x_ref[...], g, sc, mod_ref[0:1, :])
        dx, dsh, dsc, dg = _rms_bwd(dh, xhat, r, n, g, sc)
        acc_ref[0:1, :] += dsh
        acc_ref[1:2, :] += dsc
        acc_ref[6:7, :] += dg
        dx_ref[...] = dx1v + dx

        @pl.when(i == nb - 1)
        def _():
            sm_ref[7:8, :] = sm_ref[7:8, :] * jax.nn.sigmoid(-lam)

    rev = lambda i: (nb - 1 - i, 0)
    tile = pl.BlockSpec((tt, W), rev)
    halo = pl.BlockSpec((8, W), lambda i: (jnp.maximum((nb - 1 - i) * (tt // 8) - 1, 0), 0))
    row = pl.BlockSpec((1, W), lambda i: (0, 0))
    wblk = pl.BlockSpec((HEADS, hd, hd), lambda i: (0, 0, 0))
    bf16o = jax.ShapeDtypeStruct((S, W), BF16)
    return _pcall(
        body, name="lru_bwd", grid=(nb,),
        out_shape=(bf16o, bf16o, bf16o, jax.ShapeDtypeStruct((S, W), F32),
                   jax.ShapeDtypeStruct((16, W), F32), jax.ShapeDtypeStruct((HEADS, hd, hd), F32),
                   jax.ShapeDtypeStruct((HEADS, hd, hd), F32), jax.ShapeDtypeStruct((8, W), F32)),
        in_specs=[tile, tile, tile, tile, tile, tile, halo, tile, tile, tile, tile, tile,
                  pl.BlockSpec((8, W), lambda i: (0, 0)), row,
                  _wspec(g_wout), _wspec(g_wy), _wspec(g_win), pl.BlockSpec((CONV_W, W), lambda i: (0, 0)),
                  wblk, wblk, row],
        out_specs=(tile, tile, tile, tile, pl.BlockSpec((16, W), lambda i: (0, 0)), wblk, wblk,
                   pl.BlockSpec((8, W), lambda i: (0, 0))),
        scratch_shapes=[pltpu.VMEM((tt + 8, W), F32), pltpu.VMEM((tt + 8, W), F32), pltpu.VMEM((tt + 8, W), F32),
                        pltpu.VMEM((tt, W), F32), pltpu.VMEM((8, W), F32), pltpu.VMEM((8, W), F32)],
        operands=(dx1, y, x, xr0, gb, hs, hs, xc_, gate_r_, gate_i_, a_, mult_, mod_l, g_mix, g_wout, g_wy, g_win,
                  cw, wa, wx, lam),
        **comm)


def _pool_bwd(dx1, x, pooled, mod_l, g_mix, pw, ps, h2, du, a, dz):
    S, D = x.shape
    tm = min(TP, S)
    nb = S // tm
    ng = len(POOL_WINDOWS)
    gd = D // ng
    n = tm + 24
    f4 = du.shape[1] // NQ
    assert nb % NQ == 0
    kch = nb // NQ
    kr = S // kch

    def body(dx1_ref, dxh_ref, x_ref, pl_ref, mod_ref, gm_ref, pw_ref, ps_ref, h2_ref, du_ref, a_ref, dz_ref,
             dx_ref, acc_ref, dpw_ref, dw1_ref, dw2_ref, ext, b1, b2, acc1, acc2):
        i = pl.program_id(0)

        @pl.when(i == 0)
        def _():
            acc_ref[...] = jnp.zeros_like(acc_ref)
            dpw_ref[...] = jnp.zeros_like(dpw_ref)

        @pl.when(i % kch == 0)
        def _():
            acc1[...] = jnp.zeros_like(acc1)
            acc2[...] = jnp.zeros_like(acc2)

        acc1[...] += _dot_tn(h2_ref[...], du_ref[...])
        av = a_ref[...]
        acc2[...] += _dot_tn(av * av, dz_ref[...])

        gt, psv = mod_ref[2:3, :], ps_ref[...]
        wvec = _window_vec(D)
        dx1v = dx1_ref[...]
        pooled = pl_ref[...]
        mixed = _heads_dot(pooled, pw_ref, gd)
        acc_ref[2:3, :] += jnp.sum(dx1v * (mixed * psv), axis=0, keepdims=True)
        dy = dx1v * gt
        acc_ref[3:4, :] += jnp.sum(dy * mixed, axis=0, keepdims=True)
        dmix = (dy * psv).astype(BF16)
        for gi in range(ng):
            sl = slice(gi * gd, (gi + 1) * gd)
            dpw_ref[gi] += _dot_tn(pooled[:, sl], dmix[:, sl])
        dpooled = _heads_dot(dmix, pw_ref, gd, nt=True)
        dmix_h = (dxh_ref[...] * gt * psv).astype(BF16)
        dpooled_h = jnp.where(i < nb - 1, _heads_dot(dmix_h, pw_ref, gd, nt=True), 0.0)
        t1 = (lax.broadcasted_iota(jnp.int32, (tm, 1), 0) + (i * tm + 1)).astype(F32)
        t1h = (lax.broadcasted_iota(jnp.int32, (16, 1), 0) + ((i + 1) * tm + 1)).astype(F32)
        zeros8 = jnp.zeros((8, D), F32)
        ext[pl.ds(0, tm), :] = dpooled / jnp.minimum(t1, wvec)
        ext[pl.ds(tm, 16), :] = dpooled_h / jnp.minimum(t1h, wvec)
        ext[pl.ds(tm + 16, 8), :] = zeros8
        b1[pl.ds(tm + 16, 8), :] = zeros8
        b2[pl.ds(tm + 16, 8), :] = zeros8
        m = n - 8
        b1[pl.ds(0, m), :] = ext[pl.ds(0, m), :] + ext[pl.ds(1, m), :]
        b2[pl.ds(0, m), gd:] = b1[pl.ds(0, m), gd:] + b1[pl.ds(2, m), gd:]
        b1[pl.ds(0, m), 2 * gd:] = b2[pl.ds(0, m), 2 * gd:] + b2[pl.ds(4, m), 2 * gd:]
        b2[pl.ds(0, m), 3 * gd:] = b1[pl.ds(0, m), 3 * gd:] + b1[pl.ds(8, m), 3 * gd:]
        wsum = jnp.concatenate([b1[pl.ds(0, tm), 0:gd], b2[pl.ds(0, tm), gd:2 * gd],
                                b1[pl.ds(0, tm), 2 * gd:3 * gd], b2[pl.ds(0, tm), 3 * gd:]], axis=1)
        dh = wsum - dpooled
        g, sc = gm_ref[...], mod_ref[1:2, :]
        _, xhat, r, nn = _rms_fwd(x_ref[...], g, sc, mod_ref[0:1, :])
        dx, dsh, dsc, dg = _rms_bwd(dh, xhat, r, nn, g, sc)
        acc_ref[0:1, :] += dsh
        acc_ref[1:2, :] += dsc
        acc_ref[6:7, :] += dg
        dx_ref[...] = dx1v + dx

        @pl.when(i % kch == kch - 1)
        def _():
            dw1_ref[...] = acc1[...].astype(BF16)
            dw2_ref[...] = acc2[...].astype(BF16)

    tile = pl.BlockSpec((tm, D), lambda i: (i, 0))
    halo = pl.BlockSpec((16, D), lambda i: (jnp.minimum((i + 1) * (tm // 16), S // 16 - 1), 0))
    row = pl.BlockSpec((1, D), lambda i: (0, 0))
    wblk = pl.BlockSpec((ng, gd, gd), lambda i: (0, 0, 0))
    full_k = pl.BlockSpec((kr, D), lambda i: (i % kch, 0))
    part_k = pl.BlockSpec((kr, f4), lambda i: (i % kch, i // kch))
    return pl.pallas_call(
        body, name="pool_bwd", grid=(nb,),
        out_shape=(jax.ShapeDtypeStruct((S, D), F32), jax.ShapeDtypeStruct((8, D), F32),
                   jax.ShapeDtypeStruct((ng, gd, gd), F32), jax.ShapeDtypeStruct((NQ, D, f4), BF16),
                   jax.ShapeDtypeStruct((NQ, f4, D), BF16)),
        in_specs=[tile, halo, tile, tile, pl.BlockSpec((8, D), lambda i: (0, 0)), row, wblk, row,
                  full_k, part_k, part_k, full_k],
        out_specs=(tile, pl.BlockSpec((8, D), lambda i: (0, 0)), wblk,
                   pl.BlockSpec((None, D, f4), lambda i: (i // kch, 0, 0)),
                   pl.BlockSpec((None, f4, D), lambda i: (i // kch, 0, 0))),
        scratch_shapes=[pltpu.VMEM((n, D), F32), pltpu.VMEM((n, D), F32), pltpu.VMEM((n, D), F32),
                        pltpu.VMEM((D, f4), F32), pltpu.VMEM((f4, D), F32)],
        compiler_params=_cparams("arbitrary"),
    )(dx1, dx1, x, pooled, mod_l, g_mix, pw, ps, h2, du, a, dz)


def _shard_to_rows(w, D):
    return w.reshape(-1, D)


def _blockdiag_full(gq, na, hd):
    return gq.reshape(NQ, na, HEADS, hd // NQ, hd).transpose(1, 2, 0, 3, 4).reshape(na, HEADS, hd, hd)


def _blockdiag_by_chip(dw, D):
    na, _, hd, _ = dw.shape
    return dw.reshape(na, HEADS, NQ, hd // NQ, hd).transpose(2, 0, 1, 3, 4).reshape(NQ, -1, D)


def kernel(x, c, w_mod, b_mod, norm_mix_g, norm_ffn_g, lru_w_y, lru_b_y, lru_w_in, lru_b_in, lru_conv_w, lru_conv_b, lru_w_a, lru_b_a, lru_w_x, lru_b_x, lru_lambda, lru_w_out, lru_b_out, pool_w, pool_scale, ffn_w1, ffn_w2, final_norm_g, loss_target, m_w_mod, m_b_mod, m_norm_mix_g, m_norm_ffn_g, m_lru_w_y, m_lru_b_y, m_lru_w_in, m_lru_b_in, m_lru_conv_w, m_lru_conv_b, m_lru_w_a, m_lru_b_a, m_lru_w_x, m_lru_b_x, m_lru_lambda, m_lru_w_out, m_lru_b_out, m_pool_w, m_pool_scale, m_ffn_w1, m_ffn_w2, m_final_norm_g, v_w_mod, v_b_mod, v_norm_mix_g, v_norm_ffn_g, v_lru_w_y, v_lru_b_y, v_lru_w_in, v_lru_b_in, v_lru_conv_w, v_lru_conv_b, v_lru_w_a, v_lru_b_a, v_lru_w_x, v_lru_b_x, v_lru_lambda, v_lru_w_out, v_lru_b_out, v_pool_w, v_pool_scale, v_ffn_w1, v_ffn_w2, v_final_norm_g):
    S, D = x.shape[1], x.shape[2]
    L = w_mod.shape[0]
    NA = lru_w_y.shape[0]
    NB = pool_w.shape[0]
    F = ffn_w1.shape[2] * NQ
    f4 = F // NQ
    hd = D // HEADS
    Cs = w_mod.shape[2]
    assert L == DEPTH and Cs * NQ == N_MOD * D and D % 1024 == 0
    x2d = x.reshape(S, D)
    tgt = loss_target.reshape(S, D)
    q = 2 * lax.axis_index("x") + lax.axis_index("y")

    big = [ffn_w1, ffn_w2, lru_w_y, lru_w_in, lru_w_out, lru_w_a, lru_w_x, pool_w]
    rows = [int(w.size) // D for w in big]
    offs = [sum(rows[:k]) for k in range(len(big))]
    O_W1, O_W2, O_WY, O_WIN, O_WOUT, O_WA, O_WX, O_PW = offs
    R = sum(rows)
    dq = D // NQ
    s_w1 = [ffn_w1[i].astype(BF16) for i in range(L)]
    s_w2 = [ffn_w2[i].astype(BF16) for i in range(L)]
    s_wy = [lru_w_y[j].astype(BF16) for j in range(NA)]
    s_win = [lru_w_in[j].astype(BF16) for j in range(NA)]
    s_wout = [lru_w_out[j].astype(BF16) for j in range(NA)]
    s_tiny = jnp.concatenate([_shard_to_rows(w, D) for w in (lru_w_a, lru_w_x, pool_w)], axis=0).astype(BF16)

    cshard = lru_conv_w.reshape(-1)
    small_fwd = jnp.concatenate([c.reshape(-1), cshard, lru_b_a.reshape(-1), lru_b_x.reshape(-1),
                                 pool_scale.reshape(-1)])
    small_fwd = jnp.pad(small_fwd, (0, 8 * D - small_fwd.shape[0])).reshape(8, D)

    g_w1, g_w2 = [None] * L, [None] * L
    g_wy, g_win, g_wout = [None] * NA, [None] * NA, [None] * NA
    SG, g_wy[0], g_win[0], g_wout[0], g_tiny = _comm_only("gather_first", small=small_fwd,
                                                         gathers=(s_wy[0], s_win[0], s_wout[0], s_tiny))
    wa_full = _blockdiag_full(g_tiny[:, :rows[5]], NA, hd)
    wx_full = _blockdiag_full(g_tiny[:, rows[5]:rows[5] + rows[6]], NA, hd)
    pw_full = _blockdiag_full(g_tiny[:, rows[5] + rows[6]:], NB, hd)
    SGf = SG.reshape(NDEV, 8 * D)
    c_all = SGf[:, :D]
    SGq = SGf.reshape(NQ, 2, 8 * D)[:, 0]
    o = D
    n_cw = NA * CONV_W * D // NQ
    conv_w_full = SGq[:, o:o + n_cw].reshape(NQ, NA, CONV_W, D // NQ).transpose(1, 2, 0, 3).reshape(NA, CONV_W, D)
    o += n_cw
    n_b = NA * HEADS * hd // NQ
    b_a_full = SGq[:, o:o + n_b].reshape(NQ, NA, HEADS, hd // NQ).transpose(1, 2, 0, 3).reshape(NA, 1, D)
    o += n_b
    b_x_full = SGq[:, o:o + n_b].reshape(NQ, NA, HEADS, hd // NQ).transpose(1, 2, 0, 3).reshape(NA, 1, D)
    o += n_b
    n_ps = NB * D // NQ
    pool_scale_full = SGq[:, o:o + n_ps].reshape(NQ, NB, D // NQ).transpose(1, 0, 2).reshape(NB, 1, D)


    b_mod_sh = lax.dynamic_slice_in_dim(b_mod, q * Cs, Cs, axis=1).reshape(L, 1, Cs)
    modpart = _mod_part(c_all, w_mod, b_mod_sh)
    modq = _exchange_mod(modpart.transpose(1, 0, 2))
    mod = modq.transpose(1, 0, 2).reshape(L, N_MOD, D)
    mod = jnp.pad(mod, ((0, 0), (0, 8 - N_MOD), (0, 0)))

    saved = []
    xcur = x2d
    for i in range(L):
        j = i // 2
        gm = norm_mix_g[i].reshape(1, D)
        gf = norm_ffn_g[i].reshape(1, D)
        if i % 2 == 0:
            h, gb, xr0, hs, p, y, x1, xc_s, gr_s, gi_s, a_sv, mu_s, g_w1[i], g_w2[i] = _lru_fwd(
                xcur, mod[i], gm, g_wy[j], g_win[j], lru_b_y[j].reshape(1, D), lru_b_in[j].reshape(1, D),
                conv_w_full[j], lru_conv_b[j].reshape(1, D), wa_full[j], b_a_full[j], wx_full[j], b_x_full[j],
                lru_lambda[j].reshape(1, D), g_wout[j], lru_b_out[j].reshape(1, D), gathers=(s_w1[i], s_w2[i]))
            h2, a, z, x2, g_w1[i + 1], g_w2[i + 1] = _ffn_fwd(x1, mod[i], gf, g_w1[i], g_w2[i],
                                                              gathers=(s_w1[i + 1], s_w2[i + 1]))
            saved.append(dict(x=xcur, h=h, gb=gb, xr0=xr0, hs=hs, p=p, y=y, x1=x1, h2=h2, a=a, z=z,
                              lru=(xc_s, gr_s, gi_s, a_sv, mu_s)))
        else:
            if j + 1 < NA:
                pooled, x1, h2, a, z, x2, g_wy[j + 1], g_win[j + 1], g_wout[j + 1] = _pool_mix_ffn_fwd(
                    xcur, mod[i], gm, pw_full[j], pool_scale_full[j], gf, g_w1[i], g_w2[i],
                    gathers=(s_wy[j + 1], s_win[j + 1], s_wout[j + 1]))
            else:
                pooled, x1, h2, a, z, x2 = _pool_mix_ffn_fwd(xcur, mod[i], gm, pw_full[j], pool_scale_full[j], gf,
                                                             g_w1[i], g_w2[i])
            saved.append(dict(x=xcur, pooled=pooled, x1=x1, h2=h2, a=a, z=z))
        xcur = x2

    dx = xcur
    qv = q.reshape(1).astype(jnp.int32)
    ppack = lax.empty((R, D), F32)
    psib = lax.empty((R, D), F32)
    pending, summed = [], []

    def comm_args():
        kw = {}
        if pending:
            kw["scatters"] = tuple(dw for dw, _ in pending)
        if summed:
            kw["sib"] = (ppack, psib, tuple(summed))
        return kw

    def after_host(extra):
        nonlocal ppack, psib, pending, summed
        had_sib = bool(summed)
        summed = []
        for (dw, off), rb in zip(pending, extra[:len(pending)]):
            ppack = _sum_into(ppack, dw, rb, off, qv)
            summed.append((off, dw.shape[1]))
        if had_sib:
            psib = extra[len(pending)]
        pending = []

    dmod_rows = [None] * L
    dg_mix = [None] * L
    dg_ffn = [None] * L
    d_small = {}
    dwa_l, dwx_l, dpw_l = [None] * NA, [None] * NA, [None] * NB
    for i in reversed(range(L)):
        j = i // 2
        sv = saved[i]
        gm = norm_mix_g[i].reshape(1, D)
        gf = norm_ffn_g[i].reshape(1, D)
        head = (final_norm_g.reshape(1, D), tgt) if i == L - 1 else None
        outs = _ffn_bwd(dx, sv["x1"], sv["a"], sv["z"], mod[i], gf, g_w1[i], g_w2[i], head=head, **comm_args())
        dx1, du, dz, facc = outs[:4]
        after_host(outs[4:])
        if head:
            loss = lax.psum(0.5 * jnp.sum(facc[0]) / D, ("x", "y", "c"))
            d_final_g = facc[1]
        if i % 2 == 0:
            pending.append((_dw_blocked(sv["h2"], du, False, False, "dw1"), O_W1 + i * D))
            pending.append((_dw_blocked(sv["a"], dz, True, True, "dw2"), O_W2 + i * f4))
            outs = _lru_bwd(dx1, sv["y"], sv["x"], sv["xr0"], sv["gb"], sv["hs"], *sv["lru"], mod[i], gm, g_wout[j],
                            g_wy[j], g_win[j], conv_w_full[j], wa_full[j], wx_full[j], lru_lambda[j].reshape(1, D),
                            **comm_args())
            dyp, dgb, dxr, dx, sm, dwa, dwx, macc = outs[:8]
            after_host(outs[8:])
            dwa_l[j], dwx_l[j] = dwa, dwx
            if i == 0:
                tiny = jnp.concatenate([_blockdiag_by_chip(jnp.stack(dwa_l), D), _blockdiag_by_chip(jnp.stack(dwx_l), D),
                                        _blockdiag_by_chip(jnp.stack(dpw_l), D)], axis=1).astype(BF16)
                pending.append((tiny, O_WA))
            if i == 0:
                outs = _dw_whole(sv["h"], [dgb, dxr], "dwy_dwin", **comm_args())
                after_host(outs[2:])
                pending.append((outs[0], O_WY + j * dq))
                pending.append((outs[1], O_WIN + j * dq))
                outs = _dw_whole(sv["p"], [dyp], "dwout", **comm_args())
                after_host(outs[1:])
                pending.append((outs[0], O_WOUT + j * dq))
            else:
                outs = _dw_whole(sv["p"], [dyp], "dwout", **comm_args())
                after_host(outs[1:])
                pending.append((outs[0], O_WOUT + j * dq))
                outs = _dw_whole(sv["h"], [dgb, dxr], "dwy_dwin", **comm_args())
                after_host(outs[2:])
                pending.append((outs[0], O_WY + j * dq))
                pending.append((outs[1], O_WIN + j * dq))
            d_small[("lru", j)] = (sm, macc[3])
            dgt_m = macc[2]
        else:
            dx, macc, dpw, dw1, dw2 = _pool_bwd(dx1, sv["x"], sv["pooled"], mod[i], gm, pw_full[j], pool_scale_full[j],
                                                sv["h2"], du, sv["a"], dz)
            pending.append((dw1, O_W1 + i * D))
            pending.append((dw2, O_W2 + i * f4))
            dpw_l[j] = dpw
            d_small[("pool", j)] = macc[3]
            dgt_m = macc[2]
        dmod_rows[i] = jnp.stack([macc[0], macc[1], dgt_m, facc[3], facc[4], facc[5]])
        dg_mix[i] = macc[6]
        dg_ffn[i] = facc[7]
    grad_x = dx.reshape(x.shape)

    lru_sm = [d_small[("lru", j)] for j in range(NA)]
    small_rows = [jnp.stack(dmod_rows).reshape(L * N_MOD, D), jnp.stack(dg_mix), jnp.stack(dg_ffn),
                  jnp.stack([s[0][9] for s in lru_sm]), jnp.stack([s[0][8] for s in lru_sm]),
                  jnp.stack([s[0][4] for s in lru_sm]), jnp.stack([s[0][7] for s in lru_sm]),
                  jnp.stack([s[1] for s in lru_sm]),
                  jnp.stack([s[0][0:CONV_W] for s in lru_sm]).reshape(NA * CONV_W, D),
                  jnp.stack([s[0][5] for s in lru_sm]), jnp.stack([s[0][6] for s in lru_sm]),
                  jnp.stack([d_small[("pool", j)] for j in range(NB)]), d_final_g.reshape(1, D)]
    small_g = jnp.concatenate(small_rows, axis=0)
    n_small = small_g.shape[0]
    assert n_small <= SMALL_ROWS
    small_g = jnp.pad(small_g, ((0, SMALL_ROWS - n_small), (0, 0)))

    outs = _comm_only("scatter_last", small=small_g, reduce_small=True, **comm_args())
    sg_all, sg_sum = outs[:2]
    after_host(outs[2:])
    psum_mine = ppack
    psum_sib = _comm_only("sibling_last", sib=(ppack, psib, tuple(summed)))[0]

    def big_update(w, m, v, off, name):
        shp = w.shape
        g, dl, m2, v2 = _adam_rows(w.reshape(-1, D), m.reshape(-1, D), v.reshape(-1, D), psum_mine, psum_sib, off, name)
        return g.reshape(shp), dl.reshape(shp), m2.reshape(shp), v2.reshape(shp)

    res = {}
    res["ffn_w1"] = big_update(ffn_w1, m_ffn_w1, v_ffn_w1, O_W1, "adam_w1")
    res["ffn_w2"] = big_update(ffn_w2, m_ffn_w2, v_ffn_w2, O_W2, "adam_w2")
    res["lru_w_y"] = big_update(lru_w_y, m_lru_w_y, v_lru_w_y, O_WY, "adam_wy")
    res["lru_w_in"] = big_update(lru_w_in, m_lru_w_in, v_lru_w_in, O_WIN, "adam_win")
    res["lru_w_out"] = big_update(lru_w_out, m_lru_w_out, v_lru_w_out, O_WOUT, "adam_wout")

    def tiny_parts(w, off):
        n = int(w.size) // D
        return psum_mine[off:off + n].reshape(w.shape), psum_sib[off:off + n].reshape(w.shape)

    tiny_items = [("lru_w_a", lru_w_a, m_lru_w_a, v_lru_w_a) + tiny_parts(lru_w_a, O_WA),
                  ("lru_w_x", lru_w_x, m_lru_w_x, v_lru_w_x) + tiny_parts(lru_w_x, O_WX),
                  ("pool_w", pool_w, m_pool_w, v_pool_w) + tiny_parts(pool_w, O_PW)]

    dmod_all = sg_all[:, :L * N_MOD, :].reshape(NDEV, L, N_MOD * D)
    dmod_sh = lax.dynamic_slice_in_dim(dmod_all, q * Cs, Cs, axis=2).transpose(1, 0, 2)
    res["w_mod"] = _wmod_update(c_all.T, dmod_sh, w_mod, m_w_mod, v_w_mod)

    r0 = 0
    by_rows, names_a = [], []
    for name, w, m, v in (("b_mod", b_mod, m_b_mod, v_b_mod), ("norm_mix_g", norm_mix_g, m_norm_mix_g, v_norm_mix_g),
                          ("norm_ffn_g", norm_ffn_g, m_norm_ffn_g, v_norm_ffn_g),
                          ("lru_b_y", lru_b_y, m_lru_b_y, v_lru_b_y), ("lru_b_in", lru_b_in, m_lru_b_in, v_lru_b_in),
                          ("lru_conv_b", lru_conv_b, m_lru_conv_b, v_lru_conv_b),
                          ("lru_lambda", lru_lambda, m_lru_lambda, v_lru_lambda),
                          ("lru_b_out", lru_b_out, m_lru_b_out, v_lru_b_out)):
        by_rows.append((w, m, v, r0))
        names_a.append(name)
        r0 += int(w.size) // D
    g_conv_w = lax.dynamic_slice_in_dim(sg_sum[r0:r0 + NA * CONV_W].reshape(NA, CONV_W, D), q * dq, dq, axis=2)
    r0 += NA * CONV_W
    g_b_a = lax.dynamic_slice_in_dim(sg_sum[r0:r0 + NA].reshape(NA, HEADS, hd), q * (hd // NQ), hd // NQ, axis=2)
    r0 += NA
    g_b_x = lax.dynamic_slice_in_dim(sg_sum[r0:r0 + NA].reshape(NA, HEADS, hd), q * (hd // NQ), hd // NQ, axis=2)
    r0 += NA
    g_ps = lax.dynamic_slice_in_dim(sg_sum[r0:r0 + NB], q * dq, dq, axis=1)
    r0 += NB
    by_rows.append((final_norm_g.reshape(1, D), m_final_norm_g.reshape(1, D), v_final_norm_g.reshape(1, D), r0))
    names_a.append("final_norm_g")
    sliced = [(lru_conv_w, m_lru_conv_w, v_lru_conv_w, g_conv_w), (lru_b_a, m_lru_b_a, v_lru_b_a, g_b_a),
              (lru_b_x, m_lru_b_x, v_lru_b_x, g_b_x), (pool_scale, m_pool_scale, v_pool_scale, g_ps)]
    res_a, res_b, res_c = _adam_small(sg_sum, by_rows, sliced, [t[1:] for t in tiny_items])
    for name, r in zip(names_a, res_a):
        res[name] = r
    for t, r in zip(tiny_items, res_c):
        res[t[0]] = r
    res["final_norm_g"] = tuple(a.reshape(D) for a in res["final_norm_g"])
    for name, (_, _, _, g), r in zip(("lru_conv_w", "lru_b_a", "lru_b_x", "pool_scale"), sliced, res_b):
        res[name] = (g,) + r

    order = ["w_mod", "b_mod", "norm_mix_g", "norm_ffn_g", "lru_w_y", "lru_b_y", "lru_w_in", "lru_b_in", "lru_conv_w",
             "lru_conv_b", "lru_w_a", "lru_b_a", "lru_w_x", "lru_b_x", "lru_lambda", "lru_w_out", "lru_b_out", "pool_w",
             "pool_scale", "ffn_w1", "ffn_w2", "final_norm_g"]
    return (loss, grad_x, *[res[n][0] for n in order], *[res[n][1] for n in order],
            *[res[n][2] for n in order], *[res[n][3] for n in order])
```

```python
import jax
import jax.numpy as jnp
from jax import lax
from jax.experimental import pallas as pl
from jax.experimental.pallas import tpu as pltpu

F32 = jnp.float32
BF16 = jnp.bfloat16
MESH = pl.DeviceIdType.MESH

NQ = 4
NDEV = 8
DEPTH = 4
N_MOD = 6
HEADS = 4
CONV_W = 4
POOL_WINDOWS = (2, 4, 8, 16)
LRU_C = 8.0
EPS = 1e-6
ADAM_LR, ADAM_B1, ADAM_B2, ADAM_EPS, ADAM_WD, ADAM_STEP = 0.001, 0.9, 0.999, 1e-08, 0.01, 10

TM = 512
TT = 256
TP = 256
TPF = 512
TK = 2048
TKB = 4096
SMALL_ROWS = 64
FORWARD_STEPS = 4
VMEM_LIMIT = 60 * 1024 * 1024


def _cparams(*sem):
    return pltpu.CompilerParams(dimension_semantics=tuple(sem), vmem_limit_bytes=VMEM_LIMIT)


def _dot(a, b):
    return jnp.dot(a, b, preferred_element_type=F32)


def _dot_nt(a, b):
    return lax.dot_general(a, b, (((1,), (1,)), ((), ())), preferred_element_type=F32)


def _dot_tn(a, b):
    return lax.dot_general(a, b, (((0,), (0,)), ((), ())), preferred_element_type=F32)


def _resident(shape, index_map):
    return pl.BlockSpec(shape, index_map, pipeline_mode=pl.Buffered(1))


def _rms_fwd(x, g, sc, sh):
    r = lax.rsqrt(jnp.mean(x * x, axis=-1, keepdims=True) + EPS)
    xhat = x * r
    n = xhat * g
    return n * (1.0 + sc) + sh, xhat, r, n


def _rms_bwd(dh, xhat, r, n, g, sc):
    dsh = jnp.sum(dh, axis=0, keepdims=True)
    dsc = jnp.sum(dh * n, axis=0, keepdims=True)
    dn = dh * (1.0 + sc)
    dg = jnp.sum(dn * xhat, axis=0, keepdims=True)
    dxh = dn * g
    dx = r * (dxh - xhat * jnp.mean(dxh * xhat, axis=-1, keepdims=True))
    return dx, dsh, dsc, dg


_GELU_K = 0.7978845608028654
_GELU_C = 0.044715


def _gelu(x):
    t = jnp.tanh(_GELU_K * (x + _GELU_C * x * x * x))
    return 0.5 * x * (1.0 + t), t


def _gelu_grad(x, t):
    return 0.5 * (1.0 + t) + 0.5 * x * (1.0 - t * t) * (_GELU_K * (1.0 + 3.0 * _GELU_C * x * x))


def _neg_expm1(y, exp_y):
    series = -(y * (1.0 + y * (0.5 + y * (1.0 / 6.0))))
    return jnp.where(y > -(1.0 / 64.0), series, 1.0 - exp_y)


def _zero_first(ref):
    @pl.when(pl.program_id(0) == 0)
    def _():
        ref[...] = jnp.zeros_like(ref)


def _my_pos():
    return lax.axis_index("x"), lax.axis_index("y"), lax.axis_index("c")


def _dev_index(x, y, c):
    return 4 * x + 2 * y + c


def _chip_peers(x, y):
    return [(1 - x, y), (x, 1 - y), (1 - x, 1 - y)]


def _all_peers(x, y, c):
    return [(px, py, c) for (px, py) in _chip_peers(x, y)] + [(x, y, 1 - c)] + \
           [(px, py, 1 - c) for (px, py) in _chip_peers(x, y)]


def _comm_run(phase, x, y, c, gathers, scatters, sib, send, recv, loc):
    q = 2 * x + y
    peers = _chip_peers(x, y)
    sibling = (x, y, 1 - c)

    def rcopy(src, dst, s, dev):
        return pltpu.make_async_remote_copy(src, dst, send.at[s], recv.at[s], device_id=dev, device_id_type=MESH)

    s = 0
    for gi, (src, dst) in enumerate(gathers):
        half = src.shape[0] // 2
        mine, other = pl.ds(c * half, half), pl.ds((1 - c) * half, half)
        own = pltpu.make_async_copy(src, dst.at[q], loc.at[gi])
        if phase == "start":
            own.start()
        elif phase == "finish":
            own.wait()
        for (px, py) in peers:
            pq = 2 * px + py
            s_ici, s_fwd = s, s + 1
            s += 2
            if phase == "start":
                rcopy(src.at[mine], dst.at[q].at[mine], s_ici, (px, py, c)).start()
            elif phase == "forward":
                rcopy(src.at[mine], dst.at[pq].at[mine], s_ici, (px, py, c)).wait_recv()
                rcopy(dst.at[pq].at[mine], dst.at[pq].at[mine], s_fwd, sibling).start()
            else:
                rcopy(dst.at[pq].at[other], dst.at[pq].at[other], s_fwd, sibling).wait_recv()
                rcopy(src.at[mine], dst.at[q].at[mine], s_ici, (px, py, c)).wait_send()
                rcopy(dst.at[pq].at[mine], dst.at[pq].at[mine], s_fwd, sibling).wait_send()
    direct = []
    for (src, dst) in scatters:
        for k, (px, py) in enumerate(peers):
            direct.append((src.at[2 * px + py], dst.at[k], (px, py, c)))
    if sib is not None:
        src, dst, ranges = sib
        for (off, rows) in ranges:
            direct.append((src.at[pl.ds(off, rows)], dst.at[pl.ds(off, rows)], sibling))
    if phase == "start":
        for k, (a, b, dev) in enumerate(direct):
            rcopy(a, b, s + k, dev).start()
    elif phase == "finish":
        for k, (a, b, dev) in enumerate(direct):
            rcopy(a, b, s + k, dev).wait_recv()
        for k, (a, b, dev) in enumerate(direct):
            rcopy(a, b, s + k, dev).wait_send()


def _comm_shapes(gathers, scatters, sib):
    assert all(g.shape[0] % 32 == 0 for g in gathers)
    cin = list(gathers) + list(scatters) + ([sib[0], sib[1]] if sib else [])
    cout = [jax.ShapeDtypeStruct((NQ,) + g.shape, g.dtype) for g in gathers] + \
           [jax.ShapeDtypeStruct((3,) + s.shape[1:], s.dtype) for s in scatters] + \
           ([jax.ShapeDtypeStruct(sib[1].shape, sib[1].dtype)] if sib else [])
    n_rem = 6 * len(gathers) + 3 * len(scatters) + (len(sib[2]) if sib else 0)
    sems = [pltpu.SemaphoreType.DMA((max(n_rem, 1),)), pltpu.SemaphoreType.DMA((max(n_rem, 1),)),
            pltpu.SemaphoreType.DMA((max(len(gathers), 1),))]
    return cin, cout, sems


def _pcall(body, *, name, grid, in_specs, out_specs, out_shape, operands, scratch_shapes=(),
           gathers=(), scatters=(), sib=None):
    assert len(grid) == 1
    out_shape, out_specs = tuple(out_shape), tuple(out_specs)
    if not (gathers or scatters or sib):
        return pl.pallas_call(body, name=name, grid=grid, in_specs=list(in_specs), out_specs=out_specs,
                              out_shape=out_shape, scratch_shapes=list(scratch_shapes),
                              compiler_params=_cparams("arbitrary"))(*operands)
    cin, cout, sems = _comm_shapes(gathers, scatters, sib)
    n_in, n_cin, n_out, n_cout, n_scr = len(operands), len(cin), len(out_shape), len(cout), len(scratch_shapes)
    ng, ns = len(gathers), len(scatters)
    nsteps = grid[0]

    def wrapped(*refs):
        ins = refs[:n_in]
        cins = refs[n_in:n_in + n_cin]
        o0 = n_in + n_cin
        outs = refs[o0:o0 + n_out]
        couts = refs[o0 + n_out:o0 + n_out + n_cout]
        s0 = o0 + n_out + n_cout
        scr = refs[s0:s0 + n_scr]
        send, recv, loc = refs[s0 + n_scr:s0 + n_scr + 3]
        x, y, c = _my_pos()

        def run(phase):
            g = [(cins[k], couts[k]) for k in range(ng)]
            sc = [(cins[ng + k], couts[ng + k]) for k in range(ns)]
            sb = (cins[ng + ns], couts[ng + ns], sib[2]) if sib else None
            _comm_run(phase, x, y, c, g, sc, sb, send, recv, loc)

        @pl.when(pl.program_id(0) == 0)
        def _():
            run("start")

        if ng:
            @pl.when(pl.program_id(0) == max(nsteps - FORWARD_STEPS, 0))
            def _():
                run("forward")

        body(*ins, *outs, *scr)

        @pl.when(pl.program_id(0) == nsteps - 1)
        def _():
            run("finish")

    anyspec = pl.BlockSpec(memory_space=pl.ANY)
    aliases = {n_in + ng + ns + 1: n_out + ng + ns} if sib else {}
    return pl.pallas_call(
        wrapped, name=name, grid=grid,
        in_specs=list(in_specs) + [anyspec] * n_cin, out_specs=out_specs + (anyspec,) * n_cout,
        out_shape=out_shape + tuple(cout), scratch_shapes=list(scratch_shapes) + sems,
        input_output_aliases=aliases,
        compiler_params=pltpu.CompilerParams(dimension_semantics=("arbitrary",), vmem_limit_bytes=VMEM_LIMIT,
                                             has_side_effects=True),
    )(*operands, *cin)


def _comm_only(name, small=None, reduce_small=False, gathers=(), scatters=(), sib=None):
    cin, cout, sems = _comm_shapes(gathers, scatters, sib)
    n_cin, n_cout = len(cin), len(cout)
    ng, ns = len(gathers), len(scatters)
    n_sm_in = 1 if small is not None else 0
    n_sm_out = (2 if reduce_small else 1) if small is not None else 0

    def body(*refs):
        sm_in = refs[:n_sm_in]
        cins = refs[n_sm_in:n_sm_in + n_cin]
        o0 = n_sm_in + n_cin
        sm_out = refs[o0:o0 + n_sm_out]
        couts = refs[o0 + n_sm_out:o0 + n_sm_out + n_cout]
        s0 = o0 + n_sm_out + n_cout
        send, recv, loc = refs[s0:s0 + 3]
        x, y, c = _my_pos()
        g = [(cins[k], couts[k]) for k in range(ng)]
        sc = [(cins[ng + k], couts[ng + k]) for k in range(ns)]
        sb = (cins[ng + ns], couts[ng + ns], sib[2]) if sib else None
        _comm_run("start", x, y, c, g, sc, sb, send, recv, loc)
        if small is not None:
            sm_send, sm_recv = refs[s0 + 3:s0 + 5]
            small_ref, sg_ref = sm_in[0], sm_out[0]
            me = _dev_index(x, y, c)
            sg_ref[me] = small_ref[...]
            peers = _all_peers(x, y, c)
            sm = [pltpu.make_async_remote_copy(small_ref, sg_ref.at[me], sm_send.at[k], sm_recv.at[k],
                                               device_id=peer, device_id_type=MESH) for k, peer in enumerate(peers)]
            for cp in sm:
                cp.start()
            for k, (px, py, pc) in enumerate(peers):
                pltpu.make_async_remote_copy(small_ref, sg_ref.at[_dev_index(px, py, pc)], sm_send.at[k], sm_recv.at[k],
                                             device_id=(px, py, pc), device_id_type=MESH).wait_recv()
            if reduce_small:
                acc = sg_ref[0]
                for d in range(1, NDEV):
                    acc = acc + sg_ref[d]
                sm_out[1][...] = acc
            for cp in sm:
                cp.wait_send()
        if ng:
            _comm_run("forward", x, y, c, g, sc, sb, send, recv, loc)
        _comm_run("finish", x, y, c, g, sc, sb, send, recv, loc)

    anyspec = pl.BlockSpec(memory_space=pl.ANY)
    vspec = pl.BlockSpec(memory_space=pltpu.VMEM)
    sm_shapes = []
    if small is not None:
        sm_shapes.append(jax.ShapeDtypeStruct((NDEV,) + small.shape, small.dtype))
        if reduce_small:
            sm_shapes.append(jax.ShapeDtypeStruct(small.shape, small.dtype))
        sems = sems + [pltpu.SemaphoreType.DMA((NDEV - 1,)), pltpu.SemaphoreType.DMA((NDEV - 1,))]
    aliases = {n_sm_in + ng + ns + 1: n_sm_out + ng + ns} if sib else {}
    return pl.pallas_call(
        body, name=name,
        in_specs=[vspec] * n_sm_in + [anyspec] * n_cin,
        out_specs=tuple([vspec] * n_sm_out + [anyspec] * n_cout),
        out_shape=tuple(sm_shapes + cout), scratch_shapes=sems, input_output_aliases=aliases,
        compiler_params=pltpu.CompilerParams(has_side_effects=True),
    )(*([small] if small is not None else []), *cin)


def _exchange_mod(modpart):
    _, L, Cs = modpart.shape

    def body(part_ref, out_ref, send, recv):
        x, y, c = _my_pos()
        q = 2 * x + y
        me = _dev_index(x, y, c)
        out_ref[q] = part_ref[me]
        sends = []
        for k, (px, py) in enumerate(_chip_peers(x, y)):
            cp = pltpu.make_async_remote_copy(part_ref.at[_dev_index(px, py, c)], out_ref.at[q], send.at[k], recv.at[k],
                                              device_id=(px, py, c), device_id_type=MESH)
            cp.start()
            sends.append(cp)
        for k, (px, py) in enumerate(_chip_peers(x, y)):
            pltpu.make_async_remote_copy(part_ref.at[me], out_ref.at[2 * px + py], send.at[k], recv.at[k],
                                         device_id=(px, py, c), device_id_type=MESH).wait_recv()
        for cp in sends:
            cp.wait_send()

    return pl.pallas_call(
        body, name="exchange_mod",
        out_shape=jax.ShapeDtypeStruct((NQ, L, Cs), modpart.dtype),
        in_specs=[pl.BlockSpec(memory_space=pltpu.VMEM)],
        out_specs=pl.BlockSpec(memory_space=pltpu.VMEM),
        scratch_shapes=[pltpu.SemaphoreType.DMA((3,)), pltpu.SemaphoreType.DMA((3,))],
        compiler_params=pltpu.CompilerParams(has_side_effects=True),
    )(modpart)


def _mod_part(c_all, w_mod, b_mod_sh):
    L, D, Cs = w_mod.shape
    tn = 512 if Cs % 512 == 0 else Cs

    def body(c_ref, w_ref, b_ref, o_ref):
        cv = c_ref[...]
        cond = cv * jax.nn.sigmoid(cv)
        o_ref[...] = jnp.dot(cond, w_ref[...], preferred_element_type=F32, precision=lax.Precision.HIGHEST) + b_ref[...]

    return pl.pallas_call(
        body, name="mod_part", grid=(L, Cs // tn),
        out_shape=jax.ShapeDtypeStruct((L, NDEV, Cs), F32),
        in_specs=[pl.BlockSpec((NDEV, D), lambda i, j: (0, 0)),
                  pl.BlockSpec((None, D, tn), lambda i, j: (i, 0, j)),
                  pl.BlockSpec((None, 1, tn), lambda i, j: (i, 0, j))],
        out_specs=pl.BlockSpec((None, NDEV, tn), lambda i, j: (i, 0, j)),
        compiler_params=_cparams("parallel", "parallel"),
    )(c_all, w_mod, b_mod_sh)


def _adam(w, g, m, v):
    m2 = ADAM_B1 * m + (1.0 - ADAM_B1) * g
    v2 = ADAM_B2 * v + (1.0 - ADAM_B2) * (g * g)
    m_hat = m2 / (1.0 - ADAM_B1 ** ADAM_STEP)
    v_hat = v2 / (1.0 - ADAM_B2 ** ADAM_STEP)
    delta = -ADAM_LR * (m_hat / (jnp.sqrt(v_hat) + ADAM_EPS) + ADAM_WD * w)
    return delta, m2, v2


def _wmod_update(c_all_t, dmod_sh, w, m, v):
    L, D, Cs = w.shape
    td = 256 if D % 256 == 0 else D

    def body(ct_ref, d_ref, w_ref, m_ref, v_ref, g_ref, dl_ref, m2_ref, v2_ref):
        cv = ct_ref[...]
        cond = cv * jax.nn.sigmoid(cv)
        g = cond[:, 0:1] * d_ref[0:1, :]
        for b in range(1, NDEV):
            g = g + cond[:, b:b + 1] * d_ref[b:b + 1, :]
        g_ref[...] = g
        dl_ref[...], m2_ref[...], v2_ref[...] = _adam(w_ref[...], g, m_ref[...], v_ref[...])

    blk = pl.BlockSpec((None, td, Cs), lambda i, j: (i, j, 0))
    out = jax.ShapeDtypeStruct((L, D, Cs), F32)
    return pl.pallas_call(
        body, name="wmod_update", grid=(L, D // td),
        out_shape=(out, out, out, out),
        in_specs=[pl.BlockSpec((td, NDEV), lambda i, j: (j, 0)),
                  pl.BlockSpec((None, NDEV, Cs), lambda i, j: (i, 0, 0)), blk, blk, blk],
        out_specs=(blk, blk, blk, blk),
        compiler_params=_cparams("parallel", "parallel"),
    )(c_all_t, dmod_sh, w, m, v)


def _adam_rows(w, m, v, pa, pb, row_off, name):
    rows, C = w.shape
    tr = 512 if rows % 512 == 0 else (128 if rows % 128 == 0 else rows)
    assert row_off % tr == 0
    ob = row_off // tr

    def body(w_ref, m_ref, v_ref, pa_ref, pb_ref, g_ref, dl_ref, m2_ref, v2_ref):
        g = pa_ref[...] + pb_ref[...]
        g_ref[...] = g
        dl_ref[...], m2_ref[...], v2_ref[...] = _adam(w_ref[...], g, m_ref[...], v_ref[...])

    blk = pl.BlockSpec((tr, C), lambda i: (i, 0))
    pblk = pl.BlockSpec((tr, C), lambda i: (ob + i, 0))
    out = jax.ShapeDtypeStruct((rows, C), F32)
    return pl.pallas_call(
        body, name=name, grid=(rows // tr,), out_shape=(out, out, out, out),
        in_specs=[blk, blk, blk, pblk, pblk], out_specs=(blk, blk, blk, blk),
        compiler_params=_cparams("parallel"),
    )(w, m, v, pa, pb)


def _adam_small(sg_sum, by_rows, sliced, pairs):
    D = sg_sum.shape[1]
    na, nb, nc = len(by_rows), len(sliced), len(pairs)

    def body(*refs):
        sg = refs[0]
        ins_a = [refs[1 + 3 * t:4 + 3 * t] for t in range(na)]
        p = 1 + 3 * na
        ins_b = [refs[p + 4 * t:p + 4 * t + 4] for t in range(nb)]
        p += 4 * nb
        ins_c = [refs[p + 5 * t:p + 5 * t + 5] for t in range(nc)]
        p += 5 * nc
        outs_a = [refs[p + 4 * t:p + 4 * t + 4] for t in range(na)]
        p += 4 * na
        outs_b = [refs[p + 3 * t:p + 3 * t + 3] for t in range(nb)]
        p += 3 * nb
        outs_c = [refs[p + 4 * t:p + 4 * t + 4] for t in range(nc)]
        for (w_ref, m_ref, v_ref, ga_ref, gb_ref), (g_ref, dl_ref, m2_ref, v2_ref) in zip(ins_c, outs_c):
            g = ga_ref[...] + gb_ref[...]
            g_ref[...] = g
            dl_ref[...], m2_ref[...], v2_ref[...] = _adam(w_ref[...], g, m_ref[...], v_ref[...])
        for (w_ref, m_ref, v_ref), (g_ref, dl_ref, m2_ref, v2_ref), (w, _, _, row0) in zip(ins_a, outs_a, by_rows):
            n, k = w.shape[0], w.shape[1] // D
            pieces = [(slice(0, n), slice(0, D), slice(row0, row0 + n))] if k == 1 else \
                     [(slice(i, i + 1), slice(kk * D, (kk + 1) * D), slice(row0 + i * k + kk, row0 + i * k + kk + 1))
                      for i in range(n) for kk in range(k)]
            for rs, cs, gs in pieces:
                g = sg[gs, :]
                g_ref[rs, cs] = g
                dl_ref[rs, cs], m2_ref[rs, cs], v2_ref[rs, cs] = _adam(w_ref[rs, cs], g, m_ref[rs, cs], v_ref[rs, cs])
        for (w_ref, m_ref, v_ref, g_ref), (dl_ref, m2_ref, v2_ref) in zip(ins_b, outs_b):
            dl_ref[...], m2_ref[...], v2_ref[...] = _adam(w_ref[...], g_ref[...], m_ref[...], v_ref[...])

    operands = [sg_sum] + [a for t in by_rows for a in t[:3]] + [a for t in sliced for a in t] + \
               [a for t in pairs for a in t]
    out_shape = [jax.ShapeDtypeStruct(t[0].shape, F32) for t in by_rows for _ in range(4)] + \
                [jax.ShapeDtypeStruct(t[0].shape, F32) for t in sliced for _ in range(3)] + \
                [jax.ShapeDtypeStruct(t[0].shape, F32) for t in pairs for _ in range(4)]
    outs = pl.pallas_call(body, name="adam_small", out_shape=tuple(out_shape))(*operands)
    res_a = [tuple(outs[4 * t:4 * t + 4]) for t in range(na)]
    o = 4 * na
    res_b = [tuple(outs[o + 3 * t:o + 3 * t + 3]) for t in range(nb)]
    o += 3 * nb
    res_c = [tuple(outs[o + 4 * t:o + 4 * t + 4]) for t in range(nc)]
    return res_a, res_b, res_c


def _sum_into(ppack, dw, rb, off, qv):
    _, rows, D = dw.shape
    tr = 256 if rows % 256 == 0 else 128
    assert rows % tr == 0 and off % tr == 0
    ob = off // tr

    def body(q_ref, o_ref, r_ref, pin_ref, p_ref):
        acc = o_ref[...].astype(F32)
        for k in range(3):
            acc = acc + r_ref[k].astype(F32)
        p_ref[...] = acc

    return pl.pallas_call(
        body, name="sum_partials", out_shape=jax.ShapeDtypeStruct(ppack.shape, ppack.dtype),
        grid_spec=pltpu.PrefetchScalarGridSpec(
            num_scalar_prefetch=1, grid=(rows // tr,),
            in_specs=[pl.BlockSpec((None, tr, D), lambda i, q_ref: (q_ref[0], i, 0)),
                      pl.BlockSpec((3, tr, D), lambda i, q_ref: (0, i, 0)),
                      pl.BlockSpec(memory_space=pl.ANY)],
            out_specs=pl.BlockSpec((tr, D), lambda i, q_ref: (ob + i, 0))),
        input_output_aliases={3: 0},
        compiler_params=_cparams("parallel"),
    )(qv, dw, rb, ppack)


def _wspec(g):
    return _resident(g.shape, lambda i: (0, 0, 0))


def _ffn_fwd_inner(x1, mod_ref, gf_ref, w1_ref, w2_ref, h2_ref, a_ref, z_ref, x2_ref):
    h2 = _rms_fwd(x1, gf_ref[...], mod_ref[4:5, :], mod_ref[3:4, :])[0]
    h2b = h2.astype(BF16)
    h2_ref[...] = h2b
    f4 = w1_ref.shape[2]
    z = jnp.zeros(x1.shape, F32)
    for q in range(NQ):
        a = jnp.maximum(_dot(h2b, w1_ref[q]), 0.0)
        a_ref[:, q * f4:(q + 1) * f4] = a.astype(BF16)
        z = z + _dot((a * a).astype(BF16), w2_ref[q])
    z_ref[...] = z.astype(BF16)
    x2_ref[...] = x1 + mod_ref[5:6, :] * z


def _sigmoid(x):
    return 0.5 + 0.5 * jnp.tanh(0.5 * x)


def _heads_dot(xb, w_ref, hd, nt=False):
    outs = []
    for h in range(HEADS):
        xs = xb[:, h * hd:(h + 1) * hd]
        outs.append(_dot_nt(xs, w_ref[h]) if nt else _dot(xs, w_ref[h]))
    return jnp.concatenate(outs, axis=1)


def _lru_gates(xc, wa_ref, ba, wx_ref, bx, lam, hd):
    xcb = xc.astype(BF16)
    gate_r = _sigmoid(_heads_dot(xcb, wa_ref, hd) + ba)
    gate_i = _sigmoid(_heads_dot(xcb, wx_ref, hd) + bx)
    ls = jax.nn.log_sigmoid(lam)
    log_a = gate_r * (LRU_C * ls)
    a = jnp.exp(log_a)
    mult = jnp.sqrt(_neg_expm1(2.0 * log_a, a * a))
    return xcb, gate_r, gate_i, ls, a, mult


def _conv_taps(xext, cw, tt):
    acc = cw[0:1, :] * xext[pl.ds(8 - (CONV_W - 1), tt), :]
    for k in range(1, CONV_W):
        acc = acc + cw[k:k + 1, :] * xext[pl.ds(8 - (CONV_W - 1) + k, tt), :]
    return acc


def _lru_fwd(x, mod_l, g_mix, g_wy, g_win, b_y, b_in, cw, cb, wa, ba, wx, bx, lam, g_wout, b_out, **comm):
    S, W = x.shape
    tt = min(TT, S)
    hd = W // HEADS

    def body(x_ref, mod_ref, g_ref, wy_ref, win_ref, by_ref, bin_ref, cw_ref, cb_ref, wa_ref, ba_ref, wx_ref, bx_ref,
             lam_ref, wo_ref, bo_ref, h_ref, gb_ref, xr_ref, hs_ref, p_ref, y_ref, x1_ref,
             xc_ref, gr_ref, gi_ref, a_s, mu_ref, xext, u_s, carry):
        i = pl.program_id(0)

        @pl.when(i == 0)
        def _():
            carry[...] = jnp.zeros_like(carry)
            xext[0:8, :] = jnp.zeros((8, W), F32)

        @pl.when(i > 0)
        def _():
            xext[0:8, :] = xext[pl.ds(tt, 8), :]

        xv = x_ref[...]
        hb = _rms_fwd(xv, g_ref[...], mod_ref[1:2, :], mod_ref[0:1, :])[0].astype(BF16)
        h_ref[...] = hb
        gbv = _dot(hb, wy_ref[...].reshape(W, W)) + by_ref[...]
        gb_ref[...] = gbv
        xr = _dot(hb, win_ref[...].reshape(W, W)) + bin_ref[...]
        xr_ref[...] = xr
        xext[pl.ds(8, tt), :] = xr
        xc = _conv_taps(xext, cw_ref[...], tt) + cb_ref[...]
        _, gate_r, gate_i, _, a, mult = _lru_gates(xc, wa_ref, ba_ref[...], wx_ref, bx_ref[...], lam_ref[...], hd)
        xc_ref[...] = xc
        gr_ref[...] = gate_r
        gi_ref[...] = gate_i
        mu_ref[...] = mult
        a_s[...] = a
        u_s[...] = mult * (gate_i * xc)
        row = lax.broadcasted_iota(jnp.int32, (8, W), 0)

        def step(k, _):
            off = pl.multiple_of(k * 8, 8)
            A = a_s[pl.ds(off, 8), :]
            U = u_s[pl.ds(off, 8), :]
            for d in (1, 2, 4):
                keep = row >= d
                Us = jnp.where(keep, pltpu.roll(U, d, 0), 0.0)
                As = jnp.where(keep, pltpu.roll(A, d, 0), 1.0)
                U = U + A * Us
                A = A * As
            H = U + A * carry[...]
            hs_ref[pl.ds(off, 8), :] = H
            carry[...] = jnp.broadcast_to(H[7:8, :], (8, W))
            return 0

        lax.fori_loop(0, tt // 8, step, 0)
        pb = (hs_ref[...] * _gelu(gbv)[0]).astype(BF16)
        p_ref[...] = pb
        y = _dot(pb, wo_ref[...].reshape(W, W)) + bo_ref[...]
        y_ref[...] = y.astype(BF16)
        x1_ref[...] = xv + mod_ref[2:3, :] * y

    tile = pl.BlockSpec((tt, W), lambda i: (i, 0))
    row = pl.BlockSpec((1, W), lambda i: (0, 0))
    wblk = pl.BlockSpec((HEADS, hd, hd), lambda i: (0, 0, 0))
    f32o, bf16o = jax.ShapeDtypeStruct((S, W), F32), jax.ShapeDtypeStruct((S, W), BF16)
    return _pcall(
        body, name="lru_fwd", grid=(S // tt,),
        out_shape=(bf16o, f32o, f32o, f32o, bf16o, bf16o, f32o, f32o, f32o, f32o, f32o, f32o),
        in_specs=[tile, pl.BlockSpec((8, W), lambda i: (0, 0)), row, _wspec(g_wy), _wspec(g_win), row, row,
                  pl.BlockSpec((CONV_W, W), lambda i: (0, 0)), row, wblk, row, wblk, row, row, _wspec(g_wout), row],
        out_specs=(tile,) * 12,
        scratch_shapes=[pltpu.VMEM((tt + 8, W), F32), pltpu.VMEM((tt, W), F32), pltpu.VMEM((8, W), F32)],
        operands=(x, mod_l, g_mix, g_wy, g_win, b_y, b_in, cw, cb, wa, ba, wx, bx, lam, g_wout, b_out), **comm)


def _ffn_out_shapes(S, D, F):
    return (jax.ShapeDtypeStruct((S, D), BF16), jax.ShapeDtypeStruct((S, F), BF16),
            jax.ShapeDtypeStruct((S, D), BF16), jax.ShapeDtypeStruct((S, D), F32))


def _ffn_fwd(x1, mod_l, g_ffn, g_w1, g_w2, **comm):
    S, D = x1.shape
    tm = min(TM, S)
    F = g_w1.shape[2] * NQ

    def body(x1_ref, mod_ref, gf_ref, w1_ref, w2_ref, h2_ref, a_ref, z_ref, x2_ref):
        _ffn_fwd_inner(x1_ref[...], mod_ref, gf_ref, w1_ref, w2_ref, h2_ref, a_ref, z_ref, x2_ref)

    tile = pl.BlockSpec((tm, D), lambda i: (i, 0))
    row = pl.BlockSpec((1, D), lambda i: (0, 0))
    return _pcall(
        body, name="ffn_fwd", grid=(S // tm,),
        out_shape=_ffn_out_shapes(S, D, F),
        in_specs=[tile, pl.BlockSpec((8, D), lambda i: (0, 0)), row, _wspec(g_w1), _wspec(g_w2)],
        out_specs=(tile, pl.BlockSpec((tm, F), lambda i: (i, 0)), tile, tile),
        operands=(x1, mod_l, g_ffn, g_w1, g_w2), **comm)


def _window_vec(D):
    gd = D // len(POOL_WINDOWS)
    lane = lax.broadcasted_iota(jnp.int32, (1, D), 1)
    w = jnp.full((1, D), float(POOL_WINDOWS[0]), F32)
    for g in range(1, len(POOL_WINDOWS)):
        w = jnp.where(lane >= g * gd, float(POOL_WINDOWS[g]), w)
    return w


def _pool_mix_ffn_fwd(x, mod_l, g_mix, pw, ps, g_ffn, g_w1, g_w2, **comm):
    S, D = x.shape
    tm = min(TPF, S)
    F = g_w1.shape[2] * NQ
    gd = D // len(POOL_WINDOWS)
    n = tm + 24

    def body(x_ref, xh_ref, mod_ref, gm_ref, pw_ref, ps_ref, gf_ref, w1_ref, w2_ref,
             pl_ref, x1_ref, h2_ref, a_ref, z_ref, x2_ref, ext, b1, b2):
        i = pl.program_id(0)
        g, sc, sh = gm_ref[...], mod_ref[1:2, :], mod_ref[0:1, :]
        xv = x_ref[...]
        h = _rms_fwd(xv, g, sc, sh)[0]
        hh = _rms_fwd(xh_ref[...], g, sc, sh)[0]
        zeros8 = jnp.zeros((8, D), F32)
        ext[0:8, :] = zeros8
        b1[0:8, :] = zeros8
        b2[0:8, :] = zeros8
        ext[8:24, :] = jnp.where(i > 0, hh, 0.0)
        ext[pl.ds(24, tm), :] = h
        m = n - 8
        b1[pl.ds(8, m), :] = ext[pl.ds(8, m), :] + ext[pl.ds(7, m), :]
        b2[pl.ds(8, m), gd:] = b1[pl.ds(8, m), gd:] + b1[pl.ds(6, m), gd:]
        b1[pl.ds(8, m), 2 * gd:] = b2[pl.ds(8, m), 2 * gd:] + b2[pl.ds(4, m), 2 * gd:]
        b2[pl.ds(8, m), 3 * gd:] = b1[pl.ds(8, m), 3 * gd:] + b1[pl.ds(0, m), 3 * gd:]
        wsum = jnp.concatenate([b1[pl.ds(24, tm), 0:gd], b2[pl.ds(24, tm), gd:2 * gd],
                                b1[pl.ds(24, tm), 2 * gd:3 * gd], b2[pl.ds(24, tm), 3 * gd:]], axis=1)
        t1 = (lax.broadcasted_iota(jnp.int32, (tm, 1), 0) + (i * tm + 1)).astype(F32)
        cnt = jnp.minimum(t1, _window_vec(D))
        pooled = (wsum / cnt - h).astype(BF16)
        pl_ref[...] = pooled
        y = _heads_dot(pooled, pw_ref, gd) * ps_ref[...]
        x1 = xv + mod_ref[2:3, :] * y
        x1_ref[...] = x1
        _ffn_fwd_inner(x1, mod_ref, gf_ref, w1_ref, w2_ref, h2_ref, a_ref, z_ref, x2_ref)

    tile = pl.BlockSpec((tm, D), lambda i: (i, 0))
    halo = pl.BlockSpec((16, D), lambda i: (jnp.maximum(i * (tm // 16) - 1, 0), 0))
    row = pl.BlockSpec((1, D), lambda i: (0, 0))
    return _pcall(
        body, name="pool_mix_ffn_fwd", grid=(S // tm,),
        out_shape=(jax.ShapeDtypeStruct((S, D), BF16), jax.ShapeDtypeStruct((S, D), F32)) + _ffn_out_shapes(S, D, F),
        in_specs=[tile, halo, pl.BlockSpec((8, D), lambda i: (0, 0)), row,
                  pl.BlockSpec((len(POOL_WINDOWS), gd, gd), lambda i: (0, 0, 0)), row, row,
                  _wspec(g_w1), _wspec(g_w2)],
        out_specs=(tile, tile, tile, pl.BlockSpec((tm, F), lambda i: (i, 0)), tile, tile),
        scratch_shapes=[pltpu.VMEM((n, D), F32), pltpu.VMEM((n, D), F32), pltpu.VMEM((n, D), F32)],
        operands=(x, x, mod_l, g_mix, pw, ps, g_ffn, g_w1, g_w2), **comm)


def _loss_head(xv, gv, tv, acc_ref):
    D = xv.shape[1]
    r = lax.rsqrt(jnp.mean(xv * xv, axis=-1, keepdims=True) + EPS)
    xhat = xv * r
    err = xhat * gv - tv
    acc_ref[0:1, :] += jnp.sum(err * err, axis=0, keepdims=True)
    dy = err * (1.0 / D)
    acc_ref[1:2, :] += jnp.sum(dy * xhat, axis=0, keepdims=True)
    dxh = dy * gv
    return r * (dxh - xhat * jnp.mean(dxh * xhat, axis=-1, keepdims=True))


def _ffn_bwd(dx2, x1, a, z, mod_l, g_ffn, g_w1, g_w2, head=None, **comm):
    S, D = dx2.shape
    F = a.shape[1]
    f4 = F // NQ
    tm = min(TM, S)
    nh = 2 if head else 0

    def body(*refs):
        dx2_ref, x1_ref, a_ref, z_ref, mod_ref, gf_ref, w1_ref, w2_ref = refs[:8]
        dx1_ref, du_ref, dz_ref, acc_ref = refs[8 + nh:]
        _zero_first(acc_ref)
        dx2v = dx2_ref[...]
        if head:
            dx2v = _loss_head(dx2v, refs[8][...], refs[9][...], acc_ref)
        acc_ref[5:6, :] +=jnp.sum(dx2v * z_ref[...].astype(F32), axis=0, keepdims=True)
        dzb = (dx2v * mod_ref[5:6, :]).astype(BF16)
        dz_ref[...] = dzb
        dh2 = jnp.zeros((tm, D), F32)
        for q in range(NQ):
            av = a_ref[:, q * f4:(q + 1) * f4].astype(F32)
            du = (_dot_nt(dzb, w2_ref[q]) * (2.0 * av)).astype(BF16)
            du_ref[:, q * f4:(q + 1) * f4] = du
            dh2 = dh2 + _dot_nt(du, w1_ref[q])
        g, sc = gf_ref[...], mod_ref[4:5, :]
        _, xhat, r, n = _rms_fwd(x1_ref[...], g, sc, mod_ref[3:4, :])
        dx, dsh, dsc, dg = _rms_bwd(dh2, xhat, r, n, g, sc)
        acc_ref[3:4, :] += dsh
        acc_ref[4:5, :] += dsc
        acc_ref[7:8, :] += dg
        dx1_ref[...] = dx2v + dx

    tile = pl.BlockSpec((tm, D), lambda i: (i, 0))
    wide = pl.BlockSpec((tm, F), lambda i: (i, 0))
    return _pcall(
        body, name="ffn_bwd", grid=(S // tm,),
        out_shape=(jax.ShapeDtypeStruct((S, D), F32), jax.ShapeDtypeStruct((S, F), BF16),
                   jax.ShapeDtypeStruct((S, D), BF16), jax.ShapeDtypeStruct((8, D), F32)),
        in_specs=[tile, tile, wide, tile, pl.BlockSpec((8, D), lambda i: (0, 0)), pl.BlockSpec((1, D), lambda i: (0, 0)),
                  _wspec(g_w1), _wspec(g_w2)] + ([pl.BlockSpec((1, D), lambda i: (0, 0)), tile] if head else []),
        out_specs=(tile, wide, tile, pl.BlockSpec((8, D), lambda i: (0, 0))),
        operands=(dx2, x1, a, z, mod_l, g_ffn, g_w1, g_w2) + (tuple(head) if head else ()), **comm)


def _dw_blocked(a, b, by_rows, square_a, name):
    S = a.shape[0]
    tk = min(TKB, S)
    nk = S // tk
    if by_rows:
        bm, bn = a.shape[1] // NQ, b.shape[1]
        a_map, b_map = (lambda q, k: (k, q)), (lambda q, k: (k, 0))
    else:
        bm, bn = a.shape[1], b.shape[1] // NQ
        a_map, b_map = (lambda q, k: (k, 0)), (lambda q, k: (k, q))

    def body(a_ref, b_ref, o_ref, acc):
        k = pl.program_id(1)

        @pl.when(k == 0)
        def _():
            acc[...] = jnp.zeros_like(acc)

        av = a_ref[...]
        if square_a:
            av = av * av
        acc[...] += _dot_tn(av, b_ref[...])

        @pl.when(k == nk - 1)
        def _():
            o_ref[...] = acc[...].astype(o_ref.dtype)

    return pl.pallas_call(
        body, name=name, grid=(NQ, nk),
        out_shape=jax.ShapeDtypeStruct((NQ, bm, bn), BF16),
        in_specs=[pl.BlockSpec((tk, bm), a_map), pl.BlockSpec((tk, bn), b_map)],
        out_specs=pl.BlockSpec((None, bm, bn), lambda q, k: (q, 0, 0)),
        scratch_shapes=[pltpu.VMEM((bm, bn), F32)],
        compiler_params=_cparams("parallel", "arbitrary"),
    )(a, b)


def _dw_whole(a, bs, name, **comm):
    S, M = a.shape
    N = bs[0].shape[1]
    tk = min(TK, S)
    nk = S // tk
    nb = len(bs)

    def body(*refs):
        a_ref, b_refs, o_refs, accs = refs[0], refs[1:1 + nb], refs[1 + nb:1 + 2 * nb], refs[1 + 2 * nb:]
        k = pl.program_id(0)

        @pl.when(k == 0)
        def _():
            for acc in accs:
                acc[...] = jnp.zeros_like(acc)

        av = a_ref[...]
        for b_ref, acc in zip(b_refs, accs):
            acc[...] += _dot_tn(av, b_ref[...])

        @pl.when(k == nk - 1)
        def _():
            for o_ref, acc in zip(o_refs, accs):
                o_ref[...] = acc[...].reshape(NQ, M // NQ, N).astype(o_ref.dtype)

    return _pcall(
        body, name=name, grid=(nk,),
        out_shape=tuple(jax.ShapeDtypeStruct((NQ, M // NQ, N), BF16) for _ in bs),
        in_specs=[pl.BlockSpec((tk, M), lambda k: (k, 0))] + [pl.BlockSpec((tk, N), lambda k: (k, 0)) for _ in bs],
        out_specs=tuple(pl.BlockSpec((NQ, M // NQ, N), lambda k: (0, 0, 0)) for _ in bs),
        scratch_shapes=[pltpu.VMEM((M, N), F32) for _ in bs],
        operands=(a, *bs), **comm)


def _lru_bwd(dx1, y, x, xr0, gb, hs, xc_, gate_r_, gate_i_, a_, mult_, mod_l, g_mix, g_wout, g_wy, g_win, cw, wa, wx,
             lam, **comm):
    S, W = xr0.shape
    tt = min(TT, S)
    nb = S // tt
    hd = W // HEADS

    def body(dx1_ref, y_ref, x_ref, xr_ref, gb_ref, hs_ref, hsh_ref, xc_ref, gr_ref, gi_ref, a_s, mu_ref,
             mod_ref, gm_ref, wo_ref, wy_ref, win_ref, cw_ref, wa_ref, wx_ref, lam_ref,
             dy_ref, dgb_ref, dxr_ref, dx_ref, sm_ref, dwa_ref, dwx_ref, acc_ref,
             hext, qext, dext, b_s, qc, dc):
        i = pl.program_id(0)
        blk = nb - 1 - i

        @pl.when(i == 0)
        def _():
            sm_ref[...] = jnp.zeros_like(sm_ref)
            dwa_ref[...] = jnp.zeros_like(dwa_ref)
            dwx_ref[...] = jnp.zeros_like(dwx_ref)
            acc_ref[...] = jnp.zeros_like(acc_ref)
            qc[...] = jnp.zeros_like(qc)
            dc[...] = jnp.zeros_like(dc)

        dx1v = dx1_ref[...]
        acc_ref[2:3, :] += jnp.sum(dx1v * y_ref[...].astype(F32), axis=0, keepdims=True)
        dy = dx1v * mod_ref[2:3, :]
        acc_ref[3:4, :] += jnp.sum(dy, axis=0, keepdims=True)
        dyb = dy.astype(BF16)
        dy_ref[...] = dyb
        dpv = _dot_nt(dyb, wo_ref[...].reshape(W, W))

        hext[0:8, :] = jnp.where(blk > 0, hsh_ref[...], 0.0)
        hext[pl.ds(8, tt), :] = hs_ref[...]
        cw = cw_ref[...]
        lam = lam_ref[...]
        xc, gate_r, gate_i, a, mult = xc_ref[...], gr_ref[...], gi_ref[...], a_s[...], mu_ref[...]
        xcb = xc.astype(BF16)
        ls = jax.nn.log_sigmoid(lam)

        gbv = gb_ref[...]
        gate, th = _gelu(gbv)
        dgb = dpv * hs_ref[...] * _gelu_grad(gbv, th)
        dgbb = dgb.astype(BF16)
        dgb_ref[...] = dgbb
        sm_ref[9:10, :] += jnp.sum(dgb, axis=0, keepdims=True)
        dhs = dpv * gate

        b_s[...] = a * dhs
        qext[pl.ds(tt, 8), :] = qc[...]
        row = lax.broadcasted_iota(jnp.int32, (8, W), 0)

        def step(k, _):
            off = pl.multiple_of((tt // 8 - 1 - k) * 8, 8)
            A = a_s[pl.ds(off, 8), :]
            B = b_s[pl.ds(off, 8), :]
            for d in (1, 2, 4):
                keep = row < 8 - d
                Bs = jnp.where(keep, pltpu.roll(B, 8 - d, 0), 0.0)
                As = jnp.where(keep, pltpu.roll(A, 8 - d, 0), 1.0)
                B = B + A * Bs
                A = A * As
            Q = B + A * qc[...]
            qext[pl.ds(off, 8), :] = Q
            qc[...] = jnp.broadcast_to(Q[0:1, :], (8, W))
            return 0

        lax.fori_loop(0, tt // 8, step, 0)
        gsc = dhs + qext[pl.ds(1, tt), :]
        da = gsc * hext[pl.ds(7, tt), :]
        t1 = gsc * xc
        dmult = t1 * gate_i
        dgate_i = t1 * mult
        dxc = gsc * (mult * gate_i)
        dlog_a = da * a - dmult * (a * a) / mult
        dgate_r = dlog_a * (LRU_C * ls)
        sm_ref[7:8, :] += jnp.sum(dlog_a * (LRU_C * gate_r), axis=0, keepdims=True)
        dga = dgate_r * gate_r * (1.0 - gate_r)
        dgx = dgate_i * gate_i * (1.0 - gate_i)
        sm_ref[5:6, :] += jnp.sum(dga, axis=0, keepdims=True)
        sm_ref[6:7, :] += jnp.sum(dgx, axis=0, keepdims=True)
        dgab = dga.astype(BF16)
        dgxb = dgx.astype(BF16)
        dxc = dxc + _heads_dot(dgab, wa_ref, hd, nt=True) + _heads_dot(dgxb, wx_ref, hd, nt=True)
        for h in range(HEADS):
            sl = slice(h * hd, (h + 1) * hd)
            dwa_ref[h] += _dot_tn(xcb[:, sl], dgab[:, sl])
            dwx_ref[h] += _dot_tn(xcb[:, sl], dgxb[:, sl])
        sm_ref[4:5, :] += jnp.sum(dxc, axis=0, keepdims=True)
        dext[pl.ds(0, tt), :] = dxc
        dext[pl.ds(tt, 8), :] = dc[...]
        xrv = xr_ref[...]
        dxr = None
        for k in range(CONV_W):
            up = dext[pl.ds(CONV_W - 1 - k, tt), :]
            sm_ref[k:k + 1, :] += jnp.sum(up * xrv, axis=0, keepdims=True)
            dxr = cw[k:k + 1, :] * up if dxr is None else dxr + cw[k:k + 1, :] * up
        dc[...] = dext[0:8, :]
        sm_ref[8:9, :] += jnp.sum(dxr, axis=0, keepdims=True)
        dxrb = dxr.astype(BF16)
        dxr_ref[...] = dxrb

        dh = _dot_nt(dxrb, win_ref[...].reshape(W, W)) + _dot_nt(dgbb, wy_ref[...].reshape(W, W))
        g, sc = gm_ref[...], mod_ref[1:2, :]
        _, xhat, r, n = _rms_fwd(x_ref[...], g, sc, mod_ref[0:1, :])
        dx, dsh, dsc, dg = _rms_bwd(dh, xhat, r, n, g, sc)
        acc_ref[0:1, :] += dsh
        acc_ref[1:2, :] += dsc
        acc_ref[6:7, :] += dg
        dx_ref[...] = dx1v + dx

        @pl.when(i == nb - 1)
        def _():
            sm_ref[7:8, :] = sm_ref[7:8, :] * jax.nn.sigmoid(-lam)

    rev = lambda i: (nb - 1 - i, 0)
    tile = pl.BlockSpec((tt, W), rev)
    halo = pl.BlockSpec((8, W), lambda i: (jnp.maximum((nb - 1 - i) * (tt // 8) - 1, 0), 0))
    row = pl.BlockSpec((1, W), lambda i: (0, 0))
    wblk = pl.BlockSpec((HEADS, hd, hd), lambda i: (0, 0, 0))
    bf16o = jax.ShapeDtypeStruct((S, W), BF16)
    return _pcall(
        body, name="lru_bwd", grid=(nb,),
        out_shape=(bf16o, bf16o, bf16o, jax.ShapeDtypeStruct((S, W), F32),
                   jax.ShapeDtypeStruct((16, W), F32), jax.ShapeDtypeStruct((HEADS, hd, hd), F32),
                   jax.ShapeDtypeStruct((HEADS, hd, hd), F32), jax.ShapeDtypeStruct((8, W), F32)),
        in_specs=[tile, tile, tile, tile, tile, tile, halo, tile, tile, tile, tile, tile,
                  pl.BlockSpec((8, W), lambda i: (0, 0)), row,
                  _wspec(g_wout), _wspec(g_wy), _wspec(g_win), pl.BlockSpec((CONV_W, W), lambda i: (0, 0)),
                  wblk, wblk, row],
        out_specs=(tile, tile, tile, tile, pl.BlockSpec((16, W), lambda i: (0, 0)), wblk, wblk,
                   pl.BlockSpec((8, W), lambda i: (0, 0))),
        scratch_shapes=[pltpu.VMEM((tt + 8, W), F32), pltpu.VMEM((tt + 8, W), F32), pltpu.VMEM((tt + 8, W), F32),
                        pltpu.VMEM((tt, W), F32), pltpu.VMEM((8, W), F32), pltpu.VMEM((8, W), F32)],
        operands=(dx1, y, x, xr0, gb, hs, hs, xc_, gate_r_, gate_i_, a_, mult_, mod_l, g_mix, g_wout, g_wy, g_win,
                  cw, wa, wx, lam),
        **comm)


def _pool_bwd(dx1, x, pooled, mod_l, g_mix, pw, ps, h2, du, a, dz):
    S, D = x.shape
    tm = min(TP, S)
    nb = S // tm
    ng = len(POOL_WINDOWS)
    gd = D // ng
    n = tm + 24
    f4 = du.shape[1] // NQ
    assert nb % NQ == 0
    kch = nb // NQ
    kr = S // kch

    def body(dx1_ref, dxh_ref, x_ref, pl_ref, mod_ref, gm_ref, pw_ref, ps_ref, h2_ref, du_ref, a_ref, dz_ref,
             dx_ref, acc_ref, dpw_ref, dw1_ref, dw2_ref, ext, b1, b2, acc1, acc2):
        i = pl.program_id(0)

        @pl.when(i == 0)
        def _():
            acc_ref[...] = jnp.zeros_like(acc_ref)
            dpw_ref[...] = jnp.zeros_like(dpw_ref)

        @pl.when(i % kch == 0)
        def _():
            acc1[...] = jnp.zeros_like(acc1)
            acc2[...] = jnp.zeros_like(acc2)

        acc1[...] += _dot_tn(h2_ref[...], du_ref[...])
        av = a_ref[...]
        acc2[...] += _dot_tn(av * av, dz_ref[...])

        gt, psv = mod_ref[2:3, :], ps_ref[...]
        wvec = _window_vec(D)
        dx1v = dx1_ref[...]
        pooled = pl_ref[...]
        mixed = _heads_dot(pooled, pw_ref, gd)
        acc_ref[2:3, :] += jnp.sum(dx1v * (mixed * psv), axis=0, keepdims=True)
        dy = dx1v * gt
        acc_ref[3:4, :] += jnp.sum(dy * mixed, axis=0, keepdims=True)
        dmix = (dy * psv).astype(BF16)
        for gi in range(ng):
            sl = slice(gi * gd, (gi + 1) * gd)
            dpw_ref[gi] += _dot_tn(pooled[:, sl], dmix[:, sl])
        dpooled = _heads_dot(dmix, pw_ref, gd, nt=True)
        dmix_h = (dxh_ref[...] * gt * psv).astype(BF16)
        dpooled_h = jnp.where(i < nb - 1, _heads_dot(dmix_h, pw_ref, gd, nt=True), 0.0)
        t1 = (lax.broadcasted_iota(jnp.int32, (tm, 1), 0) + (i * tm + 1)).astype(F32)
        t1h = (lax.broadcasted_iota(jnp.int32, (16, 1), 0) + ((i + 1) * tm + 1)).astype(F32)
        zeros8 = jnp.zeros((8, D), F32)
        ext[pl.ds(0, tm), :] = dpooled / jnp.minimum(t1, wvec)
        ext[pl.ds(tm, 16), :] = dpooled_h / jnp.minimum(t1h, wvec)
        ext[pl.ds(tm + 16, 8), :] = zeros8
        b1[pl.ds(tm + 16, 8), :] = zeros8
        b2[pl.ds(tm + 16, 8), :] = zeros8
        m = n - 8
        b1[pl.ds(0, m), :] = ext[pl.ds(0, m), :] + ext[pl.ds(1, m), :]
        b2[pl.ds(0, m), gd:] = b1[pl.ds(0, m), gd:] + b1[pl.ds(2, m), gd:]
        b1[pl.ds(0, m), 2 * gd:] = b2[pl.ds(0, m), 2 * gd:] + b2[pl.ds(4, m), 2 * gd:]
        b2[pl.ds(0, m), 3 * gd:] = b1[pl.ds(0, m), 3 * gd:] + b1[pl.ds(8, m), 3 * gd:]
        wsum = jnp.concatenate([b1[pl.ds(0, tm), 0:gd], b2[pl.ds(0, tm), gd:2 * gd],
                                b1[pl.ds(0, tm), 2 * gd:3 * gd], b2[pl.ds(0, tm), 3 * gd:]], axis=1)
        dh = wsum - dpooled
        g, sc = gm_ref[...], mod_ref[1:2, :]
        _, xhat, r, nn = _rms_fwd(x_ref[...], g, sc, mod_ref[0:1, :])
        dx, dsh, dsc, dg = _rms_bwd(dh, xhat, r, nn, g, sc)
        acc_ref[0:1, :] += dsh
        acc_ref[1:2, :] += dsc
        acc_ref[6:7, :] += dg
        dx_ref[...] = dx1v + dx

        @pl.when(i % kch == kch - 1)
        def _():
            dw1_ref[...] = acc1[...].astype(BF16)
            dw2_ref[...] = acc2[...].astype(BF16)

    tile = pl.BlockSpec((tm, D), lambda i: (i, 0))
    halo = pl.BlockSpec((16, D), lambda i: (jnp.minimum((i + 1) * (tm // 16), S // 16 - 1), 0))
    row = pl.BlockSpec((1, D), lambda i: (0, 0))
    wblk = pl.BlockSpec((ng, gd, gd), lambda i: (0, 0, 0))
    full_k = pl.BlockSpec((kr, D), lambda i: (i % kch, 0))
    part_k = pl.BlockSpec((kr, f4), lambda i: (i % kch, i // kch))
    return pl.pallas_call(
        body, name="pool_bwd", grid=(nb,),
        out_shape=(jax.ShapeDtypeStruct((S, D), F32), jax.ShapeDtypeStruct((8, D), F32),
                   jax.ShapeDtypeStruct((ng, gd, gd), F32), jax.ShapeDtypeStruct((NQ, D, f4), BF16),
                   jax.ShapeDtypeStruct((NQ, f4, D), BF16)),
        in_specs=[tile, halo, tile, tile, pl.BlockSpec((8, D), lambda i: (0, 0)), row, wblk, row,
                  full_k, part_k, part_k, full_k],
        out_specs=(tile, pl.BlockSpec((8, D), lambda i: (0, 0)), wblk,
                   pl.BlockSpec((None, D, f4), lambda i: (i // kch, 0, 0)),
                   pl.BlockSpec((None, f4, D), lambda i: (i // kch, 0, 0))),
        scratch_shapes=[pltpu.VMEM((n, D), F32), pltpu.VMEM((n, D), F32), pltpu.VMEM((n, D), F32),
                        pltpu.VMEM((D, f4), F32), pltpu.VMEM((f4, D), F32)],
        compiler_params=_cparams("arbitrary"),
    )(dx1, dx1, x, pooled, mod_l, g_mix, pw, ps, h2, du, a, dz)


def _shard_to_rows(w, D):
    return w.reshape(-1, D)


def _blockdiag_full(gq, na, hd):
    return gq.reshape(NQ, na, HEADS, hd // NQ, hd).transpose(1, 2, 0, 3, 4).reshape(na, HEADS, hd, hd)


def _blockdiag_by_chip(dw, D):
    na, _, hd, _ = dw.shape
    return dw.reshape(na, HEADS, NQ, hd // NQ, hd).transpose(2, 0, 1, 3, 4).reshape(NQ, -1, D)


def kernel(x, c, w_mod, b_mod, norm_mix_g, norm_ffn_g, lru_w_y, lru_b_y, lru_w_in, lru_b_in, lru_conv_w, lru_conv_b, lru_w_a, lru_b_a, lru_w_x, lru_b_x, lru_lambda, lru_w_out, lru_b_out, pool_w, pool_scale, ffn_w1, ffn_w2, final_norm_g, loss_target, m_w_mod, m_b_mod, m_norm_mix_g, m_norm_ffn_g, m_lru_w_y, m_lru_b_y, m_lru_w_in, m_lru_b_in, m_lru_conv_w, m_lru_conv_b, m_lru_w_a, m_lru_b_a, m_lru_w_x, m_lru_b_x, m_lru_lambda, m_lru_w_out, m_lru_b_out, m_pool_w, m_pool_scale, m_ffn_w1, m_ffn_w2, m_final_norm_g, v_w_mod, v_b_mod, v_norm_mix_g, v_norm_ffn_g, v_lru_w_y, v_lru_b_y, v_lru_w_in, v_lru_b_in, v_lru_conv_w, v_lru_conv_b, v_lru_w_a, v_lru_b_a, v_lru_w_x, v_lru_b_x, v_lru_lambda, v_lru_w_out, v_lru_b_out, v_pool_w, v_pool_scale, v_ffn_w1, v_ffn_w2, v_final_norm_g):
    S, D = x.shape[1], x.shape[2]
    L = w_mod.shape[0]
    NA = lru_w_y.shape[0]
    NB = pool_w.shape[0]
    F = ffn_w1.shape[2] * NQ
    f4 = F // NQ
    hd = D // HEADS
    Cs = w_mod.shape[2]
    assert L == DEPTH and Cs * NQ == N_MOD * D and D % 1024 == 0
    x2d = x.reshape(S, D)
    tgt = loss_target.reshape(S, D)
    q = 2 * lax.axis_index("x") + lax.axis_index("y")

    big = [ffn_w1, ffn_w2, lru_w_y, lru_w_in, lru_w_out, lru_w_a, lru_w_x, pool_w]
    rows = [int(w.size) // D for w in big]
    offs = [sum(rows[:k]) for k in range(len(big))]
    O_W1, O_W2, O_WY, O_WIN, O_WOUT, O_WA, O_WX, O_PW = offs
    R = sum(rows)
    dq = D // NQ
    s_w1 = [ffn_w1[i].astype(BF16) for i in range(L)]
    s_w2 = [ffn_w2[i].astype(BF16) for i in range(L)]
    s_wy = [lru_w_y[j].astype(BF16) for j in range(NA)]
    s_win = [lru_w_in[j].astype(BF16) for j in range(NA)]
    s_wout = [lru_w_out[j].astype(BF16) for j in range(NA)]
    s_tiny = jnp.concatenate([_shard_to_rows(w, D) for w in (lru_w_a, lru_w_x, pool_w)], axis=0).astype(BF16)

    cshard = lru_conv_w.reshape(-1)
    small_fwd = jnp.concatenate([c.reshape(-1), cshard, lru_b_a.reshape(-1), lru_b_x.reshape(-1),
                                 pool_scale.reshape(-1)])
    small_fwd = jnp.pad(small_fwd, (0, 8 * D - small_fwd.shape[0])).reshape(8, D)

    g_w1, g_w2 = [None] * L, [None] * L
    g_wy, g_win, g_wout = [None] * NA, [None] * NA, [None] * NA
    SG, g_wy[0], g_win[0], g_wout[0], g_tiny = _comm_only("gather_first", small=small_fwd,
                                                         gathers=(s_wy[0], s_win[0], s_wout[0], s_tiny))
    wa_full = _blockdiag_full(g_tiny[:, :rows[5]], NA, hd)
    wx_full = _blockdiag_full(g_tiny[:, rows[5]:rows[5] + rows[6]], NA, hd)
    pw_full = _blockdiag_full(g_tiny[:, rows[5] + rows[6]:], NB, hd)
    SGf = SG.reshape(NDEV, 8 * D)
    c_all = SGf[:, :D]
    SGq = SGf.reshape(NQ, 2, 8 * D)[:, 0]
    o = D
    n_cw = NA * CONV_W * D // NQ
    conv_w_full = SGq[:, o:o + n_cw].reshape(NQ, NA, CONV_W, D // NQ).transpose(1, 2, 0, 3).reshape(NA, CONV_W, D)
    o += n_cw
    n_b = NA * HEADS * hd // NQ
    b_a_full = SGq[:, o:o + n_b].reshape(NQ, NA, HEADS, hd // NQ).transpose(1, 2, 0, 3).reshape(NA, 1, D)
    o += n_b
    b_x_full = SGq[:, o:o + n_b].reshape(NQ, NA, HEADS, hd // NQ).transpose(1, 2, 0, 3).reshape(NA, 1, D)
    o += n_b
    n_ps = NB * D // NQ
    pool_scale_full = SGq[:, o:o + n_ps].reshape(NQ, NB, D // NQ).transpose(1, 0, 2).reshape(NB, 1, D)


    b_mod_sh = lax.dynamic_slice_in_dim(b_mod, q * Cs, Cs, axis=1).reshape(L, 1, Cs)
    modpart = _mod_part(c_all, w_mod, b_mod_sh)
    modq = _exchange_mod(modpart.transpose(1, 0, 2))
    mod = modq.transpose(1, 0, 2).reshape(L, N_MOD, D)
    mod = jnp.pad(mod, ((0, 0), (0, 8 - N_MOD), (0, 0)))

    saved = []
    xcur = x2d
    for i in range(L):
        j = i // 2
        gm = norm_mix_g[i].reshape(1, D)
        gf = norm_ffn_g[i].reshape(1, D)
        if i % 2 == 0:
            h, gb, xr0, hs, p, y, x1, xc_s, gr_s, gi_s, a_sv, mu_s, g_w1[i], g_w2[i] = _lru_fwd(
                xcur, mod[i], gm, g_wy[j], g_win[j], lru_b_y[j].reshape(1, D), lru_b_in[j].reshape(1, D),
                conv_w_full[j], lru_conv_b[j].reshape(1, D), wa_full[j], b_a_full[j], wx_full[j], b_x_full[j],
                lru_lambda[j].reshape(1, D), g_wout[j], lru_b_out[j].reshape(1, D), gathers=(s_w1[i], s_w2[i]))
            h2, a, z, x2, g_w1[i + 1], g_w2[i + 1] = _ffn_fwd(x1, mod[i], gf, g_w1[i], g_w2[i],
                                                              gathers=(s_w1[i + 1], s_w2[i + 1]))
            saved.append(dict(x=xcur, h=h, gb=gb, xr0=xr0, hs=hs, p=p, y=y, x1=x1, h2=h2, a=a, z=z,
                              lru=(xc_s, gr_s, gi_s, a_sv, mu_s)))
        else:
            if j + 1 < NA:
                pooled, x1, h2, a, z, x2, g_wy[j + 1], g_win[j + 1], g_wout[j + 1] = _pool_mix_ffn_fwd(
                    xcur, mod[i], gm, pw_full[j], pool_scale_full[j], gf, g_w1[i], g_w2[i],
                    gathers=(s_wy[j + 1], s_win[j + 1], s_wout[j + 1]))
            else:
                pooled, x1, h2, a, z, x2 = _pool_mix_ffn_fwd(xcur, mod[i], gm, pw_full[j], pool_scale_full[j], gf,
                                                             g_w1[i], g_w2[i])
            saved.append(dict(x=xcur, pooled=pooled, x1=x1, h2=h2, a=a, z=z))
        xcur = x2

    dx = xcur
    qv = q.reshape(1).astype(jnp.int32)
    ppack = lax.empty((R, D), F32)
    psib = lax.empty((R, D), F32)
    pending, summed = [], []

    def comm_args():
        kw = {}
        if pending:
            kw["scatters"] = tuple(dw for dw, _ in pending)
        if summed:
            kw["sib"] = (ppack, psib, tuple(summed))
        return kw

    def after_host(extra):
        nonlocal ppack, psib, pending, summed
        had_sib = bool(summed)
        summed = []
        for (dw, off), rb in zip(pending, extra[:len(pending)]):
            ppack = _sum_into(ppack, dw, rb, off, qv)
            summed.append((off, dw.shape[1]))
        if had_sib:
            psib = extra[len(pending)]
        pending = []

    dmod_rows = [None] * L
    dg_mix = [None] * L
    dg_ffn = [None] * L
    d_small = {}
    dwa_l, dwx_l, dpw_l = [None] * NA, [None] * NA, [None] * NB
    for i in reversed(range(L)):
        j = i // 2
        sv = saved[i]
        gm = norm_mix_g[i].reshape(1, D)
        gf = norm_ffn_g[i].reshape(1, D)
        head = (final_norm_g.reshape(1, D), tgt) if i == L - 1 else None
        outs = _ffn_bwd(dx, sv["x1"], sv["a"], sv["z"], mod[i], gf, g_w1[i], g_w2[i], head=head, **comm_args())
        dx1, du, dz, facc = outs[:4]
        after_host(outs[4:])
        if head:
            loss = lax.psum(0.5 * jnp.sum(facc[0]) / D, ("x", "y", "c"))
            d_final_g = facc[1]
        if i % 2 == 0:
            pending.append((_dw_blocked(sv["h2"], du, False, False, "dw1"), O_W1 + i * D))
            pending.append((_dw_blocked(sv["a"], dz, True, True, "dw2"), O_W2 + i * f4))
            outs = _lru_bwd(dx1, sv["y"], sv["x"], sv["xr0"], sv["gb"], sv["hs"], *sv["lru"], mod[i], gm, g_wout[j],
                            g_wy[j], g_win[j], conv_w_full[j], wa_full[j], wx_full[j], lru_lambda[j].reshape(1, D),
                            **comm_args())
            dyp, dgb, dxr, dx, sm, dwa, dwx, macc = outs[:8]
            after_host(outs[8:])
            dwa_l[j], dwx_l[j] = dwa, dwx
            if i == 0:
                tiny = jnp.concatenate([_blockdiag_by_chip(jnp.stack(dwa_l), D), _blockdiag_by_chip(jnp.stack(dwx_l), D),
                                        _blockdiag_by_chip(jnp.stack(dpw_l), D)], axis=1).astype(BF16)
                pending.append((tiny, O_WA))
            if i == 0:
                outs = _dw_whole(sv["h"], [dgb, dxr], "dwy_dwin", **comm_args())
                after_host(outs[2:])
                pending.append((outs[0], O_WY + j * dq))
                pending.append((outs[1], O_WIN + j * dq))
                outs = _dw_whole(sv["p"], [dyp], "dwout", **comm_args())
                after_host(outs[1:])
                pending.append((outs[0], O_WOUT + j * dq))
            else:
                outs = _dw_whole(sv["p"], [dyp], "dwout", **comm_args())
                after_host(outs[1:])
                pending.append((outs[0], O_WOUT + j * dq))
                outs = _dw_whole(sv["h"], [dgb, dxr], "dwy_dwin", **comm_args())
                after_host(outs[2:])
                pending.append((outs[0], O_WY + j * dq))
                pending.append((outs[1], O_WIN + j * dq))
            d_small[("lru", j)] = (sm, macc[3])
            dgt_m = macc[2]
        else:
            dx, macc, dpw, dw1, dw2 = _pool_bwd(dx1, sv["x"], sv["pooled"], mod[i], gm, pw_full[j], pool_scale_full[j],
                                                sv["h2"], du, sv["a"], dz)
            pending.append((dw1, O_W1 + i * D))
            pending.append((dw2, O_W2 + i * f4))
            dpw_l[j] = dpw
            d_small[("pool", j)] = macc[3]
            dgt_m = macc[2]
        dmod_rows[i] = jnp.stack([macc[0], macc[1], dgt_m, facc[3], facc[4], facc[5]])
        dg_mix[i] = macc[6]
        dg_ffn[i] = facc[7]
    grad_x = dx.reshape(x.shape)

    lru_sm = [d_small[("lru", j)] for j in range(NA)]
    small_rows = [jnp.stack(dmod_rows).reshape(L * N_MOD, D), jnp.stack(dg_mix), jnp.stack(dg_ffn),
                  jnp.stack([s[0][9] for s in lru_sm]), jnp.stack([s[0][8] for s in lru_sm]),
                  jnp.stack([s[0][4] for s in lru_sm]), jnp.stack([s[0][7] for s in lru_sm]),
                  jnp.stack([s[1] for s in lru_sm]),
                  jnp.stack([s[0][0:CONV_W] for s in lru_sm]).reshape(NA * CONV_W, D),
                  jnp.stack([s[0][5] for s in lru_sm]), jnp.stack([s[0][6] for s in lru_sm]),
                  jnp.stack([d_small[("pool", j)] for j in range(NB)]), d_final_g.reshape(1, D)]
    small_g = jnp.concatenate(small_rows, axis=0)
    n_small = small_g.shape[0]
    assert n_small <= SMALL_ROWS
    small_g = jnp.pad(small_g, ((0, SMALL_ROWS - n_small), (0, 0)))

    outs = _comm_only("scatter_last", small=small_g, reduce_small=True, **comm_args())
    sg_all, sg_sum = outs[:2]
    after_host(outs[2:])
    psum_mine = ppack
    psum_sib = _comm_only("sibling_last", sib=(ppack, psib, tuple(summed)))[0]

    def big_update(w, m, v, off, name):
        shp = w.shape
        g, dl, m2, v2 = _adam_rows(w.reshape(-1, D), m.reshape(-1, D), v.reshape(-1, D), psum_mine, psum_sib, off, name)
        return g.reshape(shp), dl.reshape(shp), m2.reshape(shp), v2.reshape(shp)

    res = {}
    res["ffn_w1"] = big_update(ffn_w1, m_ffn_w1, v_ffn_w1, O_W1, "adam_w1")
    res["ffn_w2"] = big_update(ffn_w2, m_ffn_w2, v_ffn_w2, O_W2, "adam_w2")
    res["lru_w_y"] = big_update(lru_w_y, m_lru_w_y, v_lru_w_y, O_WY, "adam_wy")
    res["lru_w_in"] = big_update(lru_w_in, m_lru_w_in, v_lru_w_in, O_WIN, "adam_win")
    res["lru_w_out"] = big_update(lru_w_out, m_lru_w_out, v_lru_w_out, O_WOUT, "adam_wout")

    def tiny_parts(w, off):
        n = int(w.size) // D
        return psum_mine[off:off + n].reshape(w.shape), psum_sib[off:off + n].reshape(w.shape)

    tiny_items = [("lru_w_a", lru_w_a, m_lru_w_a, v_lru_w_a) + tiny_parts(lru_w_a, O_WA),
                  ("lru_w_x", lru_w_x, m_lru_w_x, v_lru_w_x) + tiny_parts(lru_w_x, O_WX),
                  ("pool_w", pool_w, m_pool_w, v_pool_w) + tiny_parts(pool_w, O_PW)]

    dmod_all = sg_all[:, :L * N_MOD, :].reshape(NDEV, L, N_MOD * D)
    dmod_sh = lax.dynamic_slice_in_dim(dmod_all, q * Cs, Cs, axis=2).transpose(1, 0, 2)
    res["w_mod"] = _wmod_update(c_all.T, dmod_sh, w_mod, m_w_mod, v_w_mod)

    r0 = 0
    by_rows, names_a = [], []
    for name, w, m, v in (("b_mod", b_mod, m_b_mod, v_b_mod), ("norm_mix_g", norm_mix_g, m_norm_mix_g, v_norm_mix_g),
                          ("norm_ffn_g", norm_ffn_g, m_norm_ffn_g, v_norm_ffn_g),
                          ("lru_b_y", lru_b_y, m_lru_b_y, v_lru_b_y), ("lru_b_in", lru_b_in, m_lru_b_in, v_lru_b_in),
                          ("lru_conv_b", lru_conv_b, m_lru_conv_b, v_lru_conv_b),
                          ("lru_lambda", lru_lambda, m_lru_lambda, v_lru_lambda),
                          ("lru_b_out", lru_b_out, m_lru_b_out, v_lru_b_out)):
        by_rows.append((w, m, v, r0))
        names_a.append(name)
        r0 += int(w.size) // D
    g_conv_w = lax.dynamic_slice_in_dim(sg_sum[r0:r0 + NA * CONV_W].reshape(NA, CONV_W, D), q * dq, dq, axis=2)
    r0 += NA * CONV_W
    g_b_a = lax.dynamic_slice_in_dim(sg_sum[r0:r0 + NA].reshape(NA, HEADS, hd), q * (hd // NQ), hd // NQ, axis=2)
    r0 += NA
    g_b_x = lax.dynamic_slice_in_dim(sg_sum[r0:r0 + NA].reshape(NA, HEADS, hd), q * (hd // NQ), hd // NQ, axis=2)
    r0 += NA
    g_ps = lax.dynamic_slice_in_dim(sg_sum[r0:r0 + NB], q * dq, dq, axis=1)
    r0 += NB
    by_rows.append((final_norm_g.reshape(1, D), m_final_norm_g.reshape(1, D), v_final_norm_g.reshape(1, D), r0))
    names_a.append("final_norm_g")
    sliced = [(lru_conv_w, m_lru_conv_w, v_lru_conv_w, g_conv_w), (lru_b_a, m_lru_b_a, v_lru_b_a, g_b_a),
              (lru_b_x, m_lru_b_x, v_lru_b_x, g_b_x), (pool_scale, m_pool_scale, v_pool_scale, g_ps)]
    res_a, res_b, res_c = _adam_small(sg_sum, by_rows, sliced, [t[1:] for t in tiny_items])
    for name, r in zip(names_a, res_a):
        res[name] = r
    for t, r in zip(tiny_items, res_c):
        res[t[0]] = r
    res["final_norm_g"] = tuple(a.reshape(D) for a in res["final_norm_g"])
    for name, (_, _, _, g), r in zip(("lru_conv_w", "lru_b_a", "lru_b_x", "pool_scale"), sliced, res_b):
        res[name] = (g,) + r

    order = ["w_mod", "b_mod", "norm_mix_g", "norm_ffn_g", "lru_w_y", "lru_b_y", "lru_w_in", "lru_b_in", "lru_conv_w",
             "lru_conv_b", "lru_w_a", "lru_b_a", "lru_w_x", "lru_b_x", "lru_lambda", "lru_w_out", "lru_b_out", "pool_w",
             "pool_scale", "ffn_w1", "ffn_w2", "final_norm_g"]
    return (loss, grad_x, *[res[n][0] for n in order], *[res[n][1] for n in order],
            *[res[n][2] for n in order], *[res[n][3] for n in order])
```

```python
import jax
import jax.numpy as jnp
from jax import lax
from jax.experimental import pallas as pl
from jax.experimental.pallas import tpu as pltpu

F32 = jnp.float32
BF16 = jnp.bfloat16
MESH = pl.DeviceIdType.MESH

NQ = 4
NDEV = 8
DEPTH = 4
N_MOD = 6
HEADS = 4
CONV_W = 4
POOL_WINDOWS = (2, 4, 8, 16)
LRU_C = 8.0
EPS = 1e-6
ADAM_LR, ADAM_B1, ADAM_B2, ADAM_EPS, ADAM_WD, ADAM_STEP = 0.001, 0.9, 0.999, 1e-08, 0.01, 10

TM = 512
TT = 256
TP = 256
TPF = 512
TK = 2048
SMALL_ROWS = 64
FORWARD_STEPS = 4
VMEM_LIMIT = 60 * 1024 * 1024


def _cparams(*sem):
    return pltpu.CompilerParams(dimension_semantics=tuple(sem), vmem_limit_bytes=VMEM_LIMIT)


def _dot(a, b):
    return jnp.dot(a, b, preferred_element_type=F32)


def _dot_nt(a, b):
    return lax.dot_general(a, b, (((1,), (1,)), ((), ())), preferred_element_type=F32)


def _dot_tn(a, b):
    return lax.dot_general(a, b, (((0,), (0,)), ((), ())), preferred_element_type=F32)


def _resident(shape, index_map):
    return pl.BlockSpec(shape, index_map, pipeline_mode=pl.Buffered(1))


def _rms_fwd(x, g, sc, sh):
    r = lax.rsqrt(jnp.mean(x * x, axis=-1, keepdims=True) + EPS)
    xhat = x * r
    n = xhat * g
    return n * (1.0 + sc) + sh, xhat, r, n


def _rms_bwd(dh, xhat, r, n, g, sc):
    dsh = jnp.sum(dh, axis=0, keepdims=True)
    dsc = jnp.sum(dh * n, axis=0, keepdims=True)
    dn = dh * (1.0 + sc)
    dg = jnp.sum(dn * xhat, axis=0, keepdims=True)
    dxh = dn * g
    dx = r * (dxh - xhat * jnp.mean(dxh * xhat, axis=-1, keepdims=True))
    return dx, dsh, dsc, dg


_GELU_K = 0.7978845608028654
_GELU_C = 0.044715


def _gelu(x):
    t = jnp.tanh(_GELU_K * (x + _GELU_C * x * x * x))
    return 0.5 * x * (1.0 + t), t


def _gelu_grad(x, t):
    return 0.5 * (1.0 + t) + 0.5 * x * (1.0 - t * t) * (_GELU_K * (1.0 + 3.0 * _GELU_C * x * x))


def _neg_expm1(y, exp_y):
    series = -(y * (1.0 + y * (0.5 + y * (1.0 / 6.0))))
    return jnp.where(y > -(1.0 / 64.0), series, 1.0 - exp_y)


def _zero_first(ref):
    @pl.when(pl.program_id(0) == 0)
    def _():
        ref[...] = jnp.zeros_like(ref)


def _my_pos():
    return lax.axis_index("x"), lax.axis_index("y"), lax.axis_index("c")


def _dev_index(x, y, c):
    return 4 * x + 2 * y + c


def _chip_peers(x, y):
    return [(1 - x, y), (x, 1 - y), (1 - x, 1 - y)]


def _all_peers(x, y, c):
    return [(px, py, c) for (px, py) in _chip_peers(x, y)] + [(x, y, 1 - c)] + \
           [(px, py, 1 - c) for (px, py) in _chip_peers(x, y)]


def _comm_run(phase, x, y, c, gathers, scatters, sib, send, recv, loc):
    q = 2 * x + y
    peers = _chip_peers(x, y)
    sibling = (x, y, 1 - c)

    def rcopy(src, dst, s, dev):
        return pltpu.make_async_remote_copy(src, dst, send.at[s], recv.at[s], device_id=dev, device_id_type=MESH)

    s = 0
    for gi, (src, dst) in enumerate(gathers):
        half = src.shape[0] // 2
        mine, other = pl.ds(c * half, half), pl.ds((1 - c) * half, half)
        own = pltpu.make_async_copy(src, dst.at[q], loc.at[gi])
        if phase == "start":
            own.start()
        elif phase == "finish":
            own.wait()
        for (px, py) in peers:
            pq = 2 * px + py
            s_ici, s_fwd = s, s + 1
            s += 2
            if phase == "start":
                rcopy(src.at[mine], dst.at[q].at[mine], s_ici, (px, py, c)).start()
            elif phase == "forward":
                rcopy(src.at[mine], dst.at[pq].at[mine], s_ici, (px, py, c)).wait_recv()
                rcopy(dst.at[pq].at[mine], dst.at[pq].at[mine], s_fwd, sibling).start()
            else:
                rcopy(dst.at[pq].at[other], dst.at[pq].at[other], s_fwd, sibling).wait_recv()
                rcopy(src.at[mine], dst.at[q].at[mine], s_ici, (px, py, c)).wait_send()
                rcopy(dst.at[pq].at[mine], dst.at[pq].at[mine], s_fwd, sibling).wait_send()
    direct = []
    for (src, dst) in scatters:
        for k, (px, py) in enumerate(peers):
            direct.append((src.at[2 * px + py], dst.at[k], (px, py, c)))
    if sib is not None:
        src, dst, ranges = sib
        for (off, rows) in ranges:
            direct.append((src.at[pl.ds(off, rows)], dst.at[pl.ds(off, rows)], sibling))
    if phase == "start":
        for k, (a, b, dev) in enumerate(direct):
            rcopy(a, b, s + k, dev).start()
    elif phase == "finish":
        for k, (a, b, dev) in enumerate(direct):
            rcopy(a, b, s + k, dev).wait_recv()
        for k, (a, b, dev) in enumerate(direct):
            rcopy(a, b, s + k, dev).wait_send()


def _comm_shapes(gathers, scatters, sib):
    assert all(g.shape[0] % 32 == 0 for g in gathers)
    cin = list(gathers) + list(scatters) + ([sib[0], sib[1]] if sib else [])
    cout = [jax.ShapeDtypeStruct((NQ,) + g.shape, g.dtype) for g in gathers] + \
           [jax.ShapeDtypeStruct((3,) + s.shape[1:], s.dtype) for s in scatters] + \
           ([jax.ShapeDtypeStruct(sib[1].shape, sib[1].dtype)] if sib else [])
    n_rem = 6 * len(gathers) + 3 * len(scatters) + (len(sib[2]) if sib else 0)
    sems = [pltpu.SemaphoreType.DMA((max(n_rem, 1),)), pltpu.SemaphoreType.DMA((max(n_rem, 1),)),
            pltpu.SemaphoreType.DMA((max(len(gathers), 1),))]
    return cin, cout, sems


def _pcall(body, *, name, grid, in_specs, out_specs, out_shape, operands, scratch_shapes=(),
           gathers=(), scatters=(), sib=None):
    assert len(grid) == 1
    out_shape, out_specs = tuple(out_shape), tuple(out_specs)
    if not (gathers or scatters or sib):
        return pl.pallas_call(body, name=name, grid=grid, in_specs=list(in_specs), out_specs=out_specs,
                              out_shape=out_shape, scratch_shapes=list(scratch_shapes),
                              compiler_params=_cparams("arbitrary"))(*operands)
    cin, cout, sems = _comm_shapes(gathers, scatters, sib)
    n_in, n_cin, n_out, n_cout, n_scr = len(operands), len(cin), len(out_shape), len(cout), len(scratch_shapes)
    ng, ns = len(gathers), len(scatters)
    nsteps = grid[0]

    def wrapped(*refs):
        ins = refs[:n_in]
        cins = refs[n_in:n_in + n_cin]
        o0 = n_in + n_cin
        outs = refs[o0:o0 + n_out]
        couts = refs[o0 + n_out:o0 + n_out + n_cout]
        s0 = o0 + n_out + n_cout
        scr = refs[s0:s0 + n_scr]
        send, recv, loc = refs[s0 + n_scr:s0 + n_scr + 3]
        x, y, c = _my_pos()

        def run(phase):
            g = [(cins[k], couts[k]) for k in range(ng)]
            sc = [(cins[ng + k], couts[ng + k]) for k in range(ns)]
            sb = (cins[ng + ns], couts[ng + ns], sib[2]) if sib else None
            _comm_run(phase, x, y, c, g, sc, sb, send, recv, loc)

        @pl.when(pl.program_id(0) == 0)
        def _():
            run("start")

        if ng:
            @pl.when(pl.program_id(0) == max(nsteps - FORWARD_STEPS, 0))
            def _():
                run("forward")

        body(*ins, *outs, *scr)

        @pl.when(pl.program_id(0) == nsteps - 1)
        def _():
            run("finish")

    anyspec = pl.BlockSpec(memory_space=pl.ANY)
    aliases = {n_in + ng + ns + 1: n_out + ng + ns} if sib else {}
    return pl.pallas_call(
        wrapped, name=name, grid=grid,
        in_specs=list(in_specs) + [anyspec] * n_cin, out_specs=out_specs + (anyspec,) * n_cout,
        out_shape=out_shape + tuple(cout), scratch_shapes=list(scratch_shapes) + sems,
        input_output_aliases=aliases,
        compiler_params=pltpu.CompilerParams(dimension_semantics=("arbitrary",), vmem_limit_bytes=VMEM_LIMIT,
                                             has_side_effects=True),
    )(*operands, *cin)


def _comm_only(name, small=None, reduce_small=False, gathers=(), scatters=(), sib=None):
    cin, cout, sems = _comm_shapes(gathers, scatters, sib)
    n_cin, n_cout = len(cin), len(cout)
    ng, ns = len(gathers), len(scatters)
    n_sm_in = 1 if small is not None else 0
    n_sm_out = (2 if reduce_small else 1) if small is not None else 0

    def body(*refs):
        sm_in = refs[:n_sm_in]
        cins = refs[n_sm_in:n_sm_in + n_cin]
        o0 = n_sm_in + n_cin
        sm_out = refs[o0:o0 + n_sm_out]
        couts = refs[o0 + n_sm_out:o0 + n_sm_out + n_cout]
        s0 = o0 + n_sm_out + n_cout
        send, recv, loc = refs[s0:s0 + 3]
        x, y, c = _my_pos()
        g = [(cins[k], couts[k]) for k in range(ng)]
        sc = [(cins[ng + k], couts[ng + k]) for k in range(ns)]
        sb = (cins[ng + ns], couts[ng + ns], sib[2]) if sib else None
        _comm_run("start", x, y, c, g, sc, sb, send, recv, loc)
        if small is not None:
            sm_send, sm_recv = refs[s0 + 3:s0 + 5]
            small_ref, sg_ref = sm_in[0], sm_out[0]
            me = _dev_index(x, y, c)
            sg_ref[me] = small_ref[...]
            peers = _all_peers(x, y, c)
            sm = [pltpu.make_async_remote_copy(small_ref, sg_ref.at[me], sm_send.at[k], sm_recv.at[k],
                                               device_id=peer, device_id_type=MESH) for k, peer in enumerate(peers)]
            for cp in sm:
                cp.start()
            for k, (px, py, pc) in enumerate(peers):
                pltpu.make_async_remote_copy(small_ref, sg_ref.at[_dev_index(px, py, pc)], sm_send.at[k], sm_recv.at[k],
                                             device_id=(px, py, pc), device_id_type=MESH).wait_recv()
            if reduce_small:
                acc = sg_ref[0]
                for d in range(1, NDEV):
                    acc = acc + sg_ref[d]
                sm_out[1][...] = acc
            for cp in sm:
                cp.wait_send()
        if ng:
            _comm_run("forward", x, y, c, g, sc, sb, send, recv, loc)
        _comm_run("finish", x, y, c, g, sc, sb, send, recv, loc)

    anyspec = pl.BlockSpec(memory_space=pl.ANY)
    vspec = pl.BlockSpec(memory_space=pltpu.VMEM)
    sm_shapes = []
    if small is not None:
        sm_shapes.append(jax.ShapeDtypeStruct((NDEV,) + small.shape, small.dtype))
        if reduce_small:
            sm_shapes.append(jax.ShapeDtypeStruct(small.shape, small.dtype))
        sems = sems + [pltpu.SemaphoreType.DMA((NDEV - 1,)), pltpu.SemaphoreType.DMA((NDEV - 1,))]
    aliases = {n_sm_in + ng + ns + 1: n_sm_out + ng + ns} if sib else {}
    return pl.pallas_call(
        body, name=name,
        in_specs=[vspec] * n_sm_in + [anyspec] * n_cin,
        out_specs=tuple([vspec] * n_sm_out + [anyspec] * n_cout),
        out_shape=tuple(sm_shapes + cout), scratch_shapes=sems, input_output_aliases=aliases,
        compiler_params=pltpu.CompilerParams(has_side_effects=True),
    )(*([small] if small is not None else []), *cin)


def _exchange_mod(modpart):
    _, L, Cs = modpart.shape

    def body(part_ref, out_ref, send, recv):
        x, y, c = _my_pos()
        q = 2 * x + y
        me = _dev_index(x, y, c)
        out_ref[q] = part_ref[me]
        sends = []
        for k, (px, py) in enumerate(_chip_peers(x, y)):
            cp = pltpu.make_async_remote_copy(part_ref.at[_dev_index(px, py, c)], out_ref.at[q], send.at[k], recv.at[k],
                                              device_id=(px, py, c), device_id_type=MESH)
            cp.start()
            sends.append(cp)
        for k, (px, py) in enumerate(_chip_peers(x, y)):
            pltpu.make_async_remote_copy(part_ref.at[me], out_ref.at[2 * px + py], send.at[k], recv.at[k],
                                         device_id=(px, py, c), device_id_type=MESH).wait_recv()
        for cp in sends:
            cp.wait_send()

    return pl.pallas_call(
        body, name="exchange_mod",
        out_shape=jax.ShapeDtypeStruct((NQ, L, Cs), modpart.dtype),
        in_specs=[pl.BlockSpec(memory_space=pltpu.VMEM)],
        out_specs=pl.BlockSpec(memory_space=pltpu.VMEM),
        scratch_shapes=[pltpu.SemaphoreType.DMA((3,)), pltpu.SemaphoreType.DMA((3,))],
        compiler_params=pltpu.CompilerParams(has_side_effects=True),
    )(modpart)


def _mod_part(c_all, w_mod, b_mod_sh):
    L, D, Cs = w_mod.shape
    tn = 512 if Cs % 512 == 0 else Cs

    def body(c_ref, w_ref, b_ref, o_ref):
        cv = c_ref[...]
        cond = cv * jax.nn.sigmoid(cv)
        o_ref[...] = jnp.dot(cond, w_ref[...], preferred_element_type=F32, precision=lax.Precision.HIGHEST) + b_ref[...]

    return pl.pallas_call(
        body, name="mod_part", grid=(L, Cs // tn),
        out_shape=jax.ShapeDtypeStruct((L, NDEV, Cs), F32),
        in_specs=[pl.BlockSpec((NDEV, D), lambda i, j: (0, 0)),
                  pl.BlockSpec((None, D, tn), lambda i, j: (i, 0, j)),
                  pl.BlockSpec((None, 1, tn), lambda i, j: (i, 0, j))],
        out_specs=pl.BlockSpec((None, NDEV, tn), lambda i, j: (i, 0, j)),
        compiler_params=_cparams("parallel", "parallel"),
    )(c_all, w_mod, b_mod_sh)


def _adam(w, g, m, v):
    m2 = ADAM_B1 * m + (1.0 - ADAM_B1) * g
    v2 = ADAM_B2 * v + (1.0 - ADAM_B2) * (g * g)
    m_hat = m2 / (1.0 - ADAM_B1 ** ADAM_STEP)
    v_hat = v2 / (1.0 - ADAM_B2 ** ADAM_STEP)
    delta = -ADAM_LR * (m_hat / (jnp.sqrt(v_hat) + ADAM_EPS) + ADAM_WD * w)
    return delta, m2, v2


def _wmod_update(c_all_t, dmod_sh, w, m, v):
    L, D, Cs = w.shape
    td = 256 if D % 256 == 0 else D

    def body(ct_ref, d_ref, w_ref, m_ref, v_ref, g_ref, dl_ref, m2_ref, v2_ref):
        cv = ct_ref[...]
        cond = cv * jax.nn.sigmoid(cv)
        g = cond[:, 0:1] * d_ref[0:1, :]
        for b in range(1, NDEV):
            g = g + cond[:, b:b + 1] * d_ref[b:b + 1, :]
        g_ref[...] = g
        dl_ref[...], m2_ref[...], v2_ref[...] = _adam(w_ref[...], g, m_ref[...], v_ref[...])

    blk = pl.BlockSpec((None, td, Cs), lambda i, j: (i, j, 0))
    out = jax.ShapeDtypeStruct((L, D, Cs), F32)
    return pl.pallas_call(
        body, name="wmod_update", grid=(L, D // td),
        out_shape=(out, out, out, out),
        in_specs=[pl.BlockSpec((td, NDEV), lambda i, j: (j, 0)),
                  pl.BlockSpec((None, NDEV, Cs), lambda i, j: (i, 0, 0)), blk, blk, blk],
        out_specs=(blk, blk, blk, blk),
        compiler_params=_cparams("parallel", "parallel"),
    )(c_all_t, dmod_sh, w, m, v)


def _adam_rows(w, m, v, pa, pb, row_off, name):
    rows, C = w.shape
    tr = 512 if rows % 512 == 0 else (128 if rows % 128 == 0 else rows)
    assert row_off % tr == 0
    ob = row_off // tr

    def body(w_ref, m_ref, v_ref, pa_ref, pb_ref, g_ref, dl_ref, m2_ref, v2_ref):
        g = pa_ref[...] + pb_ref[...]
        g_ref[...] = g
        dl_ref[...], m2_ref[...], v2_ref[...] = _adam(w_ref[...], g, m_ref[...], v_ref[...])

    blk = pl.BlockSpec((tr, C), lambda i: (i, 0))
    pblk = pl.BlockSpec((tr, C), lambda i: (ob + i, 0))
    out = jax.ShapeDtypeStruct((rows, C), F32)
    return pl.pallas_call(
        body, name=name, grid=(rows // tr,), out_shape=(out, out, out, out),
        in_specs=[blk, blk, blk, pblk, pblk], out_specs=(blk, blk, blk, blk),
        compiler_params=_cparams("parallel"),
    )(w, m, v, pa, pb)


def _adam_small(sg_sum, by_rows, sliced, pairs):
    D = sg_sum.shape[1]
    na, nb, nc = len(by_rows), len(sliced), len(pairs)

    def body(*refs):
        sg = refs[0]
        ins_a = [refs[1 + 3 * t:4 + 3 * t] for t in range(na)]
        p = 1 + 3 * na
        ins_b = [refs[p + 4 * t:p + 4 * t + 4] for t in range(nb)]
        p += 4 * nb
        ins_c = [refs[p + 5 * t:p + 5 * t + 5] for t in range(nc)]
        p += 5 * nc
        outs_a = [refs[p + 4 * t:p + 4 * t + 4] for t in range(na)]
        p += 4 * na
        outs_b = [refs[p + 3 * t:p + 3 * t + 3] for t in range(nb)]
        p += 3 * nb
        outs_c = [refs[p + 4 * t:p + 4 * t + 4] for t in range(nc)]
        for (w_ref, m_ref, v_ref, ga_ref, gb_ref), (g_ref, dl_ref, m2_ref, v2_ref) in zip(ins_c, outs_c):
            g = ga_ref[...] + gb_ref[...]
            g_ref[...] = g
            dl_ref[...], m2_ref[...], v2_ref[...] = _adam(w_ref[...], g, m_ref[...], v_ref[...])
        for (w_ref, m_ref, v_ref), (g_ref, dl_ref, m2_ref, v2_ref), (w, _, _, row0) in zip(ins_a, outs_a, by_rows):
            n, k = w.shape[0], w.shape[1] // D
            pieces = [(slice(0, n), slice(0, D), slice(row0, row0 + n))] if k == 1 else \
                     [(slice(i, i + 1), slice(kk * D, (kk + 1) * D), slice(row0 + i * k + kk, row0 + i * k + kk + 1))
                      for i in range(n) for kk in range(k)]
            for rs, cs, gs in pieces:
                g = sg[gs, :]
                g_ref[rs, cs] = g
                dl_ref[rs, cs], m2_ref[rs, cs], v2_ref[rs, cs] = _adam(w_ref[rs, cs], g, m_ref[rs, cs], v_ref[rs, cs])
        for (w_ref, m_ref, v_ref, g_ref), (dl_ref, m2_ref, v2_ref) in zip(ins_b, outs_b):
            dl_ref[...], m2_ref[...], v2_ref[...] = _adam(w_ref[...], g_ref[...], m_ref[...], v_ref[...])

    operands = [sg_sum] + [a for t in by_rows for a in t[:3]] + [a for t in sliced for a in t] + \
               [a for t in pairs for a in t]
    out_shape = [jax.ShapeDtypeStruct(t[0].shape, F32) for t in by_rows for _ in range(4)] + \
                [jax.ShapeDtypeStruct(t[0].shape, F32) for t in sliced for _ in range(3)] + \
                [jax.ShapeDtypeStruct(t[0].shape, F32) for t in pairs for _ in range(4)]
    outs = pl.pallas_call(body, name="adam_small", out_shape=tuple(out_shape))(*operands)
    res_a = [tuple(outs[4 * t:4 * t + 4]) for t in range(na)]
    o = 4 * na
    res_b = [tuple(outs[o + 3 * t:o + 3 * t + 3]) for t in range(nb)]
    o += 3 * nb
    res_c = [tuple(outs[o + 4 * t:o + 4 * t + 4]) for t in range(nc)]
    return res_a, res_b, res_c


def _sum_into(ppack, dw, rb, off, qv):
    _, rows, D = dw.shape
    tr = 256 if rows % 256 == 0 else 128
    assert rows % tr == 0 and off % tr == 0
    ob = off // tr

    def body(q_ref, o_ref, r_ref, pin_ref, p_ref):
        acc = o_ref[...].astype(F32)
        for k in range(3):
            acc = acc + r_ref[k].astype(F32)
        p_ref[...] = acc

    return pl.pallas_call(
        body, name="sum_partials", out_shape=jax.ShapeDtypeStruct(ppack.shape, ppack.dtype),
        grid_spec=pltpu.PrefetchScalarGridSpec(
            num_scalar_prefetch=1, grid=(rows // tr,),
            in_specs=[pl.BlockSpec((None, tr, D), lambda i, q_ref: (q_ref[0], i, 0)),
                      pl.BlockSpec((3, tr, D), lambda i, q_ref: (0, i, 0)),
                      pl.BlockSpec(memory_space=pl.ANY)],
            out_specs=pl.BlockSpec((tr, D), lambda i, q_ref: (ob + i, 0))),
        input_output_aliases={3: 0},
        compiler_params=_cparams("parallel"),
    )(qv, dw, rb, ppack)


def _wspec(g):
    return _resident(g.shape, lambda i: (0, 0, 0))


def _ffn_fwd_inner(x1, mod_ref, gf_ref, w1_ref, w2_ref, h2_ref, a_ref, z_ref, x2_ref):
    h2 = _rms_fwd(x1, gf_ref[...], mod_ref[4:5, :], mod_ref[3:4, :])[0]
    h2b = h2.astype(BF16)
    h2_ref[...] = h2b
    f4 = w1_ref.shape[2]
    z = jnp.zeros(x1.shape, F32)
    for q in range(NQ):
        a = jnp.maximum(_dot(h2b, w1_ref[q]), 0.0)
        a_ref[:, q * f4:(q + 1) * f4] = a.astype(BF16)
        z = z + _dot((a * a).astype(BF16), w2_ref[q])
    z_ref[...] = z.astype(BF16)
    x2_ref[...] = x1 + mod_ref[5:6, :] * z


def _sigmoid(x):
    return 0.5 + 0.5 * jnp.tanh(0.5 * x)


def _heads_dot(xb, w_ref, hd, nt=False):
    outs = []
    for h in range(HEADS):
        xs = xb[:, h * hd:(h + 1) * hd]
        outs.append(_dot_nt(xs, w_ref[h]) if nt else _dot(xs, w_ref[h]))
    return jnp.concatenate(outs, axis=1)


def _lru_gates(xc, wa_ref, ba, wx_ref, bx, lam, hd):
    xcb = xc.astype(BF16)
    gate_r = _sigmoid(_heads_dot(xcb, wa_ref, hd) + ba)
    gate_i = _sigmoid(_heads_dot(xcb, wx_ref, hd) + bx)
    ls = jax.nn.log_sigmoid(lam)
    log_a = gate_r * (LRU_C * ls)
    a = jnp.exp(log_a)
    mult = jnp.sqrt(_neg_expm1(2.0 * log_a, a * a))
    return xcb, gate_r, gate_i, ls, a, mult


def _conv_taps(xext, cw, tt):
    acc = cw[0:1, :] * xext[pl.ds(8 - (CONV_W - 1), tt), :]
    for k in range(1, CONV_W):
        acc = acc + cw[k:k + 1, :] * xext[pl.ds(8 - (CONV_W - 1) + k, tt), :]
    return acc


def _lru_fwd(x, mod_l, g_mix, g_wy, g_win, b_y, b_in, cw, cb, wa, ba, wx, bx, lam, g_wout, b_out, **comm):
    S, W = x.shape
    tt = min(TT, S)
    hd = W // HEADS

    def body(x_ref, mod_ref, g_ref, wy_ref, win_ref, by_ref, bin_ref, cw_ref, cb_ref, wa_ref, ba_ref, wx_ref, bx_ref,
             lam_ref, wo_ref, bo_ref, h_ref, gb_ref, xr_ref, hs_ref, p_ref, y_ref, x1_ref,
             xc_ref, gr_ref, gi_ref, a_s, mu_ref, xext, u_s, carry):
        i = pl.program_id(0)

        @pl.when(i == 0)
        def _():
            carry[...] = jnp.zeros_like(carry)
            xext[0:8, :] = jnp.zeros((8, W), F32)

        @pl.when(i > 0)
        def _():
            xext[0:8, :] = xext[pl.ds(tt, 8), :]

        xv = x_ref[...]
        hb = _rms_fwd(xv, g_ref[...], mod_ref[1:2, :], mod_ref[0:1, :])[0].astype(BF16)
        h_ref[...] = hb
        gbv = _dot(hb, wy_ref[...].reshape(W, W)) + by_ref[...]
        gb_ref[...] = gbv
        xr = _dot(hb, win_ref[...].reshape(W, W)) + bin_ref[...]
        xr_ref[...] = xr
        xext[pl.ds(8, tt), :] = xr
        xc = _conv_taps(xext, cw_ref[...], tt) + cb_ref[...]
        _, gate_r, gate_i, _, a, mult = _lru_gates(xc, wa_ref, ba_ref[...], wx_ref, bx_ref[...], lam_ref[...], hd)
        xc_ref[...] = xc
        gr_ref[...] = gate_r
        gi_ref[...] = gate_i
        mu_ref[...] = mult
        a_s[...] = a
        u_s[...] = mult * (gate_i * xc)
        row = lax.broadcasted_iota(jnp.int32, (8, W), 0)

        def step(k, _):
            off = pl.multiple_of(k * 8, 8)
            A = a_s[pl.ds(off, 8), :]
            U = u_s[pl.ds(off, 8), :]
            for d in (1, 2, 4):
                keep = row >= d
                Us = jnp.where(keep, pltpu.roll(U, d, 0), 0.0)
                As = jnp.where(keep, pltpu.roll(A, d, 0), 1.0)
                U = U + A * Us
                A = A * As
            H = U + A * carry[...]
            hs_ref[pl.ds(off, 8), :] = H
            carry[...] = jnp.broadcast_to(H[7:8, :], (8, W))
            return 0

        lax.fori_loop(0, tt // 8, step, 0)
        pb = (hs_ref[...] * _gelu(gbv)[0]).astype(BF16)
        p_ref[...] = pb
        y = _dot(pb, wo_ref[...].reshape(W, W)) + bo_ref[...]
        y_ref[...] = y.astype(BF16)
        x1_ref[...] = xv + mod_ref[2:3, :] * y

    tile = pl.BlockSpec((tt, W), lambda i: (i, 0))
    row = pl.BlockSpec((1, W), lambda i: (0, 0))
    wblk = pl.BlockSpec((HEADS, hd, hd), lambda i: (0, 0, 0))
    f32o, bf16o = jax.ShapeDtypeStruct((S, W), F32), jax.ShapeDtypeStruct((S, W), BF16)
    return _pcall(
        body, name="lru_fwd", grid=(S // tt,),
        out_shape=(bf16o, f32o, f32o, f32o, bf16o, bf16o, f32o, f32o, f32o, f32o, f32o, f32o),
        in_specs=[tile, pl.BlockSpec((8, W), lambda i: (0, 0)), row, _wspec(g_wy), _wspec(g_win), row, row,
                  pl.BlockSpec((CONV_W, W), lambda i: (0, 0)), row, wblk, row, wblk, row, row, _wspec(g_wout), row],
        out_specs=(tile,) * 12,
        scratch_shapes=[pltpu.VMEM((tt + 8, W), F32), pltpu.VMEM((tt, W), F32), pltpu.VMEM((8, W), F32)],
        operands=(x, mod_l, g_mix, g_wy, g_win, b_y, b_in, cw, cb, wa, ba, wx, bx, lam, g_wout, b_out), **comm)


def _ffn_out_shapes(S, D, F):
    return (jax.ShapeDtypeStruct((S, D), BF16), jax.ShapeDtypeStruct((S, F), BF16),
            jax.ShapeDtypeStruct((S, D), BF16), jax.ShapeDtypeStruct((S, D), F32))


def _ffn_fwd(x1, mod_l, g_ffn, g_w1, g_w2, **comm):
    S, D = x1.shape
    tm = min(TM, S)
    F = g_w1.shape[2] * NQ

    def body(x1_ref, mod_ref, gf_ref, w1_ref, w2_ref, h2_ref, a_ref, z_ref, x2_ref):
        _ffn_fwd_inner(x1_ref[...], mod_ref, gf_ref, w1_ref, w2_ref, h2_ref, a_ref, z_ref, x2_ref)

    tile = pl.BlockSpec((tm, D), lambda i: (i, 0))
    row = pl.BlockSpec((1, D), lambda i: (0, 0))
    return _pcall(
        body, name="ffn_fwd", grid=(S // tm,),
        out_shape=_ffn_out_shapes(S, D, F),
        in_specs=[tile, pl.BlockSpec((8, D), lambda i: (0, 0)), row, _wspec(g_w1), _wspec(g_w2)],
        out_specs=(tile, pl.BlockSpec((tm, F), lambda i: (i, 0)), tile, tile),
        operands=(x1, mod_l, g_ffn, g_w1, g_w2), **comm)


def _window_vec(D):
    gd = D // len(POOL_WINDOWS)
    lane = lax.broadcasted_iota(jnp.int32, (1, D), 1)
    w = jnp.full((1, D), float(POOL_WINDOWS[0]), F32)
    for g in range(1, len(POOL_WINDOWS)):
        w = jnp.where(lane >= g * gd, float(POOL_WINDOWS[g]), w)
    return w


def _pool_mix_ffn_fwd(x, mod_l, g_mix, pw, ps, g_ffn, g_w1, g_w2, **comm):
    S, D = x.shape
    tm = min(TPF, S)
    F = g_w1.shape[2] * NQ
    gd = D // len(POOL_WINDOWS)
    n = tm + 24

    def body(x_ref, xh_ref, mod_ref, gm_ref, pw_ref, ps_ref, gf_ref, w1_ref, w2_ref,
             pl_ref, x1_ref, h2_ref, a_ref, z_ref, x2_ref, ext, b1, b2):
        i = pl.program_id(0)
        g, sc, sh = gm_ref[...], mod_ref[1:2, :], mod_ref[0:1, :]
        xv = x_ref[...]
        h = _rms_fwd(xv, g, sc, sh)[0]
        hh = _rms_fwd(xh_ref[...], g, sc, sh)[0]
        zeros8 = jnp.zeros((8, D), F32)
        ext[0:8, :] = zeros8
        b1[0:8, :] = zeros8
        b2[0:8, :] = zeros8
        ext[8:24, :] = jnp.where(i > 0, hh, 0.0)
        ext[pl.ds(24, tm), :] = h
        m = n - 8
        b1[pl.ds(8, m), :] = ext[pl.ds(8, m), :] + ext[pl.ds(7, m), :]
        b2[pl.ds(8, m), gd:] = b1[pl.ds(8, m), gd:] + b1[pl.ds(6, m), gd:]
        b1[pl.ds(8, m), 2 * gd:] = b2[pl.ds(8, m), 2 * gd:] + b2[pl.ds(4, m), 2 * gd:]
        b2[pl.ds(8, m), 3 * gd:] = b1[pl.ds(8, m), 3 * gd:] + b1[pl.ds(0, m), 3 * gd:]
        wsum = jnp.concatenate([b1[pl.ds(24, tm), 0:gd], b2[pl.ds(24, tm), gd:2 * gd],
                                b1[pl.ds(24, tm), 2 * gd:3 * gd], b2[pl.ds(24, tm), 3 * gd:]], axis=1)
        t1 = (lax.broadcasted_iota(jnp.int32, (tm, 1), 0) + (i * tm + 1)).astype(F32)
        cnt = jnp.minimum(t1, _window_vec(D))
        pooled = (wsum / cnt - h).astype(BF16)
        pl_ref[...] = pooled
        y = _heads_dot(pooled, pw_ref, gd) * ps_ref[...]
        x1 = xv + mod_ref[2:3, :] * y
        x1_ref[...] = x1
        _ffn_fwd_inner(x1, mod_ref, gf_ref, w1_ref, w2_ref, h2_ref, a_ref, z_ref, x2_ref)

    tile = pl.BlockSpec((tm, D), lambda i: (i, 0))
    halo = pl.BlockSpec((16, D), lambda i: (jnp.maximum(i * (tm // 16) - 1, 0), 0))
    row = pl.BlockSpec((1, D), lambda i: (0, 0))
    return _pcall(
        body, name="pool_mix_ffn_fwd", grid=(S // tm,),
        out_shape=(jax.ShapeDtypeStruct((S, D), BF16), jax.ShapeDtypeStruct((S, D), F32)) + _ffn_out_shapes(S, D, F),
        in_specs=[tile, halo, pl.BlockSpec((8, D), lambda i: (0, 0)), row,
                  pl.BlockSpec((len(POOL_WINDOWS), gd, gd), lambda i: (0, 0, 0)), row, row,
                  _wspec(g_w1), _wspec(g_w2)],
        out_specs=(tile, tile, tile, pl.BlockSpec((tm, F), lambda i: (i, 0)), tile, tile),
        scratch_shapes=[pltpu.VMEM((n, D), F32), pltpu.VMEM((n, D), F32), pltpu.VMEM((n, D), F32)],
        operands=(x, x, mod_l, g_mix, pw, ps, g_ffn, g_w1, g_w2), **comm)


def _loss_head(xv, gv, tv, acc_ref):
    D = xv.shape[1]
    r = lax.rsqrt(jnp.mean(xv * xv, axis=-1, keepdims=True) + EPS)
    xhat = xv * r
    err = xhat * gv - tv
    acc_ref[0:1, :] += jnp.sum(err * err, axis=0, keepdims=True)
    dy = err * (1.0 / D)
    acc_ref[1:2, :] += jnp.sum(dy * xhat, axis=0, keepdims=True)
    dxh = dy * gv
    return r * (dxh - xhat * jnp.mean(dxh * xhat, axis=-1, keepdims=True))


def _ffn_bwd(dx2, x1, a, z, mod_l, g_ffn, g_w1, g_w2, head=None, **comm):
    S, D = dx2.shape
    F = a.shape[1]
    f4 = F // NQ
    tm = min(TM, S)
    nh = 2 if head else 0

    def body(*refs):
        dx2_ref, x1_ref, a_ref, z_ref, mod_ref, gf_ref, w1_ref, w2_ref = refs[:8]
        dx1_ref, du_ref, dz_ref, acc_ref = refs[8 + nh:]
        _zero_first(acc_ref)
        dx2v = dx2_ref[...]
        if head:
            dx2v = _loss_head(dx2v, refs[8][...], refs[9][...], acc_ref)
        acc_ref[5:6, :] +=jnp.sum(dx2v * z_ref[...].astype(F32), axis=0, keepdims=True)
        dzb = (dx2v * mod_ref[5:6, :]).astype(BF16)
        dz_ref[...] = dzb
        dh2 = jnp.zeros((tm, D), F32)
        for q in range(NQ):
            av = a_ref[:, q * f4:(q + 1) * f4].astype(F32)
            du = (_dot_nt(dzb, w2_ref[q]) * (2.0 * av)).astype(BF16)
            du_ref[:, q * f4:(q + 1) * f4] = du
            dh2 = dh2 + _dot_nt(du, w1_ref[q])
        g, sc = gf_ref[...], mod_ref[4:5, :]
        _, xhat, r, n = _rms_fwd(x1_ref[...], g, sc, mod_ref[3:4, :])
        dx, dsh, dsc, dg = _rms_bwd(dh2, xhat, r, n, g, sc)
        acc_ref[3:4, :] += dsh
        acc_ref[4:5, :] += dsc
        acc_ref[7:8, :] += dg
        dx1_ref[...] = dx2v + dx

    tile = pl.BlockSpec((tm, D), lambda i: (i, 0))
    wide = pl.BlockSpec((tm, F), lambda i: (i, 0))
    return _pcall(
        body, name="ffn_bwd", grid=(S // tm,),
        out_shape=(jax.ShapeDtypeStruct((S, D), F32), jax.ShapeDtypeStruct((S, F), BF16),
                   jax.ShapeDtypeStruct((S, D), BF16), jax.ShapeDtypeStruct((8, D), F32)),
        in_specs=[tile, tile, wide, tile, pl.BlockSpec((8, D), lambda i: (0, 0)), pl.BlockSpec((1, D), lambda i: (0, 0)),
                  _wspec(g_w1), _wspec(g_w2)] + ([pl.BlockSpec((1, D), lambda i: (0, 0)), tile] if head else []),
        out_specs=(tile, wide, tile, pl.BlockSpec((8, D), lambda i: (0, 0))),
        operands=(dx2, x1, a, z, mod_l, g_ffn, g_w1, g_w2) + (tuple(head) if head else ()), **comm)


def _dw_blocked(a, b, by_rows, square_a, name, **comm):
    S = a.shape[0]
    tk = min(TK, S)
    nk = S // tk
    if by_rows:
        bm, bn = a.shape[1] // NQ, b.shape[1]
        a_map, b_map = (lambda s: (s % nk, s // nk)), (lambda s: (s % nk, 0))
    else:
        bm, bn = a.shape[1], b.shape[1] // NQ
        a_map, b_map = (lambda s: (s % nk, 0)), (lambda s: (s % nk, s // nk))

    def body(a_ref, b_ref, o_ref, acc):
        k = pl.program_id(0) % nk

        @pl.when(k == 0)
        def _():
            acc[...] = jnp.zeros_like(acc)

        av = a_ref[...]
        if square_a:
            av = av * av
        acc[...] += _dot_tn(av, b_ref[...])

        @pl.when(k == nk - 1)
        def _():
            o_ref[...] = acc[...].astype(o_ref.dtype)

    return _pcall(
        body, name=name, grid=(NQ * nk,),
        out_shape=(jax.ShapeDtypeStruct((NQ, bm, bn), BF16),),
        in_specs=[pl.BlockSpec((tk, bm), a_map), pl.BlockSpec((tk, bn), b_map)],
        out_specs=(pl.BlockSpec((None, bm, bn), lambda s: (s // nk, 0, 0)),),
        scratch_shapes=[pltpu.VMEM((bm, bn), F32)],
        operands=(a, b), **comm)


def _dw_whole(a, bs, name, **comm):
    S, M = a.shape
    N = bs[0].shape[1]
    tk = min(TK, S)
    nk = S // tk
    nb = len(bs)

    def body(*refs):
        a_ref, b_refs, o_refs, accs = refs[0], refs[1:1 + nb], refs[1 + nb:1 + 2 * nb], refs[1 + 2 * nb:]
        k = pl.program_id(0)

        @pl.when(k == 0)
        def _():
            for acc in accs:
                acc[...] = jnp.zeros_like(acc)

        av = a_ref[...]
        for b_ref, acc in zip(b_refs, accs):
            acc[...] += _dot_tn(av, b_ref[...])

        @pl.when(k == nk - 1)
        def _():
            for o_ref, acc in zip(o_refs, accs):
                o_ref[...] = acc[...].reshape(NQ, M // NQ, N).astype(o_ref.dtype)

    return _pcall(
        body, name=name, grid=(nk,),
        out_shape=tuple(jax.ShapeDtypeStruct((NQ, M // NQ, N), BF16) for _ in bs),
        in_specs=[pl.BlockSpec((tk, M), lambda k: (k, 0))] + [pl.BlockSpec((tk, N), lambda k: (k, 0)) for _ in bs],
        out_specs=tuple(pl.BlockSpec((NQ, M // NQ, N), lambda k: (0, 0, 0)) for _ in bs),
        scratch_shapes=[pltpu.VMEM((M, N), F32) for _ in bs],
        operands=(a, *bs), **comm)


def _lru_bwd(dx1, y, x, xr0, gb, hs, xc_, gate_r_, gate_i_, a_, mult_, mod_l, g_mix, g_wout, g_wy, g_win, cw, wa, wx,
             lam, **comm):
    S, W = xr0.shape
    tt = min(TT, S)
    nb = S // tt
    hd = W // HEADS

    def body(dx1_ref, y_ref, x_ref, xr_ref, gb_ref, hs_ref, hsh_ref, xc_ref, gr_ref, gi_ref, a_s, mu_ref,
             mod_ref, gm_ref, wo_ref, wy_ref, win_ref, cw_ref, wa_ref, wx_ref, lam_ref,
             dy_ref, dgb_ref, dxr_ref, dx_ref, sm_ref, dwa_ref, dwx_ref, acc_ref,
             hext, qext, dext, b_s, qc, dc):
        i = pl.program_id(0)
        blk = nb - 1 - i

        @pl.when(i == 0)
        def _():
            sm_ref[...] = jnp.zeros_like(sm_ref)
            dwa_ref[...] = jnp.zeros_like(dwa_ref)
            dwx_ref[...] = jnp.zeros_like(dwx_ref)
            acc_ref[...] = jnp.zeros_like(acc_ref)
            qc[...] = jnp.zeros_like(qc)
            dc[...] = jnp.zeros_like(dc)

        dx1v = dx1_ref[...]
        acc_ref[2:3, :] += jnp.sum(dx1v * y_ref[...].astype(F32), axis=0, keepdims=True)
        dy = dx1v * mod_ref[2:3, :]
        acc_ref[3:4, :] += jnp.sum(dy, axis=0, keepdims=True)
        dyb = dy.astype(BF16)
        dy_ref[...] = dyb
        dpv = _dot_nt(dyb, wo_ref[...].reshape(W, W))

        hext[0:8, :] = jnp.where(blk > 0, hsh_ref[...], 0.0)
        hext[pl.ds(8, tt), :] = hs_ref[...]
        cw = cw_ref[...]
        lam = lam_ref[...]
        xc, gate_r, gate_i, a, mult = xc_ref[...], gr_ref[...], gi_ref[...], a_s[...], mu_ref[...]
        xcb = xc.astype(BF16)
        ls = jax.nn.log_sigmoid(lam)

        gbv = gb_ref[...]
        gate, th = _gelu(gbv)
        dgb = dpv * hs_ref[...] * _gelu_grad(gbv, th)
        dgbb = dgb.astype(BF16)
        dgb_ref[...] = dgbb
        sm_ref[9:10, :] += jnp.sum(dgb, axis=0, keepdims=True)
        dhs = dpv * gate

        b_s[...] = a * dhs
        qext[pl.ds(tt, 8), :] = qc[...]
        row = lax.broadcasted_iota(jnp.int32, (8, W), 0)

        def step(k, _):
            off = pl.multiple_of((tt // 8 - 1 - k) * 8, 8)
            A = a_s[pl.ds(off, 8), :]
            B = b_s[pl.ds(off, 8), :]
            for d in (1, 2, 4):
                keep = row < 8 - d
                Bs = jnp.where(keep, pltpu.roll(B, 8 - d, 0), 0.0)
                As = jnp.where(keep, pltpu.roll(A, 8 - d, 0), 1.0)
                B = B + A * Bs
                A = A * As
            Q = B + A * qc[...]
            qext[pl.ds(off, 8), :] = Q
            qc[...] = jnp.broadcast_to(Q[0:1, :], (8, W))
            return 0

        lax.fori_loop(0, tt // 8, step, 0)
        gsc = dhs + qext[pl.ds(1, tt), :]
        da = gsc * hext[pl.ds(7, tt), :]
        t1 = gsc * xc
        dmult = t1 * gate_i
        dgate_i = t1 * mult
        dxc = gsc * (mult * gate_i)
        dlog_a = da * a - dmult * (a * a) / mult
        dgate_r = dlog_a * (LRU_C * ls)
        sm_ref[7:8, :] += jnp.sum(dlog_a * (LRU_C * gate_r), axis=0, keepdims=True)
        dga = dgate_r * gate_r * (1.0 - gate_r)
        dgx = dgate_i * gate_i * (1.0 - gate_i)
        sm_ref[5:6, :] += jnp.sum(dga, axis=0, keepdims=True)
        sm_ref[6:7, :] += jnp.sum(dgx, axis=0, keepdims=True)
        dgab = dga.astype(BF16)
        dgxb = dgx.astype(BF16)
        dxc = dxc + _heads_dot(dgab, wa_ref, hd, nt=True) + _heads_dot(dgxb, wx_ref, hd, nt=True)
        for h in range(HEADS):
            sl = slice(h * hd, (h + 1) * hd)
            dwa_ref[h] += _dot_tn(xcb[:, sl], dgab[:, sl])
            dwx_ref[h] += _dot_tn(xcb[:, sl], dgxb[:, sl])
        sm_ref[4:5, :] += jnp.sum(dxc, axis=0, keepdims=True)
        dext[pl.ds(0, tt), :] = dxc
        dext[pl.ds(tt, 8), :] = dc[...]
        xrv = xr_ref[...]
        dxr = None
        for k in range(CONV_W):
            up = dext[pl.ds(CONV_W - 1 - k, tt), :]
            sm_ref[k:k + 1, :] += jnp.sum(up * xrv, axis=0, keepdims=True)
            dxr = cw[k:k + 1, :] * up if dxr is None else dxr + cw[k:k + 1, :] * up
        dc[...] = dext[0:8, :]
        sm_ref[8:9, :] += jnp.sum(dxr, axis=0, keepdims=True)
        dxrb = dxr.astype(BF16)
        dxr_ref[...] = dxrb

        dh = _dot_nt(dxrb, win_ref[...].reshape(W, W)) + _dot_nt(dgbb, wy_ref[...].reshape(W, W))
        g, sc = gm_ref[...], mod_ref[1:2, :]
        _, xhat, r, n = _rms_fwd(x_ref[...], g, sc, mod_ref[0:1, :])
        dx, dsh, dsc, dg = _rms_bwd(dh, xhat, r, n, g, sc)
        acc_ref[0:1, :] += dsh
        acc_ref[1:2, :] += dsc
        acc_ref[6:7, :] += dg
        dx_ref[...] = dx1v + dx

        @pl.when(i == nb - 1)
        def _():
            sm_ref[7:8, :] = sm_ref[7:8, :] * jax.nn.sigmoid(-lam)

    rev = lambda i: (nb - 1 - i, 0)
    tile = pl.BlockSpec((tt, W), rev)
    halo = pl.BlockSpec((8, W), lambda i: (jnp.maximum((nb - 1 - i) * (tt // 8) - 1, 0), 0))
    row = pl.BlockSpec((1, W), lambda i: (0, 0))
    wblk = pl.BlockSpec((HEADS, hd, hd), lambda i: (0, 0, 0))
    bf16o = jax.ShapeDtypeStruct((S, W), BF16)
    return _pcall(
        body, name="lru_bwd", grid=(nb,),
        out_shape=(bf16o, bf16o, bf16o, jax.ShapeDtypeStruct((S, W), F32),
                   jax.ShapeDtypeStruct((16, W), F32), jax.ShapeDtypeStruct((HEADS, hd, hd), F32),
                   jax.ShapeDtypeStruct((HEADS, hd, hd), F32), jax.ShapeDtypeStruct((8, W), F32)),
        in_specs=[tile, tile, tile, tile, tile, tile, halo, tile, tile, tile, tile, tile,
                  pl.BlockSpec((8, W), lambda i: (0, 0)), row,
                  _wspec(g_wout), _wspec(g_wy), _wspec(g_win), pl.BlockSpec((CONV_W, W), lambda i: (0, 0)),
                  wblk, wblk, row],
        out_specs=(tile, tile, tile, tile, pl.BlockSpec((16, W), lambda i: (0, 0)), wblk, wblk,
                   pl.BlockSpec((8, W), lambda i: (0, 0))),
        scratch_shapes=[pltpu.VMEM((tt + 8, W), F32), pltpu.VMEM((tt + 8, W), F32), pltpu.VMEM((tt + 8, W), F32),
                        pltpu.VMEM((tt, W), F32), pltpu.VMEM((8, W), F32), pltpu.VMEM((8, W), F32)],
        operands=(dx1, y, x, xr0, gb, hs, hs, xc_, gate_r_, gate_i_, a_, mult_, mod_l, g_mix, g_wout, g_wy, g_win,
                  cw, wa, wx, lam),
        **comm)


def _pool_bwd(dx1, x, pooled, mod_l, g_mix, pw, ps, h2, du, a, dz):
    S, D = x.shape
    tm = min(TP, S)
    nb = S // tm
    ng = len(POOL_WINDOWS)
    gd = D // ng
    n = tm + 24
    f4 = du.shape[1] // NQ
    assert nb % NQ == 0
    kch = nb // NQ
    kr = S // kch

    def body(dx1_ref, dxh_ref, x_ref, pl_ref, mod_ref, gm_ref, pw_ref, ps_ref, h2_ref, du_ref, a_ref, dz_ref,
             dx_ref, acc_ref, dpw_ref, dw1_ref, dw2_ref, ext, b1, b2, acc1, acc2):
        i = pl.program_id(0)

        @pl.when(i == 0)
        def _():
            acc_ref[...] = jnp.zeros_like(acc_ref)
            dpw_ref[...] = jnp.zeros_like(dpw_ref)

        @pl.when(i % kch == 0)
        def _():
            acc1[...] = jnp.zeros_like(acc1)
            acc2[...] = jnp.zeros_like(acc2)

        acc1[...] += _dot_tn(h2_ref[...], du_ref[...])
        av = a_ref[...]
        acc2[...] += _dot_tn(av * av, dz_ref[...])

        gt, psv = mod_ref[2:3, :], ps_ref[...]
        wvec = _window_vec(D)
        dx1v = dx1_ref[...]
        pooled = pl_ref[...]
        mixed = _heads_dot(pooled, pw_ref, gd)
        acc_ref[2:3, :] += jnp.sum(dx1v * (mixed * psv), axis=0, keepdims=True)
        dy = dx1v * gt
        acc_ref[3:4, :] += jnp.sum(dy * mixed, axis=0, keepdims=True)
        dmix = (dy * psv).astype(BF16)
        for gi in range(ng):
            sl = slice(gi * gd, (gi + 1) * gd)
            dpw_ref[gi] += _dot_tn(pooled[:, sl], dmix[:, sl])
        dpooled = _heads_dot(dmix, pw_ref, gd, nt=True)
        dmix_h = (dxh_ref[...] * gt * psv).astype(BF16)
        dpooled_h = jnp.where(i < nb - 1, _heads_dot(dmix_h, pw_ref, gd, nt=True), 0.0)
        t1 = (lax.broadcasted_iota(jnp.int32, (tm, 1), 0) + (i * tm + 1)).astype(F32)
        t1h = (lax.broadcasted_iota(jnp.int32, (16, 1), 0) + ((i + 1) * tm + 1)).astype(F32)
        zeros8 = jnp.zeros((8, D), F32)
        ext[pl.ds(0, tm), :] = dpooled / jnp.minimum(t1, wvec)
        ext[pl.ds(tm, 16), :] = dpooled_h / jnp.minimum(t1h, wvec)
        ext[pl.ds(tm + 16, 8), :] = zeros8
        b1[pl.ds(tm + 16, 8), :] = zeros8
        b2[pl.ds(tm + 16, 8), :] = zeros8
        m = n - 8
        b1[pl.ds(0, m), :] = ext[pl.ds(0, m), :] + ext[pl.ds(1, m), :]
        b2[pl.ds(0, m), gd:] = b1[pl.ds(0, m), gd:] + b1[pl.ds(2, m), gd:]
        b1[pl.ds(0, m), 2 * gd:] = b2[pl.ds(0, m), 2 * gd:] + b2[pl.ds(4, m), 2 * gd:]
        b2[pl.ds(0, m), 3 * gd:] = b1[pl.ds(0, m), 3 * gd:] + b1[pl.ds(8, m), 3 * gd:]
        wsum = jnp.concatenate([b1[pl.ds(0, tm), 0:gd], b2[pl.ds(0, tm), gd:2 * gd],
                                b1[pl.ds(0, tm), 2 * gd:3 * gd], b2[pl.ds(0, tm), 3 * gd:]], axis=1)
        dh = wsum - dpooled
        g, sc = gm_ref[...], mod_ref[1:2, :]
        _, xhat, r, nn = _rms_fwd(x_ref[...], g, sc, mod_ref[0:1, :])
        dx, dsh, dsc, dg = _rms_bwd(dh, xhat, r, nn, g, sc)
        acc_ref[0:1, :] += dsh
        acc_ref[1:2, :] += dsc
        acc_ref[6:7, :] += dg
        dx_ref[...] = dx1v + dx

        @pl.when(i % kch == kch - 1)
        def _():
            dw1_ref[...] = acc1[...].astype(BF16)
            dw2_ref[...] = acc2[...].astype(BF16)

    tile = pl.BlockSpec((tm, D), lambda i: (i, 0))
    halo = pl.BlockSpec((16, D), lambda i: (jnp.minimum((i + 1) * (tm // 16), S // 16 - 1), 0))
    row = pl.BlockSpec((1, D), lambda i: (0, 0))
    wblk = pl.BlockSpec((ng, gd, gd), lambda i: (0, 0, 0))
    full_k = pl.BlockSpec((kr, D), lambda i: (i % kch, 0))
    part_k = pl.BlockSpec((kr, f4), lambda i: (i % kch, i // kch))
    return pl.pallas_call(
        body, name="pool_bwd", grid=(nb,),
        out_shape=(jax.ShapeDtypeStruct((S, D), F32), jax.ShapeDtypeStruct((8, D), F32),
                   jax.ShapeDtypeStruct((ng, gd, gd), F32), jax.ShapeDtypeStruct((NQ, D, f4), BF16),
                   jax.ShapeDtypeStruct((NQ, f4, D), BF16)),
        in_specs=[tile, halo, tile, tile, pl.BlockSpec((8, D), lambda i: (0, 0)), row, wblk, row,
                  full_k, part_k, part_k, full_k],
        out_specs=(tile, pl.BlockSpec((8, D), lambda i: (0, 0)), wblk,
                   pl.BlockSpec((None, D, f4), lambda i: (i // kch, 0, 0)),
                   pl.BlockSpec((None, f4, D), lambda i: (i // kch, 0, 0))),
        scratch_shapes=[pltpu.VMEM((n, D), F32), pltpu.VMEM((n, D), F32), pltpu.VMEM((n, D), F32),
                        pltpu.VMEM((D, f4), F32), pltpu.VMEM((f4, D), F32)],
        compiler_params=_cparams("arbitrary"),
    )(dx1, dx1, x, pooled, mod_l, g_mix, pw, ps, h2, du, a, dz)


def _shard_to_rows(w, D):
    return w.reshape(-1, D)


def _blockdiag_full(gq, na, hd):
    return gq.reshape(NQ, na, HEADS, hd // NQ, hd).transpose(1, 2, 0, 3, 4).reshape(na, HEADS, hd, hd)


def _blockdiag_by_chip(dw, D):
    na, _, hd, _ = dw.shape
    return dw.reshape(na, HEADS, NQ, hd // NQ, hd).transpose(2, 0, 1, 3, 4).reshape(NQ, -1, D)


def kernel(x, c, w_mod, b_mod, norm_mix_g, norm_ffn_g, lru_w_y, lru_b_y, lru_w_in, lru_b_in, lru_conv_w, lru_conv_b, lru_w_a, lru_b_a, lru_w_x, lru_b_x, lru_lambda, lru_w_out, lru_b_out, pool_w, pool_scale, ffn_w1, ffn_w2, final_norm_g, loss_target, m_w_mod, m_b_mod, m_norm_mix_g, m_norm_ffn_g, m_lru_w_y, m_lru_b_y, m_lru_w_in, m_lru_b_in, m_lru_conv_w, m_lru_conv_b, m_lru_w_a, m_lru_b_a, m_lru_w_x, m_lru_b_x, m_lru_lambda, m_lru_w_out, m_lru_b_out, m_pool_w, m_pool_scale, m_ffn_w1, m_ffn_w2, m_final_norm_g, v_w_mod, v_b_mod, v_norm_mix_g, v_norm_ffn_g, v_lru_w_y, v_lru_b_y, v_lru_w_in, v_lru_b_in, v_lru_conv_w, v_lru_conv_b, v_lru_w_a, v_lru_b_a, v_lru_w_x, v_lru_b_x, v_lru_lambda, v_lru_w_out, v_lru_b_out, v_pool_w, v_pool_scale, v_ffn_w1, v_ffn_w2, v_final_norm_g):
    S, D = x.shape[1], x.shape[2]
    L = w_mod.shape[0]
    NA = lru_w_y.shape[0]
    NB = pool_w.shape[0]
    F = ffn_w1.shape[2] * NQ
    f4 = F // NQ
    hd = D // HEADS
    Cs = w_mod.shape[2]
    assert L == DEPTH and Cs * NQ == N_MOD * D and D % 1024 == 0
    x2d = x.reshape(S, D)
    tgt = loss_target.reshape(S, D)
    q = 2 * lax.axis_index("x") + lax.axis_index("y")

    big = [ffn_w1, ffn_w2, lru_w_y, lru_w_in, lru_w_out, lru_w_a, lru_w_x, pool_w]
    rows = [int(w.size) // D for w in big]
    offs = [sum(rows[:k]) for k in range(len(big))]
    O_W1, O_W2, O_WY, O_WIN, O_WOUT, O_WA, O_WX, O_PW = offs
    R = sum(rows)
    dq = D // NQ
    s_w1 = [ffn_w1[i].astype(BF16) for i in range(L)]
    s_w2 = [ffn_w2[i].astype(BF16) for i in range(L)]
    s_wy = [lru_w_y[j].astype(BF16) for j in range(NA)]
    s_win = [lru_w_in[j].astype(BF16) for j in range(NA)]
    s_wout = [lru_w_out[j].astype(BF16) for j in range(NA)]
    s_tiny = jnp.concatenate([_shard_to_rows(w, D) for w in (lru_w_a, lru_w_x, pool_w)], axis=0).astype(BF16)

    cshard = lru_conv_w.reshape(-1)
    small_fwd = jnp.concatenate([c.reshape(-1), cshard, lru_b_a.reshape(-1), lru_b_x.reshape(-1),
                                 pool_scale.reshape(-1)])
    small_fwd = jnp.pad(small_fwd, (0, 8 * D - small_fwd.shape[0])).reshape(8, D)

    g_w1, g_w2 = [None] * L, [None] * L
    g_wy, g_win, g_wout = [None] * NA, [None] * NA, [None] * NA
    SG, g_wy[0], g_win[0], g_wout[0], g_tiny = _comm_only("gather_first", small=small_fwd,
                                                         gathers=(s_wy[0], s_win[0], s_wout[0], s_tiny))
    wa_full = _blockdiag_full(g_tiny[:, :rows[5]], NA, hd)
    wx_full = _blockdiag_full(g_tiny[:, rows[5]:rows[5] + rows[6]], NA, hd)
    pw_full = _blockdiag_full(g_tiny[:, rows[5] + rows[6]:], NB, hd)
    SGf = SG.reshape(NDEV, 8 * D)
    c_all = SGf[:, :D]
    SGq = SGf.reshape(NQ, 2, 8 * D)[:, 0]
    o = D
    n_cw = NA * CONV_W * D // NQ
    conv_w_full = SGq[:, o:o + n_cw].reshape(NQ, NA, CONV_W, D // NQ).transpose(1, 2, 0, 3).reshape(NA, CONV_W, D)
    o += n_cw
    n_b = NA * HEADS * hd // NQ
    b_a_full = SGq[:, o:o + n_b].reshape(NQ, NA, HEADS, hd // NQ).transpose(1, 2, 0, 3).reshape(NA, 1, D)
    o += n_b
    b_x_full = SGq[:, o:o + n_b].reshape(NQ, NA, HEADS, hd // NQ).transpose(1, 2, 0, 3).reshape(NA, 1, D)
    o += n_b
    n_ps = NB * D // NQ
    pool_scale_full = SGq[:, o:o + n_ps].reshape(NQ, NB, D // NQ).transpose(1, 0, 2).reshape(NB, 1, D)


    b_mod_sh = lax.dynamic_slice_in_dim(b_mod, q * Cs, Cs, axis=1).reshape(L, 1, Cs)
    modpart = _mod_part(c_all, w_mod, b_mod_sh)
    modq = _exchange_mod(modpart.transpose(1, 0, 2))
    mod = modq.transpose(1, 0, 2).reshape(L, N_MOD, D)
    mod = jnp.pad(mod, ((0, 0), (0, 8 - N_MOD), (0, 0)))

    saved = []
    xcur = x2d
    for i in range(L):
        j = i // 2
        gm = norm_mix_g[i].reshape(1, D)
        gf = norm_ffn_g[i].reshape(1, D)
        if i % 2 == 0:
            h, gb, xr0, hs, p, y, x1, xc_s, gr_s, gi_s, a_sv, mu_s, g_w1[i], g_w2[i] = _lru_fwd(
                xcur, mod[i], gm, g_wy[j], g_win[j], lru_b_y[j].reshape(1, D), lru_b_in[j].reshape(1, D),
                conv_w_full[j], lru_conv_b[j].reshape(1, D), wa_full[j], b_a_full[j], wx_full[j], b_x_full[j],
                lru_lambda[j].reshape(1, D), g_wout[j], lru_b_out[j].reshape(1, D), gathers=(s_w1[i], s_w2[i]))
            h2, a, z, x2, g_w1[i + 1], g_w2[i + 1] = _ffn_fwd(x1, mod[i], gf, g_w1[i], g_w2[i],
                                                              gathers=(s_w1[i + 1], s_w2[i + 1]))
            saved.append(dict(x=xcur, h=h, gb=gb, xr0=xr0, hs=hs, p=p, y=y, x1=x1, h2=h2, a=a, z=z,
                              lru=(xc_s, gr_s, gi_s, a_sv, mu_s)))
        else:
            if j + 1 < NA:
                pooled, x1, h2, a, z, x2, g_wy[j + 1], g_win[j + 1], g_wout[j + 1] = _pool_mix_ffn_fwd(
                    xcur, mod[i], gm, pw_full[j], pool_scale_full[j], gf, g_w1[i], g_w2[i],
                    gathers=(s_wy[j + 1], s_win[j + 1], s_wout[j + 1]))
            else:
                pooled, x1, h2, a, z, x2 = _pool_mix_ffn_fwd(xcur, mod[i], gm, pw_full[j], pool_scale_full[j], gf,
                                                             g_w1[i], g_w2[i])
            saved.append(dict(x=xcur, pooled=pooled, x1=x1, h2=h2, a=a, z=z))
        xcur = x2

    dx = xcur
    qv = q.reshape(1).astype(jnp.int32)
    ppack = lax.empty((R, D), F32)
    psib = lax.empty((R, D), F32)
    pending, summed, hosted = [], [], []

    def comm_args(cap_rows=None):
        nonlocal hosted
        hosted, n = [], 0
        for item in pending:
            if cap_rows is not None and hosted and n + item[0].shape[1] > cap_rows:
                break
            hosted.append(item)
            n += item[0].shape[1]
        kw = {}
        if hosted:
            kw["scatters"] = tuple(dw for dw, _ in hosted)
        if summed:
            kw["sib"] = (ppack, psib, tuple(summed))
        return kw

    def after_host(extra):
        nonlocal ppack, psib, pending, summed
        had_sib = bool(summed)
        summed = []
        for (dw, off), rb in zip(hosted, extra[:len(hosted)]):
            ppack = _sum_into(ppack, dw, rb, off, qv)
            summed.append((off, dw.shape[1]))
        if had_sib:
            psib = extra[len(hosted)]
        pending = pending[len(hosted):]

    dmod_rows = [None] * L
    dg_mix = [None] * L
    dg_ffn = [None] * L
    d_small = {}
    dwa_l, dwx_l, dpw_l = [None] * NA, [None] * NA, [None] * NB
    for i in reversed(range(L)):
        j = i // 2
        sv = saved[i]
        gm = norm_mix_g[i].reshape(1, D)
        gf = norm_ffn_g[i].reshape(1, D)
        head = (final_norm_g.reshape(1, D), tgt) if i == L - 1 else None
        outs = _ffn_bwd(dx, sv["x1"], sv["a"], sv["z"], mod[i], gf, g_w1[i], g_w2[i], head=head,
                        **comm_args(cap_rows=D))
        dx1, du, dz, facc = outs[:4]
        after_host(outs[4:])
        if head:
            loss = lax.psum(0.5 * jnp.sum(facc[0]) / D, ("x", "y", "c"))
            d_final_g = facc[1]
        if i % 2 == 0:
            outs = _dw_blocked(sv["h2"], du, False, False, "dw1", **comm_args(cap_rows=D))
            after_host(outs[1:])
            pending.append((outs[0], O_W1 + i * D))
            outs = _dw_blocked(sv["a"], dz, True, True, "dw2", **comm_args(cap_rows=D))
            after_host(outs[1:])
            pending.append((outs[0], O_W2 + i * f4))
            outs = _lru_bwd(dx1, sv["y"], sv["x"], sv["xr0"], sv["gb"], sv["hs"], *sv["lru"], mod[i], gm, g_wout[j],
                            g_wy[j], g_win[j], conv_w_full[j], wa_full[j], wx_full[j], lru_lambda[j].reshape(1, D),
                            **comm_args())
            dyp, dgb, dxr, dx, sm, dwa, dwx, macc = outs[:8]
            after_host(outs[8:])
            dwa_l[j], dwx_l[j] = dwa, dwx
            if i == 0:
                tiny = jnp.concatenate([_blockdiag_by_chip(jnp.stack(dwa_l), D), _blockdiag_by_chip(jnp.stack(dwx_l), D),
                                        _blockdiag_by_chip(jnp.stack(dpw_l), D)], axis=1).astype(BF16)
                pending.append((tiny, O_WA))
            if i == 0:
                outs = _dw_whole(sv["h"], [dgb, dxr], "dwy_dwin", **comm_args())
                after_host(outs[2:])
                pending.append((outs[0], O_WY + j * dq))
                pending.append((outs[1], O_WIN + j * dq))
                outs = _dw_whole(sv["p"], [dyp], "dwout", **comm_args())
                after_host(outs[1:])
                pending.append((outs[0], O_WOUT + j * dq))
            else:
                outs = _dw_whole(sv["p"], [dyp], "dwout", **comm_args())
                after_host(outs[1:])
                pending.append((outs[0], O_WOUT + j * dq))
                outs = _dw_whole(sv["h"], [dgb, dxr], "dwy_dwin", **comm_args())
                after_host(outs[2:])
                pending.append((outs[0], O_WY + j * dq))
                pending.append((outs[1], O_WIN + j * dq))
            d_small[("lru", j)] = (sm, macc[3])
            dgt_m = macc[2]
        else:
            dx, macc, dpw, dw1, dw2 = _pool_bwd(dx1, sv["x"], sv["pooled"], mod[i], gm, pw_full[j], pool_scale_full[j],
                                                sv["h2"], du, sv["a"], dz)
            pending.append((dw1, O_W1 + i * D))
            pending.append((dw2, O_W2 + i * f4))
            dpw_l[j] = dpw
            d_small[("pool", j)] = macc[3]
            dgt_m = macc[2]
        dmod_rows[i] = jnp.stack([macc[0], macc[1], dgt_m, facc[3], facc[4], facc[5]])
        dg_mix[i] = macc[6]
        dg_ffn[i] = facc[7]
    grad_x = dx.reshape(x.shape)

    lru_sm = [d_small[("lru", j)] for j in range(NA)]
    small_rows = [jnp.stack(dmod_rows).reshape(L * N_MOD, D), jnp.stack(dg_mix), jnp.stack(dg_ffn),
                  jnp.stack([s[0][9] for s in lru_sm]), jnp.stack([s[0][8] for s in lru_sm]),
                  jnp.stack([s[0][4] for s in lru_sm]), jnp.stack([s[0][7] for s in lru_sm]),
                  jnp.stack([s[1] for s in lru_sm]),
                  jnp.stack([s[0][0:CONV_W] for s in lru_sm]).reshape(NA * CONV_W, D),
                  jnp.stack([s[0][5] for s in lru_sm]), jnp.stack([s[0][6] for s in lru_sm]),
                  jnp.stack([d_small[("pool", j)] for j in range(NB)]), d_final_g.reshape(1, D)]
    small_g = jnp.concatenate(small_rows, axis=0)
    n_small = small_g.shape[0]
    assert n_small <= SMALL_ROWS
    small_g = jnp.pad(small_g, ((0, SMALL_ROWS - n_small), (0, 0)))

    outs = _comm_only("scatter_last", small=small_g, reduce_small=True, **comm_args())
    sg_all, sg_sum = outs[:2]
    after_host(outs[2:])
    psum_mine = ppack
    psum_sib = _comm_only("sibling_last", sib=(ppack, psib, tuple(summed)))[0]

    def big_update(w, m, v, off, name):
        shp = w.shape
        g, dl, m2, v2 = _adam_rows(w.reshape(-1, D), m.reshape(-1, D), v.reshape(-1, D), psum_mine, psum_sib, off, name)
        return g.reshape(shp), dl.reshape(shp), m2.reshape(shp), v2.reshape(shp)

    res = {}
    res["ffn_w1"] = big_update(ffn_w1, m_ffn_w1, v_ffn_w1, O_W1, "adam_w1")
    res["ffn_w2"] = big_update(ffn_w2, m_ffn_w2, v_ffn_w2, O_W2, "adam_w2")
    res["lru_w_y"] = big_update(lru_w_y, m_lru_w_y, v_lru_w_y, O_WY, "adam_wy")
    res["lru_w_in"] = big_update(lru_w_in, m_lru_w_in, v_lru_w_in, O_WIN, "adam_win")
    res["lru_w_out"] = big_update(lru_w_out, m_lru_w_out, v_lru_w_out, O_WOUT, "adam_wout")

    def tiny_parts(w, off):
        n = int(w.size) // D
        return psum_mine[off:off + n].reshape(w.shape), psum_sib[off:off + n].reshape(w.shape)

    tiny_items = [("lru_w_a", lru_w_a, m_lru_w_a, v_lru_w_a) + tiny_parts(lru_w_a, O_WA),
                  ("lru_w_x", lru_w_x, m_lru_w_x, v_lru_w_x) + tiny_parts(lru_w_x, O_WX),
                  ("pool_w", pool_w, m_pool_w, v_pool_w) + tiny_parts(pool_w, O_PW)]

    dmod_all = sg_all[:, :L * N_MOD, :].reshape(NDEV, L, N_MOD * D)
    dmod_sh = lax.dynamic_slice_in_dim(dmod_all, q * Cs, Cs, axis=2).transpose(1, 0, 2)
    res["w_mod"] = _wmod_update(c_all.T, dmod_sh, w_mod, m_w_mod, v_w_mod)

    r0 = 0
    by_rows, names_a = [], []
    for name, w, m, v in (("b_mod", b_mod, m_b_mod, v_b_mod), ("norm_mix_g", norm_mix_g, m_norm_mix_g, v_norm_mix_g),
                          ("norm_ffn_g", norm_ffn_g, m_norm_ffn_g, v_norm_ffn_g),
                          ("lru_b_y", lru_b_y, m_lru_b_y, v_lru_b_y), ("lru_b_in", lru_b_in, m_lru_b_in, v_lru_b_in),
                          ("lru_conv_b", lru_conv_b, m_lru_conv_b, v_lru_conv_b),
                          ("lru_lambda", lru_lambda, m_lru_lambda, v_lru_lambda),
                          ("lru_b_out", lru_b_out, m_lru_b_out, v_lru_b_out)):
        by_rows.append((w, m, v, r0))
        names_a.append(name)
        r0 += int(w.size) // D
    g_conv_w = lax.dynamic_slice_in_dim(sg_sum[r0:r0 + NA * CONV_W].reshape(NA, CONV_W, D), q * dq, dq, axis=2)
    r0 += NA * CONV_W
    g_b_a = lax.dynamic_slice_in_dim(sg_sum[r0:r0 + NA].reshape(NA, HEADS, hd), q * (hd // NQ), hd // NQ, axis=2)
    r0 += NA
    g_b_x = lax.dynamic_slice_in_dim(sg_sum[r0:r0 + NA].reshape(NA, HEADS, hd), q * (hd // NQ), hd // NQ, axis=2)
    r0 += NA
    g_ps = lax.dynamic_slice_in_dim(sg_sum[r0:r0 + NB], q * dq, dq, axis=1)
    r0 += NB
    by_rows.append((final_norm_g.reshape(1, D), m_final_norm_g.reshape(1, D), v_final_norm_g.reshape(1, D), r0))
    names_a.append("final_norm_g")
    sliced = [(lru_conv_w, m_lru_conv_w, v_lru_conv_w, g_conv_w), (lru_b_a, m_lru_b_a, v_lru_b_a, g_b_a),
              (lru_b_x, m_lru_b_x, v_lru_b_x, g_b_x), (pool_scale, m_pool_scale, v_pool_scale, g_ps)]
    res_a, res_b, res_c = _adam_small(sg_sum, by_rows, sliced, [t[1:] for t in tiny_items])
    for name, r in zip(names_a, res_a):
        res[name] = r
    for t, r in zip(tiny_items, res_c):
        res[t[0]] = r
    res["final_norm_g"] = tuple(a.reshape(D) for a in res["final_norm_g"])
    for name, (_, _, _, g), r in zip(("lru_conv_w", "lru_b_a", "lru_b_x", "pool_scale"), sliced, res_b):
        res[name] = (g,) + r

    order = ["w_mod", "b_mod", "norm_mix_g", "norm_ffn_g", "lru_w_y", "lru_b_y", "lru_w_in", "lru_b_in", "lru_conv_w",
             "lru_conv_b", "lru_w_a", "lru_b_a", "lru_w_x", "lru_b_x", "lru_lambda", "lru_w_out", "lru_b_out", "pool_w",
             "pool_scale", "ffn_w1", "ffn_w2", "final_norm_g"]
    return (loss, grad_x, *[res[n][0] for n in order], *[res[n][1] for n in order],
            *[res[n][2] for n in order], *[res[n][3] for n in order])
```

```python
import jax
import jax.numpy as jnp
from jax import lax
from jax.experimental import pallas as pl
from jax.experimental.pallas import tpu as pltpu

F32 = jnp.float32
BF16 = jnp.bfloat16
MESH = pl.DeviceIdType.MESH

NQ = 4
NDEV = 8
DEPTH = 4
N_MOD = 6
HEADS = 4
CONV_W = 4
POOL_WINDOWS = (2, 4, 8, 16)
LRU_C = 8.0
EPS = 1e-6
ADAM_LR, ADAM_B1, ADAM_B2, ADAM_EPS, ADAM_WD, ADAM_STEP = 0.001, 0.9, 0.999, 1e-08, 0.01, 10

TM = 512
TT = 256
TP = 256
TPF = 512
TK = 2048
SMALL_ROWS = 64
FORWARD_STEPS = 4
VMEM_LIMIT = 60 * 1024 * 1024


def _cparams(*sem):
    return pltpu.CompilerParams(dimension_semantics=tuple(sem), vmem_limit_bytes=VMEM_LIMIT)


def _dot(a, b):
    return jnp.dot(a, b, preferred_element_type=F32)


def _dot_nt(a, b):
    return lax.dot_general(a, b, (((1,), (1,)), ((), ())), preferred_element_type=F32)


def _dot_tn(a, b):
    return lax.dot_general(a, b, (((0,), (0,)), ((), ())), preferred_element_type=F32)


def _resident(shape, index_map):
    return pl.BlockSpec(shape, index_map, pipeline_mode=pl.Buffered(1))


def _rms_fwd(x, g, sc, sh):
    r = lax.rsqrt(jnp.mean(x * x, axis=-1, keepdims=True) + EPS)
    xhat = x * r
    n = xhat * g
    return n * (1.0 + sc) + sh, xhat, r, n


def _rms_bwd(dh, xhat, r, n, g, sc):
    dsh = jnp.sum(dh, axis=0, keepdims=True)
    dsc = jnp.sum(dh * n, axis=0, keepdims=True)
    dn = dh * (1.0 + sc)
    dg = jnp.sum(dn * xhat, axis=0, keepdims=True)
    dxh = dn * g
    dx = r * (dxh - xhat * jnp.mean(dxh * xhat, axis=-1, keepdims=True))
    return dx, dsh, dsc, dg


_GELU_K = 0.7978845608028654
_GELU_C = 0.044715


def _gelu(x):
    t = jnp.tanh(_GELU_K * (x + _GELU_C * x * x * x))
    return 0.5 * x * (1.0 + t), t


def _gelu_grad(x, t):
    return 0.5 * (1.0 + t) + 0.5 * x * (1.0 - t * t) * (_GELU_K * (1.0 + 3.0 * _GELU_C * x * x))


def _neg_expm1(y, exp_y):
    series = -(y * (1.0 + y * (0.5 + y * (1.0 / 6.0))))
    return jnp.where(y > -(1.0 / 64.0), series, 1.0 - exp_y)


def _zero_first(ref):
    @pl.when(pl.program_id(0) == 0)
    def _():
        ref[...] = jnp.zeros_like(ref)


def _my_pos():
    return lax.axis_index("x"), lax.axis_index("y"), lax.axis_index("c")


def _dev_index(x, y, c):
    return 4 * x + 2 * y + c


def _chip_peers(x, y):
    return [(1 - x, y), (x, 1 - y), (1 - x, 1 - y)]


def _all_peers(x, y, c):
    return [(px, py, c) for (px, py) in _chip_peers(x, y)] + [(x, y, 1 - c)] + \
           [(px, py, 1 - c) for (px, py) in _chip_peers(x, y)]


def _comm_run(phase, x, y, c, gathers, scatters, sib, send, recv, loc):
    q = 2 * x + y
    peers = _chip_peers(x, y)
    sibling = (x, y, 1 - c)

    def rcopy(src, dst, s, dev):
        return pltpu.make_async_remote_copy(src, dst, send.at[s], recv.at[s], device_id=dev, device_id_type=MESH)

    s = 0
    for gi, (src, dst) in enumerate(gathers):
        half = src.shape[0] // 2
        mine, other = pl.ds(c * half, half), pl.ds((1 - c) * half, half)
        own = pltpu.make_async_copy(src, dst.at[q], loc.at[gi])
        if phase == "start":
            own.start()
        elif phase == "finish":
            own.wait()
        for (px, py) in peers:
            pq = 2 * px + py
            s_ici, s_fwd = s, s + 1
            s += 2
            if phase == "start":
                rcopy(src.at[mine], dst.at[q].at[mine], s_ici, (px, py, c)).start()
            elif phase == "forward":
                rcopy(src.at[mine], dst.at[pq].at[mine], s_ici, (px, py, c)).wait_recv()
                rcopy(dst.at[pq].at[mine], dst.at[pq].at[mine], s_fwd, sibling).start()
            else:
                rcopy(dst.at[pq].at[other], dst.at[pq].at[other], s_fwd, sibling).wait_recv()
                rcopy(src.at[mine], dst.at[q].at[mine], s_ici, (px, py, c)).wait_send()
                rcopy(dst.at[pq].at[mine], dst.at[pq].at[mine], s_fwd, sibling).wait_send()
    direct = []
    for (src, dst) in scatters:
        for k, (px, py) in enumerate(peers):
            direct.append((src.at[2 * px + py], dst.at[k], (px, py, c)))
    if sib is not None:
        src, dst, ranges = sib
        for (off, rows) in ranges:
            direct.append((src.at[pl.ds(off, rows)], dst.at[pl.ds(off, rows)], sibling))
    if phase == "start":
        for k, (a, b, dev) in enumerate(direct):
            rcopy(a, b, s + k, dev).start()
    elif phase == "finish":
        for k, (a, b, dev) in enumerate(direct):
            rcopy(a, b, s + k, dev).wait_recv()
        for k, (a, b, dev) in enumerate(direct):
            rcopy(a, b, s + k, dev).wait_send()


def _comm_shapes(gathers, scatters, sib):
    assert all(g.shape[0] % 32 == 0 for g in gathers)
    cin = list(gathers) + list(scatters) + ([sib[0], sib[1]] if sib else [])
    cout = [jax.ShapeDtypeStruct((NQ,) + g.shape, g.dtype) for g in gathers] + \
           [jax.ShapeDtypeStruct((3,) + s.shape[1:], s.dtype) for s in scatters] + \
           ([jax.ShapeDtypeStruct(sib[1].shape, sib[1].dtype)] if sib else [])
    n_rem = 6 * len(gathers) + 3 * len(scatters) + (len(sib[2]) if sib else 0)
    sems = [pltpu.SemaphoreType.DMA((max(n_rem, 1),)), pltpu.SemaphoreType.DMA((max(n_rem, 1),)),
            pltpu.SemaphoreType.DMA((max(len(gathers), 1),))]
    return cin, cout, sems


def _pcall(body, *, name, grid, in_specs, out_specs, out_shape, operands, scratch_shapes=(),
           gathers=(), scatters=(), sib=None):
    assert len(grid) == 1
    out_shape, out_specs = tuple(out_shape), tuple(out_specs)
    if not (gathers or scatters or sib):
        return pl.pallas_call(body, name=name, grid=grid, in_specs=list(in_specs), out_specs=out_specs,
                              out_shape=out_shape, scratch_shapes=list(scratch_shapes),
                              compiler_params=_cparams("arbitrary"))(*operands)
    cin, cout, sems = _comm_shapes(gathers, scatters, sib)
    n_in, n_cin, n_out, n_cout, n_scr = len(operands), len(cin), len(out_shape), len(cout), len(scratch_shapes)
    ng, ns = len(gathers), len(scatters)
    nsteps = grid[0]

    def wrapped(*refs):
        ins = refs[:n_in]
        cins = refs[n_in:n_in + n_cin]
        o0 = n_in + n_cin
        outs = refs[o0:o0 + n_out]
        couts = refs[o0 + n_out:o0 + n_out + n_cout]
        s0 = o0 + n_out + n_cout
        scr = refs[s0:s0 + n_scr]
        send, recv, loc = refs[s0 + n_scr:s0 + n_scr + 3]
        x, y, c = _my_pos()

        def run(phase):
            g = [(cins[k], couts[k]) for k in range(ng)]
            sc = [(cins[ng + k], couts[ng + k]) for k in range(ns)]
            sb = (cins[ng + ns], couts[ng + ns], sib[2]) if sib else None
            _comm_run(phase, x, y, c, g, sc, sb, send, recv, loc)

        @pl.when(pl.program_id(0) == 0)
        def _():
            run("start")

        if ng:
            @pl.when(pl.program_id(0) == max(nsteps - FORWARD_STEPS, 0))
            def _():
                run("forward")

        body(*ins, *outs, *scr)

        @pl.when(pl.program_id(0) == nsteps - 1)
        def _():
            run("finish")

    anyspec = pl.BlockSpec(memory_space=pl.ANY)
    aliases = {n_in + ng + ns + 1: n_out + ng + ns} if sib else {}
    return pl.pallas_call(
        wrapped, name=name, grid=grid,
        in_specs=list(in_specs) + [anyspec] * n_cin, out_specs=out_specs + (anyspec,) * n_cout,
        out_shape=out_shape + tuple(cout), scratch_shapes=list(scratch_shapes) + sems,
        input_output_aliases=aliases,
        compiler_params=pltpu.CompilerParams(dimension_semantics=("arbitrary",), vmem_limit_bytes=VMEM_LIMIT,
                                             has_side_effects=True),
    )(*operands, *cin)


def _comm_only(name, small=None, reduce_small=False, gathers=(), scatters=(), sib=None):
    cin, cout, sems = _comm_shapes(gathers, scatters, sib)
    n_cin, n_cout = len(cin), len(cout)
    ng, ns = len(gathers), len(scatters)
    n_sm_in = 1 if small is not None else 0
    n_sm_out = (2 if reduce_small else 1) if small is not None else 0

    def body(*refs):
        sm_in = refs[:n_sm_in]
        cins = refs[n_sm_in:n_sm_in + n_cin]
        o0 = n_sm_in + n_cin
        sm_out = refs[o0:o0 + n_sm_out]
        couts = refs[o0 + n_sm_out:o0 + n_sm_out + n_cout]
        s0 = o0 + n_sm_out + n_cout
        send, recv, loc = refs[s0:s0 + 3]
        x, y, c = _my_pos()
        g = [(cins[k], couts[k]) for k in range(ng)]
        sc = [(cins[ng + k], couts[ng + k]) for k in range(ns)]
        sb = (cins[ng + ns], couts[ng + ns], sib[2]) if sib else None
        _comm_run("start", x, y, c, g, sc, sb, send, recv, loc)
        if small is not None:
            sm_send, sm_recv = refs[s0 + 3:s0 + 5]
            small_ref, sg_ref = sm_in[0], sm_out[0]
            me = _dev_index(x, y, c)
            sg_ref[me] = small_ref[...]
            peers = _all_peers(x, y, c)
            sm = [pltpu.make_async_remote_copy(small_ref, sg_ref.at[me], sm_send.at[k], sm_recv.at[k],
                                               device_id=peer, device_id_type=MESH) for k, peer in enumerate(peers)]
            for cp in sm:
                cp.start()
            for k, (px, py, pc) in enumerate(peers):
                pltpu.make_async_remote_copy(small_ref, sg_ref.at[_dev_index(px, py, pc)], sm_send.at[k], sm_recv.at[k],
                                             device_id=(px, py, pc), device_id_type=MESH).wait_recv()
            if reduce_small:
                acc = sg_ref[0]
                for d in range(1, NDEV):
                    acc = acc + sg_ref[d]
                sm_out[1][...] = acc
            for cp in sm:
                cp.wait_send()
        if ng:
            _comm_run("forward", x, y, c, g, sc, sb, send, recv, loc)
        _comm_run("finish", x, y, c, g, sc, sb, send, recv, loc)

    anyspec = pl.BlockSpec(memory_space=pl.ANY)
    vspec = pl.BlockSpec(memory_space=pltpu.VMEM)
    sm_shapes = []
    if small is not None:
        sm_shapes.append(jax.ShapeDtypeStruct((NDEV,) + small.shape, small.dtype))
        if reduce_small:
            sm_shapes.append(jax.ShapeDtypeStruct(small.shape, small.dtype))
        sems = sems + [pltpu.SemaphoreType.DMA((NDEV - 1,)), pltpu.SemaphoreType.DMA((NDEV - 1,))]
    aliases = {n_sm_in + ng + ns + 1: n_sm_out + ng + ns} if sib else {}
    return pl.pallas_call(
        body, name=name,
        in_specs=[vspec] * n_sm_in + [anyspec] * n_cin,
        out_specs=tuple([vspec] * n_sm_out + [anyspec] * n_cout),
        out_shape=tuple(sm_shapes + cout), scratch_shapes=sems, input_output_aliases=aliases,
        compiler_params=pltpu.CompilerParams(has_side_effects=True),
    )(*([small] if small is not None else []), *cin)


def _exchange_mod(modpart):
    _, L, Cs = modpart.shape

    def body(part_ref, out_ref, send, recv):
        x, y, c = _my_pos()
        q = 2 * x + y
        me = _dev_index(x, y, c)
        out_ref[q] = part_ref[me]
        sends = []
        for k, (px, py) in enumerate(_chip_peers(x, y)):
            cp = pltpu.make_async_remote_copy(part_ref.at[_dev_index(px, py, c)], out_ref.at[q], send.at[k], recv.at[k],
                                              device_id=(px, py, c), device_id_type=MESH)
            cp.start()
            sends.append(cp)
        for k, (px, py) in enumerate(_chip_peers(x, y)):
            pltpu.make_async_remote_copy(part_ref.at[me], out_ref.at[2 * px + py], send.at[k], recv.at[k],
                                         device_id=(px, py, c), device_id_type=MESH).wait_recv()
        for cp in sends:
            cp.wait_send()

    return pl.pallas_call(
        body, name="exchange_mod",
        out_shape=jax.ShapeDtypeStruct((NQ, L, Cs), modpart.dtype),
        in_specs=[pl.BlockSpec(memory_space=pltpu.VMEM)],
        out_specs=pl.BlockSpec(memory_space=pltpu.VMEM),
        scratch_shapes=[pltpu.SemaphoreType.DMA((3,)), pltpu.SemaphoreType.DMA((3,))],
        compiler_params=pltpu.CompilerParams(has_side_effects=True),
    )(modpart)


def _mod_part(c_all, w_mod, b_mod_sh):
    L, D, Cs = w_mod.shape
    tn = 512 if Cs % 512 == 0 else Cs

    def body(c_ref, w_ref, b_ref, o_ref):
        cv = c_ref[...]
        cond = cv * jax.nn.sigmoid(cv)
        o_ref[...] = jnp.dot(cond, w_ref[...], preferred_element_type=F32, precision=lax.Precision.HIGHEST) + b_ref[...]

    return pl.pallas_call(
        body, name="mod_part", grid=(L, Cs // tn),
        out_shape=jax.ShapeDtypeStruct((L, NDEV, Cs), F32),
        in_specs=[pl.BlockSpec((NDEV, D), lambda i, j: (0, 0)),
                  pl.BlockSpec((None, D, tn), lambda i, j: (i, 0, j)),
                  pl.BlockSpec((None, 1, tn), lambda i, j: (i, 0, j))],
        out_specs=pl.BlockSpec((None, NDEV, tn), lambda i, j: (i, 0, j)),
        compiler_params=_cparams("parallel", "parallel"),
    )(c_all, w_mod, b_mod_sh)


def _adam(w, g, m, v):
    m2 = ADAM_B1 * m + (1.0 - ADAM_B1) * g
    v2 = ADAM_B2 * v + (1.0 - ADAM_B2) * (g * g)
    m_hat = m2 / (1.0 - ADAM_B1 ** ADAM_STEP)
    v_hat = v2 / (1.0 - ADAM_B2 ** ADAM_STEP)
    delta = -ADAM_LR * (m_hat / (jnp.sqrt(v_hat) + ADAM_EPS) + ADAM_WD * w)
    return delta, m2, v2


def _wmod_update(c_all_t, dmod_sh, w, m, v):
    L, D, Cs = w.shape
    td = 256 if D % 256 == 0 else D

    def body(ct_ref, d_ref, w_ref, m_ref, v_ref, g_ref, dl_ref, m2_ref, v2_ref):
        cv = ct_ref[...]
        cond = cv * jax.nn.sigmoid(cv)
        g = cond[:, 0:1] * d_ref[0:1, :]
        for b in range(1, NDEV):
            g = g + cond[:, b:b + 1] * d_ref[b:b + 1, :]
        g_ref[...] = g
        dl_ref[...], m2_ref[...], v2_ref[...] = _adam(w_ref[...], g, m_ref[...], v_ref[...])

    blk = pl.BlockSpec((None, td, Cs), lambda i, j: (i, j, 0))
    out = jax.ShapeDtypeStruct((L, D, Cs), F32)
    return pl.pallas_call(
        body, name="wmod_update", grid=(L, D // td),
        out_shape=(out, out, out, out),
        in_specs=[pl.BlockSpec((td, NDEV), lambda i, j: (j, 0)),
                  pl.BlockSpec((None, NDEV, Cs), lambda i, j: (i, 0, 0)), blk, blk, blk],
        out_specs=(blk, blk, blk, blk),
        compiler_params=_cparams("parallel", "parallel"),
    )(c_all_t, dmod_sh, w, m, v)


def _adam_rows(w, m, v, pa, pb, row_off, name):
    rows, C = w.shape
    tr = 512 if rows % 512 == 0 else (128 if rows % 128 == 0 else rows)
    assert row_off % tr == 0
    ob = row_off // tr

    def body(w_ref, m_ref, v_ref, pa_ref, pb_ref, g_ref, dl_ref, m2_ref, v2_ref):
        g = pa_ref[...] + pb_ref[...]
        g_ref[...] = g
        dl_ref[...], m2_ref[...], v2_ref[...] = _adam(w_ref[...], g, m_ref[...], v_ref[...])

    blk = pl.BlockSpec((tr, C), lambda i: (i, 0))
    pblk = pl.BlockSpec((tr, C), lambda i: (ob + i, 0))
    out = jax.ShapeDtypeStruct((rows, C), F32)
    return pl.pallas_call(
        body, name=name, grid=(rows // tr,), out_shape=(out, out, out, out),
        in_specs=[blk, blk, blk, pblk, pblk], out_specs=(blk, blk, blk, blk),
        compiler_params=_cparams("parallel"),
    )(w, m, v, pa, pb)


def _adam_small(sg_sum, by_rows, sliced, pairs):
    D = sg_sum.shape[1]
    na, nb, nc = len(by_rows), len(sliced), len(pairs)

    def body(*refs):
        sg = refs[0]
        ins_a = [refs[1 + 3 * t:4 + 3 * t] for t in range(na)]
        p = 1 + 3 * na
        ins_b = [refs[p + 4 * t:p + 4 * t + 4] for t in range(nb)]
        p += 4 * nb
        ins_c = [refs[p + 5 * t:p + 5 * t + 5] for t in range(nc)]
        p += 5 * nc
        outs_a = [refs[p + 4 * t:p + 4 * t + 4] for t in range(na)]
        p += 4 * na
        outs_b = [refs[p + 3 * t:p + 3 * t + 3] for t in range(nb)]
        p += 3 * nb
        outs_c = [refs[p + 4 * t:p + 4 * t + 4] for t in range(nc)]
        for (w_ref, m_ref, v_ref, ga_ref, gb_ref), (g_ref, dl_ref, m2_ref, v2_ref) in zip(ins_c, outs_c):
            g = ga_ref[...] + gb_ref[...]
            g_ref[...] = g
            dl_ref[...], m2_ref[...], v2_ref[...] = _adam(w_ref[...], g, m_ref[...], v_ref[...])
        for (w_ref, m_ref, v_ref), (g_ref, dl_ref, m2_ref, v2_ref), (w, _, _, row0) in zip(ins_a, outs_a, by_rows):
            n, k = w.shape[0], w.shape[1] // D
            pieces = [(slice(0, n), slice(0, D), slice(row0, row0 + n))] if k == 1 else \
                     [(slice(i, i + 1), slice(kk * D, (kk + 1) * D), slice(row0 + i * k + kk, row0 + i * k + kk + 1))
                      for i in range(n) for kk in range(k)]
            for rs, cs, gs in pieces:
                g = sg[gs, :]
                g_ref[rs, cs] = g
                dl_ref[rs, cs], m2_ref[rs, cs], v2_ref[rs, cs] = _adam(w_ref[rs, cs], g, m_ref[rs, cs], v_ref[rs, cs])
        for (w_ref, m_ref, v_ref, g_ref), (dl_ref, m2_ref, v2_ref) in zip(ins_b, outs_b):
            dl_ref[...], m2_ref[...], v2_ref[...] = _adam(w_ref[...], g_ref[...], m_ref[...], v_ref[...])

    operands = [sg_sum] + [a for t in by_rows for a in t[:3]] + [a for t in sliced for a in t] + \
               [a for t in pairs for a in t]
    out_shape = [jax.ShapeDtypeStruct(t[0].shape, F32) for t in by_rows for _ in range(4)] + \
                [jax.ShapeDtypeStruct(t[0].shape, F32) for t in sliced for _ in range(3)] + \
                [jax.ShapeDtypeStruct(t[0].shape, F32) for t in pairs for _ in range(4)]
    outs = pl.pallas_call(body, name="adam_small", out_shape=tuple(out_shape))(*operands)
    res_a = [tuple(outs[4 * t:4 * t + 4]) for t in range(na)]
    o = 4 * na
    res_b = [tuple(outs[o + 3 * t:o + 3 * t + 3]) for t in range(nb)]
    o += 3 * nb
    res_c = [tuple(outs[o + 4 * t:o + 4 * t + 4]) for t in range(nc)]
    return res_a, res_b, res_c


def _sum_into(ppack, dw, rb, off, qv):
    _, rows, D = dw.shape
    tr = 256 if rows % 256 == 0 else 128
    assert rows % tr == 0 and off % tr == 0
    ob = off // tr

    def body(q_ref, o_ref, r_ref, pin_ref, p_ref):
        acc = o_ref[...].astype(F32)
        for k in range(3):
            acc = acc + r_ref[k].astype(F32)
        p_ref[...] = acc

    return pl.pallas_call(
        body, name="sum_partials", out_shape=jax.ShapeDtypeStruct(ppack.shape, ppack.dtype),
        grid_spec=pltpu.PrefetchScalarGridSpec(
            num_scalar_prefetch=1, grid=(rows // tr,),
            in_specs=[pl.BlockSpec((None, tr, D), lambda i, q_ref: (q_ref[0], i, 0)),
                      pl.BlockSpec((3, tr, D), lambda i, q_ref: (0, i, 0)),
                      pl.BlockSpec(memory_space=pl.ANY)],
            out_specs=pl.BlockSpec((tr, D), lambda i, q_ref: (ob + i, 0))),
        input_output_aliases={3: 0},
        compiler_params=_cparams("parallel"),
    )(qv, dw, rb, ppack)


def _wspec(g):
    return _resident(g.shape, lambda i: (0, 0, 0))


def _ffn_fwd_inner(x1, mod_ref, gf_ref, w1_ref, w2_ref, h2_ref, a_ref, z_ref, x2_ref):
    h2 = _rms_fwd(x1, gf_ref[...], mod_ref[4:5, :], mod_ref[3:4, :])[0]
    h2b = h2.astype(BF16)
    h2_ref[...] = h2b
    f4 = w1_ref.shape[2]
    z = jnp.zeros(x1.shape, F32)
    for q in range(NQ):
        a = jnp.maximum(_dot(h2b, w1_ref[q]), 0.0)
        a_ref[:, q * f4:(q + 1) * f4] = a.astype(BF16)
        z = z + _dot((a * a).astype(BF16), w2_ref[q])
    z_ref[...] = z.astype(BF16)
    x2_ref[...] = x1 + mod_ref[5:6, :] * z


def _sigmoid(x):
    return jax.nn.sigmoid(x)


def _heads_dot(xb, w_ref, hd, nt=False):
    outs = []
    for h in range(HEADS):
        xs = xb[:, h * hd:(h + 1) * hd]
        outs.append(_dot_nt(xs, w_ref[h]) if nt else _dot(xs, w_ref[h]))
    return jnp.concatenate(outs, axis=1)


def _lru_gates(xc, wa_ref, ba, wx_ref, bx, lam, hd):
    xcb = xc.astype(BF16)
    gate_r = _sigmoid(_heads_dot(xcb, wa_ref, hd) + ba)
    gate_i = _sigmoid(_heads_dot(xcb, wx_ref, hd) + bx)
    ls = jax.nn.log_sigmoid(lam)
    log_a = gate_r * (LRU_C * ls)
    a = jnp.exp(log_a)
    mult = jnp.sqrt(_neg_expm1(2.0 * log_a, a * a))
    return xcb, gate_r, gate_i, ls, a, mult


def _conv_taps(xext, cw, tt):
    acc = cw[0:1, :] * xext[pl.ds(8 - (CONV_W - 1), tt), :]
    for k in range(1, CONV_W):
        acc = acc + cw[k:k + 1, :] * xext[pl.ds(8 - (CONV_W - 1) + k, tt), :]
    return acc


def _lru_fwd(x, mod_l, g_mix, g_wy, g_win, b_y, b_in, cw, cb, wa, ba, wx, bx, lam, g_wout, b_out, **comm):
    S, W = x.shape
    tt = min(TT, S)
    hd = W // HEADS

    def body(x_ref, mod_ref, g_ref, wy_ref, win_ref, by_ref, bin_ref, cw_ref, cb_ref, wa_ref, ba_ref, wx_ref, bx_ref,
             lam_ref, wo_ref, bo_ref, h_ref, gb_ref, xr_ref, hs_ref, p_ref, y_ref, x1_ref,
             xc_ref, gr_ref, gi_ref, a_s, mu_ref, xext, u_s, carry):
        i = pl.program_id(0)

        @pl.when(i == 0)
        def _():
            carry[...] = jnp.zeros_like(carry)
            xext[0:8, :] = jnp.zeros((8, W), F32)

        @pl.when(i > 0)
        def _():
            xext[0:8, :] = xext[pl.ds(tt, 8), :]

        xv = x_ref[...]
        hb = _rms_fwd(xv, g_ref[...], mod_ref[1:2, :], mod_ref[0:1, :])[0].astype(BF16)
        h_ref[...] = hb
        gbv = _dot(hb, wy_ref[...].reshape(W, W)) + by_ref[...]
        gb_ref[...] = gbv
        xr = _dot(hb, win_ref[...].reshape(W, W)) + bin_ref[...]
        xr_ref[...] = xr
        xext[pl.ds(8, tt), :] = xr
        xc = _conv_taps(xext, cw_ref[...], tt) + cb_ref[...]
        _, gate_r, gate_i, _, a, mult = _lru_gates(xc, wa_ref, ba_ref[...], wx_ref, bx_ref[...], lam_ref[...], hd)
        xc_ref[...] = xc
        gr_ref[...] = gate_r
        gi_ref[...] = gate_i
        mu_ref[...] = mult
        a_s[...] = a
        u_s[...] = mult * (gate_i * xc)
        row = lax.broadcasted_iota(jnp.int32, (8, W), 0)

        def step(k, _):
            off = pl.multiple_of(k * 8, 8)
            A = a_s[pl.ds(off, 8), :]
            U = u_s[pl.ds(off, 8), :]
            for d in (1, 2, 4):
                keep = row >= d
                Us = jnp.where(keep, pltpu.roll(U, d, 0), 0.0)
                As = jnp.where(keep, pltpu.roll(A, d, 0), 1.0)
                U = U + A * Us
                A = A * As
            H = U + A * carry[...]
            hs_ref[pl.ds(off, 8), :] = H
            carry[...] = jnp.broadcast_to(H[7:8, :], (8, W))
            return 0

        lax.fori_loop(0, tt // 8, step, 0)
        pb = (hs_ref[...] * _gelu(gbv)[0]).astype(BF16)
        p_ref[...] = pb
        y = _dot(pb, wo_ref[...].reshape(W, W)) + bo_ref[...]
        y_ref[...] = y.astype(BF16)
        x1_ref[...] = xv + mod_ref[2:3, :] * y

    tile = pl.BlockSpec((tt, W), lambda i: (i, 0))
    row = pl.BlockSpec((1, W), lambda i: (0, 0))
    wblk = pl.BlockSpec((HEADS, hd, hd), lambda i: (0, 0, 0))
    f32o, bf16o = jax.ShapeDtypeStruct((S, W), F32), jax.ShapeDtypeStruct((S, W), BF16)
    return _pcall(
        body, name="lru_fwd", grid=(S // tt,),
        out_shape=(bf16o, f32o, f32o, f32o, bf16o, bf16o, f32o, f32o, f32o, f32o, f32o, f32o),
        in_specs=[tile, pl.BlockSpec((8, W), lambda i: (0, 0)), row, _wspec(g_wy), _wspec(g_win), row, row,
                  pl.BlockSpec((CONV_W, W), lambda i: (0, 0)), row, wblk, row, wblk, row, row, _wspec(g_wout), row],
        out_specs=(tile,) * 12,
        scratch_shapes=[pltpu.VMEM((tt + 8, W), F32), pltpu.VMEM((tt, W), F32), pltpu.VMEM((8, W), F32)],
        operands=(x, mod_l, g_mix, g_wy, g_win, b_y, b_in, cw, cb, wa, ba, wx, bx, lam, g_wout, b_out), **comm)


def _ffn_out_shapes(S, D, F):
    return (jax.ShapeDtypeStruct((S, D), BF16), jax.ShapeDtypeStruct((S, F), BF16),
            jax.ShapeDtypeStruct((S, D), BF16), jax.ShapeDtypeStruct((S, D), F32))


def _ffn_fwd(x1, mod_l, g_ffn, g_w1, g_w2, **comm):
    S, D = x1.shape
    tm = min(TM, S)
    F = g_w1.shape[2] * NQ

    def body(x1_ref, mod_ref, gf_ref, w1_ref, w2_ref, h2_ref, a_ref, z_ref, x2_ref):
        _ffn_fwd_inner(x1_ref[...], mod_ref, gf_ref, w1_ref, w2_ref, h2_ref, a_ref, z_ref, x2_ref)

    tile = pl.BlockSpec((tm, D), lambda i: (i, 0))
    row = pl.BlockSpec((1, D), lambda i: (0, 0))
    return _pcall(
        body, name="ffn_fwd", grid=(S // tm,),
        out_shape=_ffn_out_shapes(S, D, F),
        in_specs=[tile, pl.BlockSpec((8, D), lambda i: (0, 0)), row, _wspec(g_w1), _wspec(g_w2)],
        out_specs=(tile, pl.BlockSpec((tm, F), lambda i: (i, 0)), tile, tile),
        operands=(x1, mod_l, g_ffn, g_w1, g_w2), **comm)


def _window_vec(D):
    gd = D // len(POOL_WINDOWS)
    lane = lax.broadcasted_iota(jnp.int32, (1, D), 1)
    w = jnp.full((1, D), float(POOL_WINDOWS[0]), F32)
    for g in range(1, len(POOL_WINDOWS)):
        w = jnp.where(lane >= g * gd, float(POOL_WINDOWS[g]), w)
    return w


def _pool_mix_ffn_fwd(x, mod_l, g_mix, pw, ps, g_ffn, g_w1, g_w2, **comm):
    S, D = x.shape
    tm = min(TPF, S)
    F = g_w1.shape[2] * NQ
    gd = D // len(POOL_WINDOWS)
    n = tm + 24

    def body(x_ref, xh_ref, mod_ref, gm_ref, pw_ref, ps_ref, gf_ref, w1_ref, w2_ref,
             pl_ref, x1_ref, h2_ref, a_ref, z_ref, x2_ref, ext, b1, b2):
        i = pl.program_id(0)
        g, sc, sh = gm_ref[...], mod_ref[1:2, :], mod_ref[0:1, :]
        xv = x_ref[...]
        h = _rms_fwd(xv, g, sc, sh)[0]
        hh = _rms_fwd(xh_ref[...], g, sc, sh)[0]
        zeros8 = jnp.zeros((8, D), F32)
        ext[0:8, :] = zeros8
        b1[0:8, :] = zeros8
        b2[0:8, :] = zeros8
        ext[8:24, :] = jnp.where(i > 0, hh, 0.0)
        ext[pl.ds(24, tm), :] = h
        m = n - 8
        b1[pl.ds(8, m), :] = ext[pl.ds(8, m), :] + ext[pl.ds(7, m), :]
        b2[pl.ds(8, m), gd:] = b1[pl.ds(8, m), gd:] + b1[pl.ds(6, m), gd:]
        b1[pl.ds(8, m), 2 * gd:] = b2[pl.ds(8, m), 2 * gd:] + b2[pl.ds(4, m), 2 * gd:]
        b2[pl.ds(8, m), 3 * gd:] = b1[pl.ds(8, m), 3 * gd:] + b1[pl.ds(0, m), 3 * gd:]
        wsum = jnp.concatenate([b1[pl.ds(24, tm), 0:gd], b2[pl.ds(24, tm), gd:2 * gd],
                                b1[pl.ds(24, tm), 2 * gd:3 * gd], b2[pl.ds(24, tm), 3 * gd:]], axis=1)
        t1 = (lax.broadcasted_iota(jnp.int32, (tm, 1), 0) + (i * tm + 1)).astype(F32)
        cnt = jnp.minimum(t1, _window_vec(D))
        pooled = (wsum / cnt - h).astype(BF16)
        pl_ref[...] = pooled
        y = _heads_dot(pooled, pw_ref, gd) * ps_ref[...]
        x1 = xv + mod_ref[2:3, :] * y
        x1_ref[...] = x1
        _ffn_fwd_inner(x1, mod_ref, gf_ref, w1_ref, w2_ref, h2_ref, a_ref, z_ref, x2_ref)

    tile = pl.BlockSpec((tm, D), lambda i: (i, 0))
    halo = pl.BlockSpec((16, D), lambda i: (jnp.maximum(i * (tm // 16) - 1, 0), 0))
    row = pl.BlockSpec((1, D), lambda i: (0, 0))
    return _pcall(
        body, name="pool_mix_ffn_fwd", grid=(S // tm,),
        out_shape=(jax.ShapeDtypeStruct((S, D), BF16), jax.ShapeDtypeStruct((S, D), F32)) + _ffn_out_shapes(S, D, F),
        in_specs=[tile, halo, pl.BlockSpec((8, D), lambda i: (0, 0)), row,
                  pl.BlockSpec((len(POOL_WINDOWS), gd, gd), lambda i: (0, 0, 0)), row, row,
                  _wspec(g_w1), _wspec(g_w2)],
        out_specs=(tile, tile, tile, pl.BlockSpec((tm, F), lambda i: (i, 0)), tile, tile),
        scratch_shapes=[pltpu.VMEM((n, D), F32), pltpu.VMEM((n, D), F32), pltpu.VMEM((n, D), F32)],
        operands=(x, x, mod_l, g_mix, pw, ps, g_ffn, g_w1, g_w2), **comm)


def _loss_head(xv, gv, tv, acc_ref):
    D = xv.shape[1]
    r = lax.rsqrt(jnp.mean(xv * xv, axis=-1, keepdims=True) + EPS)
    xhat = xv * r
    err = xhat * gv - tv
    acc_ref[0:1, :] += jnp.sum(err * err, axis=0, keepdims=True)
    dy = err * (1.0 / D)
    acc_ref[1:2, :] += jnp.sum(dy * xhat, axis=0, keepdims=True)
    dxh = dy * gv
    return r * (dxh - xhat * jnp.mean(dxh * xhat, axis=-1, keepdims=True))


def _ffn_bwd(dx2, x1, a, z, mod_l, g_ffn, g_w1, g_w2, head=None, **comm):
    S, D = dx2.shape
    F = a.shape[1]
    f4 = F // NQ
    tm = min(TM, S)
    nh = 2 if head else 0

    def body(*refs):
        dx2_ref, x1_ref, a_ref, z_ref, mod_ref, gf_ref, w1_ref, w2_ref = refs[:8]
        dx1_ref, du_ref, dz_ref, acc_ref = refs[8 + nh:]
        _zero_first(acc_ref)
        dx2v = dx2_ref[...]
        if head:
            dx2v = _loss_head(dx2v, refs[8][...], refs[9][...], acc_ref)
        acc_ref[5:6, :] +=jnp.sum(dx2v * z_ref[...].astype(F32), axis=0, keepdims=True)
        dzb = (dx2v * mod_ref[5:6, :]).astype(BF16)
        dz_ref[...] = dzb
        dh2 = jnp.zeros((tm, D), F32)
        for q in range(NQ):
            av = a_ref[:, q * f4:(q + 1) * f4].astype(F32)
            du = (_dot_nt(dzb, w2_ref[q]) * (2.0 * av)).astype(BF16)
            du_ref[:, q * f4:(q + 1) * f4] = du
            dh2 = dh2 + _dot_nt(du, w1_ref[q])
        g, sc = gf_ref[...], mod_ref[4:5, :]
        _, xhat, r, n = _rms_fwd(x1_ref[...], g, sc, mod_ref[3:4, :])
        dx, dsh, dsc, dg = _rms_bwd(dh2, xhat, r, n, g, sc)
        acc_ref[3:4, :] += dsh
        acc_ref[4:5, :] += dsc
        acc_ref[7:8, :] += dg
        dx1_ref[...] = dx2v + dx

    tile = pl.BlockSpec((tm, D), lambda i: (i, 0))
    wide = pl.BlockSpec((tm, F), lambda i: (i, 0))
    return _pcall(
        body, name="ffn_bwd", grid=(S // tm,),
        out_shape=(jax.ShapeDtypeStruct((S, D), F32), jax.ShapeDtypeStruct((S, F), BF16),
                   jax.ShapeDtypeStruct((S, D), BF16), jax.ShapeDtypeStruct((8, D), F32)),
        in_specs=[tile, tile, wide, tile, pl.BlockSpec((8, D), lambda i: (0, 0)), pl.BlockSpec((1, D), lambda i: (0, 0)),
                  _wspec(g_w1), _wspec(g_w2)] + ([pl.BlockSpec((1, D), lambda i: (0, 0)), tile] if head else []),
        out_specs=(tile, wide, tile, pl.BlockSpec((8, D), lambda i: (0, 0))),
        operands=(dx2, x1, a, z, mod_l, g_ffn, g_w1, g_w2) + (tuple(head) if head else ()), **comm)


def _dw_blocked(a, b, by_rows, square_a, name):
    S = a.shape[0]
    tk = min(TK, S)
    nk = S // tk
    if by_rows:
        bm, bn = a.shape[1] // NQ, b.shape[1]
        a_map, b_map = (lambda q, k: (k, q)), (lambda q, k: (k, 0))
    else:
        bm, bn = a.shape[1], b.shape[1] // NQ
        a_map, b_map = (lambda q, k: (k, 0)), (lambda q, k: (k, q))

    def body(a_ref, b_ref, o_ref, acc):
        k = pl.program_id(1)

        @pl.when(k == 0)
        def _():
            acc[...] = jnp.zeros_like(acc)

        av = a_ref[...]
        if square_a:
            av = av * av
        acc[...] += _dot_tn(av, b_ref[...])

        @pl.when(k == nk - 1)
        def _():
            o_ref[...] = acc[...].astype(o_ref.dtype)

    return pl.pallas_call(
        body, name=name, grid=(NQ, nk),
        out_shape=jax.ShapeDtypeStruct((NQ, bm, bn), BF16),
        in_specs=[pl.BlockSpec((tk, bm), a_map), pl.BlockSpec((tk, bn), b_map)],
        out_specs=pl.BlockSpec((None, bm, bn), lambda q, k: (q, 0, 0)),
        scratch_shapes=[pltpu.VMEM((bm, bn), F32)],
        compiler_params=_cparams("parallel", "arbitrary"),
    )(a, b)


def _dw_whole(a, bs, name, **comm):
    S, M = a.shape
    N = bs[0].shape[1]
    tk = min(TK, S)
    nk = S // tk
    nb = len(bs)

    def body(*refs):
        a_ref, b_refs, o_refs, accs = refs[0], refs[1:1 + nb], refs[1 + nb:1 + 2 * nb], refs[1 + 2 * nb:]
        k = pl.program_id(0)

        @pl.when(k == 0)
        def _():
            for acc in accs:
                acc[...] = jnp.zeros_like(acc)

        av = a_ref[...]
        for b_ref, acc in zip(b_refs, accs):
            acc[...] += _dot_tn(av, b_ref[...])

        @pl.when(k == nk - 1)
        def _():
            for o_ref, acc in zip(o_refs, accs):
                o_ref[...] = acc[...].reshape(NQ, M // NQ, N).astype(o_ref.dtype)

    return _pcall(
        body, name=name, grid=(nk,),
        out_shape=tuple(jax.ShapeDtypeStruct((NQ, M // NQ, N), BF16) for _ in bs),
        in_specs=[pl.BlockSpec((tk, M), lambda k: (k, 0))] + [pl.BlockSpec((tk, N), lambda k: (k, 0)) for _ in bs],
        out_specs=tuple(pl.BlockSpec((NQ, M // NQ, N), lambda k: (0, 0, 0)) for _ in bs),
        scratch_shapes=[pltpu.VMEM((M, N), F32) for _ in bs],
        operands=(a, *bs), **comm)


def _lru_bwd(dx1, y, x, xr0, gb, hs, xc_, gate_r_, gate_i_, a_, mult_, mod_l, g_mix, g_wout, g_wy, g_win, cw, wa, wx,
             lam, **comm):
    S, W = xr0.shape
    tt = min(TT, S)
    nb = S // tt
    hd = W // HEADS

    def body(dx1_ref, y_ref, x_ref, xr_ref, gb_ref, hs_ref, hsh_ref, xc_ref, gr_ref, gi_ref, a_s, mu_ref,
             mod_ref, gm_ref, wo_ref, wy_ref, win_ref, cw_ref, wa_ref, wx_ref, lam_ref,
             dy_ref, dgb_ref, dxr_ref, dx_ref, sm_ref, dwa_ref, dwx_ref, acc_ref,
             hext, qext, dext, b_s, qc, dc):
        i = pl.program_id(0)
        blk = nb - 1 - i

        @pl.when(i == 0)
        def _():
            sm_ref[...] = jnp.zeros_like(sm_ref)
            dwa_ref[...] = jnp.zeros_like(dwa_ref)
            dwx_ref[...] = jnp.zeros_like(dwx_ref)
            acc_ref[...] = jnp.zeros_like(acc_ref)
            qc[...] = jnp.zeros_like(qc)
            dc[...] = jnp.zeros_like(dc)

        dx1v = dx1_ref[...]
        acc_ref[2:3, :] += jnp.sum(dx1v * y_ref[...].astype(F32), axis=0, keepdims=True)
        dy = dx1v * mod_ref[2:3, :]
        acc_ref[3:4, :] += jnp.sum(dy, axis=0, keepdims=True)
        dyb = dy.astype(BF16)
        dy_ref[...] = dyb
        dpv = _dot_nt(dyb, wo_ref[...].reshape(W, W))

        hext[0:8, :] = jnp.where(blk > 0, hsh_ref[...], 0.0)
        hext[pl.ds(8, tt), :] = hs_ref[...]
        cw = cw_ref[...]
        lam = lam_ref[...]
        xc, gate_r, gate_i, a, mult = xc_ref[...], gr_ref[...], gi_ref[...], a_s[...], mu_ref[...]
        xcb = xc.astype(BF16)
        ls = jax.nn.log_sigmoid(lam)

        gbv = gb_ref[...]
        gate, th = _gelu(gbv)
        dgb = dpv * hs_ref[...] * _gelu_grad(gbv, th)
        dgbb = dgb.astype(BF16)
        dgb_ref[...] = dgbb
        sm_ref[9:10, :] += jnp.sum(dgb, axis=0, keepdims=True)
        dhs = dpv * gate

        b_s[...] = a * dhs
        qext[pl.ds(tt, 8), :] = qc[...]
        row = lax.broadcasted_iota(jnp.int32, (8, W), 0)

        def step(k, _):
            off = pl.multiple_of((tt // 8 - 1 - k) * 8, 8)
            A = a_s[pl.ds(off, 8), :]
            B = b_s[pl.ds(off, 8), :]
            for d in (1, 2, 4):
                keep = row < 8 - d
                Bs = jnp.where(keep, pltpu.roll(B, 8 - d, 0), 0.0)
                As = jnp.where(keep, pltpu.roll(A, 8 - d, 0), 1.0)
                B = B + A * Bs
                A = A * As
            Q = B + A * qc[...]
            qext[pl.ds(off, 8), :] = Q
            qc[...] = jnp.broadcast_to(Q[0:1, :], (8, W))
            return 0

        lax.fori_loop(0, tt // 8, step, 0)
        gsc = dhs + qext[pl.ds(1, tt), :]
        da = gsc * hext[pl.ds(7, tt), :]
        t1 = gsc * xc
        dmult = t1 * gate_i
        dgate_i = t1 * mult
        dxc = gsc * (mult * gate_i)
        dlog_a = da * a - dmult * (a * a) / mult
        dgate_r = dlog_a * (LRU_C * ls)
        sm_ref[7:8, :] += jnp.sum(dlog_a * (LRU_C * gate_r), axis=0, keepdims=True)
        dga = dgate_r * gate_r * (1.0 - gate_r)
        dgx = dgate_i * gate_i * (1.0 - gate_i)
        sm_ref[5:6, :] += jnp.sum(dga, axis=0, keepdims=True)
        sm_ref[6:7, :] += jnp.sum(dgx, axis=0, keepdims=True)
        dgab = dga.astype(BF16)
        dgxb = dgx.astype(BF16)
        dxc = dxc + _heads_dot(dgab, wa_ref, hd, nt=True) + _heads_dot(dgxb, wx_ref, hd, nt=True)
        for h in range(HEADS):
            sl = slice(h * hd, (h + 1) * hd)
            dwa_ref[h] += _dot_tn(xcb[:, sl], dgab[:, sl])
            dwx_ref[h] += _dot_tn(xcb[:, sl], dgxb[:, sl])
        sm_ref[4:5, :] += jnp.sum(dxc, axis=0, keepdims=True)
        dext[pl.ds(0, tt), :] = dxc
        dext[pl.ds(tt, 8), :] = dc[...]
        xrv = xr_ref[...]
        dxr = None
        for k in range(CONV_W):
            up = dext[pl.ds(CONV_W - 1 - k, tt), :]
            sm_ref[k:k + 1, :] += jnp.sum(up * xrv, axis=0, keepdims=True)
            dxr = cw[k:k + 1, :] * up if dxr is None else dxr + cw[k:k + 1, :] * up
        dc[...] = dext[0:8, :]
        sm_ref[8:9, :] += jnp.sum(dxr, axis=0, keepdims=True)
        dxrb = dxr.astype(BF16)
        dxr_ref[...] = dxrb

        dh = _dot_nt(dxrb, win_ref[...].reshape(W, W)) + _dot_nt(dgbb, wy_ref[...].reshape(W, W))
        g, sc = gm_ref[...], mod_ref[1:2, :]
        _, xhat, r, n = _rms_fwd(x_ref[...], g, sc, mod_ref[0:1, :])
        dx, dsh, dsc, dg = _rms_bwd(dh, xhat, r, n, g, sc)
        acc_ref[0:1, :] += dsh
        acc_ref[1:2, :] += dsc
        acc_ref[6:7, :] += dg
        dx_ref[...] = dx1v + dx

        @pl.when(i == nb - 1)
        def _():
            sm_ref[7:8, :] = sm_ref[7:8, :] * jax.nn.sigmoid(-lam)

    rev = lambda i: (nb - 1 - i, 0)
    tile = pl.BlockSpec((tt, W), rev)
    halo = pl.BlockSpec((8, W), lambda i: (jnp.maximum((nb - 1 - i) * (tt // 8) - 1, 0), 0))
    row = pl.BlockSpec((1, W), lambda i: (0, 0))
    wblk = pl.BlockSpec((HEADS, hd, hd), lambda i: (0, 0, 0))
    bf16o = jax.ShapeDtypeStruct((S, W), BF16)
    return _pcall(
        body, name="lru_bwd", grid=(nb,),
        out_shape=(bf16o, bf16o, bf16o, jax.ShapeDtypeStruct((S, W), F32),
                   jax.ShapeDtypeStruct((16, W), F32), jax.ShapeDtypeStruct((HEADS, hd, hd), F32),
                   jax.ShapeDtypeStruct((HEADS, hd, hd), F32), jax.ShapeDtypeStruct((8, W), F32)),
        in_specs=[tile, tile, tile, tile, tile, tile, halo, tile, tile, tile, tile, tile,
                  pl.BlockSpec((8, W), lambda i: (0, 0)), row,
                  _wspec(g_wout), _wspec(g_wy), _wspec(g_win), pl.BlockSpec((CONV_W, W), lambda i: (0, 0)),
                  wblk, wblk, row],
        out_specs=(tile, tile, tile, tile, pl.BlockSpec((16, W), lambda i: (0, 0)), wblk, wblk,
                   pl.BlockSpec((8, W), lambda i: (0, 0))),
        scratch_shapes=[pltpu.VMEM((tt + 8, W), F32), pltpu.VMEM((tt + 8, W), F32), pltpu.VMEM((tt + 8, W), F32),
                        pltpu.VMEM((tt, W), F32), pltpu.VMEM((8, W), F32), pltpu.VMEM((8, W), F32)],
        operands=(dx1, y, x, xr0, gb, hs, hs, xc_, gate_r_, gate_i_, a_, mult_, mod_l, g_mix, g_wout, g_wy, g_win,
                  cw, wa, wx, lam),
        **comm)


def _pool_bwd(dx1, x, pooled, mod_l, g_mix, pw, ps, h2, du, a, dz):
    S, D = x.shape
    tm = min(TP, S)
    nb = S // tm
    ng = len(POOL_WINDOWS)
    gd = D // ng
    n = tm + 24
    f4 = du.shape[1] // NQ
    assert nb % NQ == 0
    kch = nb // NQ
    kr = S // kch

    def body(dx1_ref, dxh_ref, x_ref, pl_ref, mod_ref, gm_ref, pw_ref, ps_ref, h2_ref, du_ref, a_ref, dz_ref,
             dx_ref, acc_ref, dpw_ref, dw1_ref, dw2_ref, ext, b1, b2, acc1, acc2):
        i = pl.program_id(0)

        @pl.when(i == 0)
        def _():
            acc_ref[...] = jnp.zeros_like(acc_ref)
            dpw_ref[...] = jnp.zeros_like(dpw_ref)

        @pl.when(i % kch == 0)
        def _():
            acc1[...] = jnp.zeros_like(acc1)
            acc2[...] = jnp.zeros_like(acc2)

        acc1[...] += _dot_tn(h2_ref[...], du_ref[...])
        av = a_ref[...]
        acc2[...] += _dot_tn(av * av, dz_ref[...])

        gt, psv = mod_ref[2:3, :], ps_ref[...]
        wvec = _window_vec(D)
        dx1v = dx1_ref[...]
        pooled = pl_ref[...]
        mixed = _heads_dot(pooled, pw_ref, gd)
        acc_ref[2:3, :] += jnp.sum(dx1v * (mixed * psv), axis=0, keepdims=True)
        dy = dx1v * gt
        acc_ref[3:4, :] += jnp.sum(dy * mixed, axis=0, keepdims=True)
        dmix = (dy * psv).astype(BF16)
        for gi in range(ng):
            sl = slice(gi * gd, (gi + 1) * gd)
            dpw_ref[gi] += _dot_tn(pooled[:, sl], dmix[:, sl])
        dpooled = _heads_dot(dmix, pw_ref, gd, nt=True)
        dmix_h = (dxh_ref[...] * gt * psv).astype(BF16)
        dpooled_h = jnp.where(i < nb - 1, _heads_dot(dmix_h, pw_ref, gd, nt=True), 0.0)
        t1 = (lax.broadcasted_iota(jnp.int32, (tm, 1), 0) + (i * tm + 1)).astype(F32)
        t1h = (lax.broadcasted_iota(jnp.int32, (16, 1), 0) + ((i + 1) * tm + 1)).astype(F32)
        zeros8 = jnp.zeros((8, D), F32)
        ext[pl.ds(0, tm), :] = dpooled / jnp.minimum(t1, wvec)
        ext[pl.ds(tm, 16), :] = dpooled_h / jnp.minimum(t1h, wvec)
        ext[pl.ds(tm + 16, 8), :] = zeros8
        b1[pl.ds(tm + 16, 8), :] = zeros8
        b2[pl.ds(tm + 16, 8), :] = zeros8
        m = n - 8
        b1[pl.ds(0, m), :] = ext[pl.ds(0, m), :] + ext[pl.ds(1, m), :]
        b2[pl.ds(0, m), gd:] = b1[pl.ds(0, m), gd:] + b1[pl.ds(2, m), gd:]
        b1[pl.ds(0, m), 2 * gd:] = b2[pl.ds(0, m), 2 * gd:] + b2[pl.ds(4, m), 2 * gd:]
        b2[pl.ds(0, m), 3 * gd:] = b1[pl.ds(0, m), 3 * gd:] + b1[pl.ds(8, m), 3 * gd:]
        wsum = jnp.concatenate([b1[pl.ds(0, tm), 0:gd], b2[pl.ds(0, tm), gd:2 * gd],
                                b1[pl.ds(0, tm), 2 * gd:3 * gd], b2[pl.ds(0, tm), 3 * gd:]], axis=1)
        dh = wsum - dpooled
        g, sc = gm_ref[...], mod_ref[1:2, :]
        _, xhat, r, nn = _rms_fwd(x_ref[...], g, sc, mod_ref[0:1, :])
        dx, dsh, dsc, dg = _rms_bwd(dh, xhat, r, nn, g, sc)
        acc_ref[0:1, :] += dsh
        acc_ref[1:2, :] += dsc
        acc_ref[6:7, :] += dg
        dx_ref[...] = dx1v + dx

        @pl.when(i % kch == kch - 1)
        def _():
            dw1_ref[...] = acc1[...].astype(BF16)
            dw2_ref[...] = acc2[...].astype(BF16)

    tile = pl.BlockSpec((tm, D), lambda i: (i, 0))
    halo = pl.BlockSpec((16, D), lambda i: (jnp.minimum((i + 1) * (tm // 16), S // 16 - 1), 0))
    row = pl.BlockSpec((1, D), lambda i: (0, 0))
    wblk = pl.BlockSpec((ng, gd, gd), lambda i: (0, 0, 0))
    full_k = pl.BlockSpec((kr, D), lambda i: (i % kch, 0))
    part_k = pl.BlockSpec((kr, f4), lambda i: (i % kch, i // kch))
    return pl.pallas_call(
        body, name="pool_bwd", grid=(nb,),
        out_shape=(jax.ShapeDtypeStruct((S, D), F32), jax.ShapeDtypeStruct((8, D), F32),
                   jax.ShapeDtypeStruct((ng, gd, gd), F32), jax.ShapeDtypeStruct((NQ, D, f4), BF16),
                   jax.ShapeDtypeStruct((NQ, f4, D), BF16)),
        in_specs=[tile, halo, tile, tile, pl.BlockSpec((8, D), lambda i: (0, 0)), row, wblk, row,
                  full_k, part_k, part_k, full_k],
        out_specs=(tile, pl.BlockSpec((8, D), lambda i: (0, 0)), wblk,
                   pl.BlockSpec((None, D, f4), lambda i: (i // kch, 0, 0)),
                   pl.BlockSpec((None, f4, D), lambda i: (i // kch, 0, 0))),
        scratch_shapes=[pltpu.VMEM((n, D), F32), pltpu.VMEM((n, D), F32), pltpu.VMEM((n, D), F32),
                        pltpu.VMEM((D, f4), F32), pltpu.VMEM((f4, D), F32)],
        compiler_params=_cparams("arbitrary"),
    )(dx1, dx1, x, pooled, mod_l, g_mix, pw, ps, h2, du, a, dz)


def _shard_to_rows(w, D):
    return w.reshape(-1, D)


def _blockdiag_full(gq, na, hd):
    return gq.reshape(NQ, na, HEADS, hd // NQ, hd).transpose(1, 2, 0, 3, 4).reshape(na, HEADS, hd, hd)


def _blockdiag_by_chip(dw, D):
    na, _, hd, _ = dw.shape
    return dw.reshape(na, HEADS, NQ, hd // NQ, hd).transpose(2, 0, 1, 3, 4).reshape(NQ, -1, D)


def kernel(x, c, w_mod, b_mod, norm_mix_g, norm_ffn_g, lru_w_y, lru_b_y, lru_w_in, lru_b_in, lru_conv_w, lru_conv_b, lru_w_a, lru_b_a, lru_w_x, lru_b_x, lru_lambda, lru_w_out, lru_b_out, pool_w, pool_scale, ffn_w1, ffn_w2, final_norm_g, loss_target, m_w_mod, m_b_mod, m_norm_mix_g, m_norm_ffn_g, m_lru_w_y, m_lru_b_y, m_lru_w_in, m_lru_b_in, m_lru_conv_w, m_lru_conv_b, m_lru_w_a, m_lru_b_a, m_lru_w_x, m_lru_b_x, m_lru_lambda, m_lru_w_out, m_lru_b_out, m_pool_w, m_pool_scale, m_ffn_w1, m_ffn_w2, m_final_norm_g, v_w_mod, v_b_mod, v_norm_mix_g, v_norm_ffn_g, v_lru_w_y, v_lru_b_y, v_lru_w_in, v_lru_b_in, v_lru_conv_w, v_lru_conv_b, v_lru_w_a, v_lru_b_a, v_lru_w_x, v_lru_b_x, v_lru_lambda, v_lru_w_out, v_lru_b_out, v_pool_w, v_pool_scale, v_ffn_w1, v_ffn_w2, v_final_norm_g):
    S, D = x.shape[1], x.shape[2]
    L = w_mod.shape[0]
    NA = lru_w_y.shape[0]
    NB = pool_w.shape[0]
    F = ffn_w1.shape[2] * NQ
    f4 = F // NQ
    hd = D // HEADS
    Cs = w_mod.shape[2]
    assert L == DEPTH and Cs * NQ == N_MOD * D and D % 1024 == 0
    x2d = x.reshape(S, D)
    tgt = loss_target.reshape(S, D)
    q = 2 * lax.axis_index("x") + lax.axis_index("y")

    big = [ffn_w1, ffn_w2, lru_w_y, lru_w_in, lru_w_out, lru_w_a, lru_w_x, pool_w]
    rows = [int(w.size) // D for w in big]
    offs = [sum(rows[:k]) for k in range(len(big))]
    O_W1, O_W2, O_WY, O_WIN, O_WOUT, O_WA, O_WX, O_PW = offs
    R = sum(rows)
    dq = D // NQ
    s_w1 = [ffn_w1[i].astype(BF16) for i in range(L)]
    s_w2 = [ffn_w2[i].astype(BF16) for i in range(L)]
    s_wy = [lru_w_y[j].astype(BF16) for j in range(NA)]
    s_win = [lru_w_in[j].astype(BF16) for j in range(NA)]
    s_wout = [lru_w_out[j].astype(BF16) for j in range(NA)]
    s_tiny = jnp.concatenate([_shard_to_rows(w, D) for w in (lru_w_a, lru_w_x, pool_w)], axis=0).astype(BF16)

    cshard = lru_conv_w.reshape(-1)
    small_fwd = jnp.concatenate([c.reshape(-1), cshard, lru_b_a.reshape(-1), lru_b_x.reshape(-1),
                                 pool_scale.reshape(-1)])
    small_fwd = jnp.pad(small_fwd, (0, 8 * D - small_fwd.shape[0])).reshape(8, D)

    g_w1, g_w2 = [None] * L, [None] * L
    g_wy, g_win, g_wout = [None] * NA, [None] * NA, [None] * NA
    SG, g_wy[0], g_win[0], g_wout[0], g_tiny = _comm_only("gather_first", small=small_fwd,
                                                         gathers=(s_wy[0], s_win[0], s_wout[0], s_tiny))
    wa_full = _blockdiag_full(g_tiny[:, :rows[5]], NA, hd)
    wx_full = _blockdiag_full(g_tiny[:, rows[5]:rows[5] + rows[6]], NA, hd)
    pw_full = _blockdiag_full(g_tiny[:, rows[5] + rows[6]:], NB, hd)
    SGf = SG.reshape(NDEV, 8 * D)
    c_all = SGf[:, :D]
    SGq = SGf.reshape(NQ, 2, 8 * D)[:, 0]
    o = D
    n_cw = NA * CONV_W * D // NQ
    conv_w_full = SGq[:, o:o + n_cw].reshape(NQ, NA, CONV_W, D // NQ).transpose(1, 2, 0, 3).reshape(NA, CONV_W, D)
    o += n_cw
    n_b = NA * HEADS * hd // NQ
    b_a_full = SGq[:, o:o + n_b].reshape(NQ, NA, HEADS, hd // NQ).transpose(1, 2, 0, 3).reshape(NA, 1, D)
    o += n_b
    b_x_full = SGq[:, o:o + n_b].reshape(NQ, NA, HEADS, hd // NQ).transpose(1, 2, 0, 3).reshape(NA, 1, D)
    o += n_b
    n_ps = NB * D // NQ
    pool_scale_full = SGq[:, o:o + n_ps].reshape(NQ, NB, D // NQ).transpose(1, 0, 2).reshape(NB, 1, D)


    b_mod_sh = lax.dynamic_slice_in_dim(b_mod, q * Cs, Cs, axis=1).reshape(L, 1, Cs)
    modpart = _mod_part(c_all, w_mod, b_mod_sh)
    modq = _exchange_mod(modpart.transpose(1, 0, 2))
    mod = modq.transpose(1, 0, 2).reshape(L, N_MOD, D)
    mod = jnp.pad(mod, ((0, 0), (0, 8 - N_MOD), (0, 0)))

    saved = []
    xcur = x2d
    for i in range(L):
        j = i // 2
        gm = norm_mix_g[i].reshape(1, D)
        gf = norm_ffn_g[i].reshape(1, D)
        if i % 2 == 0:
            h, gb, xr0, hs, p, y, x1, xc_s, gr_s, gi_s, a_sv, mu_s, g_w1[i], g_w2[i] = _lru_fwd(
                xcur, mod[i], gm, g_wy[j], g_win[j], lru_b_y[j].reshape(1, D), lru_b_in[j].reshape(1, D),
                conv_w_full[j], lru_conv_b[j].reshape(1, D), wa_full[j], b_a_full[j], wx_full[j], b_x_full[j],
                lru_lambda[j].reshape(1, D), g_wout[j], lru_b_out[j].reshape(1, D), gathers=(s_w1[i], s_w2[i]))
            h2, a, z, x2, g_w1[i + 1], g_w2[i + 1] = _ffn_fwd(x1, mod[i], gf, g_w1[i], g_w2[i],
                                                              gathers=(s_w1[i + 1], s_w2[i + 1]))
            saved.append(dict(x=xcur, h=h, gb=gb, xr0=xr0, hs=hs, p=p, y=y, x1=x1, h2=h2, a=a, z=z,
                              lru=(xc_s, gr_s, gi_s, a_sv, mu_s)))
        else:
            if j + 1 < NA:
                pooled, x1, h2, a, z, x2, g_wy[j + 1], g_win[j + 1], g_wout[j + 1] = _pool_mix_ffn_fwd(
                    xcur, mod[i], gm, pw_full[j], pool_scale_full[j], gf, g_w1[i], g_w2[i],
                    gathers=(s_wy[j + 1], s_win[j + 1], s_wout[j + 1]))
            else:
                pooled, x1, h2, a, z, x2 = _pool_mix_ffn_fwd(xcur, mod[i], gm, pw_full[j], pool_scale_full[j], gf,
                                                             g_w1[i], g_w2[i])
            saved.append(dict(x=xcur, pooled=pooled, x1=x1, h2=h2, a=a, z=z))
        xcur = x2

    dx = xcur
    qv = q.reshape(1).astype(jnp.int32)
    ppack = lax.empty((R, D), F32)
    psib = lax.empty((R, D), F32)
    pending, summed = [], []

    def comm_args():
        kw = {}
        if pending:
            kw["scatters"] = tuple(dw for dw, _ in pending)
        if summed:
            kw["sib"] = (ppack, psib, tuple(summed))
        return kw

    def after_host(extra):
        nonlocal ppack, psib, pending, summed
        had_sib = bool(summed)
        summed = []
        for (dw, off), rb in zip(pending, extra[:len(pending)]):
            ppack = _sum_into(ppack, dw, rb, off, qv)
            summed.append((off, dw.shape[1]))
        if had_sib:
            psib = extra[len(pending)]
        pending = []

    dmod_rows = [None] * L
    dg_mix = [None] * L
    dg_ffn = [None] * L
    d_small = {}
    dwa_l, dwx_l, dpw_l = [None] * NA, [None] * NA, [None] * NB
    for i in reversed(range(L)):
        j = i // 2
        sv = saved[i]
        gm = norm_mix_g[i].reshape(1, D)
        gf = norm_ffn_g[i].reshape(1, D)
        head = (final_norm_g.reshape(1, D), tgt) if i == L - 1 else None
        outs = _ffn_bwd(dx, sv["x1"], sv["a"], sv["z"], mod[i], gf, g_w1[i], g_w2[i], head=head, **comm_args())
        dx1, du, dz, facc = outs[:4]
        after_host(outs[4:])
        if head:
            loss = lax.psum(0.5 * jnp.sum(facc[0]) / D, ("x", "y", "c"))
            d_final_g = facc[1]
        if i % 2 == 0:
            pending.append((_dw_blocked(sv["h2"], du, False, False, "dw1"), O_W1 + i * D))
            pending.append((_dw_blocked(sv["a"], dz, True, True, "dw2"), O_W2 + i * f4))
            outs = _lru_bwd(dx1, sv["y"], sv["x"], sv["xr0"], sv["gb"], sv["hs"], *sv["lru"], mod[i], gm, g_wout[j],
                            g_wy[j], g_win[j], conv_w_full[j], wa_full[j], wx_full[j], lru_lambda[j].reshape(1, D),
                            **comm_args())
            dyp, dgb, dxr, dx, sm, dwa, dwx, macc = outs[:8]
            after_host(outs[8:])
            dwa_l[j], dwx_l[j] = dwa, dwx
            if i == 0:
                tiny = jnp.concatenate([_blockdiag_by_chip(jnp.stack(dwa_l), D), _blockdiag_by_chip(jnp.stack(dwx_l), D),
                                        _blockdiag_by_chip(jnp.stack(dpw_l), D)], axis=1).astype(BF16)
                pending.append((tiny, O_WA))
            if i == 0:
                outs = _dw_whole(sv["h"], [dgb, dxr], "dwy_dwin", **comm_args())
                after_host(outs[2:])
                pending.append((outs[0], O_WY + j * dq))
                pending.append((outs[1], O_WIN + j * dq))
                outs = _dw_whole(sv["p"], [dyp], "dwout", **comm_args())
                after_host(outs[1:])
                pending.append((outs[0], O_WOUT + j * dq))
            else:
                outs = _dw_whole(sv["p"], [dyp], "dwout", **comm_args())
                after_host(outs[1:])
                pending.append((outs[0], O_WOUT + j * dq))
                outs = _dw_whole(sv["h"], [dgb, dxr], "dwy_dwin", **comm_args())
                after_host(outs[2:])
                pending.append((outs[0], O_WY + j * dq))
                pending.append((outs[1], O_WIN + j * dq))
            d_small[("lru", j)] = (sm, macc[3])
            dgt_m = macc[2]
        else:
            dx, macc, dpw, dw1, dw2 = _pool_bwd(dx1, sv["x"], sv["pooled"], mod[i], gm, pw_full[j], pool_scale_full[j],
                                                sv["h2"], du, sv["a"], dz)
            pending.append((dw1, O_W1 + i * D))
            pending.append((dw2, O_W2 + i * f4))
            dpw_l[j] = dpw
            d_small[("pool", j)] = macc[3]
            dgt_m = macc[2]
        dmod_rows[i] = jnp.stack([macc[0], macc[1], dgt_m, facc[3], facc[4], facc[5]])
        dg_mix[i] = macc[6]
        dg_ffn[i] = facc[7]
    grad_x = dx.reshape(x.shape)

    lru_sm = [d_small[("lru", j)] for j in range(NA)]
    small_rows = [jnp.stack(dmod_rows).reshape(L * N_MOD, D), jnp.stack(dg_mix), jnp.stack(dg_ffn),
                  jnp.stack([s[0][9] for s in lru_sm]), jnp.stack([s[0][8] for s in lru_sm]),
                  jnp.stack([s[0][4] for s in lru_sm]), jnp.stack([s[0][7] for s in lru_sm]),
                  jnp.stack([s[1] for s in lru_sm]),
                  jnp.stack([s[0][0:CONV_W] for s in lru_sm]).reshape(NA * CONV_W, D),
                  jnp.stack([s[0][5] for s in lru_sm]), jnp.stack([s[0][6] for s in lru_sm]),
                  jnp.stack([d_small[("pool", j)] for j in range(NB)]), d_final_g.reshape(1, D)]
    small_g = jnp.concatenate(small_rows, axis=0)
    n_small = small_g.shape[0]
    assert n_small <= SMALL_ROWS
    small_g = jnp.pad(small_g, ((0, SMALL_ROWS - n_small), (0, 0)))

    outs = _comm_only("scatter_last", small=small_g, reduce_small=True, **comm_args())
    sg_all, sg_sum = outs[:2]
    after_host(outs[2:])
    psum_mine = ppack
    psum_sib = _comm_only("sibling_last", sib=(ppack, psib, tuple(summed)))[0]

    def big_update(w, m, v, off, name):
        shp = w.shape
        g, dl, m2, v2 = _adam_rows(w.reshape(-1, D), m.reshape(-1, D), v.reshape(-1, D), psum_mine, psum_sib, off, name)
        return g.reshape(shp), dl.reshape(shp), m2.reshape(shp), v2.reshape(shp)

    res = {}
    res["ffn_w1"] = big_update(ffn_w1, m_ffn_w1, v_ffn_w1, O_W1, "adam_w1")
    res["ffn_w2"] = big_update(ffn_w2, m_ffn_w2, v_ffn_w2, O_W2, "adam_w2")
    res["lru_w_y"] = big_update(lru_w_y, m_lru_w_y, v_lru_w_y, O_WY, "adam_wy")
    res["lru_w_in"] = big_update(lru_w_in, m_lru_w_in, v_lru_w_in, O_WIN, "adam_win")
    res["lru_w_out"] = big_update(lru_w_out, m_lru_w_out, v_lru_w_out, O_WOUT, "adam_wout")

    def tiny_parts(w, off):
        n = int(w.size) // D
        return psum_mine[off:off + n].reshape(w.shape), psum_sib[off:off + n].reshape(w.shape)

    tiny_items = [("lru_w_a", lru_w_a, m_lru_w_a, v_lru_w_a) + tiny_parts(lru_w_a, O_WA),
                  ("lru_w_x", lru_w_x, m_lru_w_x, v_lru_w_x) + tiny_parts(lru_w_x, O_WX),
                  ("pool_w", pool_w, m_pool_w, v_pool_w) + tiny_parts(pool_w, O_PW)]

    dmod_all = sg_all[:, :L * N_MOD, :].reshape(NDEV, L, N_MOD * D)
    dmod_sh = lax.dynamic_slice_in_dim(dmod_all, q * Cs, Cs, axis=2).transpose(1, 0, 2)
    res["w_mod"] = _wmod_update(c_all.T, dmod_sh, w_mod, m_w_mod, v_w_mod)

    r0 = 0
    by_rows, names_a = [], []
    for name, w, m, v in (("b_mod", b_mod, m_b_mod, v_b_mod), ("norm_mix_g", norm_mix_g, m_norm_mix_g, v_norm_mix_g),
                          ("norm_ffn_g", norm_ffn_g, m_norm_ffn_g, v_norm_ffn_g),
                          ("lru_b_y", lru_b_y, m_lru_b_y, v_lru_b_y), ("lru_b_in", lru_b_in, m_lru_b_in, v_lru_b_in),
                          ("lru_conv_b", lru_conv_b, m_lru_conv_b, v_lru_conv_b),
                          ("lru_lambda", lru_lambda, m_lru_lambda, v_lru_lambda),
                          ("lru_b_out", lru_b_out, m_lru_b_out, v_lru_b_out)):
        by_rows.append((w, m, v, r0))
        names_a.append(name)
        r0 += int(w.size) // D
    g_conv_w = lax.dynamic_slice_in_dim(sg_sum[r0:r0 + NA * CONV_W].reshape(NA, CONV_W, D), q * dq, dq, axis=2)
    r0 += NA * CONV_W
    g_b_a = lax.dynamic_slice_in_dim(sg_sum[r0:r0 + NA].reshape(NA, HEADS, hd), q * (hd // NQ), hd // NQ, axis=2)
    r0 += NA
    g_b_x = lax.dynamic_slice_in_dim(sg_sum[r0:r0 + NA].reshape(NA, HEADS, hd), q * (hd // NQ), hd // NQ, axis=2)
    r0 += NA
    g_ps = lax.dynamic_slice_in_dim(sg_sum[r0:r0 + NB], q * dq, dq, axis=1)
    r0 += NB
    by_rows.append((final_norm_g.reshape(1, D), m_final_norm_g.reshape(1, D), v_final_norm_g.reshape(1, D), r0))
    names_a.append("final_norm_g")
    sliced = [(lru_conv_w, m_lru_conv_w, v_lru_conv_w, g_conv_w), (lru_b_a, m_lru_b_a, v_lru_b_a, g_b_a),
              (lru_b_x, m_lru_b_x, v_lru_b_x, g_b_x), (pool_scale, m_pool_scale, v_pool_scale, g_ps)]
    res_a, res_b, res_c = _adam_small(sg_sum, by_rows, sliced, [t[1:] for t in tiny_items])
    for name, r in zip(names_a, res_a):
        res[name] = r
    for t, r in zip(tiny_items, res_c):
        res[t[0]] = r
    res["final_norm_g"] = tuple(a.reshape(D) for a in res["final_norm_g"])
    for name, (_, _, _, g), r in zip(("lru_conv_w", "lru_b_a", "lru_b_x", "pool_scale"), sliced, res_b):
        res[name] = (g,) + r

    order = ["w_mod", "b_mod", "norm_mix_g", "norm_ffn_g", "lru_w_y", "lru_b_y", "lru_w_in", "lru_b_in", "lru_conv_w",
             "lru_conv_b", "lru_w_a", "lru_b_a", "lru_w_x", "lru_b_x", "lru_lambda", "lru_w_out", "lru_b_out", "pool_w",
             "pool_scale", "ffn_w1", "ffn_w2", "final_norm_g"]
    return (loss, grad_x, *[res[n][0] for n in order], *[res[n][1] for n in order],
            *[res[n][2] for n in order], *[res[n][3] for n in order])
```

```python
import jax
import jax.numpy as jnp
from jax import lax
from jax.experimental import pallas as pl
from jax.experimental.pallas import tpu as pltpu

F32 = jnp.float32
BF16 = jnp.bfloat16
MESH = pl.DeviceIdType.MESH

NQ = 4
NDEV = 8
DEPTH = 4
N_MOD = 6
HEADS = 4
CONV_W = 4
POOL_WINDOWS = (2, 4, 8, 16)
LRU_C = 8.0
EPS = 1e-6
ADAM_LR, ADAM_B1, ADAM_B2, ADAM_EPS, ADAM_WD, ADAM_STEP = 0.001, 0.9, 0.999, 1e-08, 0.01, 10

TM = 512
TT = 256
TP = 256
TPF = 512
TK = 2048
SMALL_ROWS = 64
FORWARD_STEPS = 4
VMEM_LIMIT = 60 * 1024 * 1024


def _cparams(*sem):
    return pltpu.CompilerParams(dimension_semantics=tuple(sem), vmem_limit_bytes=VMEM_LIMIT)


def _dot(a, b):
    return jnp.dot(a, b, preferred_element_type=F32)


def _dot_nt(a, b):
    return lax.dot_general(a, b, (((1,), (1,)), ((), ())), preferred_element_type=F32)


def _dot_tn(a, b):
    return lax.dot_general(a, b, (((0,), (0,)), ((), ())), preferred_element_type=F32)


def _resident(shape, index_map):
    return pl.BlockSpec(shape, index_map, pipeline_mode=pl.Buffered(1))


def _rms_fwd(x, g, sc, sh):
    r = lax.rsqrt(jnp.mean(x * x, axis=-1, keepdims=True) + EPS)
    xhat = x * r
    n = xhat * g
    return n * (1.0 + sc) + sh, xhat, r, n


def _rms_bwd(dh, xhat, r, n, g, sc):
    dsh = jnp.sum(dh, axis=0, keepdims=True)
    dsc = jnp.sum(dh * n, axis=0, keepdims=True)
    dn = dh * (1.0 + sc)
    dg = jnp.sum(dn * xhat, axis=0, keepdims=True)
    dxh = dn * g
    dx = r * (dxh - xhat * jnp.mean(dxh * xhat, axis=-1, keepdims=True))
    return dx, dsh, dsc, dg


_GELU_K = 0.7978845608028654
_GELU_C = 0.044715


def _gelu(x):
    t = jnp.tanh(_GELU_K * (x + _GELU_C * x * x * x))
    return 0.5 * x * (1.0 + t), t


def _gelu_grad(x, t):
    return 0.5 * (1.0 + t) + 0.5 * x * (1.0 - t * t) * (_GELU_K * (1.0 + 3.0 * _GELU_C * x * x))


def _neg_expm1(y, exp_y):
    series = -(y * (1.0 + y * (0.5 + y * (1.0 / 6.0))))
    return jnp.where(y > -(1.0 / 64.0), series, 1.0 - exp_y)


def _zero_first(ref):
    @pl.when(pl.program_id(0) == 0)
    def _():
        ref[...] = jnp.zeros_like(ref)


def _my_pos():
    return lax.axis_index("x"), lax.axis_index("y"), lax.axis_index("c")


def _dev_index(x, y, c):
    return 4 * x + 2 * y + c


def _chip_peers(x, y):
    return [(1 - x, y), (x, 1 - y), (1 - x, 1 - y)]


def _all_peers(x, y, c):
    return [(px, py, c) for (px, py) in _chip_peers(x, y)] + [(x, y, 1 - c)] + \
           [(px, py, 1 - c) for (px, py) in _chip_peers(x, y)]


def _comm_run(phase, x, y, c, gathers, scatters, sib, send, recv, loc):
    q = 2 * x + y
    peers = _chip_peers(x, y)
    sibling = (x, y, 1 - c)

    def rcopy(src, dst, s, dev):
        return pltpu.make_async_remote_copy(src, dst, send.at[s], recv.at[s], device_id=dev, device_id_type=MESH)

    s = 0
    for gi, (src, dst) in enumerate(gathers):
        half = src.shape[0] // 2
        mine, other = pl.ds(c * half, half), pl.ds((1 - c) * half, half)
        own = pltpu.make_async_copy(src, dst.at[q], loc.at[gi])
        if phase == "start":
            own.start()
        elif phase == "finish":
            own.wait()
        for (px, py) in peers:
            pq = 2 * px + py
            s_ici, s_fwd = s, s + 1
            s += 2
            if phase == "start":
                rcopy(src.at[mine], dst.at[q].at[mine], s_ici, (px, py, c)).start()
            elif phase == "forward":
                rcopy(src.at[mine], dst.at[pq].at[mine], s_ici, (px, py, c)).wait_recv()
                rcopy(dst.at[pq].at[mine], dst.at[pq].at[mine], s_fwd, sibling).start()
            else:
                rcopy(dst.at[pq].at[other], dst.at[pq].at[other], s_fwd, sibling).wait_recv()
                rcopy(src.at[mine], dst.at[q].at[mine], s_ici, (px, py, c)).wait_send()
                rcopy(dst.at[pq].at[mine], dst.at[pq].at[mine], s_fwd, sibling).wait_send()
    direct = []
    for (src, dst) in scatters:
        for k, (px, py) in enumerate(peers):
            direct.append((src.at[2 * px + py], dst.at[k], (px, py, c)))
    if sib is not None:
        src, dst, ranges = sib
        for (off, rows) in ranges:
            direct.append((src.at[pl.ds(off, rows)], dst.at[pl.ds(off, rows)], sibling))
    if phase == "start":
        for k, (a, b, dev) in enumerate(direct):
            rcopy(a, b, s + k, dev).start()
    elif phase == "finish":
        for k, (a, b, dev) in enumerate(direct):
            rcopy(a, b, s + k, dev).wait_recv()
        for k, (a, b, dev) in enumerate(direct):
            rcopy(a, b, s + k, dev).wait_send()


def _comm_shapes(gathers, scatters, sib):
    assert all(g.shape[0] % 32 == 0 for g in gathers)
    cin = list(gathers) + list(scatters) + ([sib[0], sib[1]] if sib else [])
    cout = [jax.ShapeDtypeStruct((NQ,) + g.shape, g.dtype) for g in gathers] + \
           [jax.ShapeDtypeStruct((3,) + s.shape[1:], s.dtype) for s in scatters] + \
           ([jax.ShapeDtypeStruct(sib[1].shape, sib[1].dtype)] if sib else [])
    n_rem = 6 * len(gathers) + 3 * len(scatters) + (len(sib[2]) if sib else 0)
    sems = [pltpu.SemaphoreType.DMA((max(n_rem, 1),)), pltpu.SemaphoreType.DMA((max(n_rem, 1),)),
            pltpu.SemaphoreType.DMA((max(len(gathers), 1),))]
    return cin, cout, sems


def _pcall(body, *, name, grid, in_specs, out_specs, out_shape, operands, scratch_shapes=(),
           gathers=(), scatters=(), sib=None):
    assert len(grid) == 1
    out_shape, out_specs = tuple(out_shape), tuple(out_specs)
    if not (gathers or scatters or sib):
        return pl.pallas_call(body, name=name, grid=grid, in_specs=list(in_specs), out_specs=out_specs,
                              out_shape=out_shape, scratch_shapes=list(scratch_shapes),
                              compiler_params=_cparams("arbitrary"))(*operands)
    cin, cout, sems = _comm_shapes(gathers, scatters, sib)
    n_in, n_cin, n_out, n_cout, n_scr = len(operands), len(cin), len(out_shape), len(cout), len(scratch_shapes)
    ng, ns = len(gathers), len(scatters)
    nsteps = grid[0]

    def wrapped(*refs):
        ins = refs[:n_in]
        cins = refs[n_in:n_in + n_cin]
        o0 = n_in + n_cin
        outs = refs[o0:o0 + n_out]
        couts = refs[o0 + n_out:o0 + n_out + n_cout]
        s0 = o0 + n_out + n_cout
        scr = refs[s0:s0 + n_scr]
        send, recv, loc = refs[s0 + n_scr:s0 + n_scr + 3]
        x, y, c = _my_pos()

        def run(phase):
            g = [(cins[k], couts[k]) for k in range(ng)]
            sc = [(cins[ng + k], couts[ng + k]) for k in range(ns)]
            sb = (cins[ng + ns], couts[ng + ns], sib[2]) if sib else None
            _comm_run(phase, x, y, c, g, sc, sb, send, recv, loc)

        @pl.when(pl.program_id(0) == 0)
        def _():
            run("start")

        if ng:
            @pl.when(pl.program_id(0) == max(nsteps - FORWARD_STEPS, 0))
            def _():
                run("forward")

        body(*ins, *outs, *scr)

        @pl.when(pl.program_id(0) == nsteps - 1)
        def _():
            run("finish")

    anyspec = pl.BlockSpec(memory_space=pl.ANY)
    aliases = {n_in + ng + ns + 1: n_out + ng + ns} if sib else {}
    return pl.pallas_call(
        wrapped, name=name, grid=grid,
        in_specs=list(in_specs) + [anyspec] * n_cin, out_specs=out_specs + (anyspec,) * n_cout,
        out_shape=out_shape + tuple(cout), scratch_shapes=list(scratch_shapes) + sems,
        input_output_aliases=aliases,
        compiler_params=pltpu.CompilerParams(dimension_semantics=("arbitrary",), vmem_limit_bytes=VMEM_LIMIT,
                                             has_side_effects=True),
    )(*operands, *cin)


def _comm_only(name, small=None, reduce_small=False, gathers=(), scatters=(), sib=None):
    cin, cout, sems = _comm_shapes(gathers, scatters, sib)
    n_cin, n_cout = len(cin), len(cout)
    ng, ns = len(gathers), len(scatters)
    n_sm_in = 1 if small is not None else 0
    n_sm_out = (2 if reduce_small else 1) if small is not None else 0

    def body(*refs):
        sm_in = refs[:n_sm_in]
        cins = refs[n_sm_in:n_sm_in + n_cin]
        o0 = n_sm_in + n_cin
        sm_out = refs[o0:o0 + n_sm_out]
        couts = refs[o0 + n_sm_out:o0 + n_sm_out + n_cout]
        s0 = o0 + n_sm_out + n_cout
        send, recv, loc = refs[s0:s0 + 3]
        x, y, c = _my_pos()
        g = [(cins[k], couts[k]) for k in range(ng)]
        sc = [(cins[ng + k], couts[ng + k]) for k in range(ns)]
        sb = (cins[ng + ns], couts[ng + ns], sib[2]) if sib else None
        _comm_run("start", x, y, c, g, sc, sb, send, recv, loc)
        if small is not None:
            sm_send, sm_recv = refs[s0 + 3:s0 + 5]
            small_ref, sg_ref = sm_in[0], sm_out[0]
            me = _dev_index(x, y, c)
            sg_ref[me] = small_ref[...]
            peers = _all_peers(x, y, c)
            sm = [pltpu.make_async_remote_copy(small_ref, sg_ref.at[me], sm_send.at[k], sm_recv.at[k],
                                               device_id=peer, device_id_type=MESH) for k, peer in enumerate(peers)]
            for cp in sm:
                cp.start()
            for k, (px, py, pc) in enumerate(peers):
                pltpu.make_async_remote_copy(small_ref, sg_ref.at[_dev_index(px, py, pc)], sm_send.at[k], sm_recv.at[k],
                                             device_id=(px, py, pc), device_id_type=MESH).wait_recv()
            if reduce_small:
                acc = sg_ref[0]
                for d in range(1, NDEV):
                    acc = acc + sg_ref[d]
                sm_out[1][...] = acc
            for cp in sm:
                cp.wait_send()
        if ng:
            _comm_run("forward", x, y, c, g, sc, sb, send, recv, loc)
        _comm_run("finish", x, y, c, g, sc, sb, send, recv, loc)

    anyspec = pl.BlockSpec(memory_space=pl.ANY)
    vspec = pl.BlockSpec(memory_space=pltpu.VMEM)
    sm_shapes = []
    if small is not None:
        sm_shapes.append(jax.ShapeDtypeStruct((NDEV,) + small.shape, small.dtype))
        if reduce_small:
            sm_shapes.append(jax.ShapeDtypeStruct(small.shape, small.dtype))
        sems = sems + [pltpu.SemaphoreType.DMA((NDEV - 1,)), pltpu.SemaphoreType.DMA((NDEV - 1,))]
    aliases = {n_sm_in + ng + ns + 1: n_sm_out + ng + ns} if sib else {}
    return pl.pallas_call(
        body, name=name,
        in_specs=[vspec] * n_sm_in + [anyspec] * n_cin,
        out_specs=tuple([vspec] * n_sm_out + [anyspec] * n_cout),
        out_shape=tuple(sm_shapes + cout), scratch_shapes=sems, input_output_aliases=aliases,
        compiler_params=pltpu.CompilerParams(has_side_effects=True),
    )(*([small] if small is not None else []), *cin)


def _exchange_mod(modpart):
    _, L, Cs = modpart.shape

    def body(part_ref, out_ref, send, recv):
        x, y, c = _my_pos()
        q = 2 * x + y
        me = _dev_index(x, y, c)
        out_ref[q] = part_ref[me]
        sends = []
        for k, (px, py) in enumerate(_chip_peers(x, y)):
            cp = pltpu.make_async_remote_copy(part_ref.at[_dev_index(px, py, c)], out_ref.at[q], send.at[k], recv.at[k],
                                              device_id=(px, py, c), device_id_type=MESH)
            cp.start()
            sends.append(cp)
        for k, (px, py) in enumerate(_chip_peers(x, y)):
            pltpu.make_async_remote_copy(part_ref.at[me], out_ref.at[2 * px + py], send.at[k], recv.at[k],
                                         device_id=(px, py, c), device_id_type=MESH).wait_recv()
        for cp in sends:
            cp.wait_send()

    return pl.pallas_call(
        body, name="exchange_mod",
        out_shape=jax.ShapeDtypeStruct((NQ, L, Cs), modpart.dtype),
        in_specs=[pl.BlockSpec(memory_space=pltpu.VMEM)],
        out_specs=pl.BlockSpec(memory_space=pltpu.VMEM),
        scratch_shapes=[pltpu.SemaphoreType.DMA((3,)), pltpu.SemaphoreType.DMA((3,))],
        compiler_params=pltpu.CompilerParams(has_side_effects=True),
    )(modpart)


def _mod_part(c_all, w_mod, b_mod_sh):
    L, D, Cs = w_mod.shape
    tn = 512 if Cs % 512 == 0 else Cs

    def body(c_ref, w_ref, b_ref, o_ref):
        cv = c_ref[...]
        cond = cv * jax.nn.sigmoid(cv)
        o_ref[...] = jnp.dot(cond, w_ref[...], preferred_element_type=F32, precision=lax.Precision.HIGHEST) + b_ref[...]

    return pl.pallas_call(
        body, name="mod_part", grid=(L, Cs // tn),
        out_shape=jax.ShapeDtypeStruct((L, NDEV, Cs), F32),
        in_specs=[pl.BlockSpec((NDEV, D), lambda i, j: (0, 0)),
                  pl.BlockSpec((None, D, tn), lambda i, j: (i, 0, j)),
                  pl.BlockSpec((None, 1, tn), lambda i, j: (i, 0, j))],
        out_specs=pl.BlockSpec((None, NDEV, tn), lambda i, j: (i, 0, j)),
        compiler_params=_cparams("parallel", "parallel"),
    )(c_all, w_mod, b_mod_sh)


def _adam(w, g, m, v):
    m2 = ADAM_B1 * m + (1.0 - ADAM_B1) * g
    v2 = ADAM_B2 * v + (1.0 - ADAM_B2) * (g * g)
    m_hat = m2 / (1.0 - ADAM_B1 ** ADAM_STEP)
    v_hat = v2 / (1.0 - ADAM_B2 ** ADAM_STEP)
    delta = -ADAM_LR * (m_hat / (jnp.sqrt(v_hat) + ADAM_EPS) + ADAM_WD * w)
    return delta, m2, v2


def _wmod_update(c_all_t, dmod_sh, w, m, v):
    L, D, Cs = w.shape
    td = 512 if D % 512 == 0 else D

    def body(ct_ref, d_ref, w_ref, m_ref, v_ref, g_ref, dl_ref, m2_ref, v2_ref):
        cv = ct_ref[...]
        cond = cv * jax.nn.sigmoid(cv)
        g = cond[:, 0:1] * d_ref[0:1, :]
        for b in range(1, NDEV):
            g = g + cond[:, b:b + 1] * d_ref[b:b + 1, :]
        g_ref[...] = g
        dl_ref[...], m2_ref[...], v2_ref[...] = _adam(w_ref[...], g, m_ref[...], v_ref[...])

    blk = pl.BlockSpec((None, td, Cs), lambda i, j: (i, j, 0))
    out = jax.ShapeDtypeStruct((L, D, Cs), F32)
    return pl.pallas_call(
        body, name="wmod_update", grid=(L, D // td),
        out_shape=(out, out, out, out),
        in_specs=[pl.BlockSpec((td, NDEV), lambda i, j: (j, 0)),
                  pl.BlockSpec((None, NDEV, Cs), lambda i, j: (i, 0, 0)), blk, blk, blk],
        out_specs=(blk, blk, blk, blk),
        compiler_params=_cparams("parallel", "parallel"),
    )(c_all_t, dmod_sh, w, m, v)


def _adam_rows(w, m, v, pa, pb, row_off, name):
    rows, C = w.shape
    tr = 512 if rows % 512 == 0 else (128 if rows % 128 == 0 else rows)
    assert row_off % tr == 0
    ob = row_off // tr

    def body(w_ref, m_ref, v_ref, pa_ref, pb_ref, g_ref, dl_ref, m2_ref, v2_ref):
        g = pa_ref[...] + pb_ref[...]
        g_ref[...] = g
        dl_ref[...], m2_ref[...], v2_ref[...] = _adam(w_ref[...], g, m_ref[...], v_ref[...])

    blk = pl.BlockSpec((tr, C), lambda i: (i, 0))
    pblk = pl.BlockSpec((tr, C), lambda i: (ob + i, 0))
    out = jax.ShapeDtypeStruct((rows, C), F32)
    return pl.pallas_call(
        body, name=name, grid=(rows // tr,), out_shape=(out, out, out, out),
        in_specs=[blk, blk, blk, pblk, pblk], out_specs=(blk, blk, blk, blk),
        compiler_params=_cparams("parallel"),
    )(w, m, v, pa, pb)


def _adam_small(sg_sum, by_rows, sliced, pairs):
    D = sg_sum.shape[1]
    na, nb, nc = len(by_rows), len(sliced), len(pairs)

    def body(*refs):
        sg = refs[0]
        ins_a = [refs[1 + 3 * t:4 + 3 * t] for t in range(na)]
        p = 1 + 3 * na
        ins_b = [refs[p + 4 * t:p + 4 * t + 4] for t in range(nb)]
        p += 4 * nb
        ins_c = [refs[p + 5 * t:p + 5 * t + 5] for t in range(nc)]
        p += 5 * nc
        outs_a = [refs[p + 4 * t:p + 4 * t + 4] for t in range(na)]
        p += 4 * na
        outs_b = [refs[p + 3 * t:p + 3 * t + 3] for t in range(nb)]
        p += 3 * nb
        outs_c = [refs[p + 4 * t:p + 4 * t + 4] for t in range(nc)]
        for (w_ref, m_ref, v_ref, ga_ref, gb_ref), (g_ref, dl_ref, m2_ref, v2_ref) in zip(ins_c, outs_c):
            g = ga_ref[...] + gb_ref[...]
            g_ref[...] = g
            dl_ref[...], m2_ref[...], v2_ref[...] = _adam(w_ref[...], g, m_ref[...], v_ref[...])
        for (w_ref, m_ref, v_ref), (g_ref, dl_ref, m2_ref, v2_ref), (w, _, _, row0) in zip(ins_a, outs_a, by_rows):
            n, k = w.shape[0], w.shape[1] // D
            pieces = [(slice(0, n), slice(0, D), slice(row0, row0 + n))] if k == 1 else \
                     [(slice(i, i + 1), slice(kk * D, (kk + 1) * D), slice(row0 + i * k + kk, row0 + i * k + kk + 1))
                      for i in range(n) for kk in range(k)]
            for rs, cs, gs in pieces:
                g = sg[gs, :]
                g_ref[rs, cs] = g
                dl_ref[rs, cs], m2_ref[rs, cs], v2_ref[rs, cs] = _adam(w_ref[rs, cs], g, m_ref[rs, cs], v_ref[rs, cs])
        for (w_ref, m_ref, v_ref, g_ref), (dl_ref, m2_ref, v2_ref) in zip(ins_b, outs_b):
            dl_ref[...], m2_ref[...], v2_ref[...] = _adam(w_ref[...], g_ref[...], m_ref[...], v_ref[...])

    operands = [sg_sum] + [a for t in by_rows for a in t[:3]] + [a for t in sliced for a in t] + \
               [a for t in pairs for a in t]
    out_shape = [jax.ShapeDtypeStruct(t[0].shape, F32) for t in by_rows for _ in range(4)] + \
                [jax.ShapeDtypeStruct(t[0].shape, F32) for t in sliced for _ in range(3)] + \
                [jax.ShapeDtypeStruct(t[0].shape, F32) for t in pairs for _ in range(4)]
    outs = pl.pallas_call(body, name="adam_small", out_shape=tuple(out_shape))(*operands)
    res_a = [tuple(outs[4 * t:4 * t + 4]) for t in range(na)]
    o = 4 * na
    res_b = [tuple(outs[o + 3 * t:o + 3 * t + 3]) for t in range(nb)]
    o += 3 * nb
    res_c = [tuple(outs[o + 4 * t:o + 4 * t + 4]) for t in range(nc)]
    return res_a, res_b, res_c


def _sum_into(ppack, dw, rb, off, qv):
    _, rows, D = dw.shape
    tr = 256 if rows % 256 == 0 else 128
    assert rows % tr == 0 and off % tr == 0
    ob = off // tr

    def body(q_ref, o_ref, r_ref, pin_ref, p_ref):
        acc = o_ref[...].astype(F32)
        for k in range(3):
            acc = acc + r_ref[k].astype(F32)
        p_ref[...] = acc

    return pl.pallas_call(
        body, name="sum_partials", out_shape=jax.ShapeDtypeStruct(ppack.shape, ppack.dtype),
        grid_spec=pltpu.PrefetchScalarGridSpec(
            num_scalar_prefetch=1, grid=(rows // tr,),
            in_specs=[pl.BlockSpec((None, tr, D), lambda i, q_ref: (q_ref[0], i, 0)),
                      pl.BlockSpec((3, tr, D), lambda i, q_ref: (0, i, 0)),
                      pl.BlockSpec(memory_space=pl.ANY)],
            out_specs=pl.BlockSpec((tr, D), lambda i, q_ref: (ob + i, 0))),
        input_output_aliases={3: 0},
        compiler_params=_cparams("parallel"),
    )(qv, dw, rb, ppack)


def _wspec(g):
    return _resident(g.shape, lambda i: (0, 0, 0))


def _ffn_fwd_inner(x1, mod_ref, gf_ref, w1_ref, w2_ref, h2_ref, a_ref, z_ref, x2_ref):
    h2 = _rms_fwd(x1, gf_ref[...], mod_ref[4:5, :], mod_ref[3:4, :])[0]
    h2b = h2.astype(BF16)
    h2_ref[...] = h2b
    f4 = w1_ref.shape[2]
    z = jnp.zeros(x1.shape, F32)
    for q in range(NQ):
        a = jnp.maximum(_dot(h2b, w1_ref[q]), 0.0)
        a_ref[:, q * f4:(q + 1) * f4] = a.astype(BF16)
        z = z + _dot((a * a).astype(BF16), w2_ref[q])
    z_ref[...] = z.astype(BF16)
    x2_ref[...] = x1 + mod_ref[5:6, :] * z


def _sigmoid(x):
    return jax.nn.sigmoid(x)


def _heads_dot(xb, w_ref, hd, nt=False):
    outs = []
    for h in range(HEADS):
        xs = xb[:, h * hd:(h + 1) * hd]
        outs.append(_dot_nt(xs, w_ref[h]) if nt else _dot(xs, w_ref[h]))
    return jnp.concatenate(outs, axis=1)


def _lru_gates(xc, wa_ref, ba, wx_ref, bx, lam, hd):
    xcb = xc.astype(BF16)
    gate_r = _sigmoid(_heads_dot(xcb, wa_ref, hd) + ba)
    gate_i = _sigmoid(_heads_dot(xcb, wx_ref, hd) + bx)
    ls = jax.nn.log_sigmoid(lam)
    log_a = gate_r * (LRU_C * ls)
    a = jnp.exp(log_a)
    mult = jnp.sqrt(_neg_expm1(2.0 * log_a, a * a))
    return xcb, gate_r, gate_i, ls, a, mult


def _conv_taps(xext, cw, tt):
    acc = cw[0:1, :] * xext[pl.ds(8 - (CONV_W - 1), tt), :]
    for k in range(1, CONV_W):
        acc = acc + cw[k:k + 1, :] * xext[pl.ds(8 - (CONV_W - 1) + k, tt), :]
    return acc


def _lru_fwd(x, mod_l, g_mix, g_wy, g_win, b_y, b_in, cw, cb, wa, ba, wx, bx, lam, g_wout, b_out, **comm):
    S, W = x.shape
    tt = min(TT, S)
    hd = W // HEADS

    def body(x_ref, mod_ref, g_ref, wy_ref, win_ref, by_ref, bin_ref, cw_ref, cb_ref, wa_ref, ba_ref, wx_ref, bx_ref,
             lam_ref, wo_ref, bo_ref, h_ref, gb_ref, xr_ref, hs_ref, p_ref, y_ref, x1_ref,
             xc_ref, gr_ref, gi_ref, a_s, mu_ref, xext, u_s, carry):
        i = pl.program_id(0)

        @pl.when(i == 0)
        def _():
            carry[...] = jnp.zeros_like(carry)
            xext[0:8, :] = jnp.zeros((8, W), F32)

        @pl.when(i > 0)
        def _():
            xext[0:8, :] = xext[pl.ds(tt, 8), :]

        xv = x_ref[...]
        hb = _rms_fwd(xv, g_ref[...], mod_ref[1:2, :], mod_ref[0:1, :])[0].astype(BF16)
        h_ref[...] = hb
        gbv = _dot(hb, wy_ref[...].reshape(W, W)) + by_ref[...]
        gb_ref[...] = gbv
        xr = _dot(hb, win_ref[...].reshape(W, W)) + bin_ref[...]
        xr_ref[...] = xr
        xext[pl.ds(8, tt), :] = xr
        xc = _conv_taps(xext, cw_ref[...], tt) + cb_ref[...]
        _, gate_r, gate_i, _, a, mult = _lru_gates(xc, wa_ref, ba_ref[...], wx_ref, bx_ref[...], lam_ref[...], hd)
        xc_ref[...] = xc
        gr_ref[...] = gate_r
        gi_ref[...] = gate_i
        mu_ref[...] = mult
        a_s[...] = a
        u_s[...] = mult * (gate_i * xc)
        row = lax.broadcasted_iota(jnp.int32, (8, W), 0)

        def step(k, _):
            off = pl.multiple_of(k * 8, 8)
            A = a_s[pl.ds(off, 8), :]
            U = u_s[pl.ds(off, 8), :]
            for d in (1, 2, 4):
                keep = row >= d
                Us = jnp.where(keep, pltpu.roll(U, d, 0), 0.0)
                As = jnp.where(keep, pltpu.roll(A, d, 0), 1.0)
                U = U + A * Us
                A = A * As
            H = U + A * carry[...]
            hs_ref[pl.ds(off, 8), :] = H
            carry[...] = jnp.broadcast_to(H[7:8, :], (8, W))
            return 0

        lax.fori_loop(0, tt // 8, step, 0)
        pb = (hs_ref[...] * _gelu(gbv)[0]).astype(BF16)
        p_ref[...] = pb
        y = _dot(pb, wo_ref[...].reshape(W, W)) + bo_ref[...]
        y_ref[...] = y.astype(BF16)
        x1_ref[...] = xv + mod_ref[2:3, :] * y

    tile = pl.BlockSpec((tt, W), lambda i: (i, 0))
    row = pl.BlockSpec((1, W), lambda i: (0, 0))
    wblk = pl.BlockSpec((HEADS, hd, hd), lambda i: (0, 0, 0))
    f32o, bf16o = jax.ShapeDtypeStruct((S, W), F32), jax.ShapeDtypeStruct((S, W), BF16)
    return _pcall(
        body, name="lru_fwd", grid=(S // tt,),
        out_shape=(bf16o, f32o, f32o, f32o, bf16o, bf16o, f32o, f32o, f32o, f32o, f32o, f32o),
        in_specs=[tile, pl.BlockSpec((8, W), lambda i: (0, 0)), row, _wspec(g_wy), _wspec(g_win), row, row,
                  pl.BlockSpec((CONV_W, W), lambda i: (0, 0)), row, wblk, row, wblk, row, row, _wspec(g_wout), row],
        out_specs=(tile,) * 12,
        scratch_shapes=[pltpu.VMEM((tt + 8, W), F32), pltpu.VMEM((tt, W), F32), pltpu.VMEM((8, W), F32)],
        operands=(x, mod_l, g_mix, g_wy, g_win, b_y, b_in, cw, cb, wa, ba, wx, bx, lam, g_wout, b_out), **comm)


def _ffn_out_shapes(S, D, F):
    return (jax.ShapeDtypeStruct((S, D), BF16), jax.ShapeDtypeStruct((S, F), BF16),
            jax.ShapeDtypeStruct((S, D), BF16), jax.ShapeDtypeStruct((S, D), F32))


def _ffn_fwd(x1, mod_l, g_ffn, g_w1, g_w2, **comm):
    S, D = x1.shape
    tm = min(TM, S)
    F = g_w1.shape[2] * NQ

    def body(x1_ref, mod_ref, gf_ref, w1_ref, w2_ref, h2_ref, a_ref, z_ref, x2_ref):
        _ffn_fwd_inner(x1_ref[...], mod_ref, gf_ref, w1_ref, w2_ref, h2_ref, a_ref, z_ref, x2_ref)

    tile = pl.BlockSpec((tm, D), lambda i: (i, 0))
    row = pl.BlockSpec((1, D), lambda i: (0, 0))
    return _pcall(
        body, name="ffn_fwd", grid=(S // tm,),
        out_shape=_ffn_out_shapes(S, D, F),
        in_specs=[tile, pl.BlockSpec((8, D), lambda i: (0, 0)), row, _wspec(g_w1), _wspec(g_w2)],
        out_specs=(tile, pl.BlockSpec((tm, F), lambda i: (i, 0)), tile, tile),
        operands=(x1, mod_l, g_ffn, g_w1, g_w2), **comm)


def _window_vec(D):
    gd = D // len(POOL_WINDOWS)
    lane = lax.broadcasted_iota(jnp.int32, (1, D), 1)
    w = jnp.full((1, D), float(POOL_WINDOWS[0]), F32)
    for g in range(1, len(POOL_WINDOWS)):
        w = jnp.where(lane >= g * gd, float(POOL_WINDOWS[g]), w)
    return w


def _pool_mix_ffn_fwd(x, mod_l, g_mix, pw, ps, g_ffn, g_w1, g_w2, **comm):
    S, D = x.shape
    tm = min(TPF, S)
    F = g_w1.shape[2] * NQ
    gd = D // len(POOL_WINDOWS)
    n = tm + 24

    def body(x_ref, xh_ref, mod_ref, gm_ref, pw_ref, ps_ref, gf_ref, w1_ref, w2_ref,
             pl_ref, x1_ref, h2_ref, a_ref, z_ref, x2_ref, ext, b1, b2):
        i = pl.program_id(0)
        g, sc, sh = gm_ref[...], mod_ref[1:2, :], mod_ref[0:1, :]
        xv = x_ref[...]
        h = _rms_fwd(xv, g, sc, sh)[0]
        hh = _rms_fwd(xh_ref[...], g, sc, sh)[0]
        zeros8 = jnp.zeros((8, D), F32)
        ext[0:8, :] = zeros8
        b1[0:8, :] = zeros8
        b2[0:8, :] = zeros8
        ext[8:24, :] = jnp.where(i > 0, hh, 0.0)
        ext[pl.ds(24, tm), :] = h
        m = n - 8
        b1[pl.ds(8, m), :] = ext[pl.ds(8, m), :] + ext[pl.ds(7, m), :]
        b2[pl.ds(8, m), gd:] = b1[pl.ds(8, m), gd:] + b1[pl.ds(6, m), gd:]
        b1[pl.ds(8, m), 2 * gd:] = b2[pl.ds(8, m), 2 * gd:] + b2[pl.ds(4, m), 2 * gd:]
        b2[pl.ds(8, m), 3 * gd:] = b1[pl.ds(8, m), 3 * gd:] + b1[pl.ds(0, m), 3 * gd:]
        wsum = jnp.concatenate([b1[pl.ds(24, tm), 0:gd], b2[pl.ds(24, tm), gd:2 * gd],
                                b1[pl.ds(24, tm), 2 * gd:3 * gd], b2[pl.ds(24, tm), 3 * gd:]], axis=1)
        t1 = (lax.broadcasted_iota(jnp.int32, (tm, 1), 0) + (i * tm + 1)).astype(F32)
        cnt = jnp.minimum(t1, _window_vec(D))
        pooled = (wsum / cnt - h).astype(BF16)
        pl_ref[...] = pooled
        y = _heads_dot(pooled, pw_ref, gd) * ps_ref[...]
        x1 = xv + mod_ref[2:3, :] * y
        x1_ref[...] = x1
        _ffn_fwd_inner(x1, mod_ref, gf_ref, w1_ref, w2_ref, h2_ref, a_ref, z_ref, x2_ref)

    tile = pl.BlockSpec((tm, D), lambda i: (i, 0))
    halo = pl.BlockSpec((16, D), lambda i: (jnp.maximum(i * (tm // 16) - 1, 0), 0))
    row = pl.BlockSpec((1, D), lambda i: (0, 0))
    return _pcall(
        body, name="pool_mix_ffn_fwd", grid=(S // tm,),
        out_shape=(jax.ShapeDtypeStruct((S, D), BF16), jax.ShapeDtypeStruct((S, D), F32)) + _ffn_out_shapes(S, D, F),
        in_specs=[tile, halo, pl.BlockSpec((8, D), lambda i: (0, 0)), row,
                  pl.BlockSpec((len(POOL_WINDOWS), gd, gd), lambda i: (0, 0, 0)), row, row,
                  _wspec(g_w1), _wspec(g_w2)],
        out_specs=(tile, tile, tile, pl.BlockSpec((tm, F), lambda i: (i, 0)), tile, tile),
        scratch_shapes=[pltpu.VMEM((n, D), F32), pltpu.VMEM((n, D), F32), pltpu.VMEM((n, D), F32)],
        operands=(x, x, mod_l, g_mix, pw, ps, g_ffn, g_w1, g_w2), **comm)


def _loss_head(xv, gv, tv, acc_ref):
    D = xv.shape[1]
    r = lax.rsqrt(jnp.mean(xv * xv, axis=-1, keepdims=True) + EPS)
    xhat = xv * r
    err = xhat * gv - tv
    acc_ref[0:1, :] += jnp.sum(err * err, axis=0, keepdims=True)
    dy = err * (1.0 / D)
    acc_ref[1:2, :] += jnp.sum(dy * xhat, axis=0, keepdims=True)
    dxh = dy * gv
    return r * (dxh - xhat * jnp.mean(dxh * xhat, axis=-1, keepdims=True))


def _ffn_bwd(dx2, x1, a, z, mod_l, g_ffn, g_w1, g_w2, head=None, **comm):
    S, D = dx2.shape
    F = a.shape[1]
    f4 = F // NQ
    tm = min(TM, S)
    nh = 2 if head else 0

    def body(*refs):
        dx2_ref, x1_ref, a_ref, z_ref, mod_ref, gf_ref, w1_ref, w2_ref = refs[:8]
        dx1_ref, du_ref, dz_ref, acc_ref = refs[8 + nh:]
        _zero_first(acc_ref)
        dx2v = dx2_ref[...]
        if head:
            dx2v = _loss_head(dx2v, refs[8][...], refs[9][...], acc_ref)
        acc_ref[5:6, :] +=jnp.sum(dx2v * z_ref[...].astype(F32), axis=0, keepdims=True)
        dzb = (dx2v * mod_ref[5:6, :]).astype(BF16)
        dz_ref[...] = dzb
        dh2 = jnp.zeros((tm, D), F32)
        for q in range(NQ):
            av = a_ref[:, q * f4:(q + 1) * f4].astype(F32)
            du = (_dot_nt(dzb, w2_ref[q]) * (2.0 * av)).astype(BF16)
            du_ref[:, q * f4:(q + 1) * f4] = du
            dh2 = dh2 + _dot_nt(du, w1_ref[q])
        g, sc = gf_ref[...], mod_ref[4:5, :]
        _, xhat, r, n = _rms_fwd(x1_ref[...], g, sc, mod_ref[3:4, :])
        dx, dsh, dsc, dg = _rms_bwd(dh2, xhat, r, n, g, sc)
        acc_ref[3:4, :] += dsh
        acc_ref[4:5, :] += dsc
        acc_ref[7:8, :] += dg
        dx1_ref[...] = dx2v + dx

    tile = pl.BlockSpec((tm, D), lambda i: (i, 0))
    wide = pl.BlockSpec((tm, F), lambda i: (i, 0))
    return _pcall(
        body, name="ffn_bwd", grid=(S // tm,),
        out_shape=(jax.ShapeDtypeStruct((S, D), F32), jax.ShapeDtypeStruct((S, F), BF16),
                   jax.ShapeDtypeStruct((S, D), BF16), jax.ShapeDtypeStruct((8, D), F32)),
        in_specs=[tile, tile, wide, tile, pl.BlockSpec((8, D), lambda i: (0, 0)), pl.BlockSpec((1, D), lambda i: (0, 0)),
                  _wspec(g_w1), _wspec(g_w2)] + ([pl.BlockSpec((1, D), lambda i: (0, 0)), tile] if head else []),
        out_specs=(tile, wide, tile, pl.BlockSpec((8, D), lambda i: (0, 0))),
        operands=(dx2, x1, a, z, mod_l, g_ffn, g_w1, g_w2) + (tuple(head) if head else ()), **comm)


def _dw_blocked(a, b, by_rows, square_a, name):
    S = a.shape[0]
    tk = min(TK, S)
    nk = S // tk
    if by_rows:
        bm, bn = a.shape[1] // NQ, b.shape[1]
        a_map, b_map = (lambda q, k: (k, q)), (lambda q, k: (k, 0))
    else:
        bm, bn = a.shape[1], b.shape[1] // NQ
        a_map, b_map = (lambda q, k: (k, 0)), (lambda q, k: (k, q))

    def body(a_ref, b_ref, o_ref, acc):
        k = pl.program_id(1)

        @pl.when(k == 0)
        def _():
            acc[...] = jnp.zeros_like(acc)

        av = a_ref[...]
        if square_a:
            av = av * av
        acc[...] += _dot_tn(av, b_ref[...])

        @pl.when(k == nk - 1)
        def _():
            o_ref[...] = acc[...].astype(o_ref.dtype)

    return pl.pallas_call(
        body, name=name, grid=(NQ, nk),
        out_shape=jax.ShapeDtypeStruct((NQ, bm, bn), BF16),
        in_specs=[pl.BlockSpec((tk, bm), a_map), pl.BlockSpec((tk, bn), b_map)],
        out_specs=pl.BlockSpec((None, bm, bn), lambda q, k: (q, 0, 0)),
        scratch_shapes=[pltpu.VMEM((bm, bn), F32)],
        compiler_params=_cparams("parallel", "arbitrary"),
    )(a, b)


def _dw_whole(a, bs, name, **comm):
    S, M = a.shape
    N = bs[0].shape[1]
    tk = min(TK, S)
    nk = S // tk
    nb = len(bs)

    def body(*refs):
        a_ref, b_refs, o_refs, accs = refs[0], refs[1:1 + nb], refs[1 + nb:1 + 2 * nb], refs[1 + 2 * nb:]
        k = pl.program_id(0)

        @pl.when(k == 0)
        def _():
            for acc in accs:
                acc[...] = jnp.zeros_like(acc)

        av = a_ref[...]
        for b_ref, acc in zip(b_refs, accs):
            acc[...] += _dot_tn(av, b_ref[...])

        @pl.when(k == nk - 1)
        def _():
            for o_ref, acc in zip(o_refs, accs):
                o_ref[...] = acc[...].reshape(NQ, M // NQ, N).astype(o_ref.dtype)

    return _pcall(
        body, name=name, grid=(nk,),
        out_shape=tuple(jax.ShapeDtypeStruct((NQ, M // NQ, N), BF16) for _ in bs),
        in_specs=[pl.BlockSpec((tk, M), lambda k: (k, 0))] + [pl.BlockSpec((tk, N), lambda k: (k, 0)) for _ in bs],
        out_specs=tuple(pl.BlockSpec((NQ, M // NQ, N), lambda k: (0, 0, 0)) for _ in bs),
        scratch_shapes=[pltpu.VMEM((M, N), F32) for _ in bs],
        operands=(a, *bs), **comm)


def _lru_bwd(dx1, y, x, xr0, gb, hs, xc_, gate_r_, gate_i_, a_, mult_, mod_l, g_mix, g_wout, g_wy, g_win, cw, wa, wx,
             lam, **comm):
    S, W = xr0.shape
    tt = min(TT, S)
    nb = S // tt
    hd = W // HEADS

    def body(dx1_ref, y_ref, x_ref, xr_ref, gb_ref, hs_ref, hsh_ref, xc_ref, gr_ref, gi_ref, a_s, mu_ref,
             mod_ref, gm_ref, wo_ref, wy_ref, win_ref, cw_ref, wa_ref, wx_ref, lam_ref,
             dy_ref, dgb_ref, dxr_ref, dx_ref, sm_ref, dwa_ref, dwx_ref, acc_ref,
             hext, qext, dext, b_s, qc, dc):
        i = pl.program_id(0)
        blk = nb - 1 - i

        @pl.when(i == 0)
        def _():
            sm_ref[...] = jnp.zeros_like(sm_ref)
            dwa_ref[...] = jnp.zeros_like(dwa_ref)
            dwx_ref[...] = jnp.zeros_like(dwx_ref)
            acc_ref[...] = jnp.zeros_like(acc_ref)
            qc[...] = jnp.zeros_like(qc)
            dc[...] = jnp.zeros_like(dc)

        dx1v = dx1_ref[...]
        acc_ref[2:3, :] += jnp.sum(dx1v * y_ref[...].astype(F32), axis=0, keepdims=True)
        dy = dx1v * mod_ref[2:3, :]
        acc_ref[3:4, :] += jnp.sum(dy, axis=0, keepdims=True)
        dyb = dy.astype(BF16)
        dy_ref[...] = dyb
        dpv = _dot_nt(dyb, wo_ref[...].reshape(W, W))

        hext[0:8, :] = jnp.where(blk > 0, hsh_ref[...], 0.0)
        hext[pl.ds(8, tt), :] = hs_ref[...]
        cw = cw_ref[...]
        lam = lam_ref[...]
        xc, gate_r, gate_i, a, mult = xc_ref[...], gr_ref[...], gi_ref[...], a_s[...], mu_ref[...]
        xcb = xc.astype(BF16)
        ls = jax.nn.log_sigmoid(lam)

        gbv = gb_ref[...]
        gate, th = _gelu(gbv)
        dgb = dpv * hs_ref[...] * _gelu_grad(gbv, th)
        dgbb = dgb.astype(BF16)
        dgb_ref[...] = dgbb
        sm_ref[9:10, :] += jnp.sum(dgb, axis=0, keepdims=True)
        dhs = dpv * gate

        b_s[...] = a * dhs
        qext[pl.ds(tt, 8), :] = qc[...]
        row = lax.broadcasted_iota(jnp.int32, (8, W), 0)

        def step(k, _):
            off = pl.multiple_of((tt // 8 - 1 - k) * 8, 8)
            A = a_s[pl.ds(off, 8), :]
            B = b_s[pl.ds(off, 8), :]
            for d in (1, 2, 4):
                keep = row < 8 - d
                Bs = jnp.where(keep, pltpu.roll(B, 8 - d, 0), 0.0)
                As = jnp.where(keep, pltpu.roll(A, 8 - d, 0), 1.0)
                B = B + A * Bs
                A = A * As
            Q = B + A * qc[...]
            qext[pl.ds(off, 8), :] = Q
            qc[...] = jnp.broadcast_to(Q[0:1, :], (8, W))
            return 0

        lax.fori_loop(0, tt // 8, step, 0)
        gsc = dhs + qext[pl.ds(1, tt), :]
        da = gsc * hext[pl.ds(7, tt), :]
        t1 = gsc * xc
        dmult = t1 * gate_i
        dgate_i = t1 * mult
        dxc = gsc * (mult * gate_i)
        dlog_a = da * a - dmult * (a * a) / mult
        dgate_r = dlog_a * (LRU_C * ls)
        sm_ref[7:8, :] += jnp.sum(dlog_a * (LRU_C * gate_r), axis=0, keepdims=True)
        dga = dgate_r * gate_r * (1.0 - gate_r)
        dgx = dgate_i * gate_i * (1.0 - gate_i)
        sm_ref[5:6, :] += jnp.sum(dga, axis=0, keepdims=True)
        sm_ref[6:7, :] += jnp.sum(dgx, axis=0, keepdims=True)
        dgab = dga.astype(BF16)
        dgxb = dgx.astype(BF16)
        dxc = dxc + _heads_dot(dgab, wa_ref, hd, nt=True) + _heads_dot(dgxb, wx_ref, hd, nt=True)
        for h in range(HEADS):
            sl = slice(h * hd, (h + 1) * hd)
            dwa_ref[h] += _dot_tn(xcb[:, sl], dgab[:, sl])
            dwx_ref[h] += _dot_tn(xcb[:, sl], dgxb[:, sl])
        sm_ref[4:5, :] += jnp.sum(dxc, axis=0, keepdims=True)
        dext[pl.ds(0, tt), :] = dxc
        dext[pl.ds(tt, 8), :] = dc[...]
        xrv = xr_ref[...]
        dxr = None
        for k in range(CONV_W):
            up = dext[pl.ds(CONV_W - 1 - k, tt), :]
            sm_ref[k:k + 1, :] += jnp.sum(up * xrv, axis=0, keepdims=True)
            dxr = cw[k:k + 1, :] * up if dxr is None else dxr + cw[k:k + 1, :] * up
        dc[...] = dext[0:8, :]
        sm_ref[8:9, :] += jnp.sum(dxr, axis=0, keepdims=True)
        dxrb = dxr.astype(BF16)
        dxr_ref[...] = dxrb

        dh = _dot_nt(dxrb, win_ref[...].reshape(W, W)) + _dot_nt(dgbb, wy_ref[...].reshape(W, W))
        g, sc = gm_ref[...], mod_ref[1:2, :]
        _, xhat, r, n = _rms_fwd(x_ref[...], g, sc, mod_ref[0:1, :])
        dx, dsh, dsc, dg = _rms_bwd(dh, xhat, r, n, g, sc)
        acc_ref[0:1, :] += dsh
        acc_ref[1:2, :] += dsc
        acc_ref[6:7, :] += dg
        dx_ref[...] = dx1v + dx

        @pl.when(i == nb - 1)
        def _():
            sm_ref[7:8, :] = sm_ref[7:8, :] * jax.nn.sigmoid(-lam)

    rev = lambda i: (nb - 1 - i, 0)
    tile = pl.BlockSpec((tt, W), rev)
    halo = pl.BlockSpec((8, W), lambda i: (jnp.maximum((nb - 1 - i) * (tt // 8) - 1, 0), 0))
    row = pl.BlockSpec((1, W), lambda i: (0, 0))
    wblk = pl.BlockSpec((HEADS, hd, hd), lambda i: (0, 0, 0))
    bf16o = jax.ShapeDtypeStruct((S, W), BF16)
    return _pcall(
        body, name="lru_bwd", grid=(nb,),
        out_shape=(bf16o, bf16o, bf16o, jax.ShapeDtypeStruct((S, W), F32),
                   jax.ShapeDtypeStruct((16, W), F32), jax.ShapeDtypeStruct((HEADS, hd, hd), F32),
                   jax.ShapeDtypeStruct((HEADS, hd, hd), F32), jax.ShapeDtypeStruct((8, W), F32)),
        in_specs=[tile, tile, tile, tile, tile, tile, halo, tile, tile, tile, tile, tile,
                  pl.BlockSpec((8, W), lambda i: (0, 0)), row,
                  _wspec(g_wout), _wspec(g_wy), _wspec(g_win), pl.BlockSpec((CONV_W, W), lambda i: (0, 0)),
                  wblk, wblk, row],
        out_specs=(tile, tile, tile, tile, pl.BlockSpec((16, W), lambda i: (0, 0)), wblk, wblk,
                   pl.BlockSpec((8, W), lambda i: (0, 0))),
        scratch_shapes=[pltpu.VMEM((tt + 8, W), F32), pltpu.VMEM((tt + 8, W), F32), pltpu.VMEM((tt + 8, W), F32),
                        pltpu.VMEM((tt, W), F32), pltpu.VMEM((8, W), F32), pltpu.VMEM((8, W), F32)],
        operands=(dx1, y, x, xr0, gb, hs, hs, xc_, gate_r_, gate_i_, a_, mult_, mod_l, g_mix, g_wout, g_wy, g_win,
                  cw, wa, wx, lam),
        **comm)


def _pool_bwd(dx1, x, pooled, mod_l, g_mix, pw, ps, h2, du, a, dz):
    S, D = x.shape
    tm = min(TP, S)
    nb = S // tm
    ng = len(POOL_WINDOWS)
    gd = D // ng
    n = tm + 24
    f4 = du.shape[1] // NQ
    assert nb % NQ == 0
    kch = nb // NQ
    kr = S // kch

    def body(dx1_ref, dxh_ref, x_ref, pl_ref, mod_ref, gm_ref, pw_ref, ps_ref, h2_ref, du_ref, a_ref, dz_ref,
             dx_ref, acc_ref, dpw_ref, dw1_ref, dw2_ref, ext, b1, b2, acc1, acc2):
        i = pl.program_id(0)

        @pl.when(i == 0)
        def _():
            acc_ref[...] = jnp.zeros_like(acc_ref)
            dpw_ref[...] = jnp.zeros_like(dpw_ref)

        @pl.when(i % kch == 0)
        def _():
            acc1[...] = jnp.zeros_like(acc1)
            acc2[...] = jnp.zeros_like(acc2)

        acc1[...] += _dot_tn(h2_ref[...], du_ref[...])
        av = a_ref[...]
        acc2[...] += _dot_tn(av * av, dz_ref[...])

        gt, psv = mod_ref[2:3, :], ps_ref[...]
        wvec = _window_vec(D)
        dx1v = dx1_ref[...]
        pooled = pl_ref[...]
        mixed = _heads_dot(pooled, pw_ref, gd)
        acc_ref[2:3, :] += jnp.sum(dx1v * (mixed * psv), axis=0, keepdims=True)
        dy = dx1v * gt
        acc_ref[3:4, :] += jnp.sum(dy * mixed, axis=0, keepdims=True)
        dmix = (dy * psv).astype(BF16)
        for gi in range(ng):
            sl = slice(gi * gd, (gi + 1) * gd)
            dpw_ref[gi] += _dot_tn(pooled[:, sl], dmix[:, sl])
        dpooled = _heads_dot(dmix, pw_ref, gd, nt=True)
        dmix_h = (dxh_ref[...] * gt * psv).astype(BF16)
        dpooled_h = jnp.where(i < nb - 1, _heads_dot(dmix_h, pw_ref, gd, nt=True), 0.0)
        t1 = (lax.broadcasted_iota(jnp.int32, (tm, 1), 0) + (i * tm + 1)).astype(F32)
        t1h = (lax.broadcasted_iota(jnp.int32, (16, 1), 0) + ((i + 1) * tm + 1)).astype(F32)
        zeros8 = jnp.zeros((8, D), F32)
        ext[pl.ds(0, tm), :] = dpooled / jnp.minimum(t1, wvec)
        ext[pl.ds(tm, 16), :] = dpooled_h / jnp.minimum(t1h, wvec)
        ext[pl.ds(tm + 16, 8), :] = zeros8
        b1[pl.ds(tm + 16, 8), :] = zeros8
        b2[pl.ds(tm + 16, 8), :] = zeros8
        m = n - 8
        b1[pl.ds(0, m), :] = ext[pl.ds(0, m), :] + ext[pl.ds(1, m), :]
        b2[pl.ds(0, m), gd:] = b1[pl.ds(0, m), gd:] + b1[pl.ds(2, m), gd:]
        b1[pl.ds(0, m), 2 * gd:] = b2[pl.ds(0, m), 2 * gd:] + b2[pl.ds(4, m), 2 * gd:]
        b2[pl.ds(0, m), 3 * gd:] = b1[pl.ds(0, m), 3 * gd:] + b1[pl.ds(8, m), 3 * gd:]
        wsum = jnp.concatenate([b1[pl.ds(0, tm), 0:gd], b2[pl.ds(0, tm), gd:2 * gd],
                                b1[pl.ds(0, tm), 2 * gd:3 * gd], b2[pl.ds(0, tm), 3 * gd:]], axis=1)
        dh = wsum - dpooled
        g, sc = gm_ref[...], mod_ref[1:2, :]
        _, xhat, r, nn = _rms_fwd(x_ref[...], g, sc, mod_ref[0:1, :])
        dx, dsh, dsc, dg = _rms_bwd(dh, xhat, r, nn, g, sc)
        acc_ref[0:1, :] += dsh
        acc_ref[1:2, :] += dsc
        acc_ref[6:7, :] += dg
        dx_ref[...] = dx1v + dx

        @pl.when(i % kch == kch - 1)
        def _():
            dw1_ref[...] = acc1[...].astype(BF16)
            dw2_ref[...] = acc2[...].astype(BF16)

    tile = pl.BlockSpec((tm, D), lambda i: (i, 0))
    halo = pl.BlockSpec((16, D), lambda i: (jnp.minimum((i + 1) * (tm // 16), S // 16 - 1), 0))
    row = pl.BlockSpec((1, D), lambda i: (0, 0))
    wblk = pl.BlockSpec((ng, gd, gd), lambda i: (0, 0, 0))
    full_k = pl.BlockSpec((kr, D), lambda i: (i % kch, 0))
    part_k = pl.BlockSpec((kr, f4), lambda i: (i % kch, i // kch))
    return pl.pallas_call(
        body, name="pool_bwd", grid=(nb,),
        out_shape=(jax.ShapeDtypeStruct((S, D), F32), jax.ShapeDtypeStruct((8, D), F32),
                   jax.ShapeDtypeStruct((ng, gd, gd), F32), jax.ShapeDtypeStruct((NQ, D, f4), BF16),
                   jax.ShapeDtypeStruct((NQ, f4, D), BF16)),
        in_specs=[tile, halo, tile, tile, pl.BlockSpec((8, D), lambda i: (0, 0)), row, wblk, row,
                  full_k, part_k, part_k, full_k],
        out_specs=(tile, pl.BlockSpec((8, D), lambda i: (0, 0)), wblk,
                   pl.BlockSpec((None, D, f4), lambda i: (i // kch, 0, 0)),
                   pl.BlockSpec((None, f4, D), lambda i: (i // kch, 0, 0))),
        scratch_shapes=[pltpu.VMEM((n, D), F32), pltpu.VMEM((n, D), F32), pltpu.VMEM((n, D), F32),
                        pltpu.VMEM((D, f4), F32), pltpu.VMEM((f4, D), F32)],
        compiler_params=_cparams("arbitrary"),
    )(dx1, dx1, x, pooled, mod_l, g_mix, pw, ps, h2, du, a, dz)


def _shard_to_rows(w, D):
    return w.reshape(-1, D)


def _blockdiag_full(gq, na, hd):
    return gq.reshape(NQ, na, HEADS, hd // NQ, hd).transpose(1, 2, 0, 3, 4).reshape(na, HEADS, hd, hd)


def _blockdiag_by_chip(dw, D):
    na, _, hd, _ = dw.shape
    return dw.reshape(na, HEADS, NQ, hd // NQ, hd).transpose(2, 0, 1, 3, 4).reshape(NQ, -1, D)


def kernel(x, c, w_mod, b_mod, norm_mix_g, norm_ffn_g, lru_w_y, lru_b_y, lru_w_in, lru_b_in, lru_conv_w, lru_conv_b, lru_w_a, lru_b_a, lru_w_x, lru_b_x, lru_lambda, lru_w_out, lru_b_out, pool_w, pool_scale, ffn_w1, ffn_w2, final_norm_g, loss_target, m_w_mod, m_b_mod, m_norm_mix_g, m_norm_ffn_g, m_lru_w_y, m_lru_b_y, m_lru_w_in, m_lru_b_in, m_lru_conv_w, m_lru_conv_b, m_lru_w_a, m_lru_b_a, m_lru_w_x, m_lru_b_x, m_lru_lambda, m_lru_w_out, m_lru_b_out, m_pool_w, m_pool_scale, m_ffn_w1, m_ffn_w2, m_final_norm_g, v_w_mod, v_b_mod, v_norm_mix_g, v_norm_ffn_g, v_lru_w_y, v_lru_b_y, v_lru_w_in, v_lru_b_in, v_lru_conv_w, v_lru_conv_b, v_lru_w_a, v_lru_b_a, v_lru_w_x, v_lru_b_x, v_lru_lambda, v_lru_w_out, v_lru_b_out, v_pool_w, v_pool_scale, v_ffn_w1, v_ffn_w2, v_final_norm_g):
    S, D = x.shape[1], x.shape[2]
    L = w_mod.shape[0]
    NA = lru_w_y.shape[0]
    NB = pool_w.shape[0]
    F = ffn_w1.shape[2] * NQ
    f4 = F // NQ
    hd = D // HEADS
    Cs = w_mod.shape[2]
    assert L == DEPTH and Cs * NQ == N_MOD * D and D % 1024 == 0
    x2d = x.reshape(S, D)
    tgt = loss_target.reshape(S, D)
    q = 2 * lax.axis_index("x") + lax.axis_index("y")

    big = [ffn_w1, ffn_w2, lru_w_y, lru_w_in, lru_w_out, lru_w_a, lru_w_x, pool_w]
    rows = [int(w.size) // D for w in big]
    offs = [sum(rows[:k]) for k in range(len(big))]
    O_W1, O_W2, O_WY, O_WIN, O_WOUT, O_WA, O_WX, O_PW = offs
    R = sum(rows)
    dq = D // NQ
    s_w1 = [ffn_w1[i].astype(BF16) for i in range(L)]
    s_w2 = [ffn_w2[i].astype(BF16) for i in range(L)]
    s_wy = [lru_w_y[j].astype(BF16) for j in range(NA)]
    s_win = [lru_w_in[j].astype(BF16) for j in range(NA)]
    s_wout = [lru_w_out[j].astype(BF16) for j in range(NA)]
    s_tiny = jnp.concatenate([_shard_to_rows(w, D) for w in (lru_w_a, lru_w_x, pool_w)], axis=0).astype(BF16)

    cshard = lru_conv_w.reshape(-1)
    small_fwd = jnp.concatenate([c.reshape(-1), cshard, lru_b_a.reshape(-1), lru_b_x.reshape(-1),
                                 pool_scale.reshape(-1)])
    small_fwd = jnp.pad(small_fwd, (0, 8 * D - small_fwd.shape[0])).reshape(8, D)

    g_w1, g_w2 = [None] * L, [None] * L
    g_wy, g_win, g_wout = [None] * NA, [None] * NA, [None] * NA
    SG, g_wy[0], g_win[0], g_wout[0], g_tiny = _comm_only("gather_first", small=small_fwd,
                                                         gathers=(s_wy[0], s_win[0], s_wout[0], s_tiny))
    wa_full = _blockdiag_full(g_tiny[:, :rows[5]], NA, hd)
    wx_full = _blockdiag_full(g_tiny[:, rows[5]:rows[5] + rows[6]], NA, hd)
    pw_full = _blockdiag_full(g_tiny[:, rows[5] + rows[6]:], NB, hd)
    SGf = SG.reshape(NDEV, 8 * D)
    c_all = SGf[:, :D]
    SGq = SGf.reshape(NQ, 2, 8 * D)[:, 0]
    o = D
    n_cw = NA * CONV_W * D // NQ
    conv_w_full = SGq[:, o:o + n_cw].reshape(NQ, NA, CONV_W, D // NQ).transpose(1, 2, 0, 3).reshape(NA, CONV_W, D)
    o += n_cw
    n_b = NA * HEADS * hd // NQ
    b_a_full = SGq[:, o:o + n_b].reshape(NQ, NA, HEADS, hd // NQ).transpose(1, 2, 0, 3).reshape(NA, 1, D)
    o += n_b
    b_x_full = SGq[:, o:o + n_b].reshape(NQ, NA, HEADS, hd // NQ).transpose(1, 2, 0, 3).reshape(NA, 1, D)
    o += n_b
    n_ps = NB * D // NQ
    pool_scale_full = SGq[:, o:o + n_ps].reshape(NQ, NB, D // NQ).transpose(1, 0, 2).reshape(NB, 1, D)


    b_mod_sh = lax.dynamic_slice_in_dim(b_mod, q * Cs, Cs, axis=1).reshape(L, 1, Cs)
    modpart = _mod_part(c_all, w_mod, b_mod_sh)
    modq = _exchange_mod(modpart.transpose(1, 0, 2))
    mod = modq.transpose(1, 0, 2).reshape(L, N_MOD, D)
    mod = jnp.pad(mod, ((0, 0), (0, 8 - N_MOD), (0, 0)))

    saved = []
    xcur = x2d
    for i in range(L):
        j = i // 2
        gm = norm_mix_g[i].reshape(1, D)
        gf = norm_ffn_g[i].reshape(1, D)
        if i % 2 == 0:
            h, gb, xr0, hs, p, y, x1, xc_s, gr_s, gi_s, a_sv, mu_s, g_w1[i], g_w2[i] = _lru_fwd(
                xcur, mod[i], gm, g_wy[j], g_win[j], lru_b_y[j].reshape(1, D), lru_b_in[j].reshape(1, D),
                conv_w_full[j], lru_conv_b[j].reshape(1, D), wa_full[j], b_a_full[j], wx_full[j], b_x_full[j],
                lru_lambda[j].reshape(1, D), g_wout[j], lru_b_out[j].reshape(1, D), gathers=(s_w1[i], s_w2[i]))
            h2, a, z, x2, g_w1[i + 1], g_w2[i + 1] = _ffn_fwd(x1, mod[i], gf, g_w1[i], g_w2[i],
                                                              gathers=(s_w1[i + 1], s_w2[i + 1]))
            saved.append(dict(x=xcur, h=h, gb=gb, xr0=xr0, hs=hs, p=p, y=y, x1=x1, h2=h2, a=a, z=z,
                              lru=(xc_s, gr_s, gi_s, a_sv, mu_s)))
        else:
            if j + 1 < NA:
                pooled, x1, h2, a, z, x2, g_wy[j + 1], g_win[j + 1], g_wout[j + 1] = _pool_mix_ffn_fwd(
                    xcur, mod[i], gm, pw_full[j], pool_scale_full[j], gf, g_w1[i], g_w2[i],
                    gathers=(s_wy[j + 1], s_win[j + 1], s_wout[j + 1]))
            else:
                pooled, x1, h2, a, z, x2 = _pool_mix_ffn_fwd(xcur, mod[i], gm, pw_full[j], pool_scale_full[j], gf,
                                                             g_w1[i], g_w2[i])
            saved.append(dict(x=xcur, pooled=pooled, x1=x1, h2=h2, a=a, z=z))
        xcur = x2

    dx = xcur
    qv = q.reshape(1).astype(jnp.int32)
    ppack = lax.empty((R, D), F32)
    psib = lax.empty((R, D), F32)
    pending, summed = [], []

    def comm_args():
        kw = {}
        if pending:
            kw["scatters"] = tuple(dw for dw, _ in pending)
        if summed:
            kw["sib"] = (ppack, psib, tuple(summed))
        return kw

    def after_host(extra):
        nonlocal ppack, psib, pending, summed
        had_sib = bool(summed)
        summed = []
        for (dw, off), rb in zip(pending, extra[:len(pending)]):
            ppack = _sum_into(ppack, dw, rb, off, qv)
            summed.append((off, dw.shape[1]))
        if had_sib:
            psib = extra[len(pending)]
        pending = []

    dmod_rows = [None] * L
    dg_mix = [None] * L
    dg_ffn = [None] * L
    d_small = {}
    dwa_l, dwx_l, dpw_l = [None] * NA, [None] * NA, [None] * NB
    for i in reversed(range(L)):
        j = i // 2
        sv = saved[i]
        gm = norm_mix_g[i].reshape(1, D)
        gf = norm_ffn_g[i].reshape(1, D)
        head = (final_norm_g.reshape(1, D), tgt) if i == L - 1 else None
        outs = _ffn_bwd(dx, sv["x1"], sv["a"], sv["z"], mod[i], gf, g_w1[i], g_w2[i], head=head, **comm_args())
        dx1, du, dz, facc = outs[:4]
        after_host(outs[4:])
        if head:
            loss = lax.psum(0.5 * jnp.sum(facc[0]) / D, ("x", "y", "c"))
            d_final_g = facc[1]
        if i % 2 == 0:
            pending.append((_dw_blocked(sv["h2"], du, False, False, "dw1"), O_W1 + i * D))
            pending.append((_dw_blocked(sv["a"], dz, True, True, "dw2"), O_W2 + i * f4))
            outs = _lru_bwd(dx1, sv["y"], sv["x"], sv["xr0"], sv["gb"], sv["hs"], *sv["lru"], mod[i], gm, g_wout[j],
                            g_wy[j], g_win[j], conv_w_full[j], wa_full[j], wx_full[j], lru_lambda[j].reshape(1, D),
                            **comm_args())
            dyp, dgb, dxr, dx, sm, dwa, dwx, macc = outs[:8]
            after_host(outs[8:])
            dwa_l[j], dwx_l[j] = dwa, dwx
            if i == 0:
                tiny = jnp.concatenate([_blockdiag_by_chip(jnp.stack(dwa_l), D), _blockdiag_by_chip(jnp.stack(dwx_l), D),
                                        _blockdiag_by_chip(jnp.stack(dpw_l), D)], axis=1).astype(BF16)
                pending.append((tiny, O_WA))
            if i == 0:
                outs = _dw_whole(sv["h"], [dgb, dxr], "dwy_dwin", **comm_args())
                after_host(outs[2:])
                pending.append((outs[0], O_WY + j * dq))
                pending.append((outs[1], O_WIN + j * dq))
                outs = _dw_whole(sv["p"], [dyp], "dwout", **comm_args())
                after_host(outs[1:])
                pending.append((outs[0], O_WOUT + j * dq))
            else:
                outs = _dw_whole(sv["p"], [dyp], "dwout", **comm_args())
                after_host(outs[1:])
                pending.append((outs[0], O_WOUT + j * dq))
                outs = _dw_whole(sv["h"], [dgb, dxr], "dwy_dwin", **comm_args())
                after_host(outs[2:])
                pending.append((outs[0], O_WY + j * dq))
                pending.append((outs[1], O_WIN + j * dq))
            d_small[("lru", j)] = (sm, macc[3])
            dgt_m = macc[2]
        else:
            dx, macc, dpw, dw1, dw2 = _pool_bwd(dx1, sv["x"], sv["pooled"], mod[i], gm, pw_full[j], pool_scale_full[j],
                                                sv["h2"], du, sv["a"], dz)
            pending.append((dw1, O_W1 + i * D))
            pending.append((dw2, O_W2 + i * f4))
            dpw_l[j] = dpw
            d_small[("pool", j)] = macc[3]
            dgt_m = macc[2]
        dmod_rows[i] = jnp.stack([macc[0], macc[1], dgt_m, facc[3], facc[4], facc[5]])
        dg_mix[i] = macc[6]
        dg_ffn[i] = facc[7]
    grad_x = dx.reshape(x.shape)

    lru_sm = [d_small[("lru", j)] for j in range(NA)]
    small_rows = [jnp.stack(dmod_rows).reshape(L * N_MOD, D), jnp.stack(dg_mix), jnp.stack(dg_ffn),
                  jnp.stack([s[0][9] for s in lru_sm]), jnp.stack([s[0][8] for s in lru_sm]),
                  jnp.stack([s[0][4] for s in lru_sm]), jnp.stack([s[0][7] for s in lru_sm]),
                  jnp.stack([s[1] for s in lru_sm]),
                  jnp.stack([s[0][0:CONV_W] for s in lru_sm]).reshape(NA * CONV_W, D),
                  jnp.stack([s[0][5] for s in lru_sm]), jnp.stack([s[0][6] for s in lru_sm]),
                  jnp.stack([d_small[("pool", j)] for j in range(NB)]), d_final_g.reshape(1, D)]
    small_g = jnp.concatenate(small_rows, axis=0)
    n_small = small_g.shape[0]
    assert n_small <= SMALL_ROWS
    small_g = jnp.pad(small_g, ((0, SMALL_ROWS - n_small), (0, 0)))

    outs = _comm_only("scatter_last", small=small_g, reduce_small=True, **comm_args())
    sg_all, sg_sum = outs[:2]
    after_host(outs[2:])
    psum_mine = ppack
    psum_sib = _comm_only("sibling_last", sib=(ppack, psib, tuple(summed)))[0]

    def big_update(w, m, v, off, name):
        shp = w.shape
        g, dl, m2, v2 = _adam_rows(w.reshape(-1, D), m.reshape(-1, D), v.reshape(-1, D), psum_mine, psum_sib, off, name)
        return g.reshape(shp), dl.reshape(shp), m2.reshape(shp), v2.reshape(shp)

    res = {}
    res["ffn_w1"] = big_update(ffn_w1, m_ffn_w1, v_ffn_w1, O_W1, "adam_w1")
    res["ffn_w2"] = big_update(ffn_w2, m_ffn_w2, v_ffn_w2, O_W2, "adam_w2")
    res["lru_w_y"] = big_update(lru_w_y, m_lru_w_y, v_lru_w_y, O_WY, "adam_wy")
    res["lru_w_in"] = big_update(lru_w_in, m_lru_w_in, v_lru_w_in, O_WIN, "adam_win")
    res["lru_w_out"] = big_update(lru_w_out, m_lru_w_out, v_lru_w_out, O_WOUT, "adam_wout")

    def tiny_parts(w, off):
        n = int(w.size) // D
        return psum_mine[off:off + n].reshape(w.shape), psum_sib[off:off + n].reshape(w.shape)

    tiny_items = [("lru_w_a", lru_w_a, m_lru_w_a, v_lru_w_a) + tiny_parts(lru_w_a, O_WA),
                  ("lru_w_x", lru_w_x, m_lru_w_x, v_lru_w_x) + tiny_parts(lru_w_x, O_WX),
                  ("pool_w", pool_w, m_pool_w, v_pool_w) + tiny_parts(pool_w, O_PW)]

    dmod_all = sg_all[:, :L * N_MOD, :].reshape(NDEV, L, N_MOD * D)
    dmod_sh = lax.dynamic_slice_in_dim(dmod_all, q * Cs, Cs, axis=2).transpose(1, 0, 2)
    res["w_mod"] = _wmod_update(c_all.T, dmod_sh, w_mod, m_w_mod, v_w_mod)

    r0 = 0
    by_rows, names_a = [], []
    for name, w, m, v in (("b_mod", b_mod, m_b_mod, v_b_mod), ("norm_mix_g", norm_mix_g, m_norm_mix_g, v_norm_mix_g),
                          ("norm_ffn_g", norm_ffn_g, m_norm_ffn_g, v_norm_ffn_g),
                          ("lru_b_y", lru_b_y, m_lru_b_y, v_lru_b_y), ("lru_b_in", lru_b_in, m_lru_b_in, v_lru_b_in),
                          ("lru_conv_b", lru_conv_b, m_lru_conv_b, v_lru_conv_b),
                          ("lru_lambda", lru_lambda, m_lru_lambda, v_lru_lambda),
                          ("lru_b_out", lru_b_out, m_lru_b_out, v_lru_b_out)):
        by_rows.append((w, m, v, r0))
        names_a.append(name)
        r0 += int(w.size) // D
    g_conv_w = lax.dynamic_slice_in_dim(sg_sum[r0:r0 + NA * CONV_W].reshape(NA, CONV_W, D), q * dq, dq, axis=2)
    r0 += NA * CONV_W
    g_b_a = lax.dynamic_slice_in_dim(sg_sum[r0:r0 + NA].reshape(NA, HEADS, hd), q * (hd // NQ), hd // NQ, axis=2)
    r0 += NA
    g_b_x = lax.dynamic_slice_in_dim(sg_sum[r0:r0 + NA].reshape(NA, HEADS, hd), q * (hd // NQ), hd // NQ, axis=2)
    r0 += NA
    g_ps = lax.dynamic_slice_in_dim(sg_sum[r0:r0 + NB], q * dq, dq, axis=1)
    r0 += NB
    by_rows.append((final_norm_g.reshape(1, D), m_final_norm_g.reshape(1, D), v_final_norm_g.reshape(1, D), r0))
    names_a.append("final_norm_g")
    sliced = [(lru_conv_w, m_lru_conv_w, v_lru_conv_w, g_conv_w), (lru_b_a, m_lru_b_a, v_lru_b_a, g_b_a),
              (lru_b_x, m_lru_b_x, v_lru_b_x, g_b_x), (pool_scale, m_pool_scale, v_pool_scale, g_ps)]
    res_a, res_b, res_c = _adam_small(sg_sum, by_rows, sliced, [t[1:] for t in tiny_items])
    for name, r in zip(names_a, res_a):
        res[name] = r
    for t, r in zip(tiny_items, res_c):
        res[t[0]] = r
    res["final_norm_g"] = tuple(a.reshape(D) for a in res["final_norm_g"])
    for name, (_, _, _, g), r in zip(("lru_conv_w", "lru_b_a", "lru_b_x", "pool_scale"), sliced, res_b):
        res[name] = (g,) + r

    order = ["w_mod", "b_mod", "norm_mix_g", "norm_ffn_g", "lru_w_y", "lru_b_y", "lru_w_in", "lru_b_in", "lru_conv_w",
             "lru_conv_b", "lru_w_a", "lru_b_a", "lru_w_x", "lru_b_x", "lru_lambda", "lru_w_out", "lru_b_out", "pool_w",
             "pool_scale", "ffn_w1", "ffn_w2", "final_norm_g"]
    return (loss, grad_x, *[res[n][0] for n in order], *[res[n][1] for n in order],
            *[res[n][2] for n in order], *[res[n][3] for n in order])
```

```python
import jax
import jax.numpy as jnp
from jax import lax
from jax.experimental import pallas as pl
from jax.experimental.pallas import tpu as pltpu

F32 = jnp.float32
BF16 = jnp.bfloat16
MESH = pl.DeviceIdType.MESH

NQ = 4
NDEV = 8
DEPTH = 4
N_MOD = 6
HEADS = 4
CONV_W = 4
POOL_WINDOWS = (2, 4, 8, 16)
LRU_C = 8.0
EPS = 1e-6
ADAM_LR, ADAM_B1, ADAM_B2, ADAM_EPS, ADAM_WD, ADAM_STEP = 0.001, 0.9, 0.999, 1e-08, 0.01, 10

TM = 512
TT = 256
TP = 256
TPF = 512
TK = 2048
SMALL_ROWS = 64
FORWARD_STEPS = 4
VMEM_LIMIT = 60 * 1024 * 1024


def _cparams(*sem):
    return pltpu.CompilerParams(dimension_semantics=tuple(sem), vmem_limit_bytes=VMEM_LIMIT)


def _dot(a, b):
    return jnp.dot(a, b, preferred_element_type=F32)


def _dot_nt(a, b):
    return lax.dot_general(a, b, (((1,), (1,)), ((), ())), preferred_element_type=F32)


def _dot_tn(a, b):
    return lax.dot_general(a, b, (((0,), (0,)), ((), ())), preferred_element_type=F32)


def _resident(shape, index_map):
    return pl.BlockSpec(shape, index_map, pipeline_mode=pl.Buffered(1))


def _rms_fwd(x, g, sc, sh):
    r = lax.rsqrt(jnp.mean(x * x, axis=-1, keepdims=True) + EPS)
    xhat = x * r
    n = xhat * g
    return n * (1.0 + sc) + sh, xhat, r, n


def _rms_bwd(dh, xhat, r, n, g, sc):
    dsh = jnp.sum(dh, axis=0, keepdims=True)
    dsc = jnp.sum(dh * n, axis=0, keepdims=True)
    dn = dh * (1.0 + sc)
    dg = jnp.sum(dn * xhat, axis=0, keepdims=True)
    dxh = dn * g
    dx = r * (dxh - xhat * jnp.mean(dxh * xhat, axis=-1, keepdims=True))
    return dx, dsh, dsc, dg


_GELU_K = 0.7978845608028654
_GELU_C = 0.044715


def _gelu(x):
    t = jnp.tanh(_GELU_K * (x + _GELU_C * x * x * x))
    return 0.5 * x * (1.0 + t), t


def _gelu_grad(x, t):
    return 0.5 * (1.0 + t) + 0.5 * x * (1.0 - t * t) * (_GELU_K * (1.0 + 3.0 * _GELU_C * x * x))


def _neg_expm1(y, exp_y):
    series = -(y * (1.0 + y * (0.5 + y * (1.0 / 6.0))))
    return jnp.where(y > -(1.0 / 64.0), series, 1.0 - exp_y)


def _zero_first(ref):
    @pl.when(pl.program_id(0) == 0)
    def _():
        ref[...] = jnp.zeros_like(ref)


def _my_pos():
    return lax.axis_index("x"), lax.axis_index("y"), lax.axis_index("c")


def _dev_index(x, y, c):
    return 4 * x + 2 * y + c


def _chip_peers(x, y):
    return [(1 - x, y), (x, 1 - y), (1 - x, 1 - y)]


def _all_peers(x, y, c):
    return [(px, py, c) for (px, py) in _chip_peers(x, y)] + [(x, y, 1 - c)] + \
           [(px, py, 1 - c) for (px, py) in _chip_peers(x, y)]


def _comm_run(phase, x, y, c, gathers, scatters, sib, send, recv, loc):
    q = 2 * x + y
    peers = _chip_peers(x, y)
    sibling = (x, y, 1 - c)

    def rcopy(src, dst, s, dev):
        return pltpu.make_async_remote_copy(src, dst, send.at[s], recv.at[s], device_id=dev, device_id_type=MESH)

    s = 0
    for gi, (src, dst) in enumerate(gathers):
        half = src.shape[0] // 2
        mine, other = pl.ds(c * half, half), pl.ds((1 - c) * half, half)
        own = pltpu.make_async_copy(src, dst.at[q], loc.at[gi])
        if phase == "start":
            own.start()
        elif phase == "finish":
            own.wait()
        for (px, py) in peers:
            pq = 2 * px + py
            s_ici, s_fwd = s, s + 1
            s += 2
            if phase == "start":
                rcopy(src.at[mine], dst.at[q].at[mine], s_ici, (px, py, c)).start()
            elif phase == "forward":
                rcopy(src.at[mine], dst.at[pq].at[mine], s_ici, (px, py, c)).wait_recv()
                rcopy(dst.at[pq].at[mine], dst.at[pq].at[mine], s_fwd, sibling).start()
            else:
                rcopy(dst.at[pq].at[other], dst.at[pq].at[other], s_fwd, sibling).wait_recv()
                rcopy(src.at[mine], dst.at[q].at[mine], s_ici, (px, py, c)).wait_send()
                rcopy(dst.at[pq].at[mine], dst.at[pq].at[mine], s_fwd, sibling).wait_send()
    direct = []
    for (src, dst) in scatters:
        for k, (px, py) in enumerate(peers):
            direct.append((src.at[2 * px + py], dst.at[k], (px, py, c)))
    if sib is not None:
        src, dst, ranges = sib
        for (off, rows) in ranges:
            direct.append((src.at[pl.ds(off, rows)], dst.at[pl.ds(off, rows)], sibling))
    if phase == "start":
        for k, (a, b, dev) in enumerate(direct):
            rcopy(a, b, s + k, dev).start()
    elif phase == "finish":
        for k, (a, b, dev) in enumerate(direct):
            rcopy(a, b, s + k, dev).wait_recv()
        for k, (a, b, dev) in enumerate(direct):
            rcopy(a, b, s + k, dev).wait_send()


def _comm_shapes(gathers, scatters, sib):
    assert all(g.shape[0] % 32 == 0 for g in gathers)
    cin = list(gathers) + list(scatters) + ([sib[0], sib[1]] if sib else [])
    cout = [jax.ShapeDtypeStruct((NQ,) + g.shape, g.dtype) for g in gathers] + \
           [jax.ShapeDtypeStruct((3,) + s.shape[1:], s.dtype) for s in scatters] + \
           ([jax.ShapeDtypeStruct(sib[1].shape, sib[1].dtype)] if sib else [])
    n_rem = 6 * len(gathers) + 3 * len(scatters) + (len(sib[2]) if sib else 0)
    sems = [pltpu.SemaphoreType.DMA((max(n_rem, 1),)), pltpu.SemaphoreType.DMA((max(n_rem, 1),)),
            pltpu.SemaphoreType.DMA((max(len(gathers), 1),))]
    return cin, cout, sems


def _pcall(body, *, name, grid, in_specs, out_specs, out_shape, operands, scratch_shapes=(),
           gathers=(), scatters=(), sib=None):
    assert len(grid) == 1
    out_shape, out_specs = tuple(out_shape), tuple(out_specs)
    if not (gathers or scatters or sib):
        return pl.pallas_call(body, name=name, grid=grid, in_specs=list(in_specs), out_specs=out_specs,
                              out_shape=out_shape, scratch_shapes=list(scratch_shapes),
                              compiler_params=_cparams("arbitrary"))(*operands)
    cin, cout, sems = _comm_shapes(gathers, scatters, sib)
    n_in, n_cin, n_out, n_cout, n_scr = len(operands), len(cin), len(out_shape), len(cout), len(scratch_shapes)
    ng, ns = len(gathers), len(scatters)
    nsteps = grid[0]

    def wrapped(*refs):
        ins = refs[:n_in]
        cins = refs[n_in:n_in + n_cin]
        o0 = n_in + n_cin
        outs = refs[o0:o0 + n_out]
        couts = refs[o0 + n_out:o0 + n_out + n_cout]
        s0 = o0 + n_out + n_cout
        scr = refs[s0:s0 + n_scr]
        send, recv, loc = refs[s0 + n_scr:s0 + n_scr + 3]
        x, y, c = _my_pos()

        def run(phase):
            g = [(cins[k], couts[k]) for k in range(ng)]
            sc = [(cins[ng + k], couts[ng + k]) for k in range(ns)]
            sb = (cins[ng + ns], couts[ng + ns], sib[2]) if sib else None
            _comm_run(phase, x, y, c, g, sc, sb, send, recv, loc)

        @pl.when(pl.program_id(0) == 0)
        def _():
            run("start")

        if ng:
            @pl.when(pl.program_id(0) == max(nsteps - FORWARD_STEPS, 0))
            def _():
                run("forward")

        body(*ins, *outs, *scr)

        @pl.when(pl.program_id(0) == nsteps - 1)
        def _():
            run("finish")

    anyspec = pl.BlockSpec(memory_space=pl.ANY)
    aliases = {n_in + ng + ns + 1: n_out + ng + ns} if sib else {}
    return pl.pallas_call(
        wrapped, name=name, grid=grid,
        in_specs=list(in_specs) + [anyspec] * n_cin, out_specs=out_specs + (anyspec,) * n_cout,
        out_shape=out_shape + tuple(cout), scratch_shapes=list(scratch_shapes) + sems,
        input_output_aliases=aliases,
        compiler_params=pltpu.CompilerParams(dimension_semantics=("arbitrary",), vmem_limit_bytes=VMEM_LIMIT,
                                             has_side_effects=True),
    )(*operands, *cin)


def _comm_only(name, small=None, reduce_small=False, gathers=(), scatters=(), sib=None):
    cin, cout, sems = _comm_shapes(gathers, scatters, sib)
    n_cin, n_cout = len(cin), len(cout)
    ng, ns = len(gathers), len(scatters)
    n_sm_in = 1 if small is not None else 0
    n_sm_out = (2 if reduce_small else 1) if small is not None else 0

    def body(*refs):
        sm_in = refs[:n_sm_in]
        cins = refs[n_sm_in:n_sm_in + n_cin]
        o0 = n_sm_in + n_cin
        sm_out = refs[o0:o0 + n_sm_out]
        couts = refs[o0 + n_sm_out:o0 + n_sm_out + n_cout]
        s0 = o0 + n_sm_out + n_cout
        send, recv, loc = refs[s0:s0 + 3]
        x, y, c = _my_pos()
        g = [(cins[k], couts[k]) for k in range(ng)]
        sc = [(cins[ng + k], couts[ng + k]) for k in range(ns)]
        sb = (cins[ng + ns], couts[ng + ns], sib[2]) if sib else None
        _comm_run("start", x, y, c, g, sc, sb, send, recv, loc)
        if small is not None:
            sm_send, sm_recv = refs[s0 + 3:s0 + 5]
            small_ref, sg_ref = sm_in[0], sm_out[0]
            me = _dev_index(x, y, c)
            sg_ref[me] = small_ref[...]
            peers = _all_peers(x, y, c)
            sm = [pltpu.make_async_remote_copy(small_ref, sg_ref.at[me], sm_send.at[k], sm_recv.at[k],
                                               device_id=peer, device_id_type=MESH) for k, peer in enumerate(peers)]
            for cp in sm:
                cp.start()
            for k, (px, py, pc) in enumerate(peers):
                pltpu.make_async_remote_copy(small_ref, sg_ref.at[_dev_index(px, py, pc)], sm_send.at[k], sm_recv.at[k],
                                             device_id=(px, py, pc), device_id_type=MESH).wait_recv()
            if reduce_small:
                acc = sg_ref[0]
                for d in range(1, NDEV):
                    acc = acc + sg_ref[d]
                sm_out[1][...] = acc
            for cp in sm:
                cp.wait_send()
        if ng:
            _comm_run("forward", x, y, c, g, sc, sb, send, recv, loc)
        _comm_run("finish", x, y, c, g, sc, sb, send, recv, loc)

    anyspec = pl.BlockSpec(memory_space=pl.ANY)
    vspec = pl.BlockSpec(memory_space=pltpu.VMEM)
    sm_shapes = []
    if small is not None:
        sm_shapes.append(jax.ShapeDtypeStruct((NDEV,) + small.shape, small.dtype))
        if reduce_small:
            sm_shapes.append(jax.ShapeDtypeStruct(small.shape, small.dtype))
        sems = sems + [pltpu.SemaphoreType.DMA((NDEV - 1,)), pltpu.SemaphoreType.DMA((NDEV - 1,))]
    aliases = {n_sm_in + ng + ns + 1: n_sm_out + ng + ns} if sib else {}
    return pl.pallas_call(
        body, name=name,
        in_specs=[vspec] * n_sm_in + [anyspec] * n_cin,
        out_specs=tuple([vspec] * n_sm_out + [anyspec] * n_cout),
        out_shape=tuple(sm_shapes + cout), scratch_shapes=sems, input_output_aliases=aliases,
        compiler_params=pltpu.CompilerParams(has_side_effects=True),
    )(*([small] if small is not None else []), *cin)


def _exchange_mod(modpart):
    _, L, Cs = modpart.shape

    def body(part_ref, out_ref, send, recv):
        x, y, c = _my_pos()
        q = 2 * x + y
        me = _dev_index(x, y, c)
        out_ref[q] = part_ref[me]
        sends = []
        for k, (px, py) in enumerate(_chip_peers(x, y)):
            cp = pltpu.make_async_remote_copy(part_ref.at[_dev_index(px, py, c)], out_ref.at[q], send.at[k], recv.at[k],
                                              device_id=(px, py, c), device_id_type=MESH)
            cp.start()
            sends.append(cp)
        for k, (px, py) in enumerate(_chip_peers(x, y)):
            pltpu.make_async_remote_copy(part_ref.at[me], out_ref.at[2 * px + py], send.at[k], recv.at[k],
                                         device_id=(px, py, c), device_id_type=MESH).wait_recv()
        for cp in sends:
            cp.wait_send()

    return pl.pallas_call(
        body, name="exchange_mod",
        out_shape=jax.ShapeDtypeStruct((NQ, L, Cs), modpart.dtype),
        in_specs=[pl.BlockSpec(memory_space=pltpu.VMEM)],
        out_specs=pl.BlockSpec(memory_space=pltpu.VMEM),
        scratch_shapes=[pltpu.SemaphoreType.DMA((3,)), pltpu.SemaphoreType.DMA((3,))],
        compiler_params=pltpu.CompilerParams(has_side_effects=True),
    )(modpart)


def _mod_part(c_all, w_mod, b_mod_sh):
    L, D, Cs = w_mod.shape
    tn = 512 if Cs % 512 == 0 else Cs

    def body(c_ref, w_ref, b_ref, o_ref):
        cv = c_ref[...]
        cond = cv * jax.nn.sigmoid(cv)
        o_ref[...] = jnp.dot(cond, w_ref[...], preferred_element_type=F32, precision=lax.Precision.HIGHEST) + b_ref[...]

    return pl.pallas_call(
        body, name="mod_part", grid=(L, Cs // tn),
        out_shape=jax.ShapeDtypeStruct((L, NDEV, Cs), F32),
        in_specs=[pl.BlockSpec((NDEV, D), lambda i, j: (0, 0)),
                  pl.BlockSpec((None, D, tn), lambda i, j: (i, 0, j)),
                  pl.BlockSpec((None, 1, tn), lambda i, j: (i, 0, j))],
        out_specs=pl.BlockSpec((None, NDEV, tn), lambda i, j: (i, 0, j)),
        compiler_params=_cparams("parallel", "parallel"),
    )(c_all, w_mod, b_mod_sh)


def _adam(w, g, m, v):
    m2 = ADAM_B1 * m + (1.0 - ADAM_B1) * g
    v2 = ADAM_B2 * v + (1.0 - ADAM_B2) * (g * g)
    m_hat = m2 / (1.0 - ADAM_B1 ** ADAM_STEP)
    v_hat = v2 / (1.0 - ADAM_B2 ** ADAM_STEP)
    delta = -ADAM_LR * (m_hat / (jnp.sqrt(v_hat) + ADAM_EPS) + ADAM_WD * w)
    return delta, m2, v2


def _wmod_update(c_all_t, dmod_sh, w, m, v):
    L, D, Cs = w.shape
    td = 512 if D % 512 == 0 else D

    def body(ct_ref, d_ref, w_ref, m_ref, v_ref, g_ref, dl_ref, m2_ref, v2_ref):
        cv = ct_ref[...]
        cond = cv * jax.nn.sigmoid(cv)
        g = cond[:, 0:1] * d_ref[0:1, :]
        for b in range(1, NDEV):
            g = g + cond[:, b:b + 1] * d_ref[b:b + 1, :]
        g_ref[...] = g
        dl_ref[...], m2_ref[...], v2_ref[...] = _adam(w_ref[...], g, m_ref[...], v_ref[...])

    blk = pl.BlockSpec((None, td, Cs), lambda i, j: (i, j, 0))
    out = jax.ShapeDtypeStruct((L, D, Cs), F32)
    return pl.pallas_call(
        body, name="wmod_update", grid=(L, D // td),
        out_shape=(out, out, out, out),
        in_specs=[pl.BlockSpec((td, NDEV), lambda i, j: (j, 0)),
                  pl.BlockSpec((None, NDEV, Cs), lambda i, j: (i, 0, 0)), blk, blk, blk],
        out_specs=(blk, blk, blk, blk),
        compiler_params=_cparams("parallel", "parallel"),
    )(c_all_t, dmod_sh, w, m, v)


def _adam_rows(w, m, v, pa, pb, row_off, name):
    rows, C = w.shape
    tr = 512 if rows % 512 == 0 else (128 if rows % 128 == 0 else rows)
    assert row_off % tr == 0
    ob = row_off // tr

    def body(w_ref, m_ref, v_ref, pa_ref, pb_ref, g_ref, dl_ref, m2_ref, v2_ref):
        g = pa_ref[...] + pb_ref[...]
        g_ref[...] = g
        dl_ref[...], m2_ref[...], v2_ref[...] = _adam(w_ref[...], g, m_ref[...], v_ref[...])

    blk = pl.BlockSpec((tr, C), lambda i: (i, 0))
    pblk = pl.BlockSpec((tr, C), lambda i: (ob + i, 0))
    out = jax.ShapeDtypeStruct((rows, C), F32)
    return pl.pallas_call(
        body, name=name, grid=(rows // tr,), out_shape=(out, out, out, out),
        in_specs=[blk, blk, blk, pblk, pblk], out_specs=(blk, blk, blk, blk),
        compiler_params=_cparams("parallel"),
    )(w, m, v, pa, pb)


def _adam_small(sg_sum, by_rows, sliced, pairs):
    D = sg_sum.shape[1]
    na, nb, nc = len(by_rows), len(sliced), len(pairs)

    def body(*refs):
        sg = refs[0]
        ins_a = [refs[1 + 3 * t:4 + 3 * t] for t in range(na)]
        p = 1 + 3 * na
        ins_b = [refs[p + 4 * t:p + 4 * t + 4] for t in range(nb)]
        p += 4 * nb
        ins_c = [refs[p + 5 * t:p + 5 * t + 5] for t in range(nc)]
        p += 5 * nc
        outs_a = [refs[p + 4 * t:p + 4 * t + 4] for t in range(na)]
        p += 4 * na
        outs_b = [refs[p + 3 * t:p + 3 * t + 3] for t in range(nb)]
        p += 3 * nb
        outs_c = [refs[p + 4 * t:p + 4 * t + 4] for t in range(nc)]
        for (w_ref, m_ref, v_ref, ga_ref, gb_ref), (g_ref, dl_ref, m2_ref, v2_ref) in zip(ins_c, outs_c):
            g = ga_ref[...] + gb_ref[...]
            g_ref[...] = g
            dl_ref[...], m2_ref[...], v2_ref[...] = _adam(w_ref[...], g, m_ref[...], v_ref[...])
        for (w_ref, m_ref, v_ref), (g_ref, dl_ref, m2_ref, v2_ref), (w, _, _, row0) in zip(ins_a, outs_a, by_rows):
            n, k = w.shape[0], w.shape[1] // D
            pieces = [(slice(0, n), slice(0, D), slice(row0, row0 + n))] if k == 1 else \
                     [(slice(i, i + 1), slice(kk * D, (kk + 1) * D), slice(row0 + i * k + kk, row0 + i * k + kk + 1))
                      for i in range(n) for kk in range(k)]
            for rs, cs, gs in pieces:
                g = sg[gs, :]
                g_ref[rs, cs] = g
                dl_ref[rs, cs], m2_ref[rs, cs], v2_ref[rs, cs] = _adam(w_ref[rs, cs], g, m_ref[rs, cs], v_ref[rs, cs])
        for (w_ref, m_ref, v_ref, g_ref), (dl_ref, m2_ref, v2_ref) in zip(ins_b, outs_b):
            dl_ref[...], m2_ref[...], v2_ref[...] = _adam(w_ref[...], g_ref[...], m_ref[...], v_ref[...])

    operands = [sg_sum] + [a for t in by_rows for a in t[:3]] + [a for t in sliced for a in t] + \
               [a for t in pairs for a in t]
    out_shape = [jax.ShapeDtypeStruct(t[0].shape, F32) for t in by_rows for _ in range(4)] + \
                [jax.ShapeDtypeStruct(t[0].shape, F32) for t in sliced for _ in range(3)] + \
                [jax.ShapeDtypeStruct(t[0].shape, F32) for t in pairs for _ in range(4)]
    outs = pl.pallas_call(body, name="adam_small", out_shape=tuple(out_shape))(*operands)
    res_a = [tuple(outs[4 * t:4 * t + 4]) for t in range(na)]
    o = 4 * na
    res_b = [tuple(outs[o + 3 * t:o + 3 * t + 3]) for t in range(nb)]
    o += 3 * nb
    res_c = [tuple(outs[o + 4 * t:o + 4 * t + 4]) for t in range(nc)]
    return res_a, res_b, res_c


def _sum_into(ppack, dw, rb, off, qv):
    _, rows, D = dw.shape
    tr = 512 if rows % 512 == 0 else (256 if rows % 256 == 0 else 128)
    assert rows % tr == 0 and off % tr == 0
    ob = off // tr

    def body(q_ref, o_ref, r_ref, pin_ref, p_ref):
        acc = o_ref[...].astype(F32)
        for k in range(3):
            acc = acc + r_ref[k].astype(F32)
        p_ref[...] = acc

    return pl.pallas_call(
        body, name="sum_partials", out_shape=jax.ShapeDtypeStruct(ppack.shape, ppack.dtype),
        grid_spec=pltpu.PrefetchScalarGridSpec(
            num_scalar_prefetch=1, grid=(rows // tr,),
            in_specs=[pl.BlockSpec((None, tr, D), lambda i, q_ref: (q_ref[0], i, 0)),
                      pl.BlockSpec((3, tr, D), lambda i, q_ref: (0, i, 0)),
                      pl.BlockSpec(memory_space=pl.ANY)],
            out_specs=pl.BlockSpec((tr, D), lambda i, q_ref: (ob + i, 0))),
        input_output_aliases={3: 0},
        compiler_params=_cparams("parallel"),
    )(qv, dw, rb, ppack)


def _wspec(g):
    return _resident(g.shape, lambda i: (0, 0, 0))


def _ffn_fwd_inner(x1, mod_ref, gf_ref, w1_ref, w2_ref, h2_ref, a_ref, z_ref, x2_ref):
    h2 = _rms_fwd(x1, gf_ref[...], mod_ref[4:5, :], mod_ref[3:4, :])[0]
    h2b = h2.astype(BF16)
    h2_ref[...] = h2b
    f4 = w1_ref.shape[2]
    z = jnp.zeros(x1.shape, F32)
    for q in range(NQ):
        a = jnp.maximum(_dot(h2b, w1_ref[q]), 0.0)
        a_ref[:, q * f4:(q + 1) * f4] = a.astype(BF16)
        z = z + _dot((a * a).astype(BF16), w2_ref[q])
    z_ref[...] = z.astype(BF16)
    x2_ref[...] = x1 + mod_ref[5:6, :] * z


def _sigmoid(x):
    return jax.nn.sigmoid(x)


def _heads_dot(xb, w_ref, hd, nt=False):
    outs = []
    for h in range(HEADS):
        xs = xb[:, h * hd:(h + 1) * hd]
        outs.append(_dot_nt(xs, w_ref[h]) if nt else _dot(xs, w_ref[h]))
    return jnp.concatenate(outs, axis=1)


def _lru_gates(xc, wa_ref, ba, wx_ref, bx, lam, hd):
    xcb = xc.astype(BF16)
    gate_r = _sigmoid(_heads_dot(xcb, wa_ref, hd) + ba)
    gate_i = _sigmoid(_heads_dot(xcb, wx_ref, hd) + bx)
    ls = jax.nn.log_sigmoid(lam)
    log_a = gate_r * (LRU_C * ls)
    a = jnp.exp(log_a)
    mult = jnp.sqrt(_neg_expm1(2.0 * log_a, a * a))
    return xcb, gate_r, gate_i, ls, a, mult


def _conv_taps(xext, cw, tt):
    acc = cw[0:1, :] * xext[pl.ds(8 - (CONV_W - 1), tt), :]
    for k in range(1, CONV_W):
        acc = acc + cw[k:k + 1, :] * xext[pl.ds(8 - (CONV_W - 1) + k, tt), :]
    return acc


def _lru_fwd(x, mod_l, g_mix, g_wy, g_win, b_y, b_in, cw, cb, wa, ba, wx, bx, lam, g_wout, b_out, **comm):
    S, W = x.shape
    tt = min(TT, S)
    hd = W // HEADS

    def body(x_ref, mod_ref, g_ref, wy_ref, win_ref, by_ref, bin_ref, cw_ref, cb_ref, wa_ref, ba_ref, wx_ref, bx_ref,
             lam_ref, wo_ref, bo_ref, h_ref, gb_ref, xr_ref, hs_ref, p_ref, y_ref, x1_ref,
             xc_ref, gr_ref, gi_ref, a_s, mu_ref, xext, u_s, carry):
        i = pl.program_id(0)

        @pl.when(i == 0)
        def _():
            carry[...] = jnp.zeros_like(carry)
            xext[0:8, :] = jnp.zeros((8, W), F32)

        @pl.when(i > 0)
        def _():
            xext[0:8, :] = xext[pl.ds(tt, 8), :]

        xv = x_ref[...]
        hb = _rms_fwd(xv, g_ref[...], mod_ref[1:2, :], mod_ref[0:1, :])[0].astype(BF16)
        h_ref[...] = hb
        gbv = _dot(hb, wy_ref[...].reshape(W, W)) + by_ref[...]
        gb_ref[...] = gbv
        xr = _dot(hb, win_ref[...].reshape(W, W)) + bin_ref[...]
        xr_ref[...] = xr
        xext[pl.ds(8, tt), :] = xr
        xc = _conv_taps(xext, cw_ref[...], tt) + cb_ref[...]
        _, gate_r, gate_i, _, a, mult = _lru_gates(xc, wa_ref, ba_ref[...], wx_ref, bx_ref[...], lam_ref[...], hd)
        xc_ref[...] = xc
        gr_ref[...] = gate_r
        gi_ref[...] = gate_i
        mu_ref[...] = mult
        a_s[...] = a
        u_s[...] = mult * (gate_i * xc)
        row = lax.broadcasted_iota(jnp.int32, (8, W), 0)

        def step(k, _):
            off = pl.multiple_of(k * 8, 8)
            A = a_s[pl.ds(off, 8), :]
            U = u_s[pl.ds(off, 8), :]
            for d in (1, 2, 4):
                keep = row >= d
                Us = jnp.where(keep, pltpu.roll(U, d, 0), 0.0)
                As = jnp.where(keep, pltpu.roll(A, d, 0), 1.0)
                U = U + A * Us
                A = A * As
            H = U + A * carry[...]
            hs_ref[pl.ds(off, 8), :] = H
            carry[...] = jnp.broadcast_to(H[7:8, :], (8, W))
            return 0

        lax.fori_loop(0, tt // 8, step, 0)
        pb = (hs_ref[...] * _gelu(gbv)[0]).astype(BF16)
        p_ref[...] = pb
        y = _dot(pb, wo_ref[...].reshape(W, W)) + bo_ref[...]
        y_ref[...] = y.astype(BF16)
        x1_ref[...] = xv + mod_ref[2:3, :] * y

    tile = pl.BlockSpec((tt, W), lambda i: (i, 0))
    row = pl.BlockSpec((1, W), lambda i: (0, 0))
    wblk = pl.BlockSpec((HEADS, hd, hd), lambda i: (0, 0, 0))
    f32o, bf16o = jax.ShapeDtypeStruct((S, W), F32), jax.ShapeDtypeStruct((S, W), BF16)
    return _pcall(
        body, name="lru_fwd", grid=(S // tt,),
        out_shape=(bf16o, f32o, f32o, f32o, bf16o, bf16o, f32o, f32o, f32o, f32o, f32o, f32o),
        in_specs=[tile, pl.BlockSpec((8, W), lambda i: (0, 0)), row, _wspec(g_wy), _wspec(g_win), row, row,
                  pl.BlockSpec((CONV_W, W), lambda i: (0, 0)), row, wblk, row, wblk, row, row, _wspec(g_wout), row],
        out_specs=(tile,) * 12,
        scratch_shapes=[pltpu.VMEM((tt + 8, W), F32), pltpu.VMEM((tt, W), F32), pltpu.VMEM((8, W), F32)],
        operands=(x, mod_l, g_mix, g_wy, g_win, b_y, b_in, cw, cb, wa, ba, wx, bx, lam, g_wout, b_out), **comm)


def _ffn_out_shapes(S, D, F):
    return (jax.ShapeDtypeStruct((S, D), BF16), jax.ShapeDtypeStruct((S, F), BF16),
            jax.ShapeDtypeStruct((S, D), BF16), jax.ShapeDtypeStruct((S, D), F32))


def _ffn_fwd(x1, mod_l, g_ffn, g_w1, g_w2, **comm):
    S, D = x1.shape
    tm = min(TM, S)
    F = g_w1.shape[2] * NQ

    def body(x1_ref, mod_ref, gf_ref, w1_ref, w2_ref, h2_ref, a_ref, z_ref, x2_ref):
        _ffn_fwd_inner(x1_ref[...], mod_ref, gf_ref, w1_ref, w2_ref, h2_ref, a_ref, z_ref, x2_ref)

    tile = pl.BlockSpec((tm, D), lambda i: (i, 0))
    row = pl.BlockSpec((1, D), lambda i: (0, 0))
    return _pcall(
        body, name="ffn_fwd", grid=(S // tm,),
        out_shape=_ffn_out_shapes(S, D, F),
        in_specs=[tile, pl.BlockSpec((8, D), lambda i: (0, 0)), row, _wspec(g_w1), _wspec(g_w2)],
        out_specs=(tile, pl.BlockSpec((tm, F), lambda i: (i, 0)), tile, tile),
        operands=(x1, mod_l, g_ffn, g_w1, g_w2), **comm)


def _window_vec(D):
    gd = D // len(POOL_WINDOWS)
    lane = lax.broadcasted_iota(jnp.int32, (1, D), 1)
    w = jnp.full((1, D), float(POOL_WINDOWS[0]), F32)
    for g in range(1, len(POOL_WINDOWS)):
        w = jnp.where(lane >= g * gd, float(POOL_WINDOWS[g]), w)
    return w


def _pool_mix_ffn_fwd(x, mod_l, g_mix, pw, ps, g_ffn, g_w1, g_w2, **comm):
    S, D = x.shape
    tm = min(TPF, S)
    F = g_w1.shape[2] * NQ
    gd = D // len(POOL_WINDOWS)
    n = tm + 24

    def body(x_ref, xh_ref, mod_ref, gm_ref, pw_ref, ps_ref, gf_ref, w1_ref, w2_ref,
             pl_ref, x1_ref, h2_ref, a_ref, z_ref, x2_ref, ext, b1, b2):
        i = pl.program_id(0)
        g, sc, sh = gm_ref[...], mod_ref[1:2, :], mod_ref[0:1, :]
        xv = x_ref[...]
        h = _rms_fwd(xv, g, sc, sh)[0]
        hh = _rms_fwd(xh_ref[...], g, sc, sh)[0]
        zeros8 = jnp.zeros((8, D), F32)
        ext[0:8, :] = zeros8
        b1[0:8, :] = zeros8
        b2[0:8, :] = zeros8
        ext[8:24, :] = jnp.where(i > 0, hh, 0.0)
        ext[pl.ds(24, tm), :] = h
        m = n - 8
        b1[pl.ds(8, m), :] = ext[pl.ds(8, m), :] + ext[pl.ds(7, m), :]
        b2[pl.ds(8, m), gd:] = b1[pl.ds(8, m), gd:] + b1[pl.ds(6, m), gd:]
        b1[pl.ds(8, m), 2 * gd:] = b2[pl.ds(8, m), 2 * gd:] + b2[pl.ds(4, m), 2 * gd:]
        b2[pl.ds(8, m), 3 * gd:] = b1[pl.ds(8, m), 3 * gd:] + b1[pl.ds(0, m), 3 * gd:]
        wsum = jnp.concatenate([b1[pl.ds(24, tm), 0:gd], b2[pl.ds(24, tm), gd:2 * gd],
                                b1[pl.ds(24, tm), 2 * gd:3 * gd], b2[pl.ds(24, tm), 3 * gd:]], axis=1)
        t1 = (lax.broadcasted_iota(jnp.int32, (tm, 1), 0) + (i * tm + 1)).astype(F32)
        cnt = jnp.minimum(t1, _window_vec(D))
        pooled = (wsum / cnt - h).astype(BF16)
        pl_ref[...] = pooled
        y = _heads_dot(pooled, pw_ref, gd) * ps_ref[...]
        x1 = xv + mod_ref[2:3, :] * y
        x1_ref[...] = x1
        _ffn_fwd_inner(x1, mod_ref, gf_ref, w1_ref, w2_ref, h2_ref, a_ref, z_ref, x2_ref)

    tile = pl.BlockSpec((tm, D), lambda i: (i, 0))
    halo = pl.BlockSpec((16, D), lambda i: (jnp.maximum(i * (tm // 16) - 1, 0), 0))
    row = pl.BlockSpec((1, D), lambda i: (0, 0))
    return _pcall(
        body, name="pool_mix_ffn_fwd", grid=(S // tm,),
        out_shape=(jax.ShapeDtypeStruct((S, D), BF16), jax.ShapeDtypeStruct((S, D), F32)) + _ffn_out_shapes(S, D, F),
        in_specs=[tile, halo, pl.BlockSpec((8, D), lambda i: (0, 0)), row,
                  pl.BlockSpec((len(POOL_WINDOWS), gd, gd), lambda i: (0, 0, 0)), row, row,
                  _wspec(g_w1), _wspec(g_w2)],
        out_specs=(tile, tile, tile, pl.BlockSpec((tm, F), lambda i: (i, 0)), tile, tile),
        scratch_shapes=[pltpu.VMEM((n, D), F32), pltpu.VMEM((n, D), F32), pltpu.VMEM((n, D), F32)],
        operands=(x, x, mod_l, g_mix, pw, ps, g_ffn, g_w1, g_w2), **comm)


def _loss_head(xv, gv, tv, acc_ref):
    D = xv.shape[1]
    r = lax.rsqrt(jnp.mean(xv * xv, axis=-1, keepdims=True) + EPS)
    xhat = xv * r
    err = xhat * gv - tv
    acc_ref[0:1, :] += jnp.sum(err * err, axis=0, keepdims=True)
    dy = err * (1.0 / D)
    acc_ref[1:2, :] += jnp.sum(dy * xhat, axis=0, keepdims=True)
    dxh = dy * gv
    return r * (dxh - xhat * jnp.mean(dxh * xhat, axis=-1, keepdims=True))


def _ffn_bwd(dx2, x1, a, z, mod_l, g_ffn, g_w1, g_w2, head=None, **comm):
    S, D = dx2.shape
    F = a.shape[1]
    f4 = F // NQ
    tm = min(TM, S)
    nh = 2 if head else 0

    def body(*refs):
        dx2_ref, x1_ref, a_ref, z_ref, mod_ref, gf_ref, w1_ref, w2_ref = refs[:8]
        dx1_ref, du_ref, dz_ref, acc_ref = refs[8 + nh:]
        _zero_first(acc_ref)
        dx2v = dx2_ref[...]
        if head:
            dx2v = _loss_head(dx2v, refs[8][...], refs[9][...], acc_ref)
        acc_ref[5:6, :] +=jnp.sum(dx2v * z_ref[...].astype(F32), axis=0, keepdims=True)
        dzb = (dx2v * mod_ref[5:6, :]).astype(BF16)
        dz_ref[...] = dzb
        dh2 = jnp.zeros((tm, D), F32)
        for q in range(NQ):
            av = a_ref[:, q * f4:(q + 1) * f4].astype(F32)
            du = (_dot_nt(dzb, w2_ref[q]) * (2.0 * av)).astype(BF16)
            du_ref[:, q * f4:(q + 1) * f4] = du
            dh2 = dh2 + _dot_nt(du, w1_ref[q])
        g, sc = gf_ref[...], mod_ref[4:5, :]
        _, xhat, r, n = _rms_fwd(x1_ref[...], g, sc, mod_ref[3:4, :])
        dx, dsh, dsc, dg = _rms_bwd(dh2, xhat, r, n, g, sc)
        acc_ref[3:4, :] += dsh
        acc_ref[4:5, :] += dsc
        acc_ref[7:8, :] += dg
        dx1_ref[...] = dx2v + dx

    tile = pl.BlockSpec((tm, D), lambda i: (i, 0))
    wide = pl.BlockSpec((tm, F), lambda i: (i, 0))
    return _pcall(
        body, name="ffn_bwd", grid=(S // tm,),
        out_shape=(jax.ShapeDtypeStruct((S, D), F32), jax.ShapeDtypeStruct((S, F), BF16),
                   jax.ShapeDtypeStruct((S, D), BF16), jax.ShapeDtypeStruct((8, D), F32)),
        in_specs=[tile, tile, wide, tile, pl.BlockSpec((8, D), lambda i: (0, 0)), pl.BlockSpec((1, D), lambda i: (0, 0)),
                  _wspec(g_w1), _wspec(g_w2)] + ([pl.BlockSpec((1, D), lambda i: (0, 0)), tile] if head else []),
        out_specs=(tile, wide, tile, pl.BlockSpec((8, D), lambda i: (0, 0))),
        operands=(dx2, x1, a, z, mod_l, g_ffn, g_w1, g_w2) + (tuple(head) if head else ()), **comm)


def _dw_blocked(a, b, by_rows, square_a, name):
    S = a.shape[0]
    tk = min(TK, S)
    nk = S // tk
    if by_rows:
        bm, bn = a.shape[1] // NQ, b.shape[1]
        a_map, b_map = (lambda q, k: (k, q)), (lambda q, k: (k, 0))
    else:
        bm, bn = a.shape[1], b.shape[1] // NQ
        a_map, b_map = (lambda q, k: (k, 0)), (lambda q, k: (k, q))

    def body(a_ref, b_ref, o_ref, acc):
        k = pl.program_id(1)

        @pl.when(k == 0)
        def _():
            acc[...] = jnp.zeros_like(acc)

        av = a_ref[...]
        if square_a:
            av = av * av
        acc[...] += _dot_tn(av, b_ref[...])

        @pl.when(k == nk - 1)
        def _():
            o_ref[...] = acc[...].astype(o_ref.dtype)

    return pl.pallas_call(
        body, name=name, grid=(NQ, nk),
        out_shape=jax.ShapeDtypeStruct((NQ, bm, bn), BF16),
        in_specs=[pl.BlockSpec((tk, bm), a_map), pl.BlockSpec((tk, bn), b_map)],
        out_specs=pl.BlockSpec((None, bm, bn), lambda q, k: (q, 0, 0)),
        scratch_shapes=[pltpu.VMEM((bm, bn), F32)],
        compiler_params=_cparams("parallel", "arbitrary"),
    )(a, b)


def _dw_whole(a, bs, name, **comm):
    S, M = a.shape
    N = bs[0].shape[1]
    tk = min(TK, S)
    nk = S // tk
    nb = len(bs)

    def body(*refs):
        a_ref, b_refs, o_refs, accs = refs[0], refs[1:1 + nb], refs[1 + nb:1 + 2 * nb], refs[1 + 2 * nb:]
        k = pl.program_id(0)

        @pl.when(k == 0)
        def _():
            for acc in accs:
                acc[...] = jnp.zeros_like(acc)

        av = a_ref[...]
        for b_ref, acc in zip(b_refs, accs):
            acc[...] += _dot_tn(av, b_ref[...])

        @pl.when(k == nk - 1)
        def _():
            for o_ref, acc in zip(o_refs, accs):
                o_ref[...] = acc[...].reshape(NQ, M // NQ, N).astype(o_ref.dtype)

    return _pcall(
        body, name=name, grid=(nk,),
        out_shape=tuple(jax.ShapeDtypeStruct((NQ, M // NQ, N), BF16) for _ in bs),
        in_specs=[pl.BlockSpec((tk, M), lambda k: (k, 0))] + [pl.BlockSpec((tk, N), lambda k: (k, 0)) for _ in bs],
        out_specs=tuple(pl.BlockSpec((NQ, M // NQ, N), lambda k: (0, 0, 0)) for _ in bs),
        scratch_shapes=[pltpu.VMEM((M, N), F32) for _ in bs],
        operands=(a, *bs), **comm)


def _lru_bwd(dx1, y, x, xr0, gb, hs, xc_, gate_r_, gate_i_, a_, mult_, mod_l, g_mix, g_wout, g_wy, g_win, cw, wa, wx,
             lam, **comm):
    S, W = xr0.shape
    tt = min(TT, S)
    nb = S // tt
    hd = W // HEADS

    def body(dx1_ref, y_ref, x_ref, xr_ref, gb_ref, hs_ref, hsh_ref, xc_ref, gr_ref, gi_ref, a_s, mu_ref,
             mod_ref, gm_ref, wo_ref, wy_ref, win_ref, cw_ref, wa_ref, wx_ref, lam_ref,
             dy_ref, dgb_ref, dxr_ref, dx_ref, sm_ref, dwa_ref, dwx_ref, acc_ref,
             hext, qext, dext, b_s, qc, dc):
        i = pl.program_id(0)
        blk = nb - 1 - i

        @pl.when(i == 0)
        def _():
            sm_ref[...] = jnp.zeros_like(sm_ref)
            dwa_ref[...] = jnp.zeros_like(dwa_ref)
            dwx_ref[...] = jnp.zeros_like(dwx_ref)
            acc_ref[...] = jnp.zeros_like(acc_ref)
            qc[...] = jnp.zeros_like(qc)
            dc[...] = jnp.zeros_like(dc)

        dx1v = dx1_ref[...]
        acc_ref[2:3, :] += jnp.sum(dx1v * y_ref[...].astype(F32), axis=0, keepdims=True)
        dy = dx1v * mod_ref[2:3, :]
        acc_ref[3:4, :] += jnp.sum(dy, axis=0, keepdims=True)
        dyb = dy.astype(BF16)
        dy_ref[...] = dyb
        dpv = _dot_nt(dyb, wo_ref[...].reshape(W, W))

        hext[0:8, :] = jnp.where(blk > 0, hsh_ref[...], 0.0)
        hext[pl.ds(8, tt), :] = hs_ref[...]
        cw = cw_ref[...]
        lam = lam_ref[...]
        xc, gate_r, gate_i, a, mult = xc_ref[...], gr_ref[...], gi_ref[...], a_s[...], mu_ref[...]
        xcb = xc.astype(BF16)
        ls = jax.nn.log_sigmoid(lam)

        gbv = gb_ref[...]
        gate, th = _gelu(gbv)
        dgb = dpv * hs_ref[...] * _gelu_grad(gbv, th)
        dgbb = dgb.astype(BF16)
        dgb_ref[...] = dgbb
        sm_ref[9:10, :] += jnp.sum(dgb, axis=0, keepdims=True)
        dhs = dpv * gate

        b_s[...] = a * dhs
        qext[pl.ds(tt, 8), :] = qc[...]
        row = lax.broadcasted_iota(jnp.int32, (8, W), 0)

        def step(k, _):
            off = pl.multiple_of((tt // 8 - 1 - k) * 8, 8)
            A = a_s[pl.ds(off, 8), :]
            B = b_s[pl.ds(off, 8), :]
            for d in (1, 2, 4):
                keep = row < 8 - d
                Bs = jnp.where(keep, pltpu.roll(B, 8 - d, 0), 0.0)
                As = jnp.where(keep, pltpu.roll(A, 8 - d, 0), 1.0)
                B = B + A * Bs
                A = A * As
            Q = B + A * qc[...]
            qext[pl.ds(off, 8), :] = Q
            qc[...] = jnp.broadcast_to(Q[0:1, :], (8, W))
            return 0

        lax.fori_loop(0, tt // 8, step, 0)
        gsc = dhs + qext[pl.ds(1, tt), :]
        da = gsc * hext[pl.ds(7, tt), :]
        t1 = gsc * xc
        dmult = t1 * gate_i
        dgate_i = t1 * mult
        dxc = gsc * (mult * gate_i)
        dlog_a = da * a - dmult * (a * a) / mult
        dgate_r = dlog_a * (LRU_C * ls)
        sm_ref[7:8, :] += jnp.sum(dlog_a * (LRU_C * gate_r), axis=0, keepdims=True)
        dga = dgate_r * gate_r * (1.0 - gate_r)
        dgx = dgate_i * gate_i * (1.0 - gate_i)
        sm_ref[5:6, :] += jnp.sum(dga, axis=0, keepdims=True)
        sm_ref[6:7, :] += jnp.sum(dgx, axis=0, keepdims=True)
        dgab = dga.astype(BF16)
        dgxb = dgx.astype(BF16)
        dxc = dxc + _heads_dot(dgab, wa_ref, hd, nt=True) + _heads_dot(dgxb, wx_ref, hd, nt=True)
        for h in range(HEADS):
            sl = slice(h * hd, (h + 1) * hd)
            dwa_ref[h] += _dot_tn(xcb[:, sl], dgab[:, sl])
            dwx_ref[h] += _dot_tn(xcb[:, sl], dgxb[:, sl])
        sm_ref[4:5, :] += jnp.sum(dxc, axis=0, keepdims=True)
        dext[pl.ds(0, tt), :] = dxc
        dext[pl.ds(tt, 8), :] = dc[...]
        xrv = xr_ref[...]
        dxr = None
        for k in range(CONV_W):
            up = dext[pl.ds(CONV_W - 1 - k, tt), :]
            sm_ref[k:k + 1, :] += jnp.sum(up * xrv, axis=0, keepdims=True)
            dxr = cw[k:k + 1, :] * up if dxr is None else dxr + cw[k:k + 1, :] * up
        dc[...] = dext[0:8, :]
        sm_ref[8:9, :] += jnp.sum(dxr, axis=0, keepdims=True)
        dxrb = dxr.astype(BF16)
        dxr_ref[...] = dxrb

        dh = _dot_nt(dxrb, win_ref[...].reshape(W, W)) + _dot_nt(dgbb, wy_ref[...].reshape(W, W))
        g, sc = gm_ref[...], mod_ref[1:2, :]
        _, xhat, r, n = _rms_fwd(x_ref[...], g, sc, mod_ref[0:1, :])
        dx, dsh, dsc, dg = _rms_bwd(dh, xhat, r, n, g, sc)
        acc_ref[0:1, :] += dsh
        acc_ref[1:2, :] += dsc
        acc_ref[6:7, :] += dg
        dx_ref[...] = dx1v + dx

        @pl.when(i == nb - 1)
        def _():
            sm_ref[7:8, :] = sm_ref[7:8, :] * jax.nn.sigmoid(-lam)

    rev = lambda i: (nb - 1 - i, 0)
    tile = pl.BlockSpec((tt, W), rev)
    halo = pl.BlockSpec((8, W), lambda i: (jnp.maximum((nb - 1 - i) * (tt // 8) - 1, 0), 0))
    row = pl.BlockSpec((1, W), lambda i: (0, 0))
    wblk = pl.BlockSpec((HEADS, hd, hd), lambda i: (0, 0, 0))
    bf16o = jax.ShapeDtypeStruct((S, W), BF16)
    return _pcall(
        body, name="lru_bwd", grid=(nb,),
        out_shape=(bf16o, bf16o, bf16o, jax.ShapeDtypeStruct((S, W), F32),
                   jax.ShapeDtypeStruct((16, W), F32), jax.ShapeDtypeStruct((HEADS, hd, hd), F32),
                   jax.ShapeDtypeStruct((HEADS, hd, hd), F32), jax.ShapeDtypeStruct((8, W), F32)),
        in_specs=[tile, tile, tile, tile, tile, tile, halo, tile, tile, tile, tile, tile,
                  pl.BlockSpec((8, W), lambda i: (0, 0)), row,
                  _wspec(g_wout), _wspec(g_wy), _wspec(g_win), pl.BlockSpec((CONV_W, W), lambda i: (0, 0)),
                  wblk, wblk, row],
        out_specs=(tile, tile, tile, tile, pl.BlockSpec((16, W), lambda i: (0, 0)), wblk, wblk,
                   pl.BlockSpec((8, W), lambda i: (0, 0))),
        scratch_shapes=[pltpu.VMEM((tt + 8, W), F32), pltpu.VMEM((tt + 8, W), F32), pltpu.VMEM((tt + 8, W), F32),
                        pltpu.VMEM((tt, W), F32), pltpu.VMEM((8, W), F32), pltpu.VMEM((8, W), F32)],
        operands=(dx1, y, x, xr0, gb, hs, hs, xc_, gate_r_, gate_i_, a_, mult_, mod_l, g_mix, g_wout, g_wy, g_win,
                  cw, wa, wx, lam),
        **comm)


def _pool_bwd(dx1, x, pooled, mod_l, g_mix, pw, ps, h2, du, a, dz):
    S, D = x.shape
    tm = min(TP, S)
    nb = S // tm
    ng = len(POOL_WINDOWS)
    gd = D // ng
    n = tm + 24
    f4 = du.shape[1] // NQ
    assert nb % NQ == 0
    kch = nb // NQ
    kr = S // kch

    def body(dx1_ref, dxh_ref, x_ref, pl_ref, mod_ref, gm_ref, pw_ref, ps_ref, h2_ref, du_ref, a_ref, dz_ref,
             dx_ref, acc_ref, dpw_ref, dw1_ref, dw2_ref, ext, b1, b2, acc1, acc2):
        i = pl.program_id(0)

        @pl.when(i == 0)
        def _():
            acc_ref[...] = jnp.zeros_like(acc_ref)
            dpw_ref[...] = jnp.zeros_like(dpw_ref)

        @pl.when(i % kch == 0)
        def _():
            acc1[...] = jnp.zeros_like(acc1)
            acc2[...] = jnp.zeros_like(acc2)

        acc1[...] += _dot_tn(h2_ref[...], du_ref[...])
        av = a_ref[...]
        acc2[...] += _dot_tn(av * av, dz_ref[...])

        gt, psv = mod_ref[2:3, :], ps_ref[...]
        wvec = _window_vec(D)
        dx1v = dx1_ref[...]
        pooled = pl_ref[...]
        mixed = _heads_dot(pooled, pw_ref, gd)
        acc_ref[2:3, :] += jnp.sum(dx1v * (mixed * psv), axis=0, keepdims=True)
        dy = dx1v * gt
        acc_ref[3:4, :] += jnp.sum(dy * mixed, axis=0, keepdims=True)
        dmix = (dy * psv).astype(BF16)
        for gi in range(ng):
            sl = slice(gi * gd, (gi + 1) * gd)
            dpw_ref[gi] += _dot_tn(pooled[:, sl], dmix[:, sl])
        dpooled = _heads_dot(dmix, pw_ref, gd, nt=True)
        dmix_h = (dxh_ref[...] * gt * psv).astype(BF16)
        dpooled_h = jnp.where(i < nb - 1, _heads_dot(dmix_h, pw_ref, gd, nt=True), 0.0)
        t1 = (lax.broadcasted_iota(jnp.int32, (tm, 1), 0) + (i * tm + 1)).astype(F32)
        t1h = (lax.broadcasted_iota(jnp.int32, (16, 1), 0) + ((i + 1) * tm + 1)).astype(F32)
        zeros8 = jnp.zeros((8, D), F32)
        ext[pl.ds(0, tm), :] = dpooled / jnp.minimum(t1, wvec)
        ext[pl.ds(tm, 16), :] = dpooled_h / jnp.minimum(t1h, wvec)
        ext[pl.ds(tm + 16, 8), :] = zeros8
        b1[pl.ds(tm + 16, 8), :] = zeros8
        b2[pl.ds(tm + 16, 8), :] = zeros8
        m = n - 8
        b1[pl.ds(0, m), :] = ext[pl.ds(0, m), :] + ext[pl.ds(1, m), :]
        b2[pl.ds(0, m), gd:] = b1[pl.ds(0, m), gd:] + b1[pl.ds(2, m), gd:]
        b1[pl.ds(0, m), 2 * gd:] = b2[pl.ds(0, m), 2 * gd:] + b2[pl.ds(4, m), 2 * gd:]
        b2[pl.ds(0, m), 3 * gd:] = b1[pl.ds(0, m), 3 * gd:] + b1[pl.ds(8, m), 3 * gd:]
        wsum = jnp.concatenate([b1[pl.ds(0, tm), 0:gd], b2[pl.ds(0, tm), gd:2 * gd],
                                b1[pl.ds(0, tm), 2 * gd:3 * gd], b2[pl.ds(0, tm), 3 * gd:]], axis=1)
        dh = wsum - dpooled
        g, sc = gm_ref[...], mod_ref[1:2, :]
        _, xhat, r, nn = _rms_fwd(x_ref[...], g, sc, mod_ref[0:1, :])
        dx, dsh, dsc, dg = _rms_bwd(dh, xhat, r, nn, g, sc)
        acc_ref[0:1, :] += dsh
        acc_ref[1:2, :] += dsc
        acc_ref[6:7, :] += dg
        dx_ref[...] = dx1v + dx

        @pl.when(i % kch == kch - 1)
        def _():
            dw1_ref[...] = acc1[...].astype(BF16)
            dw2_ref[...] = acc2[...].astype(BF16)

    tile = pl.BlockSpec((tm, D), lambda i: (i, 0))
    halo = pl.BlockSpec((16, D), lambda i: (jnp.minimum((i + 1) * (tm // 16), S // 16 - 1), 0))
    row = pl.BlockSpec((1, D), lambda i: (0, 0))
    wblk = pl.BlockSpec((ng, gd, gd), lambda i: (0, 0, 0))
    full_k = pl.BlockSpec((kr, D), lambda i: (i % kch, 0))
    part_k = pl.BlockSpec((kr, f4), lambda i: (i % kch, i // kch))
    return pl.pallas_call(
        body, name="pool_bwd", grid=(nb,),
        out_shape=(jax.ShapeDtypeStruct((S, D), F32), jax.ShapeDtypeStruct((8, D), F32),
                   jax.ShapeDtypeStruct((ng, gd, gd), F32), jax.ShapeDtypeStruct((NQ, D, f4), BF16),
                   jax.ShapeDtypeStruct((NQ, f4, D), BF16)),
        in_specs=[tile, halo, tile, tile, pl.BlockSpec((8, D), lambda i: (0, 0)), row, wblk, row,
                  full_k, part_k, part_k, full_k],
        out_specs=(tile, pl.BlockSpec((8, D), lambda i: (0, 0)), wblk,
                   pl.BlockSpec((None, D, f4), lambda i: (i // kch, 0, 0)),
                   pl.BlockSpec((None, f4, D), lambda i: (i // kch, 0, 0))),
        scratch_shapes=[pltpu.VMEM((n, D), F32), pltpu.VMEM((n, D), F32), pltpu.VMEM((n, D), F32),
                        pltpu.VMEM((D, f4), F32), pltpu.VMEM((f4, D), F32)],
        compiler_params=_cparams("arbitrary"),
    )(dx1, dx1, x, pooled, mod_l, g_mix, pw, ps, h2, du, a, dz)


def _shard_to_rows(w, D):
    return w.reshape(-1, D)


def _blockdiag_full(gq, na, hd):
    return gq.reshape(NQ, na, HEADS, hd // NQ, hd).transpose(1, 2, 0, 3, 4).reshape(na, HEADS, hd, hd)


def _blockdiag_by_chip(dw, D):
    na, _, hd, _ = dw.shape
    return dw.reshape(na, HEADS, NQ, hd // NQ, hd).transpose(2, 0, 1, 3, 4).reshape(NQ, -1, D)


def kernel(x, c, w_mod, b_mod, norm_mix_g, norm_ffn_g, lru_w_y, lru_b_y, lru_w_in, lru_b_in, lru_conv_w, lru_conv_b, lru_w_a, lru_b_a, lru_w_x, lru_b_x, lru_lambda, lru_w_out, lru_b_out, pool_w, pool_scale, ffn_w1, ffn_w2, final_norm_g, loss_target, m_w_mod, m_b_mod, m_norm_mix_g, m_norm_ffn_g, m_lru_w_y, m_lru_b_y, m_lru_w_in, m_lru_b_in, m_lru_conv_w, m_lru_conv_b, m_lru_w_a, m_lru_b_a, m_lru_w_x, m_lru_b_x, m_lru_lambda, m_lru_w_out, m_lru_b_out, m_pool_w, m_pool_scale, m_ffn_w1, m_ffn_w2, m_final_norm_g, v_w_mod, v_b_mod, v_norm_mix_g, v_norm_ffn_g, v_lru_w_y, v_lru_b_y, v_lru_w_in, v_lru_b_in, v_lru_conv_w, v_lru_conv_b, v_lru_w_a, v_lru_b_a, v_lru_w_x, v_lru_b_x, v_lru_lambda, v_lru_w_out, v_lru_b_out, v_pool_w, v_pool_scale, v_ffn_w1, v_ffn_w2, v_final_norm_g):
    S, D = x.shape[1], x.shape[2]
    L = w_mod.shape[0]
    NA = lru_w_y.shape[0]
    NB = pool_w.shape[0]
    F = ffn_w1.shape[2] * NQ
    f4 = F // NQ
    hd = D // HEADS
    Cs = w_mod.shape[2]
    assert L == DEPTH and Cs * NQ == N_MOD * D and D % 1024 == 0
    x2d = x.reshape(S, D)
    tgt = loss_target.reshape(S, D)
    q = 2 * lax.axis_index("x") + lax.axis_index("y")

    big = [ffn_w1, ffn_w2, lru_w_y, lru_w_in, lru_w_out, lru_w_a, lru_w_x, pool_w]
    rows = [int(w.size) // D for w in big]
    offs = [sum(rows[:k]) for k in range(len(big))]
    O_W1, O_W2, O_WY, O_WIN, O_WOUT, O_WA, O_WX, O_PW = offs
    R = sum(rows)
    dq = D // NQ
    s_w1 = [ffn_w1[i].astype(BF16) for i in range(L)]
    s_w2 = [ffn_w2[i].astype(BF16) for i in range(L)]
    s_wy = [lru_w_y[j].astype(BF16) for j in range(NA)]
    s_win = [lru_w_in[j].astype(BF16) for j in range(NA)]
    s_wout = [lru_w_out[j].astype(BF16) for j in range(NA)]
    s_tiny = jnp.concatenate([_shard_to_rows(w, D) for w in (lru_w_a, lru_w_x, pool_w)], axis=0).astype(BF16)

    cshard = lru_conv_w.reshape(-1)
    small_fwd = jnp.concatenate([c.reshape(-1), cshard, lru_b_a.reshape(-1), lru_b_x.reshape(-1),
                                 pool_scale.reshape(-1)])
    small_fwd = jnp.pad(small_fwd, (0, 8 * D - small_fwd.shape[0])).reshape(8, D)

    g_w1, g_w2 = [None] * L, [None] * L
    g_wy, g_win, g_wout = [None] * NA, [None] * NA, [None] * NA
    SG, g_wy[0], g_win[0], g_wout[0], g_tiny = _comm_only("gather_first", small=small_fwd,
                                                         gathers=(s_wy[0], s_win[0], s_wout[0], s_tiny))
    wa_full = _blockdiag_full(g_tiny[:, :rows[5]], NA, hd)
    wx_full = _blockdiag_full(g_tiny[:, rows[5]:rows[5] + rows[6]], NA, hd)
    pw_full = _blockdiag_full(g_tiny[:, rows[5] + rows[6]:], NB, hd)
    SGf = SG.reshape(NDEV, 8 * D)
    c_all = SGf[:, :D]
    SGq = SGf.reshape(NQ, 2, 8 * D)[:, 0]
    o = D
    n_cw = NA * CONV_W * D // NQ
    conv_w_full = SGq[:, o:o + n_cw].reshape(NQ, NA, CONV_W, D // NQ).transpose(1, 2, 0, 3).reshape(NA, CONV_W, D)
    o += n_cw
    n_b = NA * HEADS * hd // NQ
    b_a_full = SGq[:, o:o + n_b].reshape(NQ, NA, HEADS, hd // NQ).transpose(1, 2, 0, 3).reshape(NA, 1, D)
    o += n_b
    b_x_full = SGq[:, o:o + n_b].reshape(NQ, NA, HEADS, hd // NQ).transpose(1, 2, 0, 3).reshape(NA, 1, D)
    o += n_b
    n_ps = NB * D // NQ
    pool_scale_full = SGq[:, o:o + n_ps].reshape(NQ, NB, D // NQ).transpose(1, 0, 2).reshape(NB, 1, D)


    b_mod_sh = lax.dynamic_slice_in_dim(b_mod, q * Cs, Cs, axis=1).reshape(L, 1, Cs)
    modpart = _mod_part(c_all, w_mod, b_mod_sh)
    modq = _exchange_mod(modpart.transpose(1, 0, 2))
    mod = modq.transpose(1, 0, 2).reshape(L, N_MOD, D)
    mod = jnp.pad(mod, ((0, 0), (0, 8 - N_MOD), (0, 0)))

    saved = []
    xcur = x2d
    for i in range(L):
        j = i // 2
        gm = norm_mix_g[i].reshape(1, D)
        gf = norm_ffn_g[i].reshape(1, D)
        if i % 2 == 0:
            h, gb, xr0, hs, p, y, x1, xc_s, gr_s, gi_s, a_sv, mu_s, g_w1[i], g_w2[i] = _lru_fwd(
                xcur, mod[i], gm, g_wy[j], g_win[j], lru_b_y[j].reshape(1, D), lru_b_in[j].reshape(1, D),
                conv_w_full[j], lru_conv_b[j].reshape(1, D), wa_full[j], b_a_full[j], wx_full[j], b_x_full[j],
                lru_lambda[j].reshape(1, D), g_wout[j], lru_b_out[j].reshape(1, D), gathers=(s_w1[i], s_w2[i]))
            h2, a, z, x2, g_w1[i + 1], g_w2[i + 1] = _ffn_fwd(x1, mod[i], gf, g_w1[i], g_w2[i],
                                                              gathers=(s_w1[i + 1], s_w2[i + 1]))
            saved.append(dict(x=xcur, h=h, gb=gb, xr0=xr0, hs=hs, p=p, y=y, x1=x1, h2=h2, a=a, z=z,
                              lru=(xc_s, gr_s, gi_s, a_sv, mu_s)))
        else:
            if j + 1 < NA:
                pooled, x1, h2, a, z, x2, g_wy[j + 1], g_win[j + 1], g_wout[j + 1] = _pool_mix_ffn_fwd(
                    xcur, mod[i], gm, pw_full[j], pool_scale_full[j], gf, g_w1[i], g_w2[i],
                    gathers=(s_wy[j + 1], s_win[j + 1], s_wout[j + 1]))
            else:
                pooled, x1, h2, a, z, x2 = _pool_mix_ffn_fwd(xcur, mod[i], gm, pw_full[j], pool_scale_full[j], gf,
                                                             g_w1[i], g_w2[i])
            saved.append(dict(x=xcur, pooled=pooled, x1=x1, h2=h2, a=a, z=z))
        xcur = x2

    dx = xcur
    qv = q.reshape(1).astype(jnp.int32)
    ppack = lax.empty((R, D), F32)
    psib = lax.empty((R, D), F32)
    pending, summed = [], []

    def comm_args():
        kw = {}
        if pending:
            kw["scatters"] = tuple(dw for dw, _ in pending)
        if summed:
            kw["sib"] = (ppack, psib, tuple(summed))
        return kw

    def after_host(extra):
        nonlocal ppack, psib, pending, summed
        had_sib = bool(summed)
        summed = []
        for (dw, off), rb in zip(pending, extra[:len(pending)]):
            ppack = _sum_into(ppack, dw, rb, off, qv)
            summed.append((off, dw.shape[1]))
        if had_sib:
            psib = extra[len(pending)]
        pending = []

    dmod_rows = [None] * L
    dg_mix = [None] * L
    dg_ffn = [None] * L
    d_small = {}
    dwa_l, dwx_l, dpw_l = [None] * NA, [None] * NA, [None] * NB
    for i in reversed(range(L)):
        j = i // 2
        sv = saved[i]
        gm = norm_mix_g[i].reshape(1, D)
        gf = norm_ffn_g[i].reshape(1, D)
        head = (final_norm_g.reshape(1, D), tgt) if i == L - 1 else None
        outs = _ffn_bwd(dx, sv["x1"], sv["a"], sv["z"], mod[i], gf, g_w1[i], g_w2[i], head=head, **comm_args())
        dx1, du, dz, facc = outs[:4]
        after_host(outs[4:])
        if head:
            loss = lax.psum(0.5 * jnp.sum(facc[0]) / D, ("x", "y", "c"))
            d_final_g = facc[1]
        if i % 2 == 0:
            pending.append((_dw_blocked(sv["h2"], du, False, False, "dw1"), O_W1 + i * D))
            pending.append((_dw_blocked(sv["a"], dz, True, True, "dw2"), O_W2 + i * f4))
            outs = _lru_bwd(dx1, sv["y"], sv["x"], sv["xr0"], sv["gb"], sv["hs"], *sv["lru"], mod[i], gm, g_wout[j],
                            g_wy[j], g_win[j], conv_w_full[j], wa_full[j], wx_full[j], lru_lambda[j].reshape(1, D),
                            **comm_args())
            dyp, dgb, dxr, dx, sm, dwa, dwx, macc = outs[:8]
            after_host(outs[8:])
            dwa_l[j], dwx_l[j] = dwa, dwx
            if i == 0:
                tiny = jnp.concatenate([_blockdiag_by_chip(jnp.stack(dwa_l), D), _blockdiag_by_chip(jnp.stack(dwx_l), D),
                                        _blockdiag_by_chip(jnp.stack(dpw_l), D)], axis=1).astype(BF16)
                pending.append((tiny, O_WA))
            if i == 0:
                outs = _dw_whole(sv["h"], [dgb, dxr], "dwy_dwin", **comm_args())
                after_host(outs[2:])
                pending.append((outs[0], O_WY + j * dq))
                pending.append((outs[1], O_WIN + j * dq))
                outs = _dw_whole(sv["p"], [dyp], "dwout", **comm_args())
                after_host(outs[1:])
                pending.append((outs[0], O_WOUT + j * dq))
            else:
                outs = _dw_whole(sv["p"], [dyp], "dwout", **comm_args())
                after_host(outs[1:])
                pending.append((outs[0], O_WOUT + j * dq))
                outs = _dw_whole(sv["h"], [dgb, dxr], "dwy_dwin", **comm_args())
                after_host(outs[2:])
                pending.append((outs[0], O_WY + j * dq))
                pending.append((outs[1], O_WIN + j * dq))
            d_small[("lru", j)] = (sm, macc[3])
            dgt_m = macc[2]
        else:
            dx, macc, dpw, dw1, dw2 = _pool_bwd(dx1, sv["x"], sv["pooled"], mod[i], gm, pw_full[j], pool_scale_full[j],
                                                sv["h2"], du, sv["a"], dz)
            pending.append((dw1, O_W1 + i * D))
            pending.append((dw2, O_W2 + i * f4))
            dpw_l[j] = dpw
            d_small[("pool", j)] = macc[3]
            dgt_m = macc[2]
        dmod_rows[i] = jnp.stack([macc[0], macc[1], dgt_m, facc[3], facc[4], facc[5]])
        dg_mix[i] = macc[6]
        dg_ffn[i] = facc[7]
    grad_x = dx.reshape(x.shape)

    lru_sm = [d_small[("lru", j)] for j in range(NA)]
    small_rows = [jnp.stack(dmod_rows).reshape(L * N_MOD, D), jnp.stack(dg_mix), jnp.stack(dg_ffn),
                  jnp.stack([s[0][9] for s in lru_sm]), jnp.stack([s[0][8] for s in lru_sm]),
                  jnp.stack([s[0][4] for s in lru_sm]), jnp.stack([s[0][7] for s in lru_sm]),
                  jnp.stack([s[1] for s in lru_sm]),
                  jnp.stack([s[0][0:CONV_W] for s in lru_sm]).reshape(NA * CONV_W, D),
                  jnp.stack([s[0][5] for s in lru_sm]), jnp.stack([s[0][6] for s in lru_sm]),
                  jnp.stack([d_small[("pool", j)] for j in range(NB)]), d_final_g.reshape(1, D)]
    small_g = jnp.concatenate(small_rows, axis=0)
    n_small = small_g.shape[0]
    assert n_small <= SMALL_ROWS
    small_g = jnp.pad(small_g, ((0, SMALL_ROWS - n_small), (0, 0)))

    outs = _comm_only("scatter_last", small=small_g, reduce_small=True, **comm_args())
    sg_all, sg_sum = outs[:2]
    after_host(outs[2:])
    psum_mine = ppack
    psum_sib = _comm_only("sibling_last", sib=(ppack, psib, tuple(summed)))[0]

    def big_update(w, m, v, off, name):
        shp = w.shape
        g, dl, m2, v2 = _adam_rows(w.reshape(-1, D), m.reshape(-1, D), v.reshape(-1, D), psum_mine, psum_sib, off, name)
        return g.reshape(shp), dl.reshape(shp), m2.reshape(shp), v2.reshape(shp)

    res = {}
    res["ffn_w1"] = big_update(ffn_w1, m_ffn_w1, v_ffn_w1, O_W1, "adam_w1")
    res["ffn_w2"] = big_update(ffn_w2, m_ffn_w2, v_ffn_w2, O_W2, "adam_w2")
    res["lru_w_y"] = big_update(lru_w_y, m_lru_w_y, v_lru_w_y, O_WY, "adam_wy")
    res["lru_w_in"] = big_update(lru_w_in, m_lru_w_in, v_lru_w_in, O_WIN, "adam_win")
    res["lru_w_out"] = big_update(lru_w_out, m_lru_w_out, v_lru_w_out, O_WOUT, "adam_wout")

    def tiny_parts(w, off):
        n = int(w.size) // D
        return psum_mine[off:off + n].reshape(w.shape), psum_sib[off:off + n].reshape(w.shape)

    tiny_items = [("lru_w_a", lru_w_a, m_lru_w_a, v_lru_w_a) + tiny_parts(lru_w_a, O_WA),
                  ("lru_w_x", lru_w_x, m_lru_w_x, v_lru_w_x) + tiny_parts(lru_w_x, O_WX),
                  ("pool_w", pool_w, m_pool_w, v_pool_w) + tiny_parts(pool_w, O_PW)]

    dmod_all = sg_all[:, :L * N_MOD, :].reshape(NDEV, L, N_MOD * D)
    dmod_sh = lax.dynamic_slice_in_dim(dmod_all, q * Cs, Cs, axis=2).transpose(1, 0, 2)
    res["w_mod"] = _wmod_update(c_all.T, dmod_sh, w_mod, m_w_mod, v_w_mod)

    r0 = 0
    by_rows, names_a = [], []
    for name, w, m, v in (("b_mod", b_mod, m_b_mod, v_b_mod), ("norm_mix_g", norm_mix_g, m_norm_mix_g, v_norm_mix_g),
                          ("norm_ffn_g", norm_ffn_g, m_norm_ffn_g, v_norm_ffn_g),
                          ("lru_b_y", lru_b_y, m_lru_b_y, v_lru_b_y), ("lru_b_in", lru_b_in, m_lru_b_in, v_lru_b_in),
                          ("lru_conv_b", lru_conv_b, m_lru_conv_b, v_lru_conv_b),
                          ("lru_lambda", lru_lambda, m_lru_lambda, v_lru_lambda),
                          ("lru_b_out", lru_b_out, m_lru_b_out, v_lru_b_out)):
        by_rows.append((w, m, v, r0))
        names_a.append(name)
        r0 += int(w.size) // D
    g_conv_w = lax.dynamic_slice_in_dim(sg_sum[r0:r0 + NA * CONV_W].reshape(NA, CONV_W, D), q * dq, dq, axis=2)
    r0 += NA * CONV_W
    g_b_a = lax.dynamic_slice_in_dim(sg_sum[r0:r0 + NA].reshape(NA, HEADS, hd), q * (hd // NQ), hd // NQ, axis=2)
    r0 += NA
    g_b_x = lax.dynamic_slice_in_dim(sg_sum[r0:r0 + NA].reshape(NA, HEADS, hd), q * (hd // NQ), hd // NQ, axis=2)
    r0 += NA
    g_ps = lax.dynamic_slice_in_dim(sg_sum[r0:r0 + NB], q * dq, dq, axis=1)
    r0 += NB
    by_rows.append((final_norm_g.reshape(1, D), m_final_norm_g.reshape(1, D), v_final_norm_g.reshape(1, D), r0))
    names_a.append("final_norm_g")
    sliced = [(lru_conv_w, m_lru_conv_w, v_lru_conv_w, g_conv_w), (lru_b_a, m_lru_b_a, v_lru_b_a, g_b_a),
              (lru_b_x, m_lru_b_x, v_lru_b_x, g_b_x), (pool_scale, m_pool_scale, v_pool_scale, g_ps)]
    res_a, res_b, res_c = _adam_small(sg_sum, by_rows, sliced, [t[1:] for t in tiny_items])
    for name, r in zip(names_a, res_a):
        res[name] = r
    for t, r in zip(tiny_items, res_c):
        res[t[0]] = r
    res["final_norm_g"] = tuple(a.reshape(D) for a in res["final_norm_g"])
    for name, (_, _, _, g), r in zip(("lru_conv_w", "lru_b_a", "lru_b_x", "pool_scale"), sliced, res_b):
        res[name] = (g,) + r

    order = ["w_mod", "b_mod", "norm_mix_g", "norm_ffn_g", "lru_w_y", "lru_b_y", "lru_w_in", "lru_b_in", "lru_conv_w",
             "lru_conv_b", "lru_w_a", "lru_b_a", "lru_w_x", "lru_b_x", "lru_lambda", "lru_w_out", "lru_b_out", "pool_w",
             "pool_scale", "ffn_w1", "ffn_w2", "final_norm_g"]
    return (loss, grad_x, *[res[n][0] for n in order], *[res[n][1] for n in order],
            *[res[n][2] for n in order], *[res[n][3] for n in order])
```

```python
import jax
import jax.numpy as jnp
from jax import lax
from jax.experimental import pallas as pl
from jax.experimental.pallas import tpu as pltpu

F32 = jnp.float32
BF16 = jnp.bfloat16
MESH = pl.DeviceIdType.MESH

NQ = 4
NDEV = 8
DEPTH = 4
N_MOD = 6
HEADS = 4
CONV_W = 4
POOL_WINDOWS = (2, 4, 8, 16)
LRU_C = 8.0
EPS = 1e-6
ADAM_LR, ADAM_B1, ADAM_B2, ADAM_EPS, ADAM_WD, ADAM_STEP = 0.001, 0.9, 0.999, 1e-08, 0.01, 10

TM = 512
TT = 256
TP = 256
TPF = 512
TK = 2048
SMALL_ROWS = 64
FORWARD_STEPS = 4
VMEM_LIMIT = 60 * 1024 * 1024


def _cparams(*sem):
    return pltpu.CompilerParams(dimension_semantics=tuple(sem), vmem_limit_bytes=VMEM_LIMIT)


def _dot(a, b):
    return jnp.dot(a, b, preferred_element_type=F32)


def _dot_nt(a, b):
    return lax.dot_general(a, b, (((1,), (1,)), ((), ())), preferred_element_type=F32)


def _dot_tn(a, b):
    return lax.dot_general(a, b, (((0,), (0,)), ((), ())), preferred_element_type=F32)


def _resident(shape, index_map):
    return pl.BlockSpec(shape, index_map, pipeline_mode=pl.Buffered(1))


def _rms_fwd(x, g, sc, sh):
    r = lax.rsqrt(jnp.mean(x * x, axis=-1, keepdims=True) + EPS)
    xhat = x * r
    n = xhat * g
    return n * (1.0 + sc) + sh, xhat, r, n


def _rms_bwd(dh, xhat, r, n, g, sc):
    dsh = jnp.sum(dh, axis=0, keepdims=True)
    dsc = jnp.sum(dh * n, axis=0, keepdims=True)
    dn = dh * (1.0 + sc)
    dg = jnp.sum(dn * xhat, axis=0, keepdims=True)
    dxh = dn * g
    dx = r * (dxh - xhat * jnp.mean(dxh * xhat, axis=-1, keepdims=True))
    return dx, dsh, dsc, dg


_GELU_K = 0.7978845608028654
_GELU_C = 0.044715


def _gelu(x):
    t = jnp.tanh(_GELU_K * (x + _GELU_C * x * x * x))
    return 0.5 * x * (1.0 + t), t


def _gelu_grad(x, t):
    return 0.5 * (1.0 + t) + 0.5 * x * (1.0 - t * t) * (_GELU_K * (1.0 + 3.0 * _GELU_C * x * x))


def _neg_expm1(y, exp_y):
    series = -(y * (1.0 + y * (0.5 + y * (1.0 / 6.0))))
    return jnp.where(y > -(1.0 / 64.0), series, 1.0 - exp_y)


def _zero_first(ref):
    @pl.when(pl.program_id(0) == 0)
    def _():
        ref[...] = jnp.zeros_like(ref)


def _my_pos():
    return lax.axis_index("x"), lax.axis_index("y"), lax.axis_index("c")


def _dev_index(x, y, c):
    return 4 * x + 2 * y + c


def _chip_peers(x, y):
    return [(1 - x, y), (x, 1 - y), (1 - x, 1 - y)]


def _all_peers(x, y, c):
    return [(px, py, c) for (px, py) in _chip_peers(x, y)] + [(x, y, 1 - c)] + \
           [(px, py, 1 - c) for (px, py) in _chip_peers(x, y)]


def _comm_run(phase, x, y, c, gathers, scatters, sib, send, recv, loc):
    q = 2 * x + y
    peers = _chip_peers(x, y)
    sibling = (x, y, 1 - c)

    def rcopy(src, dst, s, dev):
        return pltpu.make_async_remote_copy(src, dst, send.at[s], recv.at[s], device_id=dev, device_id_type=MESH)

    s = 0
    for gi, (src, dst) in enumerate(gathers):
        half = src.shape[0] // 2
        mine, other = pl.ds(c * half, half), pl.ds((1 - c) * half, half)
        own = pltpu.make_async_copy(src, dst.at[q], loc.at[gi])
        if phase == "start":
            own.start()
        elif phase == "finish":
            own.wait()
        for (px, py) in peers:
            pq = 2 * px + py
            s_ici, s_fwd = s, s + 1
            s += 2
            if phase == "start":
                rcopy(src.at[mine], dst.at[q].at[mine], s_ici, (px, py, c)).start()
            elif phase == "forward":
                rcopy(src.at[mine], dst.at[pq].at[mine], s_ici, (px, py, c)).wait_recv()
                rcopy(dst.at[pq].at[mine], dst.at[pq].at[mine], s_fwd, sibling).start()
            else:
                rcopy(dst.at[pq].at[other], dst.at[pq].at[other], s_fwd, sibling).wait_recv()
                rcopy(src.at[mine], dst.at[q].at[mine], s_ici, (px, py, c)).wait_send()
                rcopy(dst.at[pq].at[mine], dst.at[pq].at[mine], s_fwd, sibling).wait_send()
    direct = []
    for (src, dst) in scatters:
        for k, (px, py) in enumerate(peers):
            direct.append((src.at[2 * px + py], dst.at[k], (px, py, c)))
    if sib is not None:
        src, dst, ranges = sib
        for (off, rows) in ranges:
            direct.append((src.at[pl.ds(off, rows)], dst.at[pl.ds(off, rows)], sibling))
    if phase == "start":
        for k, (a, b, dev) in enumerate(direct):
            rcopy(a, b, s + k, dev).start()
    elif phase == "finish":
        for k, (a, b, dev) in enumerate(direct):
            rcopy(a, b, s + k, dev).wait_recv()
        for k, (a, b, dev) in enumerate(direct):
            rcopy(a, b, s + k, dev).wait_send()


def _comm_shapes(gathers, scatters, sib):
    assert all(g.shape[0] % 32 == 0 for g in gathers)
    cin = list(gathers) + list(scatters) + ([sib[0], sib[1]] if sib else [])
    cout = [jax.ShapeDtypeStruct((NQ,) + g.shape, g.dtype) for g in gathers] + \
           [jax.ShapeDtypeStruct((3,) + s.shape[1:], s.dtype) for s in scatters] + \
           ([jax.ShapeDtypeStruct(sib[1].shape, sib[1].dtype)] if sib else [])
    n_rem = 6 * len(gathers) + 3 * len(scatters) + (len(sib[2]) if sib else 0)
    sems = [pltpu.SemaphoreType.DMA((max(n_rem, 1),)), pltpu.SemaphoreType.DMA((max(n_rem, 1),)),
            pltpu.SemaphoreType.DMA((max(len(gathers), 1),))]
    return cin, cout, sems


def _pcall(body, *, name, grid, in_specs, out_specs, out_shape, operands, scratch_shapes=(),
           gathers=(), scatters=(), sib=None):
    assert len(grid) == 1
    out_shape, out_specs = tuple(out_shape), tuple(out_specs)
    if not (gathers or scatters or sib):
        return pl.pallas_call(body, name=name, grid=grid, in_specs=list(in_specs), out_specs=out_specs,
                              out_shape=out_shape, scratch_shapes=list(scratch_shapes),
                              compiler_params=_cparams("arbitrary"))(*operands)
    cin, cout, sems = _comm_shapes(gathers, scatters, sib)
    n_in, n_cin, n_out, n_cout, n_scr = len(operands), len(cin), len(out_shape), len(cout), len(scratch_shapes)
    ng, ns = len(gathers), len(scatters)
    nsteps = grid[0]

    def wrapped(*refs):
        ins = refs[:n_in]
        cins = refs[n_in:n_in + n_cin]
        o0 = n_in + n_cin
        outs = refs[o0:o0 + n_out]
        couts = refs[o0 + n_out:o0 + n_out + n_cout]
        s0 = o0 + n_out + n_cout
        scr = refs[s0:s0 + n_scr]
        send, recv, loc = refs[s0 + n_scr:s0 + n_scr + 3]
        x, y, c = _my_pos()

        def run(phase):
            g = [(cins[k], couts[k]) for k in range(ng)]
            sc = [(cins[ng + k], couts[ng + k]) for k in range(ns)]
            sb = (cins[ng + ns], couts[ng + ns], sib[2]) if sib else None
            _comm_run(phase, x, y, c, g, sc, sb, send, recv, loc)

        @pl.when(pl.program_id(0) == 0)
        def _():
            run("start")

        if ng:
            @pl.when(pl.program_id(0) == max(nsteps - FORWARD_STEPS, 0))
            def _():
                run("forward")

        body(*ins, *outs, *scr)

        @pl.when(pl.program_id(0) == nsteps - 1)
        def _():
            run("finish")

    anyspec = pl.BlockSpec(memory_space=pl.ANY)
    aliases = {n_in + ng + ns + 1: n_out + ng + ns} if sib else {}
    return pl.pallas_call(
        wrapped, name=name, grid=grid,
        in_specs=list(in_specs) + [anyspec] * n_cin, out_specs=out_specs + (anyspec,) * n_cout,
        out_shape=out_shape + tuple(cout), scratch_shapes=list(scratch_shapes) + sems,
        input_output_aliases=aliases,
        compiler_params=pltpu.CompilerParams(dimension_semantics=("arbitrary",), vmem_limit_bytes=VMEM_LIMIT,
                                             has_side_effects=True),
    )(*operands, *cin)


def _comm_only(name, small=None, reduce_small=False, gathers=(), scatters=(), sib=None):
    cin, cout, sems = _comm_shapes(gathers, scatters, sib)
    n_cin, n_cout = len(cin), len(cout)
    ng, ns = len(gathers), len(scatters)
    n_sm_in = 1 if small is not None else 0
    n_sm_out = (2 if reduce_small else 1) if small is not None else 0

    def body(*refs):
        sm_in = refs[:n_sm_in]
        cins = refs[n_sm_in:n_sm_in + n_cin]
        o0 = n_sm_in + n_cin
        sm_out = refs[o0:o0 + n_sm_out]
        couts = refs[o0 + n_sm_out:o0 + n_sm_out + n_cout]
        s0 = o0 + n_sm_out + n_cout
        send, recv, loc = refs[s0:s0 + 3]
        x, y, c = _my_pos()
        g = [(cins[k], couts[k]) for k in range(ng)]
        sc = [(cins[ng + k], couts[ng + k]) for k in range(ns)]
        sb = (cins[ng + ns], couts[ng + ns], sib[2]) if sib else None
        _comm_run("start", x, y, c, g, sc, sb, send, recv, loc)
        if small is not None:
            sm_send, sm_recv = refs[s0 + 3:s0 + 5]
            small_ref, sg_ref = sm_in[0], sm_out[0]
            me = _dev_index(x, y, c)
            sg_ref[me] = small_ref[...]
            peers = _all_peers(x, y, c)
            sm = [pltpu.make_async_remote_copy(small_ref, sg_ref.at[me], sm_send.at[k], sm_recv.at[k],
                                               device_id=peer, device_id_type=MESH) for k, peer in enumerate(peers)]
            for cp in sm:
                cp.start()
            for k, (px, py, pc) in enumerate(peers):
                pltpu.make_async_remote_copy(small_ref, sg_ref.at[_dev_index(px, py, pc)], sm_send.at[k], sm_recv.at[k],
                                             device_id=(px, py, pc), device_id_type=MESH).wait_recv()
            if reduce_small:
                acc = sg_ref[0]
                for d in range(1, NDEV):
                    acc = acc + sg_ref[d]
                sm_out[1][...] = acc
            for cp in sm:
                cp.wait_send()
        if ng:
            _comm_run("forward", x, y, c, g, sc, sb, send, recv, loc)
        _comm_run("finish", x, y, c, g, sc, sb, send, recv, loc)

    anyspec = pl.BlockSpec(memory_space=pl.ANY)
    vspec = pl.BlockSpec(memory_space=pltpu.VMEM)
    sm_shapes = []
    if small is not None:
        sm_shapes.append(jax.ShapeDtypeStruct((NDEV,) + small.shape, small.dtype))
        if reduce_small:
            sm_shapes.append(jax.ShapeDtypeStruct(small.shape, small.dtype))
        sems = sems + [pltpu.SemaphoreType.DMA((NDEV - 1,)), pltpu.SemaphoreType.DMA((NDEV - 1,))]
    aliases = {n_sm_in + ng + ns + 1: n_sm_out + ng + ns} if sib else {}
    return pl.pallas_call(
        body, name=name,
        in_specs=[vspec] * n_sm_in + [anyspec] * n_cin,
        out_specs=tuple([vspec] * n_sm_out + [anyspec] * n_cout),
        out_shape=tuple(sm_shapes + cout), scratch_shapes=sems, input_output_aliases=aliases,
        compiler_params=pltpu.CompilerParams(has_side_effects=True),
    )(*([small] if small is not None else []), *cin)


def _exchange_mod(modpart):
    _, L, Cs = modpart.shape

    def body(part_ref, out_ref, send, recv):
        x, y, c = _my_pos()
        q = 2 * x + y
        me = _dev_index(x, y, c)
        out_ref[q] = part_ref[me]
        sends = []
        for k, (px, py) in enumerate(_chip_peers(x, y)):
            cp = pltpu.make_async_remote_copy(part_ref.at[_dev_index(px, py, c)], out_ref.at[q], send.at[k], recv.at[k],
                                              device_id=(px, py, c), device_id_type=MESH)
            cp.start()
            sends.append(cp)
        for k, (px, py) in enumerate(_chip_peers(x, y)):
            pltpu.make_async_remote_copy(part_ref.at[me], out_ref.at[2 * px + py], send.at[k], recv.at[k],
                                         device_id=(px, py, c), device_id_type=MESH).wait_recv()
        for cp in sends:
            cp.wait_send()

    return pl.pallas_call(
        body, name="exchange_mod",
        out_shape=jax.ShapeDtypeStruct((NQ, L, Cs), modpart.dtype),
        in_specs=[pl.BlockSpec(memory_space=pltpu.VMEM)],
        out_specs=pl.BlockSpec(memory_space=pltpu.VMEM),
        scratch_shapes=[pltpu.SemaphoreType.DMA((3,)), pltpu.SemaphoreType.DMA((3,))],
        compiler_params=pltpu.CompilerParams(has_side_effects=True),
    )(modpart)


def _mod_part(c_all, w_mod, b_mod_sh):
    L, D, Cs = w_mod.shape
    tn = 512 if Cs % 512 == 0 else Cs

    def body(c_ref, w_ref, b_ref, o_ref):
        cv = c_ref[...]
        cond = cv * jax.nn.sigmoid(cv)
        o_ref[...] = jnp.dot(cond, w_ref[...], preferred_element_type=F32, precision=lax.Precision.HIGHEST) + b_ref[...]

    return pl.pallas_call(
        body, name="mod_part", grid=(L, Cs // tn),
        out_shape=jax.ShapeDtypeStruct((L, NDEV, Cs), F32),
        in_specs=[pl.BlockSpec((NDEV, D), lambda i, j: (0, 0)),
                  pl.BlockSpec((None, D, tn), lambda i, j: (i, 0, j)),
                  pl.BlockSpec((None, 1, tn), lambda i, j: (i, 0, j))],
        out_specs=pl.BlockSpec((None, NDEV, tn), lambda i, j: (i, 0, j)),
        compiler_params=_cparams("parallel", "parallel"),
    )(c_all, w_mod, b_mod_sh)


def _adam(w, g, m, v):
    m2 = ADAM_B1 * m + (1.0 - ADAM_B1) * g
    v2 = ADAM_B2 * v + (1.0 - ADAM_B2) * (g * g)
    m_hat = m2 / (1.0 - ADAM_B1 ** ADAM_STEP)
    v_hat = v2 / (1.0 - ADAM_B2 ** ADAM_STEP)
    delta = -ADAM_LR * (m_hat / (jnp.sqrt(v_hat) + ADAM_EPS) + ADAM_WD * w)
    return delta, m2, v2


def _wmod_update(c_all_t, dmod_sh, w, m, v):
    L, D, Cs = w.shape
    td = 512 if D % 512 == 0 else D

    def body(ct_ref, d_ref, w_ref, m_ref, v_ref, g_ref, dl_ref, m2_ref, v2_ref):
        cv = ct_ref[...]
        cond = cv * jax.nn.sigmoid(cv)
        g = cond[:, 0:1] * d_ref[0:1, :]
        for b in range(1, NDEV):
            g = g + cond[:, b:b + 1] * d_ref[b:b + 1, :]
        g_ref[...] = g
        dl_ref[...], m2_ref[...], v2_ref[...] = _adam(w_ref[...], g, m_ref[...], v_ref[...])

    blk = pl.BlockSpec((None, td, Cs), lambda i, j: (i, j, 0))
    out = jax.ShapeDtypeStruct((L, D, Cs), F32)
    return pl.pallas_call(
        body, name="wmod_update", grid=(L, D // td),
        out_shape=(out, out, out, out),
        in_specs=[pl.BlockSpec((td, NDEV), lambda i, j: (j, 0)),
                  pl.BlockSpec((None, NDEV, Cs), lambda i, j: (i, 0, 0)), blk, blk, blk],
        out_specs=(blk, blk, blk, blk),
        compiler_params=_cparams("parallel", "parallel"),
    )(c_all_t, dmod_sh, w, m, v)


def _adam_rows(w, m, v, pa, pb, row_off, name):
    rows, C = w.shape
    tr = 512 if rows % 512 == 0 else (128 if rows % 128 == 0 else rows)
    assert row_off % tr == 0
    ob = row_off // tr

    def body(w_ref, m_ref, v_ref, pa_ref, pb_ref, g_ref, dl_ref, m2_ref, v2_ref):
        g = pa_ref[...] + pb_ref[...]
        g_ref[...] = g
        dl_ref[...], m2_ref[...], v2_ref[...] = _adam(w_ref[...], g, m_ref[...], v_ref[...])

    blk = pl.BlockSpec((tr, C), lambda i: (i, 0))
    pblk = pl.BlockSpec((tr, C), lambda i: (ob + i, 0))
    out = jax.ShapeDtypeStruct((rows, C), F32)
    return pl.pallas_call(
        body, name=name, grid=(rows // tr,), out_shape=(out, out, out, out),
        in_specs=[blk, blk, blk, pblk, pblk], out_specs=(blk, blk, blk, blk),
        compiler_params=_cparams("parallel"),
    )(w, m, v, pa, pb)


def _adam_small(sg_sum, by_rows, sliced, pairs):
    D = sg_sum.shape[1]
    na, nb, nc = len(by_rows), len(sliced), len(pairs)

    def body(*refs):
        sg = refs[0]
        ins_a = [refs[1 + 3 * t:4 + 3 * t] for t in range(na)]
        p = 1 + 3 * na
        ins_b = [refs[p + 4 * t:p + 4 * t + 4] for t in range(nb)]
        p += 4 * nb
        ins_c = [refs[p + 5 * t:p + 5 * t + 5] for t in range(nc)]
        p += 5 * nc
        outs_a = [refs[p + 4 * t:p + 4 * t + 4] for t in range(na)]
        p += 4 * na
        outs_b = [refs[p + 3 * t:p + 3 * t + 3] for t in range(nb)]
        p += 3 * nb
        outs_c = [refs[p + 4 * t:p + 4 * t + 4] for t in range(nc)]
        for (w_ref, m_ref, v_ref, ga_ref, gb_ref), (g_ref, dl_ref, m2_ref, v2_ref) in zip(ins_c, outs_c):
            g = ga_ref[...] + gb_ref[...]
            g_ref[...] = g
            dl_ref[...], m2_ref[...], v2_ref[...] = _adam(w_ref[...], g, m_ref[...], v_ref[...])
        for (w_ref, m_ref, v_ref), (g_ref, dl_ref, m2_ref, v2_ref), (w, _, _, row0) in zip(ins_a, outs_a, by_rows):
            n, k = w.shape[0], w.shape[1] // D
            pieces = [(slice(0, n), slice(0, D), slice(row0, row0 + n))] if k == 1 else \
                     [(slice(i, i + 1), slice(kk * D, (kk + 1) * D), slice(row0 + i * k + kk, row0 + i * k + kk + 1))
                      for i in range(n) for kk in range(k)]
            for rs, cs, gs in pieces:
                g = sg[gs, :]
                g_ref[rs, cs] = g
                dl_ref[rs, cs], m2_ref[rs, cs], v2_ref[rs, cs] = _adam(w_ref[rs, cs], g, m_ref[rs, cs], v_ref[rs, cs])
        for (w_ref, m_ref, v_ref, g_ref), (dl_ref, m2_ref, v2_ref) in zip(ins_b, outs_b):
            dl_ref[...], m2_ref[...], v2_ref[...] = _adam(w_ref[...], g_ref[...], m_ref[...], v_ref[...])

    operands = [sg_sum] + [a for t in by_rows for a in t[:3]] + [a for t in sliced for a in t] + \
               [a for t in pairs for a in t]
    out_shape = [jax.ShapeDtypeStruct(t[0].shape, F32) for t in by_rows for _ in range(4)] + \
                [jax.ShapeDtypeStruct(t[0].shape, F32) for t in sliced for _ in range(3)] + \
                [jax.ShapeDtypeStruct(t[0].shape, F32) for t in pairs for _ in range(4)]
    outs = pl.pallas_call(body, name="adam_small", out_shape=tuple(out_shape))(*operands)
    res_a = [tuple(outs[4 * t:4 * t + 4]) for t in range(na)]
    o = 4 * na
    res_b = [tuple(outs[o + 3 * t:o + 3 * t + 3]) for t in range(nb)]
    o += 3 * nb
    res_c = [tuple(outs[o + 4 * t:o + 4 * t + 4]) for t in range(nc)]
    return res_a, res_b, res_c


def _sum_into(ppack, dw, rb, off, qv):
    _, rows, D = dw.shape
    tr = 256 if rows % 256 == 0 else 128
    assert rows % tr == 0 and off % tr == 0
    ob = off // tr

    def body(q_ref, o_ref, r_ref, pin_ref, p_ref):
        acc = o_ref[...].astype(F32)
        for k in range(3):
            acc = acc + r_ref[k].astype(F32)
        p_ref[...] = acc

    return pl.pallas_call(
        body, name="sum_partials", out_shape=jax.ShapeDtypeStruct(ppack.shape, ppack.dtype),
        grid_spec=pltpu.PrefetchScalarGridSpec(
            num_scalar_prefetch=1, grid=(rows // tr,),
            in_specs=[pl.BlockSpec((None, tr, D), lambda i, q_ref: (q_ref[0], i, 0)),
                      pl.BlockSpec((3, tr, D), lambda i, q_ref: (0, i, 0)),
                      pl.BlockSpec(memory_space=pl.ANY)],
            out_specs=pl.BlockSpec((tr, D), lambda i, q_ref: (ob + i, 0))),
        input_output_aliases={3: 0},
        compiler_params=_cparams("parallel"),
    )(qv, dw, rb, ppack)


def _wspec(g):
    return _resident(g.shape, lambda i: (0, 0, 0))


def _ffn_fwd_inner(x1, mod_ref, gf_ref, w1_ref, w2_ref, h2_ref, a_ref, z_ref, x2_ref):
    h2 = _rms_fwd(x1, gf_ref[...], mod_ref[4:5, :], mod_ref[3:4, :])[0]
    h2b = h2.astype(BF16)
    h2_ref[...] = h2b
    f4 = w1_ref.shape[2]
    z = jnp.zeros(x1.shape, F32)
    for q in range(NQ):
        a = jnp.maximum(_dot(h2b, w1_ref[q]), 0.0)
        a_ref[:, q * f4:(q + 1) * f4] = a.astype(BF16)
        z = z + _dot((a * a).astype(BF16), w2_ref[q])
    z_ref[...] = z.astype(BF16)
    x2_ref[...] = x1 + mod_ref[5:6, :] * z


def _sigmoid(x):
    return jax.nn.sigmoid(x)


def _heads_dot(xb, w_ref, hd, nt=False):
    outs = []
    for h in range(HEADS):
        xs = xb[:, h * hd:(h + 1) * hd]
        outs.append(_dot_nt(xs, w_ref[h]) if nt else _dot(xs, w_ref[h]))
    return jnp.concatenate(outs, axis=1)


def _lru_gates(xc, wa_ref, ba, wx_ref, bx, lam, hd):
    xcb = xc.astype(BF16)
    gate_r = _sigmoid(_heads_dot(xcb, wa_ref, hd) + ba)
    gate_i = _sigmoid(_heads_dot(xcb, wx_ref, hd) + bx)
    ls = jax.nn.log_sigmoid(lam)
    log_a = gate_r * (LRU_C * ls)
    a = jnp.exp(log_a)
    mult = jnp.sqrt(_neg_expm1(2.0 * log_a, a * a))
    return xcb, gate_r, gate_i, ls, a, mult


def _conv_taps(xext, cw, tt):
    acc = cw[0:1, :] * xext[pl.ds(8 - (CONV_W - 1), tt), :]
    for k in range(1, CONV_W):
        acc = acc + cw[k:k + 1, :] * xext[pl.ds(8 - (CONV_W - 1) + k, tt), :]
    return acc


def _lru_fwd(x, mod_l, g_mix, g_wy, g_win, b_y, b_in, cw, cb, wa, ba, wx, bx, lam, g_wout, b_out, **comm):
    S, W = x.shape
    tt = min(TT, S)
    hd = W // HEADS

    def body(x_ref, mod_ref, g_ref, wy_ref, win_ref, by_ref, bin_ref, cw_ref, cb_ref, wa_ref, ba_ref, wx_ref, bx_ref,
             lam_ref, wo_ref, bo_ref, h_ref, gb_ref, xr_ref, hs_ref, p_ref, y_ref, x1_ref,
             xc_ref, gr_ref, gi_ref, a_s, mu_ref, xext, u_s, carry):
        i = pl.program_id(0)

        @pl.when(i == 0)
        def _():
            carry[...] = jnp.zeros_like(carry)
            xext[0:8, :] = jnp.zeros((8, W), F32)

        @pl.when(i > 0)
        def _():
            xext[0:8, :] = xext[pl.ds(tt, 8), :]

        xv = x_ref[...]
        hb = _rms_fwd(xv, g_ref[...], mod_ref[1:2, :], mod_ref[0:1, :])[0].astype(BF16)
        h_ref[...] = hb
        gbv = _dot(hb, wy_ref[...].reshape(W, W)) + by_ref[...]
        gb_ref[...] = gbv
        xr = _dot(hb, win_ref[...].reshape(W, W)) + bin_ref[...]
        xr_ref[...] = xr
        xext[pl.ds(8, tt), :] = xr
        xc = _conv_taps(xext, cw_ref[...], tt) + cb_ref[...]
        _, gate_r, gate_i, _, a, mult = _lru_gates(xc, wa_ref, ba_ref[...], wx_ref, bx_ref[...], lam_ref[...], hd)
        xc_ref[...] = xc
        gr_ref[...] = gate_r
        gi_ref[...] = gate_i
        mu_ref[...] = mult
        a_s[...] = a
        u_s[...] = mult * (gate_i * xc)
        row = lax.broadcasted_iota(jnp.int32, (8, W), 0)

        def step(k, _):
            off = pl.multiple_of(k * 8, 8)
            A = a_s[pl.ds(off, 8), :]
            U = u_s[pl.ds(off, 8), :]
            for d in (1, 2, 4):
                keep = row >= d
                Us = jnp.where(keep, pltpu.roll(U, d, 0), 0.0)
                As = jnp.where(keep, pltpu.roll(A, d, 0), 1.0)
                U = U + A * Us
                A = A * As
            H = U + A * carry[...]
            hs_ref[pl.ds(off, 8), :] = H
            carry[...] = jnp.broadcast_to(H[7:8, :], (8, W))
            return 0

        lax.fori_loop(0, tt // 8, step, 0)
        pb = (hs_ref[...] * _gelu(gbv)[0]).astype(BF16)
        p_ref[...] = pb
        y = _dot(pb, wo_ref[...].reshape(W, W)) + bo_ref[...]
        y_ref[...] = y.astype(BF16)
        x1_ref[...] = xv + mod_ref[2:3, :] * y

    tile = pl.BlockSpec((tt, W), lambda i: (i, 0))
    row = pl.BlockSpec((1, W), lambda i: (0, 0))
    wblk = pl.BlockSpec((HEADS, hd, hd), lambda i: (0, 0, 0))
    f32o, bf16o = jax.ShapeDtypeStruct((S, W), F32), jax.ShapeDtypeStruct((S, W), BF16)
    return _pcall(
        body, name="lru_fwd", grid=(S // tt,),
        out_shape=(bf16o, f32o, f32o, f32o, bf16o, bf16o, f32o, f32o, f32o, f32o, f32o, f32o),
        in_specs=[tile, pl.BlockSpec((8, W), lambda i: (0, 0)), row, _wspec(g_wy), _wspec(g_win), row, row,
                  pl.BlockSpec((CONV_W, W), lambda i: (0, 0)), row, wblk, row, wblk, row, row, _wspec(g_wout), row],
        out_specs=(tile,) * 12,
        scratch_shapes=[pltpu.VMEM((tt + 8, W), F32), pltpu.VMEM((tt, W), F32), pltpu.VMEM((8, W), F32)],
        operands=(x, mod_l, g_mix, g_wy, g_win, b_y, b_in, cw, cb, wa, ba, wx, bx, lam, g_wout, b_out), **comm)


def _ffn_out_shapes(S, D, F):
    return (jax.ShapeDtypeStruct((S, D), BF16), jax.ShapeDtypeStruct((S, F), BF16),
            jax.ShapeDtypeStruct((S, D), BF16), jax.ShapeDtypeStruct((S, D), F32))


def _ffn_fwd(x1, mod_l, g_ffn, g_w1, g_w2, **comm):
    S, D = x1.shape
    tm = min(TM, S)
    F = g_w1.shape[2] * NQ

    def body(x1_ref, mod_ref, gf_ref, w1_ref, w2_ref, h2_ref, a_ref, z_ref, x2_ref):
        _ffn_fwd_inner(x1_ref[...], mod_ref, gf_ref, w1_ref, w2_ref, h2_ref, a_ref, z_ref, x2_ref)

    tile = pl.BlockSpec((tm, D), lambda i: (i, 0))
    row = pl.BlockSpec((1, D), lambda i: (0, 0))
    return _pcall(
        body, name="ffn_fwd", grid=(S // tm,),
        out_shape=_ffn_out_shapes(S, D, F),
        in_specs=[tile, pl.BlockSpec((8, D), lambda i: (0, 0)), row, _wspec(g_w1), _wspec(g_w2)],
        out_specs=(tile, pl.BlockSpec((tm, F), lambda i: (i, 0)), tile, tile),
        operands=(x1, mod_l, g_ffn, g_w1, g_w2), **comm)


def _window_vec(D):
    gd = D // len(POOL_WINDOWS)
    lane = lax.broadcasted_iota(jnp.int32, (1, D), 1)
    w = jnp.full((1, D), float(POOL_WINDOWS[0]), F32)
    for g in range(1, len(POOL_WINDOWS)):
        w = jnp.where(lane >= g * gd, float(POOL_WINDOWS[g]), w)
    return w


def _pool_mix_ffn_fwd(x, mod_l, g_mix, pw, ps, g_ffn, g_w1, g_w2, **comm):
    S, D = x.shape
    tm = min(TPF, S)
    F = g_w1.shape[2] * NQ
    gd = D // len(POOL_WINDOWS)
    n = tm + 24

    def body(x_ref, xh_ref, mod_ref, gm_ref, pw_ref, ps_ref, gf_ref, w1_ref, w2_ref,
             pl_ref, x1_ref, h2_ref, a_ref, z_ref, x2_ref, ext, b1, b2):
        i = pl.program_id(0)
        g, sc, sh = gm_ref[...], mod_ref[1:2, :], mod_ref[0:1, :]
        xv = x_ref[...]
        h = _rms_fwd(xv, g, sc, sh)[0]
        hh = _rms_fwd(xh_ref[...], g, sc, sh)[0]
        zeros8 = jnp.zeros((8, D), F32)
        ext[0:8, :] = zeros8
        b1[0:8, :] = zeros8
        b2[0:8, :] = zeros8
        ext[8:24, :] = jnp.where(i > 0, hh, 0.0)
        ext[pl.ds(24, tm), :] = h
        m = n - 8
        b1[pl.ds(8, m), :] = ext[pl.ds(8, m), :] + ext[pl.ds(7, m), :]
        b2[pl.ds(8, m), gd:] = b1[pl.ds(8, m), gd:] + b1[pl.ds(6, m), gd:]
        b1[pl.ds(8, m), 2 * gd:] = b2[pl.ds(8, m), 2 * gd:] + b2[pl.ds(4, m), 2 * gd:]
        b2[pl.ds(8, m), 3 * gd:] = b1[pl.ds(8, m), 3 * gd:] + b1[pl.ds(0, m), 3 * gd:]
        wsum = jnp.concatenate([b1[pl.ds(24, tm), 0:gd], b2[pl.ds(24, tm), gd:2 * gd],
                                b1[pl.ds(24, tm), 2 * gd:3 * gd], b2[pl.ds(24, tm), 3 * gd:]], axis=1)
        t1 = (lax.broadcasted_iota(jnp.int32, (tm, 1), 0) + (i * tm + 1)).astype(F32)
        cnt = jnp.minimum(t1, _window_vec(D))
        pooled = (wsum / cnt - h).astype(BF16)
        pl_ref[...] = pooled
        y = _heads_dot(pooled, pw_ref, gd) * ps_ref[...]
        x1 = xv + mod_ref[2:3, :] * y
        x1_ref[...] = x1
        _ffn_fwd_inner(x1, mod_ref, gf_ref, w1_ref, w2_ref, h2_ref, a_ref, z_ref, x2_ref)

    tile = pl.BlockSpec((tm, D), lambda i: (i, 0))
    halo = pl.BlockSpec((16, D), lambda i: (jnp.maximum(i * (tm // 16) - 1, 0), 0))
    row = pl.BlockSpec((1, D), lambda i: (0, 0))
    return _pcall(
        body, name="pool_mix_ffn_fwd", grid=(S // tm,),
        out_shape=(jax.ShapeDtypeStruct((S, D), BF16), jax.ShapeDtypeStruct((S, D), F32)) + _ffn_out_shapes(S, D, F),
        in_specs=[tile, halo, pl.BlockSpec((8, D), lambda i: (0, 0)), row,
                  pl.BlockSpec((len(POOL_WINDOWS), gd, gd), lambda i: (0, 0, 0)), row, row,
                  _wspec(g_w1), _wspec(g_w2)],
        out_specs=(tile, tile, tile, pl.BlockSpec((tm, F), lambda i: (i, 0)), tile, tile),
        scratch_shapes=[pltpu.VMEM((n, D), F32), pltpu.VMEM((n, D), F32), pltpu.VMEM((n, D), F32)],
        operands=(x, x, mod_l, g_mix, pw, ps, g_ffn, g_w1, g_w2), **comm)


def _loss_head(xv, gv, tv, acc_ref):
    D = xv.shape[1]
    r = lax.rsqrt(jnp.mean(xv * xv, axis=-1, keepdims=True) + EPS)
    xhat = xv * r
    err = xhat * gv - tv
    acc_ref[0:1, :] += jnp.sum(err * err, axis=0, keepdims=True)
    dy = err * (1.0 / D)
    acc_ref[1:2, :] += jnp.sum(dy * xhat, axis=0, keepdims=True)
    dxh = dy * gv
    return r * (dxh - xhat * jnp.mean(dxh * xhat, axis=-1, keepdims=True))


def _ffn_bwd(dx2, x1, a, z, mod_l, g_ffn, g_w1, g_w2, head=None, **comm):
    S, D = dx2.shape
    F = a.shape[1]
    f4 = F // NQ
    tm = min(TM, S)
    nh = 2 if head else 0

    def body(*refs):
        dx2_ref, x1_ref, a_ref, z_ref, mod_ref, gf_ref, w1_ref, w2_ref = refs[:8]
        dx1_ref, du_ref, dz_ref, acc_ref = refs[8 + nh:]
        _zero_first(acc_ref)
        dx2v = dx2_ref[...]
        if head:
            dx2v = _loss_head(dx2v, refs[8][...], refs[9][...], acc_ref)
        acc_ref[5:6, :] +=jnp.sum(dx2v * z_ref[...].astype(F32), axis=0, keepdims=True)
        dzb = (dx2v * mod_ref[5:6, :]).astype(BF16)
        dz_ref[...] = dzb
        dh2 = jnp.zeros((tm, D), F32)
        for q in range(NQ):
            av = a_ref[:, q * f4:(q + 1) * f4].astype(F32)
            du = (_dot_nt(dzb, w2_ref[q]) * (2.0 * av)).astype(BF16)
            du_ref[:, q * f4:(q + 1) * f4] = du
            dh2 = dh2 + _dot_nt(du, w1_ref[q])
        g, sc = gf_ref[...], mod_ref[4:5, :]
        _, xhat, r, n = _rms_fwd(x1_ref[...], g, sc, mod_ref[3:4, :])
        dx, dsh, dsc, dg = _rms_bwd(dh2, xhat, r, n, g, sc)
        acc_ref[3:4, :] += dsh
        acc_ref[4:5, :] += dsc
        acc_ref[7:8, :] += dg
        dx1_ref[...] = dx2v + dx

    tile = pl.BlockSpec((tm, D), lambda i: (i, 0))
    wide = pl.BlockSpec((tm, F), lambda i: (i, 0))
    return _pcall(
        body, name="ffn_bwd", grid=(S // tm,),
        out_shape=(jax.ShapeDtypeStruct((S, D), F32), jax.ShapeDtypeStruct((S, F), BF16),
                   jax.ShapeDtypeStruct((S, D), BF16), jax.ShapeDtypeStruct((8, D), F32)),
        in_specs=[tile, tile, wide, tile, pl.BlockSpec((8, D), lambda i: (0, 0)), pl.BlockSpec((1, D), lambda i: (0, 0)),
                  _wspec(g_w1), _wspec(g_w2)] + ([pl.BlockSpec((1, D), lambda i: (0, 0)), tile] if head else []),
        out_specs=(tile, wide, tile, pl.BlockSpec((8, D), lambda i: (0, 0))),
        operands=(dx2, x1, a, z, mod_l, g_ffn, g_w1, g_w2) + (tuple(head) if head else ()), **comm)


def _dw_blocked(a, b, by_rows, square_a, name):
    S = a.shape[0]
    tk = min(TK, S)
    nk = S // tk
    if by_rows:
        bm, bn = a.shape[1] // NQ, b.shape[1]
        a_map, b_map = (lambda q, k: (k, q)), (lambda q, k: (k, 0))
    else:
        bm, bn = a.shape[1], b.shape[1] // NQ
        a_map, b_map = (lambda q, k: (k, 0)), (lambda q, k: (k, q))

    def body(a_ref, b_ref, o_ref, acc):
        k = pl.program_id(1)

        @pl.when(k == 0)
        def _():
            acc[...] = jnp.zeros_like(acc)

        av = a_ref[...]
        if square_a:
            av = av * av
        acc[...] += _dot_tn(av, b_ref[...])

        @pl.when(k == nk - 1)
        def _():
            o_ref[...] = acc[...].astype(o_ref.dtype)

    return pl.pallas_call(
        body, name=name, grid=(NQ, nk),
        out_shape=jax.ShapeDtypeStruct((NQ, bm, bn), BF16),
        in_specs=[pl.BlockSpec((tk, bm), a_map), pl.BlockSpec((tk, bn), b_map)],
        out_specs=pl.BlockSpec((None, bm, bn), lambda q, k: (q, 0, 0)),
        scratch_shapes=[pltpu.VMEM((bm, bn), F32)],
        compiler_params=_cparams("parallel", "arbitrary"),
    )(a, b)


def _dw_whole(a, bs, name, **comm):
    S, M = a.shape
    N = bs[0].shape[1]
    tk = min(TK, S)
    nk = S // tk
    nb = len(bs)

    def body(*refs):
        a_ref, b_refs, o_refs, accs = refs[0], refs[1:1 + nb], refs[1 + nb:1 + 2 * nb], refs[1 + 2 * nb:]
        k = pl.program_id(0)

        @pl.when(k == 0)
        def _():
            for acc in accs:
                acc[...] = jnp.zeros_like(acc)

        av = a_ref[...]
        for b_ref, acc in zip(b_refs, accs):
            acc[...] += _dot_tn(av, b_ref[...])

        @pl.when(k == nk - 1)
        def _():
            for o_ref, acc in zip(o_refs, accs):
                o_ref[...] = acc[...].reshape(NQ, M // NQ, N).astype(o_ref.dtype)

    return _pcall(
        body, name=name, grid=(nk,),
        out_shape=tuple(jax.ShapeDtypeStruct((NQ, M // NQ, N), BF16) for _ in bs),
        in_specs=[pl.BlockSpec((tk, M), lambda k: (k, 0))] + [pl.BlockSpec((tk, N), lambda k: (k, 0)) for _ in bs],
        out_specs=tuple(pl.BlockSpec((NQ, M // NQ, N), lambda k: (0, 0, 0)) for _ in bs),
        scratch_shapes=[pltpu.VMEM((M, N), F32) for _ in bs],
        operands=(a, *bs), **comm)


def _lru_bwd(dx1, y, x, xr0, gb, hs, xc_, gate_r_, gate_i_, a_, mult_, mod_l, g_mix, g_wout, g_wy, g_win, cw, wa, wx,
             lam, **comm):
    S, W = xr0.shape
    tt = min(TT, S)
    nb = S // tt
    hd = W // HEADS

    def body(dx1_ref, y_ref, x_ref, xr_ref, gb_ref, hs_ref, hsh_ref, xc_ref, gr_ref, gi_ref, a_s, mu_ref,
             mod_ref, gm_ref, wo_ref, wy_ref, win_ref, cw_ref, wa_ref, wx_ref, lam_ref,
             dy_ref, dgb_ref, dxr_ref, dx_ref, sm_ref, dwa_ref, dwx_ref, acc_ref,
             hext, qext, dext, b_s, qc, dc):
        i = pl.program_id(0)
        blk = nb - 1 - i

        @pl.when(i == 0)
        def _():
            sm_ref[...] = jnp.zeros_like(sm_ref)
            dwa_ref[...] = jnp.zeros_like(dwa_ref)
            dwx_ref[...] = jnp.zeros_like(dwx_ref)
            acc_ref[...] = jnp.zeros_like(acc_ref)
            qc[...] = jnp.zeros_like(qc)
            dc[...] = jnp.zeros_like(dc)

        dx1v = dx1_ref[...]
        acc_ref[2:3, :] += jnp.sum(dx1v * y_ref[...].astype(F32), axis=0, keepdims=True)
        dy = dx1v * mod_ref[2:3, :]
        acc_ref[3:4, :] += jnp.sum(dy, axis=0, keepdims=True)
        dyb = dy.astype(BF16)
        dy_ref[...] = dyb
        dpv = _dot_nt(dyb, wo_ref[...].reshape(W, W))

        hext[0:8, :] = jnp.where(blk > 0, hsh_ref[...], 0.0)
        hext[pl.ds(8, tt), :] = hs_ref[...]
        cw = cw_ref[...]
        lam = lam_ref[...]
        xc, gate_r, gate_i, a, mult = xc_ref[...], gr_ref[...], gi_ref[...], a_s[...], mu_ref[...]
        xcb = xc.astype(BF16)
        ls = jax.nn.log_sigmoid(lam)

        gbv = gb_ref[...]
        gate, th = _gelu(gbv)
        dgb = dpv * hs_ref[...] * _gelu_grad(gbv, th)
        dgbb = dgb.astype(BF16)
        dgb_ref[...] = dgbb
        sm_ref[9:10, :] += jnp.sum(dgb, axis=0, keepdims=True)
        dhs = dpv * gate

        b_s[...] = a * dhs
        qext[pl.ds(tt, 8), :] = qc[...]
        row = lax.broadcasted_iota(jnp.int32, (8, W), 0)

        def step(k, _):
            off = pl.multiple_of((tt // 8 - 1 - k) * 8, 8)
            A = a_s[pl.ds(off, 8), :]
            B = b_s[pl.ds(off, 8), :]
            for d in (1, 2, 4):
                keep = row < 8 - d
                Bs = jnp.where(keep, pltpu.roll(B, 8 - d, 0), 0.0)
                As = jnp.where(keep, pltpu.roll(A, 8 - d, 0), 1.0)
                B = B + A * Bs
                A = A * As
            Q = B + A * qc[...]
            qext[pl.ds(off, 8), :] = Q
            qc[...] = jnp.broadcast_to(Q[0:1, :], (8, W))
            return 0

        lax.fori_loop(0, tt // 8, step, 0)
        gsc = dhs + qext[pl.ds(1, tt), :]
        da = gsc * hext[pl.ds(7, tt), :]
        t1 = gsc * xc
        dmult = t1 * gate_i
        dgate_i = t1 * mult
        dxc = gsc * (mult * gate_i)
        dlog_a = da * a - dmult * (a * a) / mult
        dgate_r = dlog_a * (LRU_C * ls)
        sm_ref[7:8, :] += jnp.sum(dlog_a * (LRU_C * gate_r), axis=0, keepdims=True)
        dga = dgate_r * gate_r * (1.0 - gate_r)
        dgx = dgate_i * gate_i * (1.0 - gate_i)
        sm_ref[5:6, :] += jnp.sum(dga, axis=0, keepdims=True)
        sm_ref[6:7, :] += jnp.sum(dgx, axis=0, keepdims=True)
        dgab = dga.astype(BF16)
        dgxb = dgx.astype(BF16)
        dxc = dxc + _heads_dot(dgab, wa_ref, hd, nt=True) + _heads_dot(dgxb, wx_ref, hd, nt=True)
        for h in range(HEADS):
            sl = slice(h * hd, (h + 1) * hd)
            dwa_ref[h] += _dot_tn(xcb[:, sl], dgab[:, sl])
            dwx_ref[h] += _dot_tn(xcb[:, sl], dgxb[:, sl])
        sm_ref[4:5, :] += jnp.sum(dxc, axis=0, keepdims=True)
        dext[pl.ds(0, tt), :] = dxc
        dext[pl.ds(tt, 8), :] = dc[...]
        xrv = xr_ref[...]
        dxr = None
        for k in range(CONV_W):
            up = dext[pl.ds(CONV_W - 1 - k, tt), :]
            sm_ref[k:k + 1, :] += jnp.sum(up * xrv, axis=0, keepdims=True)
            dxr = cw[k:k + 1, :] * up if dxr is None else dxr + cw[k:k + 1, :] * up
        dc[...] = dext[0:8, :]
        sm_ref[8:9, :] += jnp.sum(dxr, axis=0, keepdims=True)
        dxrb = dxr.astype(BF16)
        dxr_ref[...] = dxrb

        dh = _dot_nt(dxrb, win_ref[...].reshape(W, W)) + _dot_nt(dgbb, wy_ref[...].reshape(W, W))
        g, sc = gm_ref[...], mod_ref[1:2, :]
        _, xhat, r, n = _rms_fwd(x_ref[...], g, sc, mod_ref[0:1, :])
        dx, dsh, dsc, dg = _rms_bwd(dh, xhat, r, n, g, sc)
        acc_ref[0:1, :] += dsh
        acc_ref[1:2, :] += dsc
        acc_ref[6:7, :] += dg
        dx_ref[...] = dx1v + dx

        @pl.when(i == nb - 1)
        def _():
            sm_ref[7:8, :] = sm_ref[7:8, :] * jax.nn.sigmoid(-lam)

    rev = lambda i: (nb - 1 - i, 0)
    tile = pl.BlockSpec((tt, W), rev)
    halo = pl.BlockSpec((8, W), lambda i: (jnp.maximum((nb - 1 - i) * (tt // 8) - 1, 0), 0))
    row = pl.BlockSpec((1, W), lambda i: (0, 0))
    wblk = pl.BlockSpec((HEADS, hd, hd), lambda i: (0, 0, 0))
    bf16o = jax.ShapeDtypeStruct((S, W), BF16)
    return _pcall(
        body, name="lru_bwd", grid=(nb,),
        out_shape=(bf16o, bf16o, bf16o, jax.ShapeDtypeStruct((S, W), F32),
                   jax.ShapeDtypeStruct((16, W), F32), jax.ShapeDtypeStruct((HEADS, hd, hd), F32),
                   jax.ShapeDtypeStruct((HEADS, hd, hd), F32), jax.ShapeDtypeStruct((8, W), F32)),
        in_specs=[tile, tile, tile, tile, tile, tile, halo, tile, tile, tile, tile, tile,
                  pl.BlockSpec((8, W), lambda i: (0, 0)), row,
                  _wspec(g_wout), _wspec(g_wy), _wspec(g_win), pl.BlockSpec((CONV_W, W), lambda i: (0, 0)),
                  wblk, wblk, row],
        out_specs=(tile, tile, tile, tile, pl.BlockSpec((16, W), lambda i: (0, 0)), wblk, wblk,
                   pl.BlockSpec((8, W), lambda i: (0, 0))),
        scratch_shapes=[pltpu.VMEM((tt + 8, W), F32), pltpu.VMEM((tt + 8, W), F32), pltpu.VMEM((tt + 8, W), F32),
                        pltpu.VMEM((tt, W), F32), pltpu.VMEM((8, W), F32), pltpu.VMEM((8, W), F32)],
        operands=(dx1, y, x, xr0, gb, hs, hs, xc_, gate_r_, gate_i_, a_, mult_, mod_l, g_mix, g_wout, g_wy, g_win,
                  cw, wa, wx, lam),
        **comm)


def _pool_bwd(dx1, x, pooled, mod_l, g_mix, pw, ps, h2, du, a, dz):
    S, D = x.shape
    tm = min(TP, S)
    nb = S // tm
    ng = len(POOL_WINDOWS)
    gd = D // ng
    n = tm + 24
    f4 = du.shape[1] // NQ
    assert nb % NQ == 0
    kch = nb // NQ
    kr = S // kch

    def body(dx1_ref, dxh_ref, x_ref, pl_ref, mod_ref, gm_ref, pw_ref, ps_ref, h2_ref, du_ref, a_ref, dz_ref,
             dx_ref, acc_ref, dpw_ref, dw1_ref, dw2_ref, ext, b1, b2, acc1, acc2):
        i = pl.program_id(0)

        @pl.when(i == 0)
        def _():
            acc_ref[...] = jnp.zeros_like(acc_ref)
            dpw_ref[...] = jnp.zeros_like(dpw_ref)

        @pl.when(i % kch == 0)
        def _():
            acc1[...] = jnp.zeros_like(acc1)
            acc2[...] = jnp.zeros_like(acc2)

        acc1[...] += _dot_tn(h2_ref[...], du_ref[...])
        av = a_ref[...]
        acc2[...] += _dot_tn(av * av, dz_ref[...])

        gt, psv = mod_ref[2:3, :], ps_ref[...]
        wvec = _window_vec(D)
        dx1v = dx1_ref[...]
        pooled = pl_ref[...]
        mixed = _heads_dot(pooled, pw_ref, gd)
        acc_ref[2:3, :] += jnp.sum(dx1v * (mixed * psv), axis=0, keepdims=True)
        dy = dx1v * gt
        acc_ref[3:4, :] += jnp.sum(dy * mixed, axis=0, keepdims=True)
        dmix = (dy * psv).astype(BF16)
        for gi in range(ng):
            sl = slice(gi * gd, (gi + 1) * gd)
            dpw_ref[gi] += _dot_tn(pooled[:, sl], dmix[:, sl])
        dpooled = _heads_dot(dmix, pw_ref, gd, nt=True)
        dmix_h = (dxh_ref[...] * gt * psv).astype(BF16)
        dpooled_h = jnp.where(i < nb - 1, _heads_dot(dmix_h, pw_ref, gd, nt=True), 0.0)
        t1 = (lax.broadcasted_iota(jnp.int32, (tm, 1), 0) + (i * tm + 1)).astype(F32)
        t1h = (lax.broadcasted_iota(jnp.int32, (16, 1), 0) + ((i + 1) * tm + 1)).astype(F32)
        zeros8 = jnp.zeros((8, D), F32)
        ext[pl.ds(0, tm), :] = dpooled / jnp.minimum(t1, wvec)
        ext[pl.ds(tm, 16), :] = dpooled_h / jnp.minimum(t1h, wvec)
        ext[pl.ds(tm + 16, 8), :] = zeros8
        b1[pl.ds(tm + 16, 8), :] = zeros8
        b2[pl.ds(tm + 16, 8), :] = zeros8
        m = n - 8
        b1[pl.ds(0, m), :] = ext[pl.ds(0, m), :] + ext[pl.ds(1, m), :]
        b2[pl.ds(0, m), gd:] = b1[pl.ds(0, m), gd:] + b1[pl.ds(2, m), gd:]
        b1[pl.ds(0, m), 2 * gd:] = b2[pl.ds(0, m), 2 * gd:] + b2[pl.ds(4, m), 2 * gd:]
        b2[pl.ds(0, m), 3 * gd:] = b1[pl.ds(0, m), 3 * gd:] + b1[pl.ds(8, m), 3 * gd:]
        wsum = jnp.concatenate([b1[pl.ds(0, tm), 0:gd], b2[pl.ds(0, tm), gd:2 * gd],
                                b1[pl.ds(0, tm), 2 * gd:3 * gd], b2[pl.ds(0, tm), 3 * gd:]], axis=1)
        dh = wsum - dpooled
        g, sc = gm_ref[...], mod_ref[1:2, :]
        _, xhat, r, nn = _rms_fwd(x_ref[...], g, sc, mod_ref[0:1, :])
        dx, dsh, dsc, dg = _rms_bwd(dh, xhat, r, nn, g, sc)
        acc_ref[0:1, :] += dsh
        acc_ref[1:2, :] += dsc
        acc_ref[6:7, :] += dg
        dx_ref[...] = dx1v + dx

        @pl.when(i % kch == kch - 1)
        def _():
            dw1_ref[...] = acc1[...].astype(BF16)
            dw2_ref[...] = acc2[...].astype(BF16)

    tile = pl.BlockSpec((tm, D), lambda i: (i, 0))
    halo = pl.BlockSpec((16, D), lambda i: (jnp.minimum((i + 1) * (tm // 16), S // 16 - 1), 0))
    row = pl.BlockSpec((1, D), lambda i: (0, 0))
    wblk = pl.BlockSpec((ng, gd, gd), lambda i: (0, 0, 0))
    full_k = pl.BlockSpec((kr, D), lambda i: (i % kch, 0))
    part_k = pl.BlockSpec((kr, f4), lambda i: (i % kch, i // kch))
    return pl.pallas_call(
        body, name="pool_bwd", grid=(nb,),
        out_shape=(jax.ShapeDtypeStruct((S, D), F32), jax.ShapeDtypeStruct((8, D), F32),
                   jax.ShapeDtypeStruct((ng, gd, gd), F32), jax.ShapeDtypeStruct((NQ, D, f4), BF16),
                   jax.ShapeDtypeStruct((NQ, f4, D), BF16)),
        in_specs=[tile, halo, tile, tile, pl.BlockSpec((8, D), lambda i: (0, 0)), row, wblk, row,
                  full_k, part_k, part_k, full_k],
        out_specs=(tile, pl.BlockSpec((8, D), lambda i: (0, 0)), wblk,
                   pl.BlockSpec((None, D, f4), lambda i: (i // kch, 0, 0)),
                   pl.BlockSpec((None, f4, D), lambda i: (i // kch, 0, 0))),
        scratch_shapes=[pltpu.VMEM((n, D), F32), pltpu.VMEM((n, D), F32), pltpu.VMEM((n, D), F32),
                        pltpu.VMEM((D, f4), F32), pltpu.VMEM((f4, D), F32)],
        compiler_params=_cparams("arbitrary"),
    )(dx1, dx1, x, pooled, mod_l, g_mix, pw, ps, h2, du, a, dz)


def _shard_to_rows(w, D):
    return w.reshape(-1, D)


def _blockdiag_full(gq, na, hd):
    return gq.reshape(NQ, na, HEADS, hd // NQ, hd).transpose(1, 2, 0, 3, 4).reshape(na, HEADS, hd, hd)


def _blockdiag_by_chip(dw, D):
    na, _, hd, _ = dw.shape
    return dw.reshape(na, HEADS, NQ, hd // NQ, hd).transpose(2, 0, 1, 3, 4).reshape(NQ, -1, D)


def kernel(x, c, w_mod, b_mod, norm_mix_g, norm_ffn_g, lru_w_y, lru_b_y, lru_w_in, lru_b_in, lru_conv_w, lru_conv_b, lru_w_a, lru_b_a, lru_w_x, lru_b_x, lru_lambda, lru_w_out, lru_b_out, pool_w, pool_scale, ffn_w1, ffn_w2, final_norm_g, loss_target, m_w_mod, m_b_mod, m_norm_mix_g, m_norm_ffn_g, m_lru_w_y, m_lru_b_y, m_lru_w_in, m_lru_b_in, m_lru_conv_w, m_lru_conv_b, m_lru_w_a, m_lru_b_a, m_lru_w_x, m_lru_b_x, m_lru_lambda, m_lru_w_out, m_lru_b_out, m_pool_w, m_pool_scale, m_ffn_w1, m_ffn_w2, m_final_norm_g, v_w_mod, v_b_mod, v_norm_mix_g, v_norm_ffn_g, v_lru_w_y, v_lru_b_y, v_lru_w_in, v_lru_b_in, v_lru_conv_w, v_lru_conv_b, v_lru_w_a, v_lru_b_a, v_lru_w_x, v_lru_b_x, v_lru_lambda, v_lru_w_out, v_lru_b_out, v_pool_w, v_pool_scale, v_ffn_w1, v_ffn_w2, v_final_norm_g):
    S, D = x.shape[1], x.shape[2]
    L = w_mod.shape[0]
    NA = lru_w_y.shape[0]
    NB = pool_w.shape[0]
    F = ffn_w1.shape[2] * NQ
    f4 = F // NQ
    hd = D // HEADS
    Cs = w_mod.shape[2]
    assert L == DEPTH and Cs * NQ == N_MOD * D and D % 1024 == 0
    x2d = x.reshape(S, D)
    tgt = loss_target.reshape(S, D)
    q = 2 * lax.axis_index("x") + lax.axis_index("y")

    big = [ffn_w1, ffn_w2, lru_w_y, lru_w_in, lru_w_out, lru_w_a, lru_w_x, pool_w]
    rows = [int(w.size) // D for w in big]
    offs = [sum(rows[:k]) for k in range(len(big))]
    O_W1, O_W2, O_WY, O_WIN, O_WOUT, O_WA, O_WX, O_PW = offs
    R = sum(rows)
    dq = D // NQ
    s_w1 = [ffn_w1[i].astype(BF16) for i in range(L)]
    s_w2 = [ffn_w2[i].astype(BF16) for i in range(L)]
    s_wy = [lru_w_y[j].astype(BF16) for j in range(NA)]
    s_win = [lru_w_in[j].astype(BF16) for j in range(NA)]
    s_wout = [lru_w_out[j].astype(BF16) for j in range(NA)]
    s_tiny = jnp.concatenate([_shard_to_rows(w, D) for w in (lru_w_a, lru_w_x, pool_w)], axis=0).astype(BF16)

    cshard = lru_conv_w.reshape(-1)
    small_fwd = jnp.concatenate([c.reshape(-1), cshard, lru_b_a.reshape(-1), lru_b_x.reshape(-1),
                                 pool_scale.reshape(-1)])
    small_fwd = jnp.pad(small_fwd, (0, 8 * D - small_fwd.shape[0])).reshape(8, D)

    g_w1, g_w2 = [None] * L, [None] * L
    g_wy, g_win, g_wout = [None] * NA, [None] * NA, [None] * NA
    SG, g_wy[0], g_win[0], g_wout[0], g_tiny = _comm_only("gather_first", small=small_fwd,
                                                         gathers=(s_wy[0], s_win[0], s_wout[0], s_tiny))
    wa_full = _blockdiag_full(g_tiny[:, :rows[5]], NA, hd)
    wx_full = _blockdiag_full(g_tiny[:, rows[5]:rows[5] + rows[6]], NA, hd)
    pw_full = _blockdiag_full(g_tiny[:, rows[5] + rows[6]:], NB, hd)
    SGf = SG.reshape(NDEV, 8 * D)
    c_all = SGf[:, :D]
    SGq = SGf.reshape(NQ, 2, 8 * D)[:, 0]
    o = D
    n_cw = NA * CONV_W * D // NQ
    conv_w_full = SGq[:, o:o + n_cw].reshape(NQ, NA, CONV_W, D // NQ).transpose(1, 2, 0, 3).reshape(NA, CONV_W, D)
    o += n_cw
    n_b = NA * HEADS * hd // NQ
    b_a_full = SGq[:, o:o + n_b].reshape(NQ, NA, HEADS, hd // NQ).transpose(1, 2, 0, 3).reshape(NA, 1, D)
    o += n_b
    b_x_full = SGq[:, o:o + n_b].reshape(NQ, NA, HEADS, hd // NQ).transpose(1, 2, 0, 3).reshape(NA, 1, D)
    o += n_b
    n_ps = NB * D // NQ
    pool_scale_full = SGq[:, o:o + n_ps].reshape(NQ, NB, D // NQ).transpose(1, 0, 2).reshape(NB, 1, D)


    b_mod_sh = lax.dynamic_slice_in_dim(b_mod, q * Cs, Cs, axis=1).reshape(L, 1, Cs)
    modpart = _mod_part(c_all, w_mod, b_mod_sh)
    modq = _exchange_mod(modpart.transpose(1, 0, 2))
    mod = modq.transpose(1, 0, 2).reshape(L, N_MOD, D)
    mod = jnp.pad(mod, ((0, 0), (0, 8 - N_MOD), (0, 0)))

    saved = []
    xcur = x2d
    for i in range(L):
        j = i // 2
        gm = norm_mix_g[i].reshape(1, D)
        gf = norm_ffn_g[i].reshape(1, D)
        if i % 2 == 0:
            h, gb, xr0, hs, p, y, x1, xc_s, gr_s, gi_s, a_sv, mu_s, g_w1[i], g_w2[i] = _lru_fwd(
                xcur, mod[i], gm, g_wy[j], g_win[j], lru_b_y[j].reshape(1, D), lru_b_in[j].reshape(1, D),
                conv_w_full[j], lru_conv_b[j].reshape(1, D), wa_full[j], b_a_full[j], wx_full[j], b_x_full[j],
                lru_lambda[j].reshape(1, D), g_wout[j], lru_b_out[j].reshape(1, D), gathers=(s_w1[i], s_w2[i]))
            h2, a, z, x2, g_w1[i + 1], g_w2[i + 1] = _ffn_fwd(x1, mod[i], gf, g_w1[i], g_w2[i],
                                                              gathers=(s_w1[i + 1], s_w2[i + 1]))
            saved.append(dict(x=xcur, h=h, gb=gb, xr0=xr0, hs=hs, p=p, y=y, x1=x1, h2=h2, a=a, z=z,
                              lru=(xc_s, gr_s, gi_s, a_sv, mu_s)))
        else:
            if j + 1 < NA:
                pooled, x1, h2, a, z, x2, g_wy[j + 1], g_win[j + 1], g_wout[j + 1] = _pool_mix_ffn_fwd(
                    xcur, mod[i], gm, pw_full[j], pool_scale_full[j], gf, g_w1[i], g_w2[i],
                    gathers=(s_wy[j + 1], s_win[j + 1], s_wout[j + 1]))
            else:
                pooled, x1, h2, a, z, x2 = _pool_mix_ffn_fwd(xcur, mod[i], gm, pw_full[j], pool_scale_full[j], gf,
                                                             g_w1[i], g_w2[i])
            saved.append(dict(x=xcur, pooled=pooled, x1=x1, h2=h2, a=a, z=z))
        xcur = x2

    dx = xcur
    qv = q.reshape(1).astype(jnp.int32)
    ppack = lax.empty((R, D), F32)
    psib = lax.empty((R, D), F32)
    pending, summed, hosted = [], [], []

    def comm_args(cap_rows=None):
        nonlocal hosted
        hosted, n = [], 0
        for item in pending:
            if cap_rows is not None and hosted and n + item[0].shape[1] > cap_rows:
                break
            hosted.append(item)
            n += item[0].shape[1]
        kw = {}
        if hosted:
            kw["scatters"] = tuple(dw for dw, _ in hosted)
        if summed:
            kw["sib"] = (ppack, psib, tuple(summed))
        return kw

    def after_host(extra):
        nonlocal ppack, psib, pending, summed
        had_sib = bool(summed)
        summed = []
        for (dw, off), rb in zip(hosted, extra[:len(hosted)]):
            ppack = _sum_into(ppack, dw, rb, off, qv)
            summed.append((off, dw.shape[1]))
        if had_sib:
            psib = extra[len(hosted)]
        pending = pending[len(hosted):]

    dmod_rows = [None] * L
    dg_mix = [None] * L
    dg_ffn = [None] * L
    d_small = {}
    dwa_l, dwx_l, dpw_l = [None] * NA, [None] * NA, [None] * NB
    for i in reversed(range(L)):
        j = i // 2
        sv = saved[i]
        gm = norm_mix_g[i].reshape(1, D)
        gf = norm_ffn_g[i].reshape(1, D)
        head = (final_norm_g.reshape(1, D), tgt) if i == L - 1 else None
        outs = _ffn_bwd(dx, sv["x1"], sv["a"], sv["z"], mod[i], gf, g_w1[i], g_w2[i], head=head,
                        **comm_args(cap_rows=D))
        dx1, du, dz, facc = outs[:4]
        after_host(outs[4:])
        if head:
            loss = lax.psum(0.5 * jnp.sum(facc[0]) / D, ("x", "y", "c"))
            d_final_g = facc[1]
        if i % 2 == 0:
            pending.append((_dw_blocked(sv["h2"], du, False, False, "dw1"), O_W1 + i * D))
            pending.append((_dw_blocked(sv["a"], dz, True, True, "dw2"), O_W2 + i * f4))
            outs = _lru_bwd(dx1, sv["y"], sv["x"], sv["xr0"], sv["gb"], sv["hs"], *sv["lru"], mod[i], gm, g_wout[j],
                            g_wy[j], g_win[j], conv_w_full[j], wa_full[j], wx_full[j], lru_lambda[j].reshape(1, D),
                            **comm_args())
            dyp, dgb, dxr, dx, sm, dwa, dwx, macc = outs[:8]
            after_host(outs[8:])
            dwa_l[j], dwx_l[j] = dwa, dwx
            if i == 0:
                tiny = jnp.concatenate([_blockdiag_by_chip(jnp.stack(dwa_l), D), _blockdiag_by_chip(jnp.stack(dwx_l), D),
                                        _blockdiag_by_chip(jnp.stack(dpw_l), D)], axis=1).astype(BF16)
                pending.append((tiny, O_WA))
            if i == 0:
                outs = _dw_whole(sv["h"], [dgb, dxr], "dwy_dwin", **comm_args())
                after_host(outs[2:])
                pending.append((outs[0], O_WY + j * dq))
                pending.append((outs[1], O_WIN + j * dq))
                outs = _dw_whole(sv["p"], [dyp], "dwout", **comm_args())
                after_host(outs[1:])
                pending.append((outs[0], O_WOUT + j * dq))
            else:
                outs = _dw_whole(sv["p"], [dyp], "dwout", **comm_args())
                after_host(outs[1:])
                pending.append((outs[0], O_WOUT + j * dq))
                outs = _dw_whole(sv["h"], [dgb, dxr], "dwy_dwin", **comm_args())
                after_host(outs[2:])
                pending.append((outs[0], O_WY + j * dq))
                pending.append((outs[1], O_WIN + j * dq))
            d_small[("lru", j)] = (sm, macc[3])
            dgt_m = macc[2]
        else:
            dx, macc, dpw, dw1, dw2 = _pool_bwd(dx1, sv["x"], sv["pooled"], mod[i], gm, pw_full[j], pool_scale_full[j],
                                                sv["h2"], du, sv["a"], dz)
            pending.append((dw1, O_W1 + i * D))
            pending.append((dw2, O_W2 + i * f4))
            dpw_l[j] = dpw
            d_small[("pool", j)] = macc[3]
            dgt_m = macc[2]
        dmod_rows[i] = jnp.stack([macc[0], macc[1], dgt_m, facc[3], facc[4], facc[5]])
        dg_mix[i] = macc[6]
        dg_ffn[i] = facc[7]
    grad_x = dx.reshape(x.shape)

    lru_sm = [d_small[("lru", j)] for j in range(NA)]
    small_rows = [jnp.stack(dmod_rows).reshape(L * N_MOD, D), jnp.stack(dg_mix), jnp.stack(dg_ffn),
                  jnp.stack([s[0][9] for s in lru_sm]), jnp.stack([s[0][8] for s in lru_sm]),
                  jnp.stack([s[0][4] for s in lru_sm]), jnp.stack([s[0][7] for s in lru_sm]),
                  jnp.stack([s[1] for s in lru_sm]),
                  jnp.stack([s[0][0:CONV_W] for s in lru_sm]).reshape(NA * CONV_W, D),
                  jnp.stack([s[0][5] for s in lru_sm]), jnp.stack([s[0][6] for s in lru_sm]),
                  jnp.stack([d_small[("pool", j)] for j in range(NB)]), d_final_g.reshape(1, D)]
    small_g = jnp.concatenate(small_rows, axis=0)
    n_small = small_g.shape[0]
    assert n_small <= SMALL_ROWS
    small_g = jnp.pad(small_g, ((0, SMALL_ROWS - n_small), (0, 0)))

    outs = _comm_only("scatter_last", small=small_g, reduce_small=True, **comm_args())
    sg_all, sg_sum = outs[:2]
    after_host(outs[2:])
    psum_mine = ppack
    psum_sib = _comm_only("sibling_last", sib=(ppack, psib, tuple(summed)))[0]

    def big_update(w, m, v, off, name):
        shp = w.shape
        g, dl, m2, v2 = _adam_rows(w.reshape(-1, D), m.reshape(-1, D), v.reshape(-1, D), psum_mine, psum_sib, off, name)
        return g.reshape(shp), dl.reshape(shp), m2.reshape(shp), v2.reshape(shp)

    res = {}
    res["ffn_w1"] = big_update(ffn_w1, m_ffn_w1, v_ffn_w1, O_W1, "adam_w1")
    res["ffn_w2"] = big_update(ffn_w2, m_ffn_w2, v_ffn_w2, O_W2, "adam_w2")
    res["lru_w_y"] = big_update(lru_w_y, m_lru_w_y, v_lru_w_y, O_WY, "adam_wy")
    res["lru_w_in"] = big_update(lru_w_in, m_lru_w_in, v_lru_w_in, O_WIN, "adam_win")
    res["lru_w_out"] = big_update(lru_w_out, m_lru_w_out, v_lru_w_out, O_WOUT, "adam_wout")

    def tiny_parts(w, off):
        n = int(w.size) // D
        return psum_mine[off:off + n].reshape(w.shape), psum_sib[off:off + n].reshape(w.shape)

    tiny_items = [("lru_w_a", lru_w_a, m_lru_w_a, v_lru_w_a) + tiny_parts(lru_w_a, O_WA),
                  ("lru_w_x", lru_w_x, m_lru_w_x, v_lru_w_x) + tiny_parts(lru_w_x, O_WX),
                  ("pool_w", pool_w, m_pool_w, v_pool_w) + tiny_parts(pool_w, O_PW)]

    dmod_all = sg_all[:, :L * N_MOD, :].reshape(NDEV, L, N_MOD * D)
    dmod_sh = lax.dynamic_slice_in_dim(dmod_all, q * Cs, Cs, axis=2).transpose(1, 0, 2)
    res["w_mod"] = _wmod_update(c_all.T, dmod_sh, w_mod, m_w_mod, v_w_mod)

    r0 = 0
    by_rows, names_a = [], []
    for name, w, m, v in (("b_mod", b_mod, m_b_mod, v_b_mod), ("norm_mix_g", norm_mix_g, m_norm_mix_g, v_norm_mix_g),
                          ("norm_ffn_g", norm_ffn_g, m_norm_ffn_g, v_norm_ffn_g),
                          ("lru_b_y", lru_b_y, m_lru_b_y, v_lru_b_y), ("lru_b_in", lru_b_in, m_lru_b_in, v_lru_b_in),
                          ("lru_conv_b", lru_conv_b, m_lru_conv_b, v_lru_conv_b),
                          ("lru_lambda", lru_lambda, m_lru_lambda, v_lru_lambda),
                          ("lru_b_out", lru_b_out, m_lru_b_out, v_lru_b_out)):
        by_rows.append((w, m, v, r0))
        names_a.append(name)
        r0 += int(w.size) // D
    g_conv_w = lax.dynamic_slice_in_dim(sg_sum[r0:r0 + NA * CONV_W].reshape(NA, CONV_W, D), q * dq, dq, axis=2)
    r0 += NA * CONV_W
    g_b_a = lax.dynamic_slice_in_dim(sg_sum[r0:r0 + NA].reshape(NA, HEADS, hd), q * (hd // NQ), hd // NQ, axis=2)
    r0 += NA
    g_b_x = lax.dynamic_slice_in_dim(sg_sum[r0:r0 + NA].reshape(NA, HEADS, hd), q * (hd // NQ), hd // NQ, axis=2)
    r0 += NA
    g_ps = lax.dynamic_slice_in_dim(sg_sum[r0:r0 + NB], q * dq, dq, axis=1)
    r0 += NB
    by_rows.append((final_norm_g.reshape(1, D), m_final_norm_g.reshape(1, D), v_final_norm_g.reshape(1, D), r0))
    names_a.append("final_norm_g")
    sliced = [(lru_conv_w, m_lru_conv_w, v_lru_conv_w, g_conv_w), (lru_b_a, m_lru_b_a, v_lru_b_a, g_b_a),
              (lru_b_x, m_lru_b_x, v_lru_b_x, g_b_x), (pool_scale, m_pool_scale, v_pool_scale, g_ps)]
    res_a, res_b, res_c = _adam_small(sg_sum, by_rows, sliced, [t[1:] for t in tiny_items])
    for name, r in zip(names_a, res_a):
        res[name] = r
    for t, r in zip(tiny_items, res_c):
        res[t[0]] = r
    res["final_norm_g"] = tuple(a.reshape(D) for a in res["final_norm_g"])
    for name, (_, _, _, g), r in zip(("lru_conv_w", "lru_b_a", "lru_b_x", "pool_scale"), sliced, res_b):
        res[name] = (g,) + r

    order = ["w_mod", "b_mod", "norm_mix_g", "norm_ffn_g", "lru_w_y", "lru_b_y", "lru_w_in", "lru_b_in", "lru_conv_w",
             "lru_conv_b", "lru_w_a", "lru_b_a", "lru_w_x", "lru_b_x", "lru_lambda", "lru_w_out", "lru_b_out", "pool_w",
             "pool_scale", "ffn_w1", "ffn_w2", "final_norm_g"]
    return (loss, grad_x, *[res[n][0] for n in order], *[res[n][1] for n in order],
            *[res[n][2] for n in order], *[res[n][3] for n in order])
```
